```python
import math
import jax, jax.numpy as jnp
from jax import lax
import numpy as np

D_MODEL = 1024
BATCH = 8
SEQ = 4096
DEPTH = 4

N_MIXERS = 3
BLOCK = 128
NORM_EPS = 1e-6

SB_HEADS = 16
SB_HEAD_DIM = 64
SB_WIDTH = SB_HEADS * SB_HEAD_DIM

MLA_HEADS = 8
MLA_NOPE_DIM = 128
MLA_ROPE_DIM = 64
MLA_QK_DIM = MLA_NOPE_DIM + MLA_ROPE_DIM
MLA_V_DIM = 128
MLA_Q_RANK = 256
MLA_KV_RANK = 128
MLA_WIDTH = MLA_HEADS * MLA_V_DIM
ROPE_THETA = 10000.0

SWA_HEADS = 16
SWA_KV_HEADS = 4
SWA_GROUP = SWA_HEADS // SWA_KV_HEADS
SWA_HEAD_DIM = 64
SWA_WINDOW = 128
SWA_WIDTH = SWA_HEADS * SWA_HEAD_DIM

kernel_name = 'hybrid_sb_mla_swa_gated_trunk'


def rms_norm(x, g):
    xf = x.astype(jnp.float32)
    y = xf * lax.rsqrt(jnp.mean(xf * xf, axis=-1, keepdims=True) + NORM_EPS)
    return (y * g.astype(jnp.float32)).astype(x.dtype)


def rope(x, pos):
    half = x.shape[-1] // 2
    inv_freq = ROPE_THETA ** (-jnp.arange(half, dtype=jnp.float32) / half)
    ang = pos.astype(jnp.float32)[:, None] * inv_freq[None, :]
    cos = jnp.cos(ang)[None, :, None, :]
    sin = jnp.sin(ang)[None, :, None, :]
    xf = x.astype(jnp.float32)
    x1, x2 = xf[..., :half], xf[..., half:]
    out = jnp.concatenate([x1 * cos - x2 * sin, x2 * cos + x1 * sin], axis=-1)
    return out.astype(x.dtype)


def alibi_slopes(n_heads):
    return 2.0 ** (-8.0 * jnp.arange(1, n_heads + 1, dtype=jnp.float32) / n_heads)


def stick_breaking_attention(q, k, v):
    B, S, H, d = q.shape
    scale = 1.0 / math.sqrt(d)
    outs = []
    for i in range(S // BLOCK):
        t0 = i * BLOCK
        kl = t0 + BLOCK
        z = jnp.einsum('bthd,bshd->bhts', q[:, t0:kl], k[:, :kl]).astype(jnp.float32) * scale
        t_idx = t0 + jnp.arange(BLOCK)[:, None]
        s_idx = jnp.arange(kl)[None, :]
        mask = s_idx < t_idx
        log_fail = jnp.where(mask, jax.nn.log_sigmoid(-z), 0.0)
        later = lax.cumsum(log_fail, axis=3, reverse=True) - log_fail
        a = jnp.where(mask, jnp.exp(jax.nn.log_sigmoid(z) + later), 0.0)
        outs.append(jnp.einsum('bhts,bshd->bthd', a.astype(v.dtype), v[:, :kl]))
    return jnp.concatenate(outs, axis=1)


def causal_softmax_attention(q, k, v, scale):
    B, S, H, _ = q.shape
    outs = []
    for i in range(S // BLOCK):
        t0 = i * BLOCK
        kl = t0 + BLOCK
        s = jnp.einsum('bthd,bshd->bhts', q[:, t0:kl], k[:, :kl]).astype(jnp.float32) * scale
        mask = jnp.arange(kl)[None, :] <= (t0 + jnp.arange(BLOCK)[:, None])
        p = jax.nn.softmax(jnp.where(mask, s, -jnp.inf), axis=-1)
        outs.append(jnp.einsum('bhts,bshd->bthd', p.astype(v.dtype), v[:, :kl]))
    return jnp.concatenate(outs, axis=1)


def sliding_window_sink_attention(q, k, v, sinks):
    B, S, H, d = q.shape
    nb = S // BLOCK
    qb = q.reshape(B, nb, BLOCK, SWA_KV_HEADS, SWA_GROUP, d)

    def band(t):
        prev = jnp.pad(t, ((0, 0), (BLOCK, 0), (0, 0), (0, 0)))[:, :S]
        return jnp.concatenate([prev.reshape(B, nb, BLOCK, SWA_KV_HEADS, d),
                                t.reshape(B, nb, BLOCK, SWA_KV_HEADS, d)], axis=2)

    kb, vb = band(k), band(v)
    s = jnp.einsum('bnqkgd,bnskd->bnkgqs', qb, kb).astype(jnp.float32) / math.sqrt(d)
    rel = jnp.arange(BLOCK)[:, None] + BLOCK - jnp.arange(2 * BLOCK)[None, :]
    abs_s = jnp.arange(nb)[:, None] * BLOCK - BLOCK + jnp.arange(2 * BLOCK)[None, :]
    valid = ((rel >= 0) & (rel < SWA_WINDOW))[None, :, :] & (abs_s >= 0)[:, None, :]
    slopes = alibi_slopes(SWA_HEADS).reshape(SWA_KV_HEADS, SWA_GROUP)
    s = s - slopes[:, :, None, None] * rel.astype(jnp.float32)
    s = jnp.where(valid[None, :, None, None, :, :], s, -jnp.inf)
    sink = sinks.astype(jnp.float32).reshape(SWA_KV_HEADS, SWA_GROUP)[None, None, :, :, None, None]
    m = jnp.maximum(jnp.max(s, axis=-1, keepdims=True), sink)
    e = jnp.exp(s - m)
    p = e / (jnp.sum(e, axis=-1, keepdims=True) + jnp.exp(sink - m))
    o = jnp.einsum('bnkgqs,bnskd->bnqkgd', p.astype(v.dtype), vb)
    return o.reshape(B, S, H, d)


def stick_breaking_layer(x, norm_g, w_in, w_out):
    B, S, _ = x.shape
    proj = rms_norm(x, norm_g) @ w_in
    q, k, v, gate = jnp.split(proj, 4, axis=-1)
    q = q.reshape(B, S, SB_HEADS, SB_HEAD_DIM)
    k = k.reshape(B, S, SB_HEADS, SB_HEAD_DIM)
    v = v.reshape(B, S, SB_HEADS, SB_HEAD_DIM)
    o = stick_breaking_attention(q, k, v).reshape(B, S, SB_WIDTH)
    return x + (o * jax.nn.silu(gate)) @ w_out


def mla_layer(x, norm_g, w_in, q_a_norm, w_uq, kv_a_norm, w_ukv, q_head_norm, k_head_norm, w_out):
    B, S, _ = x.shape
    proj = rms_norm(x, norm_g) @ w_in
    c1 = MLA_Q_RANK
    c2 = c1 + MLA_KV_RANK
    c3 = c2 + MLA_ROPE_DIM
    q_lat, kv_lat, k_pe, gate = proj[..., :c1], proj[..., c1:c2], proj[..., c2:c3], proj[..., c3:]
    q = (rms_norm(q_lat, q_a_norm) @ w_uq).reshape(B, S, MLA_HEADS, MLA_QK_DIM)
    kv = (rms_norm(kv_lat, kv_a_norm) @ w_ukv).reshape(B, S, MLA_HEADS, MLA_NOPE_DIM + MLA_V_DIM)
    k_nope, v = kv[..., :MLA_NOPE_DIM], kv[..., MLA_NOPE_DIM:]
    k_pe = jnp.broadcast_to(k_pe[:, :, None, :], (B, S, MLA_HEADS, MLA_ROPE_DIM))
    k = jnp.concatenate([k_nope, k_pe], axis=-1)
    q = rms_norm(q, q_head_norm)
    k = rms_norm(k, k_head_norm)
    pos = jnp.arange(S)
    q = jnp.concatenate([q[..., :MLA_NOPE_DIM], rope(q[..., MLA_NOPE_DIM:], pos)], axis=-1)
    k = jnp.concatenate([k[..., :MLA_NOPE_DIM], rope(k[..., MLA_NOPE_DIM:], pos)], axis=-1)
    o = causal_softmax_attention(q, k, v, 1.0 / math.sqrt(MLA_QK_DIM)).reshape(B, S, MLA_WIDTH)
    return x + (o * jax.nn.silu(gate)) @ w_out


def swa_layer(x, norm_g, w_in, q_head_norm, k_head_norm, sinks, w_out):
    B, S, _ = x.shape
    proj = rms_norm(x, norm_g) @ w_in
    kv_w = SWA_KV_HEADS * SWA_HEAD_DIM
    c1 = SWA_WIDTH
    c2 = c1 + kv_w
    c3 = c2 + kv_w
    q = proj[..., :c1].reshape(B, S, SWA_HEADS, SWA_HEAD_DIM)
    k = proj[..., c1:c2].reshape(B, S, SWA_KV_HEADS, SWA_HEAD_DIM)
    v = proj[..., c2:c3].reshape(B, S, SWA_KV_HEADS, SWA_HEAD_DIM)
    gate = proj[..., c3:]
    q = rms_norm(q, q_head_norm)
    k = rms_norm(k, k_head_norm)
    o = sliding_window_sink_attention(q, k, v, sinks).reshape(B, S, SWA_WIDTH)
    return x + (o * jax.nn.silu(gate)) @ w_out


def _dense(key, fan_in, fan_out):
    return jax.random.normal(key, (fan_in, fan_out), jnp.float32) * fan_in ** -0.5


def _gain(key, n):
    return 1.0 + 0.02 * jax.random.normal(key, (n,), jnp.float32)


def _fwd_setup_inputs(seed: int = 0) -> dict:
    key = jax.random.key(seed)
    ks = jax.random.split(key, 24)
    sb_in = 4 * SB_WIDTH
    mla_in = MLA_Q_RANK + MLA_KV_RANK + MLA_ROPE_DIM + MLA_WIDTH
    swa_in = SWA_WIDTH + 2 * SWA_KV_HEADS * SWA_HEAD_DIM + SWA_WIDTH
    return {
        'x': jax.random.normal(ks[0], (BATCH, SEQ, D_MODEL), jnp.float32),
        'l0_norm': _gain(ks[1], D_MODEL),
        'l0_w_in': _dense(ks[2], D_MODEL, sb_in),
        'l0_w_out': _dense(ks[3], SB_WIDTH, D_MODEL),
        'l1_norm': _gain(ks[4], D_MODEL),
        'l1_w_in': _dense(ks[5], D_MODEL, mla_in),
        'l1_q_a_norm': _gain(ks[6], MLA_Q_RANK),
        'l1_w_uq': _dense(ks[7], MLA_Q_RANK, MLA_HEADS * MLA_QK_DIM),
        'l1_kv_a_norm': _gain(ks[8], MLA_KV_RANK),
        'l1_w_ukv': _dense(ks[9], MLA_KV_RANK, MLA_HEADS * (MLA_NOPE_DIM + MLA_V_DIM)),
        'l1_q_head_norm': _gain(ks[10], MLA_QK_DIM),
        'l1_k_head_norm': _gain(ks[11], MLA_QK_DIM),
        'l1_w_out': _dense(ks[12], MLA_WIDTH, D_MODEL),
        'l2_norm': _gain(ks[13], D_MODEL),
        'l2_w_in': _dense(ks[14], D_MODEL, swa_in),
        'l2_q_head_norm': _gain(ks[15], SWA_HEAD_DIM),
        'l2_k_head_norm': _gain(ks[16], SWA_HEAD_DIM),
        'l2_sinks': 0.5 * jax.random.normal(ks[17], (SWA_HEADS,), jnp.float32),
        'l2_w_out': _dense(ks[18], SWA_WIDTH, D_MODEL),
        'l3_norm': _gain(ks[19], D_MODEL),
        'l3_w_in': _dense(ks[20], D_MODEL, sb_in),
        'l3_w_out': _dense(ks[21], SB_WIDTH, D_MODEL),
    }


def _fwd_reference(x, l0_norm, l0_w_in, l0_w_out,
              l1_norm, l1_w_in, l1_q_a_norm, l1_w_uq, l1_kv_a_norm, l1_w_ukv,
              l1_q_head_norm, l1_k_head_norm, l1_w_out,
              l2_norm, l2_w_in, l2_q_head_norm, l2_k_head_norm, l2_sinks, l2_w_out,
              l3_norm, l3_w_in, l3_w_out):
    layer_params = [
        (l0_norm, l0_w_in, l0_w_out),
        (l1_norm, l1_w_in, l1_q_a_norm, l1_w_uq, l1_kv_a_norm, l1_w_ukv,
         l1_q_head_norm, l1_k_head_norm, l1_w_out),
        (l2_norm, l2_w_in, l2_q_head_norm, l2_k_head_norm, l2_sinks, l2_w_out),
        (l3_norm, l3_w_in, l3_w_out),
    ]
    mixers = (stick_breaking_layer, mla_layer, swa_layer)
    for i in range(DEPTH):
        x = mixers[i % N_MIXERS](x, *layer_params[i])
    return x


import jax as _jax
import jax.numpy as _jnp

TWIN_FORMAT = 'train_step'
FWD_PARAMS = ['x', 'l0_norm', 'l0_w_in', 'l0_w_out', 'l1_norm', 'l1_w_in', 'l1_q_a_norm', 'l1_w_uq', 'l1_kv_a_norm', 'l1_w_ukv', 'l1_q_head_norm', 'l1_k_head_norm', 'l1_w_out', 'l2_norm', 'l2_w_in', 'l2_q_head_norm', 'l2_k_head_norm', 'l2_sinks', 'l2_w_out', 'l3_norm', 'l3_w_in', 'l3_w_out']
TWIN_WEIGHTS = ['l0_norm', 'l0_w_in', 'l0_w_out', 'l1_norm', 'l1_w_in', 'l1_q_a_norm', 'l1_w_uq', 'l1_kv_a_norm', 'l1_w_ukv', 'l1_q_head_norm', 'l1_k_head_norm', 'l1_w_out', 'l2_norm', 'l2_w_in', 'l2_q_head_norm', 'l2_k_head_norm', 'l2_sinks', 'l2_w_out', 'l3_norm', 'l3_w_in', 'l3_w_out']
TWIN_DIFF_INPUT = 'x'
TWIN_INPUTS = ['x', 'l0_norm', 'l0_w_in', 'l0_w_out', 'l1_norm', 'l1_w_in', 'l1_q_a_norm', 'l1_w_uq', 'l1_kv_a_norm', 'l1_w_ukv', 'l1_q_head_norm', 'l1_k_head_norm', 'l1_w_out', 'l2_norm', 'l2_w_in', 'l2_q_head_norm', 'l2_k_head_norm', 'l2_sinks', 'l2_w_out', 'l3_norm', 'l3_w_in', 'l3_w_out', 'loss_target', 'm_l0_norm', 'm_l0_w_in', 'm_l0_w_out', 'm_l1_norm', 'm_l1_w_in', 'm_l1_q_a_norm', 'm_l1_w_uq', 'm_l1_kv_a_norm', 'm_l1_w_ukv', 'm_l1_q_head_norm', 'm_l1_k_head_norm', 'm_l1_w_out', 'm_l2_norm', 'm_l2_w_in', 'm_l2_q_head_norm', 'm_l2_k_head_norm', 'm_l2_sinks', 'm_l2_w_out', 'm_l3_norm', 'm_l3_w_in', 'm_l3_w_out', 'v_l0_norm', 'v_l0_w_in', 'v_l0_w_out', 'v_l1_norm', 'v_l1_w_in', 'v_l1_q_a_norm', 'v_l1_w_uq', 'v_l1_kv_a_norm', 'v_l1_w_ukv', 'v_l1_q_head_norm', 'v_l1_k_head_norm', 'v_l1_w_out', 'v_l2_norm', 'v_l2_w_in', 'v_l2_q_head_norm', 'v_l2_k_head_norm', 'v_l2_sinks', 'v_l2_w_out', 'v_l3_norm', 'v_l3_w_in', 'v_l3_w_out']
TWIN_OUTPUTS = ['loss', 'grad_x', 'grad_l0_norm', 'grad_l0_w_in', 'grad_l0_w_out', 'grad_l1_norm', 'grad_l1_w_in', 'grad_l1_q_a_norm', 'grad_l1_w_uq', 'grad_l1_kv_a_norm', 'grad_l1_w_ukv', 'grad_l1_q_head_norm', 'grad_l1_k_head_norm', 'grad_l1_w_out', 'grad_l2_norm', 'grad_l2_w_in', 'grad_l2_q_head_norm', 'grad_l2_k_head_norm', 'grad_l2_sinks', 'grad_l2_w_out', 'grad_l3_norm', 'grad_l3_w_in', 'grad_l3_w_out', 'delta_l0_norm', 'delta_l0_w_in', 'delta_l0_w_out', 'delta_l1_norm', 'delta_l1_w_in', 'delta_l1_q_a_norm', 'delta_l1_w_uq', 'delta_l1_kv_a_norm', 'delta_l1_w_ukv', 'delta_l1_q_head_norm', 'delta_l1_k_head_norm', 'delta_l1_w_out', 'delta_l2_norm', 'delta_l2_w_in', 'delta_l2_q_head_norm', 'delta_l2_k_head_norm', 'delta_l2_sinks', 'delta_l2_w_out', 'delta_l3_norm', 'delta_l3_w_in', 'delta_l3_w_out', 'new_m_l0_norm', 'new_m_l0_w_in', 'new_m_l0_w_out', 'new_m_l1_norm', 'new_m_l1_w_in', 'new_m_l1_q_a_norm', 'new_m_l1_w_uq', 'new_m_l1_kv_a_norm', 'new_m_l1_w_ukv', 'new_m_l1_q_head_norm', 'new_m_l1_k_head_norm', 'new_m_l1_w_out', 'new_m_l2_norm', 'new_m_l2_w_in', 'new_m_l2_q_head_norm', 'new_m_l2_k_head_norm', 'new_m_l2_sinks', 'new_m_l2_w_out', 'new_m_l3_norm', 'new_m_l3_w_in', 'new_m_l3_w_out', 'new_v_l0_norm', 'new_v_l0_w_in', 'new_v_l0_w_out', 'new_v_l1_norm', 'new_v_l1_w_in', 'new_v_l1_q_a_norm', 'new_v_l1_w_uq', 'new_v_l1_kv_a_norm', 'new_v_l1_w_ukv', 'new_v_l1_q_head_norm', 'new_v_l1_k_head_norm', 'new_v_l1_w_out', 'new_v_l2_norm', 'new_v_l2_w_in', 'new_v_l2_q_head_norm', 'new_v_l2_k_head_norm', 'new_v_l2_sinks', 'new_v_l2_w_out', 'new_v_l3_norm', 'new_v_l3_w_in', 'new_v_l3_w_out']
TWIN_LEAF_KINDS = {'loss': 'loss', 'grad_x': 'grad_x', 'grad_l0_norm': 'grad_w', 'grad_l0_w_in': 'grad_w', 'grad_l0_w_out': 'grad_w', 'grad_l1_norm': 'grad_w', 'grad_l1_w_in': 'grad_w', 'grad_l1_q_a_norm': 'grad_w', 'grad_l1_w_uq': 'grad_w', 'grad_l1_kv_a_norm': 'grad_w', 'grad_l1_w_ukv': 'grad_w', 'grad_l1_q_head_norm': 'grad_w', 'grad_l1_k_head_norm': 'grad_w', 'grad_l1_w_out': 'grad_w', 'grad_l2_norm': 'grad_w', 'grad_l2_w_in': 'grad_w', 'grad_l2_q_head_norm': 'grad_w', 'grad_l2_k_head_norm': 'grad_w', 'grad_l2_sinks': 'grad_w', 'grad_l2_w_out': 'grad_w', 'grad_l3_norm': 'grad_w', 'grad_l3_w_in': 'grad_w', 'grad_l3_w_out': 'grad_w', 'delta_l0_norm': 'delta_w', 'delta_l0_w_in': 'delta_w', 'delta_l0_w_out': 'delta_w', 'delta_l1_norm': 'delta_w', 'delta_l1_w_in': 'delta_w', 'delta_l1_q_a_norm': 'delta_w', 'delta_l1_w_uq': 'delta_w', 'delta_l1_kv_a_norm': 'delta_w', 'delta_l1_w_ukv': 'delta_w', 'delta_l1_q_head_norm': 'delta_w', 'delta_l1_k_head_norm': 'delta_w', 'delta_l1_w_out': 'delta_w', 'delta_l2_norm': 'delta_w', 'delta_l2_w_in': 'delta_w', 'delta_l2_q_head_norm': 'delta_w', 'delta_l2_k_head_norm': 'delta_w', 'delta_l2_sinks': 'delta_w', 'delta_l2_w_out': 'delta_w', 'delta_l3_norm': 'delta_w', 'delta_l3_w_in': 'delta_w', 'delta_l3_w_out': 'delta_w', 'new_m_l0_norm': 'new_m', 'new_m_l0_w_in': 'new_m', 'new_m_l0_w_out': 'new_m', 'new_m_l1_norm': 'new_m', 'new_m_l1_w_in': 'new_m', 'new_m_l1_q_a_norm': 'new_m', 'new_m_l1_w_uq': 'new_m', 'new_m_l1_kv_a_norm': 'new_m', 'new_m_l1_w_ukv': 'new_m', 'new_m_l1_q_head_norm': 'new_m', 'new_m_l1_k_head_norm': 'new_m', 'new_m_l1_w_out': 'new_m', 'new_m_l2_norm': 'new_m', 'new_m_l2_w_in': 'new_m', 'new_m_l2_q_head_norm': 'new_m', 'new_m_l2_k_head_norm': 'new_m', 'new_m_l2_sinks': 'new_m', 'new_m_l2_w_out': 'new_m', 'new_m_l3_norm': 'new_m', 'new_m_l3_w_in': 'new_m', 'new_m_l3_w_out': 'new_m', 'new_v_l0_norm': 'new_v', 'new_v_l0_w_in': 'new_v', 'new_v_l0_w_out': 'new_v', 'new_v_l1_norm': 'new_v', 'new_v_l1_w_in': 'new_v', 'new_v_l1_q_a_norm': 'new_v', 'new_v_l1_w_uq': 'new_v', 'new_v_l1_kv_a_norm': 'new_v', 'new_v_l1_w_ukv': 'new_v', 'new_v_l1_q_head_norm': 'new_v', 'new_v_l1_k_head_norm': 'new_v', 'new_v_l1_w_out': 'new_v', 'new_v_l2_norm': 'new_v', 'new_v_l2_w_in': 'new_v', 'new_v_l2_q_head_norm': 'new_v', 'new_v_l2_k_head_norm': 'new_v', 'new_v_l2_sinks': 'new_v', 'new_v_l2_w_out': 'new_v', 'new_v_l3_norm': 'new_v', 'new_v_l3_w_in': 'new_v', 'new_v_l3_w_out': 'new_v'}


def _forward(args):
    return _fwd_reference(*[args[k] for k in FWD_PARAMS])


def _output_shape():
    out = _jax.eval_shape(lambda: _forward(_fwd_setup_inputs(0)))
    return out.shape, out.dtype

N_MICROBATCH = 1
ADAM_LR = 0.001
ADAM_B1 = 0.9
ADAM_B2 = 0.999
ADAM_EPS = 1e-08
ADAM_WD = 0.01
ADAM_STEP = 10
PER_EXAMPLE_BATCH_AXIS = {'x': 0, 'loss_target': 0}
SHARED_INPUTS = []
_WEIGHT_DTYPES = {'l0_norm': _jnp.float32, 'l0_w_in': _jnp.float32, 'l0_w_out': _jnp.float32, 'l1_norm': _jnp.float32, 'l1_w_in': _jnp.float32, 'l1_q_a_norm': _jnp.float32, 'l1_w_uq': _jnp.float32, 'l1_kv_a_norm': _jnp.float32, 'l1_w_ukv': _jnp.float32, 'l1_q_head_norm': _jnp.float32, 'l1_k_head_norm': _jnp.float32, 'l1_w_out': _jnp.float32, 'l2_norm': _jnp.float32, 'l2_w_in': _jnp.float32, 'l2_q_head_norm': _jnp.float32, 'l2_k_head_norm': _jnp.float32, 'l2_sinks': _jnp.float32, 'l2_w_out': _jnp.float32, 'l3_norm': _jnp.float32, 'l3_w_in': _jnp.float32, 'l3_w_out': _jnp.float32}
MOMENT_SCALE = {'l0_norm': 1.166807e+01, 'l0_w_in': 1.404567e-01, 'l0_w_out': 1.568785e-01, 'l1_norm': 1.573472e-01, 'l1_w_in': 8.455730e-02, 'l1_q_a_norm': 9.145296e-02, 'l1_w_uq': 3.618717e-02, 'l1_kv_a_norm': 8.845263e-01, 'l1_w_ukv': 4.775379e-02, 'l1_q_head_norm': 3.371516e-01, 'l1_k_head_norm': 3.374606e-01, 'l1_w_out': 5.100406e-02, 'l2_norm': 1.843875e+00, 'l2_w_in': 8.098590e-02, 'l2_q_head_norm': 7.094494e+00, 'l2_k_head_norm': 7.087071e+00, 'l2_sinks': 1.408942e+01, 'l2_w_out': 6.348087e-02, 'l3_norm': 1.170260e+01, 'l3_w_in': 1.256538e-01, 'l3_w_out': 1.329425e-01}


def _to_microbatches(a, axis):
    t = _jnp.moveaxis(a, axis, 0)
    t = t.reshape((N_MICROBATCH, t.shape[0] // N_MICROBATCH) + t.shape[1:])
    return _jnp.moveaxis(t, 1, axis + 1)


def setup_inputs(seed: int = 0) -> dict:
    inp = _fwd_setup_inputs(seed)
    key = _jax.random.fold_in(_jax.random.key(seed), 7919)
    shape, _ = _output_shape()
    out = dict(inp)
    out["loss_target"] = _jax.random.normal(_jax.random.fold_in(key, 0), shape, _jnp.float32)
    for i, name in enumerate(TWIN_WEIGHTS):
        w = inp[name].astype(_jnp.float32)
        if MOMENT_SCALE is None:
            s = _jnp.sqrt(_jnp.mean(_jnp.square(w)) + 1e-30)
        else:
            s = MOMENT_SCALE[name]
        km, kv = _jax.random.split(_jax.random.fold_in(key, i + 1))
        out[name] = w
        out["m_" + name] = s * _jax.random.normal(km, w.shape, _jnp.float32)
        out["v_" + name] = (s * s) * _jax.random.uniform(kv, w.shape, _jnp.float32, 0.5, 1.5)
    if N_MICROBATCH > 1:
        for name, axis in PER_EXAMPLE_BATCH_AXIS.items():
            out[name] = _to_microbatches(out[name], axis)
    return {'x': out['x'], 'l0_norm': out['l0_norm'], 'l0_w_in': out['l0_w_in'], 'l0_w_out': out['l0_w_out'], 'l1_norm': out['l1_norm'], 'l1_w_in': out['l1_w_in'], 'l1_q_a_norm': out['l1_q_a_norm'], 'l1_w_uq': out['l1_w_uq'], 'l1_kv_a_norm': out['l1_kv_a_norm'], 'l1_w_ukv': out['l1_w_ukv'], 'l1_q_head_norm': out['l1_q_head_norm'], 'l1_k_head_norm': out['l1_k_head_norm'], 'l1_w_out': out['l1_w_out'], 'l2_norm': out['l2_norm'], 'l2_w_in': out['l2_w_in'], 'l2_q_head_norm': out['l2_q_head_norm'], 'l2_k_head_norm': out['l2_k_head_norm'], 'l2_sinks': out['l2_sinks'], 'l2_w_out': out['l2_w_out'], 'l3_norm': out['l3_norm'], 'l3_w_in': out['l3_w_in'], 'l3_w_out': out['l3_w_out'], 'loss_target': out['loss_target'], 'm_l0_norm': out['m_l0_norm'], 'm_l0_w_in': out['m_l0_w_in'], 'm_l0_w_out': out['m_l0_w_out'], 'm_l1_norm': out['m_l1_norm'], 'm_l1_w_in': out['m_l1_w_in'], 'm_l1_q_a_norm': out['m_l1_q_a_norm'], 'm_l1_w_uq': out['m_l1_w_uq'], 'm_l1_kv_a_norm': out['m_l1_kv_a_norm'], 'm_l1_w_ukv': out['m_l1_w_ukv'], 'm_l1_q_head_norm': out['m_l1_q_head_norm'], 'm_l1_k_head_norm': out['m_l1_k_head_norm'], 'm_l1_w_out': out['m_l1_w_out'], 'm_l2_norm': out['m_l2_norm'], 'm_l2_w_in': out['m_l2_w_in'], 'm_l2_q_head_norm': out['m_l2_q_head_norm'], 'm_l2_k_head_norm': out['m_l2_k_head_norm'], 'm_l2_sinks': out['m_l2_sinks'], 'm_l2_w_out': out['m_l2_w_out'], 'm_l3_norm': out['m_l3_norm'], 'm_l3_w_in': out['m_l3_w_in'], 'm_l3_w_out': out['m_l3_w_out'], 'v_l0_norm': out['v_l0_norm'], 'v_l0_w_in': out['v_l0_w_in'], 'v_l0_w_out': out['v_l0_w_out'], 'v_l1_norm': out['v_l1_norm'], 'v_l1_w_in': out['v_l1_w_in'], 'v_l1_q_a_norm': out['v_l1_q_a_norm'], 'v_l1_w_uq': out['v_l1_w_uq'], 'v_l1_kv_a_norm': out['v_l1_kv_a_norm'], 'v_l1_w_ukv': out['v_l1_w_ukv'], 'v_l1_q_head_norm': out['v_l1_q_head_norm'], 'v_l1_k_head_norm': out['v_l1_k_head_norm'], 'v_l1_w_out': out['v_l1_w_out'], 'v_l2_norm': out['v_l2_norm'], 'v_l2_w_in': out['v_l2_w_in'], 'v_l2_q_head_norm': out['v_l2_q_head_norm'], 'v_l2_k_head_norm': out['v_l2_k_head_norm'], 'v_l2_sinks': out['v_l2_sinks'], 'v_l2_w_out': out['v_l2_w_out'], 'v_l3_norm': out['v_l3_norm'], 'v_l3_w_in': out['v_l3_w_in'], 'v_l3_w_out': out['v_l3_w_out']}


def _loss(weights, diff, rest, loss_target):
    with _jax.named_scope("forward"):
        args = {**rest, TWIN_DIFF_INPUT: diff, **{k: w.astype(_WEIGHT_DTYPES[k]) for k, w in weights.items()}}
        y = _forward(args)
    with _jax.named_scope("loss_head"):
        err = _jnp.square(y.astype(_jnp.float32) - loss_target)
        return 0.5 * _jnp.sum(_jnp.mean(err, axis=-1)) if err.ndim else 0.5 * err


def _adamw(w, g, m, v):
    m = ADAM_B1 * m + (1.0 - ADAM_B1) * g
    v = ADAM_B2 * v + (1.0 - ADAM_B2) * _jnp.square(g)
    m_hat = m / (1.0 - ADAM_B1 ** ADAM_STEP)
    v_hat = v / (1.0 - ADAM_B2 ** ADAM_STEP)
    delta = -ADAM_LR * (m_hat / (_jnp.sqrt(v_hat) + ADAM_EPS) + ADAM_WD * w)
    return delta, m, v


def reference(x, l0_norm, l0_w_in, l0_w_out, l1_norm, l1_w_in, l1_q_a_norm, l1_w_uq, l1_kv_a_norm, l1_w_ukv, l1_q_head_norm, l1_k_head_norm, l1_w_out, l2_norm, l2_w_in, l2_q_head_norm, l2_k_head_norm, l2_sinks, l2_w_out, l3_norm, l3_w_in, l3_w_out, loss_target, m_l0_norm, m_l0_w_in, m_l0_w_out, m_l1_norm, m_l1_w_in, m_l1_q_a_norm, m_l1_w_uq, m_l1_kv_a_norm, m_l1_w_ukv, m_l1_q_head_norm, m_l1_k_head_norm, m_l1_w_out, m_l2_norm, m_l2_w_in, m_l2_q_head_norm, m_l2_k_head_norm, m_l2_sinks, m_l2_w_out, m_l3_norm, m_l3_w_in, m_l3_w_out, v_l0_norm, v_l0_w_in, v_l0_w_out, v_l1_norm, v_l1_w_in, v_l1_q_a_norm, v_l1_w_uq, v_l1_kv_a_norm, v_l1_w_ukv, v_l1_q_head_norm, v_l1_k_head_norm, v_l1_w_out, v_l2_norm, v_l2_w_in, v_l2_q_head_norm, v_l2_k_head_norm, v_l2_sinks, v_l2_w_out, v_l3_norm, v_l3_w_in, v_l3_w_out):
    given = dict(x=x, l0_norm=l0_norm, l0_w_in=l0_w_in, l0_w_out=l0_w_out, l1_norm=l1_norm, l1_w_in=l1_w_in, l1_q_a_norm=l1_q_a_norm, l1_w_uq=l1_w_uq, l1_kv_a_norm=l1_kv_a_norm, l1_w_ukv=l1_w_ukv, l1_q_head_norm=l1_q_head_norm, l1_k_head_norm=l1_k_head_norm, l1_w_out=l1_w_out, l2_norm=l2_norm, l2_w_in=l2_w_in, l2_q_head_norm=l2_q_head_norm, l2_k_head_norm=l2_k_head_norm, l2_sinks=l2_sinks, l2_w_out=l2_w_out, l3_norm=l3_norm, l3_w_in=l3_w_in, l3_w_out=l3_w_out, loss_target=loss_target, m_l0_norm=m_l0_norm, m_l0_w_in=m_l0_w_in, m_l0_w_out=m_l0_w_out, m_l1_norm=m_l1_norm, m_l1_w_in=m_l1_w_in, m_l1_q_a_norm=m_l1_q_a_norm, m_l1_w_uq=m_l1_w_uq, m_l1_kv_a_norm=m_l1_kv_a_norm, m_l1_w_ukv=m_l1_w_ukv, m_l1_q_head_norm=m_l1_q_head_norm, m_l1_k_head_norm=m_l1_k_head_norm, m_l1_w_out=m_l1_w_out, m_l2_norm=m_l2_norm, m_l2_w_in=m_l2_w_in, m_l2_q_head_norm=m_l2_q_head_norm, m_l2_k_head_norm=m_l2_k_head_norm, m_l2_sinks=m_l2_sinks, m_l2_w_out=m_l2_w_out, m_l3_norm=m_l3_norm, m_l3_w_in=m_l3_w_in, m_l3_w_out=m_l3_w_out, v_l0_norm=v_l0_norm, v_l0_w_in=v_l0_w_in, v_l0_w_out=v_l0_w_out, v_l1_norm=v_l1_norm, v_l1_w_in=v_l1_w_in, v_l1_q_a_norm=v_l1_q_a_norm, v_l1_w_uq=v_l1_w_uq, v_l1_kv_a_norm=v_l1_kv_a_norm, v_l1_w_ukv=v_l1_w_ukv, v_l1_q_head_norm=v_l1_q_head_norm, v_l1_k_head_norm=v_l1_k_head_norm, v_l1_w_out=v_l1_w_out, v_l2_norm=v_l2_norm, v_l2_w_in=v_l2_w_in, v_l2_q_head_norm=v_l2_q_head_norm, v_l2_k_head_norm=v_l2_k_head_norm, v_l2_sinks=v_l2_sinks, v_l2_w_out=v_l2_w_out, v_l3_norm=v_l3_norm, v_l3_w_in=v_l3_w_in, v_l3_w_out=v_l3_w_out)
    weights = {n: given[n] for n in TWIN_WEIGHTS}
    shared = {n: given[n] for n in SHARED_INPUTS}
    per_example = {n: given[n] for n in ['x']}
    grad_fn = _jax.value_and_grad(_loss, argnums=(0, 1))

    def one_microbatch(ex, loss_target):
        ex = dict(ex)
        diff = ex.pop(TWIN_DIFF_INPUT)
        return grad_fn(weights, diff, {**shared, **ex}, loss_target)

    if N_MICROBATCH == 1:
        loss, (grad_w, grad_x) = one_microbatch(per_example, given["loss_target"])
    else:
        def body(carry, xs):
            loss_sum, grad_sum = carry
            l_k, (gw_k, gx_k) = one_microbatch(xs[0], xs[1])
            with _jax.named_scope("update"):
                return (loss_sum + l_k, _jax.tree.map(_jnp.add, grad_sum, gw_k)), gx_k

        init = (_jnp.zeros((), _jnp.float32), _jax.tree.map(_jnp.zeros_like, weights))
        (loss, grad_w), grad_x = _jax.lax.scan(body, init, (per_example, given["loss_target"]))
    with _jax.named_scope("update"):
        delta_w, new_m, new_v = {}, {}, {}
        for n in TWIN_WEIGHTS:
            delta_w[n], new_m[n], new_v[n] = _adamw(weights[n], grad_w[n], given["m_" + n], given["v_" + n])
    return (loss, grad_x, *[grad_w[n] for n in TWIN_WEIGHTS], *[delta_w[n] for n in TWIN_WEIGHTS],
            *[new_m[n] for n in TWIN_WEIGHTS], *[new_v[n] for n in TWIN_WEIGHTS])
```

```python
import math

import numpy as np
import jax
import jax.numpy as jnp
from jax import lax
from jax.experimental import pallas as pl
from jax.experimental.pallas import tpu as pltpu

F32 = jnp.float32
BF16 = jnp.bfloat16
SDS = jax.ShapeDtypeStruct
MESH_ID = pl.DeviceIdType.MESH

D_MODEL = 1024
NORM_EPS = 1e-6
N_DEV = 8
LANES = 128
HALF = 64
ROWS_PACK = 2176
ROWS_USED = 2120
SMALL_ROWS = 40

ADAM_LR = 0.001
ADAM_B1 = 0.9
ADAM_B2 = 0.999
ADAM_EPS = 1e-08
ADAM_WD = 0.01
ADAM_STEP = 10

NT_DIMS = (((1,), (1,)), ((), ()))
TN_DIMS = (((0,), (0,)), ((), ()))


def _cparams(sem=None, vmem_mb=48):
    return pltpu.CompilerParams(dimension_semantics=sem, vmem_limit_bytes=vmem_mb * 2 ** 20)


def _dot(a, b):
    return jnp.dot(a, b, preferred_element_type=F32)


def _dot_nt(a, b):
    return lax.dot_general(a, b, NT_DIMS, preferred_element_type=F32)


def _dot_tn(a, b):
    return lax.dot_general(a, b, TN_DIMS, preferred_element_type=F32)


def _split_dot(a, b):
    hi = a.astype(BF16)
    lo = (a - hi.astype(F32)).astype(BF16)
    return _dot(hi, b) + _dot(lo, b)


def _sigmoid(x):
    return 1.0 / (1.0 + jnp.exp(-x))


def norm_matmul(x, colblk, g, w, name):
    s = x.shape[0]
    k, n = w.shape
    tm = min(512, s)
    tn = min(512, n)
    assert s % tm == 0 and n % tn == 0

    def body(x_ref, g_ref, w_ref, o_ref, xn_ref, xn_s):
        @pl.when(pl.program_id(1) == 0)
        def _():
            xf = x_ref[...].astype(F32)
            ms = jnp.mean(xf * xf, axis=-1, keepdims=True)
            y = (xf * lax.rsqrt(ms + NORM_EPS) * g_ref[...]).astype(BF16)
            xn_s[...] = y
            xn_ref[...] = y

        o_ref[...] = _dot(xn_s[...], w_ref[...]).astype(o_ref.dtype)

    return pl.pallas_call(
        body, name=name, grid=(s // tm, n // tn),
        in_specs=[pl.BlockSpec((tm, k), lambda i, j: (i, colblk)),
                  pl.BlockSpec((1, k), lambda i, j: (0, 0)),
                  pl.BlockSpec((k, tn), lambda i, j: (0, j))],
        out_specs=[pl.BlockSpec((tm, tn), lambda i, j: (i, j)),
                   pl.BlockSpec((tm, k), lambda i, j: (i, 0))],
        out_shape=[SDS((s, n), BF16), SDS((s, k), BF16)],
        scratch_shapes=[pltpu.VMEM((tm, k), BF16)],
        compiler_params=_cparams(("parallel", "arbitrary")),
    )(x, g, w)


def gate_out_fwd(o, proj, gate_blk, w_out, x, name):
    s = x.shape[0]
    d = D_MODEL
    tm = min(512, s)

    def body(o_ref, g_ref, w_ref, x_ref, y_ref):
        g = g_ref[...].astype(F32)
        hg = (o_ref[...].astype(F32) * (g * _sigmoid(g))).astype(BF16)
        y_ref[...] = x_ref[...] + _dot(hg, w_ref[...])

    return pl.pallas_call(
        body, name=name, grid=(s // tm,),
        in_specs=[pl.BlockSpec((tm, d), lambda i: (i, 0)),
                  pl.BlockSpec((tm, d), lambda i: (i, gate_blk)),
                  pl.BlockSpec((d, d), lambda i: (0, 0)),
                  pl.BlockSpec((tm, d), lambda i: (i, 0))],
        out_specs=pl.BlockSpec((tm, d), lambda i: (i, 0)),
        out_shape=SDS((s, d), F32),
        compiler_params=_cparams(("parallel",)),
    )(o, proj, w_out, x)


def gate_out_bwd(dy, w_out, o, proj, gate_blk, name):
    s = dy.shape[0]
    d = D_MODEL
    tm = min(512, s)

    def body(dy_ref, w_ref, o_ref, g_ref, do_ref, dg_ref, hg_ref):
        dh = _dot_nt(dy_ref[...].astype(BF16), w_ref[...])
        g = g_ref[...].astype(F32)
        o = o_ref[...].astype(F32)
        sg = _sigmoid(g)
        silu = g * sg
        do_ref[...] = (dh * silu).astype(BF16)
        dg_ref[...] = (dh * o * (sg * (1.0 + g * (1.0 - sg)))).astype(BF16)
        hg_ref[...] = (o * silu).astype(BF16)

    blk = pl.BlockSpec((tm, d), lambda i: (i, 0))
    return pl.pallas_call(
        body, name=name, grid=(s // tm,),
        in_specs=[blk, pl.BlockSpec((d, d), lambda i: (0, 0)), blk,
                  pl.BlockSpec((tm, d), lambda i: (i, gate_blk))],
        out_specs=[blk, blk, blk],
        out_shape=[SDS((s, d), BF16)] * 3,
        compiler_params=_cparams(("parallel",)),
    )(dy, w_out, o, proj)


def tn_matmul(a, b, name):
    s, k = a.shape
    n = b.shape[1]
    tk = min(512, k)
    tn = min(512, n)
    ts = min(1024, s)
    assert k % tk == 0 and n % tn == 0 and s % ts == 0

    def body(a_ref, b_ref, o_ref):
        @pl.when(pl.program_id(2) == 0)
        def _():
            o_ref[...] = jnp.zeros_like(o_ref)

        o_ref[...] += _dot_tn(a_ref[...].astype(BF16), b_ref[...].astype(BF16))

    return pl.pallas_call(
        body, name=name, grid=(k // tk, n // tn, s // ts),
        in_specs=[pl.BlockSpec((ts, tk), lambda i, j, r: (r, i)),
                  pl.BlockSpec((ts, tn), lambda i, j, r: (r, j))],
        out_specs=pl.BlockSpec((tk, tn), lambda i, j, r: (i, j)),
        out_shape=SDS((k, n), F32),
        compiler_params=_cparams(("parallel", "parallel", "arbitrary")),
    )(a, b)


def nt_norm_bwd(dp, w, x, xcol, g, res, name, out_dtype):
    s, n = dp.shape
    k = w.shape[0]
    tm = min(512, s)
    tk = min(512, n)
    assert s % tm == 0 and n % tk == 0
    nk = n // tk
    has_res = res is not None

    def body(*refs):
        if has_res:
            dp_ref, w_ref, x_ref, g_ref, r_ref, dx_ref, dg_ref, acc = refs
        else:
            dp_ref, w_ref, x_ref, g_ref, dx_ref, dg_ref, acc = refs
        i, j = pl.program_id(0), pl.program_id(1)

        @pl.when(j == 0)
        def _():
            acc[...] = jnp.zeros_like(acc)

        @pl.when((i == 0) & (j == 0))
        def _():
            dg_ref[...] = jnp.zeros_like(dg_ref)

        acc[...] += _dot_nt(dp_ref[...], w_ref[...])

        @pl.when(j == nk - 1)
        def _():
            xf = x_ref[...].astype(F32)
            rstd = lax.rsqrt(jnp.mean(xf * xf, axis=-1, keepdims=True) + NORM_EPS)
            xhat = xf * rstd
            dxn = acc[...]
            dg_ref[...] += jnp.sum(dxn * xhat, axis=0, keepdims=True)
            dxh = dxn * g_ref[...]
            dx = rstd * (dxh - xhat * jnp.mean(dxh * xhat, axis=-1, keepdims=True))
            if has_res:
                dx = dx + r_ref[...]
            dx_ref[...] = dx.astype(out_dtype)

    in_specs = [pl.BlockSpec((tm, tk), lambda i, j: (i, j)),
                pl.BlockSpec((k, tk), lambda i, j: (0, j)),
                pl.BlockSpec((tm, k), lambda i, j: (i, xcol)),
                pl.BlockSpec((1, k), lambda i, j: (0, 0))]
    args = [dp, w, x, g]
    if has_res:
        in_specs.append(pl.BlockSpec((tm, k), lambda i, j: (i, 0)))
        args.append(res)
    return pl.pallas_call(
        body, name=name, grid=(s // tm, nk),
        in_specs=in_specs,
        out_specs=[pl.BlockSpec((tm, k), lambda i, j: (i, 0)),
                   pl.BlockSpec((1, k), lambda i, j: (0, 0))],
        out_shape=[SDS((s, k), out_dtype), SDS((1, k), F32)],
        scratch_shapes=[pltpu.VMEM((tm, k), F32)],
        compiler_params=_cparams(("arbitrary", "arbitrary")),
    )(*args)


def loss_kernel(y, target, name):
    s, d = y.shape
    tm = min(512, s)

    def body(y_ref, t_ref, l_ref, dy_ref):
        @pl.when(pl.program_id(0) == 0)
        def _():
            l_ref[...] = jnp.zeros_like(l_ref)

        e = y_ref[...] - t_ref[...]
        dy_ref[...] = e * (1.0 / d)
        l_ref[...] += 0.5 * jnp.sum(jnp.mean(e * e, axis=-1, keepdims=True))

    blk = pl.BlockSpec((tm, d), lambda i: (i, 0))
    return pl.pallas_call(
        body, name=name, grid=(s // tm,),
        in_specs=[blk, blk],
        out_specs=[pl.BlockSpec((8, LANES), lambda i: (0, 0)), blk],
        out_shape=[SDS((8, LANES), F32), SDS((s, d), F32)],
        compiler_params=_cparams(("arbitrary",)),
    )(y, target)


def _tile_masks(t):
    row = lax.broadcasted_iota(jnp.int32, (t, t), 0)
    col = lax.broadcasted_iota(jnp.int32, (t, t), 1)
    return row, col


def sb_attn_fwd(proj, name):
    s = proj.shape[0]
    t = min(256, s)
    nq = s // t
    scale = 1.0 / math.sqrt(HALF)

    def body(q_ref, k_ref, v_ref, o_ref):
        lane = lax.broadcasted_iota(jnp.int32, (1, LANES), 1)
        row, col = _tile_masks(t)
        u_incl = (row >= col).astype(BF16)
        tri = col < row

        def tile(qm, kj, c, acc, diag):
            k0 = pl.multiple_of(kj * t, t)
            k = k_ref[pl.ds(k0, t), :]
            v = v_ref[pl.ds(k0, t), :]
            z = _dot_nt(qm, k) * scale
            e = jnp.exp(-jnp.abs(z))
            lf = -(jnp.maximum(z, 0.0) + jnp.log(1.0 + e))
            if diag:
                lf = jnp.where(tri, lf, 0.0)
            incl = _split_dot(lf, u_incl) + c
            a = jnp.exp(z + incl)
            if diag:
                a = jnp.where(tri, a, 0.0)
            acc = acc + _split_dot(a, v)
            c = c + jnp.sum(lf, axis=1, keepdims=True)
            return c, acc

        def q_body(qi, carry):
            q0 = pl.multiple_of(qi * t, t)
            q = q_ref[pl.ds(q0, t), :]
            outs = []
            for half in range(2):
                hm = (lane < HALF) if half == 0 else (lane >= HALF)
                qm = jnp.where(hm, q, jnp.zeros_like(q))
                c, acc = tile(qm, qi, jnp.zeros((t, 1), F32), jnp.zeros((t, LANES), F32), True)

                def k_body(jj, cr, qm=qm):
                    return tile(qm, qi - jj, cr[0], cr[1], False)

                c, acc = lax.fori_loop(1, qi + 1, k_body, (c, acc))
                outs.append(acc)
            o_ref[pl.ds(q0, t), :] = jnp.where(lane < HALF, outs[0], outs[1])
            return carry

        lax.fori_loop(0, nq, q_body, 0)

    return pl.pallas_call(
        body, name=name, grid=(8,),
        in_specs=[pl.BlockSpec((s, LANES), lambda p: (0, p)),
                  pl.BlockSpec((s, LANES), lambda p: (0, 8 + p)),
                  pl.BlockSpec((s, LANES), lambda p: (0, 16 + p))],
        out_specs=pl.BlockSpec((s, LANES), lambda p: (0, p)),
        out_shape=SDS((s, D_MODEL), F32),
        compiler_params=_cparams(("parallel",)),
    )(proj, proj, proj)


def sb_attn_bwd(proj, o, do, name):
    s = proj.shape[0]
    t = min(256, s)
    nq = s // t
    scale = 1.0 / math.sqrt(HALF)

    def body(q_ref, k_ref, v_ref, o_ref, do_ref, dq_ref, dk_ref, dv_ref, dk_s, dv_s):
        dk_s[...] = jnp.zeros_like(dk_s)
        dv_s[...] = jnp.zeros_like(dv_s)
        lane = lax.broadcasted_iota(jnp.int32, (1, LANES), 1)
        row, col = _tile_masks(t)
        u_incl = (row >= col).astype(BF16)
        u_excl = (row > col).astype(BF16)
        tri = col < row

        def tile(qm, dom, tsum, kj, c, r, dq, diag):
            k0 = pl.multiple_of(kj * t, t)
            k = k_ref[pl.ds(k0, t), :]
            v = v_ref[pl.ds(k0, t), :]
            z = _dot_nt(qm, k) * scale
            e = jnp.exp(-jnp.abs(z))
            den = 1.0 + e
            lf = -(jnp.maximum(z, 0.0) + jnp.log(den))
            sig = jnp.where(z >= 0.0, 1.0, e) / den
            if diag:
                lf = jnp.where(tri, lf, 0.0)
            incl = _split_dot(lf, u_incl) + c
            a = jnp.exp(z + incl)
            if diag:
                a = jnp.where(tri, a, 0.0)
            dl = a * _dot_nt(dom, v)
            pre = tsum - r - _split_dot(dl, u_excl)
            dz = (dl - sig * pre) * scale
            if diag:
                dz = jnp.where(tri, dz, 0.0)
            dzb = dz.astype(BF16)
            dq = dq + _dot(dzb, k)
            dk_s[pl.ds(k0, t), :] += _dot_tn(dzb, qm)
            dv_s[pl.ds(k0, t), :] += _dot_tn(a.astype(BF16), dom)
            c = c + jnp.sum(lf, axis=1, keepdims=True)
            r = r + jnp.sum(dl, axis=1, keepdims=True)
            return c, r, dq

        def q_body(qi, carry):
            q0 = pl.multiple_of(qi * t, t)
            q = q_ref[pl.ds(q0, t), :]
            dob = do_ref[pl.ds(q0, t), :]
            prod = dob.astype(F32) * o_ref[pl.ds(q0, t), :].astype(F32)
            outs = []
            for half in range(2):
                hm = (lane < HALF) if half == 0 else (lane >= HALF)
                qm = jnp.where(hm, q, jnp.zeros_like(q))
                dom = jnp.where(hm, dob, jnp.zeros_like(dob))
                tsum = jnp.sum(jnp.where(hm, prod, 0.0), axis=1, keepdims=True)
                zc = jnp.zeros((t, 1), F32)
                c, r, dq = tile(qm, dom, tsum, qi, zc, zc, jnp.zeros((t, LANES), F32), True)

                def k_body(jj, cr, qm=qm, dom=dom, tsum=tsum):
                    return tile(qm, dom, tsum, qi - jj, cr[0], cr[1], cr[2], False)

                c, r, dq = lax.fori_loop(1, qi + 1, k_body, (c, r, dq))
                outs.append(dq)
            dq_ref[pl.ds(q0, t), :] = jnp.where(lane < HALF, outs[0], outs[1]).astype(BF16)
            return carry

        lax.fori_loop(0, nq, q_body, 0)
        dk_ref[...] = dk_s[...].astype(BF16)
        dv_ref[...] = dv_s[...].astype(BF16)

    blk = lambda off: pl.BlockSpec((s, LANES), lambda p: (0, off + p))
    return pl.pallas_call(
        body, name=name, grid=(8,),
        in_specs=[blk(0), blk(8), blk(16), blk(0), blk(0)],
        out_specs=[blk(0), blk(0), blk(0)],
        out_shape=[SDS((s, D_MODEL), BF16)] * 3,
        scratch_shapes=[pltpu.VMEM((s, LANES), F32), pltpu.VMEM((s, LANES), F32)],
        compiler_params=_cparams(("parallel",)),
    )(proj, proj, proj, o, do)


MLA_HEADS = 8
MLA_QK = 192
MLA_SCALE = 1.0 / math.sqrt(MLA_QK)


def _rope_tables(s):
    inv_freq = 10000.0 ** (-jnp.arange(32, dtype=F32) / 32)
    ang = jnp.arange(s, dtype=F32)[:, None] * inv_freq[None, :]
    cos = jnp.tile(jnp.cos(ang), (1, 4))
    sin = jnp.sin(ang)
    sin_signed = jnp.tile(jnp.concatenate([-sin, sin], axis=1), (1, 2))
    return cos, sin_signed


def _partner(u, lane):
    return jnp.where((lane % HALF) < 32, pltpu.roll(u, 96, 1), pltpu.roll(u, 32, 1))


def _mla_head_inputs(h, qa_ref, kva_ref, kpe, lane):
    a = h % 2
    hm = (lane < HALF) if a == 0 else (lane >= HALF)
    qn = qa_ref[:, h * 128:(h + 1) * 128].astype(F32)
    qr = jnp.where(hm, qa_ref[:, 1024 + (h // 2) * 128:1024 + (h // 2 + 1) * 128].astype(F32), 0.0)
    kn = kva_ref[:, h * 256:h * 256 + 128].astype(F32)
    kr = kpe if a == 0 else pltpu.roll(kpe, HALF, 1)
    return qn, qr, kn, kr


def mla_prep_fwd(qa, kva, proj, gq, gk, cos, sin, name):
    s = qa.shape[0]
    tm = min(256, s)

    def body(qa_ref, kva_ref, kpe_ref, gq_ref, gk_ref, cos_ref, sin_ref, q_out, k_out):
        lane = lax.broadcasted_iota(jnp.int32, (1, LANES), 1)
        kpe = kpe_ref[...].astype(F32)
        cs, sn = cos_ref[...], sin_ref[...]
        for h in range(MLA_HEADS):
            qn, qr, kn, kr = _mla_head_inputs(h, qa_ref, kva_ref, kpe, lane)
            for xn, xr, g_ref, out in ((qn, qr, gq_ref, q_out), (kn, kr, gk_ref, k_out)):
                ss = jnp.sum(xn * xn, axis=1, keepdims=True) + jnp.sum(xr * xr, axis=1, keepdims=True)
                rstd = lax.rsqrt(ss * (1.0 / MLA_QK) + NORM_EPS)
                yn = xn * rstd * g_ref[:, 0:128]
                ur = xr * rstd * g_ref[:, 128:256]
                yr = ur * cs + _partner(ur, lane) * sn
                out[:, h * 256:h * 256 + 128] = yn.astype(BF16)
                out[:, h * 256 + 128:h * 256 + 256] = yr.astype(BF16)

    row = lambda w, c: pl.BlockSpec((tm, w), lambda i: (i, c))
    full = lambda w: pl.BlockSpec((1, w), lambda i: (0, 0))
    return pl.pallas_call(
        body, name=name, grid=(s // tm,),
        in_specs=[row(1536, 0), row(2048, 0), row(128, 11), full(256), full(256), row(128, 0), row(128, 0)],
        out_specs=[row(2048, 0), row(2048, 0)],
        out_shape=[SDS((s, 2048), BF16)] * 2,
        compiler_params=_cparams(("parallel",)),
    )(qa, kva, proj, gq, gk, cos, sin)


def mla_prep_bwd(dq_p, dk_p, dv, qa, kva, proj, gq, gk, cos, sin, name):
    s = qa.shape[0]
    tm = min(256, s)

    def body(dq_ref, dk_ref, dv_ref, qa_ref, kva_ref, kpe_ref, gq_ref, gk_ref, cos_ref, sin_ref,
             dqa_ref, dkva_ref, dkpe_ref, dgq_ref, dgk_ref):
        @pl.when(pl.program_id(0) == 0)
        def _():
            dgq_ref[...] = jnp.zeros_like(dgq_ref)
            dgk_ref[...] = jnp.zeros_like(dgk_ref)

        lane = lax.broadcasted_iota(jnp.int32, (1, LANES), 1)
        kpe = kpe_ref[...].astype(F32)
        cs, sn = cos_ref[...], sin_ref[...]
        dkpe = jnp.zeros((tm, LANES), F32)
        dqr_pair = None
        for h in range(MLA_HEADS):
            qn, qr, kn, kr = _mla_head_inputs(h, qa_ref, kva_ref, kpe, lane)
            res = []
            for xn, xr, g_ref, d_ref, dg_ref in ((qn, qr, gq_ref, dq_ref, dgq_ref),
                                                 (kn, kr, gk_ref, dk_ref, dgk_ref)):
                ss = jnp.sum(xn * xn, axis=1, keepdims=True) + jnp.sum(xr * xr, axis=1, keepdims=True)
                rstd = lax.rsqrt(ss * (1.0 / MLA_QK) + NORM_EPS)
                hn, hr = xn * rstd, xr * rstd
                dyn = d_ref[:, h * 256:h * 256 + 128]
                dyr = d_ref[:, h * 256 + 128:h * 256 + 256]
                dur = dyr * cs - _partner(dyr, lane) * sn
                dg_ref[:, 0:128] += jnp.sum(dyn * hn, axis=0, keepdims=True)
                dg_ref[:, 128:256] += jnp.sum(dur * hr, axis=0, keepdims=True)
                dhn = dyn * g_ref[:, 0:128]
                dhr = dur * g_ref[:, 128:256]
                mu = (jnp.sum(dhn * hn, axis=1, keepdims=True)
                      + jnp.sum(dhr * hr, axis=1, keepdims=True)) * (1.0 / MLA_QK)
                res.append((rstd * (dhn - hn * mu), rstd * (dhr - hr * mu)))
            (dqn, dqr), (dkn, dkr) = res
            dqa_ref[:, h * 128:(h + 1) * 128] = dqn.astype(BF16)
            if h % 2 == 0:
                dqr_pair = dqr
            else:
                dqa_ref[:, 1024 + (h // 2) * 128:1024 + (h // 2 + 1) * 128] = (dqr_pair + dqr).astype(BF16)
            dkva_ref[:, h * 256:h * 256 + 128] = dkn.astype(BF16)
            dkva_ref[:, h * 256 + 128:h * 256 + 256] = dv_ref[:, h * 128:(h + 1) * 128]
            dkpe = dkpe + (dkr if h % 2 == 0 else pltpu.roll(dkr, HALF, 1))
        dkpe_ref[...] = dkpe.astype(BF16)

    row = lambda w, c: pl.BlockSpec((tm, w), lambda i: (i, c))
    full = lambda w: pl.BlockSpec((1, w), lambda i: (0, 0))
    return pl.pallas_call(
        body, name=name, grid=(s // tm,),
        in_specs=[row(2048, 0), row(2048, 0), row(1024, 0), row(1536, 0), row(2048, 0), row(128, 11),
                  full(256), full(256), row(128, 0), row(128, 0)],
        out_specs=[row(1536, 0), row(2048, 0), row(128, 0), full(256), full(256)],
        out_shape=[SDS((s, 1536), BF16), SDS((s, 2048), BF16), SDS((s, 128), BF16),
                   SDS((1, 256), F32), SDS((1, 256), F32)],
        compiler_params=_cparams(("arbitrary",)),
    )(dq_p, dk_p, dv, qa, kva, proj, gq, gk, cos, sin)


def mla_attn_fwd(q_p, k_p, kva, name):
    s = q_p.shape[0]
    t = min(256, s)
    nq = s // t

    def body(q_ref, k_ref, v_ref, o_ref, lse_ref):
        row, col = _tile_masks(t)
        causal = col <= row

        def q_body(qi, carry):
            q0 = pl.multiple_of(qi * t, t)
            q = q_ref[pl.ds(q0, t), :]
            sc = _dot_nt(q, k_ref[pl.ds(q0, t), :]) * MLA_SCALE
            sc = jnp.where(causal, sc, -1e30)
            m = jnp.max(sc, axis=1, keepdims=True)
            p = jnp.exp(sc - m)
            l = jnp.sum(p, axis=1, keepdims=True)
            acc = _dot(p.astype(BF16), v_ref[pl.ds(q0, t), :])

            def k_body(kj, cr):
                m, l, acc = cr
                k0 = pl.multiple_of(kj * t, t)
                sc = _dot_nt(q, k_ref[pl.ds(k0, t), :]) * MLA_SCALE
                m_new = jnp.maximum(m, jnp.max(sc, axis=1, keepdims=True))
                alpha = jnp.exp(m - m_new)
                p = jnp.exp(sc - m_new)
                l = alpha * l + jnp.sum(p, axis=1, keepdims=True)
                acc = alpha * acc + _dot(p.astype(BF16), v_ref[pl.ds(k0, t), :])
                return m_new, l, acc

            m, l, acc = lax.fori_loop(0, qi, k_body, (m, l, acc))
            o_ref[pl.ds(q0, t), :] = acc / l
            lse_ref[pl.ds(q0, t), :] = jnp.broadcast_to(m + jnp.log(l), (t, LANES))
            return carry

        lax.fori_loop(0, nq, q_body, 0)

    return pl.pallas_call(
        body, name=name, grid=(MLA_HEADS,),
        in_specs=[pl.BlockSpec((s, 256), lambda h: (0, h)),
                  pl.BlockSpec((s, 256), lambda h: (0, h)),
                  pl.BlockSpec((s, 128), lambda h: (0, 2 * h + 1))],
        out_specs=[pl.BlockSpec((s, 128), lambda h: (0, h)),
                   pl.BlockSpec((s, 128), lambda h: (0, h))],
        out_shape=[SDS((s, 1024), F32), SDS((s, 1024), F32)],
        compiler_params=_cparams(("parallel",)),
    )(q_p, k_p, kva)


def mla_attn_bwd(q_p, k_p, kva, o, do, lse, name):
    s = q_p.shape[0]
    t = min(256, s)
    nq = s // t

    def body(q_ref, k_ref, v_ref, o_ref, do_ref, lse_ref, dq_ref, dk_ref, dv_ref, dv_s):
        dk_ref[...] = jnp.zeros_like(dk_ref)
        dv_s[...] = jnp.zeros_like(dv_s)
        row, col = _tile_masks(t)
        causal = col <= row

        def tile(q, dob, lse, delta, kj, dq, diag):
            k0 = pl.multiple_of(kj * t, t)
            k = k_ref[pl.ds(k0, t), :]
            v = v_ref[pl.ds(k0, t), :]
            sc = _dot_nt(q, k) * MLA_SCALE
            p = jnp.exp(sc - lse)
            if diag:
                p = jnp.where(causal, p, 0.0)
            ds = (p * (_dot_nt(dob, v) - delta) * MLA_SCALE).astype(BF16)
            dq = dq + _dot(ds, k)
            dk_ref[pl.ds(k0, t), :] += _dot_tn(ds, q)
            dv_s[pl.ds(k0, t), :] += _dot_tn(p.astype(BF16), dob)
            return dq

        def q_body(qi, carry):
            q0 = pl.multiple_of(qi * t, t)
            q = q_ref[pl.ds(q0, t), :]
            dob = do_ref[pl.ds(q0, t), :]
            lse = lse_ref[pl.ds(q0, t), 0:1]
            delta = jnp.sum(dob.astype(F32) * o_ref[pl.ds(q0, t), :].astype(F32), axis=1, keepdims=True)
            dq = tile(q, dob, lse, delta, qi, jnp.zeros((t, 256), F32), True)
            dq = lax.fori_loop(0, qi, lambda kj, dq: tile(q, dob, lse, delta, kj, dq, False), dq)
            dq_ref[pl.ds(q0, t), :] = dq
            return carry

        lax.fori_loop(0, nq, q_body, 0)
        dv_ref[...] = dv_s[...].astype(BF16)

    b256 = pl.BlockSpec((s, 256), lambda h: (0, h))
    b128 = pl.BlockSpec((s, 128), lambda h: (0, h))
    return pl.pallas_call(
        body, name=name, grid=(MLA_HEADS,),
        in_specs=[b256, b256, pl.BlockSpec((s, 128), lambda h: (0, 2 * h + 1)), b128, b128, b128],
        out_specs=[b256, b256, b128],
        out_shape=[SDS((s, 2048), F32), SDS((s, 2048), F32), SDS((s, 1024), BF16)],
        scratch_shapes=[pltpu.VMEM((s, 128), F32)],
        compiler_params=_cparams(("parallel",), vmem_mb=56),
    )(q_p, k_p, kva, o, do, lse)


SWA_HEADS = 16
SWA_BLOCK = 128


def _swa_head(tile_idx, half):
    return (2 * (tile_idx // 4) + half) * 4 + tile_idx % 4


SWA_Q_PERM = np.concatenate([np.arange(_swa_head(tt, a) * HALF, (_swa_head(tt, a) + 1) * HALF)
                             for tt in range(8) for a in range(2)])
SWA_SLOPES = [2.0 ** (-8.0 * (h + 1) / SWA_HEADS) for h in range(SWA_HEADS)]


def swa_prep_fwd(proj, g128, name):
    s = proj.shape[0]
    tm = min(512, s)

    def body(q_ref, k_ref, g_ref, o_ref):
        lane = lax.broadcasted_iota(jnp.int32, (1, LANES), 1)
        lo = lane < HALF
        for tt in range(10):
            if tt < 8:
                x = q_ref[:, tt * 128:(tt + 1) * 128].astype(F32)
            else:
                x = k_ref[:, (tt - 8) * 128:(tt - 7) * 128].astype(F32)
            sq = x * x
            sa = jnp.sum(jnp.where(lo, sq, 0.0), axis=1, keepdims=True)
            sb = jnp.sum(sq, axis=1, keepdims=True) - sa
            rstd = jnp.where(lo, lax.rsqrt(sa * (1.0 / HALF) + NORM_EPS), lax.rsqrt(sb * (1.0 / HALF) + NORM_EPS))
            gi = 0 if tt < 8 else 1
            o_ref[:, tt * 128:(tt + 1) * 128] = (x * rstd * g_ref[gi:gi + 1, :]).astype(BF16)

    return pl.pallas_call(
        body, name=name, grid=(s // tm,),
        in_specs=[pl.BlockSpec((tm, 1024), lambda i: (i, 0)), pl.BlockSpec((tm, 256), lambda i: (i, 8)),
                  pl.BlockSpec((2, 128), lambda i: (0, 0))],
        out_specs=pl.BlockSpec((tm, 1280), lambda i: (i, 0)),
        out_shape=SDS((s, 1280), BF16),
        compiler_params=_cparams(("parallel",)),
    )(proj, proj, g128)


def swa_prep_bwd(dqn, dkn, proj, g128, name):
    s = proj.shape[0]
    tm = min(512, s)
    nsteps = s // tm

    def body(dq_ref, dk_ref, xq_ref, xk_ref, g_ref, o_ref, dg_ref):
        @pl.when(pl.program_id(0) == 0)
        def _():
            dg_ref[...] = jnp.zeros_like(dg_ref)

        lane = lax.broadcasted_iota(jnp.int32, (1, LANES), 1)
        lo = lane < HALF
        for tt in range(10):
            if tt < 8:
                x = xq_ref[:, tt * 128:(tt + 1) * 128].astype(F32)
                dy = dq_ref[:, tt * 128:(tt + 1) * 128].astype(F32)
            else:
                x = xk_ref[:, (tt - 8) * 128:(tt - 7) * 128].astype(F32)
                dy = dk_ref[:, (tt - 8) * 128:(tt - 7) * 128].astype(F32)
            gi = 0 if tt < 8 else 1
            sq = x * x
            sa = jnp.sum(jnp.where(lo, sq, 0.0), axis=1, keepdims=True)
            sb = jnp.sum(sq, axis=1, keepdims=True) - sa
            rstd = jnp.where(lo, lax.rsqrt(sa * (1.0 / HALF) + NORM_EPS), lax.rsqrt(sb * (1.0 / HALF) + NORM_EPS))
            xh = x * rstd
            dg_ref[gi:gi + 1, :] += jnp.sum(dy * xh, axis=0, keepdims=True)
            dxh = dy * g_ref[gi:gi + 1, :]
            pr = dxh * xh
            ma = jnp.sum(jnp.where(lo, pr, 0.0), axis=1, keepdims=True)
            mb = jnp.sum(pr, axis=1, keepdims=True) - ma
            mu = jnp.where(lo, ma, mb) * (1.0 / HALF)
            o_ref[:, tt * 128:(tt + 1) * 128] = (rstd * (dxh - xh * mu)).astype(BF16)

        @pl.when(pl.program_id(0) == nsteps - 1)
        def _():
            acc = dg_ref[...]
            dg_ref[...] = acc + pltpu.roll(acc, HALF, 1)

    return pl.pallas_call(
        body, name=name, grid=(nsteps,),
        in_specs=[pl.BlockSpec((tm, 1024), lambda i: (i, 0)), pl.BlockSpec((tm, 256), lambda i: (i, 0)),
                  pl.BlockSpec((tm, 1024), lambda i: (i, 0)), pl.BlockSpec((tm, 256), lambda i: (i, 8)),
                  pl.BlockSpec((2, 128), lambda i: (0, 0))],
        out_specs=[pl.BlockSpec((tm, 1280), lambda i: (i, 0)), pl.BlockSpec((2, 128), lambda i: (0, 0))],
        out_shape=[SDS((s, 1280), BF16), SDS((2, 128), F32)],
        compiler_params=_cparams(("arbitrary",)),
    )(dqn, dkn, proj, proj, g128)


def _swa_masks(n):
    b = SWA_BLOCK
    row = lax.broadcasted_iota(jnp.int32, (b, 2 * b), 0)
    col = lax.broadcasted_iota(jnp.int32, (b, 2 * b), 1)
    rel = row + b - col
    valid = (rel >= 0) & (rel < b) & ((col >= b) | (n > 0))
    return rel.astype(F32), valid


def _swa_probs(qm, kt, relf, valid, slope, sink):
    z = _dot_nt(qm, kt) * (1.0 / math.sqrt(HALF))
    sc = jnp.where(valid, z - slope * relf, -1e30)
    m = jnp.maximum(jnp.max(sc, axis=1, keepdims=True), sink)
    e = jnp.where(valid, jnp.exp(sc - m), 0.0)
    es = jnp.exp(sink - m)
    den = jnp.sum(e, axis=1, keepdims=True) + es
    return e / den, es / den


def swa_attn_fwd(qkn, proj, sinks, name):
    s = qkn.shape[0]
    b = SWA_BLOCK
    nb = s // b

    def body(sink_ref, q_ref, kc_ref, kp_ref, vc_ref, vp_ref, o_ref):
        n = pl.program_id(0)
        lane = lax.broadcasted_iota(jnp.int32, (1, LANES), 1)
        relf, valid = _swa_masks(n)
        for tt in range(8):
            gp = tt // 4
            kt = jnp.concatenate([kp_ref[:, gp * 128:(gp + 1) * 128], kc_ref[:, gp * 128:(gp + 1) * 128]], axis=0)
            vt = jnp.concatenate([vp_ref[:, gp * 128:(gp + 1) * 128], vc_ref[:, gp * 128:(gp + 1) * 128]], axis=0)
            qt = q_ref[:, tt * 128:(tt + 1) * 128]
            outs = []
            for a in range(2):
                h = _swa_head(tt, a)
                hm = (lane < HALF) if a == 0 else (lane >= HALF)
                qm = jnp.where(hm, qt, jnp.zeros_like(qt))
                p, _ = _swa_probs(qm, kt, relf, valid, SWA_SLOPES[h], sink_ref[h])
                outs.append(_dot(p.astype(BF16), vt))
            o_ref[:, tt * 128:(tt + 1) * 128] = jnp.where(lane < HALF, outs[0], outs[1])

    cur = lambda w, c: pl.BlockSpec((b, w), lambda n: (n, c))
    prev = lambda w, c: pl.BlockSpec((b, w), lambda n: (jnp.maximum(n - 1, 0), c))
    return pl.pallas_call(
        body, name=name, grid=(nb,),
        in_specs=[pl.BlockSpec(memory_space=pltpu.SMEM), cur(1024, 0), cur(256, 4), prev(256, 4),
                  cur(256, 9), prev(256, 9)],
        out_specs=cur(1024, 0),
        out_shape=SDS((s, 1024), F32),
        compiler_params=_cparams(("parallel",)),
    )(sinks, qkn, qkn, qkn, proj, proj)


def swa_attn_bwd(qkn, proj, sinks, do, name):
    s = qkn.shape[0]
    b = SWA_BLOCK
    nb = s // b

    def body(sink_ref, q_ref, kc_ref, kp_ref, vc_ref, vp_ref, do_ref, dq_ref, dk_ref, dv_ref, ds_ref):
        n = pl.program_id(0)

        @pl.when(n == 0)
        def _():
            dk_ref[...] = jnp.zeros_like(dk_ref)
            dv_ref[...] = jnp.zeros_like(dv_ref)
            ds_ref[...] = jnp.zeros_like(ds_ref)

        lane = lax.broadcasted_iota(jnp.int32, (1, LANES), 1)
        relf, valid = _swa_masks(n)
        cur0 = pl.multiple_of(n * b, b)
        prev0 = pl.multiple_of(jnp.maximum(n - 1, 0) * b, b)
        dsink = jnp.zeros((1, LANES), F32)
        for gp in range(2):
            kt = jnp.concatenate([kp_ref[:, gp * 128:(gp + 1) * 128], kc_ref[:, gp * 128:(gp + 1) * 128]], axis=0)
            vt = jnp.concatenate([vp_ref[:, gp * 128:(gp + 1) * 128], vc_ref[:, gp * 128:(gp + 1) * 128]], axis=0)
            dkt = jnp.zeros((2 * b, LANES), F32)
            dvt = jnp.zeros((2 * b, LANES), F32)
            for tt in range(gp * 4, gp * 4 + 4):
                qt = q_ref[:, tt * 128:(tt + 1) * 128]
                dot_ = do_ref[:, tt * 128:(tt + 1) * 128]
                outs = []
                for a in range(2):
                    h = _swa_head(tt, a)
                    hm = (lane < HALF) if a == 0 else (lane >= HALF)
                    qm = jnp.where(hm, qt, jnp.zeros_like(qt))
                    dom = jnp.where(hm, dot_, jnp.zeros_like(dot_))
                    p, ps = _swa_probs(qm, kt, relf, valid, SWA_SLOPES[h], sink_ref[h])
                    dp = _dot_nt(dom, vt)
                    delta = jnp.sum(p * dp, axis=1, keepdims=True)
                    dz = (p * (dp - delta) * (1.0 / math.sqrt(HALF))).astype(BF16)
                    dsink = dsink + jnp.where(lane == h, -jnp.sum(ps * delta), 0.0)
                    outs.append(_dot(dz, kt))
                    dkt = dkt + _dot_tn(dz, qm)
                    dvt = dvt + _dot_tn(p.astype(BF16), dom)
                dq_ref[:, tt * 128:(tt + 1) * 128] = jnp.where(lane < HALF, outs[0], outs[1]).astype(BF16)
            cols = slice(gp * 128, (gp + 1) * 128)
            dk_ref[pl.ds(prev0, b), cols] += dkt[:b]
            dk_ref[pl.ds(cur0, b), cols] += dkt[b:]
            dv_ref[pl.ds(prev0, b), cols] += dvt[:b]
            dv_ref[pl.ds(cur0, b), cols] += dvt[b:]
        ds_ref[0:1, :] += dsink

    cur = lambda w, c: pl.BlockSpec((b, w), lambda n: (n, c))
    prev = lambda w, c: pl.BlockSpec((b, w), lambda n: (jnp.maximum(n - 1, 0), c))
    full = pl.BlockSpec((s, 256), lambda n: (0, 0))
    return pl.pallas_call(
        body, name=name, grid=(nb,),
        in_specs=[pl.BlockSpec(memory_space=pltpu.SMEM), cur(1024, 0), cur(256, 4), prev(256, 4),
                  cur(256, 9), prev(256, 9), cur(1024, 0)],
        out_specs=[cur(1024, 0), full, full, pl.BlockSpec((8, LANES), lambda n: (0, 0))],
        out_shape=[SDS((s, 1024), BF16), SDS((s, 256), F32), SDS((s, 256), F32), SDS((8, LANES), F32)],
        compiler_params=_cparams(("arbitrary",)),
    )(sinks, qkn, qkn, qkn, proj, proj, do)


def _my_pos():
    return lax.axis_index("x"), lax.axis_index("y"), lax.axis_index("c")


ANY = pl.BlockSpec(memory_space=pl.ANY)


def all_gather_packed(xs, name):
    r, n = xs.shape

    def body(x_ref, out_ref, send_sems, recv_sems, local_sem):
        x, y, c = _my_pos()
        me, sibling = (x, y, c), (x, y, 1 - c)
        chips = [(1 - x, y), (x, 1 - y), (1 - x, 1 - y)]

        def slot(px, py, pc):
            return out_ref.at[4 * px + 2 * py + pc]

        def copy(k, block, to, src=None):
            return pltpu.make_async_remote_copy(
                src_ref=slot(*block) if src is None else src, dst_ref=slot(*block),
                send_sem=send_sems.at[k], recv_sem=recv_sems.at[k],
                device_id=to, device_id_type=MESH_ID)

        mine = pltpu.make_async_copy(x_ref, slot(*me), local_sem)
        mine.start()
        first = [copy(0, me, sibling, src=x_ref)]
        first += [copy(1 + j, me, (*chip, c), src=x_ref) for j, chip in enumerate(chips)]
        for cp in first:
            cp.start()
        passed = [copy(4 + j, (*chip, c), sibling) for j, chip in enumerate(chips)]
        for j, chip in enumerate(chips):
            copy(1 + j, (*chip, c), me).wait_recv()
            passed[j].start()
        copy(0, sibling, me).wait_recv()
        for j, chip in enumerate(chips):
            copy(4 + j, (*chip, 1 - c), me).wait_recv()
        for cp in first + passed:
            cp.wait_send()
        mine.wait()

    return pl.pallas_call(
        body, name=name,
        out_shape=SDS((N_DEV, r, n), xs.dtype),
        in_specs=[ANY], out_specs=ANY,
        scratch_shapes=[pltpu.SemaphoreType.DMA((7,)), pltpu.SemaphoreType.DMA((7,)), pltpu.SemaphoreType.DMA],
    )(xs)


def rs_sibling_exchange(g, name):
    _, r, n = g.shape

    def body(g_ref, ra_ref, send_sems, recv_sems):
        x, y, c = _my_pos()
        copies = [pltpu.make_async_remote_copy(
            src_ref=g_ref.at[2 * chip + (1 - c)], dst_ref=ra_ref.at[chip],
            send_sem=send_sems.at[chip], recv_sem=recv_sems.at[chip],
            device_id=(x, y, 1 - c), device_id_type=MESH_ID) for chip in range(4)]
        for cp in copies:
            cp.start()
        for cp in copies:
            cp.wait()

    return pl.pallas_call(
        body, name=name, out_shape=SDS((4, r, n), g.dtype), in_specs=[ANY], out_specs=ANY,
        scratch_shapes=[pltpu.SemaphoreType.DMA((4,)), pltpu.SemaphoreType.DMA((4,))],
    )(g)


def rs_pair_add(g, ra, cidx, name):
    _, r, n = g.shape
    tr = r // 8

    def body(c_ref, g_ref, ra_ref, p_ref):
        p_ref[...] = (g_ref[...].astype(F32) + ra_ref[...].astype(F32)).astype(p_ref.dtype)

    return pl.pallas_call(
        body, name=name,
        grid_spec=pltpu.PrefetchScalarGridSpec(
            num_scalar_prefetch=1, grid=(4, r // tr),
            in_specs=[pl.BlockSpec((1, tr, n), lambda i, j, c: (2 * i + c[0], j, 0)),
                      pl.BlockSpec((1, tr, n), lambda i, j, c: (i, j, 0))],
            out_specs=pl.BlockSpec((1, tr, n), lambda i, j, c: (i, j, 0))),
        out_shape=SDS((4, r, n), g.dtype),
        compiler_params=_cparams(("parallel", "parallel")),
    )(cidx, g, ra)


def rs_cross_exchange(p, name):
    _, r, n = p.shape

    def body(p_ref, rb_ref, send_sems, recv_sems):
        x, y, c = _my_pos()
        copies = []
        for k, (dx, dy) in enumerate(((1, 0), (0, 1), (1, 1))):
            tx = 1 - x if dx else x
            ty = 1 - y if dy else y
            copies.append(pltpu.make_async_remote_copy(
                src_ref=p_ref.at[2 * tx + ty], dst_ref=rb_ref.at[k],
                send_sem=send_sems.at[k], recv_sem=recv_sems.at[k],
                device_id=(tx, ty, c), device_id_type=MESH_ID))
        for cp in copies:
            cp.start()
        for cp in copies:
            cp.wait()

    return pl.pallas_call(
        body, name=name, out_shape=SDS((3, r, n), p.dtype), in_specs=[ANY], out_specs=ANY,
        scratch_shapes=[pltpu.SemaphoreType.DMA((3,)), pltpu.SemaphoreType.DMA((3,))],
    )(p)


def _adamw(w, g, m, v):
    m = ADAM_B1 * m + (1.0 - ADAM_B1) * g
    v = ADAM_B2 * v + (1.0 - ADAM_B2) * (g * g)
    m_hat = m / (1.0 - ADAM_B1 ** ADAM_STEP)
    v_hat = v / (1.0 - ADAM_B2 ** ADAM_STEP)
    delta = -ADAM_LR * (m_hat / (jnp.sqrt(v_hat) + ADAM_EPS) + ADAM_WD * w)
    return delta, m, v


def rs_final_adamw(p, rb, chipidx, w, m, v, name):
    _, r, n = p.shape
    tr = r // 8

    def body(c_ref, p_ref, rb_ref, w_ref, m_ref, v_ref, g_out, d_out, m_out, v_out):
        g = p_ref[0].astype(F32)
        for k in range(3):
            g = g + rb_ref[k].astype(F32)
        d, mn, vn = _adamw(w_ref[...], g, m_ref[...], v_ref[...])
        g_out[...] = g
        d_out[...] = d
        m_out[...] = mn
        v_out[...] = vn

    blk = pl.BlockSpec((tr, n), lambda j, c: (j, 0))
    return pl.pallas_call(
        body, name=name,
        grid_spec=pltpu.PrefetchScalarGridSpec(
            num_scalar_prefetch=1, grid=(r // tr,),
            in_specs=[pl.BlockSpec((1, tr, n), lambda j, c: (c[0], j, 0)),
                      pl.BlockSpec((3, tr, n), lambda j, c: (0, j, 0)), blk, blk, blk],
            out_specs=[blk, blk, blk, blk]),
        out_shape=[SDS((r, n), F32)] * 4,
        compiler_params=_cparams(("parallel",)),
    )(chipidx, p, rb, w, m, v)


def small_allreduce_adamw(g, w, m, v, name):
    rows = g.shape[0]

    def body(g_ref, w_ref, m_ref, v_ref, g_out, d_out, m_out, v_out, buf, send_sems, recv_sems):
        x, y, c = _my_pos()
        my = 4 * x + 2 * y + c
        buf[my] = g_ref[...]
        copies = []
        for k in range(1, N_DEV):
            dx, dy, dc = (k >> 2) & 1, (k >> 1) & 1, k & 1
            tgt = (1 - x if dx else x, 1 - y if dy else y, 1 - c if dc else c)
            copies.append(pltpu.make_async_remote_copy(
                src_ref=buf.at[my], dst_ref=buf.at[my],
                send_sem=send_sems.at[k - 1], recv_sem=recv_sems.at[k - 1],
                device_id=tgt, device_id_type=MESH_ID))
        for cp in copies:
            cp.start()
        for cp in copies:
            cp.wait()
        tot = buf[0]
        for j in range(1, N_DEV):
            tot = tot + buf[j]
        d, mn, vn = _adamw(w_ref[...], tot, m_ref[...], v_ref[...])
        g_out[...] = tot
        d_out[...] = d
        m_out[...] = mn
        v_out[...] = vn

    vm = pl.BlockSpec(memory_space=pltpu.VMEM)
    return pl.pallas_call(
        body, name=name, out_shape=[SDS((rows, LANES), F32)] * 4,
        in_specs=[vm] * 4, out_specs=[vm] * 4,
        scratch_shapes=[pltpu.VMEM((N_DEV, rows, LANES), F32),
                        pltpu.SemaphoreType.DMA((N_DEV - 1,)), pltpu.SemaphoreType.DMA((N_DEV - 1,))],
    )(g, w, m, v)


BIG = ["l0_w_in", "l0_w_out", "l1_w_in", "l1_w_uq", "l1_w_ukv", "l1_w_out", "l2_w_in", "l2_w_out",
       "l3_w_in", "l3_w_out"]
SMALL = ["l0_norm", "l1_norm", "l1_q_a_norm", "l1_kv_a_norm", "l1_q_head_norm", "l1_k_head_norm",
         "l2_norm", "l2_q_head_norm", "l2_k_head_norm", "l2_sinks", "l3_norm"]
ROW_SHARDED = {"l0_w_out", "l1_w_out", "l2_w_out", "l3_w_out"}

MLA_UQ_PERM = np.concatenate([np.arange(h * 192, h * 192 + 128) for h in range(8)]
                             + [np.arange(h * 192 + 128, (h + 1) * 192) for h in range(8)])
SWA_IN_PERM = np.concatenate([SWA_Q_PERM, 1536 + SWA_Q_PERM, np.arange(1024, 1536)])


def _inv(perm):
    inv = np.empty_like(perm)
    inv[perm] = np.arange(perm.size)
    return inv


def _pack(shards):
    parts = [shards[nm].reshape(-1, D_MODEL) for nm in BIG]
    used = sum(p.shape[0] for p in parts)
    assert used == ROWS_USED
    parts.append(jnp.zeros((ROWS_PACK - used, D_MODEL), parts[0].dtype))
    return jnp.concatenate(parts, axis=0)


def _unpack(packed, shapes):
    out, off = {}, 0
    for nm in BIG:
        shp = shapes[nm]
        rows = shp[0] * shp[1] // D_MODEL
        out[nm] = packed[off:off + rows].reshape(shp)
        off += rows
    return out


def _full_weights(gathered, shapes):
    out, off = {}, 0
    for nm in BIG:
        shp = shapes[nm]
        rows = shp[0] * shp[1] // D_MODEL
        blk = gathered[:, off:off + rows].reshape((N_DEV,) + shp)
        if nm in ROW_SHARDED:
            out[nm] = blk.reshape(N_DEV * shp[0], shp[1])
        else:
            out[nm] = jnp.transpose(blk, (1, 0, 2)).reshape(shp[0], N_DEV * shp[1])
        off += rows
    return out


def _grad_pieces(full_grads, shapes):
    parts = []
    for nm in BIG:
        shp = shapes[nm]
        g = full_grads[nm]
        if nm in ROW_SHARDED:
            blk = g.reshape((N_DEV,) + shp)
        else:
            blk = jnp.transpose(g.reshape(shp[0], N_DEV, shp[1]), (1, 0, 2))
        parts.append(blk.reshape(N_DEV, -1, D_MODEL).astype(BF16))
    parts.append(jnp.zeros((N_DEV, ROWS_PACK - ROWS_USED, D_MODEL), BF16))
    return jnp.concatenate(parts, axis=1)


def _pack_small(vals):
    flat = jnp.concatenate([vals[nm].reshape(-1).astype(F32) for nm in SMALL])
    flat = jnp.concatenate([flat, jnp.zeros((SMALL_ROWS * LANES - flat.shape[0],), F32)])
    return flat.reshape(SMALL_ROWS, LANES)


def _unpack_small(packed, shapes):
    flat = packed.reshape(-1)
    out, off = {}, 0
    for nm in SMALL:
        n = shapes[nm][0]
        out[nm] = flat[off:off + n]
        off += n
    return out


def _row(v):
    return v.reshape(1, -1).astype(F32)


def _mla_gain_rows(g):
    return jnp.concatenate([g[:128], g[128:], g[128:]]).reshape(1, 256).astype(F32)


def _layer_fwd_sb(x, norm, w_in, w_out, tag):
    proj, xn = norm_matmul(x, 0, _row(norm), w_in, f"{tag}_in")
    o = sb_attn_fwd(proj, f"{tag}_attn")
    y = gate_out_fwd(o, proj, 3, w_out, x, f"{tag}_out")
    return y, (x, xn, proj, o)


def _layer_bwd_sb(dy, saved, norm, w_in, w_out, tag):
    x, xn, proj, o = saved
    do, dgate, hg = gate_out_bwd(dy, w_out, o, proj, 3, f"{tag}_outb")
    dw_out = tn_matmul(hg, dy, f"{tag}_dwout")
    dq, dk, dv = sb_attn_bwd(proj, o, do, f"{tag}_attnb")
    dproj = jnp.concatenate([dq, dk, dv, dgate], axis=1)
    dw_in = tn_matmul(xn, dproj, f"{tag}_dwin")
    dx, dnorm = nt_norm_bwd(dproj, w_in, x, 0, _row(norm), dy, f"{tag}_inb", F32)
    return dx, dw_in, dw_out, dnorm.reshape(-1)


def kernel(x, l0_norm, l0_w_in, l0_w_out, l1_norm, l1_w_in, l1_q_a_norm, l1_w_uq, l1_kv_a_norm, l1_w_ukv, l1_q_head_norm, l1_k_head_norm, l1_w_out, l2_norm, l2_w_in, l2_q_head_norm, l2_k_head_norm, l2_sinks, l2_w_out, l3_norm, l3_w_in, l3_w_out, loss_target, m_l0_norm, m_l0_w_in, m_l0_w_out, m_l1_norm, m_l1_w_in, m_l1_q_a_norm, m_l1_w_uq, m_l1_kv_a_norm, m_l1_w_ukv, m_l1_q_head_norm, m_l1_k_head_norm, m_l1_w_out, m_l2_norm, m_l2_w_in, m_l2_q_head_norm, m_l2_k_head_norm, m_l2_sinks, m_l2_w_out, m_l3_norm, m_l3_w_in, m_l3_w_out, v_l0_norm, v_l0_w_in, v_l0_w_out, v_l1_norm, v_l1_w_in, v_l1_q_a_norm, v_l1_w_uq, v_l1_kv_a_norm, v_l1_w_ukv, v_l1_q_head_norm, v_l1_k_head_norm, v_l1_w_out, v_l2_norm, v_l2_w_in, v_l2_q_head_norm, v_l2_k_head_norm, v_l2_sinks, v_l2_w_out, v_l3_norm, v_l3_w_in, v_l3_w_out):
    loc = dict(locals())
    names = BIG + SMALL
    w = {nm: loc[nm] for nm in names}
    mom = {nm: loc["m_" + nm] for nm in names}
    vel = {nm: loc["v_" + nm] for nm in names}
    shapes = {nm: w[nm].shape for nm in names}
    xs = x[0]
    tgt = loss_target[0]

    gathered = all_gather_packed(_pack({nm: w[nm].astype(BF16) for nm in BIG}), "ag_weights")
    fw = _full_weights(gathered, shapes)
    loss_local, dx0, fg, sg = _local_step(xs, tgt, fw, {nm: w[nm] for nm in SMALL})
    loss = lax.psum(loss_local, ("x", "y", "c"))
    return _reduce_and_update(loss, dx0, fg, sg, w, mom, vel, shapes)


def _local_step(xs, tgt, fw, sm):
    s = xs.shape[0]
    l0_norm, l1_norm, l2_norm, l3_norm = sm["l0_norm"], sm["l1_norm"], sm["l2_norm"], sm["l3_norm"]
    l1_q_a_norm, l1_kv_a_norm = sm["l1_q_a_norm"], sm["l1_kv_a_norm"]
    l1_q_head_norm, l1_k_head_norm = sm["l1_q_head_norm"], sm["l1_k_head_norm"]
    l2_q_head_norm, l2_k_head_norm, l2_sinks = sm["l2_q_head_norm"], sm["l2_k_head_norm"], sm["l2_sinks"]
    w1_in = jnp.concatenate([fw["l1_w_in"][:, 448:], fw["l1_w_in"][:, :448], jnp.zeros((D_MODEL, 64), BF16)], axis=1)
    w1_uq = fw["l1_w_uq"][:, MLA_UQ_PERM]
    w2_in = fw["l2_w_in"][:, SWA_IN_PERM]
    w2_out = fw["l2_w_out"][SWA_Q_PERM, :]

    x1, sv0 = _layer_fwd_sb(xs, l0_norm, fw["l0_w_in"], fw["l0_w_out"], "l0")

    cos, sin = _rope_tables(s)
    gq, gk = _mla_gain_rows(l1_q_head_norm), _mla_gain_rows(l1_k_head_norm)
    proj1, xn1 = norm_matmul(x1, 0, _row(l1_norm), w1_in, "l1_in")
    qa, qln = norm_matmul(proj1, 4, _row(l1_q_a_norm), w1_uq, "l1_uq")
    kva, kvn = norm_matmul(proj1, 10, _row(l1_kv_a_norm), fw["l1_w_ukv"], "l1_ukv")
    q_p, k_p = mla_prep_fwd(qa, kva, proj1, gq, gk, cos, sin, "l1_prep")
    o1, lse1 = mla_attn_fwd(q_p, k_p, kva, "l1_attn")
    x2 = gate_out_fwd(o1, proj1, 0, fw["l1_w_out"], x1, "l1_out")

    g2 = jnp.stack([jnp.tile(l2_q_head_norm, 2), jnp.tile(l2_k_head_norm, 2)]).astype(F32)
    proj2, xn2 = norm_matmul(x2, 0, _row(l2_norm), w2_in, "l2_in")
    qkn2 = swa_prep_fwd(proj2, g2, "l2_prep")
    o2 = swa_attn_fwd(qkn2, proj2, l2_sinks, "l2_attn")
    x3 = gate_out_fwd(o2, proj2, 1, w2_out, x2, "l2_out")

    x4, sv3 = _layer_fwd_sb(x3, l3_norm, fw["l3_w_in"], fw["l3_w_out"], "l3")

    lossblk, dy = loss_kernel(x4, tgt, "loss")

    fg, sg = {}, {}
    dx3, fg["l3_w_in"], fg["l3_w_out"], sg["l3_norm"] = _layer_bwd_sb(dy, sv3, l3_norm, fw["l3_w_in"], fw["l3_w_out"], "l3")

    do2, dgate2, hg2 = gate_out_bwd(dx3, w2_out, o2, proj2, 1, "l2_outb")
    dw2_out = tn_matmul(hg2, dx3, "l2_dwout")
    dqn2, dkn2, dv2, dsink2 = swa_attn_bwd(qkn2, proj2, l2_sinks, do2, "l2_attnb")
    dqk2, dg2 = swa_prep_bwd(dqn2, dkn2, proj2, g2, "l2_prepb")
    dproj2 = jnp.concatenate([dqk2[:, :1024], dgate2, dqk2[:, 1024:], dv2.astype(BF16)], axis=1)
    dw2_in = tn_matmul(xn2, dproj2, "l2_dwin")
    dx2, dn2 = nt_norm_bwd(dproj2, w2_in, x2, 0, _row(l2_norm), dx3, "l2_inb", F32)
    fg["l2_w_in"] = dw2_in[:, _inv(SWA_IN_PERM)]
    fg["l2_w_out"] = dw2_out[_inv(SWA_Q_PERM), :]
    sg["l2_norm"] = dn2.reshape(-1)
    sg["l2_q_head_norm"] = dg2[0, :HALF]
    sg["l2_k_head_norm"] = dg2[1, :HALF]
    sg["l2_sinks"] = dsink2[0, :SWA_HEADS]

    do1, dgate1, hg1 = gate_out_bwd(dx2, fw["l1_w_out"], o1, proj1, 0, "l1_outb")
    fg["l1_w_out"] = tn_matmul(hg1, dx2, "l1_dwout")
    dq_p, dk_p, dv1 = mla_attn_bwd(q_p, k_p, kva, o1, do1, lse1, "l1_attnb")
    dqa, dkva, dkpe, dgq, dgk = mla_prep_bwd(dq_p, dk_p, dv1, qa, kva, proj1, gq, gk, cos, sin, "l1_prepb")
    dw_uq = tn_matmul(qln, dqa, "l1_dwuq")
    fg["l1_w_ukv"] = tn_matmul(kvn, dkva, "l1_dwukv")
    dqlat, dgqa = nt_norm_bwd(dqa, w1_uq, proj1, 4, _row(l1_q_a_norm), None, "l1_uqb", BF16)
    dkvlat, dgkva = nt_norm_bwd(dkva, fw["l1_w_ukv"], proj1, 10, _row(l1_kv_a_norm), None, "l1_ukvb", BF16)
    dproj1 = jnp.concatenate([dgate1, dqlat, dkvlat, dkpe], axis=1)
    dw1_in = tn_matmul(xn1, dproj1, "l1_dwin")
    dx1, dn1 = nt_norm_bwd(dproj1, w1_in, x1, 0, _row(l1_norm), dx2, "l1_inb", F32)
    fg["l1_w_in"] = jnp.concatenate([dw1_in[:, 1024:1472], dw1_in[:, :1024]], axis=1)
    fg["l1_w_uq"] = dw_uq[:, _inv(MLA_UQ_PERM)]
    sg["l1_norm"] = dn1.reshape(-1)
    sg["l1_q_a_norm"] = dgqa.reshape(-1)
    sg["l1_kv_a_norm"] = dgkva.reshape(-1)
    sg["l1_q_head_norm"] = jnp.concatenate([dgq[0, :128], dgq[0, 128:192] + dgq[0, 192:256]])
    sg["l1_k_head_norm"] = jnp.concatenate([dgk[0, :128], dgk[0, 128:192] + dgk[0, 192:256]])

    dx0, fg["l0_w_in"], fg["l0_w_out"], sg["l0_norm"] = _layer_bwd_sb(dx1, sv0, l0_norm, fw["l0_w_in"], fw["l0_w_out"], "l0")
    return lossblk[0, 0], dx0, fg, sg


def _reduce_and_update(loss, dx0, fg, sg, w, mom, vel, shapes):
    xi, yi, ci = _my_pos()
    cidx = jnp.reshape(ci, (1,)).astype(jnp.int32)
    chipidx = jnp.reshape(2 * xi + yi, (1,)).astype(jnp.int32)
    pieces = _grad_pieces(fg, shapes)
    ra = rs_sibling_exchange(pieces, "rs_sibling")
    part = rs_pair_add(pieces, ra, cidx, "rs_pair_add")
    rb = rs_cross_exchange(part, "rs_cross")
    gb, db, mb, vb = rs_final_adamw(part, rb, chipidx, _pack({nm: w[nm] for nm in BIG}),
                                    _pack({nm: mom[nm] for nm in BIG}), _pack({nm: vel[nm] for nm in BIG}), "rs_adamw")
    gs, ds, ms, vs = small_allreduce_adamw(_pack_small(sg), _pack_small({nm: w[nm] for nm in SMALL}),
                                           _pack_small({nm: mom[nm] for nm in SMALL}),
                                           _pack_small({nm: vel[nm] for nm in SMALL}), "small_allreduce")

    outs = []
    for big, small in ((gb, gs), (db, ds), (mb, ms), (vb, vs)):
        d = _unpack(big, shapes)
        d.update(_unpack_small(small, shapes))
        outs.append(d)
    order = ["l0_norm", "l0_w_in", "l0_w_out", "l1_norm", "l1_w_in", "l1_q_a_norm", "l1_w_uq", "l1_kv_a_norm",
             "l1_w_ukv", "l1_q_head_norm", "l1_k_head_norm", "l1_w_out", "l2_norm", "l2_w_in", "l2_q_head_norm",
             "l2_k_head_norm", "l2_sinks", "l2_w_out", "l3_norm", "l3_w_in", "l3_w_out"]
    flat = [loss, dx0[None]]
    for d in outs:
        flat += [d[nm] for nm in order]
    return tuple(flat)
```

```python
import math

import numpy as np
import jax
import jax.numpy as jnp
from jax import lax
from jax.experimental import pallas as pl
from jax.experimental.pallas import tpu as pltpu

F32 = jnp.float32
BF16 = jnp.bfloat16
SDS = jax.ShapeDtypeStruct
MESH_ID = pl.DeviceIdType.MESH

D_MODEL = 1024
NORM_EPS = 1e-6
N_DEV = 8
LANES = 128
HALF = 64
ROWS_PACK = 2176
ROWS_USED = 2120
SMALL_ROWS = 40

ADAM_LR = 0.001
ADAM_B1 = 0.9
ADAM_B2 = 0.999
ADAM_EPS = 1e-08
ADAM_WD = 0.01
ADAM_STEP = 10

NT_DIMS = (((1,), (1,)), ((), ()))
TN_DIMS = (((0,), (0,)), ((), ()))


def _cparams(sem=None, vmem_mb=48):
    return pltpu.CompilerParams(dimension_semantics=sem, vmem_limit_bytes=vmem_mb * 2 ** 20)


def _dot(a, b):
    return jnp.dot(a, b, preferred_element_type=F32)


def _dot_nt(a, b):
    return lax.dot_general(a, b, NT_DIMS, preferred_element_type=F32)


def _dot_tn(a, b):
    return lax.dot_general(a, b, TN_DIMS, preferred_element_type=F32)


def _split_dot(a, b):
    hi = a.astype(BF16)
    lo = (a - hi.astype(F32)).astype(BF16)
    return _dot(hi, b) + _dot(lo, b)


def _cumsum_dot(lf, u):
    return _split_dot(lf, u)


def _sigmoid(x):
    return 1.0 / (1.0 + jnp.exp(-x))


def norm_matmul(x, colblk, g, w, name):
    s = x.shape[0]
    k, n = w.shape
    tm = min(512, s)
    tn = min(512, n)
    assert s % tm == 0 and n % tn == 0

    def body(x_ref, g_ref, w_ref, o_ref, xn_ref, xn_s):
        @pl.when(pl.program_id(1) == 0)
        def _():
            xf = x_ref[...].astype(F32)
            ms = jnp.mean(xf * xf, axis=-1, keepdims=True)
            y = (xf * lax.rsqrt(ms + NORM_EPS) * g_ref[...]).astype(BF16)
            xn_s[...] = y
            xn_ref[...] = y

        o_ref[...] = _dot(xn_s[...], w_ref[...]).astype(o_ref.dtype)

    return pl.pallas_call(
        body, name=name, grid=(s // tm, n // tn),
        in_specs=[pl.BlockSpec((tm, k), lambda i, j: (i, colblk)),
                  pl.BlockSpec((1, k), lambda i, j: (0, 0)),
                  pl.BlockSpec((k, tn), lambda i, j: (0, j))],
        out_specs=[pl.BlockSpec((tm, tn), lambda i, j: (i, j)),
                   pl.BlockSpec((tm, k), lambda i, j: (i, 0))],
        out_shape=[SDS((s, n), BF16), SDS((s, k), BF16)],
        scratch_shapes=[pltpu.VMEM((tm, k), BF16)],
        compiler_params=_cparams(("parallel", "arbitrary")),
    )(x, g, w)


def gate_out_fwd(o, proj, gate_blk, w_out, x, name):
    s = x.shape[0]
    d = D_MODEL
    tm = min(512, s)

    def body(o_ref, g_ref, w_ref, x_ref, y_ref):
        g = g_ref[...].astype(F32)
        hg = (o_ref[...].astype(F32) * (g * _sigmoid(g))).astype(BF16)
        y_ref[...] = x_ref[...] + _dot(hg, w_ref[...])

    return pl.pallas_call(
        body, name=name, grid=(s // tm,),
        in_specs=[pl.BlockSpec((tm, d), lambda i: (i, 0)),
                  pl.BlockSpec((tm, d), lambda i: (i, gate_blk)),
                  pl.BlockSpec((d, d), lambda i: (0, 0)),
                  pl.BlockSpec((tm, d), lambda i: (i, 0))],
        out_specs=pl.BlockSpec((tm, d), lambda i: (i, 0)),
        out_shape=SDS((s, d), F32),
        compiler_params=_cparams(("parallel",)),
    )(o, proj, w_out, x)


def gate_out_bwd(dy, w_out, o, proj, gate_blk, name):
    s = dy.shape[0]
    d = D_MODEL
    tm = min(512, s)

    def body(dy_ref, w_ref, o_ref, g_ref, do_ref, dg_ref, hg_ref):
        dh = _dot_nt(dy_ref[...].astype(BF16), w_ref[...])
        g = g_ref[...].astype(F32)
        o = o_ref[...].astype(F32)
        sg = _sigmoid(g)
        silu = g * sg
        do_ref[...] = (dh * silu).astype(BF16)
        dg_ref[...] = (dh * o * (sg * (1.0 + g * (1.0 - sg)))).astype(BF16)
        hg_ref[...] = (o * silu).astype(BF16)

    blk = pl.BlockSpec((tm, d), lambda i: (i, 0))
    return pl.pallas_call(
        body, name=name, grid=(s // tm,),
        in_specs=[blk, pl.BlockSpec((d, d), lambda i: (0, 0)), blk,
                  pl.BlockSpec((tm, d), lambda i: (i, gate_blk))],
        out_specs=[blk, blk, blk],
        out_shape=[SDS((s, d), BF16)] * 3,
        compiler_params=_cparams(("parallel",)),
    )(dy, w_out, o, proj)


def tn_matmul(a, b, name):
    s, k = a.shape
    n = b.shape[1]
    tk = min(512, k)
    tn = min(512, n)
    ts = min(1024, s)
    assert k % tk == 0 and n % tn == 0 and s % ts == 0

    def body(a_ref, b_ref, o_ref):
        @pl.when(pl.program_id(2) == 0)
        def _():
            o_ref[...] = jnp.zeros_like(o_ref)

        o_ref[...] += _dot_tn(a_ref[...].astype(BF16), b_ref[...].astype(BF16))

    return pl.pallas_call(
        body, name=name, grid=(k // tk, n // tn, s // ts),
        in_specs=[pl.BlockSpec((ts, tk), lambda i, j, r: (r, i)),
                  pl.BlockSpec((ts, tn), lambda i, j, r: (r, j))],
        out_specs=pl.BlockSpec((tk, tn), lambda i, j, r: (i, j)),
        out_shape=SDS((k, n), F32),
        compiler_params=_cparams(("parallel", "parallel", "arbitrary")),
    )(a, b)


def nt_norm_bwd(dp, w, x, xcol, g, res, name, out_dtype):
    s, n = dp.shape
    k = w.shape[0]
    tm = min(512, s)
    tk = min(512, n)
    assert s % tm == 0 and n % tk == 0
    nk = n // tk
    has_res = res is not None

    def body(*refs):
        if has_res:
            dp_ref, w_ref, x_ref, g_ref, r_ref, dx_ref, dg_ref, acc = refs
        else:
            dp_ref, w_ref, x_ref, g_ref, dx_ref, dg_ref, acc = refs
        i, j = pl.program_id(0), pl.program_id(1)

        @pl.when(j == 0)
        def _():
            acc[...] = jnp.zeros_like(acc)

        @pl.when((i == 0) & (j == 0))
        def _():
            dg_ref[...] = jnp.zeros_like(dg_ref)

        acc[...] += _dot_nt(dp_ref[...], w_ref[...])

        @pl.when(j == nk - 1)
        def _():
            xf = x_ref[...].astype(F32)
            rstd = lax.rsqrt(jnp.mean(xf * xf, axis=-1, keepdims=True) + NORM_EPS)
            xhat = xf * rstd
            dxn = acc[...]
            dg_ref[...] += jnp.sum(dxn * xhat, axis=0, keepdims=True)
            dxh = dxn * g_ref[...]
            dx = rstd * (dxh - xhat * jnp.mean(dxh * xhat, axis=-1, keepdims=True))
            if has_res:
                dx = dx + r_ref[...]
            dx_ref[...] = dx.astype(out_dtype)

    in_specs = [pl.BlockSpec((tm, tk), lambda i, j: (i, j)),
                pl.BlockSpec((k, tk), lambda i, j: (0, j)),
                pl.BlockSpec((tm, k), lambda i, j: (i, xcol)),
                pl.BlockSpec((1, k), lambda i, j: (0, 0))]
    args = [dp, w, x, g]
    if has_res:
        in_specs.append(pl.BlockSpec((tm, k), lambda i, j: (i, 0)))
        args.append(res)
    return pl.pallas_call(
        body, name=name, grid=(s // tm, nk),
        in_specs=in_specs,
        out_specs=[pl.BlockSpec((tm, k), lambda i, j: (i, 0)),
                   pl.BlockSpec((1, k), lambda i, j: (0, 0))],
        out_shape=[SDS((s, k), out_dtype), SDS((1, k), F32)],
        scratch_shapes=[pltpu.VMEM((tm, k), F32)],
        compiler_params=_cparams(("arbitrary", "arbitrary")),
    )(*args)


def loss_kernel(y, target, name):
    s, d = y.shape
    tm = min(512, s)

    def body(y_ref, t_ref, l_ref, dy_ref):
        @pl.when(pl.program_id(0) == 0)
        def _():
            l_ref[...] = jnp.zeros_like(l_ref)

        e = y_ref[...] - t_ref[...]
        dy_ref[...] = e * (1.0 / d)
        l_ref[...] += 0.5 * jnp.sum(jnp.mean(e * e, axis=-1, keepdims=True))

    blk = pl.BlockSpec((tm, d), lambda i: (i, 0))
    return pl.pallas_call(
        body, name=name, grid=(s // tm,),
        in_specs=[blk, blk],
        out_specs=[pl.BlockSpec((8, LANES), lambda i: (0, 0)), blk],
        out_shape=[SDS((8, LANES), F32), SDS((s, d), F32)],
        compiler_params=_cparams(("arbitrary",)),
    )(y, target)


SB_LOG_CUTOFF = -80.0


def _sb_sweep(qi, c, rest, step):
    def cond(st):
        return (st[0] <= qi) & (st[1] > SB_LOG_CUTOFF)

    def body(st):
        out = tuple(step(st[0], st[2], st[3:]))
        return (st[0] + 1, jnp.max(out[0])) + out

    return lax.while_loop(cond, body, (jnp.int32(1), jnp.max(c), c) + tuple(rest))[3:]


def _tile_masks(t):
    row = lax.broadcasted_iota(jnp.int32, (t, t), 0)
    col = lax.broadcasted_iota(jnp.int32, (t, t), 1)
    return row, col


def sb_attn_fwd(proj, name):
    s = proj.shape[0]
    t = min(256, s)
    nq = s // t
    scale = 1.0 / math.sqrt(HALF)

    def body(q_ref, k_ref, v_ref, o_ref):
        lane = lax.broadcasted_iota(jnp.int32, (1, LANES), 1)
        row, col = _tile_masks(t)
        u_incl = (row >= col).astype(BF16)
        tri = jnp.concatenate([col < row] * 2, axis=0)

        def scores(qs, kj):
            k = k_ref[pl.ds(pl.multiple_of(kj * t, t), t), :]
            z = _dot_nt(qs, k)
            nz = -z
            lf = jnp.minimum(nz, 0.0) - jnp.log(1.0 + jnp.exp(jnp.minimum(z, nz)))
            return z, lf

        def accumulate(kj, z, lf, c, acc, diag):
            v = v_ref[pl.ds(pl.multiple_of(kj * t, t), t), :]
            if diag:
                lf = jnp.where(tri, lf, 0.0)
            incl = _cumsum_dot(lf, u_incl) + c
            a = jnp.exp(z + incl)
            if diag:
                a = jnp.where(tri, a, 0.0)
            acc = acc + _dot(a.astype(BF16), v)
            c = c + jnp.sum(lf, axis=1, keepdims=True)
            return c, acc

        def q_body(qi, carry):
            q0 = pl.multiple_of(qi * t, t)
            q = q_ref[pl.ds(q0, t), :] * scale
            zq = jnp.zeros_like(q)
            qs = jnp.concatenate([jnp.where(lane < HALF, q, zq), jnp.where(lane >= HALF, q, zq)], axis=0)

            def step(jj, cr, diag=False):
                z, lf = scores(qs, qi - jj)
                return accumulate(qi - jj, z, lf, cr[0], cr[1], diag)

            c, acc = step(0, (jnp.zeros((2 * t, 1), F32), jnp.zeros((2 * t, LANES), F32)), True)
            acc = _sb_sweep(qi, c, (acc,), lambda jj, c, rest: step(jj, (c, rest[0])))[0]
            o_ref[pl.ds(q0, t), :] = jnp.where(lane < HALF, acc[:t], acc[t:])
            return carry

        lax.fori_loop(0, nq, q_body, 0)

    return pl.pallas_call(
        body, name=name, grid=(8,),
        in_specs=[pl.BlockSpec((s, LANES), lambda p: (0, p)),
                  pl.BlockSpec((s, LANES), lambda p: (0, 8 + p)),
                  pl.BlockSpec((s, LANES), lambda p: (0, 16 + p))],
        out_specs=pl.BlockSpec((s, LANES), lambda p: (0, p)),
        out_shape=SDS((s, D_MODEL), F32),
        compiler_params=_cparams(("parallel",)),
    )(proj, proj, proj)


def sb_attn_bwd(proj, o, do, name):
    s = proj.shape[0]
    t = min(256, s)
    nq = s // t
    scale = 1.0 / math.sqrt(HALF)

    def body(q_ref, k_ref, v_ref, o_ref, do_ref, dq_ref, dk_ref, dv_ref, dk_s, dv_s):
        dk_s[...] = jnp.zeros_like(dk_s)
        dv_s[...] = jnp.zeros_like(dv_s)
        lane = lax.broadcasted_iota(jnp.int32, (1, LANES), 1)
        row, col = _tile_masks(t)
        u_incl = (row >= col).astype(BF16)
        u_excl = (row > col).astype(BF16)
        tri = jnp.concatenate([col < row] * 2, axis=0)

        def scores(qs, dos, kj):
            k0 = pl.multiple_of(kj * t, t)
            z = _dot_nt(qs, k_ref[pl.ds(k0, t), :])
            nz = -z
            lf = jnp.minimum(nz, 0.0) - jnp.log(1.0 + jnp.exp(jnp.minimum(z, nz)))
            return z, lf, _dot_nt(dos, v_ref[pl.ds(k0, t), :])

        def accumulate(qs, dos, tsum, kj, z, lf, da, c, r, dq, diag):
            k0 = pl.multiple_of(kj * t, t)
            sig = jnp.exp(z + lf)
            if diag:
                lf = jnp.where(tri, lf, 0.0)
            incl = _cumsum_dot(lf, u_incl) + c
            a = jnp.exp(z + incl)
            if diag:
                a = jnp.where(tri, a, 0.0)
            ab = a.astype(BF16)
            dl = ab.astype(F32) * da
            pre = tsum - r - _split_dot(dl, u_excl)
            dz = dl - sig * pre
            if diag:
                dz = jnp.where(tri, dz, 0.0)
            dzb = dz.astype(BF16)
            dq = dq + _dot(dzb, k_ref[pl.ds(k0, t), :])
            dk_s[pl.ds(k0, t), :] += _dot_tn(dzb, qs)
            dv_s[pl.ds(k0, t), :] += _dot_tn(ab, dos)
            c = c + jnp.sum(lf, axis=1, keepdims=True)
            r = r + jnp.sum(dl, axis=1, keepdims=True)
            return c, r, dq

        def q_body(qi, carry):
            q0 = pl.multiple_of(qi * t, t)
            q = q_ref[pl.ds(q0, t), :] * scale
            dob = do_ref[pl.ds(q0, t), :]
            prod = dob.astype(F32) * o_ref[pl.ds(q0, t), :]
            zb = jnp.zeros_like(q)
            lo, hi = lane < HALF, lane >= HALF
            qs = jnp.concatenate([jnp.where(lo, q, zb), jnp.where(hi, q, zb)], axis=0)
            dos = jnp.concatenate([jnp.where(lo, dob, zb), jnp.where(hi, dob, zb)], axis=0)
            tsum = jnp.concatenate([jnp.sum(jnp.where(lo, prod, 0.0), axis=1, keepdims=True),
                                    jnp.sum(jnp.where(hi, prod, 0.0), axis=1, keepdims=True)], axis=0)
            zc = jnp.zeros((2 * t, 1), F32)

            def step(jj, cr, diag=False):
                z, lf, da = scores(qs, dos, qi - jj)
                return accumulate(qs, dos, tsum, qi - jj, z, lf, da, cr[0], cr[1], cr[2], diag)

            c, r, dq = step(0, (zc, zc, jnp.zeros((2 * t, LANES), F32)), True)
            dq = _sb_sweep(qi, c, (r, dq), lambda jj, c, rest: step(jj, (c,) + tuple(rest)))[1]
            dq_ref[pl.ds(q0, t), :] = (jnp.where(lo, dq[:t], dq[t:]) * scale).astype(BF16)
            return carry

        lax.fori_loop(0, nq, q_body, 0)
        dk_ref[...] = dk_s[...].astype(BF16)
        dv_ref[...] = dv_s[...].astype(BF16)

    blk = lambda off: pl.BlockSpec((s, LANES), lambda p: (0, off + p))
    return pl.pallas_call(
        body, name=name, grid=(8,),
        in_specs=[blk(0), blk(8), blk(16), blk(0), blk(0)],
        out_specs=[blk(0), blk(0), blk(0)],
        out_shape=[SDS((s, D_MODEL), BF16)] * 3,
        scratch_shapes=[pltpu.VMEM((s, LANES), F32), pltpu.VMEM((s, LANES), F32)],
        compiler_params=_cparams(("parallel",)),
    )(proj, proj, proj, o, do)


MLA_HEADS = 8
MLA_QK = 192
MLA_SCALE = 1.0 / math.sqrt(MLA_QK)


def _rope_tables(s):
    inv_freq = 10000.0 ** (-jnp.arange(32, dtype=F32) / 32)
    ang = jnp.arange(s, dtype=F32)[:, None] * inv_freq[None, :]
    cos = jnp.tile(jnp.cos(ang), (1, 4))
    sin = jnp.sin(ang)
    sin_signed = jnp.tile(jnp.concatenate([-sin, sin], axis=1), (1, 2))
    return cos, sin_signed


def _partner(u, lane):
    return jnp.where((lane % HALF) < 32, pltpu.roll(u, 96, 1), pltpu.roll(u, 32, 1))


def _mla_head_inputs(h, qa_ref, kva_ref, kpe, lane):
    a = h % 2
    hm = (lane < HALF) if a == 0 else (lane >= HALF)
    qn = qa_ref[:, h * 128:(h + 1) * 128].astype(F32)
    qr = jnp.where(hm, qa_ref[:, 1024 + (h // 2) * 128:1024 + (h // 2 + 1) * 128].astype(F32), 0.0)
    kn = kva_ref[:, h * 256:h * 256 + 128].astype(F32)
    kr = kpe if a == 0 else pltpu.roll(kpe, HALF, 1)
    return qn, qr, kn, kr


def mla_prep_fwd(qa, kva, proj, gq, gk, cos, sin, name):
    s = qa.shape[0]
    tm = min(256, s)

    def body(qa_ref, kva_ref, kpe_ref, gq_ref, gk_ref, cos_ref, sin_ref, q_out, k_out):
        lane = lax.broadcasted_iota(jnp.int32, (1, LANES), 1)
        kpe = kpe_ref[...].astype(F32)
        cs, sn = cos_ref[...], sin_ref[...]
        for h in range(MLA_HEADS):
            qn, qr, kn, kr = _mla_head_inputs(h, qa_ref, kva_ref, kpe, lane)
            for xn, xr, g_ref, out, sc in ((qn, qr, gq_ref, q_out, MLA_SCALE), (kn, kr, gk_ref, k_out, 1.0)):
                ss = jnp.sum(xn * xn, axis=1, keepdims=True) + jnp.sum(xr * xr, axis=1, keepdims=True)
                rstd = lax.rsqrt(ss * (1.0 / MLA_QK) + NORM_EPS) * sc
                yn = xn * rstd * g_ref[:, 0:128]
                ur = xr * rstd * g_ref[:, 128:256]
                yr = ur * cs + _partner(ur, lane) * sn
                out[:, h * 256:h * 256 + 128] = yn.astype(BF16)
                out[:, h * 256 + 128:h * 256 + 256] = yr.astype(BF16)

    row = lambda w, c: pl.BlockSpec((tm, w), lambda i: (i, c))
    full = lambda w: pl.BlockSpec((1, w), lambda i: (0, 0))
    return pl.pallas_call(
        body, name=name, grid=(s // tm,),
        in_specs=[row(1536, 0), row(2048, 0), row(128, 11), full(256), full(256), row(128, 0), row(128, 0)],
        out_specs=[row(2048, 0), row(2048, 0)],
        out_shape=[SDS((s, 2048), BF16)] * 2,
        compiler_params=_cparams(("parallel",)),
    )(qa, kva, proj, gq, gk, cos, sin)


def mla_prep_bwd(dq_p, dk_p, dv, qa, kva, proj, gq, gk, cos, sin, name):
    s = qa.shape[0]
    tm = min(256, s)

    def body(dq_ref, dk_ref, dv_ref, qa_ref, kva_ref, kpe_ref, gq_ref, gk_ref, cos_ref, sin_ref,
             dqa_ref, dkva_ref, dkpe_ref, dgq_ref, dgk_ref):
        @pl.when(pl.program_id(0) == 0)
        def _():
            dgq_ref[...] = jnp.zeros_like(dgq_ref)
            dgk_ref[...] = jnp.zeros_like(dgk_ref)

        lane = lax.broadcasted_iota(jnp.int32, (1, LANES), 1)
        kpe = kpe_ref[...].astype(F32)
        cs, sn = cos_ref[...], sin_ref[...]
        dkpe = jnp.zeros((tm, LANES), F32)
        dqr_pair = None
        for h in range(MLA_HEADS):
            qn, qr, kn, kr = _mla_head_inputs(h, qa_ref, kva_ref, kpe, lane)
            res = []
            for xn, xr, g_ref, d_ref, dg_ref, sc in ((qn, qr, gq_ref, dq_ref, dgq_ref, MLA_SCALE),
                                                     (kn, kr, gk_ref, dk_ref, dgk_ref, 1.0)):
                ss = jnp.sum(xn * xn, axis=1, keepdims=True) + jnp.sum(xr * xr, axis=1, keepdims=True)
                rstd = lax.rsqrt(ss * (1.0 / MLA_QK) + NORM_EPS)
                hn, hr = xn * rstd, xr * rstd
                dyn = d_ref[:, h * 256:h * 256 + 128] * sc
                dyr = d_ref[:, h * 256 + 128:h * 256 + 256] * sc
                dur = dyr * cs - _partner(dyr, lane) * sn
                dg_ref[:, 0:128] += jnp.sum(dyn * hn, axis=0, keepdims=True)
                dg_ref[:, 128:256] += jnp.sum(dur * hr, axis=0, keepdims=True)
                dhn = dyn * g_ref[:, 0:128]
                dhr = dur * g_ref[:, 128:256]
                mu = (jnp.sum(dhn * hn, axis=1, keepdims=True)
                      + jnp.sum(dhr * hr, axis=1, keepdims=True)) * (1.0 / MLA_QK)
                res.append((rstd * (dhn - hn * mu), rstd * (dhr - hr * mu)))
            (dqn, dqr), (dkn, dkr) = res
            dqa_ref[:, h * 128:(h + 1) * 128] = dqn.astype(BF16)
            if h % 2 == 0:
                dqr_pair = dqr
            else:
                dqa_ref[:, 1024 + (h // 2) * 128:1024 + (h // 2 + 1) * 128] = (dqr_pair + dqr).astype(BF16)
            dkva_ref[:, h * 256:h * 256 + 128] = dkn.astype(BF16)
            dkva_ref[:, h * 256 + 128:h * 256 + 256] = dv_ref[:, h * 128:(h + 1) * 128]
            dkpe = dkpe + (dkr if h % 2 == 0 else pltpu.roll(dkr, HALF, 1))
        dkpe_ref[...] = dkpe.astype(BF16)

    row = lambda w, c: pl.BlockSpec((tm, w), lambda i: (i, c))
    full = lambda w: pl.BlockSpec((1, w), lambda i: (0, 0))
    return pl.pallas_call(
        body, name=name, grid=(s // tm,),
        in_specs=[row(2048, 0), row(2048, 0), row(1024, 0), row(1536, 0), row(2048, 0), row(128, 11),
                  full(256), full(256), row(128, 0), row(128, 0)],
        out_specs=[row(1536, 0), row(2048, 0), row(128, 0), full(256), full(256)],
        out_shape=[SDS((s, 1536), BF16), SDS((s, 2048), BF16), SDS((s, 128), BF16),
                   SDS((1, 256), F32), SDS((1, 256), F32)],
        compiler_params=_cparams(("arbitrary",)),
    )(dq_p, dk_p, dv, qa, kva, proj, gq, gk, cos, sin)


def mla_attn_fwd(q_p, k_p, kva, name):
    s = q_p.shape[0]
    tk = min(256, s)
    tq = min(512, s)
    r = tq // tk
    nq = s // tq

    def body(q_ref, k_ref, v_ref, o_ref, lse_ref):
        row = lax.broadcasted_iota(jnp.int32, (tq, tk), 0)
        col = lax.broadcasted_iota(jnp.int32, (tq, tk), 1)

        def tile(q, kj, m, l, acc, mask):
            k0 = pl.multiple_of(kj * tk, tk)
            sc = _dot_nt(q, k_ref[pl.ds(k0, tk), :])
            if mask is not None:
                sc = jnp.where(mask, sc, -1e30)
            m_new = jnp.maximum(m, jnp.max(sc, axis=1, keepdims=True))
            alpha = jnp.exp(m - m_new)
            p = jnp.exp(sc - m_new)
            l = alpha * l + jnp.sum(p, axis=1, keepdims=True)
            acc = alpha * acc + _dot(p.astype(BF16), v_ref[pl.ds(k0, tk), :])
            return m_new, l, acc

        def q_body(qi, carry):
            q0 = pl.multiple_of(qi * tq, tq)
            q = q_ref[pl.ds(q0, tq), :]
            cr = (jnp.full((tq, 1), -1e30, F32), jnp.zeros((tq, 1), F32), jnp.zeros((tq, LANES), F32))
            cr = lax.fori_loop(0, qi * r, lambda kj, cr: tile(q, kj, cr[0], cr[1], cr[2], None), cr)
            for d in range(r):
                cr = tile(q, qi * r + d, cr[0], cr[1], cr[2], col + d * tk <= row)
            m, l, acc = cr
            o_ref[pl.ds(q0, tq), :] = acc / l
            lse_ref[pl.ds(q0, tq), :] = jnp.broadcast_to(m + jnp.log(l), (tq, LANES))
            return carry

        lax.fori_loop(0, nq, q_body, 0)

    return pl.pallas_call(
        body, name=name, grid=(MLA_HEADS,),
        in_specs=[pl.BlockSpec((s, 256), lambda h: (0, h)),
                  pl.BlockSpec((s, 256), lambda h: (0, h)),
                  pl.BlockSpec((s, 128), lambda h: (0, 2 * h + 1))],
        out_specs=[pl.BlockSpec((s, 128), lambda h: (0, h)),
                   pl.BlockSpec((s, 128), lambda h: (0, h))],
        out_shape=[SDS((s, 1024), F32), SDS((s, 1024), F32)],
        compiler_params=_cparams(("parallel",)),
    )(q_p, k_p, kva)


def mla_attn_bwd(q_p, k_p, kva, o, do, lse, name):
    s = q_p.shape[0]
    tk = min(256, s)
    tq = min(512, s)
    r = tq // tk
    nq = s // tq

    def body(q_ref, k_ref, v_ref, o_ref, do_ref, lse_ref, dq_ref, dk_ref, dv_ref, dv_s):
        dk_ref[...] = jnp.zeros_like(dk_ref)
        dv_s[...] = jnp.zeros_like(dv_s)
        row = lax.broadcasted_iota(jnp.int32, (tq, tk), 0)
        col = lax.broadcasted_iota(jnp.int32, (tq, tk), 1)

        def tile(q, dob, lse, delta, kj, dq, mask):
            k0 = pl.multiple_of(kj * tk, tk)
            k = k_ref[pl.ds(k0, tk), :]
            v = v_ref[pl.ds(k0, tk), :]
            p = jnp.exp(_dot_nt(q, k) - lse)
            if mask is not None:
                p = jnp.where(mask, p, 0.0)
            ds = (p * (_dot_nt(dob, v) - delta)).astype(BF16)
            dq = dq + _dot(ds, k)
            dk_ref[pl.ds(k0, tk), :] += _dot_tn(ds, q)
            dv_s[pl.ds(k0, tk), :] += _dot_tn(p.astype(BF16), dob)
            return dq

        def q_body(qi, carry):
            q0 = pl.multiple_of(qi * tq, tq)
            q = q_ref[pl.ds(q0, tq), :]
            dob = do_ref[pl.ds(q0, tq), :]
            lse = lse_ref[pl.ds(q0, tq), 0:1]
            delta = jnp.sum(dob.astype(F32) * o_ref[pl.ds(q0, tq), :], axis=1, keepdims=True)
            dq = lax.fori_loop(0, qi * r, lambda kj, dq: tile(q, dob, lse, delta, kj, dq, None),
                               jnp.zeros((tq, 256), F32))
            for d in range(r):
                dq = tile(q, dob, lse, delta, qi * r + d, dq, col + d * tk <= row)
            dq_ref[pl.ds(q0, tq), :] = dq
            return carry

        lax.fori_loop(0, nq, q_body, 0)
        dv_ref[...] = dv_s[...].astype(BF16)

    b256 = pl.BlockSpec((s, 256), lambda h: (0, h))
    b128 = pl.BlockSpec((s, 128), lambda h: (0, h))
    return pl.pallas_call(
        body, name=name, grid=(MLA_HEADS,),
        in_specs=[b256, b256, pl.BlockSpec((s, 128), lambda h: (0, 2 * h + 1)), b128, b128, b128],
        out_specs=[b256, b256, b128],
        out_shape=[SDS((s, 2048), F32), SDS((s, 2048), F32), SDS((s, 1024), BF16)],
        scratch_shapes=[pltpu.VMEM((s, 128), F32)],
        compiler_params=_cparams(("parallel",), vmem_mb=56),
    )(q_p, k_p, kva, o, do, lse)


SWA_HEADS = 16
SWA_BLOCK = 128


def _swa_head(tile_idx, half):
    return (2 * (tile_idx // 4) + half) * 4 + tile_idx % 4


SWA_Q_PERM = np.concatenate([np.arange(_swa_head(tt, a) * HALF, (_swa_head(tt, a) + 1) * HALF)
                             for tt in range(8) for a in range(2)])
SWA_SLOPES = [2.0 ** (-8.0 * (h + 1) / SWA_HEADS) for h in range(SWA_HEADS)]


def swa_prep_fwd(proj, g128, name):
    s = proj.shape[0]
    tm = min(512, s)

    def body(q_ref, k_ref, g_ref, o_ref):
        lane = lax.broadcasted_iota(jnp.int32, (1, LANES), 1)
        lo = lane < HALF
        for tt in range(10):
            if tt < 8:
                x = q_ref[:, tt * 128:(tt + 1) * 128].astype(F32)
            else:
                x = k_ref[:, (tt - 8) * 128:(tt - 7) * 128].astype(F32)
            sq = x * x
            sa = jnp.sum(jnp.where(lo, sq, 0.0), axis=1, keepdims=True)
            sb = jnp.sum(sq, axis=1, keepdims=True) - sa
            rstd = jnp.where(lo, lax.rsqrt(sa * (1.0 / HALF) + NORM_EPS), lax.rsqrt(sb * (1.0 / HALF) + NORM_EPS))
            gi = 0 if tt < 8 else 1
            o_ref[:, tt * 128:(tt + 1) * 128] = (x * rstd * g_ref[gi:gi + 1, :]).astype(BF16)

    return pl.pallas_call(
        body, name=name, grid=(s // tm,),
        in_specs=[pl.BlockSpec((tm, 1024), lambda i: (i, 0)), pl.BlockSpec((tm, 256), lambda i: (i, 8)),
                  pl.BlockSpec((2, 128), lambda i: (0, 0))],
        out_specs=pl.BlockSpec((tm, 1280), lambda i: (i, 0)),
        out_shape=SDS((s, 1280), BF16),
        compiler_params=_cparams(("parallel",)),
    )(proj, proj, g128)


def swa_prep_bwd(dqn, dkn, proj, g128, name):
    s = proj.shape[0]
    tm = min(512, s)
    nsteps = s // tm

    def body(dq_ref, dk_ref, xq_ref, xk_ref, g_ref, o_ref, dg_ref):
        @pl.when(pl.program_id(0) == 0)
        def _():
            dg_ref[...] = jnp.zeros_like(dg_ref)

        lane = lax.broadcasted_iota(jnp.int32, (1, LANES), 1)
        lo = lane < HALF
        for tt in range(10):
            if tt < 8:
                x = xq_ref[:, tt * 128:(tt + 1) * 128].astype(F32)
                dy = dq_ref[:, tt * 128:(tt + 1) * 128].astype(F32)
            else:
                x = xk_ref[:, (tt - 8) * 128:(tt - 7) * 128].astype(F32)
                dy = dk_ref[:, (tt - 8) * 128:(tt - 7) * 128].astype(F32)
            gi = 0 if tt < 8 else 1
            sq = x * x
            sa = jnp.sum(jnp.where(lo, sq, 0.0), axis=1, keepdims=True)
            sb = jnp.sum(sq, axis=1, keepdims=True) - sa
            rstd = jnp.where(lo, lax.rsqrt(sa * (1.0 / HALF) + NORM_EPS), lax.rsqrt(sb * (1.0 / HALF) + NORM_EPS))
            xh = x * rstd
            dg_ref[gi:gi + 1, :] += jnp.sum(dy * xh, axis=0, keepdims=True)
            dxh = dy * g_ref[gi:gi + 1, :]
            pr = dxh * xh
            ma = jnp.sum(jnp.where(lo, pr, 0.0), axis=1, keepdims=True)
            mb = jnp.sum(pr, axis=1, keepdims=True) - ma
            mu = jnp.where(lo, ma, mb) * (1.0 / HALF)
            o_ref[:, tt * 128:(tt + 1) * 128] = (rstd * (dxh - xh * mu)).astype(BF16)

        @pl.when(pl.program_id(0) == nsteps - 1)
        def _():
            acc = dg_ref[...]
            dg_ref[...] = acc + pltpu.roll(acc, HALF, 1)

    return pl.pallas_call(
        body, name=name, grid=(nsteps,),
        in_specs=[pl.BlockSpec((tm, 1024), lambda i: (i, 0)), pl.BlockSpec((tm, 256), lambda i: (i, 0)),
                  pl.BlockSpec((tm, 1024), lambda i: (i, 0)), pl.BlockSpec((tm, 256), lambda i: (i, 8)),
                  pl.BlockSpec((2, 128), lambda i: (0, 0))],
        out_specs=[pl.BlockSpec((tm, 1280), lambda i: (i, 0)), pl.BlockSpec((2, 128), lambda i: (0, 0))],
        out_shape=[SDS((s, 1280), BF16), SDS((2, 128), F32)],
        compiler_params=_cparams(("arbitrary",)),
    )(dqn, dkn, proj, proj, g128)


def _swa_masks(n):
    b = SWA_BLOCK
    row = lax.broadcasted_iota(jnp.int32, (b, 2 * b), 0)
    col = lax.broadcasted_iota(jnp.int32, (b, 2 * b), 1)
    rel = row + b - col
    valid = (rel >= 0) & (rel < b) & ((col >= b) | (n > 0))
    return rel.astype(F32), valid


def _swa_probs(qm, kt, relf, valid, slope, sink):
    z = _dot_nt(qm, kt) * (1.0 / math.sqrt(HALF))
    sc = jnp.where(valid, z - slope * relf, -1e30)
    m = jnp.maximum(jnp.max(sc, axis=1, keepdims=True), sink)
    e = jnp.where(valid, jnp.exp(sc - m), 0.0)
    es = jnp.exp(sink - m)
    den = jnp.sum(e, axis=1, keepdims=True) + es
    return e / den, es / den


def swa_attn_fwd(qkn, proj, sinks, name):
    s = qkn.shape[0]
    b = SWA_BLOCK
    nb = s // b

    def body(sink_ref, q_ref, kc_ref, kp_ref, vc_ref, vp_ref, o_ref):
        n = pl.program_id(0)
        lane = lax.broadcasted_iota(jnp.int32, (1, LANES), 1)
        relf, valid = _swa_masks(n)
        for tt in range(8):
            gp = tt // 4
            kt = jnp.concatenate([kp_ref[:, gp * 128:(gp + 1) * 128], kc_ref[:, gp * 128:(gp + 1) * 128]], axis=0)
            vt = jnp.concatenate([vp_ref[:, gp * 128:(gp + 1) * 128], vc_ref[:, gp * 128:(gp + 1) * 128]], axis=0)
            qt = q_ref[:, tt * 128:(tt + 1) * 128]
            outs = []
            for a in range(2):
                h = _swa_head(tt, a)
                hm = (lane < HALF) if a == 0 else (lane >= HALF)
                qm = jnp.where(hm, qt, jnp.zeros_like(qt))
                p, _ = _swa_probs(qm, kt, relf, valid, SWA_SLOPES[h], sink_ref[h])
                outs.append(_dot(p.astype(BF16), vt))
            o_ref[:, tt * 128:(tt + 1) * 128] = jnp.where(lane < HALF, outs[0], outs[1])

    cur = lambda w, c: pl.BlockSpec((b, w), lambda n: (n, c))
    prev = lambda w, c: pl.BlockSpec((b, w), lambda n: (jnp.maximum(n - 1, 0), c))
    return pl.pallas_call(
        body, name=name, grid=(nb,),
        in_specs=[pl.BlockSpec(memory_space=pltpu.SMEM), cur(1024, 0), cur(256, 4), prev(256, 4),
                  cur(256, 9), prev(256, 9)],
        out_specs=cur(1024, 0),
        out_shape=SDS((s, 1024), F32),
        compiler_params=_cparams(("parallel",)),
    )(sinks, qkn, qkn, qkn, proj, proj)


def swa_attn_bwd(qkn, proj, sinks, do, name):
    s = qkn.shape[0]
    b = SWA_BLOCK
    nb = s // b

    def body(sink_ref, q_ref, kc_ref, kp_ref, vc_ref, vp_ref, do_ref, dq_ref, dk_ref, dv_ref, ds_ref):
        n = pl.program_id(0)

        @pl.when(n == 0)
        def _():
            dk_ref[...] = jnp.zeros_like(dk_ref)
            dv_ref[...] = jnp.zeros_like(dv_ref)
            ds_ref[...] = jnp.zeros_like(ds_ref)

        lane = lax.broadcasted_iota(jnp.int32, (1, LANES), 1)
        relf, valid = _swa_masks(n)
        cur0 = pl.multiple_of(n * b, b)
        prev0 = pl.multiple_of(jnp.maximum(n - 1, 0) * b, b)
        dsink = jnp.zeros((1, LANES), F32)
        for gp in range(2):
            kt = jnp.concatenate([kp_ref[:, gp * 128:(gp + 1) * 128], kc_ref[:, gp * 128:(gp + 1) * 128]], axis=0)
            vt = jnp.concatenate([vp_ref[:, gp * 128:(gp + 1) * 128], vc_ref[:, gp * 128:(gp + 1) * 128]], axis=0)
            dkt = jnp.zeros((2 * b, LANES), F32)
            dvt = jnp.zeros((2 * b, LANES), F32)
            for tt in range(gp * 4, gp * 4 + 4):
                qt = q_ref[:, tt * 128:(tt + 1) * 128]
                dot_ = do_ref[:, tt * 128:(tt + 1) * 128]
                outs = []
                for a in range(2):
                    h = _swa_head(tt, a)
                    hm = (lane < HALF) if a == 0 else (lane >= HALF)
                    qm = jnp.where(hm, qt, jnp.zeros_like(qt))
                    dom = jnp.where(hm, dot_, jnp.zeros_like(dot_))
                    p, ps = _swa_probs(qm, kt, relf, valid, SWA_SLOPES[h], sink_ref[h])
                    dp = _dot_nt(dom, vt)
                    delta = jnp.sum(p * dp, axis=1, keepdims=True)
                    dz = (p * (dp - delta) * (1.0 / math.sqrt(HALF))).astype(BF16)
                    dsink = dsink + jnp.where(lane == h, -jnp.sum(ps * delta), 0.0)
                    outs.append(_dot(dz, kt))
                    dkt = dkt + _dot_tn(dz, qm)
                    dvt = dvt + _dot_tn(p.astype(BF16), dom)
                dq_ref[:, tt * 128:(tt + 1) * 128] = jnp.where(lane < HALF, outs[0], outs[1]).astype(BF16)
            cols = slice(gp * 128, (gp + 1) * 128)
            dk_ref[pl.ds(prev0, b), cols] += dkt[:b]
            dk_ref[pl.ds(cur0, b), cols] += dkt[b:]
            dv_ref[pl.ds(prev0, b), cols] += dvt[:b]
            dv_ref[pl.ds(cur0, b), cols] += dvt[b:]
        ds_ref[0:1, :] += dsink

    cur = lambda w, c: pl.BlockSpec((b, w), lambda n: (n, c))
    prev = lambda w, c: pl.BlockSpec((b, w), lambda n: (jnp.maximum(n - 1, 0), c))
    full = pl.BlockSpec((s, 256), lambda n: (0, 0))
    return pl.pallas_call(
        body, name=name, grid=(nb,),
        in_specs=[pl.BlockSpec(memory_space=pltpu.SMEM), cur(1024, 0), cur(256, 4), prev(256, 4),
                  cur(256, 9), prev(256, 9), cur(1024, 0)],
        out_specs=[cur(1024, 0), full, full, pl.BlockSpec((8, LANES), lambda n: (0, 0))],
        out_shape=[SDS((s, 1024), BF16), SDS((s, 256), F32), SDS((s, 256), F32), SDS((8, LANES), F32)],
        compiler_params=_cparams(("arbitrary",)),
    )(sinks, qkn, qkn, qkn, proj, proj, do)


def _my_pos():
    return lax.axis_index("x"), lax.axis_index("y"), lax.axis_index("c")


ANY = pl.BlockSpec(memory_space=pl.ANY)


def all_gather_packed(xs, name):
    r, n = xs.shape

    def body(x_ref, out_ref, send_sems, recv_sems, local_sem):
        x, y, c = _my_pos()
        me, sibling = (x, y, c), (x, y, 1 - c)
        chips = [(1 - x, y), (x, 1 - y), (1 - x, 1 - y)]

        def slot(px, py, pc):
            return out_ref.at[4 * px + 2 * py + pc]

        def copy(k, block, to, src=None):
            return pltpu.make_async_remote_copy(
                src_ref=slot(*block) if src is None else src, dst_ref=slot(*block),
                send_sem=send_sems.at[k], recv_sem=recv_sems.at[k],
                device_id=to, device_id_type=MESH_ID)

        mine = pltpu.make_async_copy(x_ref, slot(*me), local_sem)
        mine.start()
        first = [copy(0, me, sibling, src=x_ref)]
        first += [copy(1 + j, me, (*chip, c), src=x_ref) for j, chip in enumerate(chips)]
        for cp in first:
            cp.start()
        passed = [copy(4 + j, (*chip, c), sibling) for j, chip in enumerate(chips)]
        for j, chip in enumerate(chips):
            copy(1 + j, (*chip, c), me).wait_recv()
            passed[j].start()
        copy(0, sibling, me).wait_recv()
        for j, chip in enumerate(chips):
            copy(4 + j, (*chip, 1 - c), me).wait_recv()
        for cp in first + passed:
            cp.wait_send()
        mine.wait()

    return pl.pallas_call(
        body, name=name,
        out_shape=SDS((N_DEV, r, n), xs.dtype),
        in_specs=[ANY], out_specs=ANY,
        scratch_shapes=[pltpu.SemaphoreType.DMA((7,)), pltpu.SemaphoreType.DMA((7,)), pltpu.SemaphoreType.DMA],
    )(xs)


def rs_sibling_exchange(g, name):
    _, r, n = g.shape

    def body(g_ref, ra_ref, send_sems, recv_sems):
        x, y, c = _my_pos()
        copies = [pltpu.make_async_remote_copy(
            src_ref=g_ref.at[2 * chip + (1 - c)], dst_ref=ra_ref.at[chip],
            send_sem=send_sems.at[chip], recv_sem=recv_sems.at[chip],
            device_id=(x, y, 1 - c), device_id_type=MESH_ID) for chip in range(4)]
        for cp in copies:
            cp.start()
        for cp in copies:
            cp.wait()

    return pl.pallas_call(
        body, name=name, out_shape=SDS((4, r, n), g.dtype), in_specs=[ANY], out_specs=ANY,
        scratch_shapes=[pltpu.SemaphoreType.DMA((4,)), pltpu.SemaphoreType.DMA((4,))],
    )(g)


def rs_pair_add(g, ra, cidx, name):
    _, r, n = g.shape
    tr = r // 8

    def body(c_ref, g_ref, ra_ref, p_ref):
        p_ref[...] = (g_ref[...].astype(F32) + ra_ref[...].astype(F32)).astype(p_ref.dtype)

    return pl.pallas_call(
        body, name=name,
        grid_spec=pltpu.PrefetchScalarGridSpec(
            num_scalar_prefetch=1, grid=(4, r // tr),
            in_specs=[pl.BlockSpec((1, tr, n), lambda i, j, c: (2 * i + c[0], j, 0)),
                      pl.BlockSpec((1, tr, n), lambda i, j, c: (i, j, 0))],
            out_specs=pl.BlockSpec((1, tr, n), lambda i, j, c: (i, j, 0))),
        out_shape=SDS((4, r, n), g.dtype),
        compiler_params=_cparams(("parallel", "parallel")),
    )(cidx, g, ra)


def rs_cross_exchange(p, name):
    _, r, n = p.shape

    def body(p_ref, rb_ref, send_sems, recv_sems):
        x, y, c = _my_pos()
        copies = []
        for k, (dx, dy) in enumerate(((1, 0), (0, 1), (1, 1))):
            tx = 1 - x if dx else x
            ty = 1 - y if dy else y
            copies.append(pltpu.make_async_remote_copy(
                src_ref=p_ref.at[2 * tx + ty], dst_ref=rb_ref.at[k],
                send_sem=send_sems.at[k], recv_sem=recv_sems.at[k],
                device_id=(tx, ty, c), device_id_type=MESH_ID))
        for cp in copies:
            cp.start()
        for cp in copies:
            cp.wait()

    return pl.pallas_call(
        body, name=name, out_shape=SDS((3, r, n), p.dtype), in_specs=[ANY], out_specs=ANY,
        scratch_shapes=[pltpu.SemaphoreType.DMA((3,)), pltpu.SemaphoreType.DMA((3,))],
    )(p)


def _adamw(w, g, m, v):
    m = ADAM_B1 * m + (1.0 - ADAM_B1) * g
    v = ADAM_B2 * v + (1.0 - ADAM_B2) * (g * g)
    m_hat = m / (1.0 - ADAM_B1 ** ADAM_STEP)
    v_hat = v / (1.0 - ADAM_B2 ** ADAM_STEP)
    delta = -ADAM_LR * (m_hat / (jnp.sqrt(v_hat) + ADAM_EPS) + ADAM_WD * w)
    return delta, m, v


def rs_final_adamw(p, rb, chipidx, w, m, v, name):
    _, r, n = p.shape
    tr = r // 8

    def body(c_ref, p_ref, rb_ref, w_ref, m_ref, v_ref, g_out, d_out, m_out, v_out):
        g = p_ref[0].astype(F32)
        for k in range(3):
            g = g + rb_ref[k].astype(F32)
        d, mn, vn = _adamw(w_ref[...], g, m_ref[...], v_ref[...])
        g_out[...] = g
        d_out[...] = d
        m_out[...] = mn
        v_out[...] = vn

    blk = pl.BlockSpec((tr, n), lambda j, c: (j, 0))
    return pl.pallas_call(
        body, name=name,
        grid_spec=pltpu.PrefetchScalarGridSpec(
            num_scalar_prefetch=1, grid=(r // tr,),
            in_specs=[pl.BlockSpec((1, tr, n), lambda j, c: (c[0], j, 0)),
                      pl.BlockSpec((3, tr, n), lambda j, c: (0, j, 0)), blk, blk, blk],
            out_specs=[blk, blk, blk, blk]),
        out_shape=[SDS((r, n), F32)] * 4,
        compiler_params=_cparams(("parallel",)),
    )(chipidx, p, rb, w, m, v)


def small_allreduce_adamw(g, w, m, v, name):
    rows = g.shape[0]

    def body(g_ref, w_ref, m_ref, v_ref, g_out, d_out, m_out, v_out, buf, send_sems, recv_sems):
        x, y, c = _my_pos()
        my = 4 * x + 2 * y + c
        buf[my] = g_ref[...]
        copies = []
        for k in range(1, N_DEV):
            dx, dy, dc = (k >> 2) & 1, (k >> 1) & 1, k & 1
            tgt = (1 - x if dx else x, 1 - y if dy else y, 1 - c if dc else c)
            copies.append(pltpu.make_async_remote_copy(
                src_ref=buf.at[my], dst_ref=buf.at[my],
                send_sem=send_sems.at[k - 1], recv_sem=recv_sems.at[k - 1],
                device_id=tgt, device_id_type=MESH_ID))
        for cp in copies:
            cp.start()
        for cp in copies:
            cp.wait()
        tot = buf[0]
        for j in range(1, N_DEV):
            tot = tot + buf[j]
        d, mn, vn = _adamw(w_ref[...], tot, m_ref[...], v_ref[...])
        g_out[...] = tot
        d_out[...] = d
        m_out[...] = mn
        v_out[...] = vn

    vm = pl.BlockSpec(memory_space=pltpu.VMEM)
    return pl.pallas_call(
        body, name=name, out_shape=[SDS((rows, LANES), F32)] * 4,
        in_specs=[vm] * 4, out_specs=[vm] * 4,
        scratch_shapes=[pltpu.VMEM((N_DEV, rows, LANES), F32),
                        pltpu.SemaphoreType.DMA((N_DEV - 1,)), pltpu.SemaphoreType.DMA((N_DEV - 1,))],
    )(g, w, m, v)


BIG = ["l0_w_in", "l0_w_out", "l1_w_in", "l1_w_uq", "l1_w_ukv", "l1_w_out", "l2_w_in", "l2_w_out",
       "l3_w_in", "l3_w_out"]
SMALL = ["l0_norm", "l1_norm", "l1_q_a_norm", "l1_kv_a_norm", "l1_q_head_norm", "l1_k_head_norm",
         "l2_norm", "l2_q_head_norm", "l2_k_head_norm", "l2_sinks", "l3_norm"]
ROW_SHARDED = {"l0_w_out", "l1_w_out", "l2_w_out", "l3_w_out"}

MLA_UQ_PERM = np.concatenate([np.arange(h * 192, h * 192 + 128) for h in range(8)]
                             + [np.arange(h * 192 + 128, (h + 1) * 192) for h in range(8)])
SWA_IN_PERM = np.concatenate([SWA_Q_PERM, 1536 + SWA_Q_PERM, np.arange(1024, 1536)])


def _inv(perm):
    inv = np.empty_like(perm)
    inv[perm] = np.arange(perm.size)
    return inv


def _pack(shards):
    parts = [shards[nm].reshape(-1, D_MODEL) for nm in BIG]
    used = sum(p.shape[0] for p in parts)
    assert used == ROWS_USED
    parts.append(jnp.zeros((ROWS_PACK - used, D_MODEL), parts[0].dtype))
    return jnp.concatenate(parts, axis=0)


def _unpack(packed, shapes):
    out, off = {}, 0
    for nm in BIG:
        shp = shapes[nm]
        rows = shp[0] * shp[1] // D_MODEL
        out[nm] = packed[off:off + rows].reshape(shp)
        off += rows
    return out


def _full_weights(gathered, shapes):
    out, off = {}, 0
    for nm in BIG:
        shp = shapes[nm]
        rows = shp[0] * shp[1] // D_MODEL
        blk = gathered[:, off:off + rows].reshape((N_DEV,) + shp)
        if nm in ROW_SHARDED:
            out[nm] = blk.reshape(N_DEV * shp[0], shp[1])
        else:
            out[nm] = jnp.transpose(blk, (1, 0, 2)).reshape(shp[0], N_DEV * shp[1])
        off += rows
    return out


def _grad_pieces(full_grads, shapes):
    parts = []
    for nm in BIG:
        shp = shapes[nm]
        g = full_grads[nm]
        if nm in ROW_SHARDED:
            blk = g.reshape((N_DEV,) + shp)
        else:
            blk = jnp.transpose(g.reshape(shp[0], N_DEV, shp[1]), (1, 0, 2))
        parts.append(blk.reshape(N_DEV, -1, D_MODEL).astype(BF16))
    parts.append(jnp.zeros((N_DEV, ROWS_PACK - ROWS_USED, D_MODEL), BF16))
    return jnp.concatenate(parts, axis=1)


def _pack_small(vals):
    flat = jnp.concatenate([vals[nm].reshape(-1).astype(F32) for nm in SMALL])
    flat = jnp.concatenate([flat, jnp.zeros((SMALL_ROWS * LANES - flat.shape[0],), F32)])
    return flat.reshape(SMALL_ROWS, LANES)


def _unpack_small(packed, shapes):
    flat = packed.reshape(-1)
    out, off = {}, 0
    for nm in SMALL:
        n = shapes[nm][0]
        out[nm] = flat[off:off + n]
        off += n
    return out


def _row(v):
    return v.reshape(1, -1).astype(F32)


def _mla_gain_rows(g):
    return jnp.concatenate([g[:128], g[128:], g[128:]]).reshape(1, 256).astype(F32)


def _layer_fwd_sb(x, norm, w_in, w_out, tag):
    proj, xn = norm_matmul(x, 0, _row(norm), w_in, f"{tag}_in")
    o = sb_attn_fwd(proj, f"{tag}_attn")
    y = gate_out_fwd(o, proj, 3, w_out, x, f"{tag}_out")
    return y, (x, xn, proj, o)


def _layer_bwd_sb(dy, saved, norm, w_in, w_out, tag):
    x, xn, proj, o = saved
    do, dgate, hg = gate_out_bwd(dy, w_out, o, proj, 3, f"{tag}_outb")
    dw_out = tn_matmul(hg, dy, f"{tag}_dwout")
    dq, dk, dv = sb_attn_bwd(proj, o, do, f"{tag}_attnb")
    dproj = jnp.concatenate([dq, dk, dv, dgate], axis=1)
    dw_in = tn_matmul(xn, dproj, f"{tag}_dwin")
    dx, dnorm = nt_norm_bwd(dproj, w_in, x, 0, _row(norm), dy, f"{tag}_inb", F32)
    return dx, dw_in, dw_out, dnorm.reshape(-1)


def kernel(x, l0_norm, l0_w_in, l0_w_out, l1_norm, l1_w_in, l1_q_a_norm, l1_w_uq, l1_kv_a_norm, l1_w_ukv, l1_q_head_norm, l1_k_head_norm, l1_w_out, l2_norm, l2_w_in, l2_q_head_norm, l2_k_head_norm, l2_sinks, l2_w_out, l3_norm, l3_w_in, l3_w_out, loss_target, m_l0_norm, m_l0_w_in, m_l0_w_out, m_l1_norm, m_l1_w_in, m_l1_q_a_norm, m_l1_w_uq, m_l1_kv_a_norm, m_l1_w_ukv, m_l1_q_head_norm, m_l1_k_head_norm, m_l1_w_out, m_l2_norm, m_l2_w_in, m_l2_q_head_norm, m_l2_k_head_norm, m_l2_sinks, m_l2_w_out, m_l3_norm, m_l3_w_in, m_l3_w_out, v_l0_norm, v_l0_w_in, v_l0_w_out, v_l1_norm, v_l1_w_in, v_l1_q_a_norm, v_l1_w_uq, v_l1_kv_a_norm, v_l1_w_ukv, v_l1_q_head_norm, v_l1_k_head_norm, v_l1_w_out, v_l2_norm, v_l2_w_in, v_l2_q_head_norm, v_l2_k_head_norm, v_l2_sinks, v_l2_w_out, v_l3_norm, v_l3_w_in, v_l3_w_out):
    loc = dict(locals())
    names = BIG + SMALL
    w = {nm: loc[nm] for nm in names}
    mom = {nm: loc["m_" + nm] for nm in names}
    vel = {nm: loc["v_" + nm] for nm in names}
    shapes = {nm: w[nm].shape for nm in names}
    xs = x[0]
    tgt = loss_target[0]

    gathered = all_gather_packed(_pack({nm: w[nm].astype(BF16) for nm in BIG}), "ag_weights")
    fw = _full_weights(gathered, shapes)
    loss_local, dx0, fg, sg = _local_step(xs, tgt, fw, {nm: w[nm] for nm in SMALL})
    loss = lax.psum(loss_local, ("x", "y", "c"))
    return _reduce_and_update(loss, dx0, fg, sg, w, mom, vel, shapes)


def _local_step(xs, tgt, fw, sm):
    s = xs.shape[0]
    l0_norm, l1_norm, l2_norm, l3_norm = sm["l0_norm"], sm["l1_norm"], sm["l2_norm"], sm["l3_norm"]
    l1_q_a_norm, l1_kv_a_norm = sm["l1_q_a_norm"], sm["l1_kv_a_norm"]
    l1_q_head_norm, l1_k_head_norm = sm["l1_q_head_norm"], sm["l1_k_head_norm"]
    l2_q_head_norm, l2_k_head_norm, l2_sinks = sm["l2_q_head_norm"], sm["l2_k_head_norm"], sm["l2_sinks"]
    w1_in = jnp.concatenate([fw["l1_w_in"][:, 448:], fw["l1_w_in"][:, :448], jnp.zeros((D_MODEL, 64), BF16)], axis=1)
    w1_uq = fw["l1_w_uq"][:, MLA_UQ_PERM]
    w2_in = fw["l2_w_in"][:, SWA_IN_PERM]
    w2_out = fw["l2_w_out"][SWA_Q_PERM, :]

    x1, sv0 = _layer_fwd_sb(xs, l0_norm, fw["l0_w_in"], fw["l0_w_out"], "l0")

    cos, sin = _rope_tables(s)
    gq, gk = _mla_gain_rows(l1_q_head_norm), _mla_gain_rows(l1_k_head_norm)
    proj1, xn1 = norm_matmul(x1, 0, _row(l1_norm), w1_in, "l1_in")
    qa, qln = norm_matmul(proj1, 4, _row(l1_q_a_norm), w1_uq, "l1_uq")
    kva, kvn = norm_matmul(proj1, 10, _row(l1_kv_a_norm), fw["l1_w_ukv"], "l1_ukv")
    q_p, k_p = mla_prep_fwd(qa, kva, proj1, gq, gk, cos, sin, "l1_prep")
    o1, lse1 = mla_attn_fwd(q_p, k_p, kva, "l1_attn")
    x2 = gate_out_fwd(o1, proj1, 0, fw["l1_w_out"], x1, "l1_out")

    g2 = jnp.stack([jnp.tile(l2_q_head_norm, 2), jnp.tile(l2_k_head_norm, 2)]).astype(F32)
    proj2, xn2 = norm_matmul(x2, 0, _row(l2_norm), w2_in, "l2_in")
    qkn2 = swa_prep_fwd(proj2, g2, "l2_prep")
    o2 = swa_attn_fwd(qkn2, proj2, l2_sinks, "l2_attn")
    x3 = gate_out_fwd(o2, proj2, 1, w2_out, x2, "l2_out")

    x4, sv3 = _layer_fwd_sb(x3, l3_norm, fw["l3_w_in"], fw["l3_w_out"], "l3")

    lossblk, dy = loss_kernel(x4, tgt, "loss")

    fg, sg = {}, {}
    dx3, fg["l3_w_in"], fg["l3_w_out"], sg["l3_norm"] = _layer_bwd_sb(dy, sv3, l3_norm, fw["l3_w_in"], fw["l3_w_out"], "l3")

    do2, dgate2, hg2 = gate_out_bwd(dx3, w2_out, o2, proj2, 1, "l2_outb")
    dw2_out = tn_matmul(hg2, dx3, "l2_dwout")
    dqn2, dkn2, dv2, dsink2 = swa_attn_bwd(qkn2, proj2, l2_sinks, do2, "l2_attnb")
    dqk2, dg2 = swa_prep_bwd(dqn2, dkn2, proj2, g2, "l2_prepb")
    dproj2 = jnp.concatenate([dqk2[:, :1024], dgate2, dqk2[:, 1024:], dv2.astype(BF16)], axis=1)
    dw2_in = tn_matmul(xn2, dproj2, "l2_dwin")
    dx2, dn2 = nt_norm_bwd(dproj2, w2_in, x2, 0, _row(l2_norm), dx3, "l2_inb", F32)
    fg["l2_w_in"] = dw2_in[:, _inv(SWA_IN_PERM)]
    fg["l2_w_out"] = dw2_out[_inv(SWA_Q_PERM), :]
    sg["l2_norm"] = dn2.reshape(-1)
    sg["l2_q_head_norm"] = dg2[0, :HALF]
    sg["l2_k_head_norm"] = dg2[1, :HALF]
    sg["l2_sinks"] = dsink2[0, :SWA_HEADS]

    do1, dgate1, hg1 = gate_out_bwd(dx2, fw["l1_w_out"], o1, proj1, 0, "l1_outb")
    fg["l1_w_out"] = tn_matmul(hg1, dx2, "l1_dwout")
    dq_p, dk_p, dv1 = mla_attn_bwd(q_p, k_p, kva, o1, do1, lse1, "l1_attnb")
    dqa, dkva, dkpe, dgq, dgk = mla_prep_bwd(dq_p, dk_p, dv1, qa, kva, proj1, gq, gk, cos, sin, "l1_prepb")
    dw_uq = tn_matmul(qln, dqa, "l1_dwuq")
    fg["l1_w_ukv"] = tn_matmul(kvn, dkva, "l1_dwukv")
    dqlat, dgqa = nt_norm_bwd(dqa, w1_uq, proj1, 4, _row(l1_q_a_norm), None, "l1_uqb", BF16)
    dkvlat, dgkva = nt_norm_bwd(dkva, fw["l1_w_ukv"], proj1, 10, _row(l1_kv_a_norm), None, "l1_ukvb", BF16)
    dproj1 = jnp.concatenate([dgate1, dqlat, dkvlat, dkpe], axis=1)
    dw1_in = tn_matmul(xn1, dproj1, "l1_dwin")
    dx1, dn1 = nt_norm_bwd(dproj1, w1_in, x1, 0, _row(l1_norm), dx2, "l1_inb", F32)
    fg["l1_w_in"] = jnp.concatenate([dw1_in[:, 1024:1472], dw1_in[:, :1024]], axis=1)
    fg["l1_w_uq"] = dw_uq[:, _inv(MLA_UQ_PERM)]
    sg["l1_norm"] = dn1.reshape(-1)
    sg["l1_q_a_norm"] = dgqa.reshape(-1)
    sg["l1_kv_a_norm"] = dgkva.reshape(-1)
    sg["l1_q_head_norm"] = jnp.concatenate([dgq[0, :128], dgq[0, 128:192] + dgq[0, 192:256]])
    sg["l1_k_head_norm"] = jnp.concatenate([dgk[0, :128], dgk[0, 128:192] + dgk[0, 192:256]])

    dx0, fg["l0_w_in"], fg["l0_w_out"], sg["l0_norm"] = _layer_bwd_sb(dx1, sv0, l0_norm, fw["l0_w_in"], fw["l0_w_out"], "l0")
    return lossblk[0, 0], dx0, fg, sg


def _reduce_and_update(loss, dx0, fg, sg, w, mom, vel, shapes):
    xi, yi, ci = _my_pos()
    cidx = jnp.reshape(ci, (1,)).astype(jnp.int32)
    chipidx = jnp.reshape(2 * xi + yi, (1,)).astype(jnp.int32)
    pieces = _grad_pieces(fg, shapes)
    ra = rs_sibling_exchange(pieces, "rs_sibling")
    part = rs_pair_add(pieces, ra, cidx, "rs_pair_add")
    rb = rs_cross_exchange(part, "rs_cross")
    gb, db, mb, vb = rs_final_adamw(part, rb, chipidx, _pack({nm: w[nm] for nm in BIG}),
                                    _pack({nm: mom[nm] for nm in BIG}), _pack({nm: vel[nm] for nm in BIG}), "rs_adamw")
    gs, ds, ms, vs = small_allreduce_adamw(_pack_small(sg), _pack_small({nm: w[nm] for nm in SMALL}),
                                           _pack_small({nm: mom[nm] for nm in SMALL}),
                                           _pack_small({nm: vel[nm] for nm in SMALL}), "small_allreduce")

    outs = []
    for big, small in ((gb, gs), (db, ds), (mb, ms), (vb, vs)):
        d = _unpack(big, shapes)
        d.update(_unpack_small(small, shapes))
        outs.append(d)
    order = ["l0_norm", "l0_w_in", "l0_w_out", "l1_norm", "l1_w_in", "l1_q_a_norm", "l1_w_uq", "l1_kv_a_norm",
             "l1_w_ukv", "l1_q_head_norm", "l1_k_head_norm", "l1_w_out", "l2_norm", "l2_w_in", "l2_q_head_norm",
             "l2_k_head_norm", "l2_sinks", "l2_w_out", "l3_norm", "l3_w_in", "l3_w_out"]
    flat = [loss, dx0[None]]
    for d in outs:
        flat += [d[nm] for nm in order]
    return tuple(flat)
```

```python
import math

import numpy as np
import jax
import jax.numpy as jnp
from jax import lax
from jax.experimental import pallas as pl
from jax.experimental.pallas import tpu as pltpu

F32 = jnp.float32
BF16 = jnp.bfloat16
SDS = jax.ShapeDtypeStruct
MESH_ID = pl.DeviceIdType.MESH

D_MODEL = 1024
NORM_EPS = 1e-6
N_DEV = 8
LANES = 128
HALF = 64
ROWS_PACK = 2176
ROWS_USED = 2120
SMALL_ROWS = 40

ADAM_LR = 0.001
ADAM_B1 = 0.9
ADAM_B2 = 0.999
ADAM_EPS = 1e-08
ADAM_WD = 0.01
ADAM_STEP = 10

NT_DIMS = (((1,), (1,)), ((), ()))
TN_DIMS = (((0,), (0,)), ((), ()))


def _cparams(sem=None, vmem_mb=48):
    return pltpu.CompilerParams(dimension_semantics=sem, vmem_limit_bytes=vmem_mb * 2 ** 20)


def _dot(a, b):
    return jnp.dot(a, b, preferred_element_type=F32)


def _dot_nt(a, b):
    return lax.dot_general(a, b, NT_DIMS, preferred_element_type=F32)


def _dot_tn(a, b):
    return lax.dot_general(a, b, TN_DIMS, preferred_element_type=F32)


def _split_dot(a, b):
    hi = a.astype(BF16)
    lo = (a - hi.astype(F32)).astype(BF16)
    return _dot(hi, b) + _dot(lo, b)


def _cumsum_dot(lf, u):
    return _split_dot(lf, u)


def _pick_tile(n, cap):
    return max(t for t in range(LANES, min(n, cap) + 1, LANES) if n % t == 0)


def _sigmoid(x):
    return 1.0 / (1.0 + jnp.exp(-x))


def norm_matmul(x, colblk, g, w, name):
    s = x.shape[0]
    k, n = w.shape
    tm = min(1024, s)
    tn = _pick_tile(n, 1024)
    assert s % tm == 0

    def body(x_ref, g_ref, w_ref, o_ref, xn_ref, xn_s):
        @pl.when(pl.program_id(1) == 0)
        def _():
            xf = x_ref[...].astype(F32)
            ms = jnp.mean(xf * xf, axis=-1, keepdims=True)
            y = (xf * lax.rsqrt(ms + NORM_EPS) * g_ref[...]).astype(BF16)
            xn_s[...] = y
            xn_ref[...] = y

        o_ref[...] = _dot(xn_s[...], w_ref[...]).astype(o_ref.dtype)

    return pl.pallas_call(
        body, name=name, grid=(s // tm, n // tn),
        in_specs=[pl.BlockSpec((tm, k), lambda i, j: (i, colblk)),
                  pl.BlockSpec((1, k), lambda i, j: (0, 0)),
                  pl.BlockSpec((k, tn), lambda i, j: (0, j))],
        out_specs=[pl.BlockSpec((tm, tn), lambda i, j: (i, j)),
                   pl.BlockSpec((tm, k), lambda i, j: (i, 0))],
        out_shape=[SDS((s, n), BF16), SDS((s, k), BF16)],
        scratch_shapes=[pltpu.VMEM((tm, k), BF16)],
        compiler_params=_cparams(("parallel", "arbitrary")),
    )(x, g, w)


def gate_out_fwd(o, proj, gate_blk, w_out, x, name):
    s = x.shape[0]
    d = D_MODEL
    tm = min(512, s)

    def body(o_ref, g_ref, w_ref, x_ref, y_ref):
        g = g_ref[...].astype(F32)
        hg = (o_ref[...].astype(F32) * (g * _sigmoid(g))).astype(BF16)
        y_ref[...] = x_ref[...] + _dot(hg, w_ref[...])

    return pl.pallas_call(
        body, name=name, grid=(s // tm,),
        in_specs=[pl.BlockSpec((tm, d), lambda i: (i, 0)),
                  pl.BlockSpec((tm, d), lambda i: (i, gate_blk)),
                  pl.BlockSpec((d, d), lambda i: (0, 0)),
                  pl.BlockSpec((tm, d), lambda i: (i, 0))],
        out_specs=pl.BlockSpec((tm, d), lambda i: (i, 0)),
        out_shape=SDS((s, d), F32),
        compiler_params=_cparams(("parallel",)),
    )(o, proj, w_out, x)


def gate_out_bwd(dy, w_out, o, proj, gate_blk, name):
    s = dy.shape[0]
    d = D_MODEL
    tm = min(512, s)

    def body(dy_ref, w_ref, o_ref, g_ref, do_ref, dg_ref, hg_ref):
        dh = _dot_nt(dy_ref[...].astype(BF16), w_ref[...])
        g = g_ref[...].astype(F32)
        o = o_ref[...].astype(F32)
        sg = _sigmoid(g)
        silu = g * sg
        do_ref[...] = (dh * silu).astype(BF16)
        dg_ref[...] = (dh * o * (sg * (1.0 + g * (1.0 - sg)))).astype(BF16)
        hg_ref[...] = (o * silu).astype(BF16)

    blk = pl.BlockSpec((tm, d), lambda i: (i, 0))
    return pl.pallas_call(
        body, name=name, grid=(s // tm,),
        in_specs=[blk, pl.BlockSpec((d, d), lambda i: (0, 0)), blk,
                  pl.BlockSpec((tm, d), lambda i: (i, gate_blk))],
        out_specs=[blk, blk, blk],
        out_shape=[SDS((s, d), BF16)] * 3,
        compiler_params=_cparams(("parallel",)),
    )(dy, w_out, o, proj)


def tn_matmul(a, b, name):
    s, k = a.shape
    n = b.shape[1]
    tk = min(1024, k)
    tn = _pick_tile(n, 512)
    ts = min(2048, s)
    assert k % tk == 0 and s % ts == 0

    def body(a_ref, b_ref, o_ref):
        @pl.when(pl.program_id(2) == 0)
        def _():
            o_ref[...] = jnp.zeros_like(o_ref)

        o_ref[...] += _dot_tn(a_ref[...].astype(BF16), b_ref[...].astype(BF16))

    return pl.pallas_call(
        body, name=name, grid=(k // tk, n // tn, s // ts),
        in_specs=[pl.BlockSpec((ts, tk), lambda i, j, r: (r, i)),
                  pl.BlockSpec((ts, tn), lambda i, j, r: (r, j))],
        out_specs=pl.BlockSpec((tk, tn), lambda i, j, r: (i, j)),
        out_shape=SDS((k, n), F32),
        compiler_params=_cparams(("parallel", "parallel", "arbitrary")),
    )(a, b)


def nt_norm_bwd(dp, w, x, xcol, g, res, name, out_dtype):
    s, n = dp.shape
    k = w.shape[0]
    tm = min(512, s)
    assert s % tm == 0
    has_res = res is not None

    def body(*refs):
        if has_res:
            dp_ref, w_ref, x_ref, g_ref, r_ref, dx_ref, dg_ref = refs
        else:
            dp_ref, w_ref, x_ref, g_ref, dx_ref, dg_ref = refs

        @pl.when(pl.program_id(0) == 0)
        def _():
            dg_ref[...] = jnp.zeros_like(dg_ref)

        dxn = _dot_nt(dp_ref[...], w_ref[...])
        xf = x_ref[...].astype(F32)
        rstd = lax.rsqrt(jnp.mean(xf * xf, axis=-1, keepdims=True) + NORM_EPS)
        xhat = xf * rstd
        dg_ref[...] += jnp.sum(dxn * xhat, axis=0, keepdims=True)
        dxh = dxn * g_ref[...]
        dx = rstd * (dxh - xhat * jnp.mean(dxh * xhat, axis=-1, keepdims=True))
        if has_res:
            dx = dx + r_ref[...]
        dx_ref[...] = dx.astype(out_dtype)

    in_specs = [pl.BlockSpec((tm, n), lambda i: (i, 0)),
                pl.BlockSpec((k, n), lambda i: (0, 0)),
                pl.BlockSpec((tm, k), lambda i: (i, xcol)),
                pl.BlockSpec((1, k), lambda i: (0, 0))]
    args = [dp, w, x, g]
    if has_res:
        in_specs.append(pl.BlockSpec((tm, k), lambda i: (i, 0)))
        args.append(res)
    return pl.pallas_call(
        body, name=name, grid=(s // tm,),
        in_specs=in_specs,
        out_specs=[pl.BlockSpec((tm, k), lambda i: (i, 0)),
                   pl.BlockSpec((1, k), lambda i: (0, 0))],
        out_shape=[SDS((s, k), out_dtype), SDS((1, k), F32)],
        compiler_params=_cparams(("arbitrary",)),
    )(*args)


def loss_kernel(y, target, name):
    s, d = y.shape
    tm = min(512, s)

    def body(y_ref, t_ref, l_ref, dy_ref):
        @pl.when(pl.program_id(0) == 0)
        def _():
            l_ref[...] = jnp.zeros_like(l_ref)

        e = y_ref[...] - t_ref[...]
        dy_ref[...] = e * (1.0 / d)
        l_ref[...] += 0.5 * jnp.sum(jnp.mean(e * e, axis=-1, keepdims=True))

    blk = pl.BlockSpec((tm, d), lambda i: (i, 0))
    return pl.pallas_call(
        body, name=name, grid=(s // tm,),
        in_specs=[blk, blk],
        out_specs=[pl.BlockSpec((8, LANES), lambda i: (0, 0)), blk],
        out_shape=[SDS((8, LANES), F32), SDS((s, d), F32)],
        compiler_params=_cparams(("arbitrary",)),
    )(y, target)


SB_LOG_CUTOFF = -80.0


def _sb_sweep(qi, c, rest, step):
    def cond(st):
        return (st[0] <= qi) & (st[1] > SB_LOG_CUTOFF)

    def body(st):
        out = tuple(step(st[0], st[2], st[3:]))
        return (st[0] + 1, jnp.max(out[0])) + out

    return lax.while_loop(cond, body, (jnp.int32(1), jnp.max(c), c) + tuple(rest))[3:]


def _tile_masks(t):
    row = lax.broadcasted_iota(jnp.int32, (t, t), 0)
    col = lax.broadcasted_iota(jnp.int32, (t, t), 1)
    return row, col


def sb_attn_fwd(proj, name):
    s = proj.shape[0]
    t = min(256, s)
    nq = s // t
    scale = 1.0 / math.sqrt(HALF)

    def body(q_ref, k_ref, v_ref, o_ref):
        lane = lax.broadcasted_iota(jnp.int32, (1, LANES), 1)
        row, col = _tile_masks(t)
        u_incl = (row >= col).astype(BF16)
        tri = jnp.concatenate([col < row] * 2, axis=0)

        def scores(qs, kj):
            k = k_ref[pl.ds(pl.multiple_of(kj * t, t), t), :]
            z = _dot_nt(qs, k)
            nz = -z
            lf = jnp.minimum(nz, 0.0) - jnp.log(1.0 + jnp.exp(jnp.minimum(z, nz)))
            return z, lf

        def accumulate(kj, z, lf, c, acc, diag):
            v = v_ref[pl.ds(pl.multiple_of(kj * t, t), t), :]
            if diag:
                lf = jnp.where(tri, lf, 0.0)
            incl = _cumsum_dot(lf, u_incl) + c
            a = jnp.exp(z + incl)
            if diag:
                a = jnp.where(tri, a, 0.0)
            acc = acc + _dot(a.astype(BF16), v)
            c = c + jnp.sum(lf, axis=1, keepdims=True)
            return c, acc

        def q_body(qi, carry):
            q0 = pl.multiple_of(qi * t, t)
            q = q_ref[pl.ds(q0, t), :] * scale
            zq = jnp.zeros_like(q)
            qs = jnp.concatenate([jnp.where(lane < HALF, q, zq), jnp.where(lane >= HALF, q, zq)], axis=0)

            def step(jj, cr, diag=False):
                z, lf = scores(qs, qi - jj)
                return accumulate(qi - jj, z, lf, cr[0], cr[1], diag)

            c, acc = step(0, (jnp.zeros((2 * t, 1), F32), jnp.zeros((2 * t, LANES), F32)), True)
            acc = _sb_sweep(qi, c, (acc,), lambda jj, c, rest: step(jj, (c, rest[0])))[0]
            o_ref[pl.ds(q0, t), :] = jnp.where(lane < HALF, acc[:t], acc[t:])
            return carry

        lax.fori_loop(0, nq, q_body, 0)

    return pl.pallas_call(
        body, name=name, grid=(8,),
        in_specs=[pl.BlockSpec((s, LANES), lambda p: (0, p)),
                  pl.BlockSpec((s, LANES), lambda p: (0, 8 + p)),
                  pl.BlockSpec((s, LANES), lambda p: (0, 16 + p))],
        out_specs=pl.BlockSpec((s, LANES), lambda p: (0, p)),
        out_shape=SDS((s, D_MODEL), F32),
        compiler_params=_cparams(("parallel",)),
    )(proj, proj, proj)


def sb_attn_bwd(proj, o, do, name):
    s = proj.shape[0]
    t = min(256, s)
    nq = s // t
    scale = 1.0 / math.sqrt(HALF)

    def body(q_ref, k_ref, v_ref, o_ref, do_ref, dq_ref, dk_ref, dv_ref, dk_s, dv_s):
        dk_s[...] = jnp.zeros_like(dk_s)
        dv_s[...] = jnp.zeros_like(dv_s)
        lane = lax.broadcasted_iota(jnp.int32, (1, LANES), 1)
        row, col = _tile_masks(t)
        u_incl = (row >= col).astype(BF16)
        u_excl = (row > col).astype(BF16)
        tri = jnp.concatenate([col < row] * 2, axis=0)

        def scores(qs, dos, kj):
            k0 = pl.multiple_of(kj * t, t)
            z = _dot_nt(qs, k_ref[pl.ds(k0, t), :])
            nz = -z
            lf = jnp.minimum(nz, 0.0) - jnp.log(1.0 + jnp.exp(jnp.minimum(z, nz)))
            return z, lf, _dot_nt(dos, v_ref[pl.ds(k0, t), :])

        def accumulate(qs, dos, tsum, kj, z, lf, da, c, r, dq, diag):
            k0 = pl.multiple_of(kj * t, t)
            sig = jnp.exp(z + lf)
            if diag:
                lf = jnp.where(tri, lf, 0.0)
            incl = _cumsum_dot(lf, u_incl) + c
            a = jnp.exp(z + incl)
            if diag:
                a = jnp.where(tri, a, 0.0)
            ab = a.astype(BF16)
            dl = ab.astype(F32) * da
            pre = tsum - r - _split_dot(dl, u_excl)
            dz = dl - sig * pre
            if diag:
                dz = jnp.where(tri, dz, 0.0)
            dzb = dz.astype(BF16)
            dq = dq + _dot(dzb, k_ref[pl.ds(k0, t), :])
            dk_s[pl.ds(k0, t), :] += _dot_tn(dzb, qs)
            dv_s[pl.ds(k0, t), :] += _dot_tn(ab, dos)
            c = c + jnp.sum(lf, axis=1, keepdims=True)
            r = r + jnp.sum(dl, axis=1, keepdims=True)
            return c, r, dq

        def q_body(qi, carry):
            q0 = pl.multiple_of(qi * t, t)
            q = q_ref[pl.ds(q0, t), :] * scale
            dob = do_ref[pl.ds(q0, t), :]
            prod = dob.astype(F32) * o_ref[pl.ds(q0, t), :]
            zb = jnp.zeros_like(q)
            lo, hi = lane < HALF, lane >= HALF
            qs = jnp.concatenate([jnp.where(lo, q, zb), jnp.where(hi, q, zb)], axis=0)
            dos = jnp.concatenate([jnp.where(lo, dob, zb), jnp.where(hi, dob, zb)], axis=0)
            tsum = jnp.concatenate([jnp.sum(jnp.where(lo, prod, 0.0), axis=1, keepdims=True),
                                    jnp.sum(jnp.where(hi, prod, 0.0), axis=1, keepdims=True)], axis=0)
            zc = jnp.zeros((2 * t, 1), F32)

            def step(jj, cr, diag=False):
                z, lf, da = scores(qs, dos, qi - jj)
                return accumulate(qs, dos, tsum, qi - jj, z, lf, da, cr[0], cr[1], cr[2], diag)

            c, r, dq = step(0, (zc, zc, jnp.zeros((2 * t, LANES), F32)), True)
            dq = _sb_sweep(qi, c, (r, dq), lambda jj, c, rest: step(jj, (c,) + tuple(rest)))[1]
            dq_ref[pl.ds(q0, t), :] = (jnp.where(lo, dq[:t], dq[t:]) * scale).astype(BF16)
            return carry

        lax.fori_loop(0, nq, q_body, 0)
        dk_ref[...] = dk_s[...].astype(BF16)
        dv_ref[...] = dv_s[...].astype(BF16)

    blk = lambda off: pl.BlockSpec((s, LANES), lambda p: (0, off + p))
    return pl.pallas_call(
        body, name=name, grid=(8,),
        in_specs=[blk(0), blk(8), blk(16), blk(0), blk(0)],
        out_specs=[blk(0), blk(0), blk(0)],
        out_shape=[SDS((s, D_MODEL), BF16)] * 3,
        scratch_shapes=[pltpu.VMEM((s, LANES), F32), pltpu.VMEM((s, LANES), F32)],
        compiler_params=_cparams(("parallel",)),
    )(proj, proj, proj, o, do)


MLA_HEADS = 8
MLA_QK = 192
MLA_SCALE = 1.0 / math.sqrt(MLA_QK)
MLA_TK = 512


def _rope_tables(s):
    inv_freq = 10000.0 ** (-jnp.arange(32, dtype=F32) / 32)
    ang = jnp.arange(s, dtype=F32)[:, None] * inv_freq[None, :]
    cos = jnp.tile(jnp.cos(ang), (1, 4))
    sin = jnp.sin(ang)
    sin_signed = jnp.tile(jnp.concatenate([-sin, sin], axis=1), (1, 2))
    return cos, sin_signed


def _partner(u, lane):
    return jnp.where((lane % HALF) < 32, pltpu.roll(u, 96, 1), pltpu.roll(u, 32, 1))


def _mla_head_inputs(h, qa_ref, kva_ref, kpe, lane):
    a = h % 2
    hm = (lane < HALF) if a == 0 else (lane >= HALF)
    qn = qa_ref[:, h * 128:(h + 1) * 128].astype(F32)
    qr = jnp.where(hm, qa_ref[:, 1024 + (h // 2) * 128:1024 + (h // 2 + 1) * 128].astype(F32), 0.0)
    kn = kva_ref[:, h * 256:h * 256 + 128].astype(F32)
    kr = kpe if a == 0 else pltpu.roll(kpe, HALF, 1)
    return qn, qr, kn, kr


def mla_prep_fwd(qa, kva, proj, gq, gk, cos, sin, name):
    s = qa.shape[0]
    tm = min(256, s)

    def body(qa_ref, kva_ref, kpe_ref, gq_ref, gk_ref, cos_ref, sin_ref, q_out, k_out):
        lane = lax.broadcasted_iota(jnp.int32, (1, LANES), 1)
        kpe = kpe_ref[...].astype(F32)
        cs, sn = cos_ref[...], sin_ref[...]
        for h in range(MLA_HEADS):
            qn, qr, kn, kr = _mla_head_inputs(h, qa_ref, kva_ref, kpe, lane)
            for xn, xr, g_ref, out, sc in ((qn, qr, gq_ref, q_out, MLA_SCALE), (kn, kr, gk_ref, k_out, 1.0)):
                ss = jnp.sum(xn * xn, axis=1, keepdims=True) + jnp.sum(xr * xr, axis=1, keepdims=True)
                rstd = lax.rsqrt(ss * (1.0 / MLA_QK) + NORM_EPS) * sc
                yn = xn * rstd * g_ref[:, 0:128]
                ur = xr * rstd * g_ref[:, 128:256]
                yr = ur * cs + _partner(ur, lane) * sn
                out[:, h * 256:h * 256 + 128] = yn.astype(BF16)
                out[:, h * 256 + 128:h * 256 + 256] = yr.astype(BF16)

    row = lambda w, c: pl.BlockSpec((tm, w), lambda i: (i, c))
    full = lambda w: pl.BlockSpec((1, w), lambda i: (0, 0))
    return pl.pallas_call(
        body, name=name, grid=(s // tm,),
        in_specs=[row(1536, 0), row(2048, 0), row(128, 11), full(256), full(256), row(128, 0), row(128, 0)],
        out_specs=[row(2048, 0), row(2048, 0)],
        out_shape=[SDS((s, 2048), BF16)] * 2,
        compiler_params=_cparams(("parallel",)),
    )(qa, kva, proj, gq, gk, cos, sin)


def mla_prep_bwd(dq_p, dk_p, dv, qa, kva, proj, gq, gk, cos, sin, name):
    s = qa.shape[0]
    tm = min(256, s)

    def body(dq_ref, dk_ref, dv_ref, qa_ref, kva_ref, kpe_ref, gq_ref, gk_ref, cos_ref, sin_ref,
             dqa_ref, dkva_ref, dkpe_ref, dgq_ref, dgk_ref):
        @pl.when(pl.program_id(0) == 0)
        def _():
            dgq_ref[...] = jnp.zeros_like(dgq_ref)
            dgk_ref[...] = jnp.zeros_like(dgk_ref)

        lane = lax.broadcasted_iota(jnp.int32, (1, LANES), 1)
        kpe = kpe_ref[...].astype(F32)
        cs, sn = cos_ref[...], sin_ref[...]
        dkpe = jnp.zeros((tm, LANES), F32)
        dqr_pair = None
        for h in range(MLA_HEADS):
            qn, qr, kn, kr = _mla_head_inputs(h, qa_ref, kva_ref, kpe, lane)
            res = []
            for xn, xr, g_ref, d_ref, dg_ref, sc in ((qn, qr, gq_ref, dq_ref, dgq_ref, MLA_SCALE),
                                                     (kn, kr, gk_ref, dk_ref, dgk_ref, 1.0)):
                ss = jnp.sum(xn * xn, axis=1, keepdims=True) + jnp.sum(xr * xr, axis=1, keepdims=True)
                rstd = lax.rsqrt(ss * (1.0 / MLA_QK) + NORM_EPS)
                hn, hr = xn * rstd, xr * rstd
                dyn = d_ref[:, h * 256:h * 256 + 128] * sc
                dyr = d_ref[:, h * 256 + 128:h * 256 + 256] * sc
                dur = dyr * cs - _partner(dyr, lane) * sn
                dg_ref[:, 0:128] += jnp.sum(dyn * hn, axis=0, keepdims=True)
                dg_ref[:, 128:256] += jnp.sum(dur * hr, axis=0, keepdims=True)
                dhn = dyn * g_ref[:, 0:128]
                dhr = dur * g_ref[:, 128:256]
                mu = (jnp.sum(dhn * hn, axis=1, keepdims=True)
                      + jnp.sum(dhr * hr, axis=1, keepdims=True)) * (1.0 / MLA_QK)
                res.append((rstd * (dhn - hn * mu), rstd * (dhr - hr * mu)))
            (dqn, dqr), (dkn, dkr) = res
            dqa_ref[:, h * 128:(h + 1) * 128] = dqn.astype(BF16)
            if h % 2 == 0:
                dqr_pair = dqr
            else:
                dqa_ref[:, 1024 + (h // 2) * 128:1024 + (h // 2 + 1) * 128] = (dqr_pair + dqr).astype(BF16)
            dkva_ref[:, h * 256:h * 256 + 128] = dkn.astype(BF16)
            dkva_ref[:, h * 256 + 128:h * 256 + 256] = dv_ref[:, h * 128:(h + 1) * 128]
            dkpe = dkpe + (dkr if h % 2 == 0 else pltpu.roll(dkr, HALF, 1))
        dkpe_ref[...] = dkpe.astype(BF16)

    row = lambda w, c: pl.BlockSpec((tm, w), lambda i: (i, c))
    full = lambda w: pl.BlockSpec((1, w), lambda i: (0, 0))
    return pl.pallas_call(
        body, name=name, grid=(s // tm,),
        in_specs=[row(2048, 0), row(2048, 0), row(1024, 0), row(1536, 0), row(2048, 0), row(128, 11),
                  full(256), full(256), row(128, 0), row(128, 0)],
        out_specs=[row(1536, 0), row(2048, 0), row(128, 0), full(256), full(256)],
        out_shape=[SDS((s, 1536), BF16), SDS((s, 2048), BF16), SDS((s, 128), BF16),
                   SDS((1, 256), F32), SDS((1, 256), F32)],
        compiler_params=_cparams(("arbitrary",)),
    )(dq_p, dk_p, dv, qa, kva, proj, gq, gk, cos, sin)


def mla_attn_fwd(q_p, k_p, kva, name):
    s = q_p.shape[0]
    tk = min(MLA_TK, s)
    tq = min(512, s)
    r = tq // tk
    nq = s // tq

    def body(q_ref, k_ref, v_ref, o_ref, lse_ref):
        row = lax.broadcasted_iota(jnp.int32, (tq, tk), 0)
        col = lax.broadcasted_iota(jnp.int32, (tq, tk), 1)

        def tile(q, kj, m, l, acc, mask):
            k0 = pl.multiple_of(kj * tk, tk)
            sc = _dot_nt(q, k_ref[pl.ds(k0, tk), :])
            if mask is not None:
                sc = jnp.where(mask, sc, -1e30)
            m_new = jnp.maximum(m, jnp.max(sc, axis=1, keepdims=True))
            alpha = jnp.exp(m - m_new)
            p = jnp.exp(sc - m_new)
            l = alpha * l + jnp.sum(p, axis=1, keepdims=True)
            acc = alpha * acc + _dot(p.astype(BF16), v_ref[pl.ds(k0, tk), :])
            return m_new, l, acc

        def q_body(qi, carry):
            q0 = pl.multiple_of(qi * tq, tq)
            q = q_ref[pl.ds(q0, tq), :]
            cr = (jnp.full((tq, 1), -1e30, F32), jnp.zeros((tq, 1), F32), jnp.zeros((tq, LANES), F32))
            cr = lax.fori_loop(0, qi * r, lambda kj, cr: tile(q, kj, cr[0], cr[1], cr[2], None), cr)
            for d in range(r):
                cr = tile(q, qi * r + d, cr[0], cr[1], cr[2], col + d * tk <= row)
            m, l, acc = cr
            o_ref[pl.ds(q0, tq), :] = acc / l
            lse_ref[pl.ds(q0, tq), :] = jnp.broadcast_to(m + jnp.log(l), (tq, LANES))
            return carry

        lax.fori_loop(0, nq, q_body, 0)

    return pl.pallas_call(
        body, name=name, grid=(MLA_HEADS,),
        in_specs=[pl.BlockSpec((s, 256), lambda h: (0, h)),
                  pl.BlockSpec((s, 256), lambda h: (0, h)),
                  pl.BlockSpec((s, 128), lambda h: (0, 2 * h + 1))],
        out_specs=[pl.BlockSpec((s, 128), lambda h: (0, h)),
                   pl.BlockSpec((s, 128), lambda h: (0, h))],
        out_shape=[SDS((s, 1024), F32), SDS((s, 1024), F32)],
        compiler_params=_cparams(("parallel",)),
    )(q_p, k_p, kva)


def mla_attn_bwd(q_p, k_p, kva, o, do, lse, name):
    s = q_p.shape[0]
    tk = min(MLA_TK, s)
    tq = min(512, s)
    r = tq // tk
    nq = s // tq

    def body(q_ref, k_ref, v_ref, o_ref, do_ref, lse_ref, dq_ref, dk_ref, dv_ref, dv_s):
        dk_ref[...] = jnp.zeros_like(dk_ref)
        dv_s[...] = jnp.zeros_like(dv_s)
        row = lax.broadcasted_iota(jnp.int32, (tq, tk), 0)
        col = lax.broadcasted_iota(jnp.int32, (tq, tk), 1)

        def tile(q, dob, lse, delta, kj, dq, mask):
            k0 = pl.multiple_of(kj * tk, tk)
            k = k_ref[pl.ds(k0, tk), :]
            v = v_ref[pl.ds(k0, tk), :]
            p = jnp.exp(_dot_nt(q, k) - lse)
            if mask is not None:
                p = jnp.where(mask, p, 0.0)
            ds = (p * (_dot_nt(dob, v) - delta)).astype(BF16)
            dq = dq + _dot(ds, k)
            dk_ref[pl.ds(k0, tk), :] += _dot_tn(ds, q)
            dv_s[pl.ds(k0, tk), :] += _dot_tn(p.astype(BF16), dob)
            return dq

        def q_body(qi, carry):
            q0 = pl.multiple_of(qi * tq, tq)
            q = q_ref[pl.ds(q0, tq), :]
            dob = do_ref[pl.ds(q0, tq), :]
            lse = lse_ref[pl.ds(q0, tq), 0:1]
            delta = jnp.sum(dob.astype(F32) * o_ref[pl.ds(q0, tq), :], axis=1, keepdims=True)
            dq = lax.fori_loop(0, qi * r, lambda kj, dq: tile(q, dob, lse, delta, kj, dq, None),
                               jnp.zeros((tq, 256), F32))
            for d in range(r):
                dq = tile(q, dob, lse, delta, qi * r + d, dq, col + d * tk <= row)
            dq_ref[pl.ds(q0, tq), :] = dq
            return carry

        lax.fori_loop(0, nq, q_body, 0)
        dv_ref[...] = dv_s[...].astype(BF16)

    b256 = pl.BlockSpec((s, 256), lambda h: (0, h))
    b128 = pl.BlockSpec((s, 128), lambda h: (0, h))
    return pl.pallas_call(
        body, name=name, grid=(MLA_HEADS,),
        in_specs=[b256, b256, pl.BlockSpec((s, 128), lambda h: (0, 2 * h + 1)), b128, b128, b128],
        out_specs=[b256, b256, b128],
        out_shape=[SDS((s, 2048), F32), SDS((s, 2048), F32), SDS((s, 1024), BF16)],
        scratch_shapes=[pltpu.VMEM((s, 128), F32)],
        compiler_params=_cparams(("parallel",), vmem_mb=56),
    )(q_p, k_p, kva, o, do, lse)


SWA_HEADS = 16
SWA_BLOCK = 128


def _swa_head(tile_idx, half):
    return (2 * (tile_idx // 4) + half) * 4 + tile_idx % 4


SWA_Q_PERM = np.concatenate([np.arange(_swa_head(tt, a) * HALF, (_swa_head(tt, a) + 1) * HALF)
                             for tt in range(8) for a in range(2)])
SWA_SLOPES = [2.0 ** (-8.0 * (h + 1) / SWA_HEADS) for h in range(SWA_HEADS)]


def swa_prep_fwd(proj, g128, name):
    s = proj.shape[0]
    tm = min(512, s)

    def body(q_ref, k_ref, g_ref, o_ref):
        lane = lax.broadcasted_iota(jnp.int32, (1, LANES), 1)
        lo = lane < HALF
        for tt in range(10):
            if tt < 8:
                x = q_ref[:, tt * 128:(tt + 1) * 128].astype(F32)
            else:
                x = k_ref[:, (tt - 8) * 128:(tt - 7) * 128].astype(F32)
            sq = x * x
            sa = jnp.sum(jnp.where(lo, sq, 0.0), axis=1, keepdims=True)
            sb = jnp.sum(sq, axis=1, keepdims=True) - sa
            rstd = jnp.where(lo, lax.rsqrt(sa * (1.0 / HALF) + NORM_EPS), lax.rsqrt(sb * (1.0 / HALF) + NORM_EPS))
            gi = 0 if tt < 8 else 1
            o_ref[:, tt * 128:(tt + 1) * 128] = (x * rstd * g_ref[gi:gi + 1, :]).astype(BF16)

    return pl.pallas_call(
        body, name=name, grid=(s // tm,),
        in_specs=[pl.BlockSpec((tm, 1024), lambda i: (i, 0)), pl.BlockSpec((tm, 256), lambda i: (i, 8)),
                  pl.BlockSpec((2, 128), lambda i: (0, 0))],
        out_specs=pl.BlockSpec((tm, 1280), lambda i: (i, 0)),
        out_shape=SDS((s, 1280), BF16),
        compiler_params=_cparams(("parallel",)),
    )(proj, proj, g128)


def swa_prep_bwd(dqn, dkn, proj, g128, name):
    s = proj.shape[0]
    tm = min(512, s)
    nsteps = s // tm

    def body(dq_ref, dk_ref, xq_ref, xk_ref, g_ref, o_ref, dg_ref):
        @pl.when(pl.program_id(0) == 0)
        def _():
            dg_ref[...] = jnp.zeros_like(dg_ref)

        lane = lax.broadcasted_iota(jnp.int32, (1, LANES), 1)
        lo = lane < HALF
        for tt in range(10):
            if tt < 8:
                x = xq_ref[:, tt * 128:(tt + 1) * 128].astype(F32)
                dy = dq_ref[:, tt * 128:(tt + 1) * 128].astype(F32)
            else:
                x = xk_ref[:, (tt - 8) * 128:(tt - 7) * 128].astype(F32)
                dy = dk_ref[:, (tt - 8) * 128:(tt - 7) * 128].astype(F32)
            gi = 0 if tt < 8 else 1
            sq = x * x
            sa = jnp.sum(jnp.where(lo, sq, 0.0), axis=1, keepdims=True)
            sb = jnp.sum(sq, axis=1, keepdims=True) - sa
            rstd = jnp.where(lo, lax.rsqrt(sa * (1.0 / HALF) + NORM_EPS), lax.rsqrt(sb * (1.0 / HALF) + NORM_EPS))
            xh = x * rstd
            dg_ref[gi:gi + 1, :] += jnp.sum(dy * xh, axis=0, keepdims=True)
            dxh = dy * g_ref[gi:gi + 1, :]
            pr = dxh * xh
            ma = jnp.sum(jnp.where(lo, pr, 0.0), axis=1, keepdims=True)
            mb = jnp.sum(pr, axis=1, keepdims=True) - ma
            mu = jnp.where(lo, ma, mb) * (1.0 / HALF)
            o_ref[:, tt * 128:(tt + 1) * 128] = (rstd * (dxh - xh * mu)).astype(BF16)

        @pl.when(pl.program_id(0) == nsteps - 1)
        def _():
            acc = dg_ref[...]
            dg_ref[...] = acc + pltpu.roll(acc, HALF, 1)

    return pl.pallas_call(
        body, name=name, grid=(nsteps,),
        in_specs=[pl.BlockSpec((tm, 1024), lambda i: (i, 0)), pl.BlockSpec((tm, 256), lambda i: (i, 0)),
                  pl.BlockSpec((tm, 1024), lambda i: (i, 0)), pl.BlockSpec((tm, 256), lambda i: (i, 8)),
                  pl.BlockSpec((2, 128), lambda i: (0, 0))],
        out_specs=[pl.BlockSpec((tm, 1280), lambda i: (i, 0)), pl.BlockSpec((2, 128), lambda i: (0, 0))],
        out_shape=[SDS((s, 1280), BF16), SDS((2, 128), F32)],
        compiler_params=_cparams(("arbitrary",)),
    )(dqn, dkn, proj, proj, g128)


SWA_STACK = 8
SWA_QSCALE = 1.0 / math.sqrt(HALF)


def _swa_masks(n, stack):
    b = SWA_BLOCK
    row = lax.broadcasted_iota(jnp.int32, (stack * b, 2 * b), 0) & (b - 1)
    col = lax.broadcasted_iota(jnp.int32, (stack * b, 2 * b), 1)
    rel = row + b - col
    valid = (rel >= 0) & (rel < b) & ((col >= b) | (n > 0))
    return rel.astype(F32), valid


def _swa_stack(ref, gp, lane, scale=None):
    parts = []
    for tt in range(4 * gp, 4 * gp + 4):
        x = ref[:, tt * 128:(tt + 1) * 128]
        if scale is not None:
            x = x * scale
        zx = jnp.zeros_like(x)
        parts += [jnp.where(lane < HALF, x, zx), jnp.where(lane >= HALF, x, zx)]
    return jnp.concatenate(parts, axis=0)


def _swa_unstack(st, lane):
    b = SWA_BLOCK
    return [jnp.where(lane < HALF, st[(2 * i) * b:(2 * i + 1) * b], st[(2 * i + 1) * b:(2 * i + 2) * b])
            for i in range(4)]


def _swa_head_cols(gp, sink_ref):
    b = SWA_BLOCK
    heads = [_swa_head(tt, a) for tt in range(4 * gp, 4 * gp + 4) for a in range(2)]
    slope = jnp.concatenate([jnp.full((b, 1), SWA_SLOPES[h], F32) for h in heads], axis=0)
    sink = jnp.concatenate([jnp.full((b, 1), sink_ref[h], F32) for h in heads], axis=0)
    return heads, slope, sink


def _swa_probs(qs, kt, relf, valid, slope, sink):
    sc = jnp.where(valid, _dot_nt(qs, kt) - slope * relf, -1e30)
    m = jnp.maximum(jnp.max(sc, axis=1, keepdims=True), sink)
    e = jnp.where(valid, jnp.exp(sc - m), 0.0)
    es = jnp.exp(sink - m)
    inv = 1.0 / (jnp.sum(e, axis=1, keepdims=True) + es)
    return e * inv, es * inv


def swa_attn_fwd(qkn, proj, sinks, name):
    s = qkn.shape[0]
    b = SWA_BLOCK
    nb = s // b

    def body(sink_ref, q_ref, kc_ref, kp_ref, vc_ref, vp_ref, o_ref):
        n = pl.program_id(0)
        lane = lax.broadcasted_iota(jnp.int32, (1, LANES), 1)
        relf, valid = _swa_masks(n, 1)
        for tt in range(8):
            gp = tt // 4
            kt = jnp.concatenate([kp_ref[:, gp * 128:(gp + 1) * 128], kc_ref[:, gp * 128:(gp + 1) * 128]], axis=0)
            vt = jnp.concatenate([vp_ref[:, gp * 128:(gp + 1) * 128], vc_ref[:, gp * 128:(gp + 1) * 128]], axis=0)
            qt = q_ref[:, tt * 128:(tt + 1) * 128] * SWA_QSCALE
            outs = []
            for a in range(2):
                h = _swa_head(tt, a)
                hm = (lane < HALF) if a == 0 else (lane >= HALF)
                qm = jnp.where(hm, qt, jnp.zeros_like(qt))
                p, _ = _swa_probs(qm, kt, relf, valid, SWA_SLOPES[h], sink_ref[h])
                outs.append(_dot(p.astype(BF16), vt))
            o_ref[:, tt * 128:(tt + 1) * 128] = jnp.where(lane < HALF, outs[0], outs[1])

    cur = lambda w, c: pl.BlockSpec((b, w), lambda n: (n, c))
    prev = lambda w, c: pl.BlockSpec((b, w), lambda n: (jnp.maximum(n - 1, 0), c))
    return pl.pallas_call(
        body, name=name, grid=(nb,),
        in_specs=[pl.BlockSpec(memory_space=pltpu.SMEM), cur(1024, 0), cur(256, 4), prev(256, 4),
                  cur(256, 9), prev(256, 9)],
        out_specs=cur(1024, 0),
        out_shape=SDS((s, 1024), F32),
        compiler_params=_cparams(("parallel",)),
    )(sinks, qkn, qkn, qkn, proj, proj)


def swa_attn_bwd(qkn, proj, sinks, do, name):
    s = qkn.shape[0]
    b = SWA_BLOCK
    nb = s // b

    def body(sink_ref, q_ref, kc_ref, kp_ref, vc_ref, vp_ref, do_ref, dq_ref, dk_ref, dv_ref, ds_ref):
        n = pl.program_id(0)

        @pl.when(n == 0)
        def _():
            dk_ref[...] = jnp.zeros_like(dk_ref)
            dv_ref[...] = jnp.zeros_like(dv_ref)
            ds_ref[...] = jnp.zeros_like(ds_ref)

        lane = lax.broadcasted_iota(jnp.int32, (1, LANES), 1)
        relf, valid = _swa_masks(n, SWA_STACK)
        cur0 = pl.multiple_of(n * b, b)
        prev0 = pl.multiple_of(jnp.maximum(n - 1, 0) * b, b)
        dsink = jnp.zeros((1, LANES), F32)
        for gp in range(2):
            kt = jnp.concatenate([kp_ref[:, gp * 128:(gp + 1) * 128], kc_ref[:, gp * 128:(gp + 1) * 128]], axis=0)
            vt = jnp.concatenate([vp_ref[:, gp * 128:(gp + 1) * 128], vc_ref[:, gp * 128:(gp + 1) * 128]], axis=0)
            heads, slope, sink = _swa_head_cols(gp, sink_ref)
            qs = _swa_stack(q_ref, gp, lane, SWA_QSCALE)
            dos = _swa_stack(do_ref, gp, lane)
            p, ps = _swa_probs(qs, kt, relf, valid, slope, sink)
            dp = _dot_nt(dos, vt)
            delta = jnp.sum(p * dp, axis=1, keepdims=True)
            dz = (p * (dp - delta)).astype(BF16)
            sd = ps * delta
            for i, h in enumerate(heads):
                dsink = dsink + jnp.where(lane == h, -jnp.sum(sd[i * b:(i + 1) * b]), 0.0)
            for i, dq_tile in enumerate(_swa_unstack(_dot(dz, kt), lane)):
                tt = 4 * gp + i
                dq_ref[:, tt * 128:(tt + 1) * 128] = (dq_tile * SWA_QSCALE).astype(BF16)
            dkt = _dot_tn(dz, qs)
            dvt = _dot_tn(p.astype(BF16), dos)
            cols = slice(gp * 128, (gp + 1) * 128)
            dk_ref[pl.ds(prev0, b), cols] += dkt[:b]
            dk_ref[pl.ds(cur0, b), cols] += dkt[b:]
            dv_ref[pl.ds(prev0, b), cols] += dvt[:b]
            dv_ref[pl.ds(cur0, b), cols] += dvt[b:]
        ds_ref[0:1, :] += dsink

    cur = lambda w, c: pl.BlockSpec((b, w), lambda n: (n, c))
    prev = lambda w, c: pl.BlockSpec((b, w), lambda n: (jnp.maximum(n - 1, 0), c))
    full = pl.BlockSpec((s, 256), lambda n: (0, 0))
    return pl.pallas_call(
        body, name=name, grid=(nb,),
        in_specs=[pl.BlockSpec(memory_space=pltpu.SMEM), cur(1024, 0), cur(256, 4), prev(256, 4),
                  cur(256, 9), prev(256, 9), cur(1024, 0)],
        out_specs=[cur(1024, 0), full, full, pl.BlockSpec((8, LANES), lambda n: (0, 0))],
        out_shape=[SDS((s, 1024), BF16), SDS((s, 256), F32), SDS((s, 256), F32), SDS((8, LANES), F32)],
        compiler_params=_cparams(("arbitrary",)),
    )(sinks, qkn, qkn, qkn, proj, proj, do)


def _my_pos():
    return lax.axis_index("x"), lax.axis_index("y"), lax.axis_index("c")


ANY = pl.BlockSpec(memory_space=pl.ANY)


def all_gather_packed(xs, name):
    r, n = xs.shape

    def body(x_ref, out_ref, send_sems, recv_sems, local_sem):
        x, y, c = _my_pos()
        me, sibling = (x, y, c), (x, y, 1 - c)
        chips = [(1 - x, y), (x, 1 - y), (1 - x, 1 - y)]

        def slot(px, py, pc):
            return out_ref.at[4 * px + 2 * py + pc]

        def copy(k, block, to, src=None):
            return pltpu.make_async_remote_copy(
                src_ref=slot(*block) if src is None else src, dst_ref=slot(*block),
                send_sem=send_sems.at[k], recv_sem=recv_sems.at[k],
                device_id=to, device_id_type=MESH_ID)

        mine = pltpu.make_async_copy(x_ref, slot(*me), local_sem)
        mine.start()
        first = [copy(0, me, sibling, src=x_ref)]
        first += [copy(1 + j, me, (*chip, c), src=x_ref) for j, chip in enumerate(chips)]
        for cp in first:
            cp.start()
        passed = [copy(4 + j, (*chip, c), sibling) for j, chip in enumerate(chips)]
        for j, chip in enumerate(chips):
            copy(1 + j, (*chip, c), me).wait_recv()
            passed[j].start()
        copy(0, sibling, me).wait_recv()
        for j, chip in enumerate(chips):
            copy(4 + j, (*chip, 1 - c), me).wait_recv()
        for cp in first + passed:
            cp.wait_send()
        mine.wait()

    return pl.pallas_call(
        body, name=name,
        out_shape=SDS((N_DEV, r, n), xs.dtype),
        in_specs=[ANY], out_specs=ANY,
        scratch_shapes=[pltpu.SemaphoreType.DMA((7,)), pltpu.SemaphoreType.DMA((7,)), pltpu.SemaphoreType.DMA],
    )(xs)


def rs_sibling_exchange(g, name):
    _, r, n = g.shape

    def body(g_ref, ra_ref, send_sems, recv_sems):
        x, y, c = _my_pos()
        copies = [pltpu.make_async_remote_copy(
            src_ref=g_ref.at[2 * chip + (1 - c)], dst_ref=ra_ref.at[chip],
            send_sem=send_sems.at[chip], recv_sem=recv_sems.at[chip],
            device_id=(x, y, 1 - c), device_id_type=MESH_ID) for chip in range(4)]
        for cp in copies:
            cp.start()
        for cp in copies:
            cp.wait()

    return pl.pallas_call(
        body, name=name, out_shape=SDS((4, r, n), g.dtype), in_specs=[ANY], out_specs=ANY,
        scratch_shapes=[pltpu.SemaphoreType.DMA((4,)), pltpu.SemaphoreType.DMA((4,))],
    )(g)


def rs_pair_add(g, ra, cidx, name):
    _, r, n = g.shape
    tr = r // 8

    def body(c_ref, g_ref, ra_ref, p_ref):
        p_ref[...] = (g_ref[...].astype(F32) + ra_ref[...].astype(F32)).astype(p_ref.dtype)

    return pl.pallas_call(
        body, name=name,
        grid_spec=pltpu.PrefetchScalarGridSpec(
            num_scalar_prefetch=1, grid=(4, r // tr),
            in_specs=[pl.BlockSpec((1, tr, n), lambda i, j, c: (2 * i + c[0], j, 0)),
                      pl.BlockSpec((1, tr, n), lambda i, j, c: (i, j, 0))],
            out_specs=pl.BlockSpec((1, tr, n), lambda i, j, c: (i, j, 0))),
        out_shape=SDS((4, r, n), g.dtype),
        compiler_params=_cparams(("parallel", "parallel")),
    )(cidx, g, ra)


def rs_cross_exchange(p, name):
    _, r, n = p.shape

    def body(p_ref, rb_ref, send_sems, recv_sems):
        x, y, c = _my_pos()
        copies = []
        for k, (dx, dy) in enumerate(((1, 0), (0, 1), (1, 1))):
            tx = 1 - x if dx else x
            ty = 1 - y if dy else y
            copies.append(pltpu.make_async_remote_copy(
                src_ref=p_ref.at[2 * tx + ty], dst_ref=rb_ref.at[k],
                send_sem=send_sems.at[k], recv_sem=recv_sems.at[k],
                device_id=(tx, ty, c), device_id_type=MESH_ID))
        for cp in copies:
            cp.start()
        for cp in copies:
            cp.wait()

    return pl.pallas_call(
        body, name=name, out_shape=SDS((3, r, n), p.dtype), in_specs=[ANY], out_specs=ANY,
        scratch_shapes=[pltpu.SemaphoreType.DMA((3,)), pltpu.SemaphoreType.DMA((3,))],
    )(p)


def _adamw(w, g, m, v):
    m = ADAM_B1 * m + (1.0 - ADAM_B1) * g
    v = ADAM_B2 * v + (1.0 - ADAM_B2) * (g * g)
    m_hat = m / (1.0 - ADAM_B1 ** ADAM_STEP)
    v_hat = v / (1.0 - ADAM_B2 ** ADAM_STEP)
    delta = -ADAM_LR * (m_hat / (jnp.sqrt(v_hat) + ADAM_EPS) + ADAM_WD * w)
    return delta, m, v


def rs_final_adamw(p, rb, chipidx, w, m, v, name):
    _, r, n = p.shape
    tr = r // 8

    def body(c_ref, p_ref, rb_ref, w_ref, m_ref, v_ref, g_out, d_out, m_out, v_out):
        g = p_ref[0].astype(F32)
        for k in range(3):
            g = g + rb_ref[k].astype(F32)
        d, mn, vn = _adamw(w_ref[...], g, m_ref[...], v_ref[...])
        g_out[...] = g
        d_out[...] = d
        m_out[...] = mn
        v_out[...] = vn

    blk = pl.BlockSpec((tr, n), lambda j, c: (j, 0))
    return pl.pallas_call(
        body, name=name,
        grid_spec=pltpu.PrefetchScalarGridSpec(
            num_scalar_prefetch=1, grid=(r // tr,),
            in_specs=[pl.BlockSpec((1, tr, n), lambda j, c: (c[0], j, 0)),
                      pl.BlockSpec((3, tr, n), lambda j, c: (0, j, 0)), blk, blk, blk],
            out_specs=[blk, blk, blk, blk]),
        out_shape=[SDS((r, n), F32)] * 4,
        compiler_params=_cparams(("parallel",)),
    )(chipidx, p, rb, w, m, v)


def small_allreduce_adamw(g, w, m, v, name):
    rows = g.shape[0]

    def body(g_ref, w_ref, m_ref, v_ref, g_out, d_out, m_out, v_out, buf, send_sems, recv_sems):
        x, y, c = _my_pos()
        my = 4 * x + 2 * y + c
        buf[my] = g_ref[...]
        copies = []
        for k in range(1, N_DEV):
            dx, dy, dc = (k >> 2) & 1, (k >> 1) & 1, k & 1
            tgt = (1 - x if dx else x, 1 - y if dy else y, 1 - c if dc else c)
            copies.append(pltpu.make_async_remote_copy(
                src_ref=buf.at[my], dst_ref=buf.at[my],
                send_sem=send_sems.at[k - 1], recv_sem=recv_sems.at[k - 1],
                device_id=tgt, device_id_type=MESH_ID))
        for cp in copies:
            cp.start()
        for cp in copies:
            cp.wait()
        tot = buf[0]
        for j in range(1, N_DEV):
            tot = tot + buf[j]
        d, mn, vn = _adamw(w_ref[...], tot, m_ref[...], v_ref[...])
        g_out[...] = tot
        d_out[...] = d
        m_out[...] = mn
        v_out[...] = vn

    vm = pl.BlockSpec(memory_space=pltpu.VMEM)
    return pl.pallas_call(
        body, name=name, out_shape=[SDS((rows, LANES), F32)] * 4,
        in_specs=[vm] * 4, out_specs=[vm] * 4,
        scratch_shapes=[pltpu.VMEM((N_DEV, rows, LANES), F32),
                        pltpu.SemaphoreType.DMA((N_DEV - 1,)), pltpu.SemaphoreType.DMA((N_DEV - 1,))],
    )(g, w, m, v)


BIG = ["l0_w_in", "l0_w_out", "l1_w_in", "l1_w_uq", "l1_w_ukv", "l1_w_out", "l2_w_in", "l2_w_out",
       "l3_w_in", "l3_w_out"]
SMALL = ["l0_norm", "l1_norm", "l1_q_a_norm", "l1_kv_a_norm", "l1_q_head_norm", "l1_k_head_norm",
         "l2_norm", "l2_q_head_norm", "l2_k_head_norm", "l2_sinks", "l3_norm"]
ROW_SHARDED = {"l0_w_out", "l1_w_out", "l2_w_out", "l3_w_out"}

MLA_UQ_PERM = np.concatenate([np.arange(h * 192, h * 192 + 128) for h in range(8)]
                             + [np.arange(h * 192 + 128, (h + 1) * 192) for h in range(8)])
SWA_IN_PERM = np.concatenate([SWA_Q_PERM, 1536 + SWA_Q_PERM, np.arange(1024, 1536)])


def _inv(perm):
    inv = np.empty_like(perm)
    inv[perm] = np.arange(perm.size)
    return inv


def _pack(shards):
    parts = [shards[nm].reshape(-1, D_MODEL) for nm in BIG]
    used = sum(p.shape[0] for p in parts)
    assert used == ROWS_USED
    parts.append(jnp.zeros((ROWS_PACK - used, D_MODEL), parts[0].dtype))
    return jnp.concatenate(parts, axis=0)


def _unpack(packed, shapes):
    out, off = {}, 0
    for nm in BIG:
        shp = shapes[nm]
        rows = shp[0] * shp[1] // D_MODEL
        out[nm] = packed[off:off + rows].reshape(shp)
        off += rows
    return out


def _full_weights(gathered, shapes):
    out, off = {}, 0
    for nm in BIG:
        shp = shapes[nm]
        rows = shp[0] * shp[1] // D_MODEL
        blk = gathered[:, off:off + rows].reshape((N_DEV,) + shp)
        if nm in ROW_SHARDED:
            out[nm] = blk.reshape(N_DEV * shp[0], shp[1])
        else:
            out[nm] = jnp.transpose(blk, (1, 0, 2)).reshape(shp[0], N_DEV * shp[1])
        off += rows
    return out


def _grad_pieces(full_grads, shapes):
    parts = []
    for nm in BIG:
        shp = shapes[nm]
        g = full_grads[nm]
        if nm in ROW_SHARDED:
            blk = g.reshape((N_DEV,) + shp)
        else:
            blk = jnp.transpose(g.reshape(shp[0], N_DEV, shp[1]), (1, 0, 2))
        parts.append(blk.reshape(N_DEV, -1, D_MODEL).astype(BF16))
    parts.append(jnp.zeros((N_DEV, ROWS_PACK - ROWS_USED, D_MODEL), BF16))
    return jnp.concatenate(parts, axis=1)


def _pack_small(vals):
    flat = jnp.concatenate([vals[nm].reshape(-1).astype(F32) for nm in SMALL])
    flat = jnp.concatenate([flat, jnp.zeros((SMALL_ROWS * LANES - flat.shape[0],), F32)])
    return flat.reshape(SMALL_ROWS, LANES)


def _unpack_small(packed, shapes):
    flat = packed.reshape(-1)
    out, off = {}, 0
    for nm in SMALL:
        n = shapes[nm][0]
        out[nm] = flat[off:off + n]
        off += n
    return out


def _row(v):
    return v.reshape(1, -1).astype(F32)


def _mla_gain_rows(g):
    return jnp.concatenate([g[:128], g[128:], g[128:]]).reshape(1, 256).astype(F32)


def _layer_fwd_sb(x, norm, w_in, w_out, tag):
    proj, xn = norm_matmul(x, 0, _row(norm), w_in, f"{tag}_in")
    o = sb_attn_fwd(proj, f"{tag}_attn")
    y = gate_out_fwd(o, proj, 3, w_out, x, f"{tag}_out")
    return y, (x, xn, proj, o)


def _layer_bwd_sb(dy, saved, norm, w_in, w_out, tag):
    x, xn, proj, o = saved
    do, dgate, hg = gate_out_bwd(dy, w_out, o, proj, 3, f"{tag}_outb")
    dw_out = tn_matmul(hg, dy, f"{tag}_dwout")
    dq, dk, dv = sb_attn_bwd(proj, o, do, f"{tag}_attnb")
    dproj = jnp.concatenate([dq, dk, dv, dgate], axis=1)
    dw_in = tn_matmul(xn, dproj, f"{tag}_dwin")
    dx, dnorm = nt_norm_bwd(dproj, w_in, x, 0, _row(norm), dy, f"{tag}_inb", F32)
    return dx, dw_in, dw_out, dnorm.reshape(-1)


def kernel(x, l0_norm, l0_w_in, l0_w_out, l1_norm, l1_w_in, l1_q_a_norm, l1_w_uq, l1_kv_a_norm, l1_w_ukv, l1_q_head_norm, l1_k_head_norm, l1_w_out, l2_norm, l2_w_in, l2_q_head_norm, l2_k_head_norm, l2_sinks, l2_w_out, l3_norm, l3_w_in, l3_w_out, loss_target, m_l0_norm, m_l0_w_in, m_l0_w_out, m_l1_norm, m_l1_w_in, m_l1_q_a_norm, m_l1_w_uq, m_l1_kv_a_norm, m_l1_w_ukv, m_l1_q_head_norm, m_l1_k_head_norm, m_l1_w_out, m_l2_norm, m_l2_w_in, m_l2_q_head_norm, m_l2_k_head_norm, m_l2_sinks, m_l2_w_out, m_l3_norm, m_l3_w_in, m_l3_w_out, v_l0_norm, v_l0_w_in, v_l0_w_out, v_l1_norm, v_l1_w_in, v_l1_q_a_norm, v_l1_w_uq, v_l1_kv_a_norm, v_l1_w_ukv, v_l1_q_head_norm, v_l1_k_head_norm, v_l1_w_out, v_l2_norm, v_l2_w_in, v_l2_q_head_norm, v_l2_k_head_norm, v_l2_sinks, v_l2_w_out, v_l3_norm, v_l3_w_in, v_l3_w_out):
    loc = dict(locals())
    names = BIG + SMALL
    w = {nm: loc[nm] for nm in names}
    mom = {nm: loc["m_" + nm] for nm in names}
    vel = {nm: loc["v_" + nm] for nm in names}
    shapes = {nm: w[nm].shape for nm in names}
    xs = x[0]
    tgt = loss_target[0]

    gathered = all_gather_packed(_pack({nm: w[nm].astype(BF16) for nm in BIG}), "ag_weights")
    fw = _full_weights(gathered, shapes)
    loss_local, dx0, fg, sg = _local_step(xs, tgt, fw, {nm: w[nm] for nm in SMALL})
    loss = lax.psum(loss_local, ("x", "y", "c"))
    return _reduce_and_update(loss, dx0, fg, sg, w, mom, vel, shapes)


def _local_step(xs, tgt, fw, sm):
    s = xs.shape[0]
    l0_norm, l1_norm, l2_norm, l3_norm = sm["l0_norm"], sm["l1_norm"], sm["l2_norm"], sm["l3_norm"]
    l1_q_a_norm, l1_kv_a_norm = sm["l1_q_a_norm"], sm["l1_kv_a_norm"]
    l1_q_head_norm, l1_k_head_norm = sm["l1_q_head_norm"], sm["l1_k_head_norm"]
    l2_q_head_norm, l2_k_head_norm, l2_sinks = sm["l2_q_head_norm"], sm["l2_k_head_norm"], sm["l2_sinks"]
    w1_in = jnp.concatenate([fw["l1_w_in"][:, 448:], fw["l1_w_in"][:, :448], jnp.zeros((D_MODEL, 64), BF16)], axis=1)
    w1_uq = fw["l1_w_uq"][:, MLA_UQ_PERM]
    w2_in = fw["l2_w_in"][:, SWA_IN_PERM]
    w2_out = fw["l2_w_out"][SWA_Q_PERM, :]

    x1, sv0 = _layer_fwd_sb(xs, l0_norm, fw["l0_w_in"], fw["l0_w_out"], "l0")

    cos, sin = _rope_tables(s)
    gq, gk = _mla_gain_rows(l1_q_head_norm), _mla_gain_rows(l1_k_head_norm)
    proj1, xn1 = norm_matmul(x1, 0, _row(l1_norm), w1_in, "l1_in")
    qa, qln = norm_matmul(proj1, 4, _row(l1_q_a_norm), w1_uq, "l1_uq")
    kva, kvn = norm_matmul(proj1, 10, _row(l1_kv_a_norm), fw["l1_w_ukv"], "l1_ukv")
    q_p, k_p = mla_prep_fwd(qa, kva, proj1, gq, gk, cos, sin, "l1_prep")
    o1, lse1 = mla_attn_fwd(q_p, k_p, kva, "l1_attn")
    x2 = gate_out_fwd(o1, proj1, 0, fw["l1_w_out"], x1, "l1_out")

    g2 = jnp.stack([jnp.tile(l2_q_head_norm, 2), jnp.tile(l2_k_head_norm, 2)]).astype(F32)
    proj2, xn2 = norm_matmul(x2, 0, _row(l2_norm), w2_in, "l2_in")
    qkn2 = swa_prep_fwd(proj2, g2, "l2_prep")
    o2 = swa_attn_fwd(qkn2, proj2, l2_sinks, "l2_attn")
    x3 = gate_out_fwd(o2, proj2, 1, w2_out, x2, "l2_out")

    x4, sv3 = _layer_fwd_sb(x3, l3_norm, fw["l3_w_in"], fw["l3_w_out"], "l3")

    lossblk, dy = loss_kernel(x4, tgt, "loss")

    fg, sg = {}, {}
    dx3, fg["l3_w_in"], fg["l3_w_out"], sg["l3_norm"] = _layer_bwd_sb(dy, sv3, l3_norm, fw["l3_w_in"], fw["l3_w_out"], "l3")

    do2, dgate2, hg2 = gate_out_bwd(dx3, w2_out, o2, proj2, 1, "l2_outb")
    dw2_out = tn_matmul(hg2, dx3, "l2_dwout")
    dqn2, dkn2, dv2, dsink2 = swa_attn_bwd(qkn2, proj2, l2_sinks, do2, "l2_attnb")
    dqk2, dg2 = swa_prep_bwd(dqn2, dkn2, proj2, g2, "l2_prepb")
    dproj2 = jnp.concatenate([dqk2[:, :1024], dgate2, dqk2[:, 1024:], dv2.astype(BF16)], axis=1)
    dw2_in = tn_matmul(xn2, dproj2, "l2_dwin")
    dx2, dn2 = nt_norm_bwd(dproj2, w2_in, x2, 0, _row(l2_norm), dx3, "l2_inb", F32)
    fg["l2_w_in"] = dw2_in[:, _inv(SWA_IN_PERM)]
    fg["l2_w_out"] = dw2_out[_inv(SWA_Q_PERM), :]
    sg["l2_norm"] = dn2.reshape(-1)
    sg["l2_q_head_norm"] = dg2[0, :HALF]
    sg["l2_k_head_norm"] = dg2[1, :HALF]
    sg["l2_sinks"] = dsink2[0, :SWA_HEADS]

    do1, dgate1, hg1 = gate_out_bwd(dx2, fw["l1_w_out"], o1, proj1, 0, "l1_outb")
    fg["l1_w_out"] = tn_matmul(hg1, dx2, "l1_dwout")
    dq_p, dk_p, dv1 = mla_attn_bwd(q_p, k_p, kva, o1, do1, lse1, "l1_attnb")
    dqa, dkva, dkpe, dgq, dgk = mla_prep_bwd(dq_p, dk_p, dv1, qa, kva, proj1, gq, gk, cos, sin, "l1_prepb")
    dw_uq = tn_matmul(qln, dqa, "l1_dwuq")
    fg["l1_w_ukv"] = tn_matmul(kvn, dkva, "l1_dwukv")
    dqlat, dgqa = nt_norm_bwd(dqa, w1_uq, proj1, 4, _row(l1_q_a_norm), None, "l1_uqb", BF16)
    dkvlat, dgkva = nt_norm_bwd(dkva, fw["l1_w_ukv"], proj1, 10, _row(l1_kv_a_norm), None, "l1_ukvb", BF16)
    dproj1 = jnp.concatenate([dgate1, dqlat, dkvlat, dkpe], axis=1)
    dw1_in = tn_matmul(xn1, dproj1, "l1_dwin")
    dx1, dn1 = nt_norm_bwd(dproj1, w1_in, x1, 0, _row(l1_norm), dx2, "l1_inb", F32)
    fg["l1_w_in"] = jnp.concatenate([dw1_in[:, 1024:1472], dw1_in[:, :1024]], axis=1)
    fg["l1_w_uq"] = dw_uq[:, _inv(MLA_UQ_PERM)]
    sg["l1_norm"] = dn1.reshape(-1)
    sg["l1_q_a_norm"] = dgqa.reshape(-1)
    sg["l1_kv_a_norm"] = dgkva.reshape(-1)
    sg["l1_q_head_norm"] = jnp.concatenate([dgq[0, :128], dgq[0, 128:192] + dgq[0, 192:256]])
    sg["l1_k_head_norm"] = jnp.concatenate([dgk[0, :128], dgk[0, 128:192] + dgk[0, 192:256]])

    dx0, fg["l0_w_in"], fg["l0_w_out"], sg["l0_norm"] = _layer_bwd_sb(dx1, sv0, l0_norm, fw["l0_w_in"], fw["l0_w_out"], "l0")
    return lossblk[0, 0], dx0, fg, sg


def _reduce_and_update(loss, dx0, fg, sg, w, mom, vel, shapes):
    xi, yi, ci = _my_pos()
    cidx = jnp.reshape(ci, (1,)).astype(jnp.int32)
    chipidx = jnp.reshape(2 * xi + yi, (1,)).astype(jnp.int32)
    pieces = _grad_pieces(fg, shapes)
    ra = rs_sibling_exchange(pieces, "rs_sibling")
    part = rs_pair_add(pieces, ra, cidx, "rs_pair_add")
    rb = rs_cross_exchange(part, "rs_cross")
    gb, db, mb, vb = rs_final_adamw(part, rb, chipidx, _pack({nm: w[nm] for nm in BIG}),
                                    _pack({nm: mom[nm] for nm in BIG}), _pack({nm: vel[nm] for nm in BIG}), "rs_adamw")
    gs, ds, ms, vs = small_allreduce_adamw(_pack_small(sg), _pack_small({nm: w[nm] for nm in SMALL}),
                                           _pack_small({nm: mom[nm] for nm in SMALL}),
                                           _pack_small({nm: vel[nm] for nm in SMALL}), "small_allreduce")

    outs = []
    for big, small in ((gb, gs), (db, ds), (mb, ms), (vb, vs)):
        d = _unpack(big, shapes)
        d.update(_unpack_small(small, shapes))
        outs.append(d)
    order = ["l0_norm", "l0_w_in", "l0_w_out", "l1_norm", "l1_w_in", "l1_q_a_norm", "l1_w_uq", "l1_kv_a_norm",
             "l1_w_ukv", "l1_q_head_norm", "l1_k_head_norm", "l1_w_out", "l2_norm", "l2_w_in", "l2_q_head_norm",
             "l2_k_head_norm", "l2_sinks", "l2_w_out", "l3_norm", "l3_w_in", "l3_w_out"]
    flat = [loss, dx0[None]]
    for d in outs:
        flat += [d[nm] for nm in order]
    return tuple(flat)
```

```python
import math

import numpy as np
import jax
import jax.numpy as jnp
from jax import lax
from jax.experimental import pallas as pl
from jax.experimental.pallas import tpu as pltpu

F32 = jnp.float32
BF16 = jnp.bfloat16
SDS = jax.ShapeDtypeStruct
MESH_ID = pl.DeviceIdType.MESH

D_MODEL = 1024
NORM_EPS = 1e-6
N_DEV = 8
LANES = 128
HALF = 64
ROWS_PACK = 2176
ROWS_USED = 2120
PAIR_COLS = 512
OUT_ROW0 = 1024
REST_ROW0 = 1536
REST_ROWS = ROWS_PACK - REST_ROW0
SMALL_ROWS = 40

ADAM_LR = 0.001
ADAM_B1 = 0.9
ADAM_B2 = 0.999
ADAM_EPS = 1e-08
ADAM_WD = 0.01
ADAM_STEP = 10

NT_DIMS = (((1,), (1,)), ((), ()))
TN_DIMS = (((0,), (0,)), ((), ()))


def _cparams(sem=None, vmem_mb=48):
    return pltpu.CompilerParams(dimension_semantics=sem, vmem_limit_bytes=vmem_mb * 2 ** 20)


def _dot(a, b):
    return jnp.dot(a, b, preferred_element_type=F32)


def _dot_nt(a, b):
    return lax.dot_general(a, b, NT_DIMS, preferred_element_type=F32)


def _dot_tn(a, b):
    return lax.dot_general(a, b, TN_DIMS, preferred_element_type=F32)


def _split_dot(a, b):
    hi = a.astype(BF16)
    lo = (a - hi.astype(F32)).astype(BF16)
    return _dot(hi, b) + _dot(lo, b)


def _cumsum_dot(lf, u):
    return _split_dot(lf, u)


def _pick_tile(n, cap):
    return max(t for t in range(LANES, min(n, cap) + 1, LANES) if n % t == 0)


def _sigmoid(x):
    return 1.0 / (1.0 + jnp.exp(-x))


def norm_matmul(x, colblk, g, w, name, shard_col=None):
    s = x.shape[0]
    if shard_col is None:
        k, n = w.shape
        tn = _pick_tile(n, 1024)
        w_spec = pl.BlockSpec((k, tn), lambda i, j: (0, j))
    else:
        k, n, tn = D_MODEL, N_DEV * PAIR_COLS, 2 * PAIR_COLS
        w_spec = pl.BlockSpec((2, k, PAIR_COLS), lambda i, j: (j, 0, shard_col))
    tm = min(1024, s)
    assert s % tm == 0

    def body(x_ref, g_ref, w_ref, o_ref, xn_ref, xn_s):
        @pl.when(pl.program_id(1) == 0)
        def _():
            xf = x_ref[...].astype(F32)
            ms = jnp.mean(xf * xf, axis=-1, keepdims=True)
            y = (xf * lax.rsqrt(ms + NORM_EPS) * g_ref[...]).astype(BF16)
            xn_s[...] = y
            xn_ref[...] = y

        if shard_col is None:
            o_ref[...] = _dot(xn_s[...], w_ref[...]).astype(o_ref.dtype)
        else:
            for half in range(2):
                o_ref[:, half * PAIR_COLS:(half + 1) * PAIR_COLS] = _dot(xn_s[...], w_ref[half]).astype(o_ref.dtype)

    return pl.pallas_call(
        body, name=name, grid=(s // tm, n // tn),
        in_specs=[pl.BlockSpec((tm, k), lambda i, j: (i, colblk)),
                  pl.BlockSpec((1, k), lambda i, j: (0, 0)),
                  w_spec],
        out_specs=[pl.BlockSpec((tm, tn), lambda i, j: (i, j)),
                   pl.BlockSpec((tm, k), lambda i, j: (i, 0))],
        out_shape=[SDS((s, n), BF16), SDS((s, k), BF16)],
        scratch_shapes=[pltpu.VMEM((tm, k), BF16)],
        compiler_params=_cparams(("parallel", "arbitrary")),
    )(x, g, w)


def gate_out_fwd(o, proj, gate_blk, w_out, x, name):
    s = x.shape[0]
    d = D_MODEL
    tm = min(512, s)

    def body(o_ref, g_ref, w_ref, x_ref, y_ref):
        g = g_ref[...].astype(F32)
        hg = (o_ref[...].astype(F32) * (g * _sigmoid(g))).astype(BF16)
        y_ref[...] = x_ref[...] + _dot(hg, w_ref[...])

    return pl.pallas_call(
        body, name=name, grid=(s // tm,),
        in_specs=[pl.BlockSpec((tm, d), lambda i: (i, 0)),
                  pl.BlockSpec((tm, d), lambda i: (i, gate_blk)),
                  pl.BlockSpec((d, d), lambda i: (0, 0)),
                  pl.BlockSpec((tm, d), lambda i: (i, 0))],
        out_specs=pl.BlockSpec((tm, d), lambda i: (i, 0)),
        out_shape=SDS((s, d), F32),
        compiler_params=_cparams(("parallel",)),
    )(o, proj, w_out, x)


def gate_out_bwd(dy, w_out, o, proj, gate_blk, name):
    s = dy.shape[0]
    d = D_MODEL
    tm = min(512, s)

    def body(dy_ref, w_ref, o_ref, g_ref, do_ref, dg_ref, hg_ref):
        dh = _dot_nt(dy_ref[...].astype(BF16), w_ref[...])
        g = g_ref[...].astype(F32)
        o = o_ref[...].astype(F32)
        sg = _sigmoid(g)
        silu = g * sg
        do_ref[...] = (dh * silu).astype(BF16)
        dg_ref[...] = (dh * o * (sg * (1.0 + g * (1.0 - sg)))).astype(BF16)
        hg_ref[...] = (o * silu).astype(BF16)

    blk = pl.BlockSpec((tm, d), lambda i: (i, 0))
    return pl.pallas_call(
        body, name=name, grid=(s // tm,),
        in_specs=[blk, pl.BlockSpec((d, d), lambda i: (0, 0)), blk,
                  pl.BlockSpec((tm, d), lambda i: (i, gate_blk))],
        out_specs=[blk, blk, blk],
        out_shape=[SDS((s, d), BF16)] * 3,
        compiler_params=_cparams(("parallel",)),
    )(dy, w_out, o, proj)


def tn_matmul(a, b, name):
    s, k = a.shape
    n = b.shape[1]
    tk = min(1024, k)
    tn = _pick_tile(n, 512)
    ts = min(2048, s)
    assert k % tk == 0 and s % ts == 0

    def body(a_ref, b_ref, o_ref):
        @pl.when(pl.program_id(2) == 0)
        def _():
            o_ref[...] = jnp.zeros_like(o_ref)

        o_ref[...] += _dot_tn(a_ref[...].astype(BF16), b_ref[...].astype(BF16))

    return pl.pallas_call(
        body, name=name, grid=(k // tk, n // tn, s // ts),
        in_specs=[pl.BlockSpec((ts, tk), lambda i, j, r: (r, i)),
                  pl.BlockSpec((ts, tn), lambda i, j, r: (r, j))],
        out_specs=pl.BlockSpec((tk, tn), lambda i, j, r: (i, j)),
        out_shape=SDS((k, n), F32),
        compiler_params=_cparams(("parallel", "parallel", "arbitrary")),
    )(a, b)


def tn_matmul_shards(a, b, name):
    s, k = a.shape
    n = b.shape[1]
    assert n == N_DEV * PAIR_COLS and a.dtype == BF16 and b.dtype == BF16

    def body(a_ref, b_ref, o_ref):
        o_ref[...] = _dot_tn(a_ref[...], b_ref[...]).astype(BF16)

    return pl.pallas_call(
        body, name=name, grid=(N_DEV,),
        in_specs=[pl.BlockSpec((s, k), lambda j: (0, 0)),
                  pl.BlockSpec((s, PAIR_COLS), lambda j: (0, j))],
        out_specs=pl.BlockSpec((None, k, PAIR_COLS), lambda j: (j, 0, 0)),
        out_shape=SDS((N_DEV, k, PAIR_COLS), BF16),
        compiler_params=_cparams(("parallel",)),
    )(a, b)


def nt_norm_bwd(dp, w, x, xcol, g, res, name, out_dtype, shard_col=None):
    s, n = dp.shape
    if shard_col is None:
        k = w.shape[0]
        w_spec = pl.BlockSpec((k, n), lambda i: (0, 0))
    else:
        k = D_MODEL
        w_spec = pl.BlockSpec((N_DEV, k, PAIR_COLS), lambda i: (0, 0, shard_col))
    tm = min(512, s)
    assert s % tm == 0
    has_res = res is not None

    def body(*refs):
        if has_res:
            dp_ref, w_ref, x_ref, g_ref, r_ref, dx_ref, dg_ref = refs
        else:
            dp_ref, w_ref, x_ref, g_ref, dx_ref, dg_ref = refs

        @pl.when(pl.program_id(0) == 0)
        def _():
            dg_ref[...] = jnp.zeros_like(dg_ref)

        if shard_col is None:
            dxn = _dot_nt(dp_ref[...], w_ref[...])
        else:
            dxn = _dot_nt(dp_ref[:, 0:PAIR_COLS], w_ref[0])
            for j in range(1, N_DEV):
                dxn = dxn + _dot_nt(dp_ref[:, j * PAIR_COLS:(j + 1) * PAIR_COLS], w_ref[j])
        xf = x_ref[...].astype(F32)
        rstd = lax.rsqrt(jnp.mean(xf * xf, axis=-1, keepdims=True) + NORM_EPS)
        xhat = xf * rstd
        dg_ref[...] += jnp.sum(dxn * xhat, axis=0, keepdims=True)
        dxh = dxn * g_ref[...]
        dx = rstd * (dxh - xhat * jnp.mean(dxh * xhat, axis=-1, keepdims=True))
        if has_res:
            dx = dx + r_ref[...]
        dx_ref[...] = dx.astype(out_dtype)

    in_specs = [pl.BlockSpec((tm, n), lambda i: (i, 0)),
                w_spec,
                pl.BlockSpec((tm, k), lambda i: (i, xcol)),
                pl.BlockSpec((1, k), lambda i: (0, 0))]
    args = [dp, w, x, g]
    if has_res:
        in_specs.append(pl.BlockSpec((tm, k), lambda i: (i, 0)))
        args.append(res)
    return pl.pallas_call(
        body, name=name, grid=(s // tm,),
        in_specs=in_specs,
        out_specs=[pl.BlockSpec((tm, k), lambda i: (i, 0)),
                   pl.BlockSpec((1, k), lambda i: (0, 0))],
        out_shape=[SDS((s, k), out_dtype), SDS((1, k), F32)],
        compiler_params=_cparams(("arbitrary",)),
    )(*args)


def loss_kernel(y, target, name):
    s, d = y.shape
    tm = min(512, s)

    def body(y_ref, t_ref, l_ref, dy_ref):
        @pl.when(pl.program_id(0) == 0)
        def _():
            l_ref[...] = jnp.zeros_like(l_ref)

        e = y_ref[...] - t_ref[...]
        dy_ref[...] = e * (1.0 / d)
        l_ref[...] += 0.5 * jnp.sum(jnp.mean(e * e, axis=-1, keepdims=True))

    blk = pl.BlockSpec((tm, d), lambda i: (i, 0))
    return pl.pallas_call(
        body, name=name, grid=(s // tm,),
        in_specs=[blk, blk],
        out_specs=[pl.BlockSpec((8, LANES), lambda i: (0, 0)), blk],
        out_shape=[SDS((8, LANES), F32), SDS((s, d), F32)],
        compiler_params=_cparams(("arbitrary",)),
    )(y, target)


SB_LOG_CUTOFF = -80.0


def _sb_sweep(qi, c, rest, step):
    def cond(st):
        return (st[0] <= qi) & (st[1] > SB_LOG_CUTOFF)

    def body(st):
        out = tuple(step(st[0], st[2], st[3:]))
        return (st[0] + 1, jnp.max(out[0])) + out

    return lax.while_loop(cond, body, (jnp.int32(1), jnp.max(c), c) + tuple(rest))[3:]


def _tile_masks(t):
    row = lax.broadcasted_iota(jnp.int32, (t, t), 0)
    col = lax.broadcasted_iota(jnp.int32, (t, t), 1)
    return row, col


def sb_attn_fwd(proj, name):
    s = proj.shape[0]
    t = min(256, s)
    nq = s // t
    scale = 1.0 / math.sqrt(HALF)

    def body(q_ref, k_ref, v_ref, o_ref):
        lane = lax.broadcasted_iota(jnp.int32, (1, LANES), 1)
        row, col = _tile_masks(t)
        u_incl = (row >= col).astype(BF16)
        tri = jnp.concatenate([col < row] * 2, axis=0)

        def scores(qs, kj):
            k = k_ref[pl.ds(pl.multiple_of(kj * t, t), t), :]
            z = _dot_nt(qs, k)
            nz = -z
            lf = jnp.minimum(nz, 0.0) - jnp.log(1.0 + jnp.exp(jnp.minimum(z, nz)))
            return z, lf

        def accumulate(kj, z, lf, c, acc, diag):
            v = v_ref[pl.ds(pl.multiple_of(kj * t, t), t), :]
            if diag:
                lf = jnp.where(tri, lf, 0.0)
            incl = _cumsum_dot(lf, u_incl) + c
            a = jnp.exp(z + incl)
            if diag:
                a = jnp.where(tri, a, 0.0)
            acc = acc + _dot(a.astype(BF16), v)
            c = c + jnp.sum(lf, axis=1, keepdims=True)
            return c, acc

        def q_body(qi, carry):
            q0 = pl.multiple_of(qi * t, t)
            q = q_ref[pl.ds(q0, t), :] * scale
            zq = jnp.zeros_like(q)
            qs = jnp.concatenate([jnp.where(lane < HALF, q, zq), jnp.where(lane >= HALF, q, zq)], axis=0)

            def step(jj, cr, diag=False):
                z, lf = scores(qs, qi - jj)
                return accumulate(qi - jj, z, lf, cr[0], cr[1], diag)

            c, acc = step(0, (jnp.zeros((2 * t, 1), F32), jnp.zeros((2 * t, LANES), F32)), True)
            acc = _sb_sweep(qi, c, (acc,), lambda jj, c, rest: step(jj, (c, rest[0])))[0]
            o_ref[pl.ds(q0, t), :] = jnp.where(lane < HALF, acc[:t], acc[t:])
            return carry

        lax.fori_loop(0, nq, q_body, 0)

    return pl.pallas_call(
        body, name=name, grid=(8,),
        in_specs=[pl.BlockSpec((s, LANES), lambda p: (0, p)),
                  pl.BlockSpec((s, LANES), lambda p: (0, 8 + p)),
                  pl.BlockSpec((s, LANES), lambda p: (0, 16 + p))],
        out_specs=pl.BlockSpec((s, LANES), lambda p: (0, p)),
        out_shape=SDS((s, D_MODEL), F32),
        compiler_params=_cparams(("parallel",)),
    )(proj, proj, proj)


def sb_attn_bwd(proj, o, do, name):
    s = proj.shape[0]
    t = min(256, s)
    nq = s // t
    scale = 1.0 / math.sqrt(HALF)

    def body(q_ref, k_ref, v_ref, o_ref, do_ref, dq_ref, dk_ref, dv_ref, dk_s, dv_s):
        dk_s[...] = jnp.zeros_like(dk_s)
        dv_s[...] = jnp.zeros_like(dv_s)
        lane = lax.broadcasted_iota(jnp.int32, (1, LANES), 1)
        row, col = _tile_masks(t)
        u_incl = (row >= col).astype(BF16)
        u_excl = (row > col).astype(BF16)
        tri = jnp.concatenate([col < row] * 2, axis=0)

        def scores(qs, dos, kj):
            k0 = pl.multiple_of(kj * t, t)
            z = _dot_nt(qs, k_ref[pl.ds(k0, t), :])
            nz = -z
            lf = jnp.minimum(nz, 0.0) - jnp.log(1.0 + jnp.exp(jnp.minimum(z, nz)))
            return z, lf, _dot_nt(dos, v_ref[pl.ds(k0, t), :])

        def accumulate(qs, dos, tsum, kj, z, lf, da, c, r, dq, diag):
            k0 = pl.multiple_of(kj * t, t)
            sig = jnp.exp(z + lf)
            if diag:
                lf = jnp.where(tri, lf, 0.0)
            incl = _cumsum_dot(lf, u_incl) + c
            a = jnp.exp(z + incl)
            if diag:
                a = jnp.where(tri, a, 0.0)
            ab = a.astype(BF16)
            dl = ab.astype(F32) * da
            pre = tsum - r - _split_dot(dl, u_excl)
            dz = dl - sig * pre
            if diag:
                dz = jnp.where(tri, dz, 0.0)
            dzb = dz.astype(BF16)
            dq = dq + _dot(dzb, k_ref[pl.ds(k0, t), :])
            dk_s[pl.ds(k0, t), :] += _dot_tn(dzb, qs)
            dv_s[pl.ds(k0, t), :] += _dot_tn(ab, dos)
            c = c + jnp.sum(lf, axis=1, keepdims=True)
            r = r + jnp.sum(dl, axis=1, keepdims=True)
            return c, r, dq

        def q_body(qi, carry):
            q0 = pl.multiple_of(qi * t, t)
            q = q_ref[pl.ds(q0, t), :] * scale
            dob = do_ref[pl.ds(q0, t), :]
            prod = dob.astype(F32) * o_ref[pl.ds(q0, t), :]
            zb = jnp.zeros_like(q)
            lo, hi = lane < HALF, lane >= HALF
            qs = jnp.concatenate([jnp.where(lo, q, zb), jnp.where(hi, q, zb)], axis=0)
            dos = jnp.concatenate([jnp.where(lo, dob, zb), jnp.where(hi, dob, zb)], axis=0)
            tsum = jnp.concatenate([jnp.sum(jnp.where(lo, prod, 0.0), axis=1, keepdims=True),
                                    jnp.sum(jnp.where(hi, prod, 0.0), axis=1, keepdims=True)], axis=0)
            zc = jnp.zeros((2 * t, 1), F32)

            def step(jj, cr, diag=False):
                z, lf, da = scores(qs, dos, qi - jj)
                return accumulate(qs, dos, tsum, qi - jj, z, lf, da, cr[0], cr[1], cr[2], diag)

            c, r, dq = step(0, (zc, zc, jnp.zeros((2 * t, LANES), F32)), True)
            dq = _sb_sweep(qi, c, (r, dq), lambda jj, c, rest: step(jj, (c,) + tuple(rest)))[1]
            dq_ref[pl.ds(q0, t), :] = (jnp.where(lo, dq[:t], dq[t:]) * scale).astype(BF16)
            return carry

        lax.fori_loop(0, nq, q_body, 0)
        dk_ref[...] = dk_s[...].astype(BF16)
        dv_ref[...] = dv_s[...].astype(BF16)

    blk = lambda off: pl.BlockSpec((s, LANES), lambda p: (0, off + p))
    return pl.pallas_call(
        body, name=name, grid=(8,),
        in_specs=[blk(0), blk(8), blk(16), blk(0), blk(0)],
        out_specs=[blk(0), blk(0), blk(0)],
        out_shape=[SDS((s, D_MODEL), BF16)] * 3,
        scratch_shapes=[pltpu.VMEM((s, LANES), F32), pltpu.VMEM((s, LANES), F32)],
        compiler_params=_cparams(("parallel",)),
    )(proj, proj, proj, o, do)


MLA_HEADS = 8
MLA_QK = 192
MLA_SCALE = 1.0 / math.sqrt(MLA_QK)
MLA_TK = 512


def _rope_tables(s):
    inv_freq = 10000.0 ** (-jnp.arange(32, dtype=F32) / 32)
    ang = jnp.arange(s, dtype=F32)[:, None] * inv_freq[None, :]
    cos = jnp.tile(jnp.cos(ang), (1, 4))
    sin = jnp.sin(ang)
    sin_signed = jnp.tile(jnp.concatenate([-sin, sin], axis=1), (1, 2))
    return cos, sin_signed


def _partner(u, lane):
    return jnp.where((lane % HALF) < 32, pltpu.roll(u, 96, 1), pltpu.roll(u, 32, 1))


def _mla_head_inputs(h, qa_ref, kva_ref, kpe, lane):
    a = h % 2
    hm = (lane < HALF) if a == 0 else (lane >= HALF)
    qn = qa_ref[:, h * 128:(h + 1) * 128].astype(F32)
    qr = jnp.where(hm, qa_ref[:, 1024 + (h // 2) * 128:1024 + (h // 2 + 1) * 128].astype(F32), 0.0)
    kn = kva_ref[:, h * 256:h * 256 + 128].astype(F32)
    kr = kpe if a == 0 else pltpu.roll(kpe, HALF, 1)
    return qn, qr, kn, kr


def mla_prep_fwd(qa, kva, proj, gq, gk, cos, sin, name):
    s = qa.shape[0]
    tm = min(256, s)

    def body(qa_ref, kva_ref, kpe_ref, gq_ref, gk_ref, cos_ref, sin_ref, q_out, k_out):
        lane = lax.broadcasted_iota(jnp.int32, (1, LANES), 1)
        kpe = kpe_ref[...].astype(F32)
        cs, sn = cos_ref[...], sin_ref[...]
        for h in range(MLA_HEADS):
            qn, qr, kn, kr = _mla_head_inputs(h, qa_ref, kva_ref, kpe, lane)
            for xn, xr, g_ref, out, sc in ((qn, qr, gq_ref, q_out, MLA_SCALE), (kn, kr, gk_ref, k_out, 1.0)):
                ss = jnp.sum(xn * xn, axis=1, keepdims=True) + jnp.sum(xr * xr, axis=1, keepdims=True)
                rstd = lax.rsqrt(ss * (1.0 / MLA_QK) + NORM_EPS) * sc
                yn = xn * rstd * g_ref[:, 0:128]
                ur = xr * rstd * g_ref[:, 128:256]
                yr = ur * cs + _partner(ur, lane) * sn
                out[:, h * 256:h * 256 + 128] = yn.astype(BF16)
                out[:, h * 256 + 128:h * 256 + 256] = yr.astype(BF16)

    row = lambda w, c: pl.BlockSpec((tm, w), lambda i: (i, c))
    full = lambda w: pl.BlockSpec((1, w), lambda i: (0, 0))
    return pl.pallas_call(
        body, name=name, grid=(s // tm,),
        in_specs=[row(1536, 0), row(2048, 0), row(128, 11), full(256), full(256), row(128, 0), row(128, 0)],
        out_specs=[row(2048, 0), row(2048, 0)],
        out_shape=[SDS((s, 2048), BF16)] * 2,
        compiler_params=_cparams(("parallel",)),
    )(qa, kva, proj, gq, gk, cos, sin)


def mla_prep_bwd(dq_p, dk_p, dv, qa, kva, proj, gq, gk, cos, sin, name):
    s = qa.shape[0]
    tm = min(256, s)

    def body(dq_ref, dk_ref, dv_ref, qa_ref, kva_ref, kpe_ref, gq_ref, gk_ref, cos_ref, sin_ref,
             dqa_ref, dkva_ref, dkpe_ref, dgq_ref, dgk_ref):
        @pl.when(pl.program_id(0) == 0)
        def _():
            dgq_ref[...] = jnp.zeros_like(dgq_ref)
            dgk_ref[...] = jnp.zeros_like(dgk_ref)

        lane = lax.broadcasted_iota(jnp.int32, (1, LANES), 1)
        kpe = kpe_ref[...].astype(F32)
        cs, sn = cos_ref[...], sin_ref[...]
        dkpe = jnp.zeros((tm, LANES), F32)
        dqr_pair = None
        for h in range(MLA_HEADS):
            qn, qr, kn, kr = _mla_head_inputs(h, qa_ref, kva_ref, kpe, lane)
            res = []
            for xn, xr, g_ref, d_ref, dg_ref, sc in ((qn, qr, gq_ref, dq_ref, dgq_ref, MLA_SCALE),
                                                     (kn, kr, gk_ref, dk_ref, dgk_ref, 1.0)):
                ss = jnp.sum(xn * xn, axis=1, keepdims=True) + jnp.sum(xr * xr, axis=1, keepdims=True)
                rstd = lax.rsqrt(ss * (1.0 / MLA_QK) + NORM_EPS)
                hn, hr = xn * rstd, xr * rstd
                dyn = d_ref[:, h * 256:h * 256 + 128] * sc
                dyr = d_ref[:, h * 256 + 128:h * 256 + 256] * sc
                dur = dyr * cs - _partner(dyr, lane) * sn
                dg_ref[:, 0:128] += jnp.sum(dyn * hn, axis=0, keepdims=True)
                dg_ref[:, 128:256] += jnp.sum(dur * hr, axis=0, keepdims=True)
                dhn = dyn * g_ref[:, 0:128]
                dhr = dur * g_ref[:, 128:256]
                mu = (jnp.sum(dhn * hn, axis=1, keepdims=True)
                      + jnp.sum(dhr * hr, axis=1, keepdims=True)) * (1.0 / MLA_QK)
                res.append((rstd * (dhn - hn * mu), rstd * (dhr - hr * mu)))
            (dqn, dqr), (dkn, dkr) = res
            dqa_ref[:, h * 128:(h + 1) * 128] = dqn.astype(BF16)
            if h % 2 == 0:
                dqr_pair = dqr
            else:
                dqa_ref[:, 1024 + (h // 2) * 128:1024 + (h // 2 + 1) * 128] = (dqr_pair + dqr).astype(BF16)
            dkva_ref[:, h * 256:h * 256 + 128] = dkn.astype(BF16)
            dkva_ref[:, h * 256 + 128:h * 256 + 256] = dv_ref[:, h * 128:(h + 1) * 128]
            dkpe = dkpe + (dkr if h % 2 == 0 else pltpu.roll(dkr, HALF, 1))
        dkpe_ref[...] = dkpe.astype(BF16)

    row = lambda w, c: pl.BlockSpec((tm, w), lambda i: (i, c))
    full = lambda w: pl.BlockSpec((1, w), lambda i: (0, 0))
    return pl.pallas_call(
        body, name=name, grid=(s // tm,),
        in_specs=[row(2048, 0), row(2048, 0), row(1024, 0), row(1536, 0), row(2048, 0), row(128, 11),
                  full(256), full(256), row(128, 0), row(128, 0)],
        out_specs=[row(1536, 0), row(2048, 0), row(128, 0), full(256), full(256)],
        out_shape=[SDS((s, 1536), BF16), SDS((s, 2048), BF16), SDS((s, 128), BF16),
                   SDS((1, 256), F32), SDS((1, 256), F32)],
        compiler_params=_cparams(("arbitrary",)),
    )(dq_p, dk_p, dv, qa, kva, proj, gq, gk, cos, sin)


def mla_attn_fwd(q_p, k_p, kva, name):
    s = q_p.shape[0]
    tk = min(MLA_TK, s)
    tq = min(512, s)
    r = tq // tk
    nq = s // tq

    def body(q_ref, k_ref, v_ref, o_ref, lse_ref):
        row = lax.broadcasted_iota(jnp.int32, (tq, tk), 0)
        col = lax.broadcasted_iota(jnp.int32, (tq, tk), 1)

        def tile(q, kj, m, l, acc, mask):
            k0 = pl.multiple_of(kj * tk, tk)
            sc = _dot_nt(q, k_ref[pl.ds(k0, tk), :])
            if mask is not None:
                sc = jnp.where(mask, sc, -1e30)
            m_new = jnp.maximum(m, jnp.max(sc, axis=1, keepdims=True))
            alpha = jnp.exp(m - m_new)
            p = jnp.exp(sc - m_new)
            l = alpha * l + jnp.sum(p, axis=1, keepdims=True)
            acc = alpha * acc + _dot(p.astype(BF16), v_ref[pl.ds(k0, tk), :])
            return m_new, l, acc

        def q_body(qi, carry):
            q0 = pl.multiple_of(qi * tq, tq)
            q = q_ref[pl.ds(q0, tq), :]
            cr = (jnp.full((tq, 1), -1e30, F32), jnp.zeros((tq, 1), F32), jnp.zeros((tq, LANES), F32))
            cr = lax.fori_loop(0, qi * r, lambda kj, cr: tile(q, kj, cr[0], cr[1], cr[2], None), cr)
            for d in range(r):
                cr = tile(q, qi * r + d, cr[0], cr[1], cr[2], col + d * tk <= row)
            m, l, acc = cr
            o_ref[pl.ds(q0, tq), :] = acc / l
            lse_ref[pl.ds(q0, tq), :] = jnp.broadcast_to(m + jnp.log(l), (tq, LANES))
            return carry

        lax.fori_loop(0, nq, q_body, 0)

    return pl.pallas_call(
        body, name=name, grid=(MLA_HEADS,),
        in_specs=[pl.BlockSpec((s, 256), lambda h: (0, h)),
                  pl.BlockSpec((s, 256), lambda h: (0, h)),
                  pl.BlockSpec((s, 128), lambda h: (0, 2 * h + 1))],
        out_specs=[pl.BlockSpec((s, 128), lambda h: (0, h)),
                   pl.BlockSpec((s, 128), lambda h: (0, h))],
        out_shape=[SDS((s, 1024), F32), SDS((s, 1024), F32)],
        compiler_params=_cparams(("parallel",)),
    )(q_p, k_p, kva)


def mla_attn_bwd(q_p, k_p, kva, o, do, lse, name):
    s = q_p.shape[0]
    tk = min(MLA_TK, s)
    tq = min(512, s)
    r = tq // tk
    nq = s // tq

    def body(q_ref, k_ref, v_ref, o_ref, do_ref, lse_ref, dq_ref, dk_ref, dv_ref, dv_s):
        dk_ref[...] = jnp.zeros_like(dk_ref)
        dv_s[...] = jnp.zeros_like(dv_s)
        row = lax.broadcasted_iota(jnp.int32, (tq, tk), 0)
        col = lax.broadcasted_iota(jnp.int32, (tq, tk), 1)

        def tile(q, dob, lse, delta, kj, dq, mask):
            k0 = pl.multiple_of(kj * tk, tk)
            k = k_ref[pl.ds(k0, tk), :]
            v = v_ref[pl.ds(k0, tk), :]
            p = jnp.exp(_dot_nt(q, k) - lse)
            if mask is not None:
                p = jnp.where(mask, p, 0.0)
            ds = (p * (_dot_nt(dob, v) - delta)).astype(BF16)
            dq = dq + _dot(ds, k)
            dk_ref[pl.ds(k0, tk), :] += _dot_tn(ds, q)
            dv_s[pl.ds(k0, tk), :] += _dot_tn(p.astype(BF16), dob)
            return dq

        def q_body(qi, carry):
            q0 = pl.multiple_of(qi * tq, tq)
            q = q_ref[pl.ds(q0, tq), :]
            dob = do_ref[pl.ds(q0, tq), :]
            lse = lse_ref[pl.ds(q0, tq), 0:1]
            delta = jnp.sum(dob.astype(F32) * o_ref[pl.ds(q0, tq), :], axis=1, keepdims=True)
            dq = lax.fori_loop(0, qi * r, lambda kj, dq: tile(q, dob, lse, delta, kj, dq, None),
                               jnp.zeros((tq, 256), F32))
            for d in range(r):
                dq = tile(q, dob, lse, delta, qi * r + d, dq, col + d * tk <= row)
            dq_ref[pl.ds(q0, tq), :] = dq
            return carry

        lax.fori_loop(0, nq, q_body, 0)
        dv_ref[...] = dv_s[...].astype(BF16)

    b256 = pl.BlockSpec((s, 256), lambda h: (0, h))
    b128 = pl.BlockSpec((s, 128), lambda h: (0, h))
    return pl.pallas_call(
        body, name=name, grid=(MLA_HEADS,),
        in_specs=[b256, b256, pl.BlockSpec((s, 128), lambda h: (0, 2 * h + 1)), b128, b128, b128],
        out_specs=[b256, b256, b128],
        out_shape=[SDS((s, 2048), F32), SDS((s, 2048), F32), SDS((s, 1024), BF16)],
        scratch_shapes=[pltpu.VMEM((s, 128), F32)],
        compiler_params=_cparams(("parallel",), vmem_mb=56),
    )(q_p, k_p, kva, o, do, lse)


SWA_HEADS = 16
SWA_BLOCK = 128


def _swa_head(tile_idx, half):
    return (2 * (tile_idx // 4) + half) * 4 + tile_idx % 4


SWA_Q_PERM = np.concatenate([np.arange(_swa_head(tt, a) * HALF, (_swa_head(tt, a) + 1) * HALF)
                             for tt in range(8) for a in range(2)])
SWA_SLOPES = [2.0 ** (-8.0 * (h + 1) / SWA_HEADS) for h in range(SWA_HEADS)]


def swa_prep_fwd(proj, g128, name):
    s = proj.shape[0]
    tm = min(512, s)

    def body(q_ref, k_ref, g_ref, o_ref):
        lane = lax.broadcasted_iota(jnp.int32, (1, LANES), 1)
        lo = lane < HALF
        for tt in range(10):
            if tt < 8:
                x = q_ref[:, tt * 128:(tt + 1) * 128].astype(F32)
            else:
                x = k_ref[:, (tt - 8) * 128:(tt - 7) * 128].astype(F32)
            sq = x * x
            sa = jnp.sum(jnp.where(lo, sq, 0.0), axis=1, keepdims=True)
            sb = jnp.sum(sq, axis=1, keepdims=True) - sa
            rstd = jnp.where(lo, lax.rsqrt(sa * (1.0 / HALF) + NORM_EPS), lax.rsqrt(sb * (1.0 / HALF) + NORM_EPS))
            gi = 0 if tt < 8 else 1
            o_ref[:, tt * 128:(tt + 1) * 128] = (x * rstd * g_ref[gi:gi + 1, :]).astype(BF16)

    return pl.pallas_call(
        body, name=name, grid=(s // tm,),
        in_specs=[pl.BlockSpec((tm, 1024), lambda i: (i, 0)), pl.BlockSpec((tm, 256), lambda i: (i, 8)),
                  pl.BlockSpec((2, 128), lambda i: (0, 0))],
        out_specs=pl.BlockSpec((tm, 1280), lambda i: (i, 0)),
        out_shape=SDS((s, 1280), BF16),
        compiler_params=_cparams(("parallel",)),
    )(proj, proj, g128)


def swa_prep_bwd(dqn, dkn, proj, g128, name):
    s = proj.shape[0]
    tm = min(512, s)
    nsteps = s // tm

    def body(dq_ref, dk_ref, xq_ref, xk_ref, g_ref, o_ref, dg_ref):
        @pl.when(pl.program_id(0) == 0)
        def _():
            dg_ref[...] = jnp.zeros_like(dg_ref)

        lane = lax.broadcasted_iota(jnp.int32, (1, LANES), 1)
        lo = lane < HALF
        for tt in range(10):
            if tt < 8:
                x = xq_ref[:, tt * 128:(tt + 1) * 128].astype(F32)
                dy = dq_ref[:, tt * 128:(tt + 1) * 128].astype(F32)
            else:
                x = xk_ref[:, (tt - 8) * 128:(tt - 7) * 128].astype(F32)
                dy = dk_ref[:, (tt - 8) * 128:(tt - 7) * 128].astype(F32)
            gi = 0 if tt < 8 else 1
            sq = x * x
            sa = jnp.sum(jnp.where(lo, sq, 0.0), axis=1, keepdims=True)
            sb = jnp.sum(sq, axis=1, keepdims=True) - sa
            rstd = jnp.where(lo, lax.rsqrt(sa * (1.0 / HALF) + NORM_EPS), lax.rsqrt(sb * (1.0 / HALF) + NORM_EPS))
            xh = x * rstd
            dg_ref[gi:gi + 1, :] += jnp.sum(dy * xh, axis=0, keepdims=True)
            dxh = dy * g_ref[gi:gi + 1, :]
            pr = dxh * xh
            ma = jnp.sum(jnp.where(lo, pr, 0.0), axis=1, keepdims=True)
            mb = jnp.sum(pr, axis=1, keepdims=True) - ma
            mu = jnp.where(lo, ma, mb) * (1.0 / HALF)
            o_ref[:, tt * 128:(tt + 1) * 128] = (rstd * (dxh - xh * mu)).astype(BF16)

        @pl.when(pl.program_id(0) == nsteps - 1)
        def _():
            acc = dg_ref[...]
            dg_ref[...] = acc + pltpu.roll(acc, HALF, 1)

    return pl.pallas_call(
        body, name=name, grid=(nsteps,),
        in_specs=[pl.BlockSpec((tm, 1024), lambda i: (i, 0)), pl.BlockSpec((tm, 256), lambda i: (i, 0)),
                  pl.BlockSpec((tm, 1024), lambda i: (i, 0)), pl.BlockSpec((tm, 256), lambda i: (i, 8)),
                  pl.BlockSpec((2, 128), lambda i: (0, 0))],
        out_specs=[pl.BlockSpec((tm, 1280), lambda i: (i, 0)), pl.BlockSpec((2, 128), lambda i: (0, 0))],
        out_shape=[SDS((s, 1280), BF16), SDS((2, 128), F32)],
        compiler_params=_cparams(("arbitrary",)),
    )(dqn, dkn, proj, proj, g128)


SWA_STACK = 8
SWA_QSCALE = 1.0 / math.sqrt(HALF)


def _swa_masks(n, stack):
    b = SWA_BLOCK
    row = lax.broadcasted_iota(jnp.int32, (stack * b, 2 * b), 0) & (b - 1)
    col = lax.broadcasted_iota(jnp.int32, (stack * b, 2 * b), 1)
    rel = row + b - col
    valid = (rel >= 0) & (rel < b) & ((col >= b) | (n > 0))
    return rel.astype(F32), valid


def _swa_stack(ref, gp, lane, scale=None):
    parts = []
    for tt in range(4 * gp, 4 * gp + 4):
        x = ref[:, tt * 128:(tt + 1) * 128]
        if scale is not None:
            x = x * scale
        zx = jnp.zeros_like(x)
        parts += [jnp.where(lane < HALF, x, zx), jnp.where(lane >= HALF, x, zx)]
    return jnp.concatenate(parts, axis=0)


def _swa_unstack(st, lane):
    b = SWA_BLOCK
    return [jnp.where(lane < HALF, st[(2 * i) * b:(2 * i + 1) * b], st[(2 * i + 1) * b:(2 * i + 2) * b])
            for i in range(4)]


def _swa_head_cols(gp, sink_ref):
    b = SWA_BLOCK
    heads = [_swa_head(tt, a) for tt in range(4 * gp, 4 * gp + 4) for a in range(2)]
    slope = jnp.concatenate([jnp.full((b, 1), SWA_SLOPES[h], F32) for h in heads], axis=0)
    sink = jnp.concatenate([jnp.full((b, 1), sink_ref[h], F32) for h in heads], axis=0)
    return heads, slope, sink


def _swa_probs(qs, kt, relf, valid, slope, sink):
    sc = jnp.where(valid, _dot_nt(qs, kt) - slope * relf, -1e30)
    m = jnp.maximum(jnp.max(sc, axis=1, keepdims=True), sink)
    e = jnp.where(valid, jnp.exp(sc - m), 0.0)
    es = jnp.exp(sink - m)
    inv = 1.0 / (jnp.sum(e, axis=1, keepdims=True) + es)
    return e * inv, es * inv


def swa_attn_fwd(qkn, proj, sinks, name):
    s = qkn.shape[0]
    b = SWA_BLOCK
    nb = s // b

    def body(sink_ref, q_ref, kc_ref, kp_ref, vc_ref, vp_ref, o_ref):
        n = pl.program_id(0)
        lane = lax.broadcasted_iota(jnp.int32, (1, LANES), 1)
        relf, valid = _swa_masks(n, 1)
        for tt in range(8):
            gp = tt // 4
            kt = jnp.concatenate([kp_ref[:, gp * 128:(gp + 1) * 128], kc_ref[:, gp * 128:(gp + 1) * 128]], axis=0)
            vt = jnp.concatenate([vp_ref[:, gp * 128:(gp + 1) * 128], vc_ref[:, gp * 128:(gp + 1) * 128]], axis=0)
            qt = q_ref[:, tt * 128:(tt + 1) * 128] * SWA_QSCALE
            outs = []
            for a in range(2):
                h = _swa_head(tt, a)
                hm = (lane < HALF) if a == 0 else (lane >= HALF)
                qm = jnp.where(hm, qt, jnp.zeros_like(qt))
                p, _ = _swa_probs(qm, kt, relf, valid, SWA_SLOPES[h], sink_ref[h])
                outs.append(_dot(p.astype(BF16), vt))
            o_ref[:, tt * 128:(tt + 1) * 128] = jnp.where(lane < HALF, outs[0], outs[1])

    cur = lambda w, c: pl.BlockSpec((b, w), lambda n: (n, c))
    prev = lambda w, c: pl.BlockSpec((b, w), lambda n: (jnp.maximum(n - 1, 0), c))
    return pl.pallas_call(
        body, name=name, grid=(nb,),
        in_specs=[pl.BlockSpec(memory_space=pltpu.SMEM), cur(1024, 0), cur(256, 4), prev(256, 4),
                  cur(256, 9), prev(256, 9)],
        out_specs=cur(1024, 0),
        out_shape=SDS((s, 1024), F32),
        compiler_params=_cparams(("parallel",)),
    )(sinks, qkn, qkn, qkn, proj, proj)


def swa_attn_bwd(qkn, proj, sinks, do, name):
    s = qkn.shape[0]
    b = SWA_BLOCK
    nb = s // b

    def body(sink_ref, q_ref, kc_ref, kp_ref, vc_ref, vp_ref, do_ref, dq_ref, dk_ref, dv_ref, ds_ref):
        n = pl.program_id(0)

        @pl.when(n == 0)
        def _():
            dk_ref[...] = jnp.zeros_like(dk_ref)
            dv_ref[...] = jnp.zeros_like(dv_ref)
            ds_ref[...] = jnp.zeros_like(ds_ref)

        lane = lax.broadcasted_iota(jnp.int32, (1, LANES), 1)
        relf, valid = _swa_masks(n, SWA_STACK)
        cur0 = pl.multiple_of(n * b, b)
        prev0 = pl.multiple_of(jnp.maximum(n - 1, 0) * b, b)
        dsink = jnp.zeros((1, LANES), F32)
        for gp in range(2):
            kt = jnp.concatenate([kp_ref[:, gp * 128:(gp + 1) * 128], kc_ref[:, gp * 128:(gp + 1) * 128]], axis=0)
            vt = jnp.concatenate([vp_ref[:, gp * 128:(gp + 1) * 128], vc_ref[:, gp * 128:(gp + 1) * 128]], axis=0)
            heads, slope, sink = _swa_head_cols(gp, sink_ref)
            qs = _swa_stack(q_ref, gp, lane, SWA_QSCALE)
            dos = _swa_stack(do_ref, gp, lane)
            p, ps = _swa_probs(qs, kt, relf, valid, slope, sink)
            dp = _dot_nt(dos, vt)
            delta = jnp.sum(p * dp, axis=1, keepdims=True)
            dz = (p * (dp - delta)).astype(BF16)
            sd = ps * delta
            for i, h in enumerate(heads):
                dsink = dsink + jnp.where(lane == h, -jnp.sum(sd[i * b:(i + 1) * b]), 0.0)
            for i, dq_tile in enumerate(_swa_unstack(_dot(dz, kt), lane)):
                tt = 4 * gp + i
                dq_ref[:, tt * 128:(tt + 1) * 128] = (dq_tile * SWA_QSCALE).astype(BF16)
            dkt = _dot_tn(dz, qs)
            dvt = _dot_tn(p.astype(BF16), dos)
            cols = slice(gp * 128, (gp + 1) * 128)
            dk_ref[pl.ds(prev0, b), cols] += dkt[:b]
            dk_ref[pl.ds(cur0, b), cols] += dkt[b:]
            dv_ref[pl.ds(prev0, b), cols] += dvt[:b]
            dv_ref[pl.ds(cur0, b), cols] += dvt[b:]
        ds_ref[0:1, :] += dsink

    cur = lambda w, c: pl.BlockSpec((b, w), lambda n: (n, c))
    prev = lambda w, c: pl.BlockSpec((b, w), lambda n: (jnp.maximum(n - 1, 0), c))
    full = pl.BlockSpec((s, 256), lambda n: (0, 0))
    return pl.pallas_call(
        body, name=name, grid=(nb,),
        in_specs=[pl.BlockSpec(memory_space=pltpu.SMEM), cur(1024, 0), cur(256, 4), prev(256, 4),
                  cur(256, 9), prev(256, 9), cur(1024, 0)],
        out_specs=[cur(1024, 0), full, full, pl.BlockSpec((8, LANES), lambda n: (0, 0))],
        out_shape=[SDS((s, 1024), BF16), SDS((s, 256), F32), SDS((s, 256), F32), SDS((8, LANES), F32)],
        compiler_params=_cparams(("arbitrary",)),
    )(sinks, qkn, qkn, qkn, proj, proj, do)


def _my_pos():
    return lax.axis_index("x"), lax.axis_index("y"), lax.axis_index("c")


ANY = pl.BlockSpec(memory_space=pl.ANY)


def all_gather_packed(xs, name):
    r, n = xs.shape

    def body(x_ref, out_ref, send_sems, recv_sems, local_sem):
        x, y, c = _my_pos()
        me, sibling = (x, y, c), (x, y, 1 - c)
        chips = [(1 - x, y), (x, 1 - y), (1 - x, 1 - y)]

        def slot(px, py, pc):
            return out_ref.at[4 * px + 2 * py + pc]

        def copy(k, block, to, src=None):
            return pltpu.make_async_remote_copy(
                src_ref=slot(*block) if src is None else src, dst_ref=slot(*block),
                send_sem=send_sems.at[k], recv_sem=recv_sems.at[k],
                device_id=to, device_id_type=MESH_ID)

        mine = pltpu.make_async_copy(x_ref, slot(*me), local_sem)
        mine.start()
        first = [copy(0, me, sibling, src=x_ref)]
        first += [copy(1 + j, me, (*chip, c), src=x_ref) for j, chip in enumerate(chips)]
        for cp in first:
            cp.start()
        passed = [copy(4 + j, (*chip, c), sibling) for j, chip in enumerate(chips)]
        for j, chip in enumerate(chips):
            copy(1 + j, (*chip, c), me).wait_recv()
            passed[j].start()
        copy(0, sibling, me).wait_recv()
        for j, chip in enumerate(chips):
            copy(4 + j, (*chip, 1 - c), me).wait_recv()
        for cp in first + passed:
            cp.wait_send()
        mine.wait()

    return pl.pallas_call(
        body, name=name,
        out_shape=SDS((N_DEV, r, n), xs.dtype),
        in_specs=[ANY], out_specs=ANY,
        scratch_shapes=[pltpu.SemaphoreType.DMA((7,)), pltpu.SemaphoreType.DMA((7,)), pltpu.SemaphoreType.DMA],
    )(xs)


def rs_sibling_exchange(g, name):
    _, r, n = g.shape

    def body(g_ref, ra_ref, send_sems, recv_sems):
        x, y, c = _my_pos()
        copies = [pltpu.make_async_remote_copy(
            src_ref=g_ref.at[2 * chip + (1 - c)], dst_ref=ra_ref.at[chip],
            send_sem=send_sems.at[chip], recv_sem=recv_sems.at[chip],
            device_id=(x, y, 1 - c), device_id_type=MESH_ID) for chip in range(4)]
        for cp in copies:
            cp.start()
        for cp in copies:
            cp.wait()

    return pl.pallas_call(
        body, name=name, out_shape=SDS((4, r, n), g.dtype), in_specs=[ANY], out_specs=ANY,
        scratch_shapes=[pltpu.SemaphoreType.DMA((4,)), pltpu.SemaphoreType.DMA((4,))],
    )(g)


def rs_pair_add(g, ra, cidx, name):
    _, r, n = g.shape
    tr = r // 8

    def body(c_ref, g_ref, ra_ref, p_ref):
        p_ref[...] = (g_ref[...].astype(F32) + ra_ref[...].astype(F32)).astype(p_ref.dtype)

    return pl.pallas_call(
        body, name=name,
        grid_spec=pltpu.PrefetchScalarGridSpec(
            num_scalar_prefetch=1, grid=(4, r // tr),
            in_specs=[pl.BlockSpec((1, tr, n), lambda i, j, c: (2 * i + c[0], j, 0)),
                      pl.BlockSpec((1, tr, n), lambda i, j, c: (i, j, 0))],
            out_specs=pl.BlockSpec((1, tr, n), lambda i, j, c: (i, j, 0))),
        out_shape=SDS((4, r, n), g.dtype),
        compiler_params=_cparams(("parallel", "parallel")),
    )(cidx, g, ra)


def rs_cross_exchange(p, name):
    _, r, n = p.shape

    def body(p_ref, rb_ref, send_sems, recv_sems):
        x, y, c = _my_pos()
        copies = []
        for k, (dx, dy) in enumerate(((1, 0), (0, 1), (1, 1))):
            tx = 1 - x if dx else x
            ty = 1 - y if dy else y
            copies.append(pltpu.make_async_remote_copy(
                src_ref=p_ref.at[2 * tx + ty], dst_ref=rb_ref.at[k],
                send_sem=send_sems.at[k], recv_sem=recv_sems.at[k],
                device_id=(tx, ty, c), device_id_type=MESH_ID))
        for cp in copies:
            cp.start()
        for cp in copies:
            cp.wait()

    return pl.pallas_call(
        body, name=name, out_shape=SDS((3, r, n), p.dtype), in_specs=[ANY], out_specs=ANY,
        scratch_shapes=[pltpu.SemaphoreType.DMA((3,)), pltpu.SemaphoreType.DMA((3,))],
    )(p)


def _adamw(w, g, m, v):
    m = ADAM_B1 * m + (1.0 - ADAM_B1) * g
    v = ADAM_B2 * v + (1.0 - ADAM_B2) * (g * g)
    m_hat = m / (1.0 - ADAM_B1 ** ADAM_STEP)
    v_hat = v / (1.0 - ADAM_B2 ** ADAM_STEP)
    delta = -ADAM_LR * (m_hat / (jnp.sqrt(v_hat) + ADAM_EPS) + ADAM_WD * w)
    return delta, m, v


def rs_final_adamw(p, rb, chipidx, w, m, v, row0, col_blk, name):
    r, n = w.shape
    tr = min(r, 256 if n <= PAIR_COLS else 128)
    assert r % tr == 0 and row0 % tr == 0
    rb0 = row0 // tr

    def body(c_ref, p_ref, rb_ref, w_ref, m_ref, v_ref, g_out, d_out, m_out, v_out):
        g = p_ref[0].astype(F32)
        for k in range(3):
            g = g + rb_ref[k].astype(F32)
        d, mn, vn = _adamw(w_ref[...], g, m_ref[...], v_ref[...])
        g_out[...] = g
        d_out[...] = d
        m_out[...] = mn
        v_out[...] = vn

    blk = pl.BlockSpec((tr, n), lambda j, c: (j, 0))
    return pl.pallas_call(
        body, name=name,
        grid_spec=pltpu.PrefetchScalarGridSpec(
            num_scalar_prefetch=1, grid=(r // tr,),
            in_specs=[pl.BlockSpec((1, tr, n), lambda j, c: (c[0], rb0 + j, col_blk)),
                      pl.BlockSpec((3, tr, n), lambda j, c: (0, rb0 + j, col_blk)), blk, blk, blk],
            out_specs=[blk, blk, blk, blk]),
        out_shape=[SDS((r, n), F32)] * 4,
        compiler_params=_cparams(("parallel",)),
    )(chipidx, p, rb, w, m, v)


def small_allreduce_adamw(g, w, m, v, name):
    rows = g.shape[0]

    def body(g_ref, w_ref, m_ref, v_ref, g_out, d_out, m_out, v_out, buf, send_sems, recv_sems):
        x, y, c = _my_pos()
        my = 4 * x + 2 * y + c
        buf[my] = g_ref[...]
        copies = []
        for k in range(1, N_DEV):
            dx, dy, dc = (k >> 2) & 1, (k >> 1) & 1, k & 1
            tgt = (1 - x if dx else x, 1 - y if dy else y, 1 - c if dc else c)
            copies.append(pltpu.make_async_remote_copy(
                src_ref=buf.at[my], dst_ref=buf.at[my],
                send_sem=send_sems.at[k - 1], recv_sem=recv_sems.at[k - 1],
                device_id=tgt, device_id_type=MESH_ID))
        for cp in copies:
            cp.start()
        for cp in copies:
            cp.wait()
        tot = buf[0]
        for j in range(1, N_DEV):
            tot = tot + buf[j]
        d, mn, vn = _adamw(w_ref[...], tot, m_ref[...], v_ref[...])
        g_out[...] = tot
        d_out[...] = d
        m_out[...] = mn
        v_out[...] = vn

    vm = pl.BlockSpec(memory_space=pltpu.VMEM)
    return pl.pallas_call(
        body, name=name, out_shape=[SDS((rows, LANES), F32)] * 4,
        in_specs=[vm] * 4, out_specs=[vm] * 4,
        scratch_shapes=[pltpu.VMEM((N_DEV, rows, LANES), F32),
                        pltpu.SemaphoreType.DMA((N_DEV - 1,)), pltpu.SemaphoreType.DMA((N_DEV - 1,))],
    )(g, w, m, v)


BIG = ["l0_w_in", "l0_w_out", "l1_w_in", "l1_w_uq", "l1_w_ukv", "l1_w_out", "l2_w_in", "l2_w_out",
       "l3_w_in", "l3_w_out"]
SMALL = ["l0_norm", "l1_norm", "l1_q_a_norm", "l1_kv_a_norm", "l1_q_head_norm", "l1_k_head_norm",
         "l2_norm", "l2_q_head_norm", "l2_k_head_norm", "l2_sinks", "l3_norm"]
ROW_SHARDED = {"l0_w_out", "l1_w_out", "l2_w_out", "l3_w_out"}

MLA_UQ_PERM = np.concatenate([np.arange(h * 192, h * 192 + 128) for h in range(8)]
                             + [np.arange(h * 192 + 128, (h + 1) * 192) for h in range(8)])
SWA_IN_PERM = np.concatenate([SWA_Q_PERM, 1536 + SWA_Q_PERM, np.arange(1024, 1536)])


def _inv(perm):
    inv = np.empty_like(perm)
    inv[perm] = np.arange(perm.size)
    return inv


OUTS = ["l0_w_out", "l1_w_out", "l2_w_out", "l3_w_out"]
REST = ["l1_w_in", "l1_w_uq", "l1_w_ukv", "l2_w_in"]


def _pack_rest(shards):
    parts = [shards[nm].reshape(-1, D_MODEL) for nm in REST]
    used = sum(p.shape[0] for p in parts)
    assert REST_ROW0 + used == ROWS_USED
    parts.append(jnp.zeros((REST_ROWS - used, D_MODEL), parts[0].dtype))
    return jnp.concatenate(parts, axis=0)


def _unpack_rest(packed, shapes):
    out, off = {}, 0
    for nm in REST:
        shp = shapes[nm]
        rows = shp[0] * shp[1] // D_MODEL
        out[nm] = packed[off:off + rows].reshape(shp)
        off += rows
    return out


def _pack(shards):
    pair = jnp.concatenate([shards["l0_w_in"], shards["l3_w_in"]], axis=1)
    return jnp.concatenate([pair] + [shards[nm] for nm in OUTS] + [_pack_rest(shards)], axis=0)


def _out_weight(gathered, layer):
    r0 = OUT_ROW0 + 128 * layer
    return gathered[:, r0:r0 + 128].reshape(D_MODEL, D_MODEL)


def _rest_full_weights(gathered, shapes):
    out, off = {}, REST_ROW0
    for nm in REST:
        shp = shapes[nm]
        rows = shp[0] * shp[1] // D_MODEL
        blk = gathered[:, off:off + rows].reshape((N_DEV,) + shp)
        out[nm] = jnp.transpose(blk, (1, 0, 2)).reshape(shp[0], N_DEV * shp[1])
        off += rows
    return out


def _grad_pieces(pair_l0, pair_l3, dw_out, rest_grads, shapes):
    parts = [jnp.concatenate([pair_l0, pair_l3], axis=2)]
    parts += [g.reshape(N_DEV, 128, D_MODEL).astype(BF16) for g in dw_out]
    used = REST_ROW0
    for nm in REST:
        shp = shapes[nm]
        blk = jnp.transpose(rest_grads[nm].reshape(shp[0], N_DEV, shp[1]), (1, 0, 2))
        parts.append(blk.reshape(N_DEV, -1, D_MODEL).astype(BF16))
        used += parts[-1].shape[1]
    parts.append(jnp.zeros((N_DEV, ROWS_PACK - used, D_MODEL), BF16))
    return jnp.concatenate(parts, axis=1)


def _pack_small(vals):
    flat = jnp.concatenate([vals[nm].reshape(-1).astype(F32) for nm in SMALL])
    flat = jnp.concatenate([flat, jnp.zeros((SMALL_ROWS * LANES - flat.shape[0],), F32)])
    return flat.reshape(SMALL_ROWS, LANES)


def _unpack_small(packed, shapes):
    flat = packed.reshape(-1)
    out, off = {}, 0
    for nm in SMALL:
        n = shapes[nm][0]
        out[nm] = flat[off:off + n]
        off += n
    return out


def _row(v):
    return v.reshape(1, -1).astype(F32)


def _mla_gain_rows(g):
    return jnp.concatenate([g[:128], g[128:], g[128:]]).reshape(1, 256).astype(F32)


def _layer_fwd_sb(x, norm, gathered, shard_col, w_out, tag):
    proj, xn = norm_matmul(x, 0, _row(norm), gathered, f"{tag}_in", shard_col=shard_col)
    o = sb_attn_fwd(proj, f"{tag}_attn")
    y = gate_out_fwd(o, proj, 3, w_out, x, f"{tag}_out")
    return y, (x, xn, proj, o)


def _layer_bwd_sb(dy, saved, norm, gathered, shard_col, w_out, tag):
    x, xn, proj, o = saved
    do, dgate, hg = gate_out_bwd(dy, w_out, o, proj, 3, f"{tag}_outb")
    dw_out = tn_matmul(hg, dy, f"{tag}_dwout")
    dq, dk, dv = sb_attn_bwd(proj, o, do, f"{tag}_attnb")
    dproj = jnp.concatenate([dq, dk, dv, dgate], axis=1)
    dw_in = tn_matmul_shards(xn, dproj, f"{tag}_dwin")
    dx, dnorm = nt_norm_bwd(dproj, gathered, x, 0, _row(norm), dy, f"{tag}_inb", F32, shard_col=shard_col)
    return dx, dw_in, dw_out, dnorm.reshape(-1)


def kernel(x, l0_norm, l0_w_in, l0_w_out, l1_norm, l1_w_in, l1_q_a_norm, l1_w_uq, l1_kv_a_norm, l1_w_ukv, l1_q_head_norm, l1_k_head_norm, l1_w_out, l2_norm, l2_w_in, l2_q_head_norm, l2_k_head_norm, l2_sinks, l2_w_out, l3_norm, l3_w_in, l3_w_out, loss_target, m_l0_norm, m_l0_w_in, m_l0_w_out, m_l1_norm, m_l1_w_in, m_l1_q_a_norm, m_l1_w_uq, m_l1_kv_a_norm, m_l1_w_ukv, m_l1_q_head_norm, m_l1_k_head_norm, m_l1_w_out, m_l2_norm, m_l2_w_in, m_l2_q_head_norm, m_l2_k_head_norm, m_l2_sinks, m_l2_w_out, m_l3_norm, m_l3_w_in, m_l3_w_out, v_l0_norm, v_l0_w_in, v_l0_w_out, v_l1_norm, v_l1_w_in, v_l1_q_a_norm, v_l1_w_uq, v_l1_kv_a_norm, v_l1_w_ukv, v_l1_q_head_norm, v_l1_k_head_norm, v_l1_w_out, v_l2_norm, v_l2_w_in, v_l2_q_head_norm, v_l2_k_head_norm, v_l2_sinks, v_l2_w_out, v_l3_norm, v_l3_w_in, v_l3_w_out):
    loc = dict(locals())
    names = BIG + SMALL
    w = {nm: loc[nm] for nm in names}
    mom = {nm: loc["m_" + nm] for nm in names}
    vel = {nm: loc["v_" + nm] for nm in names}
    shapes = {nm: w[nm].shape for nm in names}
    xs = x[0]
    tgt = loss_target[0]

    gathered = all_gather_packed(_pack({nm: w[nm].astype(BF16) for nm in BIG}), "ag_weights")
    loss_local, dx0, pieces, sg = _local_step(xs, tgt, gathered, {nm: w[nm] for nm in SMALL}, shapes)
    loss = lax.psum(loss_local, ("x", "y", "c"))
    return _reduce_and_update(loss, dx0, pieces, sg, w, mom, vel, shapes)


def _local_step(xs, tgt, gathered, sm, shapes):
    s = xs.shape[0]
    fw = _rest_full_weights(gathered, shapes)
    w_out = [_out_weight(gathered, layer) for layer in range(4)]
    l0_norm, l1_norm, l2_norm, l3_norm = sm["l0_norm"], sm["l1_norm"], sm["l2_norm"], sm["l3_norm"]
    l1_q_a_norm, l1_kv_a_norm = sm["l1_q_a_norm"], sm["l1_kv_a_norm"]
    l1_q_head_norm, l1_k_head_norm = sm["l1_q_head_norm"], sm["l1_k_head_norm"]
    l2_q_head_norm, l2_k_head_norm, l2_sinks = sm["l2_q_head_norm"], sm["l2_k_head_norm"], sm["l2_sinks"]
    w1_in = jnp.concatenate([fw["l1_w_in"][:, 448:], fw["l1_w_in"][:, :448], jnp.zeros((D_MODEL, 64), BF16)], axis=1)
    w1_uq = fw["l1_w_uq"][:, MLA_UQ_PERM]
    w2_in = fw["l2_w_in"][:, SWA_IN_PERM]
    w2_out = w_out[2][SWA_Q_PERM, :]

    x1, sv0 = _layer_fwd_sb(xs, l0_norm, gathered, 0, w_out[0], "l0")

    cos, sin = _rope_tables(s)
    gq, gk = _mla_gain_rows(l1_q_head_norm), _mla_gain_rows(l1_k_head_norm)
    proj1, xn1 = norm_matmul(x1, 0, _row(l1_norm), w1_in, "l1_in")
    qa, qln = norm_matmul(proj1, 4, _row(l1_q_a_norm), w1_uq, "l1_uq")
    kva, kvn = norm_matmul(proj1, 10, _row(l1_kv_a_norm), fw["l1_w_ukv"], "l1_ukv")
    q_p, k_p = mla_prep_fwd(qa, kva, proj1, gq, gk, cos, sin, "l1_prep")
    o1, lse1 = mla_attn_fwd(q_p, k_p, kva, "l1_attn")
    x2 = gate_out_fwd(o1, proj1, 0, w_out[1], x1, "l1_out")

    g2 = jnp.stack([jnp.tile(l2_q_head_norm, 2), jnp.tile(l2_k_head_norm, 2)]).astype(F32)
    proj2, xn2 = norm_matmul(x2, 0, _row(l2_norm), w2_in, "l2_in")
    qkn2 = swa_prep_fwd(proj2, g2, "l2_prep")
    o2 = swa_attn_fwd(qkn2, proj2, l2_sinks, "l2_attn")
    x3 = gate_out_fwd(o2, proj2, 1, w2_out, x2, "l2_out")

    x4, sv3 = _layer_fwd_sb(x3, l3_norm, gathered, 1, w_out[3], "l3")

    lossblk, dy = loss_kernel(x4, tgt, "loss")

    fg, sg = {}, {}
    dw_out = [None] * 4
    dx3, pair_l3, dw_out[3], sg["l3_norm"] = _layer_bwd_sb(dy, sv3, l3_norm, gathered, 1, w_out[3], "l3")

    do2, dgate2, hg2 = gate_out_bwd(dx3, w2_out, o2, proj2, 1, "l2_outb")
    dw2_out = tn_matmul(hg2, dx3, "l2_dwout")
    dqn2, dkn2, dv2, dsink2 = swa_attn_bwd(qkn2, proj2, l2_sinks, do2, "l2_attnb")
    dqk2, dg2 = swa_prep_bwd(dqn2, dkn2, proj2, g2, "l2_prepb")
    dproj2 = jnp.concatenate([dqk2[:, :1024], dgate2, dqk2[:, 1024:], dv2.astype(BF16)], axis=1)
    dw2_in = tn_matmul(xn2, dproj2, "l2_dwin")
    dx2, dn2 = nt_norm_bwd(dproj2, w2_in, x2, 0, _row(l2_norm), dx3, "l2_inb", F32)
    fg["l2_w_in"] = dw2_in[:, _inv(SWA_IN_PERM)]
    dw_out[2] = dw2_out[_inv(SWA_Q_PERM), :]
    sg["l2_norm"] = dn2.reshape(-1)
    sg["l2_q_head_norm"] = dg2[0, :HALF]
    sg["l2_k_head_norm"] = dg2[1, :HALF]
    sg["l2_sinks"] = dsink2[0, :SWA_HEADS]

    do1, dgate1, hg1 = gate_out_bwd(dx2, w_out[1], o1, proj1, 0, "l1_outb")
    dw_out[1] = tn_matmul(hg1, dx2, "l1_dwout")
    dq_p, dk_p, dv1 = mla_attn_bwd(q_p, k_p, kva, o1, do1, lse1, "l1_attnb")
    dqa, dkva, dkpe, dgq, dgk = mla_prep_bwd(dq_p, dk_p, dv1, qa, kva, proj1, gq, gk, cos, sin, "l1_prepb")
    dw_uq = tn_matmul(qln, dqa, "l1_dwuq")
    fg["l1_w_ukv"] = tn_matmul(kvn, dkva, "l1_dwukv")
    dqlat, dgqa = nt_norm_bwd(dqa, w1_uq, proj1, 4, _row(l1_q_a_norm), None, "l1_uqb", BF16)
    dkvlat, dgkva = nt_norm_bwd(dkva, fw["l1_w_ukv"], proj1, 10, _row(l1_kv_a_norm), None, "l1_ukvb", BF16)
    dproj1 = jnp.concatenate([dgate1, dqlat, dkvlat, dkpe], axis=1)
    dw1_in = tn_matmul(xn1, dproj1, "l1_dwin")
    dx1, dn1 = nt_norm_bwd(dproj1, w1_in, x1, 0, _row(l1_norm), dx2, "l1_inb", F32)
    fg["l1_w_in"] = jnp.concatenate([dw1_in[:, 1024:1472], dw1_in[:, :1024]], axis=1)
    fg["l1_w_uq"] = dw_uq[:, _inv(MLA_UQ_PERM)]
    sg["l1_norm"] = dn1.reshape(-1)
    sg["l1_q_a_norm"] = dgqa.reshape(-1)
    sg["l1_kv_a_norm"] = dgkva.reshape(-1)
    sg["l1_q_head_norm"] = jnp.concatenate([dgq[0, :128], dgq[0, 128:192] + dgq[0, 192:256]])
    sg["l1_k_head_norm"] = jnp.concatenate([dgk[0, :128], dgk[0, 128:192] + dgk[0, 192:256]])

    dx0, pair_l0, dw_out[0], sg["l0_norm"] = _layer_bwd_sb(dx1, sv0, l0_norm, gathered, 0, w_out[0], "l0")
    return lossblk[0, 0], dx0, _grad_pieces(pair_l0, pair_l3, dw_out, fg, shapes), sg


def _reduce_and_update(loss, dx0, pieces, sg, w, mom, vel, shapes):
    xi, yi, ci = _my_pos()
    cidx = jnp.reshape(ci, (1,)).astype(jnp.int32)
    chipidx = jnp.reshape(2 * xi + yi, (1,)).astype(jnp.int32)
    ra = rs_sibling_exchange(pieces, "rs_sibling")
    part = rs_pair_add(pieces, ra, cidx, "rs_pair_add")
    rb = rs_cross_exchange(part, "rs_cross")

    def update(nm, row0, col_blk):
        return rs_final_adamw(part, rb, chipidx, w[nm], mom[nm], vel[nm], row0, col_blk, f"adamw_{nm}")

    big = {"l0_w_in": update("l0_w_in", 0, 0), "l3_w_in": update("l3_w_in", 0, 1)}
    for layer, nm in enumerate(OUTS):
        big[nm] = update(nm, OUT_ROW0 + 128 * layer, 0)
    rest = rs_final_adamw(part, rb, chipidx, _pack_rest(w), _pack_rest(mom), _pack_rest(vel), REST_ROW0, 0,
                          "adamw_rest")
    gs, ds, ms, vs = small_allreduce_adamw(_pack_small(sg), _pack_small({nm: w[nm] for nm in SMALL}),
                                           _pack_small({nm: mom[nm] for nm in SMALL}),
                                           _pack_small({nm: vel[nm] for nm in SMALL}), "small_allreduce")

    outs = []
    for i, small in enumerate((gs, ds, ms, vs)):
        d = {nm: big[nm][i] for nm in big}
        d.update(_unpack_rest(rest[i], shapes))
        d.update(_unpack_small(small, shapes))
        outs.append(d)
    order = ["l0_norm", "l0_w_in", "l0_w_out", "l1_norm", "l1_w_in", "l1_q_a_norm", "l1_w_uq", "l1_kv_a_norm",
             "l1_w_ukv", "l1_q_head_norm", "l1_k_head_norm", "l1_w_out", "l2_norm", "l2_w_in", "l2_q_head_norm",
             "l2_k_head_norm", "l2_sinks", "l2_w_out", "l3_norm", "l3_w_in", "l3_w_out"]
    flat = [loss, dx0[None]]
    for d in outs:
        flat += [d[nm] for nm in order]
    return tuple(flat)
```

```python
import math

import numpy as np
import jax
import jax.numpy as jnp
from jax import lax
from jax.experimental import pallas as pl
from jax.experimental.pallas import tpu as pltpu

F32 = jnp.float32
BF16 = jnp.bfloat16
SDS = jax.ShapeDtypeStruct
MESH_ID = pl.DeviceIdType.MESH

D_MODEL = 1024
NORM_EPS = 1e-6
N_DEV = 8
LANES = 128
HALF = 64
ROWS_PACK = 2176
ROWS_USED = 2120
PAIR_COLS = 512
OUT_ROW0 = 1024
REST_ROW0 = 1536
REST_ROWS = ROWS_PACK - REST_ROW0
SMALL_ROWS = 40

ADAM_LR = 0.001
ADAM_B1 = 0.9
ADAM_B2 = 0.999
ADAM_EPS = 1e-08
ADAM_WD = 0.01
ADAM_STEP = 10

NT_DIMS = (((1,), (1,)), ((), ()))
TN_DIMS = (((0,), (0,)), ((), ()))


def _cparams(sem=None, vmem_mb=48):
    return pltpu.CompilerParams(dimension_semantics=sem, vmem_limit_bytes=vmem_mb * 2 ** 20)


def _dot(a, b):
    return jnp.dot(a, b, preferred_element_type=F32)


def _dot_nt(a, b):
    return lax.dot_general(a, b, NT_DIMS, preferred_element_type=F32)


def _dot_tn(a, b):
    return lax.dot_general(a, b, TN_DIMS, preferred_element_type=F32)


def _split_dot(a, b):
    hi = a.astype(BF16)
    lo = (a - hi.astype(F32)).astype(BF16)
    return _dot(hi, b) + _dot(lo, b)


def _cumsum_dot(lf, u):
    return _dot(lf.astype(BF16), u)


def _pick_tile(n, cap):
    return max(t for t in range(LANES, min(n, cap) + 1, LANES) if n % t == 0)


def _sigmoid(x):
    return 1.0 / (1.0 + jnp.exp(-x))


def norm_matmul(x, colblk, g, w, name, shard_col=None):
    s = x.shape[0]
    if shard_col is None:
        k, n = w.shape
        tn = _pick_tile(n, 1024)
        w_spec = pl.BlockSpec((k, tn), lambda i, j: (0, j))
    else:
        k, n, tn = D_MODEL, N_DEV * PAIR_COLS, 2 * PAIR_COLS
        w_spec = pl.BlockSpec((2, k, PAIR_COLS), lambda i, j: (j, 0, shard_col))
    tm = min(1024, s)
    assert s % tm == 0

    def body(x_ref, g_ref, w_ref, o_ref, xn_ref, xn_s):
        @pl.when(pl.program_id(1) == 0)
        def _():
            xf = x_ref[...].astype(F32)
            ms = jnp.mean(xf * xf, axis=-1, keepdims=True)
            y = (xf * lax.rsqrt(ms + NORM_EPS) * g_ref[...]).astype(BF16)
            xn_s[...] = y
            xn_ref[...] = y

        if shard_col is None:
            o_ref[...] = _dot(xn_s[...], w_ref[...]).astype(o_ref.dtype)
        else:
            for half in range(2):
                o_ref[:, half * PAIR_COLS:(half + 1) * PAIR_COLS] = _dot(xn_s[...], w_ref[half]).astype(o_ref.dtype)

    return pl.pallas_call(
        body, name=name, grid=(s // tm, n // tn),
        in_specs=[pl.BlockSpec((tm, k), lambda i, j: (i, colblk)),
                  pl.BlockSpec((1, k), lambda i, j: (0, 0)),
                  w_spec],
        out_specs=[pl.BlockSpec((tm, tn), lambda i, j: (i, j)),
                   pl.BlockSpec((tm, k), lambda i, j: (i, 0))],
        out_shape=[SDS((s, n), BF16), SDS((s, k), BF16)],
        scratch_shapes=[pltpu.VMEM((tm, k), BF16)],
        compiler_params=_cparams(("parallel", "arbitrary")),
    )(x, g, w)


def gate_out_fwd(o, proj, gate_blk, w_out, x, name):
    s = x.shape[0]
    d = D_MODEL
    tm = min(512, s)

    def body(o_ref, g_ref, w_ref, x_ref, y_ref):
        g = g_ref[...].astype(F32)
        hg = (o_ref[...].astype(F32) * (g * _sigmoid(g))).astype(BF16)
        y_ref[...] = x_ref[...] + _dot(hg, w_ref[...])

    return pl.pallas_call(
        body, name=name, grid=(s // tm,),
        in_specs=[pl.BlockSpec((tm, d), lambda i: (i, 0)),
                  pl.BlockSpec((tm, d), lambda i: (i, gate_blk)),
                  pl.BlockSpec((d, d), lambda i: (0, 0)),
                  pl.BlockSpec((tm, d), lambda i: (i, 0))],
        out_specs=pl.BlockSpec((tm, d), lambda i: (i, 0)),
        out_shape=SDS((s, d), F32),
        compiler_params=_cparams(("parallel",)),
    )(o, proj, w_out, x)


def gate_out_bwd(dy, w_out, o, proj, gate_blk, name):
    s = dy.shape[0]
    d = D_MODEL
    tm = min(512, s)

    def body(dy_ref, w_ref, o_ref, g_ref, do_ref, dg_ref, hg_ref):
        dh = _dot_nt(dy_ref[...].astype(BF16), w_ref[...])
        g = g_ref[...].astype(F32)
        o = o_ref[...].astype(F32)
        sg = _sigmoid(g)
        silu = g * sg
        do_ref[...] = (dh * silu).astype(BF16)
        dg_ref[...] = (dh * o * (sg * (1.0 + g * (1.0 - sg)))).astype(BF16)
        hg_ref[...] = (o * silu).astype(BF16)

    blk = pl.BlockSpec((tm, d), lambda i: (i, 0))
    return pl.pallas_call(
        body, name=name, grid=(s // tm,),
        in_specs=[blk, pl.BlockSpec((d, d), lambda i: (0, 0)), blk,
                  pl.BlockSpec((tm, d), lambda i: (i, gate_blk))],
        out_specs=[blk, blk, blk],
        out_shape=[SDS((s, d), BF16)] * 3,
        compiler_params=_cparams(("parallel",)),
    )(dy, w_out, o, proj)


def tn_matmul(a, b, name):
    s, k = a.shape
    n = b.shape[1]
    tk = min(1024, k)
    tn = _pick_tile(n, 512)
    ts = min(2048, s)
    assert k % tk == 0 and s % ts == 0

    def body(a_ref, b_ref, o_ref):
        @pl.when(pl.program_id(2) == 0)
        def _():
            o_ref[...] = jnp.zeros_like(o_ref)

        o_ref[...] += _dot_tn(a_ref[...].astype(BF16), b_ref[...].astype(BF16))

    return pl.pallas_call(
        body, name=name, grid=(k // tk, n // tn, s // ts),
        in_specs=[pl.BlockSpec((ts, tk), lambda i, j, r: (r, i)),
                  pl.BlockSpec((ts, tn), lambda i, j, r: (r, j))],
        out_specs=pl.BlockSpec((tk, tn), lambda i, j, r: (i, j)),
        out_shape=SDS((k, n), F32),
        compiler_params=_cparams(("parallel", "parallel", "arbitrary")),
    )(a, b)


def tn_matmul_shards(a, b_parts, name):
    s, k = a.shape
    per = D_MODEL // PAIR_COLS
    assert len(b_parts) * per == N_DEV and a.dtype == BF16
    assert all(b.shape == (s, D_MODEL) and b.dtype == BF16 for b in b_parts)

    def body(a_ref, *refs):
        o_ref = refs[-1]
        j = pl.program_id(0)
        for p, b_ref in enumerate(refs[:-1]):
            @pl.when(j // per == p)
            def _(b_ref=b_ref):
                o_ref[...] = _dot_tn(a_ref[...], b_ref[...]).astype(BF16)

    part_specs = [pl.BlockSpec((s, PAIR_COLS), lambda j, p=p: (0, jnp.clip(j - per * p, 0, per - 1)))
                  for p in range(len(b_parts))]
    return pl.pallas_call(
        body, name=name, grid=(N_DEV,),
        in_specs=[pl.BlockSpec((s, k), lambda j: (0, 0))] + part_specs,
        out_specs=pl.BlockSpec((None, k, PAIR_COLS), lambda j: (j, 0, 0)),
        out_shape=SDS((N_DEV, k, PAIR_COLS), BF16),
        compiler_params=_cparams(("arbitrary",)),
    )(a, *b_parts)


def nt_norm_bwd(dp, w, x, xcol, g, res, name, out_dtype, shard_col=None):
    tm = min(512, dp[0].shape[0] if shard_col is not None else dp.shape[0])
    if shard_col is None:
        s, n = dp.shape
        k = w.shape[0]
        w_spec = pl.BlockSpec((k, n), lambda i: (0, 0))
        dp_parts = [dp]
        dp_specs = [pl.BlockSpec((tm, n), lambda i: (i, 0))]
    else:
        s, k = dp[0].shape[0], D_MODEL
        w_spec = pl.BlockSpec((N_DEV, k, PAIR_COLS), lambda i: (0, 0, shard_col))
        dp_parts = list(dp)
        dp_specs = [pl.BlockSpec((tm, D_MODEL), lambda i: (i, 0)) for _ in dp_parts]
    np_ = len(dp_parts)
    per = D_MODEL // PAIR_COLS
    assert s % tm == 0
    has_res = res is not None

    def body(*refs):
        dp_refs = refs[:np_]
        if has_res:
            w_ref, x_ref, g_ref, r_ref, dx_ref, dg_ref = refs[np_:]
        else:
            w_ref, x_ref, g_ref, dx_ref, dg_ref = refs[np_:]

        @pl.when(pl.program_id(0) == 0)
        def _():
            dg_ref[...] = jnp.zeros_like(dg_ref)

        if shard_col is None:
            dxn = _dot_nt(dp_refs[0][...], w_ref[...])
        else:
            dxn = None
            for j in range(N_DEV):
                c0 = (j % per) * PAIR_COLS
                term = _dot_nt(dp_refs[j // per][:, c0:c0 + PAIR_COLS], w_ref[j])
                dxn = term if dxn is None else dxn + term
        xf = x_ref[...].astype(F32)
        rstd = lax.rsqrt(jnp.mean(xf * xf, axis=-1, keepdims=True) + NORM_EPS)
        xhat = xf * rstd
        dg_ref[...] += jnp.sum(dxn * xhat, axis=0, keepdims=True)
        dxh = dxn * g_ref[...]
        dx = rstd * (dxh - xhat * jnp.mean(dxh * xhat, axis=-1, keepdims=True))
        if has_res:
            dx = dx + r_ref[...]
        dx_ref[...] = dx.astype(out_dtype)

    in_specs = dp_specs + [w_spec,
                           pl.BlockSpec((tm, k), lambda i: (i, xcol)),
                           pl.BlockSpec((1, k), lambda i: (0, 0))]
    args = dp_parts + [w, x, g]
    if has_res:
        in_specs.append(pl.BlockSpec((tm, k), lambda i: (i, 0)))
        args.append(res)
    return pl.pallas_call(
        body, name=name, grid=(s // tm,),
        in_specs=in_specs,
        out_specs=[pl.BlockSpec((tm, k), lambda i: (i, 0)),
                   pl.BlockSpec((1, k), lambda i: (0, 0))],
        out_shape=[SDS((s, k), out_dtype), SDS((1, k), F32)],
        compiler_params=_cparams(("arbitrary",)),
    )(*args)


def loss_kernel(y, target, name):
    s, d = y.shape
    tm = min(512, s)

    def body(y_ref, t_ref, l_ref, dy_ref):
        @pl.when(pl.program_id(0) == 0)
        def _():
            l_ref[...] = jnp.zeros_like(l_ref)

        e = y_ref[...] - t_ref[...]
        dy_ref[...] = e * (1.0 / d)
        l_ref[...] += 0.5 * jnp.sum(jnp.mean(e * e, axis=-1, keepdims=True))

    blk = pl.BlockSpec((tm, d), lambda i: (i, 0))
    return pl.pallas_call(
        body, name=name, grid=(s // tm,),
        in_specs=[blk, blk],
        out_specs=[pl.BlockSpec((8, LANES), lambda i: (0, 0)), blk],
        out_shape=[SDS((8, LANES), F32), SDS((s, d), F32)],
        compiler_params=_cparams(("arbitrary",)),
    )(y, target)


SB_LOG_CUTOFF = -80.0
SB_TILES = 2


def _sb_sweep(qi, cs, rest, step):
    n = len(cs)

    def top(arrs):
        m = jnp.max(arrs[0])
        for a in arrs[1:]:
            m = jnp.maximum(m, jnp.max(a))
        return m

    def cond(st):
        return (st[0] <= qi) & (st[1] > SB_LOG_CUTOFF)

    def body(st):
        cs2, rest2 = step(st[0], st[2:2 + n], st[2 + n:])
        return (st[0] + 1, top(cs2)) + tuple(cs2) + tuple(rest2)

    return lax.while_loop(cond, body, (jnp.int32(1), top(cs)) + tuple(cs) + tuple(rest))[2 + n:]


def _tile_masks(t):
    row = lax.broadcasted_iota(jnp.int32, (t, t), 0)
    col = lax.broadcasted_iota(jnp.int32, (t, t), 1)
    return row, col


def sb_attn_fwd(proj, name):
    s = proj.shape[0]
    t = min(256, s)
    nq = s // t
    scale = 1.0 / math.sqrt(HALF)

    def body(q_ref, k_ref, v_ref, o_ref):
        lane = lax.broadcasted_iota(jnp.int32, (1, LANES), 1)
        row, col = _tile_masks(t)
        u_incl = (row >= col).astype(BF16)
        tri = jnp.concatenate([col < row] * 2, axis=0)

        def scores(pr, qs, kj):
            k = k_ref[pl.ds(pl.multiple_of(kj * t, t), t), pr * LANES:(pr + 1) * LANES]
            z = _dot_nt(qs, k)
            nz = -z
            lf = jnp.minimum(nz, 0.0) - jnp.log(1.0 + jnp.exp(jnp.minimum(z, nz)))
            return z, lf

        def accumulate(pr, kj, z, lf, c, acc, diag):
            v = v_ref[pl.ds(pl.multiple_of(kj * t, t), t), pr * LANES:(pr + 1) * LANES]
            if diag:
                lf = jnp.where(tri, lf, 0.0)
            incl = _cumsum_dot(lf, u_incl) + c
            a = jnp.exp(z + incl)
            if diag:
                a = jnp.where(tri, a, 0.0)
            acc = acc + _dot(a.astype(BF16), v)
            c = c + jnp.sum(lf, axis=1, keepdims=True)
            return c, acc

        def q_body(qi, carry):
            q0 = pl.multiple_of(qi * t, t)
            qss = []
            for pr in range(SB_TILES):
                q = q_ref[pl.ds(q0, t), pr * LANES:(pr + 1) * LANES] * scale
                zq = jnp.zeros_like(q)
                qss.append(jnp.concatenate([jnp.where(lane < HALF, q, zq), jnp.where(lane >= HALF, q, zq)], axis=0))

            def step(jj, cs, accs, diag=False):
                outs = [accumulate(pr, qi - jj, *scores(pr, qss[pr], qi - jj), cs[pr], accs[pr], diag)
                        for pr in range(SB_TILES)]
                return tuple(o[0] for o in outs), tuple(o[1] for o in outs)

            zero = (jnp.zeros((2 * t, 1), F32),) * SB_TILES, (jnp.zeros((2 * t, LANES), F32),) * SB_TILES
            cs, accs = step(0, zero[0], zero[1], True)
            accs = _sb_sweep(qi, cs, accs, step)
            for pr in range(SB_TILES):
                o_ref[pl.ds(q0, t), pr * LANES:(pr + 1) * LANES] = jnp.where(lane < HALF, accs[pr][:t], accs[pr][t:])
            return carry

        lax.fori_loop(0, nq, q_body, 0)

    w = SB_TILES * LANES
    return pl.pallas_call(
        body, name=name, grid=(D_MODEL // w,),
        in_specs=[pl.BlockSpec((s, w), lambda p: (0, p)),
                  pl.BlockSpec((s, w), lambda p: (0, D_MODEL // w + p)),
                  pl.BlockSpec((s, w), lambda p: (0, 2 * (D_MODEL // w) + p))],
        out_specs=pl.BlockSpec((s, w), lambda p: (0, p)),
        out_shape=SDS((s, D_MODEL), F32),
        compiler_params=_cparams(("parallel",)),
    )(proj, proj, proj)


def sb_attn_bwd(proj, o, do, name):
    s = proj.shape[0]
    t = min(256, s)
    nq = s // t
    scale = 1.0 / math.sqrt(HALF)

    def body(q_ref, k_ref, v_ref, o_ref, do_ref, dq_ref, dk_ref, dv_ref, dk_s, dv_s):
        dk_s[...] = jnp.zeros_like(dk_s)
        dv_s[...] = jnp.zeros_like(dv_s)
        lane = lax.broadcasted_iota(jnp.int32, (1, LANES), 1)
        row, col = _tile_masks(t)
        u_incl = (row >= col).astype(BF16)
        u_excl = (row > col).astype(BF16)
        tri = jnp.concatenate([col < row] * 2, axis=0)

        def scores(pr, qs, dos, kj):
            k0 = pl.multiple_of(kj * t, t)
            cols = slice(pr * LANES, (pr + 1) * LANES)
            z = _dot_nt(qs, k_ref[pl.ds(k0, t), cols])
            nz = -z
            lf = jnp.minimum(nz, 0.0) - jnp.log(1.0 + jnp.exp(jnp.minimum(z, nz)))
            return z, lf, _dot_nt(dos, v_ref[pl.ds(k0, t), cols])

        def accumulate(pr, qs, dos, tsum, kj, z, lf, da, c, r, dq, diag):
            k0 = pl.multiple_of(kj * t, t)
            cols = slice(pr * LANES, (pr + 1) * LANES)
            sig = jnp.exp(z + lf)
            if diag:
                lf = jnp.where(tri, lf, 0.0)
            incl = _cumsum_dot(lf, u_incl) + c
            a = jnp.exp(z + incl)
            if diag:
                a = jnp.where(tri, a, 0.0)
            ab = a.astype(BF16)
            dl = ab.astype(F32) * da
            pre = tsum - r - _split_dot(dl, u_excl)
            dz = dl - sig * pre
            if diag:
                dz = jnp.where(tri, dz, 0.0)
            dzb = dz.astype(BF16)
            dq = dq + _dot(dzb, k_ref[pl.ds(k0, t), cols])
            dk_s[pl.ds(k0, t), cols] += _dot_tn(dzb, qs)
            dv_s[pl.ds(k0, t), cols] += _dot_tn(ab, dos)
            c = c + jnp.sum(lf, axis=1, keepdims=True)
            r = r + jnp.sum(dl, axis=1, keepdims=True)
            return c, r, dq

        def q_body(qi, carry):
            q0 = pl.multiple_of(qi * t, t)
            lo, hi = lane < HALF, lane >= HALF
            qss, doss, tsums = [], [], []
            for pr in range(SB_TILES):
                cols = slice(pr * LANES, (pr + 1) * LANES)
                q = q_ref[pl.ds(q0, t), cols] * scale
                dob = do_ref[pl.ds(q0, t), cols]
                prod = dob.astype(F32) * o_ref[pl.ds(q0, t), cols]
                zb = jnp.zeros_like(q)
                qss.append(jnp.concatenate([jnp.where(lo, q, zb), jnp.where(hi, q, zb)], axis=0))
                doss.append(jnp.concatenate([jnp.where(lo, dob, zb), jnp.where(hi, dob, zb)], axis=0))
                tsums.append(jnp.concatenate([jnp.sum(jnp.where(lo, prod, 0.0), axis=1, keepdims=True),
                                              jnp.sum(jnp.where(hi, prod, 0.0), axis=1, keepdims=True)], axis=0))

            def step(jj, cs, rest, diag=False):
                outs = [accumulate(pr, qss[pr], doss[pr], tsums[pr], qi - jj,
                                   *scores(pr, qss[pr], doss[pr], qi - jj),
                                   cs[pr], rest[pr], rest[SB_TILES + pr], diag) for pr in range(SB_TILES)]
                return tuple(o[0] for o in outs), tuple(o[1] for o in outs) + tuple(o[2] for o in outs)

            zc = (jnp.zeros((2 * t, 1), F32),) * SB_TILES
            cs, rest = step(0, zc, zc + (jnp.zeros((2 * t, LANES), F32),) * SB_TILES, True)
            rest = _sb_sweep(qi, cs, rest, step)
            for pr in range(SB_TILES):
                dq = rest[SB_TILES + pr]
                dq_ref[pl.ds(q0, t), pr * LANES:(pr + 1) * LANES] = (jnp.where(lo, dq[:t], dq[t:]) * scale).astype(BF16)
            return carry

        lax.fori_loop(0, nq, q_body, 0)
        dk_ref[...] = dk_s[...].astype(BF16)
        dv_ref[...] = dv_s[...].astype(BF16)

    w = SB_TILES * LANES
    nblk = D_MODEL // w
    blk = lambda off: pl.BlockSpec((s, w), lambda p: (0, off * nblk + p))
    return pl.pallas_call(
        body, name=name, grid=(nblk,),
        in_specs=[blk(0), blk(1), blk(2), blk(0), blk(0)],
        out_specs=[blk(0), blk(0), blk(0)],
        out_shape=[SDS((s, D_MODEL), BF16)] * 3,
        scratch_shapes=[pltpu.VMEM((s, w), F32), pltpu.VMEM((s, w), F32)],
        compiler_params=_cparams(("parallel",), vmem_mb=56),
    )(proj, proj, proj, o, do)


MLA_HEADS = 8
MLA_QK = 192
MLA_SCALE = 1.0 / math.sqrt(MLA_QK)
MLA_TK = 512


def _rope_tables(s):
    inv_freq = 10000.0 ** (-jnp.arange(32, dtype=F32) / 32)
    ang = jnp.arange(s, dtype=F32)[:, None] * inv_freq[None, :]
    cos = jnp.tile(jnp.cos(ang), (1, 4))
    sin = jnp.sin(ang)
    sin_signed = jnp.tile(jnp.concatenate([-sin, sin], axis=1), (1, 2))
    return cos, sin_signed


def _partner(u, lane):
    return jnp.where((lane % HALF) < 32, pltpu.roll(u, 96, 1), pltpu.roll(u, 32, 1))


def _mla_head_inputs(h, qa_ref, kva_ref, kpe, lane):
    a = h % 2
    hm = (lane < HALF) if a == 0 else (lane >= HALF)
    qn = qa_ref[:, h * 128:(h + 1) * 128].astype(F32)
    qr = jnp.where(hm, qa_ref[:, 1024 + (h // 2) * 128:1024 + (h // 2 + 1) * 128].astype(F32), 0.0)
    kn = kva_ref[:, h * 256:h * 256 + 128].astype(F32)
    kr = kpe if a == 0 else pltpu.roll(kpe, HALF, 1)
    return qn, qr, kn, kr


def mla_prep_fwd(qa, kva, proj, gq, gk, cos, sin, name):
    s = qa.shape[0]
    tm = min(256, s)

    def body(qa_ref, kva_ref, kpe_ref, gq_ref, gk_ref, cos_ref, sin_ref, q_out, k_out):
        lane = lax.broadcasted_iota(jnp.int32, (1, LANES), 1)
        kpe = kpe_ref[...].astype(F32)
        cs, sn = cos_ref[...], sin_ref[...]
        for h in range(MLA_HEADS):
            qn, qr, kn, kr = _mla_head_inputs(h, qa_ref, kva_ref, kpe, lane)
            for xn, xr, g_ref, out, sc in ((qn, qr, gq_ref, q_out, MLA_SCALE), (kn, kr, gk_ref, k_out, 1.0)):
                ss = jnp.sum(xn * xn, axis=1, keepdims=True) + jnp.sum(xr * xr, axis=1, keepdims=True)
                rstd = lax.rsqrt(ss * (1.0 / MLA_QK) + NORM_EPS) * sc
                yn = xn * rstd * g_ref[:, 0:128]
                ur = xr * rstd * g_ref[:, 128:256]
                yr = ur * cs + _partner(ur, lane) * sn
                out[:, h * 256:h * 256 + 128] = yn.astype(BF16)
                out[:, h * 256 + 128:h * 256 + 256] = yr.astype(BF16)

    row = lambda w, c: pl.BlockSpec((tm, w), lambda i: (i, c))
    full = lambda w: pl.BlockSpec((1, w), lambda i: (0, 0))
    return pl.pallas_call(
        body, name=name, grid=(s // tm,),
        in_specs=[row(1536, 0), row(2048, 0), row(128, 11), full(256), full(256), row(128, 0), row(128, 0)],
        out_specs=[row(2048, 0), row(2048, 0)],
        out_shape=[SDS((s, 2048), BF16)] * 2,
        compiler_params=_cparams(("parallel",)),
    )(qa, kva, proj, gq, gk, cos, sin)


def mla_prep_bwd(dq_p, dk_p, dv, qa, kva, proj, gq, gk, cos, sin, name):
    s = qa.shape[0]
    tm = min(256, s)

    def body(dq_ref, dk_ref, dv_ref, qa_ref, kva_ref, kpe_ref, gq_ref, gk_ref, cos_ref, sin_ref,
             dqa_ref, dkva_ref, dkpe_ref, dgq_ref, dgk_ref):
        @pl.when(pl.program_id(0) == 0)
        def _():
            dgq_ref[...] = jnp.zeros_like(dgq_ref)
            dgk_ref[...] = jnp.zeros_like(dgk_ref)

        lane = lax.broadcasted_iota(jnp.int32, (1, LANES), 1)
        kpe = kpe_ref[...].astype(F32)
        cs, sn = cos_ref[...], sin_ref[...]
        dkpe = jnp.zeros((tm, LANES), F32)
        dqr_pair = None
        for h in range(MLA_HEADS):
            qn, qr, kn, kr = _mla_head_inputs(h, qa_ref, kva_ref, kpe, lane)
            res = []
            for xn, xr, g_ref, d_ref, dg_ref, sc in ((qn, qr, gq_ref, dq_ref, dgq_ref, MLA_SCALE),
                                                     (kn, kr, gk_ref, dk_ref, dgk_ref, 1.0)):
                ss = jnp.sum(xn * xn, axis=1, keepdims=True) + jnp.sum(xr * xr, axis=1, keepdims=True)
                rstd = lax.rsqrt(ss * (1.0 / MLA_QK) + NORM_EPS)
                hn, hr = xn * rstd, xr * rstd
                dyn = d_ref[:, h * 256:h * 256 + 128] * sc
                dyr = d_ref[:, h * 256 + 128:h * 256 + 256] * sc
                dur = dyr * cs - _partner(dyr, lane) * sn
                dg_ref[:, 0:128] += jnp.sum(dyn * hn, axis=0, keepdims=True)
                dg_ref[:, 128:256] += jnp.sum(dur * hr, axis=0, keepdims=True)
                dhn = dyn * g_ref[:, 0:128]
                dhr = dur * g_ref[:, 128:256]
                mu = (jnp.sum(dhn * hn, axis=1, keepdims=True)
                      + jnp.sum(dhr * hr, axis=1, keepdims=True)) * (1.0 / MLA_QK)
                res.append((rstd * (dhn - hn * mu), rstd * (dhr - hr * mu)))
            (dqn, dqr), (dkn, dkr) = res
            dqa_ref[:, h * 128:(h + 1) * 128] = dqn.astype(BF16)
            if h % 2 == 0:
                dqr_pair = dqr
            else:
                dqa_ref[:, 1024 + (h // 2) * 128:1024 + (h // 2 + 1) * 128] = (dqr_pair + dqr).astype(BF16)
            dkva_ref[:, h * 256:h * 256 + 128] = dkn.astype(BF16)
            dkva_ref[:, h * 256 + 128:h * 256 + 256] = dv_ref[:, h * 128:(h + 1) * 128]
            dkpe = dkpe + (dkr if h % 2 == 0 else pltpu.roll(dkr, HALF, 1))
        dkpe_ref[...] = dkpe.astype(BF16)

    row = lambda w, c: pl.BlockSpec((tm, w), lambda i: (i, c))
    full = lambda w: pl.BlockSpec((1, w), lambda i: (0, 0))
    return pl.pallas_call(
        body, name=name, grid=(s // tm,),
        in_specs=[row(2048, 0), row(2048, 0), row(1024, 0), row(1536, 0), row(2048, 0), row(128, 11),
                  full(256), full(256), row(128, 0), row(128, 0)],
        out_specs=[row(1536, 0), row(2048, 0), row(128, 0), full(256), full(256)],
        out_shape=[SDS((s, 1536), BF16), SDS((s, 2048), BF16), SDS((s, 128), BF16),
                   SDS((1, 256), F32), SDS((1, 256), F32)],
        compiler_params=_cparams(("arbitrary",)),
    )(dq_p, dk_p, dv, qa, kva, proj, gq, gk, cos, sin)


def mla_attn_fwd(q_p, k_p, kva, name):
    s = q_p.shape[0]
    tk = min(MLA_TK, s)
    tq = min(512, s)
    r = tq // tk
    nq = s // tq

    def body(q_ref, k_ref, va_ref, vb_ref, o_ref, lse_ref):
        row = lax.broadcasted_iota(jnp.int32, (tq, tk), 0)
        col = lax.broadcasted_iota(jnp.int32, (tq, tk), 1)
        v_refs = (va_ref, vb_ref)

        def tile(hh, q, kj, m, l, acc, mask):
            k0 = pl.multiple_of(kj * tk, tk)
            sc = _dot_nt(q, k_ref[pl.ds(k0, tk), hh * 256:(hh + 1) * 256])
            if mask is not None:
                sc = jnp.where(mask, sc, -1e30)
            m_new = jnp.maximum(m, jnp.max(sc, axis=1, keepdims=True))
            alpha = jnp.exp(m - m_new)
            p = jnp.exp(sc - m_new)
            l = alpha * l + jnp.sum(p, axis=1, keepdims=True)
            acc = alpha * acc + _dot(p.astype(BF16), v_refs[hh][pl.ds(k0, tk), :])
            return m_new, l, acc

        def q_body(qi, carry):
            q0 = pl.multiple_of(qi * tq, tq)
            qs = [q_ref[pl.ds(q0, tq), hh * 256:(hh + 1) * 256] for hh in range(2)]

            def step(kj, cr, mask=None):
                out = ()
                for hh in range(2):
                    out += tile(hh, qs[hh], kj, cr[3 * hh], cr[3 * hh + 1], cr[3 * hh + 2], mask)
                return out

            cr = (jnp.full((tq, 1), -1e30, F32), jnp.zeros((tq, 1), F32), jnp.zeros((tq, LANES), F32)) * 2
            cr = lax.fori_loop(0, qi * r, step, cr)
            for d in range(r):
                cr = step(qi * r + d, cr, col + d * tk <= row)
            for hh in range(2):
                m, l, acc = cr[3 * hh:3 * hh + 3]
                o_ref[pl.ds(q0, tq), hh * 128:(hh + 1) * 128] = acc / l
                lse_ref[pl.ds(q0, tq), hh * 128:(hh + 1) * 128] = jnp.broadcast_to(m + jnp.log(l), (tq, LANES))
            return carry

        lax.fori_loop(0, nq, q_body, 0)

    return pl.pallas_call(
        body, name=name, grid=(MLA_HEADS // 2,),
        in_specs=[pl.BlockSpec((s, 512), lambda g: (0, g)),
                  pl.BlockSpec((s, 512), lambda g: (0, g)),
                  pl.BlockSpec((s, 128), lambda g: (0, 4 * g + 1)),
                  pl.BlockSpec((s, 128), lambda g: (0, 4 * g + 3))],
        out_specs=[pl.BlockSpec((s, 256), lambda g: (0, g)),
                   pl.BlockSpec((s, 256), lambda g: (0, g))],
        out_shape=[SDS((s, 1024), F32), SDS((s, 1024), F32)],
        compiler_params=_cparams(("parallel",), vmem_mb=56),
    )(q_p, k_p, kva, kva)


def mla_attn_bwd(q_p, k_p, kva, o, do, lse, name):
    s = q_p.shape[0]
    tk = min(MLA_TK, s)
    tq = min(512, s)
    r = tq // tk
    nq = s // tq

    def body(q_ref, k_ref, v_ref, o_ref, do_ref, lse_ref, dq_ref, dk_ref, dv_ref, dv_s):
        dk_ref[...] = jnp.zeros_like(dk_ref)
        dv_s[...] = jnp.zeros_like(dv_s)
        row = lax.broadcasted_iota(jnp.int32, (tq, tk), 0)
        col = lax.broadcasted_iota(jnp.int32, (tq, tk), 1)

        def tile(q, dob, lse, delta, kj, dq, mask):
            k0 = pl.multiple_of(kj * tk, tk)
            k = k_ref[pl.ds(k0, tk), :]
            v = v_ref[pl.ds(k0, tk), :]
            p = jnp.exp(_dot_nt(q, k) - lse)
            if mask is not None:
                p = jnp.where(mask, p, 0.0)
            ds = (p * (_dot_nt(dob, v) - delta)).astype(BF16)
            dq = dq + _dot(ds, k)
            dk_ref[pl.ds(k0, tk), :] += _dot_tn(ds, q)
            dv_s[pl.ds(k0, tk), :] += _dot_tn(p.astype(BF16), dob)
            return dq

        def q_body(qi, carry):
            q0 = pl.multiple_of(qi * tq, tq)
            q = q_ref[pl.ds(q0, tq), :]
            dob = do_ref[pl.ds(q0, tq), :]
            lse = lse_ref[pl.ds(q0, tq), 0:1]
            delta = jnp.sum(dob.astype(F32) * o_ref[pl.ds(q0, tq), :], axis=1, keepdims=True)
            dq = lax.fori_loop(0, qi * r, lambda kj, dq: tile(q, dob, lse, delta, kj, dq, None),
                               jnp.zeros((tq, 256), F32))
            for d in range(r):
                dq = tile(q, dob, lse, delta, qi * r + d, dq, col + d * tk <= row)
            dq_ref[pl.ds(q0, tq), :] = dq
            return carry

        lax.fori_loop(0, nq, q_body, 0)
        dv_ref[...] = dv_s[...].astype(BF16)

    b256 = pl.BlockSpec((s, 256), lambda h: (0, h))
    b128 = pl.BlockSpec((s, 128), lambda h: (0, h))
    return pl.pallas_call(
        body, name=name, grid=(MLA_HEADS,),
        in_specs=[b256, b256, pl.BlockSpec((s, 128), lambda h: (0, 2 * h + 1)), b128, b128, b128],
        out_specs=[b256, b256, b128],
        out_shape=[SDS((s, 2048), F32), SDS((s, 2048), F32), SDS((s, 1024), BF16)],
        scratch_shapes=[pltpu.VMEM((s, 128), F32)],
        compiler_params=_cparams(("parallel",), vmem_mb=56),
    )(q_p, k_p, kva, o, do, lse)


SWA_HEADS = 16
SWA_BLOCK = 128


def _swa_head(tile_idx, half):
    return (2 * (tile_idx // 4) + half) * 4 + tile_idx % 4


SWA_Q_PERM = np.concatenate([np.arange(_swa_head(tt, a) * HALF, (_swa_head(tt, a) + 1) * HALF)
                             for tt in range(8) for a in range(2)])
SWA_SLOPES = [2.0 ** (-8.0 * (h + 1) / SWA_HEADS) for h in range(SWA_HEADS)]


def swa_prep_fwd(proj, g128, name):
    s = proj.shape[0]
    tm = min(512, s)

    def body(q_ref, k_ref, g_ref, o_ref):
        lane = lax.broadcasted_iota(jnp.int32, (1, LANES), 1)
        lo = lane < HALF
        for tt in range(10):
            if tt < 8:
                x = q_ref[:, tt * 128:(tt + 1) * 128].astype(F32)
            else:
                x = k_ref[:, (tt - 8) * 128:(tt - 7) * 128].astype(F32)
            sq = x * x
            sa = jnp.sum(jnp.where(lo, sq, 0.0), axis=1, keepdims=True)
            sb = jnp.sum(sq, axis=1, keepdims=True) - sa
            rstd = jnp.where(lo, lax.rsqrt(sa * (1.0 / HALF) + NORM_EPS), lax.rsqrt(sb * (1.0 / HALF) + NORM_EPS))
            gi = 0 if tt < 8 else 1
            o_ref[:, tt * 128:(tt + 1) * 128] = (x * rstd * g_ref[gi:gi + 1, :]).astype(BF16)

    return pl.pallas_call(
        body, name=name, grid=(s // tm,),
        in_specs=[pl.BlockSpec((tm, 1024), lambda i: (i, 0)), pl.BlockSpec((tm, 256), lambda i: (i, 8)),
                  pl.BlockSpec((2, 128), lambda i: (0, 0))],
        out_specs=pl.BlockSpec((tm, 1280), lambda i: (i, 0)),
        out_shape=SDS((s, 1280), BF16),
        compiler_params=_cparams(("parallel",)),
    )(proj, proj, g128)


def swa_prep_bwd(dqn, dkn, proj, g128, name):
    s = proj.shape[0]
    tm = min(512, s)
    nsteps = s // tm

    def body(dq_ref, dk_ref, xq_ref, xk_ref, g_ref, o_ref, dg_ref):
        @pl.when(pl.program_id(0) == 0)
        def _():
            dg_ref[...] = jnp.zeros_like(dg_ref)

        lane = lax.broadcasted_iota(jnp.int32, (1, LANES), 1)
        lo = lane < HALF
        for tt in range(10):
            if tt < 8:
                x = xq_ref[:, tt * 128:(tt + 1) * 128].astype(F32)
                dy = dq_ref[:, tt * 128:(tt + 1) * 128].astype(F32)
            else:
                x = xk_ref[:, (tt - 8) * 128:(tt - 7) * 128].astype(F32)
                dy = dk_ref[:, (tt - 8) * 128:(tt - 7) * 128].astype(F32)
            gi = 0 if tt < 8 else 1
            sq = x * x
            sa = jnp.sum(jnp.where(lo, sq, 0.0), axis=1, keepdims=True)
            sb = jnp.sum(sq, axis=1, keepdims=True) - sa
            rstd = jnp.where(lo, lax.rsqrt(sa * (1.0 / HALF) + NORM_EPS), lax.rsqrt(sb * (1.0 / HALF) + NORM_EPS))
            xh = x * rstd
            dg_ref[gi:gi + 1, :] += jnp.sum(dy * xh, axis=0, keepdims=True)
            dxh = dy * g_ref[gi:gi + 1, :]
            pr = dxh * xh
            ma = jnp.sum(jnp.where(lo, pr, 0.0), axis=1, keepdims=True)
            mb = jnp.sum(pr, axis=1, keepdims=True) - ma
            mu = jnp.where(lo, ma, mb) * (1.0 / HALF)
            o_ref[:, tt * 128:(tt + 1) * 128] = (rstd * (dxh - xh * mu)).astype(BF16)

        @pl.when(pl.program_id(0) == nsteps - 1)
        def _():
            acc = dg_ref[...]
            dg_ref[...] = acc + pltpu.roll(acc, HALF, 1)

    return pl.pallas_call(
        body, name=name, grid=(nsteps,),
        in_specs=[pl.BlockSpec((tm, 1024), lambda i: (i, 0)), pl.BlockSpec((tm, 256), lambda i: (i, 0)),
                  pl.BlockSpec((tm, 1024), lambda i: (i, 0)), pl.BlockSpec((tm, 256), lambda i: (i, 8)),
                  pl.BlockSpec((2, 128), lambda i: (0, 0))],
        out_specs=[pl.BlockSpec((tm, 1280), lambda i: (i, 0)), pl.BlockSpec((2, 128), lambda i: (0, 0))],
        out_shape=[SDS((s, 1280), BF16), SDS((2, 128), F32)],
        compiler_params=_cparams(("arbitrary",)),
    )(dqn, dkn, proj, proj, g128)


SWA_QSCALE = 1.0 / math.sqrt(HALF)


def _swa_bias():
    b = SWA_BLOCK
    row = jnp.arange(b)[:, None]
    col = jnp.arange(2 * b)[None, :]
    rel = row + b - col
    valid = (rel >= 0) & (rel < b)
    slopes = jnp.asarray([SWA_SLOPES[_swa_head(tt, a)] for tt in range(8) for a in range(2)], F32)
    bias = jnp.where(valid[None], -slopes[:, None, None] * rel[None].astype(F32), -1e30)
    return jnp.stack([jnp.where((col >= b)[None], bias, -1e30), bias])


def _swa_tile_inputs(tt, q_ref, bias_ref, sink_ref, lane):
    b = SWA_BLOCK
    qt = q_ref[:, tt * 128:(tt + 1) * 128] * SWA_QSCALE
    zq = jnp.zeros_like(qt)
    qs = jnp.concatenate([jnp.where(lane < HALF, qt, zq), jnp.where(lane >= HALF, qt, zq)], axis=0)
    bias = jnp.concatenate([bias_ref[0, 2 * tt], bias_ref[0, 2 * tt + 1]], axis=0)
    sink = jnp.concatenate([jnp.full((b, 1), sink_ref[_swa_head(tt, a)], F32) for a in range(2)], axis=0)
    return qs, bias, sink


def _swa_softmax(qs, kt, bias, sink):
    sc = _dot_nt(qs, kt) + bias
    m = jnp.maximum(jnp.max(sc, axis=1, keepdims=True), sink)
    e = jnp.exp(sc - m)
    es = jnp.exp(sink - m)
    inv = 1.0 / (jnp.sum(e, axis=1, keepdims=True) + es)
    return e * inv, es * inv


def _swa_specs(b):
    cur = lambda w, c: pl.BlockSpec((b, w), lambda n: (n, c))
    prev = lambda w, c: pl.BlockSpec((b, w), lambda n: (jnp.maximum(n - 1, 0), c))
    bias = pl.BlockSpec((1, SWA_HEADS, b, 2 * b), lambda n: (jnp.minimum(n, 1), 0, 0, 0))
    return cur, prev, bias


def swa_attn_fwd(qkn, proj, sinks, bias, name):
    s = qkn.shape[0]
    b = SWA_BLOCK

    def body(sink_ref, bias_ref, q_ref, kc_ref, kp_ref, vc_ref, vp_ref, o_ref):
        lane = lax.broadcasted_iota(jnp.int32, (1, LANES), 1)
        for tt in range(8):
            gp = tt // 4
            kt = jnp.concatenate([kp_ref[:, gp * 128:(gp + 1) * 128], kc_ref[:, gp * 128:(gp + 1) * 128]], axis=0)
            vt = jnp.concatenate([vp_ref[:, gp * 128:(gp + 1) * 128], vc_ref[:, gp * 128:(gp + 1) * 128]], axis=0)
            qt = q_ref[:, tt * 128:(tt + 1) * 128] * SWA_QSCALE
            outs = []
            for a in range(2):
                qm = jnp.where((lane < HALF) if a == 0 else (lane >= HALF), qt, jnp.zeros_like(qt))
                p, _ = _swa_softmax(qm, kt, bias_ref[0, 2 * tt + a], sink_ref[_swa_head(tt, a)])
                outs.append(_dot(p.astype(BF16), vt))
            o_ref[:, tt * 128:(tt + 1) * 128] = jnp.where(lane < HALF, outs[0], outs[1])

    cur, prev, bias_spec = _swa_specs(b)
    return pl.pallas_call(
        body, name=name, grid=(s // b,),
        in_specs=[pl.BlockSpec(memory_space=pltpu.SMEM), bias_spec, cur(1024, 0), cur(256, 4), prev(256, 4),
                  cur(256, 9), prev(256, 9)],
        out_specs=cur(1024, 0),
        out_shape=SDS((s, 1024), F32),
        compiler_params=_cparams(("parallel",)),
    )(sinks, bias, qkn, qkn, qkn, proj, proj)


def swa_attn_bwd(qkn, proj, sinks, bias, do, name):
    s = qkn.shape[0]
    b = SWA_BLOCK

    def body(sink_ref, bias_ref, q_ref, kc_ref, kp_ref, vc_ref, vp_ref, do_ref, dq_ref, dk_ref, dv_ref, ds_ref):
        n = pl.program_id(0)

        @pl.when(n == 0)
        def _():
            dk_ref[...] = jnp.zeros_like(dk_ref)
            dv_ref[...] = jnp.zeros_like(dv_ref)
            ds_ref[...] = jnp.zeros_like(ds_ref)

        lane = lax.broadcasted_iota(jnp.int32, (1, LANES), 1)
        cur0 = pl.multiple_of(n * b, b)
        prev0 = pl.multiple_of(jnp.maximum(n - 1, 0) * b, b)
        dsink = jnp.zeros((1, LANES), F32)
        for gp in range(2):
            kt = jnp.concatenate([kp_ref[:, gp * 128:(gp + 1) * 128], kc_ref[:, gp * 128:(gp + 1) * 128]], axis=0)
            vt = jnp.concatenate([vp_ref[:, gp * 128:(gp + 1) * 128], vc_ref[:, gp * 128:(gp + 1) * 128]], axis=0)
            dkt = jnp.zeros((2 * b, LANES), F32)
            dvt = jnp.zeros((2 * b, LANES), F32)
            for tt in range(4 * gp, 4 * gp + 4):
                qs, bias2, sink = _swa_tile_inputs(tt, q_ref, bias_ref, sink_ref, lane)
                dot_ = do_ref[:, tt * 128:(tt + 1) * 128]
                zd = jnp.zeros_like(dot_)
                dos = jnp.concatenate([jnp.where(lane < HALF, dot_, zd), jnp.where(lane >= HALF, dot_, zd)], axis=0)
                p, ps = _swa_softmax(qs, kt, bias2, sink)
                dp = _dot_nt(dos, vt)
                delta = jnp.sum(p * dp, axis=1, keepdims=True)
                dz = (p * (dp - delta)).astype(BF16)
                sd = ps * delta
                for a in range(2):
                    dsink = dsink + jnp.where(lane == _swa_head(tt, a), -jnp.sum(sd[a * b:(a + 1) * b]), 0.0)
                dq2 = _dot(dz, kt)
                dq_ref[:, tt * 128:(tt + 1) * 128] = (jnp.where(lane < HALF, dq2[:b], dq2[b:]) * SWA_QSCALE).astype(BF16)
                dkt = dkt + _dot_tn(dz, qs)
                dvt = dvt + _dot_tn(p.astype(BF16), dos)
            cols = slice(gp * 128, (gp + 1) * 128)
            dk_ref[pl.ds(prev0, b), cols] += dkt[:b]
            dk_ref[pl.ds(cur0, b), cols] += dkt[b:]
            dv_ref[pl.ds(prev0, b), cols] += dvt[:b]
            dv_ref[pl.ds(cur0, b), cols] += dvt[b:]
        ds_ref[0:1, :] += dsink

    cur, prev, bias_spec = _swa_specs(b)
    full = pl.BlockSpec((s, 256), lambda n: (0, 0))
    return pl.pallas_call(
        body, name=name, grid=(s // b,),
        in_specs=[pl.BlockSpec(memory_space=pltpu.SMEM), bias_spec, cur(1024, 0), cur(256, 4), prev(256, 4),
                  cur(256, 9), prev(256, 9), cur(1024, 0)],
        out_specs=[cur(1024, 0), full, full, pl.BlockSpec((8, LANES), lambda n: (0, 0))],
        out_shape=[SDS((s, 1024), BF16), SDS((s, 256), F32), SDS((s, 256), F32), SDS((8, LANES), F32)],
        compiler_params=_cparams(("arbitrary",)),
    )(sinks, bias, qkn, qkn, qkn, proj, proj, do)


def _my_pos():
    return lax.axis_index("x"), lax.axis_index("y"), lax.axis_index("c")


ANY = pl.BlockSpec(memory_space=pl.ANY)


def all_gather_packed(xs, name):
    r, n = xs.shape

    def body(x_ref, out_ref, send_sems, recv_sems, local_sem):
        x, y, c = _my_pos()
        me, sibling = (x, y, c), (x, y, 1 - c)
        chips = [(1 - x, y), (x, 1 - y), (1 - x, 1 - y)]

        def slot(px, py, pc):
            return out_ref.at[4 * px + 2 * py + pc]

        def copy(k, block, to, src=None):
            return pltpu.make_async_remote_copy(
                src_ref=slot(*block) if src is None else src, dst_ref=slot(*block),
                send_sem=send_sems.at[k], recv_sem=recv_sems.at[k],
                device_id=to, device_id_type=MESH_ID)

        mine = pltpu.make_async_copy(x_ref, slot(*me), local_sem)
        mine.start()
        first = [copy(0, me, sibling, src=x_ref)]
        first += [copy(1 + j, me, (*chip, c), src=x_ref) for j, chip in enumerate(chips)]
        for cp in first:
            cp.start()
        passed = [copy(4 + j, (*chip, c), sibling) for j, chip in enumerate(chips)]
        for j, chip in enumerate(chips):
            copy(1 + j, (*chip, c), me).wait_recv()
            passed[j].start()
        copy(0, sibling, me).wait_recv()
        for j, chip in enumerate(chips):
            copy(4 + j, (*chip, 1 - c), me).wait_recv()
        for cp in first + passed:
            cp.wait_send()
        mine.wait()

    return pl.pallas_call(
        body, name=name,
        out_shape=SDS((N_DEV, r, n), xs.dtype),
        in_specs=[ANY], out_specs=ANY,
        scratch_shapes=[pltpu.SemaphoreType.DMA((7,)), pltpu.SemaphoreType.DMA((7,)), pltpu.SemaphoreType.DMA],
    )(xs)


def rs_sibling_exchange(g, name):
    _, r, n = g.shape

    def body(g_ref, ra_ref, send_sems, recv_sems):
        x, y, c = _my_pos()
        copies = [pltpu.make_async_remote_copy(
            src_ref=g_ref.at[2 * chip + (1 - c)], dst_ref=ra_ref.at[chip],
            send_sem=send_sems.at[chip], recv_sem=recv_sems.at[chip],
            device_id=(x, y, 1 - c), device_id_type=MESH_ID) for chip in range(4)]
        for cp in copies:
            cp.start()
        for cp in copies:
            cp.wait()

    return pl.pallas_call(
        body, name=name, out_shape=SDS((4, r, n), g.dtype), in_specs=[ANY], out_specs=ANY,
        scratch_shapes=[pltpu.SemaphoreType.DMA((4,)), pltpu.SemaphoreType.DMA((4,))],
    )(g)


def rs_pair_add(g, ra, cidx, name):
    _, r, n = g.shape
    tr = r // 8

    def body(c_ref, g_ref, ra_ref, p_ref):
        p_ref[...] = (g_ref[...].astype(F32) + ra_ref[...].astype(F32)).astype(p_ref.dtype)

    return pl.pallas_call(
        body, name=name,
        grid_spec=pltpu.PrefetchScalarGridSpec(
            num_scalar_prefetch=1, grid=(4, r // tr),
            in_specs=[pl.BlockSpec((1, tr, n), lambda i, j, c: (2 * i + c[0], j, 0)),
                      pl.BlockSpec((1, tr, n), lambda i, j, c: (i, j, 0))],
            out_specs=pl.BlockSpec((1, tr, n), lambda i, j, c: (i, j, 0))),
        out_shape=SDS((4, r, n), g.dtype),
        compiler_params=_cparams(("parallel", "parallel")),
    )(cidx, g, ra)


def rs_cross_exchange(p, name):
    _, r, n = p.shape

    def body(p_ref, rb_ref, send_sems, recv_sems):
        x, y, c = _my_pos()
        copies = []
        for k, (dx, dy) in enumerate(((1, 0), (0, 1), (1, 1))):
            tx = 1 - x if dx else x
            ty = 1 - y if dy else y
            copies.append(pltpu.make_async_remote_copy(
                src_ref=p_ref.at[2 * tx + ty], dst_ref=rb_ref.at[k],
                send_sem=send_sems.at[k], recv_sem=recv_sems.at[k],
                device_id=(tx, ty, c), device_id_type=MESH_ID))
        for cp in copies:
            cp.start()
        for cp in copies:
            cp.wait()

    return pl.pallas_call(
        body, name=name, out_shape=SDS((3, r, n), p.dtype), in_specs=[ANY], out_specs=ANY,
        scratch_shapes=[pltpu.SemaphoreType.DMA((3,)), pltpu.SemaphoreType.DMA((3,))],
    )(p)


def _adamw(w, g, m, v):
    m = ADAM_B1 * m + (1.0 - ADAM_B1) * g
    v = ADAM_B2 * v + (1.0 - ADAM_B2) * (g * g)
    m_hat = m / (1.0 - ADAM_B1 ** ADAM_STEP)
    v_hat = v / (1.0 - ADAM_B2 ** ADAM_STEP)
    delta = -ADAM_LR * (m_hat / (jnp.sqrt(v_hat) + ADAM_EPS) + ADAM_WD * w)
    return delta, m, v


def rs_final_adamw(p, rb, chipidx, w, m, v, row0, col_blk, name):
    r, n = w.shape
    tr = min(r, 256 if n <= PAIR_COLS else 128)
    assert r % tr == 0 and row0 % tr == 0
    rb0 = row0 // tr

    def body(c_ref, p_ref, rb_ref, w_ref, m_ref, v_ref, g_out, d_out, m_out, v_out):
        g = p_ref[0].astype(F32)
        for k in range(3):
            g = g + rb_ref[k].astype(F32)
        d, mn, vn = _adamw(w_ref[...], g, m_ref[...], v_ref[...])
        g_out[...] = g
        d_out[...] = d
        m_out[...] = mn
        v_out[...] = vn

    blk = pl.BlockSpec((tr, n), lambda j, c: (j, 0))
    return pl.pallas_call(
        body, name=name,
        grid_spec=pltpu.PrefetchScalarGridSpec(
            num_scalar_prefetch=1, grid=(r // tr,),
            in_specs=[pl.BlockSpec((1, tr, n), lambda j, c: (c[0], rb0 + j, col_blk)),
                      pl.BlockSpec((3, tr, n), lambda j, c: (0, rb0 + j, col_blk)), blk, blk, blk],
            out_specs=[blk, blk, blk, blk]),
        out_shape=[SDS((r, n), F32)] * 4,
        compiler_params=_cparams(("parallel",)),
    )(chipidx, p, rb, w, m, v)


def small_allreduce_adamw(g, w, m, v, name):
    rows = g.shape[0]

    def body(g_ref, w_ref, m_ref, v_ref, g_out, d_out, m_out, v_out, buf, send_sems, recv_sems):
        x, y, c = _my_pos()
        my = 4 * x + 2 * y + c
        buf[my] = g_ref[...]
        copies = []
        for k in range(1, N_DEV):
            dx, dy, dc = (k >> 2) & 1, (k >> 1) & 1, k & 1
            tgt = (1 - x if dx else x, 1 - y if dy else y, 1 - c if dc else c)
            copies.append(pltpu.make_async_remote_copy(
                src_ref=buf.at[my], dst_ref=buf.at[my],
                send_sem=send_sems.at[k - 1], recv_sem=recv_sems.at[k - 1],
                device_id=tgt, device_id_type=MESH_ID))
        for cp in copies:
            cp.start()
        for cp in copies:
            cp.wait()
        tot = buf[0]
        for j in range(1, N_DEV):
            tot = tot + buf[j]
        d, mn, vn = _adamw(w_ref[...], tot, m_ref[...], v_ref[...])
        g_out[...] = tot
        d_out[...] = d
        m_out[...] = mn
        v_out[...] = vn

    vm = pl.BlockSpec(memory_space=pltpu.VMEM)
    return pl.pallas_call(
        body, name=name, out_shape=[SDS((rows, LANES), F32)] * 4,
        in_specs=[vm] * 4, out_specs=[vm] * 4,
        scratch_shapes=[pltpu.VMEM((N_DEV, rows, LANES), F32),
                        pltpu.SemaphoreType.DMA((N_DEV - 1,)), pltpu.SemaphoreType.DMA((N_DEV - 1,))],
    )(g, w, m, v)


BIG = ["l0_w_in", "l0_w_out", "l1_w_in", "l1_w_uq", "l1_w_ukv", "l1_w_out", "l2_w_in", "l2_w_out",
       "l3_w_in", "l3_w_out"]
SMALL = ["l0_norm", "l1_norm", "l1_q_a_norm", "l1_kv_a_norm", "l1_q_head_norm", "l1_k_head_norm",
         "l2_norm", "l2_q_head_norm", "l2_k_head_norm", "l2_sinks", "l3_norm"]
ROW_SHARDED = {"l0_w_out", "l1_w_out", "l2_w_out", "l3_w_out"}

MLA_UQ_PERM = np.concatenate([np.arange(h * 192, h * 192 + 128) for h in range(8)]
                             + [np.arange(h * 192 + 128, (h + 1) * 192) for h in range(8)])
SWA_IN_PERM = np.concatenate([SWA_Q_PERM, 1536 + SWA_Q_PERM, np.arange(1024, 1536)])


def _inv(perm):
    inv = np.empty_like(perm)
    inv[perm] = np.arange(perm.size)
    return inv


OUTS = ["l0_w_out", "l1_w_out", "l2_w_out", "l3_w_out"]
REST = ["l1_w_in", "l1_w_uq", "l1_w_ukv", "l2_w_in"]


def _pack_rest(shards):
    parts = [shards[nm].reshape(-1, D_MODEL) for nm in REST]
    used = sum(p.shape[0] for p in parts)
    assert REST_ROW0 + used == ROWS_USED
    parts.append(jnp.zeros((REST_ROWS - used, D_MODEL), parts[0].dtype))
    return jnp.concatenate(parts, axis=0)


def _unpack_rest(packed, shapes):
    out, off = {}, 0
    for nm in REST:
        shp = shapes[nm]
        rows = shp[0] * shp[1] // D_MODEL
        out[nm] = packed[off:off + rows].reshape(shp)
        off += rows
    return out


def _pack(shards):
    pair = jnp.concatenate([shards["l0_w_in"], shards["l3_w_in"]], axis=1)
    return jnp.concatenate([pair] + [shards[nm] for nm in OUTS] + [_pack_rest(shards)], axis=0)


def _out_weight(gathered, layer):
    r0 = OUT_ROW0 + 128 * layer
    return gathered[:, r0:r0 + 128].reshape(D_MODEL, D_MODEL)


def _rest_full_weights(gathered, shapes):
    out, off = {}, REST_ROW0
    for nm in REST:
        shp = shapes[nm]
        rows = shp[0] * shp[1] // D_MODEL
        blk = gathered[:, off:off + rows].reshape((N_DEV,) + shp)
        out[nm] = jnp.transpose(blk, (1, 0, 2)).reshape(shp[0], N_DEV * shp[1])
        off += rows
    return out


def _grad_pieces(pair_l0, pair_l3, dw_out, rest_grads, shapes):
    parts = [jnp.concatenate([pair_l0, pair_l3], axis=2)]
    parts += [g.reshape(N_DEV, 128, D_MODEL).astype(BF16) for g in dw_out]
    used = REST_ROW0
    for nm in REST:
        shp = shapes[nm]
        blk = jnp.transpose(rest_grads[nm].reshape(shp[0], N_DEV, shp[1]), (1, 0, 2))
        parts.append(blk.reshape(N_DEV, -1, D_MODEL).astype(BF16))
        used += parts[-1].shape[1]
    parts.append(jnp.zeros((N_DEV, ROWS_PACK - used, D_MODEL), BF16))
    return jnp.concatenate(parts, axis=1)


SMALL_USED = 5008


def _pack_small(vals, extra=None):
    parts = [vals[nm].reshape(-1).astype(F32) for nm in SMALL]
    assert sum(p.shape[0] for p in parts) == SMALL_USED
    if extra is not None:
        parts.append(extra.reshape(1).astype(F32))
    flat = jnp.concatenate(parts)
    flat = jnp.concatenate([flat, jnp.zeros((SMALL_ROWS * LANES - flat.shape[0],), F32)])
    return flat.reshape(SMALL_ROWS, LANES)


def _unpack_small(packed, shapes):
    flat = packed.reshape(-1)
    out, off = {}, 0
    for nm in SMALL:
        n = shapes[nm][0]
        out[nm] = flat[off:off + n]
        off += n
    return out


def _row(v):
    return v.reshape(1, -1).astype(F32)


def _mla_gain_rows(g):
    return jnp.concatenate([g[:128], g[128:], g[128:]]).reshape(1, 256).astype(F32)


def _layer_fwd_sb(x, norm, gathered, shard_col, w_out, tag):
    proj, xn = norm_matmul(x, 0, _row(norm), gathered, f"{tag}_in", shard_col=shard_col)
    o = sb_attn_fwd(proj, f"{tag}_attn")
    y = gate_out_fwd(o, proj, 3, w_out, x, f"{tag}_out")
    return y, (x, xn, proj, o)


def _layer_bwd_sb(dy, saved, norm, gathered, shard_col, w_out, tag):
    x, xn, proj, o = saved
    do, dgate, hg = gate_out_bwd(dy, w_out, o, proj, 3, f"{tag}_outb")
    dw_out = tn_matmul(hg, dy, f"{tag}_dwout")
    dq, dk, dv = sb_attn_bwd(proj, o, do, f"{tag}_attnb")
    dproj = [dq, dk, dv, dgate]
    dw_in = tn_matmul_shards(xn, dproj, f"{tag}_dwin")
    dx, dnorm = nt_norm_bwd(dproj, gathered, x, 0, _row(norm), dy, f"{tag}_inb", F32, shard_col=shard_col)
    return dx, dw_in, dw_out, dnorm.reshape(-1)


def kernel(x, l0_norm, l0_w_in, l0_w_out, l1_norm, l1_w_in, l1_q_a_norm, l1_w_uq, l1_kv_a_norm, l1_w_ukv, l1_q_head_norm, l1_k_head_norm, l1_w_out, l2_norm, l2_w_in, l2_q_head_norm, l2_k_head_norm, l2_sinks, l2_w_out, l3_norm, l3_w_in, l3_w_out, loss_target, m_l0_norm, m_l0_w_in, m_l0_w_out, m_l1_norm, m_l1_w_in, m_l1_q_a_norm, m_l1_w_uq, m_l1_kv_a_norm, m_l1_w_ukv, m_l1_q_head_norm, m_l1_k_head_norm, m_l1_w_out, m_l2_norm, m_l2_w_in, m_l2_q_head_norm, m_l2_k_head_norm, m_l2_sinks, m_l2_w_out, m_l3_norm, m_l3_w_in, m_l3_w_out, v_l0_norm, v_l0_w_in, v_l0_w_out, v_l1_norm, v_l1_w_in, v_l1_q_a_norm, v_l1_w_uq, v_l1_kv_a_norm, v_l1_w_ukv, v_l1_q_head_norm, v_l1_k_head_norm, v_l1_w_out, v_l2_norm, v_l2_w_in, v_l2_q_head_norm, v_l2_k_head_norm, v_l2_sinks, v_l2_w_out, v_l3_norm, v_l3_w_in, v_l3_w_out):
    loc = dict(locals())
    names = BIG + SMALL
    w = {nm: loc[nm] for nm in names}
    mom = {nm: loc["m_" + nm] for nm in names}
    vel = {nm: loc["v_" + nm] for nm in names}
    shapes = {nm: w[nm].shape for nm in names}
    xs = x[0]
    tgt = loss_target[0]

    gathered = all_gather_packed(_pack({nm: w[nm].astype(BF16) for nm in BIG}), "ag_weights")
    loss_local, dx0, pieces, sg = _local_step(xs, tgt, gathered, {nm: w[nm] for nm in SMALL}, shapes)
    return _reduce_and_update(loss_local, dx0, pieces, sg, w, mom, vel, shapes)


def _local_step(xs, tgt, gathered, sm, shapes):
    s = xs.shape[0]
    fw = _rest_full_weights(gathered, shapes)
    w_out = [_out_weight(gathered, layer) for layer in range(4)]
    l0_norm, l1_norm, l2_norm, l3_norm = sm["l0_norm"], sm["l1_norm"], sm["l2_norm"], sm["l3_norm"]
    l1_q_a_norm, l1_kv_a_norm = sm["l1_q_a_norm"], sm["l1_kv_a_norm"]
    l1_q_head_norm, l1_k_head_norm = sm["l1_q_head_norm"], sm["l1_k_head_norm"]
    l2_q_head_norm, l2_k_head_norm, l2_sinks = sm["l2_q_head_norm"], sm["l2_k_head_norm"], sm["l2_sinks"]
    w1_in = jnp.concatenate([fw["l1_w_in"][:, 448:], fw["l1_w_in"][:, :448], jnp.zeros((D_MODEL, 64), BF16)], axis=1)
    w1_uq = fw["l1_w_uq"][:, MLA_UQ_PERM]
    w2_in = fw["l2_w_in"][:, SWA_IN_PERM]
    w2_out = w_out[2][SWA_Q_PERM, :]

    x1, sv0 = _layer_fwd_sb(xs, l0_norm, gathered, 0, w_out[0], "l0")

    cos, sin = _rope_tables(s)
    gq, gk = _mla_gain_rows(l1_q_head_norm), _mla_gain_rows(l1_k_head_norm)
    proj1, xn1 = norm_matmul(x1, 0, _row(l1_norm), w1_in, "l1_in")
    qa, qln = norm_matmul(proj1, 4, _row(l1_q_a_norm), w1_uq, "l1_uq")
    kva, kvn = norm_matmul(proj1, 10, _row(l1_kv_a_norm), fw["l1_w_ukv"], "l1_ukv")
    q_p, k_p = mla_prep_fwd(qa, kva, proj1, gq, gk, cos, sin, "l1_prep")
    o1, lse1 = mla_attn_fwd(q_p, k_p, kva, "l1_attn")
    x2 = gate_out_fwd(o1, proj1, 0, w_out[1], x1, "l1_out")

    g2 = jnp.stack([jnp.tile(l2_q_head_norm, 2), jnp.tile(l2_k_head_norm, 2)]).astype(F32)
    proj2, xn2 = norm_matmul(x2, 0, _row(l2_norm), w2_in, "l2_in")
    qkn2 = swa_prep_fwd(proj2, g2, "l2_prep")
    bias2 = _swa_bias()
    o2 = swa_attn_fwd(qkn2, proj2, l2_sinks, bias2, "l2_attn")
    x3 = gate_out_fwd(o2, proj2, 1, w2_out, x2, "l2_out")

    x4, sv3 = _layer_fwd_sb(x3, l3_norm, gathered, 1, w_out[3], "l3")

    lossblk, dy = loss_kernel(x4, tgt, "loss")

    fg, sg = {}, {}
    dw_out = [None] * 4
    dx3, pair_l3, dw_out[3], sg["l3_norm"] = _layer_bwd_sb(dy, sv3, l3_norm, gathered, 1, w_out[3], "l3")

    do2, dgate2, hg2 = gate_out_bwd(dx3, w2_out, o2, proj2, 1, "l2_outb")
    dw2_out = tn_matmul(hg2, dx3, "l2_dwout")
    dqn2, dkn2, dv2, dsink2 = swa_attn_bwd(qkn2, proj2, l2_sinks, bias2, do2, "l2_attnb")
    dqk2, dg2 = swa_prep_bwd(dqn2, dkn2, proj2, g2, "l2_prepb")
    dproj2 = jnp.concatenate([dqk2[:, :1024], dgate2, dqk2[:, 1024:], dv2.astype(BF16)], axis=1)
    dw2_in = tn_matmul(xn2, dproj2, "l2_dwin")
    dx2, dn2 = nt_norm_bwd(dproj2, w2_in, x2, 0, _row(l2_norm), dx3, "l2_inb", F32)
    fg["l2_w_in"] = dw2_in[:, _inv(SWA_IN_PERM)]
    dw_out[2] = dw2_out[_inv(SWA_Q_PERM), :]
    sg["l2_norm"] = dn2.reshape(-1)
    sg["l2_q_head_norm"] = dg2[0, :HALF]
    sg["l2_k_head_norm"] = dg2[1, :HALF]
    sg["l2_sinks"] = dsink2[0, :SWA_HEADS]

    do1, dgate1, hg1 = gate_out_bwd(dx2, w_out[1], o1, proj1, 0, "l1_outb")
    dw_out[1] = tn_matmul(hg1, dx2, "l1_dwout")
    dq_p, dk_p, dv1 = mla_attn_bwd(q_p, k_p, kva, o1, do1, lse1, "l1_attnb")
    dqa, dkva, dkpe, dgq, dgk = mla_prep_bwd(dq_p, dk_p, dv1, qa, kva, proj1, gq, gk, cos, sin, "l1_prepb")
    dw_uq = tn_matmul(qln, dqa, "l1_dwuq")
    fg["l1_w_ukv"] = tn_matmul(kvn, dkva, "l1_dwukv")
    dqlat, dgqa = nt_norm_bwd(dqa, w1_uq, proj1, 4, _row(l1_q_a_norm), None, "l1_uqb", BF16)
    dkvlat, dgkva = nt_norm_bwd(dkva, fw["l1_w_ukv"], proj1, 10, _row(l1_kv_a_norm), None, "l1_ukvb", BF16)
    dproj1 = jnp.concatenate([dgate1, dqlat, dkvlat, dkpe], axis=1)
    dw1_in = tn_matmul(xn1, dproj1, "l1_dwin")
    dx1, dn1 = nt_norm_bwd(dproj1, w1_in, x1, 0, _row(l1_norm), dx2, "l1_inb", F32)
    fg["l1_w_in"] = jnp.concatenate([dw1_in[:, 1024:1472], dw1_in[:, :1024]], axis=1)
    fg["l1_w_uq"] = dw_uq[:, _inv(MLA_UQ_PERM)]
    sg["l1_norm"] = dn1.reshape(-1)
    sg["l1_q_a_norm"] = dgqa.reshape(-1)
    sg["l1_kv_a_norm"] = dgkva.reshape(-1)
    sg["l1_q_head_norm"] = jnp.concatenate([dgq[0, :128], dgq[0, 128:192] + dgq[0, 192:256]])
    sg["l1_k_head_norm"] = jnp.concatenate([dgk[0, :128], dgk[0, 128:192] + dgk[0, 192:256]])

    dx0, pair_l0, dw_out[0], sg["l0_norm"] = _layer_bwd_sb(dx1, sv0, l0_norm, gathered, 0, w_out[0], "l0")
    return lossblk[0, 0], dx0, _grad_pieces(pair_l0, pair_l3, dw_out, fg, shapes), sg


def _reduce_and_update(loss_local, dx0, pieces, sg, w, mom, vel, shapes):
    xi, yi, ci = _my_pos()
    cidx = jnp.reshape(ci, (1,)).astype(jnp.int32)
    chipidx = jnp.reshape(2 * xi + yi, (1,)).astype(jnp.int32)
    ra = rs_sibling_exchange(pieces, "rs_sibling")
    part = rs_pair_add(pieces, ra, cidx, "rs_pair_add")
    rb = rs_cross_exchange(part, "rs_cross")

    def update(nm, row0, col_blk):
        return rs_final_adamw(part, rb, chipidx, w[nm], mom[nm], vel[nm], row0, col_blk, f"adamw_{nm}")

    big = {"l0_w_in": update("l0_w_in", 0, 0), "l3_w_in": update("l3_w_in", 0, 1)}
    for layer, nm in enumerate(OUTS):
        big[nm] = update(nm, OUT_ROW0 + 128 * layer, 0)
    rest = rs_final_adamw(part, rb, chipidx, _pack_rest(w), _pack_rest(mom), _pack_rest(vel), REST_ROW0, 0,
                          "adamw_rest")
    gs, ds, ms, vs = small_allreduce_adamw(_pack_small(sg, loss_local), _pack_small({nm: w[nm] for nm in SMALL}),
                                           _pack_small({nm: mom[nm] for nm in SMALL}),
                                           _pack_small({nm: vel[nm] for nm in SMALL}), "small_allreduce")

    outs = []
    for i, small in enumerate((gs, ds, ms, vs)):
        d = {nm: big[nm][i] for nm in big}
        d.update(_unpack_rest(rest[i], shapes))
        d.update(_unpack_small(small, shapes))
        outs.append(d)
    order = ["l0_norm", "l0_w_in", "l0_w_out", "l1_norm", "l1_w_in", "l1_q_a_norm", "l1_w_uq", "l1_kv_a_norm",
             "l1_w_ukv", "l1_q_head_norm", "l1_k_head_norm", "l1_w_out", "l2_norm", "l2_w_in", "l2_q_head_norm",
             "l2_k_head_norm", "l2_sinks", "l2_w_out", "l3_norm", "l3_w_in", "l3_w_out"]
    flat = [gs.reshape(-1)[SMALL_USED], dx0[None]]
    for d in outs:
        flat += [d[nm] for nm in order]
    return tuple(flat)
```

```python
import math

import numpy as np
import jax
import jax.numpy as jnp
from jax import lax
from jax.experimental import pallas as pl
from jax.experimental.pallas import tpu as pltpu

F32 = jnp.float32
BF16 = jnp.bfloat16
SDS = jax.ShapeDtypeStruct
MESH_ID = pl.DeviceIdType.MESH

D_MODEL = 1024
NORM_EPS = 1e-6
N_DEV = 8
LANES = 128
HALF = 64
ROWS_PACK = 2176
ROWS_USED = 2120
PAIR_COLS = 512
OUT_ROW0 = 1024
REST_ROW0 = 1536
REST_ROWS = ROWS_PACK - REST_ROW0
SMALL_ROWS = 40

ADAM_LR = 0.001
ADAM_B1 = 0.9
ADAM_B2 = 0.999
ADAM_EPS = 1e-08
ADAM_WD = 0.01
ADAM_STEP = 10

NT_DIMS = (((1,), (1,)), ((), ()))
TN_DIMS = (((0,), (0,)), ((), ()))


def _cparams(sem=None, vmem_mb=48):
    return pltpu.CompilerParams(dimension_semantics=sem, vmem_limit_bytes=vmem_mb * 2 ** 20)


def _dot(a, b):
    return jnp.dot(a, b, preferred_element_type=F32)


def _dot_nt(a, b):
    return lax.dot_general(a, b, NT_DIMS, preferred_element_type=F32)


def _dot_tn(a, b):
    return lax.dot_general(a, b, TN_DIMS, preferred_element_type=F32)


def _split_dot(a, b):
    hi = a.astype(BF16)
    lo = (a - hi.astype(F32)).astype(BF16)
    return _dot(hi, b) + _dot(lo, b)


def _cumsum_dot(lf, u):
    return _dot(lf.astype(BF16), u)


def _pick_tile(n, cap):
    return max(t for t in range(LANES, min(n, cap) + 1, LANES) if n % t == 0)


def _sigmoid(x):
    return 1.0 / (1.0 + jnp.exp(-x))


def norm_matmul(x, colblk, g, w, name, shard_col=None):
    s = x.shape[0]
    if shard_col is None:
        k, n = w.shape
        tn = _pick_tile(n, 1024)
        w_spec = pl.BlockSpec((k, tn), lambda i, j: (0, j))
    else:
        k, n, tn = D_MODEL, N_DEV * PAIR_COLS, 2 * PAIR_COLS
        w_spec = pl.BlockSpec((2, k, PAIR_COLS), lambda i, j: (j, 0, shard_col))
    tm = min(1024, s)
    assert s % tm == 0

    def body(x_ref, g_ref, w_ref, o_ref, xn_ref, xn_s):
        @pl.when(pl.program_id(1) == 0)
        def _():
            xf = x_ref[...].astype(F32)
            ms = jnp.mean(xf * xf, axis=-1, keepdims=True)
            y = (xf * lax.rsqrt(ms + NORM_EPS) * g_ref[...]).astype(BF16)
            xn_s[...] = y
            xn_ref[...] = y

        if shard_col is None:
            o_ref[...] = _dot(xn_s[...], w_ref[...]).astype(o_ref.dtype)
        else:
            for half in range(2):
                o_ref[:, half * PAIR_COLS:(half + 1) * PAIR_COLS] = _dot(xn_s[...], w_ref[half]).astype(o_ref.dtype)

    return pl.pallas_call(
        body, name=name, grid=(s // tm, n // tn),
        in_specs=[pl.BlockSpec((tm, k), lambda i, j: (i, colblk)),
                  pl.BlockSpec((1, k), lambda i, j: (0, 0)),
                  w_spec],
        out_specs=[pl.BlockSpec((tm, tn), lambda i, j: (i, j)),
                   pl.BlockSpec((tm, k), lambda i, j: (i, 0))],
        out_shape=[SDS((s, n), BF16), SDS((s, k), BF16)],
        scratch_shapes=[pltpu.VMEM((tm, k), BF16)],
        compiler_params=_cparams(("parallel", "arbitrary")),
    )(x, g, w)


def gate_out_fwd(o, proj, gate_blk, w_out, x, name):
    s = x.shape[0]
    d = D_MODEL
    tm = min(512, s)

    def body(o_ref, g_ref, w_ref, x_ref, y_ref):
        g = g_ref[...].astype(F32)
        hg = (o_ref[...].astype(F32) * (g * _sigmoid(g))).astype(BF16)
        y_ref[...] = x_ref[...] + _dot(hg, w_ref[...])

    return pl.pallas_call(
        body, name=name, grid=(s // tm,),
        in_specs=[pl.BlockSpec((tm, d), lambda i: (i, 0)),
                  pl.BlockSpec((tm, d), lambda i: (i, gate_blk)),
                  pl.BlockSpec((d, d), lambda i: (0, 0)),
                  pl.BlockSpec((tm, d), lambda i: (i, 0))],
        out_specs=pl.BlockSpec((tm, d), lambda i: (i, 0)),
        out_shape=SDS((s, d), F32),
        compiler_params=_cparams(("parallel",)),
    )(o, proj, w_out, x)


def gate_out_bwd(dy, w_out, o, proj, gate_blk, name):
    s = dy.shape[0]
    d = D_MODEL
    tm = min(512, s)

    def body(dy_ref, w_ref, o_ref, g_ref, do_ref, dg_ref, hg_ref):
        dh = _dot_nt(dy_ref[...].astype(BF16), w_ref[...])
        g = g_ref[...].astype(F32)
        o = o_ref[...].astype(F32)
        sg = _sigmoid(g)
        silu = g * sg
        do_ref[...] = (dh * silu).astype(BF16)
        dg_ref[...] = (dh * o * (sg * (1.0 + g * (1.0 - sg)))).astype(BF16)
        hg_ref[...] = (o * silu).astype(BF16)

    blk = pl.BlockSpec((tm, d), lambda i: (i, 0))
    return pl.pallas_call(
        body, name=name, grid=(s // tm,),
        in_specs=[blk, pl.BlockSpec((d, d), lambda i: (0, 0)), blk,
                  pl.BlockSpec((tm, d), lambda i: (i, gate_blk))],
        out_specs=[blk, blk, blk],
        out_shape=[SDS((s, d), BF16)] * 3,
        compiler_params=_cparams(("parallel",)),
    )(dy, w_out, o, proj)


def tn_matmul(a, b, name):
    s, k = a.shape
    n = b.shape[1]
    tk = min(1024, k)
    tn = _pick_tile(n, 512)
    ts = min(2048, s)
    assert k % tk == 0 and s % ts == 0

    def body(a_ref, b_ref, o_ref):
        @pl.when(pl.program_id(2) == 0)
        def _():
            o_ref[...] = jnp.zeros_like(o_ref)

        o_ref[...] += _dot_tn(a_ref[...].astype(BF16), b_ref[...].astype(BF16))

    return pl.pallas_call(
        body, name=name, grid=(k // tk, n // tn, s // ts),
        in_specs=[pl.BlockSpec((ts, tk), lambda i, j, r: (r, i)),
                  pl.BlockSpec((ts, tn), lambda i, j, r: (r, j))],
        out_specs=pl.BlockSpec((tk, tn), lambda i, j, r: (i, j)),
        out_shape=SDS((k, n), F32),
        compiler_params=_cparams(("parallel", "parallel", "arbitrary")),
    )(a, b)


def tn_matmul_shards(a, b_parts, name):
    s, k = a.shape
    per = D_MODEL // PAIR_COLS
    assert len(b_parts) * per == N_DEV and a.dtype == BF16
    assert all(b.shape == (s, D_MODEL) and b.dtype == BF16 for b in b_parts)

    def body(a_ref, *refs):
        o_ref = refs[-1]
        j = pl.program_id(0)
        for p, b_ref in enumerate(refs[:-1]):
            @pl.when(j // per == p)
            def _(b_ref=b_ref):
                o_ref[...] = _dot_tn(a_ref[...], b_ref[...]).astype(BF16)

    part_specs = [pl.BlockSpec((s, PAIR_COLS), lambda j, p=p: (0, jnp.clip(j - per * p, 0, per - 1)))
                  for p in range(len(b_parts))]
    return pl.pallas_call(
        body, name=name, grid=(N_DEV,),
        in_specs=[pl.BlockSpec((s, k), lambda j: (0, 0))] + part_specs,
        out_specs=pl.BlockSpec((None, k, PAIR_COLS), lambda j: (j, 0, 0)),
        out_shape=SDS((N_DEV, k, PAIR_COLS), BF16),
        compiler_params=_cparams(("arbitrary",)),
    )(a, *b_parts)


def nt_norm_bwd(dp, w, x, xcol, g, res, name, out_dtype, shard_col=None):
    tm = min(512, dp[0].shape[0] if shard_col is not None else dp.shape[0])
    if shard_col is None:
        s, n = dp.shape
        k = w.shape[0]
        w_spec = pl.BlockSpec((k, n), lambda i: (0, 0))
        dp_parts = [dp]
        dp_specs = [pl.BlockSpec((tm, n), lambda i: (i, 0))]
    else:
        s, k = dp[0].shape[0], D_MODEL
        w_spec = pl.BlockSpec((N_DEV, k, PAIR_COLS), lambda i: (0, 0, shard_col))
        dp_parts = list(dp)
        dp_specs = [pl.BlockSpec((tm, D_MODEL), lambda i: (i, 0)) for _ in dp_parts]
    np_ = len(dp_parts)
    per = D_MODEL // PAIR_COLS
    assert s % tm == 0
    has_res = res is not None

    def body(*refs):
        dp_refs = refs[:np_]
        if has_res:
            w_ref, x_ref, g_ref, r_ref, dx_ref, dg_ref = refs[np_:]
        else:
            w_ref, x_ref, g_ref, dx_ref, dg_ref = refs[np_:]

        @pl.when(pl.program_id(0) == 0)
        def _():
            dg_ref[...] = jnp.zeros_like(dg_ref)

        if shard_col is None:
            dxn = _dot_nt(dp_refs[0][...], w_ref[...])
        else:
            dxn = None
            for j in range(N_DEV):
                c0 = (j % per) * PAIR_COLS
                term = _dot_nt(dp_refs[j // per][:, c0:c0 + PAIR_COLS], w_ref[j])
                dxn = term if dxn is None else dxn + term
        xf = x_ref[...].astype(F32)
        rstd = lax.rsqrt(jnp.mean(xf * xf, axis=-1, keepdims=True) + NORM_EPS)
        xhat = xf * rstd
        dg_ref[...] += jnp.sum(dxn * xhat, axis=0, keepdims=True)
        dxh = dxn * g_ref[...]
        dx = rstd * (dxh - xhat * jnp.mean(dxh * xhat, axis=-1, keepdims=True))
        if has_res:
            dx = dx + r_ref[...]
        dx_ref[...] = dx.astype(out_dtype)

    in_specs = dp_specs + [w_spec,
                           pl.BlockSpec((tm, k), lambda i: (i, xcol)),
                           pl.BlockSpec((1, k), lambda i: (0, 0))]
    args = dp_parts + [w, x, g]
    if has_res:
        in_specs.append(pl.BlockSpec((tm, k), lambda i: (i, 0)))
        args.append(res)
    return pl.pallas_call(
        body, name=name, grid=(s // tm,),
        in_specs=in_specs,
        out_specs=[pl.BlockSpec((tm, k), lambda i: (i, 0)),
                   pl.BlockSpec((1, k), lambda i: (0, 0))],
        out_shape=[SDS((s, k), out_dtype), SDS((1, k), F32)],
        compiler_params=_cparams(("arbitrary",)),
    )(*args)


def loss_kernel(y, target, name):
    s, d = y.shape
    tm = min(512, s)

    def body(y_ref, t_ref, l_ref, dy_ref):
        @pl.when(pl.program_id(0) == 0)
        def _():
            l_ref[...] = jnp.zeros_like(l_ref)

        e = y_ref[...] - t_ref[...]
        dy_ref[...] = e * (1.0 / d)
        l_ref[...] += 0.5 * jnp.sum(jnp.mean(e * e, axis=-1, keepdims=True))

    blk = pl.BlockSpec((tm, d), lambda i: (i, 0))
    return pl.pallas_call(
        body, name=name, grid=(s // tm,),
        in_specs=[blk, blk],
        out_specs=[pl.BlockSpec((8, LANES), lambda i: (0, 0)), blk],
        out_shape=[SDS((8, LANES), F32), SDS((s, d), F32)],
        compiler_params=_cparams(("arbitrary",)),
    )(y, target)


SB_LOG_CUTOFF = -80.0
SB_T = 256
SB_FWD_TILES = 4
SB_BWD_TILES = 2


def _sb_sweep(qi, cs, rest, step):
    n = len(cs)

    def top(arrs):
        m = jnp.max(arrs[0])
        for a in arrs[1:]:
            m = jnp.maximum(m, jnp.max(a))
        return m

    def cond(st):
        return (st[0] <= qi) & (st[1] > SB_LOG_CUTOFF)

    def body(st):
        cs2, rest2 = step(st[0], st[2:2 + n], st[2 + n:])
        return (st[0] + 1, top(cs2)) + tuple(cs2) + tuple(rest2)

    return lax.while_loop(cond, body, (jnp.int32(1), top(cs)) + tuple(cs) + tuple(rest))[2 + n:]


def _tile_masks(t):
    row = lax.broadcasted_iota(jnp.int32, (t, t), 0)
    col = lax.broadcasted_iota(jnp.int32, (t, t), 1)
    return row, col


def sb_attn_fwd(proj, name):
    s = proj.shape[0]
    n_tiles = SB_FWD_TILES
    t = min(SB_T, s)
    nq = s // t
    scale = 1.0 / math.sqrt(HALF)

    def body(q_ref, k_ref, v_ref, o_ref):
        lane = lax.broadcasted_iota(jnp.int32, (1, LANES), 1)
        row, col = _tile_masks(t)
        u_incl = (row >= col).astype(BF16)
        tri = jnp.concatenate([col < row] * 2, axis=0)

        def scores(pr, qs, kj):
            k = k_ref[pl.ds(pl.multiple_of(kj * t, t), t), pr * LANES:(pr + 1) * LANES]
            z = _dot_nt(qs, k)
            nz = -z
            lf = jnp.minimum(nz, 0.0) - jnp.log(1.0 + jnp.exp(jnp.minimum(z, nz)))
            return z, lf

        def accumulate(pr, kj, z, lf, c, acc, diag):
            v = v_ref[pl.ds(pl.multiple_of(kj * t, t), t), pr * LANES:(pr + 1) * LANES]
            if diag:
                lf = jnp.where(tri, lf, 0.0)
            incl = _cumsum_dot(lf, u_incl) + c
            a = jnp.exp(z + incl)
            if diag:
                a = jnp.where(tri, a, 0.0)
            acc = acc + _dot(a.astype(BF16), v)
            c = c + jnp.sum(lf, axis=1, keepdims=True)
            return c, acc

        def q_body(qi, carry):
            q0 = pl.multiple_of(qi * t, t)
            qss = []
            for pr in range(n_tiles):
                q = q_ref[pl.ds(q0, t), pr * LANES:(pr + 1) * LANES] * scale
                zq = jnp.zeros_like(q)
                qss.append(jnp.concatenate([jnp.where(lane < HALF, q, zq), jnp.where(lane >= HALF, q, zq)], axis=0))

            def step(jj, cs, accs, diag=False):
                outs = [accumulate(pr, qi - jj, *scores(pr, qss[pr], qi - jj), cs[pr], accs[pr], diag)
                        for pr in range(n_tiles)]
                return tuple(o[0] for o in outs), tuple(o[1] for o in outs)

            zero = (jnp.zeros((2 * t, 1), F32),) * n_tiles, (jnp.zeros((2 * t, LANES), F32),) * n_tiles
            cs, accs = step(0, zero[0], zero[1], True)
            accs = _sb_sweep(qi, cs, accs, step)
            for pr in range(n_tiles):
                o_ref[pl.ds(q0, t), pr * LANES:(pr + 1) * LANES] = jnp.where(lane < HALF, accs[pr][:t], accs[pr][t:])
            return carry

        lax.fori_loop(0, nq, q_body, 0)

    w = n_tiles * LANES
    return pl.pallas_call(
        body, name=name, grid=(D_MODEL // w,),
        in_specs=[pl.BlockSpec((s, w), lambda p: (0, p)),
                  pl.BlockSpec((s, w), lambda p: (0, D_MODEL // w + p)),
                  pl.BlockSpec((s, w), lambda p: (0, 2 * (D_MODEL // w) + p))],
        out_specs=pl.BlockSpec((s, w), lambda p: (0, p)),
        out_shape=SDS((s, D_MODEL), F32),
        compiler_params=_cparams(("parallel",), vmem_mb=56),
    )(proj, proj, proj)


def sb_attn_bwd(proj, o, do, name):
    s = proj.shape[0]
    n_tiles = SB_BWD_TILES
    t = min(SB_T, s)
    nq = s // t
    scale = 1.0 / math.sqrt(HALF)

    def body(q_ref, k_ref, v_ref, o_ref, do_ref, dq_ref, dk_ref, dv_ref, dk_s, dv_s):
        dk_s[...] = jnp.zeros_like(dk_s)
        dv_s[...] = jnp.zeros_like(dv_s)
        lane = lax.broadcasted_iota(jnp.int32, (1, LANES), 1)
        row, col = _tile_masks(t)
        u_incl = (row >= col).astype(BF16)
        u_excl = (row > col).astype(BF16)
        tri = jnp.concatenate([col < row] * 2, axis=0)

        def scores(pr, qs, dos, kj):
            k0 = pl.multiple_of(kj * t, t)
            cols = slice(pr * LANES, (pr + 1) * LANES)
            z = _dot_nt(qs, k_ref[pl.ds(k0, t), cols])
            nz = -z
            lf = jnp.minimum(nz, 0.0) - jnp.log(1.0 + jnp.exp(jnp.minimum(z, nz)))
            return z, lf, _dot_nt(dos, v_ref[pl.ds(k0, t), cols])

        def accumulate(pr, qs, dos, tsum, kj, z, lf, da, c, r, dq, diag):
            k0 = pl.multiple_of(kj * t, t)
            cols = slice(pr * LANES, (pr + 1) * LANES)
            sig = jnp.exp(z + lf)
            if diag:
                lf = jnp.where(tri, lf, 0.0)
            incl = _cumsum_dot(lf, u_incl) + c
            a = jnp.exp(z + incl)
            if diag:
                a = jnp.where(tri, a, 0.0)
            ab = a.astype(BF16)
            dl = ab.astype(F32) * da
            pre = tsum - r - _split_dot(dl, u_excl)
            dz = dl - sig * pre
            if diag:
                dz = jnp.where(tri, dz, 0.0)
            dzb = dz.astype(BF16)
            dq = dq + _dot(dzb, k_ref[pl.ds(k0, t), cols])
            dk_s[pl.ds(k0, t), cols] += _dot_tn(dzb, qs)
            dv_s[pl.ds(k0, t), cols] += _dot_tn(ab, dos)
            c = c + jnp.sum(lf, axis=1, keepdims=True)
            r = r + jnp.sum(dl, axis=1, keepdims=True)
            return c, r, dq

        def q_body(qi, carry):
            q0 = pl.multiple_of(qi * t, t)
            lo, hi = lane < HALF, lane >= HALF
            qss, doss, tsums = [], [], []
            for pr in range(n_tiles):
                cols = slice(pr * LANES, (pr + 1) * LANES)
                q = q_ref[pl.ds(q0, t), cols] * scale
                dob = do_ref[pl.ds(q0, t), cols]
                prod = dob.astype(F32) * o_ref[pl.ds(q0, t), cols]
                zb = jnp.zeros_like(q)
                qss.append(jnp.concatenate([jnp.where(lo, q, zb), jnp.where(hi, q, zb)], axis=0))
                doss.append(jnp.concatenate([jnp.where(lo, dob, zb), jnp.where(hi, dob, zb)], axis=0))
                tsums.append(jnp.concatenate([jnp.sum(jnp.where(lo, prod, 0.0), axis=1, keepdims=True),
                                              jnp.sum(jnp.where(hi, prod, 0.0), axis=1, keepdims=True)], axis=0))

            def step(jj, cs, rest, diag=False):
                outs = [accumulate(pr, qss[pr], doss[pr], tsums[pr], qi - jj,
                                   *scores(pr, qss[pr], doss[pr], qi - jj),
                                   cs[pr], rest[pr], rest[n_tiles + pr], diag) for pr in range(n_tiles)]
                return tuple(o[0] for o in outs), tuple(o[1] for o in outs) + tuple(o[2] for o in outs)

            zc = (jnp.zeros((2 * t, 1), F32),) * n_tiles
            cs, rest = step(0, zc, zc + (jnp.zeros((2 * t, LANES), F32),) * n_tiles, True)
            rest = _sb_sweep(qi, cs, rest, step)
            for pr in range(n_tiles):
                dq = rest[n_tiles + pr]
                dq_ref[pl.ds(q0, t), pr * LANES:(pr + 1) * LANES] = (jnp.where(lo, dq[:t], dq[t:]) * scale).astype(BF16)
            return carry

        lax.fori_loop(0, nq, q_body, 0)
        dk_ref[...] = dk_s[...].astype(BF16)
        dv_ref[...] = dv_s[...].astype(BF16)

    w = n_tiles * LANES
    nblk = D_MODEL // w
    blk = lambda off: pl.BlockSpec((s, w), lambda p: (0, off * nblk + p))
    return pl.pallas_call(
        body, name=name, grid=(nblk,),
        in_specs=[blk(0), blk(1), blk(2), blk(0), blk(0)],
        out_specs=[blk(0), blk(0), blk(0)],
        out_shape=[SDS((s, D_MODEL), BF16)] * 3,
        scratch_shapes=[pltpu.VMEM((s, w), F32), pltpu.VMEM((s, w), F32)],
        compiler_params=_cparams(("parallel",), vmem_mb=56),
    )(proj, proj, proj, o, do)


MLA_HEADS = 8
MLA_QK = 192
MLA_SCALE = 1.0 / math.sqrt(MLA_QK)
MLA_TK = 512


def _rope_tables(s):
    inv_freq = 10000.0 ** (-jnp.arange(32, dtype=F32) / 32)
    ang = jnp.arange(s, dtype=F32)[:, None] * inv_freq[None, :]
    cos = jnp.tile(jnp.cos(ang), (1, 4))
    sin = jnp.sin(ang)
    sin_signed = jnp.tile(jnp.concatenate([-sin, sin], axis=1), (1, 2))
    return cos, sin_signed


def _partner(u, lane):
    return jnp.where((lane % HALF) < 32, pltpu.roll(u, 96, 1), pltpu.roll(u, 32, 1))


def _mla_head_inputs(h, qa_ref, kva_ref, kpe, lane):
    a = h % 2
    hm = (lane < HALF) if a == 0 else (lane >= HALF)
    qn = qa_ref[:, h * 128:(h + 1) * 128].astype(F32)
    qr = jnp.where(hm, qa_ref[:, 1024 + (h // 2) * 128:1024 + (h // 2 + 1) * 128].astype(F32), 0.0)
    kn = kva_ref[:, h * 256:h * 256 + 128].astype(F32)
    kr = kpe if a == 0 else pltpu.roll(kpe, HALF, 1)
    return qn, qr, kn, kr


def mla_prep_fwd(qa, kva, proj, gq, gk, cos, sin, name):
    s = qa.shape[0]
    tm = min(256, s)

    def body(qa_ref, kva_ref, kpe_ref, gq_ref, gk_ref, cos_ref, sin_ref, q_out, k_out):
        lane = lax.broadcasted_iota(jnp.int32, (1, LANES), 1)
        kpe = kpe_ref[...].astype(F32)
        cs, sn = cos_ref[...], sin_ref[...]
        for h in range(MLA_HEADS):
            qn, qr, kn, kr = _mla_head_inputs(h, qa_ref, kva_ref, kpe, lane)
            for xn, xr, g_ref, out, sc in ((qn, qr, gq_ref, q_out, MLA_SCALE), (kn, kr, gk_ref, k_out, 1.0)):
                ss = jnp.sum(xn * xn, axis=1, keepdims=True) + jnp.sum(xr * xr, axis=1, keepdims=True)
                rstd = lax.rsqrt(ss * (1.0 / MLA_QK) + NORM_EPS) * sc
                yn = xn * rstd * g_ref[:, 0:128]
                ur = xr * rstd * g_ref[:, 128:256]
                yr = ur * cs + _partner(ur, lane) * sn
                out[:, h * 256:h * 256 + 128] = yn.astype(BF16)
                out[:, h * 256 + 128:h * 256 + 256] = yr.astype(BF16)

    row = lambda w, c: pl.BlockSpec((tm, w), lambda i: (i, c))
    full = lambda w: pl.BlockSpec((1, w), lambda i: (0, 0))
    return pl.pallas_call(
        body, name=name, grid=(s // tm,),
        in_specs=[row(1536, 0), row(2048, 0), row(128, 11), full(256), full(256), row(128, 0), row(128, 0)],
        out_specs=[row(2048, 0), row(2048, 0)],
        out_shape=[SDS((s, 2048), BF16)] * 2,
        compiler_params=_cparams(("parallel",)),
    )(qa, kva, proj, gq, gk, cos, sin)


def mla_prep_bwd(dq_p, dk_p, dv, qa, kva, proj, gq, gk, cos, sin, name):
    s = qa.shape[0]
    tm = min(256, s)

    def body(dq_ref, dk_ref, dv_ref, qa_ref, kva_ref, kpe_ref, gq_ref, gk_ref, cos_ref, sin_ref,
             dqa_ref, dkva_ref, dkpe_ref, dgq_ref, dgk_ref):
        @pl.when(pl.program_id(0) == 0)
        def _():
            dgq_ref[...] = jnp.zeros_like(dgq_ref)
            dgk_ref[...] = jnp.zeros_like(dgk_ref)

        lane = lax.broadcasted_iota(jnp.int32, (1, LANES), 1)
        kpe = kpe_ref[...].astype(F32)
        cs, sn = cos_ref[...], sin_ref[...]
        dkpe = jnp.zeros((tm, LANES), F32)
        dqr_pair = None
        for h in range(MLA_HEADS):
            qn, qr, kn, kr = _mla_head_inputs(h, qa_ref, kva_ref, kpe, lane)
            res = []
            for xn, xr, g_ref, d_ref, dg_ref, sc in ((qn, qr, gq_ref, dq_ref, dgq_ref, MLA_SCALE),
                                                     (kn, kr, gk_ref, dk_ref, dgk_ref, 1.0)):
                ss = jnp.sum(xn * xn, axis=1, keepdims=True) + jnp.sum(xr * xr, axis=1, keepdims=True)
                rstd = lax.rsqrt(ss * (1.0 / MLA_QK) + NORM_EPS)
                hn, hr = xn * rstd, xr * rstd
                dyn = d_ref[:, h * 256:h * 256 + 128] * sc
                dyr = d_ref[:, h * 256 + 128:h * 256 + 256] * sc
                dur = dyr * cs - _partner(dyr, lane) * sn
                dg_ref[:, 0:128] += jnp.sum(dyn * hn, axis=0, keepdims=True)
                dg_ref[:, 128:256] += jnp.sum(dur * hr, axis=0, keepdims=True)
                dhn = dyn * g_ref[:, 0:128]
                dhr = dur * g_ref[:, 128:256]
                mu = (jnp.sum(dhn * hn, axis=1, keepdims=True)
                      + jnp.sum(dhr * hr, axis=1, keepdims=True)) * (1.0 / MLA_QK)
                res.append((rstd * (dhn - hn * mu), rstd * (dhr - hr * mu)))
            (dqn, dqr), (dkn, dkr) = res
            dqa_ref[:, h * 128:(h + 1) * 128] = dqn.astype(BF16)
            if h % 2 == 0:
                dqr_pair = dqr
            else:
                dqa_ref[:, 1024 + (h // 2) * 128:1024 + (h // 2 + 1) * 128] = (dqr_pair + dqr).astype(BF16)
            dkva_ref[:, h * 256:h * 256 + 128] = dkn.astype(BF16)
            dkva_ref[:, h * 256 + 128:h * 256 + 256] = dv_ref[:, h * 128:(h + 1) * 128]
            dkpe = dkpe + (dkr if h % 2 == 0 else pltpu.roll(dkr, HALF, 1))
        dkpe_ref[...] = dkpe.astype(BF16)

    row = lambda w, c: pl.BlockSpec((tm, w), lambda i: (i, c))
    full = lambda w: pl.BlockSpec((1, w), lambda i: (0, 0))
    return pl.pallas_call(
        body, name=name, grid=(s // tm,),
        in_specs=[row(2048, 0), row(2048, 0), row(1024, 0), row(1536, 0), row(2048, 0), row(128, 11),
                  full(256), full(256), row(128, 0), row(128, 0)],
        out_specs=[row(1536, 0), row(2048, 0), row(128, 0), full(256), full(256)],
        out_shape=[SDS((s, 1536), BF16), SDS((s, 2048), BF16), SDS((s, 128), BF16),
                   SDS((1, 256), F32), SDS((1, 256), F32)],
        compiler_params=_cparams(("arbitrary",)),
    )(dq_p, dk_p, dv, qa, kva, proj, gq, gk, cos, sin)


def mla_attn_fwd(q_p, k_p, kva, name):
    s = q_p.shape[0]
    tk = min(MLA_TK, s)
    tq = min(512, s)
    r = tq // tk
    nq = s // tq

    def body(q_ref, k_ref, va_ref, vb_ref, o_ref, lse_ref):
        row = lax.broadcasted_iota(jnp.int32, (tq, tk), 0)
        col = lax.broadcasted_iota(jnp.int32, (tq, tk), 1)
        v_refs = (va_ref, vb_ref)

        def tile(hh, q, kj, m, l, acc, mask):
            k0 = pl.multiple_of(kj * tk, tk)
            sc = _dot_nt(q, k_ref[pl.ds(k0, tk), hh * 256:(hh + 1) * 256])
            if mask is not None:
                sc = jnp.where(mask, sc, -1e30)
            m_new = jnp.maximum(m, jnp.max(sc, axis=1, keepdims=True))
            alpha = jnp.exp(m - m_new)
            p = jnp.exp(sc - m_new)
            l = alpha * l + jnp.sum(p, axis=1, keepdims=True)
            acc = alpha * acc + _dot(p.astype(BF16), v_refs[hh][pl.ds(k0, tk), :])
            return m_new, l, acc

        def q_body(qi, carry):
            q0 = pl.multiple_of(qi * tq, tq)
            qs = [q_ref[pl.ds(q0, tq), hh * 256:(hh + 1) * 256] for hh in range(2)]

            def step(kj, cr, mask=None):
                out = ()
                for hh in range(2):
                    out += tile(hh, qs[hh], kj, cr[3 * hh], cr[3 * hh + 1], cr[3 * hh + 2], mask)
                return out

            cr = (jnp.full((tq, 1), -1e30, F32), jnp.zeros((tq, 1), F32), jnp.zeros((tq, LANES), F32)) * 2
            cr = lax.fori_loop(0, qi * r, step, cr)
            for d in range(r):
                cr = step(qi * r + d, cr, col + d * tk <= row)
            for hh in range(2):
                m, l, acc = cr[3 * hh:3 * hh + 3]
                o_ref[pl.ds(q0, tq), hh * 128:(hh + 1) * 128] = acc / l
                lse_ref[pl.ds(q0, tq), hh * 128:(hh + 1) * 128] = jnp.broadcast_to(m + jnp.log(l), (tq, LANES))
            return carry

        lax.fori_loop(0, nq, q_body, 0)

    return pl.pallas_call(
        body, name=name, grid=(MLA_HEADS // 2,),
        in_specs=[pl.BlockSpec((s, 512), lambda g: (0, g)),
                  pl.BlockSpec((s, 512), lambda g: (0, g)),
                  pl.BlockSpec((s, 128), lambda g: (0, 4 * g + 1)),
                  pl.BlockSpec((s, 128), lambda g: (0, 4 * g + 3))],
        out_specs=[pl.BlockSpec((s, 256), lambda g: (0, g)),
                   pl.BlockSpec((s, 256), lambda g: (0, g))],
        out_shape=[SDS((s, 1024), F32), SDS((s, 1024), F32)],
        compiler_params=_cparams(("parallel",), vmem_mb=56),
    )(q_p, k_p, kva, kva)


def mla_attn_bwd(q_p, k_p, kva, o, do, lse, name):
    s = q_p.shape[0]
    tk = min(MLA_TK, s)
    tq = min(512, s)
    r = tq // tk
    nq = s // tq

    def body(q_ref, k_ref, v_ref, o_ref, do_ref, lse_ref, dq_ref, dk_ref, dv_ref, dv_s):
        dk_ref[...] = jnp.zeros_like(dk_ref)
        dv_s[...] = jnp.zeros_like(dv_s)
        row = lax.broadcasted_iota(jnp.int32, (tq, tk), 0)
        col = lax.broadcasted_iota(jnp.int32, (tq, tk), 1)

        def tile(q, dob, lse, delta, kj, dq, mask):
            k0 = pl.multiple_of(kj * tk, tk)
            k = k_ref[pl.ds(k0, tk), :]
            v = v_ref[pl.ds(k0, tk), :]
            p = jnp.exp(_dot_nt(q, k) - lse)
            if mask is not None:
                p = jnp.where(mask, p, 0.0)
            ds = (p * (_dot_nt(dob, v) - delta)).astype(BF16)
            dq = dq + _dot(ds, k)
            dk_ref[pl.ds(k0, tk), :] += _dot_tn(ds, q)
            dv_s[pl.ds(k0, tk), :] += _dot_tn(p.astype(BF16), dob)
            return dq

        def q_body(qi, carry):
            q0 = pl.multiple_of(qi * tq, tq)
            q = q_ref[pl.ds(q0, tq), :]
            dob = do_ref[pl.ds(q0, tq), :]
            lse = lse_ref[pl.ds(q0, tq), 0:1]
            delta = jnp.sum(dob.astype(F32) * o_ref[pl.ds(q0, tq), :], axis=1, keepdims=True)
            dq = lax.fori_loop(0, qi * r, lambda kj, dq: tile(q, dob, lse, delta, kj, dq, None),
                               jnp.zeros((tq, 256), F32))
            for d in range(r):
                dq = tile(q, dob, lse, delta, qi * r + d, dq, col + d * tk <= row)
            dq_ref[pl.ds(q0, tq), :] = dq
            return carry

        lax.fori_loop(0, nq, q_body, 0)
        dv_ref[...] = dv_s[...].astype(BF16)

    b256 = pl.BlockSpec((s, 256), lambda h: (0, h))
    b128 = pl.BlockSpec((s, 128), lambda h: (0, h))
    return pl.pallas_call(
        body, name=name, grid=(MLA_HEADS,),
        in_specs=[b256, b256, pl.BlockSpec((s, 128), lambda h: (0, 2 * h + 1)), b128, b128, b128],
        out_specs=[b256, b256, b128],
        out_shape=[SDS((s, 2048), F32), SDS((s, 2048), F32), SDS((s, 1024), BF16)],
        scratch_shapes=[pltpu.VMEM((s, 128), F32)],
        compiler_params=_cparams(("parallel",), vmem_mb=56),
    )(q_p, k_p, kva, o, do, lse)


SWA_HEADS = 16
SWA_BLOCK = 128


def _swa_head(tile_idx, half):
    return (2 * (tile_idx // 4) + half) * 4 + tile_idx % 4


SWA_Q_PERM = np.concatenate([np.arange(_swa_head(tt, a) * HALF, (_swa_head(tt, a) + 1) * HALF)
                             for tt in range(8) for a in range(2)])
SWA_SLOPES = [2.0 ** (-8.0 * (h + 1) / SWA_HEADS) for h in range(SWA_HEADS)]


def swa_prep_fwd(proj, g128, name):
    s = proj.shape[0]
    tm = min(512, s)

    def body(q_ref, k_ref, g_ref, o_ref):
        lane = lax.broadcasted_iota(jnp.int32, (1, LANES), 1)
        lo = lane < HALF
        for tt in range(10):
            if tt < 8:
                x = q_ref[:, tt * 128:(tt + 1) * 128].astype(F32)
            else:
                x = k_ref[:, (tt - 8) * 128:(tt - 7) * 128].astype(F32)
            sq = x * x
            sa = jnp.sum(jnp.where(lo, sq, 0.0), axis=1, keepdims=True)
            sb = jnp.sum(sq, axis=1, keepdims=True) - sa
            rstd = jnp.where(lo, lax.rsqrt(sa * (1.0 / HALF) + NORM_EPS), lax.rsqrt(sb * (1.0 / HALF) + NORM_EPS))
            gi = 0 if tt < 8 else 1
            o_ref[:, tt * 128:(tt + 1) * 128] = (x * rstd * g_ref[gi:gi + 1, :]).astype(BF16)

    return pl.pallas_call(
        body, name=name, grid=(s // tm,),
        in_specs=[pl.BlockSpec((tm, 1024), lambda i: (i, 0)), pl.BlockSpec((tm, 256), lambda i: (i, 8)),
                  pl.BlockSpec((2, 128), lambda i: (0, 0))],
        out_specs=pl.BlockSpec((tm, 1280), lambda i: (i, 0)),
        out_shape=SDS((s, 1280), BF16),
        compiler_params=_cparams(("parallel",)),
    )(proj, proj, g128)


def swa_prep_bwd(dqn, dkn, proj, g128, name):
    s = proj.shape[0]
    tm = min(512, s)
    nsteps = s // tm

    def body(dq_ref, dk_ref, xq_ref, xk_ref, g_ref, o_ref, dg_ref):
        @pl.when(pl.program_id(0) == 0)
        def _():
            dg_ref[...] = jnp.zeros_like(dg_ref)

        lane = lax.broadcasted_iota(jnp.int32, (1, LANES), 1)
        lo = lane < HALF
        for tt in range(10):
            if tt < 8:
                x = xq_ref[:, tt * 128:(tt + 1) * 128].astype(F32)
                dy = dq_ref[:, tt * 128:(tt + 1) * 128].astype(F32)
            else:
                x = xk_ref[:, (tt - 8) * 128:(tt - 7) * 128].astype(F32)
                dy = dk_ref[:, (tt - 8) * 128:(tt - 7) * 128].astype(F32)
            gi = 0 if tt < 8 else 1
            sq = x * x
            sa = jnp.sum(jnp.where(lo, sq, 0.0), axis=1, keepdims=True)
            sb = jnp.sum(sq, axis=1, keepdims=True) - sa
            rstd = jnp.where(lo, lax.rsqrt(sa * (1.0 / HALF) + NORM_EPS), lax.rsqrt(sb * (1.0 / HALF) + NORM_EPS))
            xh = x * rstd
            dg_ref[gi:gi + 1, :] += jnp.sum(dy * xh, axis=0, keepdims=True)
            dxh = dy * g_ref[gi:gi + 1, :]
            pr = dxh * xh
            ma = jnp.sum(jnp.where(lo, pr, 0.0), axis=1, keepdims=True)
            mb = jnp.sum(pr, axis=1, keepdims=True) - ma
            mu = jnp.where(lo, ma, mb) * (1.0 / HALF)
            o_ref[:, tt * 128:(tt + 1) * 128] = (rstd * (dxh - xh * mu)).astype(BF16)

        @pl.when(pl.program_id(0) == nsteps - 1)
        def _():
            acc = dg_ref[...]
            dg_ref[...] = acc + pltpu.roll(acc, HALF, 1)

    return pl.pallas_call(
        body, name=name, grid=(nsteps,),
        in_specs=[pl.BlockSpec((tm, 1024), lambda i: (i, 0)), pl.BlockSpec((tm, 256), lambda i: (i, 0)),
                  pl.BlockSpec((tm, 1024), lambda i: (i, 0)), pl.BlockSpec((tm, 256), lambda i: (i, 8)),
                  pl.BlockSpec((2, 128), lambda i: (0, 0))],
        out_specs=[pl.BlockSpec((tm, 1280), lambda i: (i, 0)), pl.BlockSpec((2, 128), lambda i: (0, 0))],
        out_shape=[SDS((s, 1280), BF16), SDS((2, 128), F32)],
        compiler_params=_cparams(("arbitrary",)),
    )(dqn, dkn, proj, proj, g128)


SWA_QSCALE = 1.0 / math.sqrt(HALF)


def _swa_bias():
    b = SWA_BLOCK
    row = jnp.arange(b)[:, None]
    col = jnp.arange(2 * b)[None, :]
    rel = row + b - col
    valid = (rel >= 0) & (rel < b)
    slopes = jnp.asarray([SWA_SLOPES[_swa_head(tt, a)] for tt in range(8) for a in range(2)], F32)
    bias = jnp.where(valid[None], -slopes[:, None, None] * rel[None].astype(F32), -1e30)
    return jnp.stack([jnp.where((col >= b)[None], bias, -1e30), bias])


def _swa_tile_inputs(tt, q_ref, bias_ref, sink_ref, lane):
    b = SWA_BLOCK
    qt = q_ref[:, tt * 128:(tt + 1) * 128] * SWA_QSCALE
    zq = jnp.zeros_like(qt)
    qs = jnp.concatenate([jnp.where(lane < HALF, qt, zq), jnp.where(lane >= HALF, qt, zq)], axis=0)
    bias = jnp.concatenate([bias_ref[0, 2 * tt], bias_ref[0, 2 * tt + 1]], axis=0)
    sink = jnp.concatenate([jnp.full((b, 1), sink_ref[_swa_head(tt, a)], F32) for a in range(2)], axis=0)
    return qs, bias, sink


def _swa_softmax(qs, kt, bias, sink):
    sc = _dot_nt(qs, kt) + bias
    m = jnp.maximum(jnp.max(sc, axis=1, keepdims=True), sink)
    e = jnp.exp(sc - m)
    es = jnp.exp(sink - m)
    inv = 1.0 / (jnp.sum(e, axis=1, keepdims=True) + es)
    return e * inv, es * inv


def _swa_specs(b):
    cur = lambda w, c: pl.BlockSpec((b, w), lambda n: (n, c))
    prev = lambda w, c: pl.BlockSpec((b, w), lambda n: (jnp.maximum(n - 1, 0), c))
    bias = pl.BlockSpec((1, SWA_HEADS, b, 2 * b), lambda n: (jnp.minimum(n, 1), 0, 0, 0))
    return cur, prev, bias


def swa_attn_fwd(qkn, proj, sinks, bias, name):
    s = qkn.shape[0]
    b = SWA_BLOCK

    def body(sink_ref, bias_ref, q_ref, kc_ref, kp_ref, vc_ref, vp_ref, o_ref):
        lane = lax.broadcasted_iota(jnp.int32, (1, LANES), 1)
        for tt in range(8):
            gp = tt // 4
            kt = jnp.concatenate([kp_ref[:, gp * 128:(gp + 1) * 128], kc_ref[:, gp * 128:(gp + 1) * 128]], axis=0)
            vt = jnp.concatenate([vp_ref[:, gp * 128:(gp + 1) * 128], vc_ref[:, gp * 128:(gp + 1) * 128]], axis=0)
            qt = q_ref[:, tt * 128:(tt + 1) * 128] * SWA_QSCALE
            outs = []
            for a in range(2):
                qm = jnp.where((lane < HALF) if a == 0 else (lane >= HALF), qt, jnp.zeros_like(qt))
                p, _ = _swa_softmax(qm, kt, bias_ref[0, 2 * tt + a], sink_ref[_swa_head(tt, a)])
                outs.append(_dot(p.astype(BF16), vt))
            o_ref[:, tt * 128:(tt + 1) * 128] = jnp.where(lane < HALF, outs[0], outs[1])

    cur, prev, bias_spec = _swa_specs(b)
    return pl.pallas_call(
        body, name=name, grid=(s // b,),
        in_specs=[pl.BlockSpec(memory_space=pltpu.SMEM), bias_spec, cur(1024, 0), cur(256, 4), prev(256, 4),
                  cur(256, 9), prev(256, 9)],
        out_specs=cur(1024, 0),
        out_shape=SDS((s, 1024), F32),
        compiler_params=_cparams(("parallel",)),
    )(sinks, bias, qkn, qkn, qkn, proj, proj)


def swa_attn_bwd(qkn, proj, sinks, bias, do, name):
    s = qkn.shape[0]
    b = SWA_BLOCK

    def body(sink_ref, bias_ref, q_ref, kc_ref, kp_ref, vc_ref, vp_ref, do_ref, dq_ref, dk_ref, dv_ref, ds_ref):
        n = pl.program_id(0)

        @pl.when(n == 0)
        def _():
            dk_ref[...] = jnp.zeros_like(dk_ref)
            dv_ref[...] = jnp.zeros_like(dv_ref)
            ds_ref[...] = jnp.zeros_like(ds_ref)

        lane = lax.broadcasted_iota(jnp.int32, (1, LANES), 1)
        cur0 = pl.multiple_of(n * b, b)
        prev0 = pl.multiple_of(jnp.maximum(n - 1, 0) * b, b)
        dsink = jnp.zeros((1, LANES), F32)
        for gp in range(2):
            kt = jnp.concatenate([kp_ref[:, gp * 128:(gp + 1) * 128], kc_ref[:, gp * 128:(gp + 1) * 128]], axis=0)
            vt = jnp.concatenate([vp_ref[:, gp * 128:(gp + 1) * 128], vc_ref[:, gp * 128:(gp + 1) * 128]], axis=0)
            dkt = jnp.zeros((2 * b, LANES), F32)
            dvt = jnp.zeros((2 * b, LANES), F32)
            for tt in range(4 * gp, 4 * gp + 4):
                qs, bias2, sink = _swa_tile_inputs(tt, q_ref, bias_ref, sink_ref, lane)
                dot_ = do_ref[:, tt * 128:(tt + 1) * 128]
                zd = jnp.zeros_like(dot_)
                dos = jnp.concatenate([jnp.where(lane < HALF, dot_, zd), jnp.where(lane >= HALF, dot_, zd)], axis=0)
                p, ps = _swa_softmax(qs, kt, bias2, sink)
                dp = _dot_nt(dos, vt)
                delta = jnp.sum(p * dp, axis=1, keepdims=True)
                dz = (p * (dp - delta)).astype(BF16)
                sd = ps * delta
                for a in range(2):
                    dsink = dsink + jnp.where(lane == _swa_head(tt, a), -jnp.sum(sd[a * b:(a + 1) * b]), 0.0)
                dq2 = _dot(dz, kt)
                dq_ref[:, tt * 128:(tt + 1) * 128] = (jnp.where(lane < HALF, dq2[:b], dq2[b:]) * SWA_QSCALE).astype(BF16)
                dkt = dkt + _dot_tn(dz, qs)
                dvt = dvt + _dot_tn(p.astype(BF16), dos)
            cols = slice(gp * 128, (gp + 1) * 128)
            dk_ref[pl.ds(prev0, b), cols] += dkt[:b]
            dk_ref[pl.ds(cur0, b), cols] += dkt[b:]
            dv_ref[pl.ds(prev0, b), cols] += dvt[:b]
            dv_ref[pl.ds(cur0, b), cols] += dvt[b:]
        ds_ref[0:1, :] += dsink

    cur, prev, bias_spec = _swa_specs(b)
    full = pl.BlockSpec((s, 256), lambda n: (0, 0))
    return pl.pallas_call(
        body, name=name, grid=(s // b,),
        in_specs=[pl.BlockSpec(memory_space=pltpu.SMEM), bias_spec, cur(1024, 0), cur(256, 4), prev(256, 4),
                  cur(256, 9), prev(256, 9), cur(1024, 0)],
        out_specs=[cur(1024, 0), full, full, pl.BlockSpec((8, LANES), lambda n: (0, 0))],
        out_shape=[SDS((s, 1024), BF16), SDS((s, 256), F32), SDS((s, 256), F32), SDS((8, LANES), F32)],
        compiler_params=_cparams(("arbitrary",)),
    )(sinks, bias, qkn, qkn, qkn, proj, proj, do)


def _my_pos():
    return lax.axis_index("x"), lax.axis_index("y"), lax.axis_index("c")


ANY = pl.BlockSpec(memory_space=pl.ANY)


def all_gather_packed(xs, name):
    r, n = xs.shape

    def body(x_ref, out_ref, send_sems, recv_sems, local_sem):
        x, y, c = _my_pos()
        me, sibling = (x, y, c), (x, y, 1 - c)
        south = c == 0
        via = (jnp.where(south, 1 - x, x), jnp.where(south, y, 1 - y))
        far = (jnp.where(south, x, 1 - x), jnp.where(south, 1 - y, y))
        rk = jnp.where(south, 1, 2)
        diag = (1 - x, 1 - y)

        def slot(px, py, pc):
            return out_ref.at[4 * px + 2 * py + pc]

        def copy(k, block, to, src=None):
            return pltpu.make_async_remote_copy(
                src_ref=slot(*block) if src is None else src, dst_ref=slot(*block),
                send_sem=send_sems.at[k], recv_sem=recv_sems.at[k],
                device_id=to, device_id_type=MESH_ID)

        mine = pltpu.make_async_copy(x_ref, slot(*me), local_sem)
        mine.start()
        sent = [copy(0, me, sibling, src=x_ref), copy(1, me, (1 - x, y, c), src=x_ref),
                copy(2, me, (x, 1 - y, c), src=x_ref)]
        for cp in sent:
            cp.start()
        copy(rk, (*via, c), me).wait_recv()
        sent += [copy(3, (*via, c), (*far, c)), copy(3 + rk, (*via, c), sibling)]
        sent[-2].start()
        sent[-1].start()
        copy(3 - rk, (*far, c), me).wait_recv()
        sent.append(copy(6 - rk, (*far, c), sibling))
        sent[-1].start()
        copy(3, (*diag, c), me).wait_recv()
        sent.append(copy(6, (*diag, c), sibling))
        sent[-1].start()
        copy(0, sibling, me).wait_recv()
        for k, chip in ((4, (1 - x, y)), (5, (x, 1 - y)), (6, diag)):
            copy(k, (*chip, 1 - c), me).wait_recv()
        for cp in sent:
            cp.wait_send()
        mine.wait()

    return pl.pallas_call(
        body, name=name,
        out_shape=SDS((N_DEV, r, n), xs.dtype),
        in_specs=[ANY], out_specs=ANY,
        scratch_shapes=[pltpu.SemaphoreType.DMA((7,)), pltpu.SemaphoreType.DMA((7,)), pltpu.SemaphoreType.DMA],
    )(xs)


def rs_sibling_exchange(g, name):
    _, r, n = g.shape

    def body(g_ref, ra_ref, send_sems, recv_sems):
        x, y, c = _my_pos()
        copies = [pltpu.make_async_remote_copy(
            src_ref=g_ref.at[2 * chip + (1 - c)], dst_ref=ra_ref.at[chip],
            send_sem=send_sems.at[chip], recv_sem=recv_sems.at[chip],
            device_id=(x, y, 1 - c), device_id_type=MESH_ID) for chip in range(4)]
        for cp in copies:
            cp.start()
        for cp in copies:
            cp.wait()

    return pl.pallas_call(
        body, name=name, out_shape=SDS((4, r, n), g.dtype), in_specs=[ANY], out_specs=ANY,
        scratch_shapes=[pltpu.SemaphoreType.DMA((4,)), pltpu.SemaphoreType.DMA((4,))],
    )(g)


def rs_pair_add(g, ra, cidx, name):
    _, r, n = g.shape
    tr = r // 8

    def body(c_ref, g_ref, ra_ref, p_ref):
        p_ref[...] = (g_ref[...].astype(F32) + ra_ref[...].astype(F32)).astype(p_ref.dtype)

    return pl.pallas_call(
        body, name=name,
        grid_spec=pltpu.PrefetchScalarGridSpec(
            num_scalar_prefetch=1, grid=(4, r // tr),
            in_specs=[pl.BlockSpec((1, tr, n), lambda i, j, c: (2 * i + c[0], j, 0)),
                      pl.BlockSpec((1, tr, n), lambda i, j, c: (i, j, 0))],
            out_specs=pl.BlockSpec((1, tr, n), lambda i, j, c: (i, j, 0))),
        out_shape=SDS((4, r, n), g.dtype),
        compiler_params=_cparams(("parallel", "parallel")),
    )(cidx, g, ra)


def rs_cross_exchange(p, name):
    _, r, n = p.shape

    def body(p_ref, rb_ref, send_sems, recv_sems):
        x, y, c = _my_pos()
        copies = []
        for k, (dx, dy) in enumerate(((1, 0), (0, 1), (1, 1))):
            tx = 1 - x if dx else x
            ty = 1 - y if dy else y
            copies.append(pltpu.make_async_remote_copy(
                src_ref=p_ref.at[2 * tx + ty], dst_ref=rb_ref.at[k],
                send_sem=send_sems.at[k], recv_sem=recv_sems.at[k],
                device_id=(tx, ty, c), device_id_type=MESH_ID))
        for cp in copies:
            cp.start()
        for cp in copies:
            cp.wait()

    return pl.pallas_call(
        body, name=name, out_shape=SDS((3, r, n), p.dtype), in_specs=[ANY], out_specs=ANY,
        scratch_shapes=[pltpu.SemaphoreType.DMA((3,)), pltpu.SemaphoreType.DMA((3,))],
    )(p)


def _adamw(w, g, m, v):
    m = ADAM_B1 * m + (1.0 - ADAM_B1) * g
    v = ADAM_B2 * v + (1.0 - ADAM_B2) * (g * g)
    m_hat = m / (1.0 - ADAM_B1 ** ADAM_STEP)
    v_hat = v / (1.0 - ADAM_B2 ** ADAM_STEP)
    delta = -ADAM_LR * (m_hat / (jnp.sqrt(v_hat) + ADAM_EPS) + ADAM_WD * w)
    return delta, m, v


def rs_final_adamw(p, rb, chipidx, w, m, v, row0, col_blk, name):
    r, n = w.shape
    tr = min(r, 256 if n <= PAIR_COLS else 128)
    assert r % tr == 0 and row0 % tr == 0
    rb0 = row0 // tr

    def body(c_ref, p_ref, rb_ref, w_ref, m_ref, v_ref, g_out, d_out, m_out, v_out):
        g = p_ref[0].astype(F32)
        for k in range(3):
            g = g + rb_ref[k].astype(F32)
        d, mn, vn = _adamw(w_ref[...], g, m_ref[...], v_ref[...])
        g_out[...] = g
        d_out[...] = d
        m_out[...] = mn
        v_out[...] = vn

    blk = pl.BlockSpec((tr, n), lambda j, c: (j, 0))
    return pl.pallas_call(
        body, name=name,
        grid_spec=pltpu.PrefetchScalarGridSpec(
            num_scalar_prefetch=1, grid=(r // tr,),
            in_specs=[pl.BlockSpec((1, tr, n), lambda j, c: (c[0], rb0 + j, col_blk)),
                      pl.BlockSpec((3, tr, n), lambda j, c: (0, rb0 + j, col_blk)), blk, blk, blk],
            out_specs=[blk, blk, blk, blk]),
        out_shape=[SDS((r, n), F32)] * 4,
        compiler_params=_cparams(("parallel",)),
    )(chipidx, p, rb, w, m, v)


def small_allreduce_adamw(g, w, m, v, name):
    rows = g.shape[0]

    def body(g_ref, w_ref, m_ref, v_ref, g_out, d_out, m_out, v_out, buf, send_sems, recv_sems):
        x, y, c = _my_pos()
        my = 4 * x + 2 * y + c
        buf[my] = g_ref[...]
        copies = []
        for k in range(1, N_DEV):
            dx, dy, dc = (k >> 2) & 1, (k >> 1) & 1, k & 1
            tgt = (1 - x if dx else x, 1 - y if dy else y, 1 - c if dc else c)
            copies.append(pltpu.make_async_remote_copy(
                src_ref=buf.at[my], dst_ref=buf.at[my],
                send_sem=send_sems.at[k - 1], recv_sem=recv_sems.at[k - 1],
                device_id=tgt, device_id_type=MESH_ID))
        for cp in copies:
            cp.start()
        for cp in copies:
            cp.wait()
        tot = buf[0]
        for j in range(1, N_DEV):
            tot = tot + buf[j]
        d, mn, vn = _adamw(w_ref[...], tot, m_ref[...], v_ref[...])
        g_out[...] = tot
        d_out[...] = d
        m_out[...] = mn
        v_out[...] = vn

    vm = pl.BlockSpec(memory_space=pltpu.VMEM)
    return pl.pallas_call(
        body, name=name, out_shape=[SDS((rows, LANES), F32)] * 4,
        in_specs=[vm] * 4, out_specs=[vm] * 4,
        scratch_shapes=[pltpu.VMEM((N_DEV, rows, LANES), F32),
                        pltpu.SemaphoreType.DMA((N_DEV - 1,)), pltpu.SemaphoreType.DMA((N_DEV - 1,))],
    )(g, w, m, v)


BIG = ["l0_w_in", "l0_w_out", "l1_w_in", "l1_w_uq", "l1_w_ukv", "l1_w_out", "l2_w_in", "l2_w_out",
       "l3_w_in", "l3_w_out"]
SMALL = ["l0_norm", "l1_norm", "l1_q_a_norm", "l1_kv_a_norm", "l1_q_head_norm", "l1_k_head_norm",
         "l2_norm", "l2_q_head_norm", "l2_k_head_norm", "l2_sinks", "l3_norm"]

MLA_UQ_PERM = np.concatenate([np.arange(h * 192, h * 192 + 128) for h in range(8)]
                             + [np.arange(h * 192 + 128, (h + 1) * 192) for h in range(8)])
SWA_IN_PERM = np.concatenate([SWA_Q_PERM, 1536 + SWA_Q_PERM, np.arange(1024, 1536)])


def _inv(perm):
    inv = np.empty_like(perm)
    inv[perm] = np.arange(perm.size)
    return inv


OUTS = ["l0_w_out", "l1_w_out", "l2_w_out", "l3_w_out"]
REST = ["l1_w_in", "l1_w_uq", "l1_w_ukv", "l2_w_in"]


def _pack_rest(shards):
    parts = [shards[nm].reshape(-1, D_MODEL) for nm in REST]
    used = sum(p.shape[0] for p in parts)
    assert REST_ROW0 + used == ROWS_USED
    parts.append(jnp.zeros((REST_ROWS - used, D_MODEL), parts[0].dtype))
    return jnp.concatenate(parts, axis=0)


def _unpack_rest(packed, shapes):
    out, off = {}, 0
    for nm in REST:
        shp = shapes[nm]
        rows = shp[0] * shp[1] // D_MODEL
        out[nm] = packed[off:off + rows].reshape(shp)
        off += rows
    return out


def _pack(shards):
    pair = jnp.concatenate([shards["l0_w_in"], shards["l3_w_in"]], axis=1)
    return jnp.concatenate([pair] + [shards[nm] for nm in OUTS] + [_pack_rest(shards)], axis=0)


def _out_weight(gathered, layer):
    r0 = OUT_ROW0 + 128 * layer
    return gathered[:, r0:r0 + 128].reshape(D_MODEL, D_MODEL)


def _rest_full_weights(gathered, shapes):
    out, off = {}, REST_ROW0
    for nm in REST:
        shp = shapes[nm]
        rows = shp[0] * shp[1] // D_MODEL
        blk = gathered[:, off:off + rows].reshape((N_DEV,) + shp)
        out[nm] = jnp.transpose(blk, (1, 0, 2)).reshape(shp[0], N_DEV * shp[1])
        off += rows
    return out


def _grad_pieces(pair_l0, pair_l3, dw_out, rest_grads, shapes):
    parts = [jnp.concatenate([pair_l0, pair_l3], axis=2)]
    parts += [g.reshape(N_DEV, 128, D_MODEL).astype(BF16) for g in dw_out]
    used = REST_ROW0
    for nm in REST:
        shp = shapes[nm]
        blk = jnp.transpose(rest_grads[nm].reshape(shp[0], N_DEV, shp[1]), (1, 0, 2))
        parts.append(blk.reshape(N_DEV, -1, D_MODEL).astype(BF16))
        used += parts[-1].shape[1]
    parts.append(jnp.zeros((N_DEV, ROWS_PACK - used, D_MODEL), BF16))
    return jnp.concatenate(parts, axis=1)


SMALL_USED = 5008


def _pack_small(vals, extra=None):
    parts = [vals[nm].reshape(-1).astype(F32) for nm in SMALL]
    assert sum(p.shape[0] for p in parts) == SMALL_USED
    if extra is not None:
        parts.append(extra.reshape(1).astype(F32))
    flat = jnp.concatenate(parts)
    flat = jnp.concatenate([flat, jnp.zeros((SMALL_ROWS * LANES - flat.shape[0],), F32)])
    return flat.reshape(SMALL_ROWS, LANES)


def _unpack_small(packed, shapes):
    flat = packed.reshape(-1)
    out, off = {}, 0
    for nm in SMALL:
        n = shapes[nm][0]
        out[nm] = flat[off:off + n]
        off += n
    return out


def _row(v):
    return v.reshape(1, -1).astype(F32)


def _mla_gain_rows(g):
    return jnp.concatenate([g[:128], g[128:], g[128:]]).reshape(1, 256).astype(F32)


def _layer_fwd_sb(x, norm, gathered, shard_col, w_out, tag):
    proj, xn = norm_matmul(x, 0, _row(norm), gathered, f"{tag}_in", shard_col=shard_col)
    o = sb_attn_fwd(proj, f"{tag}_attn")
    y = gate_out_fwd(o, proj, 3, w_out, x, f"{tag}_out")
    return y, (x, xn, proj, o)


def _layer_bwd_sb(dy, saved, norm, gathered, shard_col, w_out, tag):
    x, xn, proj, o = saved
    do, dgate, hg = gate_out_bwd(dy, w_out, o, proj, 3, f"{tag}_outb")
    dw_out = tn_matmul(hg, dy, f"{tag}_dwout")
    dq, dk, dv = sb_attn_bwd(proj, o, do, f"{tag}_attnb")
    dproj = [dq, dk, dv, dgate]
    dw_in = tn_matmul_shards(xn, dproj, f"{tag}_dwin")
    dx, dnorm = nt_norm_bwd(dproj, gathered, x, 0, _row(norm), dy, f"{tag}_inb", F32, shard_col=shard_col)
    return dx, dw_in, dw_out, dnorm.reshape(-1)


def kernel(x, l0_norm, l0_w_in, l0_w_out, l1_norm, l1_w_in, l1_q_a_norm, l1_w_uq, l1_kv_a_norm, l1_w_ukv, l1_q_head_norm, l1_k_head_norm, l1_w_out, l2_norm, l2_w_in, l2_q_head_norm, l2_k_head_norm, l2_sinks, l2_w_out, l3_norm, l3_w_in, l3_w_out, loss_target, m_l0_norm, m_l0_w_in, m_l0_w_out, m_l1_norm, m_l1_w_in, m_l1_q_a_norm, m_l1_w_uq, m_l1_kv_a_norm, m_l1_w_ukv, m_l1_q_head_norm, m_l1_k_head_norm, m_l1_w_out, m_l2_norm, m_l2_w_in, m_l2_q_head_norm, m_l2_k_head_norm, m_l2_sinks, m_l2_w_out, m_l3_norm, m_l3_w_in, m_l3_w_out, v_l0_norm, v_l0_w_in, v_l0_w_out, v_l1_norm, v_l1_w_in, v_l1_q_a_norm, v_l1_w_uq, v_l1_kv_a_norm, v_l1_w_ukv, v_l1_q_head_norm, v_l1_k_head_norm, v_l1_w_out, v_l2_norm, v_l2_w_in, v_l2_q_head_norm, v_l2_k_head_norm, v_l2_sinks, v_l2_w_out, v_l3_norm, v_l3_w_in, v_l3_w_out):
    loc = dict(locals())
    names = BIG + SMALL
    w = {nm: loc[nm] for nm in names}
    mom = {nm: loc["m_" + nm] for nm in names}
    vel = {nm: loc["v_" + nm] for nm in names}
    shapes = {nm: w[nm].shape for nm in names}
    xs = x[0]
    tgt = loss_target[0]

    gathered = all_gather_packed(_pack({nm: w[nm].astype(BF16) for nm in BIG}), "ag_weights")
    loss_local, dx0, pieces, sg = _local_step(xs, tgt, gathered, {nm: w[nm] for nm in SMALL}, shapes)
    return _reduce_and_update(loss_local, dx0, pieces, sg, w, mom, vel, shapes)


def _local_step(xs, tgt, gathered, sm, shapes):
    s = xs.shape[0]
    fw = _rest_full_weights(gathered, shapes)
    w_out = [_out_weight(gathered, layer) for layer in range(4)]
    l0_norm, l1_norm, l2_norm, l3_norm = sm["l0_norm"], sm["l1_norm"], sm["l2_norm"], sm["l3_norm"]
    l1_q_a_norm, l1_kv_a_norm = sm["l1_q_a_norm"], sm["l1_kv_a_norm"]
    l1_q_head_norm, l1_k_head_norm = sm["l1_q_head_norm"], sm["l1_k_head_norm"]
    l2_q_head_norm, l2_k_head_norm, l2_sinks = sm["l2_q_head_norm"], sm["l2_k_head_norm"], sm["l2_sinks"]
    w1_in = jnp.concatenate([fw["l1_w_in"][:, 448:], fw["l1_w_in"][:, :448], jnp.zeros((D_MODEL, 64), BF16)], axis=1)
    w1_uq = fw["l1_w_uq"][:, MLA_UQ_PERM]
    w2_in = fw["l2_w_in"][:, SWA_IN_PERM]
    w2_out = w_out[2][SWA_Q_PERM, :]

    x1, sv0 = _layer_fwd_sb(xs, l0_norm, gathered, 0, w_out[0], "l0")

    cos, sin = _rope_tables(s)
    gq, gk = _mla_gain_rows(l1_q_head_norm), _mla_gain_rows(l1_k_head_norm)
    proj1, xn1 = norm_matmul(x1, 0, _row(l1_norm), w1_in, "l1_in")
    qa, qln = norm_matmul(proj1, 4, _row(l1_q_a_norm), w1_uq, "l1_uq")
    kva, kvn = norm_matmul(proj1, 10, _row(l1_kv_a_norm), fw["l1_w_ukv"], "l1_ukv")
    q_p, k_p = mla_prep_fwd(qa, kva, proj1, gq, gk, cos, sin, "l1_prep")
    o1, lse1 = mla_attn_fwd(q_p, k_p, kva, "l1_attn")
    x2 = gate_out_fwd(o1, proj1, 0, w_out[1], x1, "l1_out")

    g2 = jnp.stack([jnp.tile(l2_q_head_norm, 2), jnp.tile(l2_k_head_norm, 2)]).astype(F32)
    proj2, xn2 = norm_matmul(x2, 0, _row(l2_norm), w2_in, "l2_in")
    qkn2 = swa_prep_fwd(proj2, g2, "l2_prep")
    bias2 = _swa_bias()
    o2 = swa_attn_fwd(qkn2, proj2, l2_sinks, bias2, "l2_attn")
    x3 = gate_out_fwd(o2, proj2, 1, w2_out, x2, "l2_out")

    x4, sv3 = _layer_fwd_sb(x3, l3_norm, gathered, 1, w_out[3], "l3")

    lossblk, dy = loss_kernel(x4, tgt, "loss")

    fg, sg = {}, {}
    dw_out = [None] * 4
    dx3, pair_l3, dw_out[3], sg["l3_norm"] = _layer_bwd_sb(dy, sv3, l3_norm, gathered, 1, w_out[3], "l3")

    do2, dgate2, hg2 = gate_out_bwd(dx3, w2_out, o2, proj2, 1, "l2_outb")
    dw2_out = tn_matmul(hg2, dx3, "l2_dwout")
    dqn2, dkn2, dv2, dsink2 = swa_attn_bwd(qkn2, proj2, l2_sinks, bias2, do2, "l2_attnb")
    dqk2, dg2 = swa_prep_bwd(dqn2, dkn2, proj2, g2, "l2_prepb")
    dproj2 = jnp.concatenate([dqk2[:, :1024], dgate2, dqk2[:, 1024:], dv2.astype(BF16)], axis=1)
    dw2_in = tn_matmul(xn2, dproj2, "l2_dwin")
    dx2, dn2 = nt_norm_bwd(dproj2, w2_in, x2, 0, _row(l2_norm), dx3, "l2_inb", F32)
    fg["l2_w_in"] = dw2_in[:, _inv(SWA_IN_PERM)]
    dw_out[2] = dw2_out[_inv(SWA_Q_PERM), :]
    sg["l2_norm"] = dn2.reshape(-1)
    sg["l2_q_head_norm"] = dg2[0, :HALF]
    sg["l2_k_head_norm"] = dg2[1, :HALF]
    sg["l2_sinks"] = dsink2[0, :SWA_HEADS]

    do1, dgate1, hg1 = gate_out_bwd(dx2, w_out[1], o1, proj1, 0, "l1_outb")
    dw_out[1] = tn_matmul(hg1, dx2, "l1_dwout")
    dq_p, dk_p, dv1 = mla_attn_bwd(q_p, k_p, kva, o1, do1, lse1, "l1_attnb")
    dqa, dkva, dkpe, dgq, dgk = mla_prep_bwd(dq_p, dk_p, dv1, qa, kva, proj1, gq, gk, cos, sin, "l1_prepb")
    dw_uq = tn_matmul(qln, dqa, "l1_dwuq")
    fg["l1_w_ukv"] = tn_matmul(kvn, dkva, "l1_dwukv")
    dqlat, dgqa = nt_norm_bwd(dqa, w1_uq, proj1, 4, _row(l1_q_a_norm), None, "l1_uqb", BF16)
    dkvlat, dgkva = nt_norm_bwd(dkva, fw["l1_w_ukv"], proj1, 10, _row(l1_kv_a_norm), None, "l1_ukvb", BF16)
    dproj1 = jnp.concatenate([dgate1, dqlat, dkvlat, dkpe], axis=1)
    dw1_in = tn_matmul(xn1, dproj1, "l1_dwin")
    dx1, dn1 = nt_norm_bwd(dproj1, w1_in, x1, 0, _row(l1_norm), dx2, "l1_inb", F32)
    fg["l1_w_in"] = jnp.concatenate([dw1_in[:, 1024:1472], dw1_in[:, :1024]], axis=1)
    fg["l1_w_uq"] = dw_uq[:, _inv(MLA_UQ_PERM)]
    sg["l1_norm"] = dn1.reshape(-1)
    sg["l1_q_a_norm"] = dgqa.reshape(-1)
    sg["l1_kv_a_norm"] = dgkva.reshape(-1)
    sg["l1_q_head_norm"] = jnp.concatenate([dgq[0, :128], dgq[0, 128:192] + dgq[0, 192:256]])
    sg["l1_k_head_norm"] = jnp.concatenate([dgk[0, :128], dgk[0, 128:192] + dgk[0, 192:256]])

    dx0, pair_l0, dw_out[0], sg["l0_norm"] = _layer_bwd_sb(dx1, sv0, l0_norm, gathered, 0, w_out[0], "l0")
    return lossblk[0, 0], dx0, _grad_pieces(pair_l0, pair_l3, dw_out, fg, shapes), sg


def _reduce_and_update(loss_local, dx0, pieces, sg, w, mom, vel, shapes):
    xi, yi, ci = _my_pos()
    cidx = jnp.reshape(ci, (1,)).astype(jnp.int32)
    chipidx = jnp.reshape(2 * xi + yi, (1,)).astype(jnp.int32)
    ra = rs_sibling_exchange(pieces, "rs_sibling")
    part = rs_pair_add(pieces, ra, cidx, "rs_pair_add")
    rb = rs_cross_exchange(part, "rs_cross")

    def update(nm, row0, col_blk):
        return rs_final_adamw(part, rb, chipidx, w[nm], mom[nm], vel[nm], row0, col_blk, f"adamw_{nm}")

    big = {"l0_w_in": update("l0_w_in", 0, 0), "l3_w_in": update("l3_w_in", 0, 1)}
    for layer, nm in enumerate(OUTS):
        big[nm] = update(nm, OUT_ROW0 + 128 * layer, 0)
    rest = rs_final_adamw(part, rb, chipidx, _pack_rest(w), _pack_rest(mom), _pack_rest(vel), REST_ROW0, 0,
                          "adamw_rest")
    gs, ds, ms, vs = small_allreduce_adamw(_pack_small(sg, loss_local), _pack_small({nm: w[nm] for nm in SMALL}),
                                           _pack_small({nm: mom[nm] for nm in SMALL}),
                                           _pack_small({nm: vel[nm] for nm in SMALL}), "small_allreduce")

    outs = []
    for i, small in enumerate((gs, ds, ms, vs)):
        d = {nm: big[nm][i] for nm in big}
        d.update(_unpack_rest(rest[i], shapes))
        d.update(_unpack_small(small, shapes))
        outs.append(d)
    order = ["l0_norm", "l0_w_in", "l0_w_out", "l1_norm", "l1_w_in", "l1_q_a_norm", "l1_w_uq", "l1_kv_a_norm",
             "l1_w_ukv", "l1_q_head_norm", "l1_k_head_norm", "l1_w_out", "l2_norm", "l2_w_in", "l2_q_head_norm",
             "l2_k_head_norm", "l2_sinks", "l2_w_out", "l3_norm", "l3_w_in", "l3_w_out"]
    flat = [gs.reshape(-1)[SMALL_USED], dx0[None]]
    for d in outs:
        flat += [d[nm] for nm in order]
    return tuple(flat)
```

```python
import math

import numpy as np
import jax
import jax.numpy as jnp
from jax import lax
from jax.experimental import pallas as pl
from jax.experimental.pallas import tpu as pltpu

F32 = jnp.float32
BF16 = jnp.bfloat16
SDS = jax.ShapeDtypeStruct
MESH_ID = pl.DeviceIdType.MESH

D_MODEL = 1024
NORM_EPS = 1e-6
N_DEV = 8
LANES = 128
HALF = 64
ROWS_PACK = 2176
ROWS_USED = 2120
PAIR_COLS = 512
OUT_ROW0 = 1024
REST_ROW0 = 1536
REST_ROWS = ROWS_PACK - REST_ROW0
SMALL_ROWS = 40

ADAM_LR = 0.001
ADAM_B1 = 0.9
ADAM_B2 = 0.999
ADAM_EPS = 1e-08
ADAM_WD = 0.01
ADAM_STEP = 10

NT_DIMS = (((1,), (1,)), ((), ()))
TN_DIMS = (((0,), (0,)), ((), ()))


def _cparams(sem=None, vmem_mb=48):
    return pltpu.CompilerParams(dimension_semantics=sem, vmem_limit_bytes=vmem_mb * 2 ** 20)


def _dot(a, b):
    return jnp.dot(a, b, preferred_element_type=F32)


def _dot_nt(a, b):
    return lax.dot_general(a, b, NT_DIMS, preferred_element_type=F32)


def _dot_tn(a, b):
    return lax.dot_general(a, b, TN_DIMS, preferred_element_type=F32)


def _split_dot(a, b):
    hi = a.astype(BF16)
    lo = (a - hi.astype(F32)).astype(BF16)
    return _dot(hi, b) + _dot(lo, b)


def _cumsum_dot(lf, u):
    return _dot(lf.astype(BF16), u)


def _pick_tile(n, cap):
    return max(t for t in range(LANES, min(n, cap) + 1, LANES) if n % t == 0)


def _sigmoid(x):
    return 1.0 / (1.0 + jnp.exp(-x))


def norm_matmul(x, colblk, g, w, name, shard_col=None):
    s = x.shape[0]
    if shard_col is None:
        k, n = w.shape
        tn = _pick_tile(n, 1024)
        w_spec = pl.BlockSpec((k, tn), lambda i, j: (0, j))
    else:
        k, n, tn = D_MODEL, N_DEV * PAIR_COLS, 2 * PAIR_COLS
        w_spec = pl.BlockSpec((2, k, PAIR_COLS), lambda i, j: (j, 0, shard_col))
    tm = min(1024, s)
    assert s % tm == 0

    def body(x_ref, g_ref, w_ref, o_ref, xn_ref, xn_s):
        @pl.when(pl.program_id(1) == 0)
        def _():
            xf = x_ref[...].astype(F32)
            ms = jnp.mean(xf * xf, axis=-1, keepdims=True)
            y = (xf * lax.rsqrt(ms + NORM_EPS) * g_ref[...]).astype(BF16)
            xn_s[...] = y
            xn_ref[...] = y

        if shard_col is None:
            o_ref[...] = _dot(xn_s[...], w_ref[...]).astype(o_ref.dtype)
        else:
            for half in range(2):
                o_ref[:, half * PAIR_COLS:(half + 1) * PAIR_COLS] = _dot(xn_s[...], w_ref[half]).astype(o_ref.dtype)

    return pl.pallas_call(
        body, name=name, grid=(s // tm, n // tn),
        in_specs=[pl.BlockSpec((tm, k), lambda i, j: (i, colblk)),
                  pl.BlockSpec((1, k), lambda i, j: (0, 0)),
                  w_spec],
        out_specs=[pl.BlockSpec((tm, tn), lambda i, j: (i, j)),
                   pl.BlockSpec((tm, k), lambda i, j: (i, 0))],
        out_shape=[SDS((s, n), BF16), SDS((s, k), BF16)],
        scratch_shapes=[pltpu.VMEM((tm, k), BF16)],
        compiler_params=_cparams(("parallel", "arbitrary")),
    )(x, g, w)


def gate_out_fwd(o, proj, gate_blk, w_out, x, name):
    s = x.shape[0]
    d = D_MODEL
    tm = min(512, s)

    def body(o_ref, g_ref, w_ref, x_ref, y_ref):
        g = g_ref[...].astype(F32)
        hg = (o_ref[...].astype(F32) * (g * _sigmoid(g))).astype(BF16)
        y_ref[...] = x_ref[...] + _dot(hg, w_ref[...])

    return pl.pallas_call(
        body, name=name, grid=(s // tm,),
        in_specs=[pl.BlockSpec((tm, d), lambda i: (i, 0)),
                  pl.BlockSpec((tm, d), lambda i: (i, gate_blk)),
                  pl.BlockSpec((d, d), lambda i: (0, 0)),
                  pl.BlockSpec((tm, d), lambda i: (i, 0))],
        out_specs=pl.BlockSpec((tm, d), lambda i: (i, 0)),
        out_shape=SDS((s, d), F32),
        compiler_params=_cparams(("parallel",)),
    )(o, proj, w_out, x)


def gate_out_bwd(dy, w_out, o, proj, gate_blk, name):
    s = dy.shape[0]
    d = D_MODEL
    tm = min(512, s)

    def body(dy_ref, w_ref, o_ref, g_ref, do_ref, dg_ref, hg_ref):
        dh = _dot_nt(dy_ref[...].astype(BF16), w_ref[...])
        g = g_ref[...].astype(F32)
        o = o_ref[...].astype(F32)
        sg = _sigmoid(g)
        silu = g * sg
        do_ref[...] = (dh * silu).astype(BF16)
        dg_ref[...] = (dh * o * (sg * (1.0 + g * (1.0 - sg)))).astype(BF16)
        hg_ref[...] = (o * silu).astype(BF16)

    blk = pl.BlockSpec((tm, d), lambda i: (i, 0))
    return pl.pallas_call(
        body, name=name, grid=(s // tm,),
        in_specs=[blk, pl.BlockSpec((d, d), lambda i: (0, 0)), blk,
                  pl.BlockSpec((tm, d), lambda i: (i, gate_blk))],
        out_specs=[blk, blk, blk],
        out_shape=[SDS((s, d), BF16)] * 3,
        compiler_params=_cparams(("parallel",)),
    )(dy, w_out, o, proj)


def tn_matmul(a, b, name):
    s, k = a.shape
    n = b.shape[1]
    tk = min(1024, k)
    tn = _pick_tile(n, 512)
    ts = min(2048, s)
    assert k % tk == 0 and s % ts == 0

    def body(a_ref, b_ref, o_ref):
        @pl.when(pl.program_id(2) == 0)
        def _():
            o_ref[...] = jnp.zeros_like(o_ref)

        o_ref[...] += _dot_tn(a_ref[...].astype(BF16), b_ref[...].astype(BF16))

    return pl.pallas_call(
        body, name=name, grid=(k // tk, n // tn, s // ts),
        in_specs=[pl.BlockSpec((ts, tk), lambda i, j, r: (r, i)),
                  pl.BlockSpec((ts, tn), lambda i, j, r: (r, j))],
        out_specs=pl.BlockSpec((tk, tn), lambda i, j, r: (i, j)),
        out_shape=SDS((k, n), F32),
        compiler_params=_cparams(("parallel", "parallel", "arbitrary")),
    )(a, b)


def tn_matmul_shards(a, b_parts, name):
    s, k = a.shape
    per = D_MODEL // PAIR_COLS
    assert len(b_parts) * per == N_DEV and a.dtype == BF16
    assert all(b.shape == (s, D_MODEL) and b.dtype == BF16 for b in b_parts)

    def body(a_ref, *refs):
        o_ref = refs[-1]
        j = pl.program_id(0)
        for p, b_ref in enumerate(refs[:-1]):
            @pl.when(j // per == p)
            def _(b_ref=b_ref):
                o_ref[...] = _dot_tn(a_ref[...], b_ref[...]).astype(BF16)

    part_specs = [pl.BlockSpec((s, PAIR_COLS), lambda j, p=p: (0, jnp.clip(j - per * p, 0, per - 1)))
                  for p in range(len(b_parts))]
    return pl.pallas_call(
        body, name=name, grid=(N_DEV,),
        in_specs=[pl.BlockSpec((s, k), lambda j: (0, 0))] + part_specs,
        out_specs=pl.BlockSpec((None, k, PAIR_COLS), lambda j: (j, 0, 0)),
        out_shape=SDS((N_DEV, k, PAIR_COLS), BF16),
        compiler_params=_cparams(("arbitrary",)),
    )(a, *b_parts)


def nt_norm_bwd(dp, w, x, xcol, g, res, name, out_dtype, shard_col=None):
    tm = min(512, dp[0].shape[0] if shard_col is not None else dp.shape[0])
    if shard_col is None:
        s, n = dp.shape
        k = w.shape[0]
        w_spec = pl.BlockSpec((k, n), lambda i: (0, 0))
        dp_parts = [dp]
        dp_specs = [pl.BlockSpec((tm, n), lambda i: (i, 0))]
    else:
        s, k = dp[0].shape[0], D_MODEL
        w_spec = pl.BlockSpec((N_DEV, k, PAIR_COLS), lambda i: (0, 0, shard_col))
        dp_parts = list(dp)
        dp_specs = [pl.BlockSpec((tm, D_MODEL), lambda i: (i, 0)) for _ in dp_parts]
    np_ = len(dp_parts)
    per = D_MODEL // PAIR_COLS
    assert s % tm == 0
    has_res = res is not None

    def body(*refs):
        dp_refs = refs[:np_]
        if has_res:
            w_ref, x_ref, g_ref, r_ref, dx_ref, dg_ref = refs[np_:]
        else:
            w_ref, x_ref, g_ref, dx_ref, dg_ref = refs[np_:]

        @pl.when(pl.program_id(0) == 0)
        def _():
            dg_ref[...] = jnp.zeros_like(dg_ref)

        if shard_col is None:
            dxn = _dot_nt(dp_refs[0][...], w_ref[...])
        else:
            dxn = None
            for j in range(N_DEV):
                c0 = (j % per) * PAIR_COLS
                term = _dot_nt(dp_refs[j // per][:, c0:c0 + PAIR_COLS], w_ref[j])
                dxn = term if dxn is None else dxn + term
        xf = x_ref[...].astype(F32)
        rstd = lax.rsqrt(jnp.mean(xf * xf, axis=-1, keepdims=True) + NORM_EPS)
        xhat = xf * rstd
        dg_ref[...] += jnp.sum(dxn * xhat, axis=0, keepdims=True)
        dxh = dxn * g_ref[...]
        dx = rstd * (dxh - xhat * jnp.mean(dxh * xhat, axis=-1, keepdims=True))
        if has_res:
            dx = dx + r_ref[...]
        dx_ref[...] = dx.astype(out_dtype)

    in_specs = dp_specs + [w_spec,
                           pl.BlockSpec((tm, k), lambda i: (i, xcol)),
                           pl.BlockSpec((1, k), lambda i: (0, 0))]
    args = dp_parts + [w, x, g]
    if has_res:
        in_specs.append(pl.BlockSpec((tm, k), lambda i: (i, 0)))
        args.append(res)
    return pl.pallas_call(
        body, name=name, grid=(s // tm,),
        in_specs=in_specs,
        out_specs=[pl.BlockSpec((tm, k), lambda i: (i, 0)),
                   pl.BlockSpec((1, k), lambda i: (0, 0))],
        out_shape=[SDS((s, k), out_dtype), SDS((1, k), F32)],
        compiler_params=_cparams(("arbitrary",)),
    )(*args)


def loss_kernel(y, target, name):
    s, d = y.shape
    tm = min(512, s)

    def body(y_ref, t_ref, l_ref, dy_ref):
        @pl.when(pl.program_id(0) == 0)
        def _():
            l_ref[...] = jnp.zeros_like(l_ref)

        e = y_ref[...] - t_ref[...]
        dy_ref[...] = e * (1.0 / d)
        l_ref[...] += 0.5 * jnp.sum(jnp.mean(e * e, axis=-1, keepdims=True))

    blk = pl.BlockSpec((tm, d), lambda i: (i, 0))
    return pl.pallas_call(
        body, name=name, grid=(s // tm,),
        in_specs=[blk, blk],
        out_specs=[pl.BlockSpec((8, LANES), lambda i: (0, 0)), blk],
        out_shape=[SDS((8, LANES), F32), SDS((s, d), F32)],
        compiler_params=_cparams(("arbitrary",)),
    )(y, target)


SB_LOG_CUTOFF = -80.0
SB_T = 256
SB_FWD_TILES = 4
SB_BWD_TILES = 2


def _sb_sweep(qi, cs, rest, step):
    n = len(cs)

    def top(arrs):
        m = jnp.max(arrs[0])
        for a in arrs[1:]:
            m = jnp.maximum(m, jnp.max(a))
        return m

    def cond(st):
        return (st[0] <= qi) & (st[1] > SB_LOG_CUTOFF)

    def body(st):
        cs2, rest2 = step(st[0], st[2:2 + n], st[2 + n:])
        return (st[0] + 1, top(cs2)) + tuple(cs2) + tuple(rest2)

    return lax.while_loop(cond, body, (jnp.int32(1), top(cs)) + tuple(cs) + tuple(rest))[2 + n:]


def _tile_masks(t):
    row = lax.broadcasted_iota(jnp.int32, (t, t), 0)
    col = lax.broadcasted_iota(jnp.int32, (t, t), 1)
    return row, col


def sb_attn_fwd(proj, name):
    s = proj.shape[0]
    n_tiles = SB_FWD_TILES
    t = min(SB_T, s)
    nq = s // t
    scale = 1.0 / math.sqrt(HALF)

    def body(q_ref, k_ref, v_ref, o_ref):
        lane = lax.broadcasted_iota(jnp.int32, (1, LANES), 1)
        row, col = _tile_masks(t)
        u_incl = (row >= col).astype(BF16)
        tri = jnp.concatenate([col < row] * 2, axis=0)

        def scores(pr, qs, kj):
            k = k_ref[pl.ds(pl.multiple_of(kj * t, t), t), pr * LANES:(pr + 1) * LANES]
            z = _dot_nt(qs, k)
            nz = -z
            lf = jnp.minimum(nz, 0.0) - jnp.log(1.0 + jnp.exp(jnp.minimum(z, nz)))
            return z, lf

        def accumulate(pr, kj, z, lf, c, acc, diag):
            v = v_ref[pl.ds(pl.multiple_of(kj * t, t), t), pr * LANES:(pr + 1) * LANES]
            if diag:
                lf = jnp.where(tri, lf, 0.0)
            incl = _cumsum_dot(lf, u_incl) + c
            a = jnp.exp(z + incl)
            if diag:
                a = jnp.where(tri, a, 0.0)
            acc = acc + _dot(a.astype(BF16), v)
            c = c + jnp.sum(lf, axis=1, keepdims=True)
            return c, acc

        def q_body(qi, carry):
            q0 = pl.multiple_of(qi * t, t)
            qss = []
            for pr in range(n_tiles):
                q = q_ref[pl.ds(q0, t), pr * LANES:(pr + 1) * LANES] * scale
                zq = jnp.zeros_like(q)
                qss.append(jnp.concatenate([jnp.where(lane < HALF, q, zq), jnp.where(lane >= HALF, q, zq)], axis=0))

            def step(jj, cs, accs, diag=False):
                outs = [accumulate(pr, qi - jj, *scores(pr, qss[pr], qi - jj), cs[pr], accs[pr], diag)
                        for pr in range(n_tiles)]
                return tuple(o[0] for o in outs), tuple(o[1] for o in outs)

            zero = (jnp.zeros((2 * t, 1), F32),) * n_tiles, (jnp.zeros((2 * t, LANES), F32),) * n_tiles
            cs, accs = step(0, zero[0], zero[1], True)
            accs = _sb_sweep(qi, cs, accs, step)
            for pr in range(n_tiles):
                o_ref[pl.ds(q0, t), pr * LANES:(pr + 1) * LANES] = jnp.where(lane < HALF, accs[pr][:t], accs[pr][t:])
            return carry

        lax.fori_loop(0, nq, q_body, 0)

    w = n_tiles * LANES
    return pl.pallas_call(
        body, name=name, grid=(D_MODEL // w,),
        in_specs=[pl.BlockSpec((s, w), lambda p: (0, p)),
                  pl.BlockSpec((s, w), lambda p: (0, D_MODEL // w + p)),
                  pl.BlockSpec((s, w), lambda p: (0, 2 * (D_MODEL // w) + p))],
        out_specs=pl.BlockSpec((s, w), lambda p: (0, p)),
        out_shape=SDS((s, D_MODEL), F32),
        compiler_params=_cparams(("parallel",), vmem_mb=56),
    )(proj, proj, proj)


def sb_attn_bwd(proj, o, do, name):
    s = proj.shape[0]
    n_tiles = SB_BWD_TILES
    t = min(SB_T, s)
    nq = s // t
    scale = 1.0 / math.sqrt(HALF)

    def body(q_ref, k_ref, v_ref, o_ref, do_ref, dq_ref, dk_ref, dv_ref, dk_s, dv_s):
        dk_s[...] = jnp.zeros_like(dk_s)
        dv_s[...] = jnp.zeros_like(dv_s)
        lane = lax.broadcasted_iota(jnp.int32, (1, LANES), 1)
        row, col = _tile_masks(t)
        u_incl = (row >= col).astype(BF16)
        u_excl = (row > col).astype(BF16)
        tri = jnp.concatenate([col < row] * 2, axis=0)

        def scores(pr, qs, dos, kj):
            k0 = pl.multiple_of(kj * t, t)
            cols = slice(pr * LANES, (pr + 1) * LANES)
            z = _dot_nt(qs, k_ref[pl.ds(k0, t), cols])
            nz = -z
            lf = jnp.minimum(nz, 0.0) - jnp.log(1.0 + jnp.exp(jnp.minimum(z, nz)))
            return z, lf, _dot_nt(dos, v_ref[pl.ds(k0, t), cols])

        def accumulate(pr, qs, dos, tsum, kj, z, lf, da, c, r, dq, diag):
            k0 = pl.multiple_of(kj * t, t)
            cols = slice(pr * LANES, (pr + 1) * LANES)
            sig = jnp.exp(z + lf)
            if diag:
                lf = jnp.where(tri, lf, 0.0)
            incl = _cumsum_dot(lf, u_incl) + c
            a = jnp.exp(z + incl)
            if diag:
                a = jnp.where(tri, a, 0.0)
            ab = a.astype(BF16)
            dl = ab.astype(F32) * da
            pre = tsum - r - _split_dot(dl, u_excl)
            dz = dl - sig * pre
            if diag:
                dz = jnp.where(tri, dz, 0.0)
            dzb = dz.astype(BF16)
            dq = dq + _dot(dzb, k_ref[pl.ds(k0, t), cols])
            dk_s[pl.ds(k0, t), cols] += _dot_tn(dzb, qs)
            dv_s[pl.ds(k0, t), cols] += _dot_tn(ab, dos)
            c = c + jnp.sum(lf, axis=1, keepdims=True)
            r = r + jnp.sum(dl, axis=1, keepdims=True)
            return c, r, dq

        def q_body(qi, carry):
            q0 = pl.multiple_of(qi * t, t)
            lo, hi = lane < HALF, lane >= HALF
            qss, doss, tsums = [], [], []
            for pr in range(n_tiles):
                cols = slice(pr * LANES, (pr + 1) * LANES)
                q = q_ref[pl.ds(q0, t), cols] * scale
                dob = do_ref[pl.ds(q0, t), cols]
                prod = dob.astype(F32) * o_ref[pl.ds(q0, t), cols]
                zb = jnp.zeros_like(q)
                qss.append(jnp.concatenate([jnp.where(lo, q, zb), jnp.where(hi, q, zb)], axis=0))
                doss.append(jnp.concatenate([jnp.where(lo, dob, zb), jnp.where(hi, dob, zb)], axis=0))
                tsums.append(jnp.concatenate([jnp.sum(jnp.where(lo, prod, 0.0), axis=1, keepdims=True),
                                              jnp.sum(jnp.where(hi, prod, 0.0), axis=1, keepdims=True)], axis=0))

            def step(jj, cs, rest, diag=False):
                outs = [accumulate(pr, qss[pr], doss[pr], tsums[pr], qi - jj,
                                   *scores(pr, qss[pr], doss[pr], qi - jj),
                                   cs[pr], rest[pr], rest[n_tiles + pr], diag) for pr in range(n_tiles)]
                return tuple(o[0] for o in outs), tuple(o[1] for o in outs) + tuple(o[2] for o in outs)

            zc = (jnp.zeros((2 * t, 1), F32),) * n_tiles
            cs, rest = step(0, zc, zc + (jnp.zeros((2 * t, LANES), F32),) * n_tiles, True)
            rest = _sb_sweep(qi, cs, rest, step)
            for pr in range(n_tiles):
                dq = rest[n_tiles + pr]
                dq_ref[pl.ds(q0, t), pr * LANES:(pr + 1) * LANES] = (jnp.where(lo, dq[:t], dq[t:]) * scale).astype(BF16)
            return carry

        lax.fori_loop(0, nq, q_body, 0)
        dk_ref[...] = dk_s[...].astype(BF16)
        dv_ref[...] = dv_s[...].astype(BF16)

    w = n_tiles * LANES
    nblk = D_MODEL // w
    blk = lambda off: pl.BlockSpec((s, w), lambda p: (0, off * nblk + p))
    return pl.pallas_call(
        body, name=name, grid=(nblk,),
        in_specs=[blk(0), blk(1), blk(2), blk(0), blk(0)],
        out_specs=[blk(0), blk(0), blk(0)],
        out_shape=[SDS((s, D_MODEL), BF16)] * 3,
        scratch_shapes=[pltpu.VMEM((s, w), F32), pltpu.VMEM((s, w), F32)],
        compiler_params=_cparams(("parallel",), vmem_mb=56),
    )(proj, proj, proj, o, do)


MLA_HEADS = 8
MLA_QK = 192
MLA_SCALE = 1.0 / math.sqrt(MLA_QK)
MLA_TK = 512


def _rope_tables(s):
    inv_freq = 10000.0 ** (-jnp.arange(32, dtype=F32) / 32)
    ang = jnp.arange(s, dtype=F32)[:, None] * inv_freq[None, :]
    cos = jnp.tile(jnp.cos(ang), (1, 4))
    sin = jnp.sin(ang)
    sin_signed = jnp.tile(jnp.concatenate([-sin, sin], axis=1), (1, 2))
    return cos, sin_signed


def _partner_matrix():
    j = lax.broadcasted_iota(jnp.int32, (LANES, LANES), 0)
    l = lax.broadcasted_iota(jnp.int32, (LANES, LANES), 1)
    return (j == l + jnp.where((l % HALF) < 32, 32, -32)).astype(BF16)


def _partner(u, pmat):
    return _split_dot(u, pmat)


def _half_sum(x):
    j = lax.broadcasted_iota(jnp.int32, (LANES, LANES), 0)
    l = lax.broadcasted_iota(jnp.int32, (LANES, LANES), 1)
    return _dot(x.astype(BF16), ((j < HALF) == (l < HALF)).astype(BF16))


def _lane_sum(x):
    return _dot(x.astype(BF16), jnp.ones((LANES, LANES), BF16))


def _mla_head_inputs(h, qa_ref, kva_ref, kpe, lane):
    a = h % 2
    hm = (lane < HALF) if a == 0 else (lane >= HALF)
    qn = qa_ref[:, h * 128:(h + 1) * 128].astype(F32)
    qr = jnp.where(hm, qa_ref[:, 1024 + (h // 2) * 128:1024 + (h // 2 + 1) * 128].astype(F32), 0.0)
    kn = kva_ref[:, h * 256:h * 256 + 128].astype(F32)
    kr = kpe if a == 0 else pltpu.roll(kpe, HALF, 1)
    return qn, qr, kn, kr


def mla_prep_fwd(qa, kva, proj, gq, gk, cos, sin, name):
    s = qa.shape[0]
    tm = min(256, s)

    def body(qa_ref, kva_ref, kpe_ref, gq_ref, gk_ref, cos_ref, sin_ref, q_out, k_out):
        lane = lax.broadcasted_iota(jnp.int32, (1, LANES), 1)
        pmat = _partner_matrix()
        kpe = kpe_ref[...].astype(F32)
        cs, sn = cos_ref[...], sin_ref[...]
        for h in range(MLA_HEADS):
            qn, qr, kn, kr = _mla_head_inputs(h, qa_ref, kva_ref, kpe, lane)
            for xn, xr, g_ref, out, sc in ((qn, qr, gq_ref, q_out, MLA_SCALE), (kn, kr, gk_ref, k_out, 1.0)):
                ss = _lane_sum(xn * xn + xr * xr)
                rstd = lax.rsqrt(ss * (1.0 / MLA_QK) + NORM_EPS) * sc
                yn = xn * rstd * g_ref[:, 0:128]
                ur = xr * rstd * g_ref[:, 128:256]
                yr = ur * cs + _partner(ur, pmat) * sn
                out[:, h * 256:h * 256 + 128] = yn.astype(BF16)
                out[:, h * 256 + 128:h * 256 + 256] = yr.astype(BF16)

    row = lambda w, c: pl.BlockSpec((tm, w), lambda i: (i, c))
    full = lambda w: pl.BlockSpec((1, w), lambda i: (0, 0))
    return pl.pallas_call(
        body, name=name, grid=(s // tm,),
        in_specs=[row(1536, 0), row(2048, 0), row(128, 11), full(256), full(256), row(128, 0), row(128, 0)],
        out_specs=[row(2048, 0), row(2048, 0)],
        out_shape=[SDS((s, 2048), BF16)] * 2,
        compiler_params=_cparams(("parallel",)),
    )(qa, kva, proj, gq, gk, cos, sin)


def mla_prep_bwd(dq_p, dk_p, dv, qa, kva, proj, gq, gk, cos, sin, name):
    s = qa.shape[0]
    tm = min(256, s)

    def body(dq_ref, dk_ref, dv_ref, qa_ref, kva_ref, kpe_ref, gq_ref, gk_ref, cos_ref, sin_ref,
             dqa_ref, dkva_ref, dkpe_ref, dgq_ref, dgk_ref):
        @pl.when(pl.program_id(0) == 0)
        def _():
            dgq_ref[...] = jnp.zeros_like(dgq_ref)
            dgk_ref[...] = jnp.zeros_like(dgk_ref)

        lane = lax.broadcasted_iota(jnp.int32, (1, LANES), 1)
        pmat = _partner_matrix()
        kpe = kpe_ref[...].astype(F32)
        cs, sn = cos_ref[...], sin_ref[...]
        dkpe = jnp.zeros((tm, LANES), F32)
        dqr_pair = None
        for h in range(MLA_HEADS):
            qn, qr, kn, kr = _mla_head_inputs(h, qa_ref, kva_ref, kpe, lane)
            res = []
            for xn, xr, g_ref, d_ref, dg_ref, sc in ((qn, qr, gq_ref, dq_ref, dgq_ref, MLA_SCALE),
                                                     (kn, kr, gk_ref, dk_ref, dgk_ref, 1.0)):
                ss = _lane_sum(xn * xn + xr * xr)
                rstd = lax.rsqrt(ss * (1.0 / MLA_QK) + NORM_EPS)
                hn, hr = xn * rstd, xr * rstd
                dyn = d_ref[:, h * 256:h * 256 + 128] * sc
                dyr = d_ref[:, h * 256 + 128:h * 256 + 256] * sc
                dur = dyr * cs - _partner(dyr, pmat) * sn
                dg_ref[:, 0:128] += jnp.sum(dyn * hn, axis=0, keepdims=True)
                dg_ref[:, 128:256] += jnp.sum(dur * hr, axis=0, keepdims=True)
                dhn = dyn * g_ref[:, 0:128]
                dhr = dur * g_ref[:, 128:256]
                mu = _lane_sum(dhn * hn + dhr * hr) * (1.0 / MLA_QK)
                res.append((rstd * (dhn - hn * mu), rstd * (dhr - hr * mu)))
            (dqn, dqr), (dkn, dkr) = res
            dqa_ref[:, h * 128:(h + 1) * 128] = dqn.astype(BF16)
            if h % 2 == 0:
                dqr_pair = dqr
            else:
                dqa_ref[:, 1024 + (h // 2) * 128:1024 + (h // 2 + 1) * 128] = (dqr_pair + dqr).astype(BF16)
            dkva_ref[:, h * 256:h * 256 + 128] = dkn.astype(BF16)
            dkva_ref[:, h * 256 + 128:h * 256 + 256] = dv_ref[:, h * 128:(h + 1) * 128]
            dkpe = dkpe + (dkr if h % 2 == 0 else pltpu.roll(dkr, HALF, 1))
        dkpe_ref[...] = dkpe.astype(BF16)

    row = lambda w, c: pl.BlockSpec((tm, w), lambda i: (i, c))
    full = lambda w: pl.BlockSpec((1, w), lambda i: (0, 0))
    return pl.pallas_call(
        body, name=name, grid=(s // tm,),
        in_specs=[row(2048, 0), row(2048, 0), row(1024, 0), row(1536, 0), row(2048, 0), row(128, 11),
                  full(256), full(256), row(128, 0), row(128, 0)],
        out_specs=[row(1536, 0), row(2048, 0), row(128, 0), full(256), full(256)],
        out_shape=[SDS((s, 1536), BF16), SDS((s, 2048), BF16), SDS((s, 128), BF16),
                   SDS((1, 256), F32), SDS((1, 256), F32)],
        compiler_params=_cparams(("arbitrary",)),
    )(dq_p, dk_p, dv, qa, kva, proj, gq, gk, cos, sin)


def mla_attn_fwd(q_p, k_p, kva, name):
    s = q_p.shape[0]
    tk = min(MLA_TK, s)
    tq = min(512, s)
    r = tq // tk
    nq = s // tq

    def body(q_ref, k_ref, va_ref, vb_ref, o_ref, lse_ref):
        row = lax.broadcasted_iota(jnp.int32, (tq, tk), 0)
        col = lax.broadcasted_iota(jnp.int32, (tq, tk), 1)
        v_refs = (va_ref, vb_ref)

        def tile(hh, q, kj, m, l, acc, mask):
            k0 = pl.multiple_of(kj * tk, tk)
            sc = _dot_nt(q, k_ref[pl.ds(k0, tk), hh * 256:(hh + 1) * 256])
            if mask is not None:
                sc = jnp.where(mask, sc, -1e30)
            m_new = jnp.maximum(m, jnp.max(sc, axis=1, keepdims=True))
            alpha = jnp.exp(m - m_new)
            p = jnp.exp(sc - m_new)
            l = alpha * l + jnp.sum(p, axis=1, keepdims=True)
            acc = alpha * acc + _dot(p.astype(BF16), v_refs[hh][pl.ds(k0, tk), :])
            return m_new, l, acc

        def q_body(qi, carry):
            q0 = pl.multiple_of(qi * tq, tq)
            qs = [q_ref[pl.ds(q0, tq), hh * 256:(hh + 1) * 256] for hh in range(2)]

            def step(kj, cr, mask=None):
                out = ()
                for hh in range(2):
                    out += tile(hh, qs[hh], kj, cr[3 * hh], cr[3 * hh + 1], cr[3 * hh + 2], mask)
                return out

            cr = (jnp.full((tq, 1), -1e30, F32), jnp.zeros((tq, 1), F32), jnp.zeros((tq, LANES), F32)) * 2
            cr = lax.fori_loop(0, qi * r, step, cr)
            for d in range(r):
                cr = step(qi * r + d, cr, col + d * tk <= row)
            for hh in range(2):
                m, l, acc = cr[3 * hh:3 * hh + 3]
                o_ref[pl.ds(q0, tq), hh * 128:(hh + 1) * 128] = acc / l
                lse_ref[pl.ds(q0, tq), hh * 128:(hh + 1) * 128] = jnp.broadcast_to(m + jnp.log(l), (tq, LANES))
            return carry

        lax.fori_loop(0, nq, q_body, 0)

    return pl.pallas_call(
        body, name=name, grid=(MLA_HEADS // 2,),
        in_specs=[pl.BlockSpec((s, 512), lambda g: (0, g)),
                  pl.BlockSpec((s, 512), lambda g: (0, g)),
                  pl.BlockSpec((s, 128), lambda g: (0, 4 * g + 1)),
                  pl.BlockSpec((s, 128), lambda g: (0, 4 * g + 3))],
        out_specs=[pl.BlockSpec((s, 256), lambda g: (0, g)),
                   pl.BlockSpec((s, 256), lambda g: (0, g))],
        out_shape=[SDS((s, 1024), F32), SDS((s, 1024), F32)],
        compiler_params=_cparams(("parallel",), vmem_mb=56),
    )(q_p, k_p, kva, kva)


def mla_attn_bwd(q_p, k_p, kva, o, do, lse, name):
    s = q_p.shape[0]
    tk = min(MLA_TK, s)
    tq = min(512, s)
    r = tq // tk
    nq = s // tq

    def body(q_ref, k_ref, v_ref, o_ref, do_ref, lse_ref, dq_ref, dk_ref, dv_ref, dv_s):
        dk_ref[...] = jnp.zeros_like(dk_ref)
        dv_s[...] = jnp.zeros_like(dv_s)
        row = lax.broadcasted_iota(jnp.int32, (tq, tk), 0)
        col = lax.broadcasted_iota(jnp.int32, (tq, tk), 1)

        def tile(q, dob, lse, delta, kj, dq, mask):
            k0 = pl.multiple_of(kj * tk, tk)
            k = k_ref[pl.ds(k0, tk), :]
            v = v_ref[pl.ds(k0, tk), :]
            p = jnp.exp(_dot_nt(q, k) - lse)
            if mask is not None:
                p = jnp.where(mask, p, 0.0)
            ds = (p * (_dot_nt(dob, v) - delta)).astype(BF16)
            dq = dq + _dot(ds, k)
            dk_ref[pl.ds(k0, tk), :] += _dot_tn(ds, q)
            dv_s[pl.ds(k0, tk), :] += _dot_tn(p.astype(BF16), dob)
            return dq

        def q_body(qi, carry):
            q0 = pl.multiple_of(qi * tq, tq)
            q = q_ref[pl.ds(q0, tq), :]
            dob = do_ref[pl.ds(q0, tq), :]
            lse = lse_ref[pl.ds(q0, tq), 0:1]
            delta = jnp.sum(dob.astype(F32) * o_ref[pl.ds(q0, tq), :], axis=1, keepdims=True)
            dq = lax.fori_loop(0, qi * r, lambda kj, dq: tile(q, dob, lse, delta, kj, dq, None),
                               jnp.zeros((tq, 256), F32))
            for d in range(r):
                dq = tile(q, dob, lse, delta, qi * r + d, dq, col + d * tk <= row)
            dq_ref[pl.ds(q0, tq), :] = dq
            return carry

        lax.fori_loop(0, nq, q_body, 0)
        dv_ref[...] = dv_s[...].astype(BF16)

    b256 = pl.BlockSpec((s, 256), lambda h: (0, h))
    b128 = pl.BlockSpec((s, 128), lambda h: (0, h))
    return pl.pallas_call(
        body, name=name, grid=(MLA_HEADS,),
        in_specs=[b256, b256, pl.BlockSpec((s, 128), lambda h: (0, 2 * h + 1)), b128, b128, b128],
        out_specs=[b256, b256, b128],
        out_shape=[SDS((s, 2048), F32), SDS((s, 2048), F32), SDS((s, 1024), BF16)],
        scratch_shapes=[pltpu.VMEM((s, 128), F32)],
        compiler_params=_cparams(("parallel",), vmem_mb=56),
    )(q_p, k_p, kva, o, do, lse)


SWA_HEADS = 16
SWA_BLOCK = 128


def _swa_head(tile_idx, half):
    return (2 * (tile_idx // 4) + half) * 4 + tile_idx % 4


SWA_Q_PERM = np.concatenate([np.arange(_swa_head(tt, a) * HALF, (_swa_head(tt, a) + 1) * HALF)
                             for tt in range(8) for a in range(2)])
SWA_SLOPES = [2.0 ** (-8.0 * (h + 1) / SWA_HEADS) for h in range(SWA_HEADS)]


def swa_prep_fwd(proj, g128, name):
    s = proj.shape[0]
    tm = min(512, s)

    def body(q_ref, k_ref, g_ref, o_ref):
        lane = lax.broadcasted_iota(jnp.int32, (1, LANES), 1)
        lo = lane < HALF
        for tt in range(10):
            if tt < 8:
                x = q_ref[:, tt * 128:(tt + 1) * 128].astype(F32)
            else:
                x = k_ref[:, (tt - 8) * 128:(tt - 7) * 128].astype(F32)
            rstd = lax.rsqrt(_half_sum(x * x) * (1.0 / HALF) + NORM_EPS)
            gi = 0 if tt < 8 else 1
            o_ref[:, tt * 128:(tt + 1) * 128] = (x * rstd * g_ref[gi:gi + 1, :]).astype(BF16)

    return pl.pallas_call(
        body, name=name, grid=(s // tm,),
        in_specs=[pl.BlockSpec((tm, 1024), lambda i: (i, 0)), pl.BlockSpec((tm, 256), lambda i: (i, 8)),
                  pl.BlockSpec((2, 128), lambda i: (0, 0))],
        out_specs=pl.BlockSpec((tm, 1280), lambda i: (i, 0)),
        out_shape=SDS((s, 1280), BF16),
        compiler_params=_cparams(("parallel",)),
    )(proj, proj, g128)


def swa_prep_bwd(dqn, dkn, proj, g128, name):
    s = proj.shape[0]
    tm = min(512, s)
    nsteps = s // tm

    def body(dq_ref, dk_ref, xq_ref, xk_ref, g_ref, o_ref, dg_ref):
        @pl.when(pl.program_id(0) == 0)
        def _():
            dg_ref[...] = jnp.zeros_like(dg_ref)

        lane = lax.broadcasted_iota(jnp.int32, (1, LANES), 1)
        lo = lane < HALF
        for tt in range(10):
            if tt < 8:
                x = xq_ref[:, tt * 128:(tt + 1) * 128].astype(F32)
                dy = dq_ref[:, tt * 128:(tt + 1) * 128].astype(F32)
            else:
                x = xk_ref[:, (tt - 8) * 128:(tt - 7) * 128].astype(F32)
                dy = dk_ref[:, (tt - 8) * 128:(tt - 7) * 128].astype(F32)
            gi = 0 if tt < 8 else 1
            rstd = lax.rsqrt(_half_sum(x * x) * (1.0 / HALF) + NORM_EPS)
            xh = x * rstd
            dg_ref[gi:gi + 1, :] += jnp.sum(dy * xh, axis=0, keepdims=True)
            dxh = dy * g_ref[gi:gi + 1, :]
            mu = _half_sum(dxh * xh) * (1.0 / HALF)
            o_ref[:, tt * 128:(tt + 1) * 128] = (rstd * (dxh - xh * mu)).astype(BF16)

        @pl.when(pl.program_id(0) == nsteps - 1)
        def _():
            acc = dg_ref[...]
            dg_ref[...] = acc + pltpu.roll(acc, HALF, 1)

    return pl.pallas_call(
        body, name=name, grid=(nsteps,),
        in_specs=[pl.BlockSpec((tm, 1024), lambda i: (i, 0)), pl.BlockSpec((tm, 256), lambda i: (i, 0)),
                  pl.BlockSpec((tm, 1024), lambda i: (i, 0)), pl.BlockSpec((tm, 256), lambda i: (i, 8)),
                  pl.BlockSpec((2, 128), lambda i: (0, 0))],
        out_specs=[pl.BlockSpec((tm, 1280), lambda i: (i, 0)), pl.BlockSpec((2, 128), lambda i: (0, 0))],
        out_shape=[SDS((s, 1280), BF16), SDS((2, 128), F32)],
        compiler_params=_cparams(("arbitrary",)),
    )(dqn, dkn, proj, proj, g128)


SWA_QSCALE = 1.0 / math.sqrt(HALF)


def _swa_bias():
    b = SWA_BLOCK
    row = jnp.arange(b)[:, None]
    col = jnp.arange(2 * b)[None, :]
    rel = row + b - col
    valid = (rel >= 0) & (rel < b)
    slopes = jnp.asarray([SWA_SLOPES[_swa_head(tt, a)] for tt in range(8) for a in range(2)], F32)
    bias = jnp.where(valid[None], -slopes[:, None, None] * rel[None].astype(F32), -1e30)
    return jnp.stack([jnp.where((col >= b)[None], bias, -1e30), bias])


def _swa_tile_inputs(tt, q_ref, bias_ref, sink_ref, lane):
    b = SWA_BLOCK
    qt = q_ref[:, tt * 128:(tt + 1) * 128] * SWA_QSCALE
    zq = jnp.zeros_like(qt)
    qs = jnp.concatenate([jnp.where(lane < HALF, qt, zq), jnp.where(lane >= HALF, qt, zq)], axis=0)
    bias = jnp.concatenate([bias_ref[0, 2 * tt], bias_ref[0, 2 * tt + 1]], axis=0)
    sink = jnp.concatenate([jnp.full((b, 1), sink_ref[_swa_head(tt, a)], F32) for a in range(2)], axis=0)
    return qs, bias, sink


def _swa_softmax(qs, kt, bias, sink):
    sc = _dot_nt(qs, kt) + bias
    m = jnp.maximum(jnp.max(sc, axis=1, keepdims=True), sink)
    e = jnp.exp(sc - m)
    es = jnp.exp(sink - m)
    inv = 1.0 / (jnp.sum(e, axis=1, keepdims=True) + es)
    return e * inv, es * inv


def _swa_specs(b):
    cur = lambda w, c: pl.BlockSpec((b, w), lambda n: (n, c))
    prev = lambda w, c: pl.BlockSpec((b, w), lambda n: (jnp.maximum(n - 1, 0), c))
    bias = pl.BlockSpec((1, SWA_HEADS, b, 2 * b), lambda n: (jnp.minimum(n, 1), 0, 0, 0))
    return cur, prev, bias


def swa_attn_fwd(qkn, proj, sinks, bias, name):
    s = qkn.shape[0]
    b = SWA_BLOCK

    def body(sink_ref, bias_ref, q_ref, kc_ref, kp_ref, vc_ref, vp_ref, o_ref):
        lane = lax.broadcasted_iota(jnp.int32, (1, LANES), 1)
        for tt in range(8):
            gp = tt // 4
            kt = jnp.concatenate([kp_ref[:, gp * 128:(gp + 1) * 128], kc_ref[:, gp * 128:(gp + 1) * 128]], axis=0)
            vt = jnp.concatenate([vp_ref[:, gp * 128:(gp + 1) * 128], vc_ref[:, gp * 128:(gp + 1) * 128]], axis=0)
            qt = q_ref[:, tt * 128:(tt + 1) * 128] * SWA_QSCALE
            outs = []
            for a in range(2):
                qm = jnp.where((lane < HALF) if a == 0 else (lane >= HALF), qt, jnp.zeros_like(qt))
                p, _ = _swa_softmax(qm, kt, bias_ref[0, 2 * tt + a], sink_ref[_swa_head(tt, a)])
                outs.append(_dot(p.astype(BF16), vt))
            o_ref[:, tt * 128:(tt + 1) * 128] = jnp.where(lane < HALF, outs[0], outs[1])

    cur, prev, bias_spec = _swa_specs(b)
    return pl.pallas_call(
        body, name=name, grid=(s // b,),
        in_specs=[pl.BlockSpec(memory_space=pltpu.SMEM), bias_spec, cur(1024, 0), cur(256, 4), prev(256, 4),
                  cur(256, 9), prev(256, 9)],
        out_specs=cur(1024, 0),
        out_shape=SDS((s, 1024), F32),
        compiler_params=_cparams(("parallel",)),
    )(sinks, bias, qkn, qkn, qkn, proj, proj)


def swa_attn_bwd(qkn, proj, sinks, bias, do, name):
    s = qkn.shape[0]
    b = SWA_BLOCK

    def body(sink_ref, bias_ref, q_ref, kc_ref, kp_ref, vc_ref, vp_ref, do_ref, dq_ref, dk_ref, dv_ref, ds_ref):
        n = pl.program_id(0)

        @pl.when(n == 0)
        def _():
            dk_ref[...] = jnp.zeros_like(dk_ref)
            dv_ref[...] = jnp.zeros_like(dv_ref)
            ds_ref[...] = jnp.zeros_like(ds_ref)

        lane = lax.broadcasted_iota(jnp.int32, (1, LANES), 1)
        cur0 = pl.multiple_of(n * b, b)
        prev0 = pl.multiple_of(jnp.maximum(n - 1, 0) * b, b)
        dsink = jnp.zeros((1, LANES), F32)
        for gp in range(2):
            kt = jnp.concatenate([kp_ref[:, gp * 128:(gp + 1) * 128], kc_ref[:, gp * 128:(gp + 1) * 128]], axis=0)
            vt = jnp.concatenate([vp_ref[:, gp * 128:(gp + 1) * 128], vc_ref[:, gp * 128:(gp + 1) * 128]], axis=0)
            dkt = jnp.zeros((2 * b, LANES), F32)
            dvt = jnp.zeros((2 * b, LANES), F32)
            for tt in range(4 * gp, 4 * gp + 4):
                qs, bias2, sink = _swa_tile_inputs(tt, q_ref, bias_ref, sink_ref, lane)
                dot_ = do_ref[:, tt * 128:(tt + 1) * 128]
                zd = jnp.zeros_like(dot_)
                dos = jnp.concatenate([jnp.where(lane < HALF, dot_, zd), jnp.where(lane >= HALF, dot_, zd)], axis=0)
                p, ps = _swa_softmax(qs, kt, bias2, sink)
                dp = _dot_nt(dos, vt)
                delta = jnp.sum(p * dp, axis=1, keepdims=True)
                dz = (p * (dp - delta)).astype(BF16)
                sd = ps * delta
                for a in range(2):
                    dsink = dsink + jnp.where(lane == _swa_head(tt, a), -jnp.sum(sd[a * b:(a + 1) * b]), 0.0)
                dq2 = _dot(dz, kt)
                dq_ref[:, tt * 128:(tt + 1) * 128] = (jnp.where(lane < HALF, dq2[:b], dq2[b:]) * SWA_QSCALE).astype(BF16)
                dkt = dkt + _dot_tn(dz, qs)
                dvt = dvt + _dot_tn(p.astype(BF16), dos)
            cols = slice(gp * 128, (gp + 1) * 128)
            dk_ref[pl.ds(prev0, b), cols] += dkt[:b]
            dk_ref[pl.ds(cur0, b), cols] += dkt[b:]
            dv_ref[pl.ds(prev0, b), cols] += dvt[:b]
            dv_ref[pl.ds(cur0, b), cols] += dvt[b:]
        ds_ref[0:1, :] += dsink

    cur, prev, bias_spec = _swa_specs(b)
    full = pl.BlockSpec((s, 256), lambda n: (0, 0))
    return pl.pallas_call(
        body, name=name, grid=(s // b,),
        in_specs=[pl.BlockSpec(memory_space=pltpu.SMEM), bias_spec, cur(1024, 0), cur(256, 4), prev(256, 4),
                  cur(256, 9), prev(256, 9), cur(1024, 0)],
        out_specs=[cur(1024, 0), full, full, pl.BlockSpec((8, LANES), lambda n: (0, 0))],
        out_shape=[SDS((s, 1024), BF16), SDS((s, 256), F32), SDS((s, 256), F32), SDS((8, LANES), F32)],
        compiler_params=_cparams(("arbitrary",)),
    )(sinks, bias, qkn, qkn, qkn, proj, proj, do)


def _my_pos():
    return lax.axis_index("x"), lax.axis_index("y"), lax.axis_index("c")


ANY = pl.BlockSpec(memory_space=pl.ANY)


def all_gather_packed(xs, name):
    r, n = xs.shape

    def body(x_ref, out_ref, send_sems, recv_sems, local_sem):
        x, y, c = _my_pos()
        me, sibling = (x, y, c), (x, y, 1 - c)
        south = c == 0
        via = (jnp.where(south, 1 - x, x), jnp.where(south, y, 1 - y))
        far = (jnp.where(south, x, 1 - x), jnp.where(south, 1 - y, y))
        rk = jnp.where(south, 1, 2)
        diag = (1 - x, 1 - y)

        def slot(px, py, pc):
            return out_ref.at[4 * px + 2 * py + pc]

        def copy(k, block, to, src=None):
            return pltpu.make_async_remote_copy(
                src_ref=slot(*block) if src is None else src, dst_ref=slot(*block),
                send_sem=send_sems.at[k], recv_sem=recv_sems.at[k],
                device_id=to, device_id_type=MESH_ID)

        mine = pltpu.make_async_copy(x_ref, slot(*me), local_sem)
        mine.start()
        sent = [copy(0, me, sibling, src=x_ref), copy(1, me, (1 - x, y, c), src=x_ref),
                copy(2, me, (x, 1 - y, c), src=x_ref)]
        for cp in sent:
            cp.start()
        copy(rk, (*via, c), me).wait_recv()
        sent += [copy(3, (*via, c), (*far, c)), copy(3 + rk, (*via, c), sibling)]
        sent[-2].start()
        sent[-1].start()
        copy(3 - rk, (*far, c), me).wait_recv()
        sent.append(copy(6 - rk, (*far, c), sibling))
        sent[-1].start()
        copy(3, (*diag, c), me).wait_recv()
        sent.append(copy(6, (*diag, c), sibling))
        sent[-1].start()
        copy(0, sibling, me).wait_recv()
        for k, chip in ((4, (1 - x, y)), (5, (x, 1 - y)), (6, diag)):
            copy(k, (*chip, 1 - c), me).wait_recv()
        for cp in sent:
            cp.wait_send()
        mine.wait()

    return pl.pallas_call(
        body, name=name,
        out_shape=SDS((N_DEV, r, n), xs.dtype),
        in_specs=[ANY], out_specs=ANY,
        scratch_shapes=[pltpu.SemaphoreType.DMA((7,)), pltpu.SemaphoreType.DMA((7,)), pltpu.SemaphoreType.DMA],
    )(xs)


def rs_sibling_exchange(g, name):
    _, r, n = g.shape

    def body(g_ref, ra_ref, send_sems, recv_sems):
        x, y, c = _my_pos()
        copies = [pltpu.make_async_remote_copy(
            src_ref=g_ref.at[2 * chip + (1 - c)], dst_ref=ra_ref.at[chip],
            send_sem=send_sems.at[chip], recv_sem=recv_sems.at[chip],
            device_id=(x, y, 1 - c), device_id_type=MESH_ID) for chip in range(4)]
        for cp in copies:
            cp.start()
        for cp in copies:
            cp.wait()

    return pl.pallas_call(
        body, name=name, out_shape=SDS((4, r, n), g.dtype), in_specs=[ANY], out_specs=ANY,
        scratch_shapes=[pltpu.SemaphoreType.DMA((4,)), pltpu.SemaphoreType.DMA((4,))],
    )(g)


def rs_pair_add(g, ra, cidx, name):
    _, r, n = g.shape
    tr = r // 2

    def body(c_ref, g_ref, ra_ref, p_ref):
        p_ref[...] = (g_ref[...].astype(F32) + ra_ref[...].astype(F32)).astype(p_ref.dtype)

    return pl.pallas_call(
        body, name=name,
        grid_spec=pltpu.PrefetchScalarGridSpec(
            num_scalar_prefetch=1, grid=(4, r // tr),
            in_specs=[pl.BlockSpec((1, tr, n), lambda i, j, c: (2 * i + c[0], j, 0)),
                      pl.BlockSpec((1, tr, n), lambda i, j, c: (i, j, 0))],
            out_specs=pl.BlockSpec((1, tr, n), lambda i, j, c: (i, j, 0))),
        out_shape=SDS((4, r, n), g.dtype),
        compiler_params=_cparams(("parallel", "parallel")),
    )(cidx, g, ra)


def rs_cross_exchange(p, name):
    _, r, n = p.shape

    def body(p_ref, rb_ref, send_sems, recv_sems):
        x, y, c = _my_pos()
        copies = []
        for k, (dx, dy) in enumerate(((1, 0), (0, 1), (1, 1))):
            tx = 1 - x if dx else x
            ty = 1 - y if dy else y
            copies.append(pltpu.make_async_remote_copy(
                src_ref=p_ref.at[2 * tx + ty], dst_ref=rb_ref.at[k],
                send_sem=send_sems.at[k], recv_sem=recv_sems.at[k],
                device_id=(tx, ty, c), device_id_type=MESH_ID))
        for cp in copies:
            cp.start()
        for cp in copies:
            cp.wait()

    return pl.pallas_call(
        body, name=name, out_shape=SDS((3, r, n), p.dtype), in_specs=[ANY], out_specs=ANY,
        scratch_shapes=[pltpu.SemaphoreType.DMA((3,)), pltpu.SemaphoreType.DMA((3,))],
    )(p)


def _adamw(w, g, m, v):
    m = ADAM_B1 * m + (1.0 - ADAM_B1) * g
    v = ADAM_B2 * v + (1.0 - ADAM_B2) * (g * g)
    m_hat = m / (1.0 - ADAM_B1 ** ADAM_STEP)
    v_hat = v / (1.0 - ADAM_B2 ** ADAM_STEP)
    delta = -ADAM_LR * (m_hat / (jnp.sqrt(v_hat) + ADAM_EPS) + ADAM_WD * w)
    return delta, m, v


def rs_final_adamw(p, rb, chipidx, w, m, v, row0, col_blk, name):
    r, n = w.shape
    tr = min(r, 256 if n <= PAIR_COLS else 128)
    assert r % tr == 0 and row0 % tr == 0
    rb0 = row0 // tr

    def body(c_ref, p_ref, rb_ref, w_ref, m_ref, v_ref, g_out, d_out, m_out, v_out):
        g = p_ref[0].astype(F32)
        for k in range(3):
            g = g + rb_ref[k].astype(F32)
        d, mn, vn = _adamw(w_ref[...], g, m_ref[...], v_ref[...])
        g_out[...] = g
        d_out[...] = d
        m_out[...] = mn
        v_out[...] = vn

    blk = pl.BlockSpec((tr, n), lambda j, c: (j, 0))
    return pl.pallas_call(
        body, name=name,
        grid_spec=pltpu.PrefetchScalarGridSpec(
            num_scalar_prefetch=1, grid=(r // tr,),
            in_specs=[pl.BlockSpec((1, tr, n), lambda j, c: (c[0], rb0 + j, col_blk)),
                      pl.BlockSpec((3, tr, n), lambda j, c: (0, rb0 + j, col_blk)), blk, blk, blk],
            out_specs=[blk, blk, blk, blk]),
        out_shape=[SDS((r, n), F32)] * 4,
        compiler_params=_cparams(("parallel",)),
    )(chipidx, p, rb, w, m, v)


def small_allreduce_adamw(g, w, m, v, name):
    rows = g.shape[0]

    def body(g_ref, w_ref, m_ref, v_ref, g_out, d_out, m_out, v_out, buf, send_sems, recv_sems):
        x, y, c = _my_pos()
        my = 4 * x + 2 * y + c
        buf[my] = g_ref[...]
        copies = []
        for k in range(1, N_DEV):
            dx, dy, dc = (k >> 2) & 1, (k >> 1) & 1, k & 1
            tgt = (1 - x if dx else x, 1 - y if dy else y, 1 - c if dc else c)
            copies.append(pltpu.make_async_remote_copy(
                src_ref=buf.at[my], dst_ref=buf.at[my],
                send_sem=send_sems.at[k - 1], recv_sem=recv_sems.at[k - 1],
                device_id=tgt, device_id_type=MESH_ID))
        for cp in copies:
            cp.start()
        for cp in copies:
            cp.wait()
        tot = buf[0]
        for j in range(1, N_DEV):
            tot = tot + buf[j]
        d, mn, vn = _adamw(w_ref[...], tot, m_ref[...], v_ref[...])
        g_out[...] = tot
        d_out[...] = d
        m_out[...] = mn
        v_out[...] = vn

    vm = pl.BlockSpec(memory_space=pltpu.VMEM)
    return pl.pallas_call(
        body, name=name, out_shape=[SDS((rows, LANES), F32)] * 4,
        in_specs=[vm] * 4, out_specs=[vm] * 4,
        scratch_shapes=[pltpu.VMEM((N_DEV, rows, LANES), F32),
                        pltpu.SemaphoreType.DMA((N_DEV - 1,)), pltpu.SemaphoreType.DMA((N_DEV - 1,))],
    )(g, w, m, v)


BIG = ["l0_w_in", "l0_w_out", "l1_w_in", "l1_w_uq", "l1_w_ukv", "l1_w_out", "l2_w_in", "l2_w_out",
       "l3_w_in", "l3_w_out"]
SMALL = ["l0_norm", "l1_norm", "l1_q_a_norm", "l1_kv_a_norm", "l1_q_head_norm", "l1_k_head_norm",
         "l2_norm", "l2_q_head_norm", "l2_k_head_norm", "l2_sinks", "l3_norm"]

MLA_UQ_PERM = np.concatenate([np.arange(h * 192, h * 192 + 128) for h in range(8)]
                             + [np.arange(h * 192 + 128, (h + 1) * 192) for h in range(8)])
SWA_IN_PERM = np.concatenate([SWA_Q_PERM, 1536 + SWA_Q_PERM, np.arange(1024, 1536)])


def _inv(perm):
    inv = np.empty_like(perm)
    inv[perm] = np.arange(perm.size)
    return inv


OUTS = ["l0_w_out", "l1_w_out", "l2_w_out", "l3_w_out"]
REST = ["l1_w_in", "l1_w_uq", "l1_w_ukv", "l2_w_in"]


def _pack_rest(shards):
    parts = [shards[nm].reshape(-1, D_MODEL) for nm in REST]
    used = sum(p.shape[0] for p in parts)
    assert REST_ROW0 + used == ROWS_USED
    parts.append(jnp.zeros((REST_ROWS - used, D_MODEL), parts[0].dtype))
    return jnp.concatenate(parts, axis=0)


def _unpack_rest(packed, shapes):
    out, off = {}, 0
    for nm in REST:
        shp = shapes[nm]
        rows = shp[0] * shp[1] // D_MODEL
        out[nm] = packed[off:off + rows].reshape(shp)
        off += rows
    return out


def _pack(shards):
    pair = jnp.concatenate([shards["l0_w_in"], shards["l3_w_in"]], axis=1)
    return jnp.concatenate([pair] + [shards[nm] for nm in OUTS] + [_pack_rest(shards)], axis=0)


def _out_weight(gathered, layer):
    r0 = OUT_ROW0 + 128 * layer
    return gathered[:, r0:r0 + 128].reshape(D_MODEL, D_MODEL)


def _rest_full_weights(gathered, shapes):
    out, off = {}, REST_ROW0
    for nm in REST:
        shp = shapes[nm]
        rows = shp[0] * shp[1] // D_MODEL
        blk = gathered[:, off:off + rows].reshape((N_DEV,) + shp)
        out[nm] = jnp.transpose(blk, (1, 0, 2)).reshape(shp[0], N_DEV * shp[1])
        off += rows
    return out


def _grad_pieces(pair_l0, pair_l3, dw_out, rest_grads, shapes):
    parts = [jnp.concatenate([pair_l0, pair_l3], axis=2)]
    parts += [g.reshape(N_DEV, 128, D_MODEL).astype(BF16) for g in dw_out]
    used = REST_ROW0
    for nm in REST:
        shp = shapes[nm]
        blk = jnp.transpose(rest_grads[nm].reshape(shp[0], N_DEV, shp[1]), (1, 0, 2))
        parts.append(blk.reshape(N_DEV, -1, D_MODEL).astype(BF16))
        used += parts[-1].shape[1]
    parts.append(jnp.zeros((N_DEV, ROWS_PACK - used, D_MODEL), BF16))
    return jnp.concatenate(parts, axis=1)


SMALL_USED = 5008


def _pack_small(vals, extra=None):
    parts = [vals[nm].reshape(-1).astype(F32) for nm in SMALL]
    assert sum(p.shape[0] for p in parts) == SMALL_USED
    if extra is not None:
        parts.append(extra.reshape(1).astype(F32))
    flat = jnp.concatenate(parts)
    flat = jnp.concatenate([flat, jnp.zeros((SMALL_ROWS * LANES - flat.shape[0],), F32)])
    return flat.reshape(SMALL_ROWS, LANES)


def _unpack_small(packed, shapes):
    flat = packed.reshape(-1)
    out, off = {}, 0
    for nm in SMALL:
        n = shapes[nm][0]
        out[nm] = flat[off:off + n]
        off += n
    return out


def _row(v):
    return v.reshape(1, -1).astype(F32)


def _mla_gain_rows(g):
    return jnp.concatenate([g[:128], g[128:], g[128:]]).reshape(1, 256).astype(F32)


def _layer_fwd_sb(x, norm, gathered, shard_col, w_out, tag):
    proj, xn = norm_matmul(x, 0, _row(norm), gathered, f"{tag}_in", shard_col=shard_col)
    o = sb_attn_fwd(proj, f"{tag}_attn")
    y = gate_out_fwd(o, proj, 3, w_out, x, f"{tag}_out")
    return y, (x, xn, proj, o)


def _layer_bwd_sb(dy, saved, norm, gathered, shard_col, w_out, tag):
    x, xn, proj, o = saved
    do, dgate, hg = gate_out_bwd(dy, w_out, o, proj, 3, f"{tag}_outb")
    dw_out = tn_matmul(hg, dy, f"{tag}_dwout")
    dq, dk, dv = sb_attn_bwd(proj, o, do, f"{tag}_attnb")
    dproj = [dq, dk, dv, dgate]
    dw_in = tn_matmul_shards(xn, dproj, f"{tag}_dwin")
    dx, dnorm = nt_norm_bwd(dproj, gathered, x, 0, _row(norm), dy, f"{tag}_inb", F32, shard_col=shard_col)
    return dx, dw_in, dw_out, dnorm.reshape(-1)


def kernel(x, l0_norm, l0_w_in, l0_w_out, l1_norm, l1_w_in, l1_q_a_norm, l1_w_uq, l1_kv_a_norm, l1_w_ukv, l1_q_head_norm, l1_k_head_norm, l1_w_out, l2_norm, l2_w_in, l2_q_head_norm, l2_k_head_norm, l2_sinks, l2_w_out, l3_norm, l3_w_in, l3_w_out, loss_target, m_l0_norm, m_l0_w_in, m_l0_w_out, m_l1_norm, m_l1_w_in, m_l1_q_a_norm, m_l1_w_uq, m_l1_kv_a_norm, m_l1_w_ukv, m_l1_q_head_norm, m_l1_k_head_norm, m_l1_w_out, m_l2_norm, m_l2_w_in, m_l2_q_head_norm, m_l2_k_head_norm, m_l2_sinks, m_l2_w_out, m_l3_norm, m_l3_w_in, m_l3_w_out, v_l0_norm, v_l0_w_in, v_l0_w_out, v_l1_norm, v_l1_w_in, v_l1_q_a_norm, v_l1_w_uq, v_l1_kv_a_norm, v_l1_w_ukv, v_l1_q_head_norm, v_l1_k_head_norm, v_l1_w_out, v_l2_norm, v_l2_w_in, v_l2_q_head_norm, v_l2_k_head_norm, v_l2_sinks, v_l2_w_out, v_l3_norm, v_l3_w_in, v_l3_w_out):
    loc = dict(locals())
    names = BIG + SMALL
    w = {nm: loc[nm] for nm in names}
    mom = {nm: loc["m_" + nm] for nm in names}
    vel = {nm: loc["v_" + nm] for nm in names}
    shapes = {nm: w[nm].shape for nm in names}
    xs = x[0]
    tgt = loss_target[0]

    gathered = all_gather_packed(_pack({nm: w[nm].astype(BF16) for nm in BIG}), "ag_weights")
    loss_local, dx0, pieces, sg = _local_step(xs, tgt, gathered, {nm: w[nm] for nm in SMALL}, shapes)
    return _reduce_and_update(loss_local, dx0, pieces, sg, w, mom, vel, shapes)


def _local_step(xs, tgt, gathered, sm, shapes):
    s = xs.shape[0]
    fw = _rest_full_weights(gathered, shapes)
    w_out = [_out_weight(gathered, layer) for layer in range(4)]
    l0_norm, l1_norm, l2_norm, l3_norm = sm["l0_norm"], sm["l1_norm"], sm["l2_norm"], sm["l3_norm"]
    l1_q_a_norm, l1_kv_a_norm = sm["l1_q_a_norm"], sm["l1_kv_a_norm"]
    l1_q_head_norm, l1_k_head_norm = sm["l1_q_head_norm"], sm["l1_k_head_norm"]
    l2_q_head_norm, l2_k_head_norm, l2_sinks = sm["l2_q_head_norm"], sm["l2_k_head_norm"], sm["l2_sinks"]
    w1_in = jnp.concatenate([fw["l1_w_in"][:, 448:], fw["l1_w_in"][:, :448], jnp.zeros((D_MODEL, 64), BF16)], axis=1)
    w1_uq = fw["l1_w_uq"][:, MLA_UQ_PERM]
    w2_in = fw["l2_w_in"][:, SWA_IN_PERM]
    w2_out = w_out[2][SWA_Q_PERM, :]

    x1, sv0 = _layer_fwd_sb(xs, l0_norm, gathered, 0, w_out[0], "l0")

    cos, sin = _rope_tables(s)
    gq, gk = _mla_gain_rows(l1_q_head_norm), _mla_gain_rows(l1_k_head_norm)
    proj1, xn1 = norm_matmul(x1, 0, _row(l1_norm), w1_in, "l1_in")
    qa, qln = norm_matmul(proj1, 4, _row(l1_q_a_norm), w1_uq, "l1_uq")
    kva, kvn = norm_matmul(proj1, 10, _row(l1_kv_a_norm), fw["l1_w_ukv"], "l1_ukv")
    q_p, k_p = mla_prep_fwd(qa, kva, proj1, gq, gk, cos, sin, "l1_prep")
    o1, lse1 = mla_attn_fwd(q_p, k_p, kva, "l1_attn")
    x2 = gate_out_fwd(o1, proj1, 0, w_out[1], x1, "l1_out")

    g2 = jnp.stack([jnp.tile(l2_q_head_norm, 2), jnp.tile(l2_k_head_norm, 2)]).astype(F32)
    proj2, xn2 = norm_matmul(x2, 0, _row(l2_norm), w2_in, "l2_in")
    qkn2 = swa_prep_fwd(proj2, g2, "l2_prep")
    bias2 = _swa_bias()
    o2 = swa_attn_fwd(qkn2, proj2, l2_sinks, bias2, "l2_attn")
    x3 = gate_out_fwd(o2, proj2, 1, w2_out, x2, "l2_out")

    x4, sv3 = _layer_fwd_sb(x3, l3_norm, gathered, 1, w_out[3], "l3")

    lossblk, dy = loss_kernel(x4, tgt, "loss")

    fg, sg = {}, {}
    dw_out = [None] * 4
    dx3, pair_l3, dw_out[3], sg["l3_norm"] = _layer_bwd_sb(dy, sv3, l3_norm, gathered, 1, w_out[3], "l3")

    do2, dgate2, hg2 = gate_out_bwd(dx3, w2_out, o2, proj2, 1, "l2_outb")
    dw2_out = tn_matmul(hg2, dx3, "l2_dwout")
    dqn2, dkn2, dv2, dsink2 = swa_attn_bwd(qkn2, proj2, l2_sinks, bias2, do2, "l2_attnb")
    dqk2, dg2 = swa_prep_bwd(dqn2, dkn2, proj2, g2, "l2_prepb")
    dproj2 = jnp.concatenate([dqk2[:, :1024], dgate2, dqk2[:, 1024:], dv2.astype(BF16)], axis=1)
    dw2_in = tn_matmul(xn2, dproj2, "l2_dwin")
    dx2, dn2 = nt_norm_bwd(dproj2, w2_in, x2, 0, _row(l2_norm), dx3, "l2_inb", F32)
    fg["l2_w_in"] = dw2_in[:, _inv(SWA_IN_PERM)]
    dw_out[2] = dw2_out[_inv(SWA_Q_PERM), :]
    sg["l2_norm"] = dn2.reshape(-1)
    sg["l2_q_head_norm"] = dg2[0, :HALF]
    sg["l2_k_head_norm"] = dg2[1, :HALF]
    sg["l2_sinks"] = dsink2[0, :SWA_HEADS]

    do1, dgate1, hg1 = gate_out_bwd(dx2, w_out[1], o1, proj1, 0, "l1_outb")
    dw_out[1] = tn_matmul(hg1, dx2, "l1_dwout")
    dq_p, dk_p, dv1 = mla_attn_bwd(q_p, k_p, kva, o1, do1, lse1, "l1_attnb")
    dqa, dkva, dkpe, dgq, dgk = mla_prep_bwd(dq_p, dk_p, dv1, qa, kva, proj1, gq, gk, cos, sin, "l1_prepb")
    dw_uq = tn_matmul(qln, dqa, "l1_dwuq")
    fg["l1_w_ukv"] = tn_matmul(kvn, dkva, "l1_dwukv")
    dqlat, dgqa = nt_norm_bwd(dqa, w1_uq, proj1, 4, _row(l1_q_a_norm), None, "l1_uqb", BF16)
    dkvlat, dgkva = nt_norm_bwd(dkva, fw["l1_w_ukv"], proj1, 10, _row(l1_kv_a_norm), None, "l1_ukvb", BF16)
    dproj1 = jnp.concatenate([dgate1, dqlat, dkvlat, dkpe], axis=1)
    dw1_in = tn_matmul(xn1, dproj1, "l1_dwin")
    dx1, dn1 = nt_norm_bwd(dproj1, w1_in, x1, 0, _row(l1_norm), dx2, "l1_inb", F32)
    fg["l1_w_in"] = jnp.concatenate([dw1_in[:, 1024:1472], dw1_in[:, :1024]], axis=1)
    fg["l1_w_uq"] = dw_uq[:, _inv(MLA_UQ_PERM)]
    sg["l1_norm"] = dn1.reshape(-1)
    sg["l1_q_a_norm"] = dgqa.reshape(-1)
    sg["l1_kv_a_norm"] = dgkva.reshape(-1)
    sg["l1_q_head_norm"] = jnp.concatenate([dgq[0, :128], dgq[0, 128:192] + dgq[0, 192:256]])
    sg["l1_k_head_norm"] = jnp.concatenate([dgk[0, :128], dgk[0, 128:192] + dgk[0, 192:256]])

    dx0, pair_l0, dw_out[0], sg["l0_norm"] = _layer_bwd_sb(dx1, sv0, l0_norm, gathered, 0, w_out[0], "l0")
    return lossblk[0, 0], dx0, _grad_pieces(pair_l0, pair_l3, dw_out, fg, shapes), sg


def _reduce_and_update(loss_local, dx0, pieces, sg, w, mom, vel, shapes):
    xi, yi, ci = _my_pos()
    cidx = jnp.reshape(ci, (1,)).astype(jnp.int32)
    chipidx = jnp.reshape(2 * xi + yi, (1,)).astype(jnp.int32)
    ra = rs_sibling_exchange(pieces, "rs_sibling")
    part = rs_pair_add(pieces, ra, cidx, "rs_pair_add")
    rb = rs_cross_exchange(part, "rs_cross")

    def update(nm, row0, col_blk):
        return rs_final_adamw(part, rb, chipidx, w[nm], mom[nm], vel[nm], row0, col_blk, f"adamw_{nm}")

    big = {"l0_w_in": update("l0_w_in", 0, 0), "l3_w_in": update("l3_w_in", 0, 1)}
    for layer, nm in enumerate(OUTS):
        big[nm] = update(nm, OUT_ROW0 + 128 * layer, 0)
    rest = rs_final_adamw(part, rb, chipidx, _pack_rest(w), _pack_rest(mom), _pack_rest(vel), REST_ROW0, 0,
                          "adamw_rest")
    gs, ds, ms, vs = small_allreduce_adamw(_pack_small(sg, loss_local), _pack_small({nm: w[nm] for nm in SMALL}),
                                           _pack_small({nm: mom[nm] for nm in SMALL}),
                                           _pack_small({nm: vel[nm] for nm in SMALL}), "small_allreduce")

    outs = []
    for i, small in enumerate((gs, ds, ms, vs)):
        d = {nm: big[nm][i] for nm in big}
        d.update(_unpack_rest(rest[i], shapes))
        d.update(_unpack_small(small, shapes))
        outs.append(d)
    order = ["l0_norm", "l0_w_in", "l0_w_out", "l1_norm", "l1_w_in", "l1_q_a_norm", "l1_w_uq", "l1_kv_a_norm",
             "l1_w_ukv", "l1_q_head_norm", "l1_k_head_norm", "l1_w_out", "l2_norm", "l2_w_in", "l2_q_head_norm",
             "l2_k_head_norm", "l2_sinks", "l2_w_out", "l3_norm", "l3_w_in", "l3_w_out"]
    flat = [gs.reshape(-1)[SMALL_USED], dx0[None]]
    for d in outs:
        flat += [d[nm] for nm in order]
    return tuple(flat)
```

```python
import math

import numpy as np
import jax
import jax.numpy as jnp
from jax import lax
from jax.experimental import pallas as pl
from jax.experimental.pallas import tpu as pltpu

F32 = jnp.float32
BF16 = jnp.bfloat16
SDS = jax.ShapeDtypeStruct
MESH_ID = pl.DeviceIdType.MESH

D_MODEL = 1024
NORM_EPS = 1e-6
N_DEV = 8
LANES = 128
HALF = 64
ROWS_PACK = 2176
ROWS_USED = 2120
PAIR_COLS = 512
OUT_ROW0 = 1024
REST_ROW0 = 1536
REST_ROWS = ROWS_PACK - REST_ROW0
SMALL_ROWS = 40

ADAM_LR = 0.001
ADAM_B1 = 0.9
ADAM_B2 = 0.999
ADAM_EPS = 1e-08
ADAM_WD = 0.01
ADAM_STEP = 10

NT_DIMS = (((1,), (1,)), ((), ()))
TN_DIMS = (((0,), (0,)), ((), ()))


def _cparams(sem=None, vmem_mb=48):
    return pltpu.CompilerParams(dimension_semantics=sem, vmem_limit_bytes=vmem_mb * 2 ** 20)


def _dot(a, b):
    return jnp.dot(a, b, preferred_element_type=F32)


def _dot_nt(a, b):
    return lax.dot_general(a, b, NT_DIMS, preferred_element_type=F32)


def _dot_tn(a, b):
    return lax.dot_general(a, b, TN_DIMS, preferred_element_type=F32)


def _split_dot(a, b):
    hi = a.astype(BF16)
    lo = (a - hi.astype(F32)).astype(BF16)
    return _dot(hi, b) + _dot(lo, b)


def _cumsum_dot(lf, u):
    return _dot(lf.astype(BF16), u)


def _pick_tile(n, cap):
    return max(t for t in range(LANES, min(n, cap) + 1, LANES) if n % t == 0)


def _sigmoid(x):
    return 1.0 / (1.0 + jnp.exp(-x))


def norm_matmul(x, colblk, g, w, name, shard_col=None):
    s = x.shape[0]
    if shard_col is None:
        k, n = w.shape
        tn = _pick_tile(n, 1024)
        w_spec = pl.BlockSpec((k, tn), lambda i, j: (0, j))
    else:
        k, n, tn = D_MODEL, N_DEV * PAIR_COLS, 2 * PAIR_COLS
        w_spec = pl.BlockSpec((2, k, PAIR_COLS), lambda i, j: (j, 0, shard_col))
    tm = min(1024, s)
    assert s % tm == 0

    def body(x_ref, g_ref, w_ref, o_ref, xn_ref, xn_s):
        @pl.when(pl.program_id(1) == 0)
        def _():
            xf = x_ref[...].astype(F32)
            ms = jnp.mean(xf * xf, axis=-1, keepdims=True)
            y = (xf * lax.rsqrt(ms + NORM_EPS) * g_ref[...]).astype(BF16)
            xn_s[...] = y
            xn_ref[...] = y

        if shard_col is None:
            o_ref[...] = _dot(xn_s[...], w_ref[...]).astype(o_ref.dtype)
        else:
            for half in range(2):
                o_ref[:, half * PAIR_COLS:(half + 1) * PAIR_COLS] = _dot(xn_s[...], w_ref[half]).astype(o_ref.dtype)

    return pl.pallas_call(
        body, name=name, grid=(s // tm, n // tn),
        in_specs=[pl.BlockSpec((tm, k), lambda i, j: (i, colblk)),
                  pl.BlockSpec((1, k), lambda i, j: (0, 0)),
                  w_spec],
        out_specs=[pl.BlockSpec((tm, tn), lambda i, j: (i, j)),
                   pl.BlockSpec((tm, k), lambda i, j: (i, 0))],
        out_shape=[SDS((s, n), BF16), SDS((s, k), BF16)],
        scratch_shapes=[pltpu.VMEM((tm, k), BF16)],
        compiler_params=_cparams(("parallel", "arbitrary")),
    )(x, g, w)


def gate_out_fwd(o, proj, gate_blk, w_out, x, name):
    s = x.shape[0]
    d = D_MODEL
    tm = min(512, s)

    def body(o_ref, g_ref, w_ref, x_ref, y_ref):
        g = g_ref[...].astype(F32)
        hg = (o_ref[...].astype(F32) * (g * _sigmoid(g))).astype(BF16)
        y_ref[...] = x_ref[...] + _dot(hg, w_ref[...])

    return pl.pallas_call(
        body, name=name, grid=(s // tm,),
        in_specs=[pl.BlockSpec((tm, d), lambda i: (i, 0)),
                  pl.BlockSpec((tm, d), lambda i: (i, gate_blk)),
                  pl.BlockSpec((d, d), lambda i: (0, 0)),
                  pl.BlockSpec((tm, d), lambda i: (i, 0))],
        out_specs=pl.BlockSpec((tm, d), lambda i: (i, 0)),
        out_shape=SDS((s, d), F32),
        compiler_params=_cparams(("parallel",)),
    )(o, proj, w_out, x)


def gate_out_bwd(dy, w_out, o, proj, gate_blk, name):
    s = dy.shape[0]
    d = D_MODEL
    tm = min(512, s)

    def body(dy_ref, w_ref, o_ref, g_ref, do_ref, dg_ref, hg_ref):
        dh = _dot_nt(dy_ref[...].astype(BF16), w_ref[...])
        g = g_ref[...].astype(F32)
        o = o_ref[...].astype(F32)
        sg = _sigmoid(g)
        silu = g * sg
        do_ref[...] = (dh * silu).astype(BF16)
        dg_ref[...] = (dh * o * (sg * (1.0 + g * (1.0 - sg)))).astype(BF16)
        hg_ref[...] = (o * silu).astype(BF16)

    blk = pl.BlockSpec((tm, d), lambda i: (i, 0))
    return pl.pallas_call(
        body, name=name, grid=(s // tm,),
        in_specs=[blk, pl.BlockSpec((d, d), lambda i: (0, 0)), blk,
                  pl.BlockSpec((tm, d), lambda i: (i, gate_blk))],
        out_specs=[blk, blk, blk],
        out_shape=[SDS((s, d), BF16)] * 3,
        compiler_params=_cparams(("parallel",)),
    )(dy, w_out, o, proj)


def tn_matmul(a, b, name):
    s, k = a.shape
    n = b.shape[1]
    tk = min(1024, k)
    tn = _pick_tile(n, 512)
    ts = min(2048, s)
    assert k % tk == 0 and s % ts == 0

    def body(a_ref, b_ref, o_ref):
        @pl.when(pl.program_id(2) == 0)
        def _():
            o_ref[...] = jnp.zeros_like(o_ref)

        o_ref[...] += _dot_tn(a_ref[...].astype(BF16), b_ref[...].astype(BF16))

    return pl.pallas_call(
        body, name=name, grid=(k // tk, n // tn, s // ts),
        in_specs=[pl.BlockSpec((ts, tk), lambda i, j, r: (r, i)),
                  pl.BlockSpec((ts, tn), lambda i, j, r: (r, j))],
        out_specs=pl.BlockSpec((tk, tn), lambda i, j, r: (i, j)),
        out_shape=SDS((k, n), F32),
        compiler_params=_cparams(("parallel", "parallel", "arbitrary")),
    )(a, b)


def tn_matmul_shards(a, b_parts, name):
    s, k = a.shape
    per = D_MODEL // PAIR_COLS
    assert len(b_parts) * per == N_DEV and a.dtype == BF16
    assert all(b.shape == (s, D_MODEL) and b.dtype == BF16 for b in b_parts)

    def body(a_ref, *refs):
        o_ref = refs[-1]
        j = pl.program_id(0)
        for p, b_ref in enumerate(refs[:-1]):
            @pl.when(j // per == p)
            def _(b_ref=b_ref):
                o_ref[...] = _dot_tn(a_ref[...], b_ref[...]).astype(BF16)

    part_specs = [pl.BlockSpec((s, PAIR_COLS), lambda j, p=p: (0, jnp.clip(j - per * p, 0, per - 1)))
                  for p in range(len(b_parts))]
    return pl.pallas_call(
        body, name=name, grid=(N_DEV,),
        in_specs=[pl.BlockSpec((s, k), lambda j: (0, 0))] + part_specs,
        out_specs=pl.BlockSpec((None, k, PAIR_COLS), lambda j: (j, 0, 0)),
        out_shape=SDS((N_DEV, k, PAIR_COLS), BF16),
        compiler_params=_cparams(("arbitrary",)),
    )(a, *b_parts)


def nt_norm_bwd(dp, w, x, xcol, g, res, name, out_dtype, shard_col=None):
    tm = min(512, dp[0].shape[0] if shard_col is not None else dp.shape[0])
    if shard_col is None:
        s, n = dp.shape
        k = w.shape[0]
        w_spec = pl.BlockSpec((k, n), lambda i: (0, 0))
        dp_parts = [dp]
        dp_specs = [pl.BlockSpec((tm, n), lambda i: (i, 0))]
    else:
        s, k = dp[0].shape[0], D_MODEL
        w_spec = pl.BlockSpec((N_DEV, k, PAIR_COLS), lambda i: (0, 0, shard_col))
        dp_parts = list(dp)
        dp_specs = [pl.BlockSpec((tm, D_MODEL), lambda i: (i, 0)) for _ in dp_parts]
    np_ = len(dp_parts)
    per = D_MODEL // PAIR_COLS
    assert s % tm == 0
    has_res = res is not None

    def body(*refs):
        dp_refs = refs[:np_]
        if has_res:
            w_ref, x_ref, g_ref, r_ref, dx_ref, dg_ref = refs[np_:]
        else:
            w_ref, x_ref, g_ref, dx_ref, dg_ref = refs[np_:]

        @pl.when(pl.program_id(0) == 0)
        def _():
            dg_ref[...] = jnp.zeros_like(dg_ref)

        if shard_col is None:
            dxn = _dot_nt(dp_refs[0][...], w_ref[...])
        else:
            dxn = None
            for j in range(N_DEV):
                c0 = (j % per) * PAIR_COLS
                term = _dot_nt(dp_refs[j // per][:, c0:c0 + PAIR_COLS], w_ref[j])
                dxn = term if dxn is None else dxn + term
        xf = x_ref[...].astype(F32)
        rstd = lax.rsqrt(jnp.mean(xf * xf, axis=-1, keepdims=True) + NORM_EPS)
        xhat = xf * rstd
        dg_ref[...] += jnp.sum(dxn * xhat, axis=0, keepdims=True)
        dxh = dxn * g_ref[...]
        dx = rstd * (dxh - xhat * jnp.mean(dxh * xhat, axis=-1, keepdims=True))
        if has_res:
            dx = dx + r_ref[...]
        dx_ref[...] = dx.astype(out_dtype)

    in_specs = dp_specs + [w_spec,
                           pl.BlockSpec((tm, k), lambda i: (i, xcol)),
                           pl.BlockSpec((1, k), lambda i: (0, 0))]
    args = dp_parts + [w, x, g]
    if has_res:
        in_specs.append(pl.BlockSpec((tm, k), lambda i: (i, 0)))
        args.append(res)
    return pl.pallas_call(
        body, name=name, grid=(s // tm,),
        in_specs=in_specs,
        out_specs=[pl.BlockSpec((tm, k), lambda i: (i, 0)),
                   pl.BlockSpec((1, k), lambda i: (0, 0))],
        out_shape=[SDS((s, k), out_dtype), SDS((1, k), F32)],
        compiler_params=_cparams(("arbitrary",)),
    )(*args)


def loss_kernel(y, target, name):
    s, d = y.shape
    tm = min(512, s)

    def body(y_ref, t_ref, l_ref, dy_ref):
        @pl.when(pl.program_id(0) == 0)
        def _():
            l_ref[...] = jnp.zeros_like(l_ref)

        e = y_ref[...] - t_ref[...]
        dy_ref[...] = e * (1.0 / d)
        l_ref[...] += 0.5 * jnp.sum(jnp.mean(e * e, axis=-1, keepdims=True))

    blk = pl.BlockSpec((tm, d), lambda i: (i, 0))
    return pl.pallas_call(
        body, name=name, grid=(s // tm,),
        in_specs=[blk, blk],
        out_specs=[pl.BlockSpec((8, LANES), lambda i: (0, 0)), blk],
        out_shape=[SDS((8, LANES), F32), SDS((s, d), F32)],
        compiler_params=_cparams(("arbitrary",)),
    )(y, target)


SB_LOG_CUTOFF = -80.0
SB_T = 256
SB_TK_LEFT = 128
SB_FWD_TILES = 4
SB_BWD_TILES = 2


def _sb_sweep(n_left, cs, rest, step):
    n = len(cs)

    def top(arrs):
        m = jnp.max(arrs[0])
        for a in arrs[1:]:
            m = jnp.maximum(m, jnp.max(a))
        return m

    def cond(st):
        return (st[0] <= n_left) & (st[1] > SB_LOG_CUTOFF)

    def body(st):
        cs2, rest2 = step(st[0], st[2:2 + n], st[2 + n:])
        return (st[0] + 1, top(cs2)) + tuple(cs2) + tuple(rest2)

    return lax.while_loop(cond, body, (jnp.int32(1), top(cs)) + tuple(cs) + tuple(rest))[2 + n:]


def _tile_masks(t):
    row = lax.broadcasted_iota(jnp.int32, (t, t), 0)
    col = lax.broadcasted_iota(jnp.int32, (t, t), 1)
    return row, col


def sb_attn_fwd(proj, name):
    s = proj.shape[0]
    n_tiles = SB_FWD_TILES
    t = min(SB_T, s)
    tko = min(SB_TK_LEFT, t)
    nq = s // t
    scale = 1.0 / math.sqrt(HALF)

    def body(q_ref, k_ref, v_ref, o_ref):
        lane = lax.broadcasted_iota(jnp.int32, (1, LANES), 1)
        row, col = _tile_masks(t)
        u_incl = (row >= col).astype(BF16)
        u_off = u_incl[:tko, :tko]
        tri = jnp.concatenate([col < row] * 2, axis=0)

        def scores(pr, qs, k0, tk):
            k = k_ref[pl.ds(k0, tk), pr * LANES:(pr + 1) * LANES]
            z = _dot_nt(qs, k)
            nz = -z
            lf = jnp.minimum(nz, 0.0) - jnp.log(1.0 + jnp.exp(jnp.minimum(z, nz)))
            return z, lf

        def accumulate(pr, k0, tk, z, lf, c, acc, diag):
            v = v_ref[pl.ds(k0, tk), pr * LANES:(pr + 1) * LANES]
            if diag:
                lf = jnp.where(tri, lf, 0.0)
            incl = _cumsum_dot(lf, u_incl if diag else u_off) + c
            a = jnp.exp(z + incl)
            if diag:
                a = jnp.where(tri, a, 0.0)
            acc = acc + _dot(a.astype(BF16), v)
            c = c + jnp.sum(lf, axis=1, keepdims=True)
            return c, acc

        def q_body(qi, carry):
            q0 = pl.multiple_of(qi * t, t)
            qss = []
            for pr in range(n_tiles):
                q = q_ref[pl.ds(q0, t), pr * LANES:(pr + 1) * LANES] * scale
                zq = jnp.zeros_like(q)
                qss.append(jnp.concatenate([jnp.where(lane < HALF, q, zq), jnp.where(lane >= HALF, q, zq)], axis=0))

            def step(jj, cs, accs, diag=False):
                k0, tk = (q0, t) if diag else (pl.multiple_of(q0 - jj * tko, tko), tko)
                outs = [accumulate(pr, k0, tk, *scores(pr, qss[pr], k0, tk), cs[pr], accs[pr], diag)
                        for pr in range(n_tiles)]
                return tuple(o[0] for o in outs), tuple(o[1] for o in outs)

            zero = (jnp.zeros((2 * t, 1), F32),) * n_tiles, (jnp.zeros((2 * t, LANES), F32),) * n_tiles
            cs, accs = step(0, zero[0], zero[1], True)
            accs = _sb_sweep(qi * (t // tko), cs, accs, step)
            for pr in range(n_tiles):
                o_ref[pl.ds(q0, t), pr * LANES:(pr + 1) * LANES] = jnp.where(lane < HALF, accs[pr][:t], accs[pr][t:])
            return carry

        lax.fori_loop(0, nq, q_body, 0)

    w = n_tiles * LANES
    return pl.pallas_call(
        body, name=name, grid=(D_MODEL // w,),
        in_specs=[pl.BlockSpec((s, w), lambda p: (0, p)),
                  pl.BlockSpec((s, w), lambda p: (0, D_MODEL // w + p)),
                  pl.BlockSpec((s, w), lambda p: (0, 2 * (D_MODEL // w) + p))],
        out_specs=pl.BlockSpec((s, w), lambda p: (0, p)),
        out_shape=SDS((s, D_MODEL), F32),
        compiler_params=_cparams(("parallel",), vmem_mb=56),
    )(proj, proj, proj)


def sb_attn_bwd(proj, o, do, name):
    s = proj.shape[0]
    n_tiles = SB_BWD_TILES
    t = min(SB_T, s)
    tko = min(SB_TK_LEFT, t)
    nq = s // t
    scale = 1.0 / math.sqrt(HALF)

    def body(q_ref, k_ref, v_ref, o_ref, do_ref, dq_ref, dk_ref, dv_ref, dk_s, dv_s):
        dk_s[...] = jnp.zeros_like(dk_s)
        dv_s[...] = jnp.zeros_like(dv_s)
        lane = lax.broadcasted_iota(jnp.int32, (1, LANES), 1)
        row, col = _tile_masks(t)
        u_incl = (row >= col).astype(BF16)
        u_excl = (row > col).astype(BF16)
        tri = jnp.concatenate([col < row] * 2, axis=0)

        def scores(pr, qs, dos, k0, tk):
            cols = slice(pr * LANES, (pr + 1) * LANES)
            z = _dot_nt(qs, k_ref[pl.ds(k0, tk), cols])
            nz = -z
            lf = jnp.minimum(nz, 0.0) - jnp.log(1.0 + jnp.exp(jnp.minimum(z, nz)))
            return z, lf, _dot_nt(dos, v_ref[pl.ds(k0, tk), cols])

        def accumulate(pr, qs, dos, tsum, k0, tk, z, lf, da, c, r, dq, diag):
            cols = slice(pr * LANES, (pr + 1) * LANES)
            ui, ue = (u_incl, u_excl) if diag else (u_incl[:tko, :tko], u_excl[:tko, :tko])
            sig = jnp.exp(z + lf)
            if diag:
                lf = jnp.where(tri, lf, 0.0)
            incl = _cumsum_dot(lf, ui) + c
            a = jnp.exp(z + incl)
            if diag:
                a = jnp.where(tri, a, 0.0)
            ab = a.astype(BF16)
            dl = ab.astype(F32) * da
            pre = tsum - r - _split_dot(dl, ue)
            dz = dl - sig * pre
            if diag:
                dz = jnp.where(tri, dz, 0.0)
            dzb = dz.astype(BF16)
            dq = dq + _dot(dzb, k_ref[pl.ds(k0, tk), cols])
            dk_s[pl.ds(k0, tk), cols] += _dot_tn(dzb, qs)
            dv_s[pl.ds(k0, tk), cols] += _dot_tn(ab, dos)
            c = c + jnp.sum(lf, axis=1, keepdims=True)
            r = r + jnp.sum(dl, axis=1, keepdims=True)
            return c, r, dq

        def q_body(qi, carry):
            q0 = pl.multiple_of(qi * t, t)
            lo, hi = lane < HALF, lane >= HALF
            qss, doss, tsums = [], [], []
            for pr in range(n_tiles):
                cols = slice(pr * LANES, (pr + 1) * LANES)
                q = q_ref[pl.ds(q0, t), cols] * scale
                dob = do_ref[pl.ds(q0, t), cols]
                prod = dob.astype(F32) * o_ref[pl.ds(q0, t), cols]
                zb = jnp.zeros_like(q)
                qss.append(jnp.concatenate([jnp.where(lo, q, zb), jnp.where(hi, q, zb)], axis=0))
                doss.append(jnp.concatenate([jnp.where(lo, dob, zb), jnp.where(hi, dob, zb)], axis=0))
                tsums.append(jnp.concatenate([jnp.sum(jnp.where(lo, prod, 0.0), axis=1, keepdims=True),
                                              jnp.sum(jnp.where(hi, prod, 0.0), axis=1, keepdims=True)], axis=0))

            def step(jj, cs, rest, diag=False):
                k0, tk = (q0, t) if diag else (pl.multiple_of(q0 - jj * tko, tko), tko)
                outs = [accumulate(pr, qss[pr], doss[pr], tsums[pr], k0, tk,
                                   *scores(pr, qss[pr], doss[pr], k0, tk),
                                   cs[pr], rest[pr], rest[n_tiles + pr], diag) for pr in range(n_tiles)]
                return tuple(o[0] for o in outs), tuple(o[1] for o in outs) + tuple(o[2] for o in outs)

            zc = (jnp.zeros((2 * t, 1), F32),) * n_tiles
            cs, rest = step(0, zc, zc + (jnp.zeros((2 * t, LANES), F32),) * n_tiles, True)
            rest = _sb_sweep(qi * (t // tko), cs, rest, step)
            for pr in range(n_tiles):
                dq = rest[n_tiles + pr]
                dq_ref[pl.ds(q0, t), pr * LANES:(pr + 1) * LANES] = (jnp.where(lo, dq[:t], dq[t:]) * scale).astype(BF16)
            return carry

        lax.fori_loop(0, nq, q_body, 0)
        dk_ref[...] = dk_s[...].astype(BF16)
        dv_ref[...] = dv_s[...].astype(BF16)

    w = n_tiles * LANES
    nblk = D_MODEL // w
    blk = lambda off: pl.BlockSpec((s, w), lambda p: (0, off * nblk + p))
    return pl.pallas_call(
        body, name=name, grid=(nblk,),
        in_specs=[blk(0), blk(1), blk(2), blk(0), blk(0)],
        out_specs=[blk(0), blk(0), blk(0)],
        out_shape=[SDS((s, D_MODEL), BF16)] * 3,
        scratch_shapes=[pltpu.VMEM((s, w), F32), pltpu.VMEM((s, w), F32)],
        compiler_params=_cparams(("parallel",), vmem_mb=56),
    )(proj, proj, proj, o, do)


MLA_HEADS = 8
MLA_QK = 192
MLA_SCALE = 1.0 / math.sqrt(MLA_QK)
MLA_TK = 512


def _rope_tables(s):
    inv_freq = 10000.0 ** (-jnp.arange(32, dtype=F32) / 32)
    ang = jnp.arange(s, dtype=F32)[:, None] * inv_freq[None, :]
    cos = jnp.tile(jnp.cos(ang), (1, 4))
    sin = jnp.sin(ang)
    sin_signed = jnp.tile(jnp.concatenate([-sin, sin], axis=1), (1, 2))
    return cos, sin_signed


def _partner_matrix():
    j = lax.broadcasted_iota(jnp.int32, (LANES, LANES), 0)
    l = lax.broadcasted_iota(jnp.int32, (LANES, LANES), 1)
    return (j == l + jnp.where((l % HALF) < 32, 32, -32)).astype(BF16)


def _partner(u, pmat):
    return _split_dot(u, pmat)


def _half_sum(x):
    j = lax.broadcasted_iota(jnp.int32, (LANES, LANES), 0)
    l = lax.broadcasted_iota(jnp.int32, (LANES, LANES), 1)
    return _dot(x.astype(BF16), ((j < HALF) == (l < HALF)).astype(BF16))


def _lane_sum(x):
    return _dot(x.astype(BF16), jnp.ones((LANES, LANES), BF16))


def _mla_head_inputs(h, qa_ref, kva_ref, kpe, lane):
    a = h % 2
    hm = (lane < HALF) if a == 0 else (lane >= HALF)
    qn = qa_ref[:, h * 128:(h + 1) * 128].astype(F32)
    qr = jnp.where(hm, qa_ref[:, 1024 + (h // 2) * 128:1024 + (h // 2 + 1) * 128].astype(F32), 0.0)
    kn = kva_ref[:, h * 256:h * 256 + 128].astype(F32)
    kr = kpe if a == 0 else pltpu.roll(kpe, HALF, 1)
    return qn, qr, kn, kr


def mla_prep_fwd(qa, kva, proj, gq, gk, cos, sin, name):
    s = qa.shape[0]
    tm = min(256, s)

    def body(qa_ref, kva_ref, kpe_ref, gq_ref, gk_ref, cos_ref, sin_ref, q_out, k_out):
        lane = lax.broadcasted_iota(jnp.int32, (1, LANES), 1)
        pmat = _partner_matrix()
        kpe = kpe_ref[...].astype(F32)
        cs, sn = cos_ref[...], sin_ref[...]
        for h in range(MLA_HEADS):
            qn, qr, kn, kr = _mla_head_inputs(h, qa_ref, kva_ref, kpe, lane)
            for xn, xr, g_ref, out, sc in ((qn, qr, gq_ref, q_out, MLA_SCALE), (kn, kr, gk_ref, k_out, 1.0)):
                ss = _lane_sum(xn * xn + xr * xr)
                rstd = lax.rsqrt(ss * (1.0 / MLA_QK) + NORM_EPS) * sc
                yn = xn * rstd * g_ref[:, 0:128]
                ur = xr * rstd * g_ref[:, 128:256]
                yr = ur * cs + _partner(ur, pmat) * sn
                out[:, h * 256:h * 256 + 128] = yn.astype(BF16)
                out[:, h * 256 + 128:h * 256 + 256] = yr.astype(BF16)

    row = lambda w, c: pl.BlockSpec((tm, w), lambda i: (i, c))
    full = lambda w: pl.BlockSpec((1, w), lambda i: (0, 0))
    return pl.pallas_call(
        body, name=name, grid=(s // tm,),
        in_specs=[row(1536, 0), row(2048, 0), row(128, 11), full(256), full(256), row(128, 0), row(128, 0)],
        out_specs=[row(2048, 0), row(2048, 0)],
        out_shape=[SDS((s, 2048), BF16)] * 2,
        compiler_params=_cparams(("parallel",)),
    )(qa, kva, proj, gq, gk, cos, sin)


def mla_prep_bwd(dq_p, dk_p, dv, qa, kva, proj, gq, gk, cos, sin, name):
    s = qa.shape[0]
    tm = min(256, s)

    def body(dq_ref, dk_ref, dv_ref, qa_ref, kva_ref, kpe_ref, gq_ref, gk_ref, cos_ref, sin_ref,
             dqa_ref, dkva_ref, dkpe_ref, dgq_ref, dgk_ref):
        @pl.when(pl.program_id(0) == 0)
        def _():
            dgq_ref[...] = jnp.zeros_like(dgq_ref)
            dgk_ref[...] = jnp.zeros_like(dgk_ref)

        lane = lax.broadcasted_iota(jnp.int32, (1, LANES), 1)
        pmat = _partner_matrix()
        kpe = kpe_ref[...].astype(F32)
        cs, sn = cos_ref[...], sin_ref[...]
        dkpe = jnp.zeros((tm, LANES), F32)
        dqr_pair = None
        for h in range(MLA_HEADS):
            qn, qr, kn, kr = _mla_head_inputs(h, qa_ref, kva_ref, kpe, lane)
            res = []
            for xn, xr, g_ref, d_ref, dg_ref, sc in ((qn, qr, gq_ref, dq_ref, dgq_ref, MLA_SCALE),
                                                     (kn, kr, gk_ref, dk_ref, dgk_ref, 1.0)):
                ss = _lane_sum(xn * xn + xr * xr)
                rstd = lax.rsqrt(ss * (1.0 / MLA_QK) + NORM_EPS)
                hn, hr = xn * rstd, xr * rstd
                dyn = d_ref[:, h * 256:h * 256 + 128] * sc
                dyr = d_ref[:, h * 256 + 128:h * 256 + 256] * sc
                dur = dyr * cs - _partner(dyr, pmat) * sn
                dg_ref[:, 0:128] += jnp.sum(dyn * hn, axis=0, keepdims=True)
                dg_ref[:, 128:256] += jnp.sum(dur * hr, axis=0, keepdims=True)
                dhn = dyn * g_ref[:, 0:128]
                dhr = dur * g_ref[:, 128:256]
                mu = _lane_sum(dhn * hn + dhr * hr) * (1.0 / MLA_QK)
                res.append((rstd * (dhn - hn * mu), rstd * (dhr - hr * mu)))
            (dqn, dqr), (dkn, dkr) = res
            dqa_ref[:, h * 128:(h + 1) * 128] = dqn.astype(BF16)
            if h % 2 == 0:
                dqr_pair = dqr
            else:
                dqa_ref[:, 1024 + (h // 2) * 128:1024 + (h // 2 + 1) * 128] = (dqr_pair + dqr).astype(BF16)
            dkva_ref[:, h * 256:h * 256 + 128] = dkn.astype(BF16)
            dkva_ref[:, h * 256 + 128:h * 256 + 256] = dv_ref[:, h * 128:(h + 1) * 128]
            dkpe = dkpe + (dkr if h % 2 == 0 else pltpu.roll(dkr, HALF, 1))
        dkpe_ref[...] = dkpe.astype(BF16)

    row = lambda w, c: pl.BlockSpec((tm, w), lambda i: (i, c))
    full = lambda w: pl.BlockSpec((1, w), lambda i: (0, 0))
    return pl.pallas_call(
        body, name=name, grid=(s // tm,),
        in_specs=[row(2048, 0), row(2048, 0), row(1024, 0), row(1536, 0), row(2048, 0), row(128, 11),
                  full(256), full(256), row(128, 0), row(128, 0)],
        out_specs=[row(1536, 0), row(2048, 0), row(128, 0), full(256), full(256)],
        out_shape=[SDS((s, 1536), BF16), SDS((s, 2048), BF16), SDS((s, 128), BF16),
                   SDS((1, 256), F32), SDS((1, 256), F32)],
        compiler_params=_cparams(("arbitrary",)),
    )(dq_p, dk_p, dv, qa, kva, proj, gq, gk, cos, sin)


def mla_attn_fwd(q_p, k_p, kva, name):
    s = q_p.shape[0]
    tk = min(MLA_TK, s)
    tq = min(512, s)
    r = tq // tk
    nq = s // tq

    def body(q_ref, k_ref, va_ref, vb_ref, o_ref, lse_ref):
        row = lax.broadcasted_iota(jnp.int32, (tq, tk), 0)
        col = lax.broadcasted_iota(jnp.int32, (tq, tk), 1)
        v_refs = (va_ref, vb_ref)

        def tile(hh, q, kj, m, l, acc, mask):
            k0 = pl.multiple_of(kj * tk, tk)
            sc = _dot_nt(q, k_ref[pl.ds(k0, tk), hh * 256:(hh + 1) * 256])
            if mask is not None:
                sc = jnp.where(mask, sc, -1e30)
            m_new = jnp.maximum(m, jnp.max(sc, axis=1, keepdims=True))
            alpha = jnp.exp(m - m_new)
            p = jnp.exp(sc - m_new)
            l = alpha * l + jnp.sum(p, axis=1, keepdims=True)
            acc = alpha * acc + _dot(p.astype(BF16), v_refs[hh][pl.ds(k0, tk), :])
            return m_new, l, acc

        def q_body(qi, carry):
            q0 = pl.multiple_of(qi * tq, tq)
            qs = [q_ref[pl.ds(q0, tq), hh * 256:(hh + 1) * 256] for hh in range(2)]

            def step(kj, cr, mask=None):
                out = ()
                for hh in range(2):
                    out += tile(hh, qs[hh], kj, cr[3 * hh], cr[3 * hh + 1], cr[3 * hh + 2], mask)
                return out

            cr = (jnp.full((tq, 1), -1e30, F32), jnp.zeros((tq, 1), F32), jnp.zeros((tq, LANES), F32)) * 2
            cr = lax.fori_loop(0, qi * r, step, cr)
            for d in range(r):
                cr = step(qi * r + d, cr, col + d * tk <= row)
            for hh in range(2):
                m, l, acc = cr[3 * hh:3 * hh + 3]
                o_ref[pl.ds(q0, tq), hh * 128:(hh + 1) * 128] = acc / l
                lse_ref[pl.ds(q0, tq), hh * 128:(hh + 1) * 128] = jnp.broadcast_to(m + jnp.log(l), (tq, LANES))
            return carry

        lax.fori_loop(0, nq, q_body, 0)

    return pl.pallas_call(
        body, name=name, grid=(MLA_HEADS // 2,),
        in_specs=[pl.BlockSpec((s, 512), lambda g: (0, g)),
                  pl.BlockSpec((s, 512), lambda g: (0, g)),
                  pl.BlockSpec((s, 128), lambda g: (0, 4 * g + 1)),
                  pl.BlockSpec((s, 128), lambda g: (0, 4 * g + 3))],
        out_specs=[pl.BlockSpec((s, 256), lambda g: (0, g)),
                   pl.BlockSpec((s, 256), lambda g: (0, g))],
        out_shape=[SDS((s, 1024), F32), SDS((s, 1024), F32)],
        compiler_params=_cparams(("parallel",), vmem_mb=56),
    )(q_p, k_p, kva, kva)


def mla_attn_bwd(q_p, k_p, kva, o, do, lse, name):
    s = q_p.shape[0]
    tk = min(MLA_TK, s)
    tq = min(512, s)
    r = tq // tk
    nq = s // tq

    def body(q_ref, k_ref, v_ref, o_ref, do_ref, lse_ref, dq_ref, dk_ref, dv_ref, dv_s):
        dk_ref[...] = jnp.zeros_like(dk_ref)
        dv_s[...] = jnp.zeros_like(dv_s)
        row = lax.broadcasted_iota(jnp.int32, (tq, tk), 0)
        col = lax.broadcasted_iota(jnp.int32, (tq, tk), 1)

        def tile(q, dob, lse, delta, kj, dq, mask):
            k0 = pl.multiple_of(kj * tk, tk)
            k = k_ref[pl.ds(k0, tk), :]
            v = v_ref[pl.ds(k0, tk), :]
            p = jnp.exp(_dot_nt(q, k) - lse)
            if mask is not None:
                p = jnp.where(mask, p, 0.0)
            ds = (p * (_dot_nt(dob, v) - delta)).astype(BF16)
            dq = dq + _dot(ds, k)
            dk_ref[pl.ds(k0, tk), :] += _dot_tn(ds, q)
            dv_s[pl.ds(k0, tk), :] += _dot_tn(p.astype(BF16), dob)
            return dq

        def q_body(qi, carry):
            q0 = pl.multiple_of(qi * tq, tq)
            q = q_ref[pl.ds(q0, tq), :]
            dob = do_ref[pl.ds(q0, tq), :]
            lse = lse_ref[pl.ds(q0, tq), 0:1]
            delta = jnp.sum(dob.astype(F32) * o_ref[pl.ds(q0, tq), :], axis=1, keepdims=True)
            dq = lax.fori_loop(0, qi * r, lambda kj, dq: tile(q, dob, lse, delta, kj, dq, None),
                               jnp.zeros((tq, 256), F32))
            for d in range(r):
                dq = tile(q, dob, lse, delta, qi * r + d, dq, col + d * tk <= row)
            dq_ref[pl.ds(q0, tq), :] = dq
            return carry

        lax.fori_loop(0, nq, q_body, 0)
        dv_ref[...] = dv_s[...].astype(BF16)

    b256 = pl.BlockSpec((s, 256), lambda h: (0, h))
    b128 = pl.BlockSpec((s, 128), lambda h: (0, h))
    return pl.pallas_call(
        body, name=name, grid=(MLA_HEADS,),
        in_specs=[b256, b256, pl.BlockSpec((s, 128), lambda h: (0, 2 * h + 1)), b128, b128, b128],
        out_specs=[b256, b256, b128],
        out_shape=[SDS((s, 2048), F32), SDS((s, 2048), F32), SDS((s, 1024), BF16)],
        scratch_shapes=[pltpu.VMEM((s, 128), F32)],
        compiler_params=_cparams(("parallel",), vmem_mb=56),
    )(q_p, k_p, kva, o, do, lse)


SWA_HEADS = 16
SWA_BLOCK = 128


def _swa_head(tile_idx, half):
    return (2 * (tile_idx // 4) + half) * 4 + tile_idx % 4


SWA_Q_PERM = np.concatenate([np.arange(_swa_head(tt, a) * HALF, (_swa_head(tt, a) + 1) * HALF)
                             for tt in range(8) for a in range(2)])
SWA_SLOPES = [2.0 ** (-8.0 * (h + 1) / SWA_HEADS) for h in range(SWA_HEADS)]


def swa_prep_fwd(proj, g128, name):
    s = proj.shape[0]
    tm = min(512, s)

    def body(q_ref, k_ref, g_ref, o_ref):
        lane = lax.broadcasted_iota(jnp.int32, (1, LANES), 1)
        lo = lane < HALF
        for tt in range(10):
            if tt < 8:
                x = q_ref[:, tt * 128:(tt + 1) * 128].astype(F32)
            else:
                x = k_ref[:, (tt - 8) * 128:(tt - 7) * 128].astype(F32)
            rstd = lax.rsqrt(_half_sum(x * x) * (1.0 / HALF) + NORM_EPS)
            gi = 0 if tt < 8 else 1
            o_ref[:, tt * 128:(tt + 1) * 128] = (x * rstd * g_ref[gi:gi + 1, :]).astype(BF16)

    return pl.pallas_call(
        body, name=name, grid=(s // tm,),
        in_specs=[pl.BlockSpec((tm, 1024), lambda i: (i, 0)), pl.BlockSpec((tm, 256), lambda i: (i, 8)),
                  pl.BlockSpec((2, 128), lambda i: (0, 0))],
        out_specs=pl.BlockSpec((tm, 1280), lambda i: (i, 0)),
        out_shape=SDS((s, 1280), BF16),
        compiler_params=_cparams(("parallel",)),
    )(proj, proj, g128)


def swa_prep_bwd(dqn, dkn, proj, g128, name):
    s = proj.shape[0]
    tm = min(512, s)
    nsteps = s // tm

    def body(dq_ref, dk_ref, xq_ref, xk_ref, g_ref, o_ref, dg_ref):
        @pl.when(pl.program_id(0) == 0)
        def _():
            dg_ref[...] = jnp.zeros_like(dg_ref)

        lane = lax.broadcasted_iota(jnp.int32, (1, LANES), 1)
        lo = lane < HALF
        for tt in range(10):
            if tt < 8:
                x = xq_ref[:, tt * 128:(tt + 1) * 128].astype(F32)
                dy = dq_ref[:, tt * 128:(tt + 1) * 128].astype(F32)
            else:
                x = xk_ref[:, (tt - 8) * 128:(tt - 7) * 128].astype(F32)
                dy = dk_ref[:, (tt - 8) * 128:(tt - 7) * 128].astype(F32)
            gi = 0 if tt < 8 else 1
            rstd = lax.rsqrt(_half_sum(x * x) * (1.0 / HALF) + NORM_EPS)
            xh = x * rstd
            dg_ref[gi:gi + 1, :] += jnp.sum(dy * xh, axis=0, keepdims=True)
            dxh = dy * g_ref[gi:gi + 1, :]
            mu = _half_sum(dxh * xh) * (1.0 / HALF)
            o_ref[:, tt * 128:(tt + 1) * 128] = (rstd * (dxh - xh * mu)).astype(BF16)

        @pl.when(pl.program_id(0) == nsteps - 1)
        def _():
            acc = dg_ref[...]
            dg_ref[...] = acc + pltpu.roll(acc, HALF, 1)

    return pl.pallas_call(
        body, name=name, grid=(nsteps,),
        in_specs=[pl.BlockSpec((tm, 1024), lambda i: (i, 0)), pl.BlockSpec((tm, 256), lambda i: (i, 0)),
                  pl.BlockSpec((tm, 1024), lambda i: (i, 0)), pl.BlockSpec((tm, 256), lambda i: (i, 8)),
                  pl.BlockSpec((2, 128), lambda i: (0, 0))],
        out_specs=[pl.BlockSpec((tm, 1280), lambda i: (i, 0)), pl.BlockSpec((2, 128), lambda i: (0, 0))],
        out_shape=[SDS((s, 1280), BF16), SDS((2, 128), F32)],
        compiler_params=_cparams(("arbitrary",)),
    )(dqn, dkn, proj, proj, g128)


SWA_QSCALE = 1.0 / math.sqrt(HALF)


def _swa_bias():
    b = SWA_BLOCK
    row = jnp.arange(b)[:, None]
    col = jnp.arange(2 * b)[None, :]
    rel = row + b - col
    valid = (rel >= 0) & (rel < b)
    slopes = jnp.asarray([SWA_SLOPES[_swa_head(tt, a)] for tt in range(8) for a in range(2)], F32)
    bias = jnp.where(valid[None], -slopes[:, None, None] * rel[None].astype(F32), -1e30)
    return jnp.stack([jnp.where((col >= b)[None], bias, -1e30), bias])


def _swa_tile_inputs(tt, q_ref, bias_ref, sink_ref, lane):
    b = SWA_BLOCK
    qt = q_ref[:, tt * 128:(tt + 1) * 128] * SWA_QSCALE
    zq = jnp.zeros_like(qt)
    qs = jnp.concatenate([jnp.where(lane < HALF, qt, zq), jnp.where(lane >= HALF, qt, zq)], axis=0)
    bias = jnp.concatenate([bias_ref[0, 2 * tt], bias_ref[0, 2 * tt + 1]], axis=0)
    sink = jnp.concatenate([jnp.full((b, 1), sink_ref[_swa_head(tt, a)], F32) for a in range(2)], axis=0)
    return qs, bias, sink


def _swa_softmax(qs, kt, bias, sink):
    sc = _dot_nt(qs, kt) + bias
    m = jnp.maximum(jnp.max(sc, axis=1, keepdims=True), sink)
    e = jnp.exp(sc - m)
    es = jnp.exp(sink - m)
    inv = 1.0 / (jnp.sum(e, axis=1, keepdims=True) + es)
    return e * inv, es * inv


def _swa_specs(b):
    cur = lambda w, c: pl.BlockSpec((b, w), lambda n: (n, c))
    prev = lambda w, c: pl.BlockSpec((b, w), lambda n: (jnp.maximum(n - 1, 0), c))
    bias = pl.BlockSpec((1, SWA_HEADS, b, 2 * b), lambda n: (jnp.minimum(n, 1), 0, 0, 0))
    return cur, prev, bias


def swa_attn_fwd(qkn, proj, sinks, bias, name):
    s = qkn.shape[0]
    b = SWA_BLOCK

    def body(sink_ref, bias_ref, q_ref, kc_ref, kp_ref, vc_ref, vp_ref, o_ref):
        lane = lax.broadcasted_iota(jnp.int32, (1, LANES), 1)
        for tt in range(8):
            gp = tt // 4
            kt = jnp.concatenate([kp_ref[:, gp * 128:(gp + 1) * 128], kc_ref[:, gp * 128:(gp + 1) * 128]], axis=0)
            vt = jnp.concatenate([vp_ref[:, gp * 128:(gp + 1) * 128], vc_ref[:, gp * 128:(gp + 1) * 128]], axis=0)
            qt = q_ref[:, tt * 128:(tt + 1) * 128] * SWA_QSCALE
            outs = []
            for a in range(2):
                qm = jnp.where((lane < HALF) if a == 0 else (lane >= HALF), qt, jnp.zeros_like(qt))
                p, _ = _swa_softmax(qm, kt, bias_ref[0, 2 * tt + a], sink_ref[_swa_head(tt, a)])
                outs.append(_dot(p.astype(BF16), vt))
            o_ref[:, tt * 128:(tt + 1) * 128] = jnp.where(lane < HALF, outs[0], outs[1])

    cur, prev, bias_spec = _swa_specs(b)
    return pl.pallas_call(
        body, name=name, grid=(s // b,),
        in_specs=[pl.BlockSpec(memory_space=pltpu.SMEM), bias_spec, cur(1024, 0), cur(256, 4), prev(256, 4),
                  cur(256, 9), prev(256, 9)],
        out_specs=cur(1024, 0),
        out_shape=SDS((s, 1024), F32),
        compiler_params=_cparams(("parallel",)),
    )(sinks, bias, qkn, qkn, qkn, proj, proj)


def swa_attn_bwd(qkn, proj, sinks, bias, do, name):
    s = qkn.shape[0]
    b = SWA_BLOCK

    def body(sink_ref, bias_ref, q_ref, kc_ref, kp_ref, vc_ref, vp_ref, do_ref, dq_ref, dk_ref, dv_ref, ds_ref):
        n = pl.program_id(0)

        @pl.when(n == 0)
        def _():
            dk_ref[...] = jnp.zeros_like(dk_ref)
            dv_ref[...] = jnp.zeros_like(dv_ref)
            ds_ref[...] = jnp.zeros_like(ds_ref)

        lane = lax.broadcasted_iota(jnp.int32, (1, LANES), 1)
        cur0 = pl.multiple_of(n * b, b)
        prev0 = pl.multiple_of(jnp.maximum(n - 1, 0) * b, b)
        dsink = jnp.zeros((1, LANES), F32)
        for gp in range(2):
            kt = jnp.concatenate([kp_ref[:, gp * 128:(gp + 1) * 128], kc_ref[:, gp * 128:(gp + 1) * 128]], axis=0)
            vt = jnp.concatenate([vp_ref[:, gp * 128:(gp + 1) * 128], vc_ref[:, gp * 128:(gp + 1) * 128]], axis=0)
            dkt = jnp.zeros((2 * b, LANES), F32)
            dvt = jnp.zeros((2 * b, LANES), F32)
            for tt in range(4 * gp, 4 * gp + 4):
                qs, bias2, sink = _swa_tile_inputs(tt, q_ref, bias_ref, sink_ref, lane)
                dot_ = do_ref[:, tt * 128:(tt + 1) * 128]
                zd = jnp.zeros_like(dot_)
                dos = jnp.concatenate([jnp.where(lane < HALF, dot_, zd), jnp.where(lane >= HALF, dot_, zd)], axis=0)
                p, ps = _swa_softmax(qs, kt, bias2, sink)
                dp = _dot_nt(dos, vt)
                delta = jnp.sum(p * dp, axis=1, keepdims=True)
                dz = (p * (dp - delta)).astype(BF16)
                sd = ps * delta
                for a in range(2):
                    dsink = dsink + jnp.where(lane == _swa_head(tt, a), -jnp.sum(sd[a * b:(a + 1) * b]), 0.0)
                dq2 = _dot(dz, kt)
                dq_ref[:, tt * 128:(tt + 1) * 128] = (jnp.where(lane < HALF, dq2[:b], dq2[b:]) * SWA_QSCALE).astype(BF16)
                dkt = dkt + _dot_tn(dz, qs)
                dvt = dvt + _dot_tn(p.astype(BF16), dos)
            cols = slice(gp * 128, (gp + 1) * 128)
            dk_ref[pl.ds(prev0, b), cols] += dkt[:b]
            dk_ref[pl.ds(cur0, b), cols] += dkt[b:]
            dv_ref[pl.ds(prev0, b), cols] += dvt[:b]
            dv_ref[pl.ds(cur0, b), cols] += dvt[b:]
        ds_ref[0:1, :] += dsink

    cur, prev, bias_spec = _swa_specs(b)
    full = pl.BlockSpec((s, 256), lambda n: (0, 0))
    return pl.pallas_call(
        body, name=name, grid=(s // b,),
        in_specs=[pl.BlockSpec(memory_space=pltpu.SMEM), bias_spec, cur(1024, 0), cur(256, 4), prev(256, 4),
                  cur(256, 9), prev(256, 9), cur(1024, 0)],
        out_specs=[cur(1024, 0), full, full, pl.BlockSpec((8, LANES), lambda n: (0, 0))],
        out_shape=[SDS((s, 1024), BF16), SDS((s, 256), F32), SDS((s, 256), F32), SDS((8, LANES), F32)],
        compiler_params=_cparams(("arbitrary",)),
    )(sinks, bias, qkn, qkn, qkn, proj, proj, do)


def _my_pos():
    return lax.axis_index("x"), lax.axis_index("y"), lax.axis_index("c")


ANY = pl.BlockSpec(memory_space=pl.ANY)


def all_gather_packed(xs, name):
    r, n = xs.shape

    def body(x_ref, out_ref, send_sems, recv_sems, local_sem):
        x, y, c = _my_pos()
        me, sibling = (x, y, c), (x, y, 1 - c)
        south = c == 0
        via = (jnp.where(south, 1 - x, x), jnp.where(south, y, 1 - y))
        far = (jnp.where(south, x, 1 - x), jnp.where(south, 1 - y, y))
        rk = jnp.where(south, 1, 2)
        diag = (1 - x, 1 - y)

        def slot(px, py, pc):
            return out_ref.at[4 * px + 2 * py + pc]

        def copy(k, block, to, src=None):
            return pltpu.make_async_remote_copy(
                src_ref=slot(*block) if src is None else src, dst_ref=slot(*block),
                send_sem=send_sems.at[k], recv_sem=recv_sems.at[k],
                device_id=to, device_id_type=MESH_ID)

        mine = pltpu.make_async_copy(x_ref, slot(*me), local_sem)
        mine.start()
        sent = [copy(0, me, sibling, src=x_ref), copy(1, me, (1 - x, y, c), src=x_ref),
                copy(2, me, (x, 1 - y, c), src=x_ref)]
        for cp in sent:
            cp.start()
        copy(rk, (*via, c), me).wait_recv()
        sent += [copy(3, (*via, c), (*far, c)), copy(3 + rk, (*via, c), sibling)]
        sent[-2].start()
        sent[-1].start()
        copy(3 - rk, (*far, c), me).wait_recv()
        sent.append(copy(6 - rk, (*far, c), sibling))
        sent[-1].start()
        copy(3, (*diag, c), me).wait_recv()
        sent.append(copy(6, (*diag, c), sibling))
        sent[-1].start()
        copy(0, sibling, me).wait_recv()
        for k, chip in ((4, (1 - x, y)), (5, (x, 1 - y)), (6, diag)):
            copy(k, (*chip, 1 - c), me).wait_recv()
        for cp in sent:
            cp.wait_send()
        mine.wait()

    return pl.pallas_call(
        body, name=name,
        out_shape=SDS((N_DEV, r, n), xs.dtype),
        in_specs=[ANY], out_specs=ANY,
        scratch_shapes=[pltpu.SemaphoreType.DMA((7,)), pltpu.SemaphoreType.DMA((7,)), pltpu.SemaphoreType.DMA],
    )(xs)


def rs_sibling_exchange(g, name):
    _, r, n = g.shape

    def body(g_ref, ra_ref, send_sems, recv_sems):
        x, y, c = _my_pos()
        copies = [pltpu.make_async_remote_copy(
            src_ref=g_ref.at[2 * chip + (1 - c)], dst_ref=ra_ref.at[chip],
            send_sem=send_sems.at[chip], recv_sem=recv_sems.at[chip],
            device_id=(x, y, 1 - c), device_id_type=MESH_ID) for chip in range(4)]
        for cp in copies:
            cp.start()
        for cp in copies:
            cp.wait()

    return pl.pallas_call(
        body, name=name, out_shape=SDS((4, r, n), g.dtype), in_specs=[ANY], out_specs=ANY,
        scratch_shapes=[pltpu.SemaphoreType.DMA((4,)), pltpu.SemaphoreType.DMA((4,))],
    )(g)


def rs_pair_add(g, ra, cidx, name):
    _, r, n = g.shape
    tr = r // 2

    def body(c_ref, g_ref, ra_ref, p_ref):
        p_ref[...] = (g_ref[...].astype(F32) + ra_ref[...].astype(F32)).astype(p_ref.dtype)

    return pl.pallas_call(
        body, name=name,
        grid_spec=pltpu.PrefetchScalarGridSpec(
            num_scalar_prefetch=1, grid=(4, r // tr),
            in_specs=[pl.BlockSpec((1, tr, n), lambda i, j, c: (2 * i + c[0], j, 0)),
                      pl.BlockSpec((1, tr, n), lambda i, j, c: (i, j, 0))],
            out_specs=pl.BlockSpec((1, tr, n), lambda i, j, c: (i, j, 0))),
        out_shape=SDS((4, r, n), g.dtype),
        compiler_params=_cparams(("parallel", "parallel")),
    )(cidx, g, ra)


def rs_cross_exchange(p, name):
    _, r, n = p.shape

    def body(p_ref, rb_ref, send_sems, recv_sems):
        x, y, c = _my_pos()
        copies = []
        for k, (dx, dy) in enumerate(((1, 0), (0, 1), (1, 1))):
            tx = 1 - x if dx else x
            ty = 1 - y if dy else y
            copies.append(pltpu.make_async_remote_copy(
                src_ref=p_ref.at[2 * tx + ty], dst_ref=rb_ref.at[k],
                send_sem=send_sems.at[k], recv_sem=recv_sems.at[k],
                device_id=(tx, ty, c), device_id_type=MESH_ID))
        for cp in copies:
            cp.start()
        for cp in copies:
            cp.wait()

    return pl.pallas_call(
        body, name=name, out_shape=SDS((3, r, n), p.dtype), in_specs=[ANY], out_specs=ANY,
        scratch_shapes=[pltpu.SemaphoreType.DMA((3,)), pltpu.SemaphoreType.DMA((3,))],
    )(p)


def _adamw(w, g, m, v):
    m = ADAM_B1 * m + (1.0 - ADAM_B1) * g
    v = ADAM_B2 * v + (1.0 - ADAM_B2) * (g * g)
    m_hat = m / (1.0 - ADAM_B1 ** ADAM_STEP)
    v_hat = v / (1.0 - ADAM_B2 ** ADAM_STEP)
    delta = -ADAM_LR * (m_hat / (jnp.sqrt(v_hat) + ADAM_EPS) + ADAM_WD * w)
    return delta, m, v


def rs_final_adamw(p, rb, chipidx, w, m, v, row0, col_blk, name):
    r, n = w.shape
    tr = min(r, 256 if n <= PAIR_COLS else 128)
    assert r % tr == 0 and row0 % tr == 0
    rb0 = row0 // tr

    def body(c_ref, p_ref, rb_ref, w_ref, m_ref, v_ref, g_out, d_out, m_out, v_out):
        g = p_ref[0].astype(F32)
        for k in range(3):
            g = g + rb_ref[k].astype(F32)
        d, mn, vn = _adamw(w_ref[...], g, m_ref[...], v_ref[...])
        g_out[...] = g
        d_out[...] = d
        m_out[...] = mn
        v_out[...] = vn

    blk = pl.BlockSpec((tr, n), lambda j, c: (j, 0))
    return pl.pallas_call(
        body, name=name,
        grid_spec=pltpu.PrefetchScalarGridSpec(
            num_scalar_prefetch=1, grid=(r // tr,),
            in_specs=[pl.BlockSpec((1, tr, n), lambda j, c: (c[0], rb0 + j, col_blk)),
                      pl.BlockSpec((3, tr, n), lambda j, c: (0, rb0 + j, col_blk)), blk, blk, blk],
            out_specs=[blk, blk, blk, blk]),
        out_shape=[SDS((r, n), F32)] * 4,
        compiler_params=_cparams(("parallel",)),
    )(chipidx, p, rb, w, m, v)


def small_allreduce_adamw(g, w, m, v, name):
    rows = g.shape[0]

    def body(g_ref, w_ref, m_ref, v_ref, g_out, d_out, m_out, v_out, buf, send_sems, recv_sems):
        x, y, c = _my_pos()
        my = 4 * x + 2 * y + c
        buf[my] = g_ref[...]
        copies = []
        for k in range(1, N_DEV):
            dx, dy, dc = (k >> 2) & 1, (k >> 1) & 1, k & 1
            tgt = (1 - x if dx else x, 1 - y if dy else y, 1 - c if dc else c)
            copies.append(pltpu.make_async_remote_copy(
                src_ref=buf.at[my], dst_ref=buf.at[my],
                send_sem=send_sems.at[k - 1], recv_sem=recv_sems.at[k - 1],
                device_id=tgt, device_id_type=MESH_ID))
        for cp in copies:
            cp.start()
        for cp in copies:
            cp.wait()
        tot = buf[0]
        for j in range(1, N_DEV):
            tot = tot + buf[j]
        d, mn, vn = _adamw(w_ref[...], tot, m_ref[...], v_ref[...])
        g_out[...] = tot
        d_out[...] = d
        m_out[...] = mn
        v_out[...] = vn

    vm = pl.BlockSpec(memory_space=pltpu.VMEM)
    return pl.pallas_call(
        body, name=name, out_shape=[SDS((rows, LANES), F32)] * 4,
        in_specs=[vm] * 4, out_specs=[vm] * 4,
        scratch_shapes=[pltpu.VMEM((N_DEV, rows, LANES), F32),
                        pltpu.SemaphoreType.DMA((N_DEV - 1,)), pltpu.SemaphoreType.DMA((N_DEV - 1,))],
    )(g, w, m, v)


BIG = ["l0_w_in", "l0_w_out", "l1_w_in", "l1_w_uq", "l1_w_ukv", "l1_w_out", "l2_w_in", "l2_w_out",
       "l3_w_in", "l3_w_out"]
SMALL = ["l0_norm", "l1_norm", "l1_q_a_norm", "l1_kv_a_norm", "l1_q_head_norm", "l1_k_head_norm",
         "l2_norm", "l2_q_head_norm", "l2_k_head_norm", "l2_sinks", "l3_norm"]

MLA_UQ_PERM = np.concatenate([np.arange(h * 192, h * 192 + 128) for h in range(8)]
                             + [np.arange(h * 192 + 128, (h + 1) * 192) for h in range(8)])
SWA_IN_PERM = np.concatenate([SWA_Q_PERM, 1536 + SWA_Q_PERM, np.arange(1024, 1536)])


def _inv(perm):
    inv = np.empty_like(perm)
    inv[perm] = np.arange(perm.size)
    return inv


OUTS = ["l0_w_out", "l1_w_out", "l2_w_out", "l3_w_out"]
REST = ["l1_w_in", "l1_w_uq", "l1_w_ukv", "l2_w_in"]


def _pack_rest(shards):
    parts = [shards[nm].reshape(-1, D_MODEL) for nm in REST]
    used = sum(p.shape[0] for p in parts)
    assert REST_ROW0 + used == ROWS_USED
    parts.append(jnp.zeros((REST_ROWS - used, D_MODEL), parts[0].dtype))
    return jnp.concatenate(parts, axis=0)


def _unpack_rest(packed, shapes):
    out, off = {}, 0
    for nm in REST:
        shp = shapes[nm]
        rows = shp[0] * shp[1] // D_MODEL
        out[nm] = packed[off:off + rows].reshape(shp)
        off += rows
    return out


def _pack(shards):
    pair = jnp.concatenate([shards["l0_w_in"], shards["l3_w_in"]], axis=1)
    return jnp.concatenate([pair] + [shards[nm] for nm in OUTS] + [_pack_rest(shards)], axis=0)


def _out_weight(gathered, layer):
    r0 = OUT_ROW0 + 128 * layer
    return gathered[:, r0:r0 + 128].reshape(D_MODEL, D_MODEL)


def _rest_full_weights(gathered, shapes):
    out, off = {}, REST_ROW0
    for nm in REST:
        shp = shapes[nm]
        rows = shp[0] * shp[1] // D_MODEL
        blk = gathered[:, off:off + rows].reshape((N_DEV,) + shp)
        out[nm] = jnp.transpose(blk, (1, 0, 2)).reshape(shp[0], N_DEV * shp[1])
        off += rows
    return out


def _grad_pieces(pair_l0, pair_l3, dw_out, rest_grads, shapes):
    parts = [jnp.concatenate([pair_l0, pair_l3], axis=2)]
    parts += [g.reshape(N_DEV, 128, D_MODEL).astype(BF16) for g in dw_out]
    used = REST_ROW0
    for nm in REST:
        shp = shapes[nm]
        blk = jnp.transpose(rest_grads[nm].reshape(shp[0], N_DEV, shp[1]), (1, 0, 2))
        parts.append(blk.reshape(N_DEV, -1, D_MODEL).astype(BF16))
        used += parts[-1].shape[1]
    parts.append(jnp.zeros((N_DEV, ROWS_PACK - used, D_MODEL), BF16))
    return jnp.concatenate(parts, axis=1)


SMALL_USED = 5008


def _pack_small(vals, extra=None):
    parts = [vals[nm].reshape(-1).astype(F32) for nm in SMALL]
    assert sum(p.shape[0] for p in parts) == SMALL_USED
    if extra is not None:
        parts.append(extra.reshape(1).astype(F32))
    flat = jnp.concatenate(parts)
    flat = jnp.concatenate([flat, jnp.zeros((SMALL_ROWS * LANES - flat.shape[0],), F32)])
    return flat.reshape(SMALL_ROWS, LANES)


def _unpack_small(packed, shapes):
    flat = packed.reshape(-1)
    out, off = {}, 0
    for nm in SMALL:
        n = shapes[nm][0]
        out[nm] = flat[off:off + n]
        off += n
    return out


def _row(v):
    return v.reshape(1, -1).astype(F32)


def _mla_gain_rows(g):
    return jnp.concatenate([g[:128], g[128:], g[128:]]).reshape(1, 256).astype(F32)


def _layer_fwd_sb(x, norm, gathered, shard_col, w_out, tag):
    proj, xn = norm_matmul(x, 0, _row(norm), gathered, f"{tag}_in", shard_col=shard_col)
    o = sb_attn_fwd(proj, f"{tag}_attn")
    y = gate_out_fwd(o, proj, 3, w_out, x, f"{tag}_out")
    return y, (x, xn, proj, o)


def _layer_bwd_sb(dy, saved, norm, gathered, shard_col, w_out, tag):
    x, xn, proj, o = saved
    do, dgate, hg = gate_out_bwd(dy, w_out, o, proj, 3, f"{tag}_outb")
    dw_out = tn_matmul(hg, dy, f"{tag}_dwout")
    dq, dk, dv = sb_attn_bwd(proj, o, do, f"{tag}_attnb")
    dproj = [dq, dk, dv, dgate]
    dw_in = tn_matmul_shards(xn, dproj, f"{tag}_dwin")
    dx, dnorm = nt_norm_bwd(dproj, gathered, x, 0, _row(norm), dy, f"{tag}_inb", F32, shard_col=shard_col)
    return dx, dw_in, dw_out, dnorm.reshape(-1)


def kernel(x, l0_norm, l0_w_in, l0_w_out, l1_norm, l1_w_in, l1_q_a_norm, l1_w_uq, l1_kv_a_norm, l1_w_ukv, l1_q_head_norm, l1_k_head_norm, l1_w_out, l2_norm, l2_w_in, l2_q_head_norm, l2_k_head_norm, l2_sinks, l2_w_out, l3_norm, l3_w_in, l3_w_out, loss_target, m_l0_norm, m_l0_w_in, m_l0_w_out, m_l1_norm, m_l1_w_in, m_l1_q_a_norm, m_l1_w_uq, m_l1_kv_a_norm, m_l1_w_ukv, m_l1_q_head_norm, m_l1_k_head_norm, m_l1_w_out, m_l2_norm, m_l2_w_in, m_l2_q_head_norm, m_l2_k_head_norm, m_l2_sinks, m_l2_w_out, m_l3_norm, m_l3_w_in, m_l3_w_out, v_l0_norm, v_l0_w_in, v_l0_w_out, v_l1_norm, v_l1_w_in, v_l1_q_a_norm, v_l1_w_uq, v_l1_kv_a_norm, v_l1_w_ukv, v_l1_q_head_norm, v_l1_k_head_norm, v_l1_w_out, v_l2_norm, v_l2_w_in, v_l2_q_head_norm, v_l2_k_head_norm, v_l2_sinks, v_l2_w_out, v_l3_norm, v_l3_w_in, v_l3_w_out):
    loc = dict(locals())
    names = BIG + SMALL
    w = {nm: loc[nm] for nm in names}
    mom = {nm: loc["m_" + nm] for nm in names}
    vel = {nm: loc["v_" + nm] for nm in names}
    shapes = {nm: w[nm].shape for nm in names}
    xs = x[0]
    tgt = loss_target[0]

    gathered = all_gather_packed(_pack({nm: w[nm].astype(BF16) for nm in BIG}), "ag_weights")
    loss_local, dx0, pieces, sg = _local_step(xs, tgt, gathered, {nm: w[nm] for nm in SMALL}, shapes)
    return _reduce_and_update(loss_local, dx0, pieces, sg, w, mom, vel, shapes)


def _local_step(xs, tgt, gathered, sm, shapes):
    s = xs.shape[0]
    fw = _rest_full_weights(gathered, shapes)
    w_out = [_out_weight(gathered, layer) for layer in range(4)]
    l0_norm, l1_norm, l2_norm, l3_norm = sm["l0_norm"], sm["l1_norm"], sm["l2_norm"], sm["l3_norm"]
    l1_q_a_norm, l1_kv_a_norm = sm["l1_q_a_norm"], sm["l1_kv_a_norm"]
    l1_q_head_norm, l1_k_head_norm = sm["l1_q_head_norm"], sm["l1_k_head_norm"]
    l2_q_head_norm, l2_k_head_norm, l2_sinks = sm["l2_q_head_norm"], sm["l2_k_head_norm"], sm["l2_sinks"]
    w1_in = jnp.concatenate([fw["l1_w_in"][:, 448:], fw["l1_w_in"][:, :448], jnp.zeros((D_MODEL, 64), BF16)], axis=1)
    w1_uq = fw["l1_w_uq"][:, MLA_UQ_PERM]
    w2_in = fw["l2_w_in"][:, SWA_IN_PERM]
    w2_out = w_out[2][SWA_Q_PERM, :]

    x1, sv0 = _layer_fwd_sb(xs, l0_norm, gathered, 0, w_out[0], "l0")

    cos, sin = _rope_tables(s)
    gq, gk = _mla_gain_rows(l1_q_head_norm), _mla_gain_rows(l1_k_head_norm)
    proj1, xn1 = norm_matmul(x1, 0, _row(l1_norm), w1_in, "l1_in")
    qa, qln = norm_matmul(proj1, 4, _row(l1_q_a_norm), w1_uq, "l1_uq")
    kva, kvn = norm_matmul(proj1, 10, _row(l1_kv_a_norm), fw["l1_w_ukv"], "l1_ukv")
    q_p, k_p = mla_prep_fwd(qa, kva, proj1, gq, gk, cos, sin, "l1_prep")
    o1, lse1 = mla_attn_fwd(q_p, k_p, kva, "l1_attn")
    x2 = gate_out_fwd(o1, proj1, 0, w_out[1], x1, "l1_out")

    g2 = jnp.stack([jnp.tile(l2_q_head_norm, 2), jnp.tile(l2_k_head_norm, 2)]).astype(F32)
    proj2, xn2 = norm_matmul(x2, 0, _row(l2_norm), w2_in, "l2_in")
    qkn2 = swa_prep_fwd(proj2, g2, "l2_prep")
    bias2 = _swa_bias()
    o2 = swa_attn_fwd(qkn2, proj2, l2_sinks, bias2, "l2_attn")
    x3 = gate_out_fwd(o2, proj2, 1, w2_out, x2, "l2_out")

    x4, sv3 = _layer_fwd_sb(x3, l3_norm, gathered, 1, w_out[3], "l3")

    lossblk, dy = loss_kernel(x4, tgt, "loss")

    fg, sg = {}, {}
    dw_out = [None] * 4
    dx3, pair_l3, dw_out[3], sg["l3_norm"] = _layer_bwd_sb(dy, sv3, l3_norm, gathered, 1, w_out[3], "l3")

    do2, dgate2, hg2 = gate_out_bwd(dx3, w2_out, o2, proj2, 1, "l2_outb")
    dw2_out = tn_matmul(hg2, dx3, "l2_dwout")
    dqn2, dkn2, dv2, dsink2 = swa_attn_bwd(qkn2, proj2, l2_sinks, bias2, do2, "l2_attnb")
    dqk2, dg2 = swa_prep_bwd(dqn2, dkn2, proj2, g2, "l2_prepb")
    dproj2 = jnp.concatenate([dqk2[:, :1024], dgate2, dqk2[:, 1024:], dv2.astype(BF16)], axis=1)
    dw2_in = tn_matmul(xn2, dproj2, "l2_dwin")
    dx2, dn2 = nt_norm_bwd(dproj2, w2_in, x2, 0, _row(l2_norm), dx3, "l2_inb", F32)
    fg["l2_w_in"] = dw2_in[:, _inv(SWA_IN_PERM)]
    dw_out[2] = dw2_out[_inv(SWA_Q_PERM), :]
    sg["l2_norm"] = dn2.reshape(-1)
    sg["l2_q_head_norm"] = dg2[0, :HALF]
    sg["l2_k_head_norm"] = dg2[1, :HALF]
    sg["l2_sinks"] = dsink2[0, :SWA_HEADS]

    do1, dgate1, hg1 = gate_out_bwd(dx2, w_out[1], o1, proj1, 0, "l1_outb")
    dw_out[1] = tn_matmul(hg1, dx2, "l1_dwout")
    dq_p, dk_p, dv1 = mla_attn_bwd(q_p, k_p, kva, o1, do1, lse1, "l1_attnb")
    dqa, dkva, dkpe, dgq, dgk = mla_prep_bwd(dq_p, dk_p, dv1, qa, kva, proj1, gq, gk, cos, sin, "l1_prepb")
    dw_uq = tn_matmul(qln, dqa, "l1_dwuq")
    fg["l1_w_ukv"] = tn_matmul(kvn, dkva, "l1_dwukv")
    dqlat, dgqa = nt_norm_bwd(dqa, w1_uq, proj1, 4, _row(l1_q_a_norm), None, "l1_uqb", BF16)
    dkvlat, dgkva = nt_norm_bwd(dkva, fw["l1_w_ukv"], proj1, 10, _row(l1_kv_a_norm), None, "l1_ukvb", BF16)
    dproj1 = jnp.concatenate([dgate1, dqlat, dkvlat, dkpe], axis=1)
    dw1_in = tn_matmul(xn1, dproj1, "l1_dwin")
    dx1, dn1 = nt_norm_bwd(dproj1, w1_in, x1, 0, _row(l1_norm), dx2, "l1_inb", F32)
    fg["l1_w_in"] = jnp.concatenate([dw1_in[:, 1024:1472], dw1_in[:, :1024]], axis=1)
    fg["l1_w_uq"] = dw_uq[:, _inv(MLA_UQ_PERM)]
    sg["l1_norm"] = dn1.reshape(-1)
    sg["l1_q_a_norm"] = dgqa.reshape(-1)
    sg["l1_kv_a_norm"] = dgkva.reshape(-1)
    sg["l1_q_head_norm"] = jnp.concatenate([dgq[0, :128], dgq[0, 128:192] + dgq[0, 192:256]])
    sg["l1_k_head_norm"] = jnp.concatenate([dgk[0, :128], dgk[0, 128:192] + dgk[0, 192:256]])

    dx0, pair_l0, dw_out[0], sg["l0_norm"] = _layer_bwd_sb(dx1, sv0, l0_norm, gathered, 0, w_out[0], "l0")
    return lossblk[0, 0], dx0, _grad_pieces(pair_l0, pair_l3, dw_out, fg, shapes), sg


def _reduce_and_update(loss_local, dx0, pieces, sg, w, mom, vel, shapes):
    xi, yi, ci = _my_pos()
    cidx = jnp.reshape(ci, (1,)).astype(jnp.int32)
    chipidx = jnp.reshape(2 * xi + yi, (1,)).astype(jnp.int32)
    ra = rs_sibling_exchange(pieces, "rs_sibling")
    part = rs_pair_add(pieces, ra, cidx, "rs_pair_add")
    rb = rs_cross_exchange(part, "rs_cross")

    def update(nm, row0, col_blk):
        return rs_final_adamw(part, rb, chipidx, w[nm], mom[nm], vel[nm], row0, col_blk, f"adamw_{nm}")

    big = {"l0_w_in": update("l0_w_in", 0, 0), "l3_w_in": update("l3_w_in", 0, 1)}
    for layer, nm in enumerate(OUTS):
        big[nm] = update(nm, OUT_ROW0 + 128 * layer, 0)
    rest = rs_final_adamw(part, rb, chipidx, _pack_rest(w), _pack_rest(mom), _pack_rest(vel), REST_ROW0, 0,
                          "adamw_rest")
    gs, ds, ms, vs = small_allreduce_adamw(_pack_small(sg, loss_local), _pack_small({nm: w[nm] for nm in SMALL}),
                                           _pack_small({nm: mom[nm] for nm in SMALL}),
                                           _pack_small({nm: vel[nm] for nm in SMALL}), "small_allreduce")

    outs = []
    for i, small in enumerate((gs, ds, ms, vs)):
        d = {nm: big[nm][i] for nm in big}
        d.update(_unpack_rest(rest[i], shapes))
        d.update(_unpack_small(small, shapes))
        outs.append(d)
    order = ["l0_norm", "l0_w_in", "l0_w_out", "l1_norm", "l1_w_in", "l1_q_a_norm", "l1_w_uq", "l1_kv_a_norm",
             "l1_w_ukv", "l1_q_head_norm", "l1_k_head_norm", "l1_w_out", "l2_norm", "l2_w_in", "l2_q_head_norm",
             "l2_k_head_norm", "l2_sinks", "l2_w_out", "l3_norm", "l3_w_in", "l3_w_out"]
    flat = [gs.reshape(-1)[SMALL_USED], dx0[None]]
    for d in outs:
        flat += [d[nm] for nm in order]
    return tuple(flat)
```

```python
import math

import numpy as np
import jax
import jax.numpy as jnp
from jax import lax
from jax.experimental import pallas as pl
from jax.experimental.pallas import tpu as pltpu

F32 = jnp.float32
BF16 = jnp.bfloat16
SDS = jax.ShapeDtypeStruct
MESH_ID = pl.DeviceIdType.MESH

D_MODEL = 1024
NORM_EPS = 1e-6
N_DEV = 8
LANES = 128
HALF = 64
ROWS_PACK = 2176
ROWS_USED = 2120
PAIR_COLS = 512
OUT_ROW0 = 1024
REST_ROW0 = 1536
REST_ROWS = ROWS_PACK - REST_ROW0
SMALL_ROWS = 40

ADAM_LR = 0.001
ADAM_B1 = 0.9
ADAM_B2 = 0.999
ADAM_EPS = 1e-08
ADAM_WD = 0.01
ADAM_STEP = 10

NT_DIMS = (((1,), (1,)), ((), ()))
TN_DIMS = (((0,), (0,)), ((), ()))


def _cparams(sem=None, vmem_mb=48):
    return pltpu.CompilerParams(dimension_semantics=sem, vmem_limit_bytes=vmem_mb * 2 ** 20)


def _dot(a, b):
    return jnp.dot(a, b, preferred_element_type=F32)


def _dot_nt(a, b):
    return lax.dot_general(a, b, NT_DIMS, preferred_element_type=F32)


def _dot_tn(a, b):
    return lax.dot_general(a, b, TN_DIMS, preferred_element_type=F32)


def _split_dot(a, b):
    hi = a.astype(BF16)
    lo = (a - hi.astype(F32)).astype(BF16)
    return _dot(hi, b) + _dot(lo, b)


def _cumsum_dot(lf, u):
    return _dot(lf.astype(BF16), u)


def _pick_tile(n, cap):
    return max(t for t in range(LANES, min(n, cap) + 1, LANES) if n % t == 0)


def _sigmoid(x):
    return 1.0 / (1.0 + jnp.exp(-x))


def norm_matmul(x, colblk, g, w, name, shard_col=None):
    s = x.shape[0]
    if shard_col is None:
        k, n = w.shape
        tn = _pick_tile(n, 1024)
        w_spec = pl.BlockSpec((k, tn), lambda i, j: (0, j))
    else:
        k, n, tn = D_MODEL, N_DEV * PAIR_COLS, 2 * PAIR_COLS
        w_spec = pl.BlockSpec((2, k, PAIR_COLS), lambda i, j: (j, 0, shard_col))
    tm = min(1024, s)
    assert s % tm == 0

    def body(x_ref, g_ref, w_ref, o_ref, xn_ref, xn_s):
        @pl.when(pl.program_id(1) == 0)
        def _():
            xf = x_ref[...].astype(F32)
            ms = jnp.mean(xf * xf, axis=-1, keepdims=True)
            y = (xf * lax.rsqrt(ms + NORM_EPS) * g_ref[...]).astype(BF16)
            xn_s[...] = y
            xn_ref[...] = y

        if shard_col is None:
            o_ref[...] = _dot(xn_s[...], w_ref[...]).astype(o_ref.dtype)
        else:
            for half in range(2):
                o_ref[:, half * PAIR_COLS:(half + 1) * PAIR_COLS] = _dot(xn_s[...], w_ref[half]).astype(o_ref.dtype)

    return pl.pallas_call(
        body, name=name, grid=(s // tm, n // tn),
        in_specs=[pl.BlockSpec((tm, k), lambda i, j: (i, colblk)),
                  pl.BlockSpec((1, k), lambda i, j: (0, 0)),
                  w_spec],
        out_specs=[pl.BlockSpec((tm, tn), lambda i, j: (i, j)),
                   pl.BlockSpec((tm, k), lambda i, j: (i, 0))],
        out_shape=[SDS((s, n), BF16), SDS((s, k), BF16)],
        scratch_shapes=[pltpu.VMEM((tm, k), BF16)],
        compiler_params=_cparams(("parallel", "arbitrary")),
    )(x, g, w)


def gate_out_fwd(o, proj, gate_blk, w_out, x, name):
    s = x.shape[0]
    d = D_MODEL
    tm = min(512, s)

    def body(o_ref, g_ref, w_ref, x_ref, y_ref):
        g = g_ref[...].astype(F32)
        hg = (o_ref[...].astype(F32) * (g * _sigmoid(g))).astype(BF16)
        y_ref[...] = x_ref[...] + _dot(hg, w_ref[...])

    return pl.pallas_call(
        body, name=name, grid=(s // tm,),
        in_specs=[pl.BlockSpec((tm, d), lambda i: (i, 0)),
                  pl.BlockSpec((tm, d), lambda i: (i, gate_blk)),
                  pl.BlockSpec((d, d), lambda i: (0, 0)),
                  pl.BlockSpec((tm, d), lambda i: (i, 0))],
        out_specs=pl.BlockSpec((tm, d), lambda i: (i, 0)),
        out_shape=SDS((s, d), F32),
        compiler_params=_cparams(("parallel",)),
    )(o, proj, w_out, x)


def gate_out_bwd(dy, w_out, o, proj, gate_blk, name):
    s = dy.shape[0]
    d = D_MODEL
    tm = min(512, s)

    def body(dy_ref, w_ref, o_ref, g_ref, do_ref, dg_ref, hg_ref):
        dh = _dot_nt(dy_ref[...].astype(BF16), w_ref[...])
        g = g_ref[...].astype(F32)
        o = o_ref[...].astype(F32)
        sg = _sigmoid(g)
        silu = g * sg
        do_ref[...] = (dh * silu).astype(BF16)
        dg_ref[...] = (dh * o * (sg * (1.0 + g * (1.0 - sg)))).astype(BF16)
        hg_ref[...] = (o * silu).astype(BF16)

    blk = pl.BlockSpec((tm, d), lambda i: (i, 0))
    return pl.pallas_call(
        body, name=name, grid=(s // tm,),
        in_specs=[blk, pl.BlockSpec((d, d), lambda i: (0, 0)), blk,
                  pl.BlockSpec((tm, d), lambda i: (i, gate_blk))],
        out_specs=[blk, blk, blk],
        out_shape=[SDS((s, d), BF16)] * 3,
        compiler_params=_cparams(("parallel",)),
    )(dy, w_out, o, proj)


def tn_matmul(a, b, name):
    s, k = a.shape
    n = b.shape[1]
    tk = min(1024, k)
    tn = _pick_tile(n, 512)
    ts = min(2048, s)
    assert k % tk == 0 and s % ts == 0

    def body(a_ref, b_ref, o_ref):
        @pl.when(pl.program_id(2) == 0)
        def _():
            o_ref[...] = jnp.zeros_like(o_ref)

        o_ref[...] += _dot_tn(a_ref[...].astype(BF16), b_ref[...].astype(BF16))

    return pl.pallas_call(
        body, name=name, grid=(k // tk, n // tn, s // ts),
        in_specs=[pl.BlockSpec((ts, tk), lambda i, j, r: (r, i)),
                  pl.BlockSpec((ts, tn), lambda i, j, r: (r, j))],
        out_specs=pl.BlockSpec((tk, tn), lambda i, j, r: (i, j)),
        out_shape=SDS((k, n), F32),
        compiler_params=_cparams(("parallel", "parallel", "arbitrary")),
    )(a, b)


def tn_matmul_shards(a, b_parts, name):
    s, k = a.shape
    per = D_MODEL // PAIR_COLS
    assert len(b_parts) * per == N_DEV and a.dtype == BF16
    assert all(b.shape == (s, D_MODEL) and b.dtype == BF16 for b in b_parts)

    def body(a_ref, *refs):
        o_ref = refs[-1]
        j = pl.program_id(0)
        for p, b_ref in enumerate(refs[:-1]):
            @pl.when(j // per == p)
            def _(b_ref=b_ref):
                o_ref[...] = _dot_tn(a_ref[...], b_ref[...]).astype(BF16)

    part_specs = [pl.BlockSpec((s, PAIR_COLS), lambda j, p=p: (0, jnp.clip(j - per * p, 0, per - 1)))
                  for p in range(len(b_parts))]
    return pl.pallas_call(
        body, name=name, grid=(N_DEV,),
        in_specs=[pl.BlockSpec((s, k), lambda j: (0, 0))] + part_specs,
        out_specs=pl.BlockSpec((None, k, PAIR_COLS), lambda j: (j, 0, 0)),
        out_shape=SDS((N_DEV, k, PAIR_COLS), BF16),
        compiler_params=_cparams(("arbitrary",)),
    )(a, *b_parts)


def nt_norm_bwd(dp, w, x, xcol, g, res, name, out_dtype, shard_col=None):
    tm = min(512, dp[0].shape[0] if shard_col is not None else dp.shape[0])
    if shard_col is None:
        s, n = dp.shape
        k = w.shape[0]
        w_spec = pl.BlockSpec((k, n), lambda i: (0, 0))
        dp_parts = [dp]
        dp_specs = [pl.BlockSpec((tm, n), lambda i: (i, 0))]
    else:
        s, k = dp[0].shape[0], D_MODEL
        w_spec = pl.BlockSpec((N_DEV, k, PAIR_COLS), lambda i: (0, 0, shard_col))
        dp_parts = list(dp)
        dp_specs = [pl.BlockSpec((tm, D_MODEL), lambda i: (i, 0)) for _ in dp_parts]
    np_ = len(dp_parts)
    per = D_MODEL // PAIR_COLS
    assert s % tm == 0
    has_res = res is not None

    def body(*refs):
        dp_refs = refs[:np_]
        if has_res:
            w_ref, x_ref, g_ref, r_ref, dx_ref, dg_ref = refs[np_:]
        else:
            w_ref, x_ref, g_ref, dx_ref, dg_ref = refs[np_:]

        @pl.when(pl.program_id(0) == 0)
        def _():
            dg_ref[...] = jnp.zeros_like(dg_ref)

        if shard_col is None:
            dxn = _dot_nt(dp_refs[0][...], w_ref[...])
        else:
            dxn = None
            for j in range(N_DEV):
                c0 = (j % per) * PAIR_COLS
                term = _dot_nt(dp_refs[j // per][:, c0:c0 + PAIR_COLS], w_ref[j])
                dxn = term if dxn is None else dxn + term
        xf = x_ref[...].astype(F32)
        rstd = lax.rsqrt(jnp.mean(xf * xf, axis=-1, keepdims=True) + NORM_EPS)
        xhat = xf * rstd
        dg_ref[...] += jnp.sum(dxn * xhat, axis=0, keepdims=True)
        dxh = dxn * g_ref[...]
        dx = rstd * (dxh - xhat * jnp.mean(dxh * xhat, axis=-1, keepdims=True))
        if has_res:
            dx = dx + r_ref[...]
        dx_ref[...] = dx.astype(out_dtype)

    in_specs = dp_specs + [w_spec,
                           pl.BlockSpec((tm, k), lambda i: (i, xcol)),
                           pl.BlockSpec((1, k), lambda i: (0, 0))]
    args = dp_parts + [w, x, g]
    if has_res:
        in_specs.append(pl.BlockSpec((tm, k), lambda i: (i, 0)))
        args.append(res)
    return pl.pallas_call(
        body, name=name, grid=(s // tm,),
        in_specs=in_specs,
        out_specs=[pl.BlockSpec((tm, k), lambda i: (i, 0)),
                   pl.BlockSpec((1, k), lambda i: (0, 0))],
        out_shape=[SDS((s, k), out_dtype), SDS((1, k), F32)],
        compiler_params=_cparams(("arbitrary",)),
    )(*args)


def loss_kernel(y, target, name):
    s, d = y.shape
    tm = min(512, s)

    def body(y_ref, t_ref, l_ref, dy_ref):
        @pl.when(pl.program_id(0) == 0)
        def _():
            l_ref[...] = jnp.zeros_like(l_ref)

        e = y_ref[...] - t_ref[...]
        dy_ref[...] = e * (1.0 / d)
        l_ref[...] += 0.5 * jnp.sum(jnp.mean(e * e, axis=-1, keepdims=True))

    blk = pl.BlockSpec((tm, d), lambda i: (i, 0))
    return pl.pallas_call(
        body, name=name, grid=(s // tm,),
        in_specs=[blk, blk],
        out_specs=[pl.BlockSpec((8, LANES), lambda i: (0, 0)), blk],
        out_shape=[SDS((8, LANES), F32), SDS((s, d), F32)],
        compiler_params=_cparams(("arbitrary",)),
    )(y, target)


SB_LOG_CUTOFF = -80.0
SB_T = 256
SB_TK_LEFT = 128
SB_FWD_TILES = 4
SB_BWD_TILES = 2


def _sb_sweep(n_left, cs, rest, step):
    n = len(cs)

    def top(arrs):
        m = jnp.max(arrs[0])
        for a in arrs[1:]:
            m = jnp.maximum(m, jnp.max(a))
        return m

    def cond(st):
        return (st[0] <= n_left) & (st[1] > SB_LOG_CUTOFF)

    def body(st):
        cs2, rest2 = step(st[0], st[2:2 + n], st[2 + n:])
        return (st[0] + 1, top(cs2)) + tuple(cs2) + tuple(rest2)

    return lax.while_loop(cond, body, (jnp.int32(1), top(cs)) + tuple(cs) + tuple(rest))[2 + n:]


def _tile_masks(t):
    row = lax.broadcasted_iota(jnp.int32, (t, t), 0)
    col = lax.broadcasted_iota(jnp.int32, (t, t), 1)
    return row, col


def sb_attn_fwd(proj, name):
    s = proj.shape[0]
    n_tiles = SB_FWD_TILES
    t = min(SB_T, s)
    tko = min(SB_TK_LEFT, t)
    nq = s // t
    scale = 1.0 / math.sqrt(HALF)

    def body(q_ref, k_ref, v_ref, o_ref):
        lane = lax.broadcasted_iota(jnp.int32, (1, LANES), 1)
        row, col = _tile_masks(t)
        u_incl = (row >= col).astype(BF16)
        u_off = u_incl[:tko, :tko]
        tri = jnp.concatenate([col < row] * 2, axis=0)

        def scores(pr, qs, k0, tk):
            k = k_ref[pl.ds(k0, tk), pr * LANES:(pr + 1) * LANES]
            z = _dot_nt(qs, k)
            nz = -z
            lf = jnp.minimum(nz, 0.0) - jnp.log(1.0 + jnp.exp(jnp.minimum(z, nz)))
            return z, lf

        def accumulate(pr, k0, tk, z, lf, c, acc, diag):
            v = v_ref[pl.ds(k0, tk), pr * LANES:(pr + 1) * LANES]
            if diag:
                lf = jnp.where(tri, lf, 0.0)
            incl = _cumsum_dot(lf, u_incl if diag else u_off) + c
            a = jnp.exp(z + incl)
            if diag:
                a = jnp.where(tri, a, 0.0)
            acc = acc + _dot(a.astype(BF16), v)
            c = c + jnp.sum(lf, axis=1, keepdims=True)
            return c, acc

        def q_body(qi, carry):
            q0 = pl.multiple_of(qi * t, t)
            qss = []
            for pr in range(n_tiles):
                q = q_ref[pl.ds(q0, t), pr * LANES:(pr + 1) * LANES] * scale
                zq = jnp.zeros_like(q)
                qss.append(jnp.concatenate([jnp.where(lane < HALF, q, zq), jnp.where(lane >= HALF, q, zq)], axis=0))

            def step(jj, cs, accs, diag=False):
                k0, tk = (q0, t) if diag else (pl.multiple_of(q0 - jj * tko, tko), tko)
                outs = [accumulate(pr, k0, tk, *scores(pr, qss[pr], k0, tk), cs[pr], accs[pr], diag)
                        for pr in range(n_tiles)]
                return tuple(o[0] for o in outs), tuple(o[1] for o in outs)

            zero = (jnp.zeros((2 * t, 1), F32),) * n_tiles, (jnp.zeros((2 * t, LANES), F32),) * n_tiles
            cs, accs = step(0, zero[0], zero[1], True)
            accs = _sb_sweep(qi * (t // tko), cs, accs, step)
            for pr in range(n_tiles):
                o_ref[pl.ds(q0, t), pr * LANES:(pr + 1) * LANES] = jnp.where(lane < HALF, accs[pr][:t], accs[pr][t:])
            return carry

        lax.fori_loop(0, nq, q_body, 0)

    w = n_tiles * LANES
    return pl.pallas_call(
        body, name=name, grid=(D_MODEL // w,),
        in_specs=[pl.BlockSpec((s, w), lambda p: (0, p)),
                  pl.BlockSpec((s, w), lambda p: (0, D_MODEL // w + p)),
                  pl.BlockSpec((s, w), lambda p: (0, 2 * (D_MODEL // w) + p))],
        out_specs=pl.BlockSpec((s, w), lambda p: (0, p)),
        out_shape=SDS((s, D_MODEL), F32),
        compiler_params=_cparams(("parallel",), vmem_mb=56),
    )(proj, proj, proj)


def sb_attn_bwd(proj, o, do, name):
    s = proj.shape[0]
    n_tiles = SB_BWD_TILES
    t = min(SB_T, s)
    tko = min(SB_TK_LEFT, t)
    nq = s // t
    scale = 1.0 / math.sqrt(HALF)

    def body(q_ref, k_ref, v_ref, o_ref, do_ref, dq_ref, dk_ref, dv_ref, dk_s, dv_s):
        dk_s[...] = jnp.zeros_like(dk_s)
        dv_s[...] = jnp.zeros_like(dv_s)
        lane = lax.broadcasted_iota(jnp.int32, (1, LANES), 1)
        row, col = _tile_masks(t)
        u_incl = (row >= col).astype(BF16)
        u_excl = (row > col).astype(BF16)
        tri = jnp.concatenate([col < row] * 2, axis=0)

        def scores(pr, qs, dos, k0, tk):
            cols = slice(pr * LANES, (pr + 1) * LANES)
            z = _dot_nt(qs, k_ref[pl.ds(k0, tk), cols])
            nz = -z
            lf = jnp.minimum(nz, 0.0) - jnp.log(1.0 + jnp.exp(jnp.minimum(z, nz)))
            return z, lf, _dot_nt(dos, v_ref[pl.ds(k0, tk), cols])

        def accumulate(pr, qs, dos, tsum, k0, tk, z, lf, da, c, r, dq, diag):
            cols = slice(pr * LANES, (pr + 1) * LANES)
            ui, ue = (u_incl, u_excl) if diag else (u_incl[:tko, :tko], u_excl[:tko, :tko])
            sig = jnp.exp(z + lf)
            if diag:
                lf = jnp.where(tri, lf, 0.0)
            incl = _cumsum_dot(lf, ui) + c
            a = jnp.exp(z + incl)
            if diag:
                a = jnp.where(tri, a, 0.0)
            ab = a.astype(BF16)
            dl = ab.astype(F32) * da
            pre = tsum - r - _split_dot(dl, ue)
            dz = dl - sig * pre
            if diag:
                dz = jnp.where(tri, dz, 0.0)
            dzb = dz.astype(BF16)
            dq = dq + _dot(dzb, k_ref[pl.ds(k0, tk), cols])
            dk_s[pl.ds(k0, tk), cols] += _dot_tn(dzb, qs)
            dv_s[pl.ds(k0, tk), cols] += _dot_tn(ab, dos)
            c = c + jnp.sum(lf, axis=1, keepdims=True)
            r = r + jnp.sum(dl, axis=1, keepdims=True)
            return c, r, dq

        def q_body(qi, carry):
            q0 = pl.multiple_of(qi * t, t)
            lo, hi = lane < HALF, lane >= HALF
            qss, doss, tsums = [], [], []
            for pr in range(n_tiles):
                cols = slice(pr * LANES, (pr + 1) * LANES)
                q = q_ref[pl.ds(q0, t), cols] * scale
                dob = do_ref[pl.ds(q0, t), cols]
                prod = dob.astype(F32) * o_ref[pl.ds(q0, t), cols]
                zb = jnp.zeros_like(q)
                qss.append(jnp.concatenate([jnp.where(lo, q, zb), jnp.where(hi, q, zb)], axis=0))
                doss.append(jnp.concatenate([jnp.where(lo, dob, zb), jnp.where(hi, dob, zb)], axis=0))
                tsums.append(jnp.concatenate([jnp.sum(jnp.where(lo, prod, 0.0), axis=1, keepdims=True),
                                              jnp.sum(jnp.where(hi, prod, 0.0), axis=1, keepdims=True)], axis=0))

            def step(jj, cs, rest, diag=False):
                k0, tk = (q0, t) if diag else (pl.multiple_of(q0 - jj * tko, tko), tko)
                outs = [accumulate(pr, qss[pr], doss[pr], tsums[pr], k0, tk,
                                   *scores(pr, qss[pr], doss[pr], k0, tk),
                                   cs[pr], rest[pr], rest[n_tiles + pr], diag) for pr in range(n_tiles)]
                return tuple(o[0] for o in outs), tuple(o[1] for o in outs) + tuple(o[2] for o in outs)

            zc = (jnp.zeros((2 * t, 1), F32),) * n_tiles
            cs, rest = step(0, zc, zc + (jnp.zeros((2 * t, LANES), F32),) * n_tiles, True)
            rest = _sb_sweep(qi * (t // tko), cs, rest, step)
            for pr in range(n_tiles):
                dq = rest[n_tiles + pr]
                dq_ref[pl.ds(q0, t), pr * LANES:(pr + 1) * LANES] = (jnp.where(lo, dq[:t], dq[t:]) * scale).astype(BF16)
            return carry

        lax.fori_loop(0, nq, q_body, 0)
        dk_ref[...] = dk_s[...].astype(BF16)
        dv_ref[...] = dv_s[...].astype(BF16)

    w = n_tiles * LANES
    nblk = D_MODEL // w
    blk = lambda off: pl.BlockSpec((s, w), lambda p: (0, off * nblk + p))
    return pl.pallas_call(
        body, name=name, grid=(nblk,),
        in_specs=[blk(0), blk(1), blk(2), blk(0), blk(0)],
        out_specs=[blk(0), blk(0), blk(0)],
        out_shape=[SDS((s, D_MODEL), BF16)] * 3,
        scratch_shapes=[pltpu.VMEM((s, w), F32), pltpu.VMEM((s, w), F32)],
        compiler_params=_cparams(("parallel",), vmem_mb=56),
    )(proj, proj, proj, o, do)


MLA_HEADS = 8
MLA_QK = 192
MLA_SCALE = 1.0 / math.sqrt(MLA_QK)
MLA_TK = 512


def _rope_tables(s):
    inv_freq = 10000.0 ** (-jnp.arange(32, dtype=F32) / 32)
    ang = jnp.arange(s, dtype=F32)[:, None] * inv_freq[None, :]
    cos = jnp.tile(jnp.cos(ang), (1, 4))
    sin = jnp.sin(ang)
    sin_signed = jnp.tile(jnp.concatenate([-sin, sin], axis=1), (1, 2))
    return cos, sin_signed


def _partner_matrix():
    j = lax.broadcasted_iota(jnp.int32, (LANES, LANES), 0)
    l = lax.broadcasted_iota(jnp.int32, (LANES, LANES), 1)
    return (j == l + jnp.where((l % HALF) < 32, 32, -32)).astype(BF16)


def _partner(u, pmat):
    return _split_dot(u, pmat)


def _half_sum(x):
    j = lax.broadcasted_iota(jnp.int32, (LANES, LANES), 0)
    l = lax.broadcasted_iota(jnp.int32, (LANES, LANES), 1)
    return _dot(x.astype(BF16), ((j < HALF) == (l < HALF)).astype(BF16))


def _lane_sum(x):
    return _dot(x.astype(BF16), jnp.ones((LANES, LANES), BF16))


def _mla_head_inputs(h, qa_ref, kva_ref, kpe, lane):
    a = h % 2
    hm = (lane < HALF) if a == 0 else (lane >= HALF)
    qn = qa_ref[:, h * 128:(h + 1) * 128].astype(F32)
    qr = jnp.where(hm, qa_ref[:, 1024 + (h // 2) * 128:1024 + (h // 2 + 1) * 128].astype(F32), 0.0)
    kn = kva_ref[:, h * 256:h * 256 + 128].astype(F32)
    kr = kpe if a == 0 else pltpu.roll(kpe, HALF, 1)
    return qn, qr, kn, kr


def mla_prep_fwd(qa, kva, proj, gq, gk, cos, sin, name):
    s = qa.shape[0]
    tm = min(256, s)

    def body(qa_ref, kva_ref, kpe_ref, gq_ref, gk_ref, cos_ref, sin_ref, q_out, k_out):
        lane = lax.broadcasted_iota(jnp.int32, (1, LANES), 1)
        pmat = _partner_matrix()
        kpe = kpe_ref[...].astype(F32)
        cs, sn = cos_ref[...], sin_ref[...]
        for h in range(MLA_HEADS):
            qn, qr, kn, kr = _mla_head_inputs(h, qa_ref, kva_ref, kpe, lane)
            for xn, xr, g_ref, out, sc in ((qn, qr, gq_ref, q_out, MLA_SCALE), (kn, kr, gk_ref, k_out, 1.0)):
                ss = _lane_sum(xn * xn + xr * xr)
                rstd = lax.rsqrt(ss * (1.0 / MLA_QK) + NORM_EPS) * sc
                yn = xn * rstd * g_ref[:, 0:128]
                ur = xr * rstd * g_ref[:, 128:256]
                yr = ur * cs + _partner(ur, pmat) * sn
                out[:, h * 256:h * 256 + 128] = yn.astype(BF16)
                out[:, h * 256 + 128:h * 256 + 256] = yr.astype(BF16)

    row = lambda w, c: pl.BlockSpec((tm, w), lambda i: (i, c))
    full = lambda w: pl.BlockSpec((1, w), lambda i: (0, 0))
    return pl.pallas_call(
        body, name=name, grid=(s // tm,),
        in_specs=[row(1536, 0), row(2048, 0), row(128, 11), full(256), full(256), row(128, 0), row(128, 0)],
        out_specs=[row(2048, 0), row(2048, 0)],
        out_shape=[SDS((s, 2048), BF16)] * 2,
        compiler_params=_cparams(("parallel",)),
    )(qa, kva, proj, gq, gk, cos, sin)


def mla_prep_bwd(dq_p, dk_p, dv, qa, kva, proj, gq, gk, cos, sin, name):
    s = qa.shape[0]
    tm = min(256, s)

    def body(dq_ref, dk_ref, dv_ref, qa_ref, kva_ref, kpe_ref, gq_ref, gk_ref, cos_ref, sin_ref,
             dqa_ref, dkva_ref, dkpe_ref, dgq_ref, dgk_ref):
        @pl.when(pl.program_id(0) == 0)
        def _():
            dgq_ref[...] = jnp.zeros_like(dgq_ref)
            dgk_ref[...] = jnp.zeros_like(dgk_ref)

        lane = lax.broadcasted_iota(jnp.int32, (1, LANES), 1)
        pmat = _partner_matrix()
        kpe = kpe_ref[...].astype(F32)
        cs, sn = cos_ref[...], sin_ref[...]
        dkpe = jnp.zeros((tm, LANES), F32)
        dqr_pair = None
        for h in range(MLA_HEADS):
            qn, qr, kn, kr = _mla_head_inputs(h, qa_ref, kva_ref, kpe, lane)
            res = []
            for xn, xr, g_ref, d_ref, dg_ref, sc in ((qn, qr, gq_ref, dq_ref, dgq_ref, MLA_SCALE),
                                                     (kn, kr, gk_ref, dk_ref, dgk_ref, 1.0)):
                ss = _lane_sum(xn * xn + xr * xr)
                rstd = lax.rsqrt(ss * (1.0 / MLA_QK) + NORM_EPS)
                hn, hr = xn * rstd, xr * rstd
                dyn = d_ref[:, h * 256:h * 256 + 128] * sc
                dyr = d_ref[:, h * 256 + 128:h * 256 + 256] * sc
                dur = dyr * cs - _partner(dyr, pmat) * sn
                dg_ref[:, 0:128] += jnp.sum(dyn * hn, axis=0, keepdims=True)
                dg_ref[:, 128:256] += jnp.sum(dur * hr, axis=0, keepdims=True)
                dhn = dyn * g_ref[:, 0:128]
                dhr = dur * g_ref[:, 128:256]
                mu = _lane_sum(dhn * hn + dhr * hr) * (1.0 / MLA_QK)
                res.append((rstd * (dhn - hn * mu), rstd * (dhr - hr * mu)))
            (dqn, dqr), (dkn, dkr) = res
            dqa_ref[:, h * 128:(h + 1) * 128] = dqn.astype(BF16)
            if h % 2 == 0:
                dqr_pair = dqr
            else:
                dqa_ref[:, 1024 + (h // 2) * 128:1024 + (h // 2 + 1) * 128] = (dqr_pair + dqr).astype(BF16)
            dkva_ref[:, h * 256:h * 256 + 128] = dkn.astype(BF16)
            dkva_ref[:, h * 256 + 128:h * 256 + 256] = dv_ref[:, h * 128:(h + 1) * 128]
            dkpe = dkpe + (dkr if h % 2 == 0 else pltpu.roll(dkr, HALF, 1))
        dkpe_ref[...] = dkpe.astype(BF16)

    row = lambda w, c: pl.BlockSpec((tm, w), lambda i: (i, c))
    full = lambda w: pl.BlockSpec((1, w), lambda i: (0, 0))
    return pl.pallas_call(
        body, name=name, grid=(s // tm,),
        in_specs=[row(2048, 0), row(2048, 0), row(1024, 0), row(1536, 0), row(2048, 0), row(128, 11),
                  full(256), full(256), row(128, 0), row(128, 0)],
        out_specs=[row(1536, 0), row(2048, 0), row(128, 0), full(256), full(256)],
        out_shape=[SDS((s, 1536), BF16), SDS((s, 2048), BF16), SDS((s, 128), BF16),
                   SDS((1, 256), F32), SDS((1, 256), F32)],
        compiler_params=_cparams(("arbitrary",)),
    )(dq_p, dk_p, dv, qa, kva, proj, gq, gk, cos, sin)


def mla_attn_fwd(q_p, k_p, kva, name):
    s = q_p.shape[0]
    tk = min(MLA_TK, s)
    tq = min(512, s)
    r = tq // tk
    nq = s // tq

    def body(q_ref, k_ref, va_ref, vb_ref, o_ref, lse_ref):
        row = lax.broadcasted_iota(jnp.int32, (tq, tk), 0)
        col = lax.broadcasted_iota(jnp.int32, (tq, tk), 1)
        v_refs = (va_ref, vb_ref)

        def tile(hh, q, kj, m, l, acc, mask):
            k0 = pl.multiple_of(kj * tk, tk)
            sc = _dot_nt(q, k_ref[pl.ds(k0, tk), hh * 256:(hh + 1) * 256])
            if mask is not None:
                sc = jnp.where(mask, sc, -1e30)
            m_new = jnp.maximum(m, jnp.max(sc, axis=1, keepdims=True))
            alpha = jnp.exp(m - m_new)
            p = jnp.exp(sc - m_new)
            l = alpha * l + jnp.sum(p, axis=1, keepdims=True)
            acc = alpha * acc + _dot(p.astype(BF16), v_refs[hh][pl.ds(k0, tk), :])
            return m_new, l, acc

        def q_body(qi, carry):
            q0 = pl.multiple_of(qi * tq, tq)
            qs = [q_ref[pl.ds(q0, tq), hh * 256:(hh + 1) * 256] for hh in range(2)]

            def step(kj, cr, mask=None):
                out = ()
                for hh in range(2):
                    out += tile(hh, qs[hh], kj, cr[3 * hh], cr[3 * hh + 1], cr[3 * hh + 2], mask)
                return out

            cr = (jnp.full((tq, 1), -1e30, F32), jnp.zeros((tq, 1), F32), jnp.zeros((tq, LANES), F32)) * 2
            cr = lax.fori_loop(0, qi * r, step, cr)
            for d in range(r):
                cr = step(qi * r + d, cr, col + d * tk <= row)
            for hh in range(2):
                m, l, acc = cr[3 * hh:3 * hh + 3]
                o_ref[pl.ds(q0, tq), hh * 128:(hh + 1) * 128] = acc / l
                lse_ref[pl.ds(q0, tq), hh * 128:(hh + 1) * 128] = jnp.broadcast_to(m + jnp.log(l), (tq, LANES))
            return carry

        lax.fori_loop(0, nq, q_body, 0)

    return pl.pallas_call(
        body, name=name, grid=(MLA_HEADS // 2,),
        in_specs=[pl.BlockSpec((s, 512), lambda g: (0, g)),
                  pl.BlockSpec((s, 512), lambda g: (0, g)),
                  pl.BlockSpec((s, 128), lambda g: (0, 4 * g + 1)),
                  pl.BlockSpec((s, 128), lambda g: (0, 4 * g + 3))],
        out_specs=[pl.BlockSpec((s, 256), lambda g: (0, g)),
                   pl.BlockSpec((s, 256), lambda g: (0, g))],
        out_shape=[SDS((s, 1024), F32), SDS((s, 1024), F32)],
        compiler_params=_cparams(("parallel",), vmem_mb=56),
    )(q_p, k_p, kva, kva)


def mla_attn_bwd(q_p, k_p, kva, o, do, lse, name):
    s = q_p.shape[0]
    tk = min(MLA_TK, s)
    tq = min(512, s)
    r = tq // tk
    nq = s // tq

    def body(q_ref, k_ref, v_ref, o_ref, do_ref, lse_ref, dq_ref, dk_ref, dv_ref, dkt_s, dvt_s):
        dkt_s[...] = jnp.zeros_like(dkt_s)
        dvt_s[...] = jnp.zeros_like(dvt_s)
        row = lax.broadcasted_iota(jnp.int32, (tq, tk), 0)
        col = lax.broadcasted_iota(jnp.int32, (tq, tk), 1)

        def tile(q, qt, dob, dot_, lse, delta, kj, dq, mask):
            k0 = pl.multiple_of(kj * tk, tk)
            k = k_ref[pl.ds(k0, tk), :]
            v = v_ref[pl.ds(k0, tk), :]
            p = jnp.exp(_dot_nt(q, k) - lse)
            if mask is not None:
                p = jnp.where(mask, p, 0.0)
            ds = (p * (_dot_nt(dob, v) - delta)).astype(BF16)
            dq = dq + _dot(ds, k)
            dkt_s[:, pl.ds(k0, tk)] += _dot(qt, ds)
            dvt_s[:, pl.ds(k0, tk)] += _dot(dot_, p.astype(BF16))
            return dq

        def q_body(qi, carry):
            q0 = pl.multiple_of(qi * tq, tq)
            q = q_ref[pl.ds(q0, tq), :]
            dob = do_ref[pl.ds(q0, tq), :]
            qt = q.astype(F32).T.astype(BF16)
            dot_ = dob.astype(F32).T.astype(BF16)
            lse = lse_ref[pl.ds(q0, tq), 0:1]
            delta = jnp.sum(dob.astype(F32) * o_ref[pl.ds(q0, tq), :], axis=1, keepdims=True)
            dq = lax.fori_loop(0, qi * r, lambda kj, dq: tile(q, qt, dob, dot_, lse, delta, kj, dq, None),
                               jnp.zeros((tq, 256), F32))
            for d in range(r):
                dq = tile(q, qt, dob, dot_, lse, delta, qi * r + d, dq, col + d * tk <= row)
            dq_ref[pl.ds(q0, tq), :] = dq
            return carry

        lax.fori_loop(0, nq, q_body, 0)
        dk_ref[...] = dkt_s[...].T
        dv_ref[...] = dvt_s[...].T.astype(BF16)

    b256 = pl.BlockSpec((s, 256), lambda h: (0, h))
    b128 = pl.BlockSpec((s, 128), lambda h: (0, h))
    return pl.pallas_call(
        body, name=name, grid=(MLA_HEADS,),
        in_specs=[b256, b256, pl.BlockSpec((s, 128), lambda h: (0, 2 * h + 1)), b128, b128, b128],
        out_specs=[b256, b256, b128],
        out_shape=[SDS((s, 2048), F32), SDS((s, 2048), F32), SDS((s, 1024), BF16)],
        scratch_shapes=[pltpu.VMEM((256, s), F32), pltpu.VMEM((128, s), F32)],
        compiler_params=_cparams(("parallel",), vmem_mb=56),
    )(q_p, k_p, kva, o, do, lse)


SWA_HEADS = 16
SWA_BLOCK = 128


def _swa_head(tile_idx, half):
    return (2 * (tile_idx // 4) + half) * 4 + tile_idx % 4


SWA_Q_PERM = np.concatenate([np.arange(_swa_head(tt, a) * HALF, (_swa_head(tt, a) + 1) * HALF)
                             for tt in range(8) for a in range(2)])
SWA_SLOPES = [2.0 ** (-8.0 * (h + 1) / SWA_HEADS) for h in range(SWA_HEADS)]


def swa_prep_fwd(proj, g128, name):
    s = proj.shape[0]
    tm = min(512, s)

    def body(q_ref, k_ref, g_ref, o_ref):
        lane = lax.broadcasted_iota(jnp.int32, (1, LANES), 1)
        lo = lane < HALF
        for tt in range(10):
            if tt < 8:
                x = q_ref[:, tt * 128:(tt + 1) * 128].astype(F32)
            else:
                x = k_ref[:, (tt - 8) * 128:(tt - 7) * 128].astype(F32)
            rstd = lax.rsqrt(_half_sum(x * x) * (1.0 / HALF) + NORM_EPS)
            gi = 0 if tt < 8 else 1
            o_ref[:, tt * 128:(tt + 1) * 128] = (x * rstd * g_ref[gi:gi + 1, :]).astype(BF16)

    return pl.pallas_call(
        body, name=name, grid=(s // tm,),
        in_specs=[pl.BlockSpec((tm, 1024), lambda i: (i, 0)), pl.BlockSpec((tm, 256), lambda i: (i, 8)),
                  pl.BlockSpec((2, 128), lambda i: (0, 0))],
        out_specs=pl.BlockSpec((tm, 1280), lambda i: (i, 0)),
        out_shape=SDS((s, 1280), BF16),
        compiler_params=_cparams(("parallel",)),
    )(proj, proj, g128)


def swa_prep_bwd(dqn, dkn, proj, g128, name):
    s = proj.shape[0]
    tm = min(512, s)
    nsteps = s // tm

    def body(dq_ref, dk_ref, xq_ref, xk_ref, g_ref, o_ref, dg_ref):
        @pl.when(pl.program_id(0) == 0)
        def _():
            dg_ref[...] = jnp.zeros_like(dg_ref)

        lane = lax.broadcasted_iota(jnp.int32, (1, LANES), 1)
        lo = lane < HALF
        for tt in range(10):
            if tt < 8:
                x = xq_ref[:, tt * 128:(tt + 1) * 128].astype(F32)
                dy = dq_ref[:, tt * 128:(tt + 1) * 128].astype(F32)
            else:
                x = xk_ref[:, (tt - 8) * 128:(tt - 7) * 128].astype(F32)
                dy = dk_ref[:, (tt - 8) * 128:(tt - 7) * 128].astype(F32)
            gi = 0 if tt < 8 else 1
            rstd = lax.rsqrt(_half_sum(x * x) * (1.0 / HALF) + NORM_EPS)
            xh = x * rstd
            dg_ref[gi:gi + 1, :] += jnp.sum(dy * xh, axis=0, keepdims=True)
            dxh = dy * g_ref[gi:gi + 1, :]
            mu = _half_sum(dxh * xh) * (1.0 / HALF)
            o_ref[:, tt * 128:(tt + 1) * 128] = (rstd * (dxh - xh * mu)).astype(BF16)

        @pl.when(pl.program_id(0) == nsteps - 1)
        def _():
            acc = dg_ref[...]
            dg_ref[...] = acc + pltpu.roll(acc, HALF, 1)

    return pl.pallas_call(
        body, name=name, grid=(nsteps,),
        in_specs=[pl.BlockSpec((tm, 1024), lambda i: (i, 0)), pl.BlockSpec((tm, 256), lambda i: (i, 0)),
                  pl.BlockSpec((tm, 1024), lambda i: (i, 0)), pl.BlockSpec((tm, 256), lambda i: (i, 8)),
                  pl.BlockSpec((2, 128), lambda i: (0, 0))],
        out_specs=[pl.BlockSpec((tm, 1280), lambda i: (i, 0)), pl.BlockSpec((2, 128), lambda i: (0, 0))],
        out_shape=[SDS((s, 1280), BF16), SDS((2, 128), F32)],
        compiler_params=_cparams(("arbitrary",)),
    )(dqn, dkn, proj, proj, g128)


SWA_QSCALE = 1.0 / math.sqrt(HALF)


def _swa_bias():
    b = SWA_BLOCK
    row = jnp.arange(b)[:, None]
    col = jnp.arange(2 * b)[None, :]
    rel = row + b - col
    valid = (rel >= 0) & (rel < b)
    slopes = jnp.asarray([SWA_SLOPES[_swa_head(tt, a)] for tt in range(8) for a in range(2)], F32)
    bias = jnp.where(valid[None], -slopes[:, None, None] * rel[None].astype(F32), -1e30)
    return jnp.stack([jnp.where((col >= b)[None], bias, -1e30), bias])


def _swa_tile_inputs(tt, q_ref, bias_ref, sink_ref, lane):
    b = SWA_BLOCK
    qt = q_ref[:, tt * 128:(tt + 1) * 128] * SWA_QSCALE
    zq = jnp.zeros_like(qt)
    qs = jnp.concatenate([jnp.where(lane < HALF, qt, zq), jnp.where(lane >= HALF, qt, zq)], axis=0)
    bias = jnp.concatenate([bias_ref[0, 2 * tt], bias_ref[0, 2 * tt + 1]], axis=0)
    sink = jnp.concatenate([jnp.full((b, 1), sink_ref[_swa_head(tt, a)], F32) for a in range(2)], axis=0)
    return qs, bias, sink


def _swa_softmax(qs, kt, bias, sink):
    sc = _dot_nt(qs, kt) + bias
    m = jnp.maximum(jnp.max(sc, axis=1, keepdims=True), sink)
    e = jnp.exp(sc - m)
    es = jnp.exp(sink - m)
    inv = 1.0 / (jnp.sum(e, axis=1, keepdims=True) + es)
    return e * inv, es * inv


def _swa_specs(b):
    cur = lambda w, c: pl.BlockSpec((b, w), lambda n: (n, c))
    prev = lambda w, c: pl.BlockSpec((b, w), lambda n: (jnp.maximum(n - 1, 0), c))
    bias = pl.BlockSpec((1, SWA_HEADS, b, 2 * b), lambda n: (jnp.minimum(n, 1), 0, 0, 0))
    return cur, prev, bias


def swa_attn_fwd(qkn, proj, sinks, bias, name):
    s = qkn.shape[0]
    b = SWA_BLOCK

    def body(sink_ref, bias_ref, q_ref, kc_ref, kp_ref, vc_ref, vp_ref, o_ref):
        lane = lax.broadcasted_iota(jnp.int32, (1, LANES), 1)
        for tt in range(8):
            gp = tt // 4
            kt = jnp.concatenate([kp_ref[:, gp * 128:(gp + 1) * 128], kc_ref[:, gp * 128:(gp + 1) * 128]], axis=0)
            vt = jnp.concatenate([vp_ref[:, gp * 128:(gp + 1) * 128], vc_ref[:, gp * 128:(gp + 1) * 128]], axis=0)
            qt = q_ref[:, tt * 128:(tt + 1) * 128] * SWA_QSCALE
            outs = []
            for a in range(2):
                qm = jnp.where((lane < HALF) if a == 0 else (lane >= HALF), qt, jnp.zeros_like(qt))
                p, _ = _swa_softmax(qm, kt, bias_ref[0, 2 * tt + a], sink_ref[_swa_head(tt, a)])
                outs.append(_dot(p.astype(BF16), vt))
            o_ref[:, tt * 128:(tt + 1) * 128] = jnp.where(lane < HALF, outs[0], outs[1])

    cur, prev, bias_spec = _swa_specs(b)
    return pl.pallas_call(
        body, name=name, grid=(s // b,),
        in_specs=[pl.BlockSpec(memory_space=pltpu.SMEM), bias_spec, cur(1024, 0), cur(256, 4), prev(256, 4),
                  cur(256, 9), prev(256, 9)],
        out_specs=cur(1024, 0),
        out_shape=SDS((s, 1024), F32),
        compiler_params=_cparams(("parallel",)),
    )(sinks, bias, qkn, qkn, qkn, proj, proj)


def swa_attn_bwd(qkn, proj, sinks, bias, do, name):
    s = qkn.shape[0]
    b = SWA_BLOCK

    def body(sink_ref, bias_ref, q_ref, kc_ref, kp_ref, vc_ref, vp_ref, do_ref, dq_ref, dk_ref, dv_ref, ds_ref):
        n = pl.program_id(0)

        @pl.when(n == 0)
        def _():
            dk_ref[...] = jnp.zeros_like(dk_ref)
            dv_ref[...] = jnp.zeros_like(dv_ref)
            ds_ref[...] = jnp.zeros_like(ds_ref)

        lane = lax.broadcasted_iota(jnp.int32, (1, LANES), 1)
        cur0 = pl.multiple_of(n * b, b)
        prev0 = pl.multiple_of(jnp.maximum(n - 1, 0) * b, b)
        dsink = jnp.zeros((1, LANES), F32)
        for gp in range(2):
            kt = jnp.concatenate([kp_ref[:, gp * 128:(gp + 1) * 128], kc_ref[:, gp * 128:(gp + 1) * 128]], axis=0)
            vt = jnp.concatenate([vp_ref[:, gp * 128:(gp + 1) * 128], vc_ref[:, gp * 128:(gp + 1) * 128]], axis=0)
            dkt = jnp.zeros((2 * b, LANES), F32)
            dvt = jnp.zeros((2 * b, LANES), F32)
            for tt in range(4 * gp, 4 * gp + 4):
                qs, bias2, sink = _swa_tile_inputs(tt, q_ref, bias_ref, sink_ref, lane)
                dot_ = do_ref[:, tt * 128:(tt + 1) * 128]
                zd = jnp.zeros_like(dot_)
                dos = jnp.concatenate([jnp.where(lane < HALF, dot_, zd), jnp.where(lane >= HALF, dot_, zd)], axis=0)
                p, ps = _swa_softmax(qs, kt, bias2, sink)
                dp = _dot_nt(dos, vt)
                delta = jnp.sum(p * dp, axis=1, keepdims=True)
                dz = (p * (dp - delta)).astype(BF16)
                sd = ps * delta
                for a in range(2):
                    dsink = dsink + jnp.where(lane == _swa_head(tt, a), -jnp.sum(sd[a * b:(a + 1) * b]), 0.0)
                dq2 = _dot(dz, kt)
                dq_ref[:, tt * 128:(tt + 1) * 128] = (jnp.where(lane < HALF, dq2[:b], dq2[b:]) * SWA_QSCALE).astype(BF16)
                dkt = dkt + _dot_tn(dz, qs)
                dvt = dvt + _dot_tn(p.astype(BF16), dos)
            cols = slice(gp * 128, (gp + 1) * 128)
            dk_ref[pl.ds(prev0, b), cols] += dkt[:b]
            dk_ref[pl.ds(cur0, b), cols] += dkt[b:]
            dv_ref[pl.ds(prev0, b), cols] += dvt[:b]
            dv_ref[pl.ds(cur0, b), cols] += dvt[b:]
        ds_ref[0:1, :] += dsink

    cur, prev, bias_spec = _swa_specs(b)
    full = pl.BlockSpec((s, 256), lambda n: (0, 0))
    return pl.pallas_call(
        body, name=name, grid=(s // b,),
        in_specs=[pl.BlockSpec(memory_space=pltpu.SMEM), bias_spec, cur(1024, 0), cur(256, 4), prev(256, 4),
                  cur(256, 9), prev(256, 9), cur(1024, 0)],
        out_specs=[cur(1024, 0), full, full, pl.BlockSpec((8, LANES), lambda n: (0, 0))],
        out_shape=[SDS((s, 1024), BF16), SDS((s, 256), F32), SDS((s, 256), F32), SDS((8, LANES), F32)],
        compiler_params=_cparams(("arbitrary",)),
    )(sinks, bias, qkn, qkn, qkn, proj, proj, do)


def _my_pos():
    return lax.axis_index("x"), lax.axis_index("y"), lax.axis_index("c")


ANY = pl.BlockSpec(memory_space=pl.ANY)


def all_gather_packed(xs, name):
    r, n = xs.shape

    def body(x_ref, out_ref, send_sems, recv_sems, local_sem):
        x, y, c = _my_pos()
        me, sibling = (x, y, c), (x, y, 1 - c)
        south = c == 0
        via = (jnp.where(south, 1 - x, x), jnp.where(south, y, 1 - y))
        far = (jnp.where(south, x, 1 - x), jnp.where(south, 1 - y, y))
        rk = jnp.where(south, 1, 2)
        diag = (1 - x, 1 - y)

        def slot(px, py, pc):
            return out_ref.at[4 * px + 2 * py + pc]

        def copy(k, block, to, src=None):
            return pltpu.make_async_remote_copy(
                src_ref=slot(*block) if src is None else src, dst_ref=slot(*block),
                send_sem=send_sems.at[k], recv_sem=recv_sems.at[k],
                device_id=to, device_id_type=MESH_ID)

        mine = pltpu.make_async_copy(x_ref, slot(*me), local_sem)
        mine.start()
        sent = [copy(0, me, sibling, src=x_ref), copy(1, me, (1 - x, y, c), src=x_ref),
                copy(2, me, (x, 1 - y, c), src=x_ref)]
        for cp in sent:
            cp.start()
        copy(rk, (*via, c), me).wait_recv()
        sent += [copy(3, (*via, c), (*far, c)), copy(3 + rk, (*via, c), sibling)]
        sent[-2].start()
        sent[-1].start()
        copy(3 - rk, (*far, c), me).wait_recv()
        sent.append(copy(6 - rk, (*far, c), sibling))
        sent[-1].start()
        copy(3, (*diag, c), me).wait_recv()
        sent.append(copy(6, (*diag, c), sibling))
        sent[-1].start()
        copy(0, sibling, me).wait_recv()
        for k, chip in ((4, (1 - x, y)), (5, (x, 1 - y)), (6, diag)):
            copy(k, (*chip, 1 - c), me).wait_recv()
        for cp in sent:
            cp.wait_send()
        mine.wait()

    return pl.pallas_call(
        body, name=name,
        out_shape=SDS((N_DEV, r, n), xs.dtype),
        in_specs=[ANY], out_specs=ANY,
        scratch_shapes=[pltpu.SemaphoreType.DMA((7,)), pltpu.SemaphoreType.DMA((7,)), pltpu.SemaphoreType.DMA],
    )(xs)


def rs_sibling_exchange(g, name):
    _, r, n = g.shape

    def body(g_ref, ra_ref, send_sems, recv_sems):
        x, y, c = _my_pos()
        copies = [pltpu.make_async_remote_copy(
            src_ref=g_ref.at[2 * chip + (1 - c)], dst_ref=ra_ref.at[chip],
            send_sem=send_sems.at[chip], recv_sem=recv_sems.at[chip],
            device_id=(x, y, 1 - c), device_id_type=MESH_ID) for chip in range(4)]
        for cp in copies:
            cp.start()
        for cp in copies:
            cp.wait()

    return pl.pallas_call(
        body, name=name, out_shape=SDS((4, r, n), g.dtype), in_specs=[ANY], out_specs=ANY,
        scratch_shapes=[pltpu.SemaphoreType.DMA((4,)), pltpu.SemaphoreType.DMA((4,))],
    )(g)


def rs_pair_add(g, ra, cidx, name):
    _, r, n = g.shape
    tr = r // 2

    def body(c_ref, g_ref, ra_ref, p_ref):
        p_ref[...] = (g_ref[...].astype(F32) + ra_ref[...].astype(F32)).astype(p_ref.dtype)

    return pl.pallas_call(
        body, name=name,
        grid_spec=pltpu.PrefetchScalarGridSpec(
            num_scalar_prefetch=1, grid=(4, r // tr),
            in_specs=[pl.BlockSpec((1, tr, n), lambda i, j, c: (2 * i + c[0], j, 0)),
                      pl.BlockSpec((1, tr, n), lambda i, j, c: (i, j, 0))],
            out_specs=pl.BlockSpec((1, tr, n), lambda i, j, c: (i, j, 0))),
        out_shape=SDS((4, r, n), g.dtype),
        compiler_params=_cparams(("parallel", "parallel")),
    )(cidx, g, ra)


def rs_cross_exchange(p, name):
    _, r, n = p.shape

    def body(p_ref, rb_ref, send_sems, recv_sems):
        x, y, c = _my_pos()
        copies = []
        for k, (dx, dy) in enumerate(((1, 0), (0, 1), (1, 1))):
            tx = 1 - x if dx else x
            ty = 1 - y if dy else y
            copies.append(pltpu.make_async_remote_copy(
                src_ref=p_ref.at[2 * tx + ty], dst_ref=rb_ref.at[k],
                send_sem=send_sems.at[k], recv_sem=recv_sems.at[k],
                device_id=(tx, ty, c), device_id_type=MESH_ID))
        for cp in copies:
            cp.start()
        for cp in copies:
            cp.wait()

    return pl.pallas_call(
        body, name=name, out_shape=SDS((3, r, n), p.dtype), in_specs=[ANY], out_specs=ANY,
        scratch_shapes=[pltpu.SemaphoreType.DMA((3,)), pltpu.SemaphoreType.DMA((3,))],
    )(p)


def _adamw(w, g, m, v):
    m = ADAM_B1 * m + (1.0 - ADAM_B1) * g
    v = ADAM_B2 * v + (1.0 - ADAM_B2) * (g * g)
    m_hat = m / (1.0 - ADAM_B1 ** ADAM_STEP)
    v_hat = v / (1.0 - ADAM_B2 ** ADAM_STEP)
    delta = -ADAM_LR * (m_hat / (jnp.sqrt(v_hat) + ADAM_EPS) + ADAM_WD * w)
    return delta, m, v


def rs_final_adamw(p, rb, chipidx, w, m, v, row0, col_blk, name):
    r, n = w.shape
    tr = min(r, 256 if n <= PAIR_COLS else 128)
    assert r % tr == 0 and row0 % tr == 0
    rb0 = row0 // tr

    def body(c_ref, p_ref, rb_ref, w_ref, m_ref, v_ref, g_out, d_out, m_out, v_out):
        g = p_ref[0].astype(F32)
        for k in range(3):
            g = g + rb_ref[k].astype(F32)
        d, mn, vn = _adamw(w_ref[...], g, m_ref[...], v_ref[...])
        g_out[...] = g
        d_out[...] = d
        m_out[...] = mn
        v_out[...] = vn

    blk = pl.BlockSpec((tr, n), lambda j, c: (j, 0))
    return pl.pallas_call(
        body, name=name,
        grid_spec=pltpu.PrefetchScalarGridSpec(
            num_scalar_prefetch=1, grid=(r // tr,),
            in_specs=[pl.BlockSpec((1, tr, n), lambda j, c: (c[0], rb0 + j, col_blk)),
                      pl.BlockSpec((3, tr, n), lambda j, c: (0, rb0 + j, col_blk)), blk, blk, blk],
            out_specs=[blk, blk, blk, blk]),
        out_shape=[SDS((r, n), F32)] * 4,
        compiler_params=_cparams(("parallel",)),
    )(chipidx, p, rb, w, m, v)


def small_allreduce_adamw(g, w, m, v, name):
    rows = g.shape[0]

    def body(g_ref, w_ref, m_ref, v_ref, g_out, d_out, m_out, v_out, buf, send_sems, recv_sems):
        x, y, c = _my_pos()
        my = 4 * x + 2 * y + c
        buf[my] = g_ref[...]
        copies = []
        for k in range(1, N_DEV):
            dx, dy, dc = (k >> 2) & 1, (k >> 1) & 1, k & 1
            tgt = (1 - x if dx else x, 1 - y if dy else y, 1 - c if dc else c)
            copies.append(pltpu.make_async_remote_copy(
                src_ref=buf.at[my], dst_ref=buf.at[my],
                send_sem=send_sems.at[k - 1], recv_sem=recv_sems.at[k - 1],
                device_id=tgt, device_id_type=MESH_ID))
        for cp in copies:
            cp.start()
        for cp in copies:
            cp.wait()
        tot = buf[0]
        for j in range(1, N_DEV):
            tot = tot + buf[j]
        d, mn, vn = _adamw(w_ref[...], tot, m_ref[...], v_ref[...])
        g_out[...] = tot
        d_out[...] = d
        m_out[...] = mn
        v_out[...] = vn

    vm = pl.BlockSpec(memory_space=pltpu.VMEM)
    return pl.pallas_call(
        body, name=name, out_shape=[SDS((rows, LANES), F32)] * 4,
        in_specs=[vm] * 4, out_specs=[vm] * 4,
        scratch_shapes=[pltpu.VMEM((N_DEV, rows, LANES), F32),
                        pltpu.SemaphoreType.DMA((N_DEV - 1,)), pltpu.SemaphoreType.DMA((N_DEV - 1,))],
    )(g, w, m, v)


BIG = ["l0_w_in", "l0_w_out", "l1_w_in", "l1_w_uq", "l1_w_ukv", "l1_w_out", "l2_w_in", "l2_w_out",
       "l3_w_in", "l3_w_out"]
SMALL = ["l0_norm", "l1_norm", "l1_q_a_norm", "l1_kv_a_norm", "l1_q_head_norm", "l1_k_head_norm",
         "l2_norm", "l2_q_head_norm", "l2_k_head_norm", "l2_sinks", "l3_norm"]

MLA_UQ_PERM = np.concatenate([np.arange(h * 192, h * 192 + 128) for h in range(8)]
                             + [np.arange(h * 192 + 128, (h + 1) * 192) for h in range(8)])
SWA_IN_PERM = np.concatenate([SWA_Q_PERM, 1536 + SWA_Q_PERM, np.arange(1024, 1536)])


def _inv(perm):
    inv = np.empty_like(perm)
    inv[perm] = np.arange(perm.size)
    return inv


OUTS = ["l0_w_out", "l1_w_out", "l2_w_out", "l3_w_out"]
REST = ["l1_w_in", "l1_w_uq", "l1_w_ukv", "l2_w_in"]


def _pack_rest(shards):
    parts = [shards[nm].reshape(-1, D_MODEL) for nm in REST]
    used = sum(p.shape[0] for p in parts)
    assert REST_ROW0 + used == ROWS_USED
    parts.append(jnp.zeros((REST_ROWS - used, D_MODEL), parts[0].dtype))
    return jnp.concatenate(parts, axis=0)


def _unpack_rest(packed, shapes):
    out, off = {}, 0
    for nm in REST:
        shp = shapes[nm]
        rows = shp[0] * shp[1] // D_MODEL
        out[nm] = packed[off:off + rows].reshape(shp)
        off += rows
    return out


def _pack(shards):
    pair = jnp.concatenate([shards["l0_w_in"], shards["l3_w_in"]], axis=1)
    return jnp.concatenate([pair] + [shards[nm] for nm in OUTS] + [_pack_rest(shards)], axis=0)


def _out_weight(gathered, layer):
    r0 = OUT_ROW0 + 128 * layer
    return gathered[:, r0:r0 + 128].reshape(D_MODEL, D_MODEL)


def _rest_full_weights(gathered, shapes):
    out, off = {}, REST_ROW0
    for nm in REST:
        shp = shapes[nm]
        rows = shp[0] * shp[1] // D_MODEL
        blk = gathered[:, off:off + rows].reshape((N_DEV,) + shp)
        out[nm] = jnp.transpose(blk, (1, 0, 2)).reshape(shp[0], N_DEV * shp[1])
        off += rows
    return out


def _grad_pieces(pair_l0, pair_l3, dw_out, rest_grads, shapes):
    parts = [jnp.concatenate([pair_l0, pair_l3], axis=2)]
    parts += [g.reshape(N_DEV, 128, D_MODEL).astype(BF16) for g in dw_out]
    used = REST_ROW0
    for nm in REST:
        shp = shapes[nm]
        blk = jnp.transpose(rest_grads[nm].reshape(shp[0], N_DEV, shp[1]), (1, 0, 2))
        parts.append(blk.reshape(N_DEV, -1, D_MODEL).astype(BF16))
        used += parts[-1].shape[1]
    parts.append(jnp.zeros((N_DEV, ROWS_PACK - used, D_MODEL), BF16))
    return jnp.concatenate(parts, axis=1)


SMALL_USED = 5008


def _pack_small(vals, extra=None):
    parts = [vals[nm].reshape(-1).astype(F32) for nm in SMALL]
    assert sum(p.shape[0] for p in parts) == SMALL_USED
    if extra is not None:
        parts.append(extra.reshape(1).astype(F32))
    flat = jnp.concatenate(parts)
    flat = jnp.concatenate([flat, jnp.zeros((SMALL_ROWS * LANES - flat.shape[0],), F32)])
    return flat.reshape(SMALL_ROWS, LANES)


def _unpack_small(packed, shapes):
    flat = packed.reshape(-1)
    out, off = {}, 0
    for nm in SMALL:
        n = shapes[nm][0]
        out[nm] = flat[off:off + n]
        off += n
    return out


def _row(v):
    return v.reshape(1, -1).astype(F32)


def _mla_gain_rows(g):
    return jnp.concatenate([g[:128], g[128:], g[128:]]).reshape(1, 256).astype(F32)


def _layer_fwd_sb(x, norm, gathered, shard_col, w_out, tag):
    proj, xn = norm_matmul(x, 0, _row(norm), gathered, f"{tag}_in", shard_col=shard_col)
    o = sb_attn_fwd(proj, f"{tag}_attn")
    y = gate_out_fwd(o, proj, 3, w_out, x, f"{tag}_out")
    return y, (x, xn, proj, o)


def _layer_bwd_sb(dy, saved, norm, gathered, shard_col, w_out, tag):
    x, xn, proj, o = saved
    do, dgate, hg = gate_out_bwd(dy, w_out, o, proj, 3, f"{tag}_outb")
    dw_out = tn_matmul(hg, dy, f"{tag}_dwout")
    dq, dk, dv = sb_attn_bwd(proj, o, do, f"{tag}_attnb")
    dproj = [dq, dk, dv, dgate]
    dw_in = tn_matmul_shards(xn, dproj, f"{tag}_dwin")
    dx, dnorm = nt_norm_bwd(dproj, gathered, x, 0, _row(norm), dy, f"{tag}_inb", F32, shard_col=shard_col)
    return dx, dw_in, dw_out, dnorm.reshape(-1)


def kernel(x, l0_norm, l0_w_in, l0_w_out, l1_norm, l1_w_in, l1_q_a_norm, l1_w_uq, l1_kv_a_norm, l1_w_ukv, l1_q_head_norm, l1_k_head_norm, l1_w_out, l2_norm, l2_w_in, l2_q_head_norm, l2_k_head_norm, l2_sinks, l2_w_out, l3_norm, l3_w_in, l3_w_out, loss_target, m_l0_norm, m_l0_w_in, m_l0_w_out, m_l1_norm, m_l1_w_in, m_l1_q_a_norm, m_l1_w_uq, m_l1_kv_a_norm, m_l1_w_ukv, m_l1_q_head_norm, m_l1_k_head_norm, m_l1_w_out, m_l2_norm, m_l2_w_in, m_l2_q_head_norm, m_l2_k_head_norm, m_l2_sinks, m_l2_w_out, m_l3_norm, m_l3_w_in, m_l3_w_out, v_l0_norm, v_l0_w_in, v_l0_w_out, v_l1_norm, v_l1_w_in, v_l1_q_a_norm, v_l1_w_uq, v_l1_kv_a_norm, v_l1_w_ukv, v_l1_q_head_norm, v_l1_k_head_norm, v_l1_w_out, v_l2_norm, v_l2_w_in, v_l2_q_head_norm, v_l2_k_head_norm, v_l2_sinks, v_l2_w_out, v_l3_norm, v_l3_w_in, v_l3_w_out):
    loc = dict(locals())
    names = BIG + SMALL
    w = {nm: loc[nm] for nm in names}
    mom = {nm: loc["m_" + nm] for nm in names}
    vel = {nm: loc["v_" + nm] for nm in names}
    shapes = {nm: w[nm].shape for nm in names}
    xs = x[0]
    tgt = loss_target[0]

    gathered = all_gather_packed(_pack({nm: w[nm].astype(BF16) for nm in BIG}), "ag_weights")
    loss_local, dx0, pieces, sg = _local_step(xs, tgt, gathered, {nm: w[nm] for nm in SMALL}, shapes)
    return _reduce_and_update(loss_local, dx0, pieces, sg, w, mom, vel, shapes)


def _local_step(xs, tgt, gathered, sm, shapes):
    s = xs.shape[0]
    fw = _rest_full_weights(gathered, shapes)
    w_out = [_out_weight(gathered, layer) for layer in range(4)]
    l0_norm, l1_norm, l2_norm, l3_norm = sm["l0_norm"], sm["l1_norm"], sm["l2_norm"], sm["l3_norm"]
    l1_q_a_norm, l1_kv_a_norm = sm["l1_q_a_norm"], sm["l1_kv_a_norm"]
    l1_q_head_norm, l1_k_head_norm = sm["l1_q_head_norm"], sm["l1_k_head_norm"]
    l2_q_head_norm, l2_k_head_norm, l2_sinks = sm["l2_q_head_norm"], sm["l2_k_head_norm"], sm["l2_sinks"]
    w1_in = jnp.concatenate([fw["l1_w_in"][:, 448:], fw["l1_w_in"][:, :448], jnp.zeros((D_MODEL, 64), BF16)], axis=1)
    w1_uq = fw["l1_w_uq"][:, MLA_UQ_PERM]
    w2_in = fw["l2_w_in"][:, SWA_IN_PERM]
    w2_out = w_out[2][SWA_Q_PERM, :]

    x1, sv0 = _layer_fwd_sb(xs, l0_norm, gathered, 0, w_out[0], "l0")

    cos, sin = _rope_tables(s)
    gq, gk = _mla_gain_rows(l1_q_head_norm), _mla_gain_rows(l1_k_head_norm)
    proj1, xn1 = norm_matmul(x1, 0, _row(l1_norm), w1_in, "l1_in")
    qa, qln = norm_matmul(proj1, 4, _row(l1_q_a_norm), w1_uq, "l1_uq")
    kva, kvn = norm_matmul(proj1, 10, _row(l1_kv_a_norm), fw["l1_w_ukv"], "l1_ukv")
    q_p, k_p = mla_prep_fwd(qa, kva, proj1, gq, gk, cos, sin, "l1_prep")
    o1, lse1 = mla_attn_fwd(q_p, k_p, kva, "l1_attn")
    x2 = gate_out_fwd(o1, proj1, 0, w_out[1], x1, "l1_out")

    g2 = jnp.stack([jnp.tile(l2_q_head_norm, 2), jnp.tile(l2_k_head_norm, 2)]).astype(F32)
    proj2, xn2 = norm_matmul(x2, 0, _row(l2_norm), w2_in, "l2_in")
    qkn2 = swa_prep_fwd(proj2, g2, "l2_prep")
    bias2 = _swa_bias()
    o2 = swa_attn_fwd(qkn2, proj2, l2_sinks, bias2, "l2_attn")
    x3 = gate_out_fwd(o2, proj2, 1, w2_out, x2, "l2_out")

    x4, sv3 = _layer_fwd_sb(x3, l3_norm, gathered, 1, w_out[3], "l3")

    lossblk, dy = loss_kernel(x4, tgt, "loss")

    fg, sg = {}, {}
    dw_out = [None] * 4
    dx3, pair_l3, dw_out[3], sg["l3_norm"] = _layer_bwd_sb(dy, sv3, l3_norm, gathered, 1, w_out[3], "l3")

    do2, dgate2, hg2 = gate_out_bwd(dx3, w2_out, o2, proj2, 1, "l2_outb")
    dw2_out = tn_matmul(hg2, dx3, "l2_dwout")
    dqn2, dkn2, dv2, dsink2 = swa_attn_bwd(qkn2, proj2, l2_sinks, bias2, do2, "l2_attnb")
    dqk2, dg2 = swa_prep_bwd(dqn2, dkn2, proj2, g2, "l2_prepb")
    dproj2 = jnp.concatenate([dqk2[:, :1024], dgate2, dqk2[:, 1024:], dv2.astype(BF16)], axis=1)
    dw2_in = tn_matmul(xn2, dproj2, "l2_dwin")
    dx2, dn2 = nt_norm_bwd(dproj2, w2_in, x2, 0, _row(l2_norm), dx3, "l2_inb", F32)
    fg["l2_w_in"] = dw2_in[:, _inv(SWA_IN_PERM)]
    dw_out[2] = dw2_out[_inv(SWA_Q_PERM), :]
    sg["l2_norm"] = dn2.reshape(-1)
    sg["l2_q_head_norm"] = dg2[0, :HALF]
    sg["l2_k_head_norm"] = dg2[1, :HALF]
    sg["l2_sinks"] = dsink2[0, :SWA_HEADS]

    do1, dgate1, hg1 = gate_out_bwd(dx2, w_out[1], o1, proj1, 0, "l1_outb")
    dw_out[1] = tn_matmul(hg1, dx2, "l1_dwout")
    dq_p, dk_p, dv1 = mla_attn_bwd(q_p, k_p, kva, o1, do1, lse1, "l1_attnb")
    dqa, dkva, dkpe, dgq, dgk = mla_prep_bwd(dq_p, dk_p, dv1, qa, kva, proj1, gq, gk, cos, sin, "l1_prepb")
    dw_uq = tn_matmul(qln, dqa, "l1_dwuq")
    fg["l1_w_ukv"] = tn_matmul(kvn, dkva, "l1_dwukv")
    dqlat, dgqa = nt_norm_bwd(dqa, w1_uq, proj1, 4, _row(l1_q_a_norm), None, "l1_uqb", BF16)
    dkvlat, dgkva = nt_norm_bwd(dkva, fw["l1_w_ukv"], proj1, 10, _row(l1_kv_a_norm), None, "l1_ukvb", BF16)
    dproj1 = jnp.concatenate([dgate1, dqlat, dkvlat, dkpe], axis=1)
    dw1_in = tn_matmul(xn1, dproj1, "l1_dwin")
    dx1, dn1 = nt_norm_bwd(dproj1, w1_in, x1, 0, _row(l1_norm), dx2, "l1_inb", F32)
    fg["l1_w_in"] = jnp.concatenate([dw1_in[:, 1024:1472], dw1_in[:, :1024]], axis=1)
    fg["l1_w_uq"] = dw_uq[:, _inv(MLA_UQ_PERM)]
    sg["l1_norm"] = dn1.reshape(-1)
    sg["l1_q_a_norm"] = dgqa.reshape(-1)
    sg["l1_kv_a_norm"] = dgkva.reshape(-1)
    sg["l1_q_head_norm"] = jnp.concatenate([dgq[0, :128], dgq[0, 128:192] + dgq[0, 192:256]])
    sg["l1_k_head_norm"] = jnp.concatenate([dgk[0, :128], dgk[0, 128:192] + dgk[0, 192:256]])

    dx0, pair_l0, dw_out[0], sg["l0_norm"] = _layer_bwd_sb(dx1, sv0, l0_norm, gathered, 0, w_out[0], "l0")
    return lossblk[0, 0], dx0, _grad_pieces(pair_l0, pair_l3, dw_out, fg, shapes), sg


def _reduce_and_update(loss_local, dx0, pieces, sg, w, mom, vel, shapes):
    xi, yi, ci = _my_pos()
    cidx = jnp.reshape(ci, (1,)).astype(jnp.int32)
    chipidx = jnp.reshape(2 * xi + yi, (1,)).astype(jnp.int32)
    ra = rs_sibling_exchange(pieces, "rs_sibling")
    part = rs_pair_add(pieces, ra, cidx, "rs_pair_add")
    rb = rs_cross_exchange(part, "rs_cross")

    def update(nm, row0, col_blk):
        return rs_final_adamw(part, rb, chipidx, w[nm], mom[nm], vel[nm], row0, col_blk, f"adamw_{nm}")

    big = {"l0_w_in": update("l0_w_in", 0, 0), "l3_w_in": update("l3_w_in", 0, 1)}
    for layer, nm in enumerate(OUTS):
        big[nm] = update(nm, OUT_ROW0 + 128 * layer, 0)
    rest = rs_final_adamw(part, rb, chipidx, _pack_rest(w), _pack_rest(mom), _pack_rest(vel), REST_ROW0, 0,
                          "adamw_rest")
    gs, ds, ms, vs = small_allreduce_adamw(_pack_small(sg, loss_local), _pack_small({nm: w[nm] for nm in SMALL}),
                                           _pack_small({nm: mom[nm] for nm in SMALL}),
                                           _pack_small({nm: vel[nm] for nm in SMALL}), "small_allreduce")

    outs = []
    for i, small in enumerate((gs, ds, ms, vs)):
        d = {nm: big[nm][i] for nm in big}
        d.update(_unpack_rest(rest[i], shapes))
        d.update(_unpack_small(small, shapes))
        outs.append(d)
    order = ["l0_norm", "l0_w_in", "l0_w_out", "l1_norm", "l1_w_in", "l1_q_a_norm", "l1_w_uq", "l1_kv_a_norm",
             "l1_w_ukv", "l1_q_head_norm", "l1_k_head_norm", "l1_w_out", "l2_norm", "l2_w_in", "l2_q_head_norm",
             "l2_k_head_norm", "l2_sinks", "l2_w_out", "l3_norm", "l3_w_in", "l3_w_out"]
    flat = [gs.reshape(-1)[SMALL_USED], dx0[None]]
    for d in outs:
        flat += [d[nm] for nm in order]
    return tuple(flat)
```

```python
import math

import numpy as np
import jax
import jax.numpy as jnp
from jax import lax
from jax.experimental import pallas as pl
from jax.experimental.pallas import tpu as pltpu

F32 = jnp.float32
BF16 = jnp.bfloat16
SDS = jax.ShapeDtypeStruct
MESH_ID = pl.DeviceIdType.MESH

D_MODEL = 1024
NORM_EPS = 1e-6
N_DEV = 8
LANES = 128
HALF = 64
ROWS_PACK = 2176
ROWS_USED = 2120
PAIR_COLS = 512
OUT_ROW0 = 1024
REST_ROW0 = 1536
REST_ROWS = ROWS_PACK - REST_ROW0
SMALL_ROWS = 40

ADAM_LR = 0.001
ADAM_B1 = 0.9
ADAM_B2 = 0.999
ADAM_EPS = 1e-08
ADAM_WD = 0.01
ADAM_STEP = 10

NT_DIMS = (((1,), (1,)), ((), ()))
TN_DIMS = (((0,), (0,)), ((), ()))


def _cparams(sem=None, vmem_mb=48):
    return pltpu.CompilerParams(dimension_semantics=sem, vmem_limit_bytes=vmem_mb * 2 ** 20)


def _dot(a, b):
    return jnp.dot(a, b, preferred_element_type=F32)


def _dot_nt(a, b):
    return lax.dot_general(a, b, NT_DIMS, preferred_element_type=F32)


def _dot_tn(a, b):
    return lax.dot_general(a, b, TN_DIMS, preferred_element_type=F32)


def _split_dot(a, b):
    hi = a.astype(BF16)
    lo = (a - hi.astype(F32)).astype(BF16)
    return _dot(hi, b) + _dot(lo, b)


def _cumsum_dot(lf, u):
    return _dot(lf.astype(BF16), u)


def _pick_tile(n, cap):
    return max(t for t in range(LANES, min(n, cap) + 1, LANES) if n % t == 0)


def _sigmoid(x):
    return 1.0 / (1.0 + jnp.exp(-x))


def norm_matmul(x, colblk, g, w, name, shard_col=None):
    s = x.shape[0]
    if shard_col is None:
        k, n = w.shape
        tn = _pick_tile(n, 1024)
        w_spec = pl.BlockSpec((k, tn), lambda i, j: (0, j))
    else:
        k, n, tn = D_MODEL, N_DEV * PAIR_COLS, 2 * PAIR_COLS
        w_spec = pl.BlockSpec((2, k, PAIR_COLS), lambda i, j: (j, 0, shard_col))
    tm = min(1024, s)
    assert s % tm == 0

    def body(x_ref, g_ref, w_ref, o_ref, xn_ref, xn_s):
        @pl.when(pl.program_id(1) == 0)
        def _():
            xf = x_ref[...].astype(F32)
            ms = jnp.mean(xf * xf, axis=-1, keepdims=True)
            y = (xf * lax.rsqrt(ms + NORM_EPS) * g_ref[...]).astype(BF16)
            xn_s[...] = y
            xn_ref[...] = y

        if shard_col is None:
            o_ref[...] = _dot(xn_s[...], w_ref[...]).astype(o_ref.dtype)
        else:
            for half in range(2):
                o_ref[:, half * PAIR_COLS:(half + 1) * PAIR_COLS] = _dot(xn_s[...], w_ref[half]).astype(o_ref.dtype)

    return pl.pallas_call(
        body, name=name, grid=(s // tm, n // tn),
        in_specs=[pl.BlockSpec((tm, k), lambda i, j: (i, colblk)),
                  pl.BlockSpec((1, k), lambda i, j: (0, 0)),
                  w_spec],
        out_specs=[pl.BlockSpec((tm, tn), lambda i, j: (i, j)),
                   pl.BlockSpec((tm, k), lambda i, j: (i, 0))],
        out_shape=[SDS((s, n), BF16), SDS((s, k), BF16)],
        scratch_shapes=[pltpu.VMEM((tm, k), BF16)],
        compiler_params=_cparams(("parallel", "arbitrary")),
    )(x, g, w)


def gate_out_fwd(o, proj, gate_blk, w_out, x, name):
    s = x.shape[0]
    d = D_MODEL
    tm = min(512, s)

    def body(o_ref, g_ref, w_ref, x_ref, y_ref):
        g = g_ref[...].astype(F32)
        hg = (o_ref[...].astype(F32) * (g * _sigmoid(g))).astype(BF16)
        y_ref[...] = x_ref[...] + _dot(hg, w_ref[...])

    return pl.pallas_call(
        body, name=name, grid=(s // tm,),
        in_specs=[pl.BlockSpec((tm, d), lambda i: (i, 0)),
                  pl.BlockSpec((tm, d), lambda i: (i, gate_blk)),
                  pl.BlockSpec((d, d), lambda i: (0, 0)),
                  pl.BlockSpec((tm, d), lambda i: (i, 0))],
        out_specs=pl.BlockSpec((tm, d), lambda i: (i, 0)),
        out_shape=SDS((s, d), F32),
        compiler_params=_cparams(("parallel",)),
    )(o, proj, w_out, x)


def gate_out_bwd(dy, w_out, o, proj, gate_blk, name):
    s = dy.shape[0]
    d = D_MODEL
    tm = min(512, s)

    def body(dy_ref, w_ref, o_ref, g_ref, do_ref, dg_ref, hg_ref):
        dh = _dot_nt(dy_ref[...].astype(BF16), w_ref[...])
        g = g_ref[...].astype(F32)
        o = o_ref[...].astype(F32)
        sg = _sigmoid(g)
        silu = g * sg
        do_ref[...] = (dh * silu).astype(BF16)
        dg_ref[...] = (dh * o * (sg * (1.0 + g * (1.0 - sg)))).astype(BF16)
        hg_ref[...] = (o * silu).astype(BF16)

    blk = pl.BlockSpec((tm, d), lambda i: (i, 0))
    return pl.pallas_call(
        body, name=name, grid=(s // tm,),
        in_specs=[blk, pl.BlockSpec((d, d), lambda i: (0, 0)), blk,
                  pl.BlockSpec((tm, d), lambda i: (i, gate_blk))],
        out_specs=[blk, blk, blk],
        out_shape=[SDS((s, d), BF16)] * 3,
        compiler_params=_cparams(("parallel",)),
    )(dy, w_out, o, proj)


def tn_matmul(a, b, name):
    s, k = a.shape
    n = b.shape[1]
    tk = min(1024, k)
    tn = _pick_tile(n, 512)
    ts = min(2048, s)
    assert k % tk == 0 and s % ts == 0

    def body(a_ref, b_ref, o_ref):
        @pl.when(pl.program_id(2) == 0)
        def _():
            o_ref[...] = jnp.zeros_like(o_ref)

        o_ref[...] += _dot_tn(a_ref[...].astype(BF16), b_ref[...].astype(BF16))

    return pl.pallas_call(
        body, name=name, grid=(k // tk, n // tn, s // ts),
        in_specs=[pl.BlockSpec((ts, tk), lambda i, j, r: (r, i)),
                  pl.BlockSpec((ts, tn), lambda i, j, r: (r, j))],
        out_specs=pl.BlockSpec((tk, tn), lambda i, j, r: (i, j)),
        out_shape=SDS((k, n), F32),
        compiler_params=_cparams(("parallel", "parallel", "arbitrary")),
    )(a, b)


def tn_matmul_shards(a, b_parts, name):
    s, k = a.shape
    per = D_MODEL // PAIR_COLS
    assert len(b_parts) * per == N_DEV and a.dtype == BF16
    assert all(b.shape == (s, D_MODEL) and b.dtype == BF16 for b in b_parts)

    def body(a_ref, *refs):
        o_ref = refs[-1]
        j = pl.program_id(0)
        for p, b_ref in enumerate(refs[:-1]):
            @pl.when(j // per == p)
            def _(b_ref=b_ref):
                o_ref[...] = _dot_tn(a_ref[...], b_ref[...]).astype(BF16)

    part_specs = [pl.BlockSpec((s, PAIR_COLS), lambda j, p=p: (0, jnp.clip(j - per * p, 0, per - 1)))
                  for p in range(len(b_parts))]
    return pl.pallas_call(
        body, name=name, grid=(N_DEV,),
        in_specs=[pl.BlockSpec((s, k), lambda j: (0, 0))] + part_specs,
        out_specs=pl.BlockSpec((None, k, PAIR_COLS), lambda j: (j, 0, 0)),
        out_shape=SDS((N_DEV, k, PAIR_COLS), BF16),
        compiler_params=_cparams(("arbitrary",)),
    )(a, *b_parts)


def nt_norm_bwd(dp, w, x, xcol, g, res, name, out_dtype, shard_col=None):
    tm = min(512, dp[0].shape[0] if shard_col is not None else dp.shape[0])
    if shard_col is None:
        s, n = dp.shape
        k = w.shape[0]
        w_spec = pl.BlockSpec((k, n), lambda i: (0, 0))
        dp_parts = [dp]
        dp_specs = [pl.BlockSpec((tm, n), lambda i: (i, 0))]
    else:
        s, k = dp[0].shape[0], D_MODEL
        w_spec = pl.BlockSpec((N_DEV, k, PAIR_COLS), lambda i: (0, 0, shard_col))
        dp_parts = list(dp)
        dp_specs = [pl.BlockSpec((tm, D_MODEL), lambda i: (i, 0)) for _ in dp_parts]
    np_ = len(dp_parts)
    per = D_MODEL // PAIR_COLS
    assert s % tm == 0
    has_res = res is not None

    def body(*refs):
        dp_refs = refs[:np_]
        if has_res:
            w_ref, x_ref, g_ref, r_ref, dx_ref, dg_ref = refs[np_:]
        else:
            w_ref, x_ref, g_ref, dx_ref, dg_ref = refs[np_:]

        @pl.when(pl.program_id(0) == 0)
        def _():
            dg_ref[...] = jnp.zeros_like(dg_ref)

        if shard_col is None:
            dxn = _dot_nt(dp_refs[0][...], w_ref[...])
        else:
            dxn = None
            for j in range(N_DEV):
                c0 = (j % per) * PAIR_COLS
                term = _dot_nt(dp_refs[j // per][:, c0:c0 + PAIR_COLS], w_ref[j])
                dxn = term if dxn is None else dxn + term
        xf = x_ref[...].astype(F32)
        rstd = lax.rsqrt(jnp.mean(xf * xf, axis=-1, keepdims=True) + NORM_EPS)
        xhat = xf * rstd
        dg_ref[...] += jnp.sum(dxn * xhat, axis=0, keepdims=True)
        dxh = dxn * g_ref[...]
        dx = rstd * (dxh - xhat * jnp.mean(dxh * xhat, axis=-1, keepdims=True))
        if has_res:
            dx = dx + r_ref[...]
        dx_ref[...] = dx.astype(out_dtype)

    in_specs = dp_specs + [w_spec,
                           pl.BlockSpec((tm, k), lambda i: (i, xcol)),
                           pl.BlockSpec((1, k), lambda i: (0, 0))]
    args = dp_parts + [w, x, g]
    if has_res:
        in_specs.append(pl.BlockSpec((tm, k), lambda i: (i, 0)))
        args.append(res)
    return pl.pallas_call(
        body, name=name, grid=(s // tm,),
        in_specs=in_specs,
        out_specs=[pl.BlockSpec((tm, k), lambda i: (i, 0)),
                   pl.BlockSpec((1, k), lambda i: (0, 0))],
        out_shape=[SDS((s, k), out_dtype), SDS((1, k), F32)],
        compiler_params=_cparams(("arbitrary",)),
    )(*args)


def loss_kernel(y, target, name):
    s, d = y.shape
    tm = min(512, s)

    def body(y_ref, t_ref, l_ref, dy_ref):
        @pl.when(pl.program_id(0) == 0)
        def _():
            l_ref[...] = jnp.zeros_like(l_ref)

        e = y_ref[...] - t_ref[...]
        dy_ref[...] = e * (1.0 / d)
        l_ref[...] += 0.5 * jnp.sum(jnp.mean(e * e, axis=-1, keepdims=True))

    blk = pl.BlockSpec((tm, d), lambda i: (i, 0))
    return pl.pallas_call(
        body, name=name, grid=(s // tm,),
        in_specs=[blk, blk],
        out_specs=[pl.BlockSpec((8, LANES), lambda i: (0, 0)), blk],
        out_shape=[SDS((8, LANES), F32), SDS((s, d), F32)],
        compiler_params=_cparams(("arbitrary",)),
    )(y, target)


SB_LOG_CUTOFF = -80.0
SB_T = 256
SB_TK_LEFT = 128
SB_FWD_TILES = 4
SB_BWD_TILES = 2


def _sb_sweep(n_left, cs, rest, step):
    n = len(cs)

    def top(arrs):
        m = jnp.max(arrs[0])
        for a in arrs[1:]:
            m = jnp.maximum(m, jnp.max(a))
        return m

    def cond(st):
        return (st[0] <= n_left) & (st[1] > SB_LOG_CUTOFF)

    def body(st):
        cs2, rest2 = step(st[0], st[2:2 + n], st[2 + n:])
        return (st[0] + 1, top(cs2)) + tuple(cs2) + tuple(rest2)

    return lax.while_loop(cond, body, (jnp.int32(1), top(cs)) + tuple(cs) + tuple(rest))[2 + n:]


def _tile_masks(t):
    row = lax.broadcasted_iota(jnp.int32, (t, t), 0)
    col = lax.broadcasted_iota(jnp.int32, (t, t), 1)
    return row, col


def sb_attn_fwd(proj, name):
    s = proj.shape[0]
    n_tiles = SB_FWD_TILES
    t = min(SB_T, s)
    tko = min(SB_TK_LEFT, t)
    nq = s // t
    scale = 1.0 / math.sqrt(HALF)

    def body(q_ref, k_ref, v_ref, o_ref):
        lane = lax.broadcasted_iota(jnp.int32, (1, LANES), 1)
        row, col = _tile_masks(t)
        u_incl = (row >= col).astype(BF16)
        u_off = u_incl[:tko, :tko]
        tri = jnp.concatenate([col < row] * 2, axis=0)

        def scores(pr, qs, k0, tk):
            k = k_ref[pl.ds(k0, tk), pr * LANES:(pr + 1) * LANES]
            z = _dot_nt(qs, k)
            nz = -z
            lf = jnp.minimum(nz, 0.0) - jnp.log(1.0 + jnp.exp(jnp.minimum(z, nz)))
            return z, lf

        def accumulate(pr, k0, tk, z, lf, c, acc, diag):
            v = v_ref[pl.ds(k0, tk), pr * LANES:(pr + 1) * LANES]
            if diag:
                lf = jnp.where(tri, lf, 0.0)
            incl = _cumsum_dot(lf, u_incl if diag else u_off) + c
            a = jnp.exp(z + incl)
            if diag:
                a = jnp.where(tri, a, 0.0)
            acc = acc + _dot(a.astype(BF16), v)
            c = c + jnp.sum(lf, axis=1, keepdims=True)
            return c, acc

        def q_body(qi, carry):
            q0 = pl.multiple_of(qi * t, t)
            qss = []
            for pr in range(n_tiles):
                q = q_ref[pl.ds(q0, t), pr * LANES:(pr + 1) * LANES] * scale
                zq = jnp.zeros_like(q)
                qss.append(jnp.concatenate([jnp.where(lane < HALF, q, zq), jnp.where(lane >= HALF, q, zq)], axis=0))

            def step(jj, cs, accs, diag=False):
                k0, tk = (q0, t) if diag else (pl.multiple_of(q0 - jj * tko, tko), tko)
                outs = [accumulate(pr, k0, tk, *scores(pr, qss[pr], k0, tk), cs[pr], accs[pr], diag)
                        for pr in range(n_tiles)]
                return tuple(o[0] for o in outs), tuple(o[1] for o in outs)

            zero = (jnp.zeros((2 * t, 1), F32),) * n_tiles, (jnp.zeros((2 * t, LANES), F32),) * n_tiles
            cs, accs = step(0, zero[0], zero[1], True)
            accs = _sb_sweep(qi * (t // tko), cs, accs, step)
            for pr in range(n_tiles):
                o_ref[pl.ds(q0, t), pr * LANES:(pr + 1) * LANES] = jnp.where(lane < HALF, accs[pr][:t], accs[pr][t:])
            return carry

        lax.fori_loop(0, nq, q_body, 0)

    w = n_tiles * LANES
    return pl.pallas_call(
        body, name=name, grid=(D_MODEL // w,),
        in_specs=[pl.BlockSpec((s, w), lambda p: (0, p)),
                  pl.BlockSpec((s, w), lambda p: (0, D_MODEL // w + p)),
                  pl.BlockSpec((s, w), lambda p: (0, 2 * (D_MODEL // w) + p))],
        out_specs=pl.BlockSpec((s, w), lambda p: (0, p)),
        out_shape=SDS((s, D_MODEL), F32),
        compiler_params=_cparams(("parallel",), vmem_mb=56),
    )(proj, proj, proj)


def sb_attn_bwd(proj, o, do, name):
    s = proj.shape[0]
    n_tiles = SB_BWD_TILES
    t = min(SB_T, s)
    tko = min(SB_TK_LEFT, t)
    nq = s // t
    scale = 1.0 / math.sqrt(HALF)

    def body(q_ref, k_ref, v_ref, o_ref, do_ref, dq_ref, dk_ref, dv_ref, dk_s, dv_s):
        dk_s[...] = jnp.zeros_like(dk_s)
        dv_s[...] = jnp.zeros_like(dv_s)
        lane = lax.broadcasted_iota(jnp.int32, (1, LANES), 1)
        row, col = _tile_masks(t)
        u_incl = (row >= col).astype(BF16)
        u_excl = (row > col).astype(BF16)
        tri = jnp.concatenate([col < row] * 2, axis=0)

        def scores(pr, qs, dos, k0, tk):
            cols = slice(pr * LANES, (pr + 1) * LANES)
            z = _dot_nt(qs, k_ref[pl.ds(k0, tk), cols])
            nz = -z
            lf = jnp.minimum(nz, 0.0) - jnp.log(1.0 + jnp.exp(jnp.minimum(z, nz)))
            return z, lf, _dot_nt(dos, v_ref[pl.ds(k0, tk), cols])

        def accumulate(pr, qs, dos, tsum, k0, tk, z, lf, da, c, r, dq, diag):
            cols = slice(pr * LANES, (pr + 1) * LANES)
            ui, ue = (u_incl, u_excl) if diag else (u_incl[:tko, :tko], u_excl[:tko, :tko])
            sig = jnp.exp(z + lf)
            if diag:
                lf = jnp.where(tri, lf, 0.0)
            incl = _cumsum_dot(lf, ui) + c
            a = jnp.exp(z + incl)
            if diag:
                a = jnp.where(tri, a, 0.0)
            ab = a.astype(BF16)
            dl = ab.astype(F32) * da
            pre = tsum - r - _split_dot(dl, ue)
            dz = dl - sig * pre
            if diag:
                dz = jnp.where(tri, dz, 0.0)
            dzb = dz.astype(BF16)
            dq = dq + _dot(dzb, k_ref[pl.ds(k0, tk), cols])
            dk_s[pl.ds(k0, tk), cols] += _dot_tn(dzb, qs)
            dv_s[pl.ds(k0, tk), cols] += _dot_tn(ab, dos)
            c = c + jnp.sum(lf, axis=1, keepdims=True)
            r = r + jnp.sum(dl, axis=1, keepdims=True)
            return c, r, dq

        def q_body(qi, carry):
            q0 = pl.multiple_of(qi * t, t)
            lo, hi = lane < HALF, lane >= HALF
            qss, doss, tsums = [], [], []
            for pr in range(n_tiles):
                cols = slice(pr * LANES, (pr + 1) * LANES)
                q = q_ref[pl.ds(q0, t), cols] * scale
                dob = do_ref[pl.ds(q0, t), cols]
                prod = dob.astype(F32) * o_ref[pl.ds(q0, t), cols]
                zb = jnp.zeros_like(q)
                qss.append(jnp.concatenate([jnp.where(lo, q, zb), jnp.where(hi, q, zb)], axis=0))
                doss.append(jnp.concatenate([jnp.where(lo, dob, zb), jnp.where(hi, dob, zb)], axis=0))
                tsums.append(jnp.concatenate([jnp.sum(jnp.where(lo, prod, 0.0), axis=1, keepdims=True),
                                              jnp.sum(jnp.where(hi, prod, 0.0), axis=1, keepdims=True)], axis=0))

            def step(jj, cs, rest, diag=False):
                k0, tk = (q0, t) if diag else (pl.multiple_of(q0 - jj * tko, tko), tko)
                outs = [accumulate(pr, qss[pr], doss[pr], tsums[pr], k0, tk,
                                   *scores(pr, qss[pr], doss[pr], k0, tk),
                                   cs[pr], rest[pr], rest[n_tiles + pr], diag) for pr in range(n_tiles)]
                return tuple(o[0] for o in outs), tuple(o[1] for o in outs) + tuple(o[2] for o in outs)

            zc = (jnp.zeros((2 * t, 1), F32),) * n_tiles
            cs, rest = step(0, zc, zc + (jnp.zeros((2 * t, LANES), F32),) * n_tiles, True)
            rest = _sb_sweep(qi * (t // tko), cs, rest, step)
            for pr in range(n_tiles):
                dq = rest[n_tiles + pr]
                dq_ref[pl.ds(q0, t), pr * LANES:(pr + 1) * LANES] = (jnp.where(lo, dq[:t], dq[t:]) * scale).astype(BF16)
            return carry

        lax.fori_loop(0, nq, q_body, 0)
        dk_ref[...] = dk_s[...].astype(BF16)
        dv_ref[...] = dv_s[...].astype(BF16)

    w = n_tiles * LANES
    nblk = D_MODEL // w
    blk = lambda off: pl.BlockSpec((s, w), lambda p: (0, off * nblk + p))
    return pl.pallas_call(
        body, name=name, grid=(nblk,),
        in_specs=[blk(0), blk(1), blk(2), blk(0), blk(0)],
        out_specs=[blk(0), blk(0), blk(0)],
        out_shape=[SDS((s, D_MODEL), BF16)] * 3,
        scratch_shapes=[pltpu.VMEM((s, w), F32), pltpu.VMEM((s, w), F32)],
        compiler_params=_cparams(("parallel",), vmem_mb=56),
    )(proj, proj, proj, o, do)


MLA_HEADS = 8
MLA_QK = 192
MLA_SCALE = 1.0 / math.sqrt(MLA_QK)
MLA_TK = 512


def _rope_tables(s):
    inv_freq = 10000.0 ** (-jnp.arange(32, dtype=F32) / 32)
    ang = jnp.arange(s, dtype=F32)[:, None] * inv_freq[None, :]
    cos = jnp.tile(jnp.cos(ang), (1, 4))
    sin = jnp.sin(ang)
    sin_signed = jnp.tile(jnp.concatenate([-sin, sin], axis=1), (1, 2))
    return cos, sin_signed


def _partner_matrix():
    j = lax.broadcasted_iota(jnp.int32, (LANES, LANES), 0)
    l = lax.broadcasted_iota(jnp.int32, (LANES, LANES), 1)
    return (j == l + jnp.where((l % HALF) < 32, 32, -32)).astype(BF16)


def _partner(u, pmat):
    return _split_dot(u, pmat)


def _half_sum(x):
    j = lax.broadcasted_iota(jnp.int32, (LANES, LANES), 0)
    l = lax.broadcasted_iota(jnp.int32, (LANES, LANES), 1)
    return _dot(x.astype(BF16), ((j < HALF) == (l < HALF)).astype(BF16))


def _lane_sum(x):
    return _dot(x.astype(BF16), jnp.ones((LANES, LANES), BF16))


def _mla_head_inputs(h, qa_ref, kva_ref, kpe, lane):
    a = h % 2
    hm = (lane < HALF) if a == 0 else (lane >= HALF)
    qn = qa_ref[:, h * 128:(h + 1) * 128].astype(F32)
    qr = jnp.where(hm, qa_ref[:, 1024 + (h // 2) * 128:1024 + (h // 2 + 1) * 128].astype(F32), 0.0)
    kn = kva_ref[:, h * 256:h * 256 + 128].astype(F32)
    kr = kpe if a == 0 else pltpu.roll(kpe, HALF, 1)
    return qn, qr, kn, kr


def mla_prep_fwd(qa, kva, proj, gq, gk, cos, sin, name):
    s = qa.shape[0]
    tm = min(256, s)

    def body(qa_ref, kva_ref, kpe_ref, gq_ref, gk_ref, cos_ref, sin_ref, q_out, k_out):
        lane = lax.broadcasted_iota(jnp.int32, (1, LANES), 1)
        pmat = _partner_matrix()
        kpe = kpe_ref[...].astype(F32)
        cs, sn = cos_ref[...], sin_ref[...]
        for h in range(MLA_HEADS):
            qn, qr, kn, kr = _mla_head_inputs(h, qa_ref, kva_ref, kpe, lane)
            for xn, xr, g_ref, out, sc in ((qn, qr, gq_ref, q_out, MLA_SCALE), (kn, kr, gk_ref, k_out, 1.0)):
                ss = _lane_sum(xn * xn + xr * xr)
                rstd = lax.rsqrt(ss * (1.0 / MLA_QK) + NORM_EPS) * sc
                yn = xn * rstd * g_ref[:, 0:128]
                ur = xr * rstd * g_ref[:, 128:256]
                yr = ur * cs + _partner(ur, pmat) * sn
                out[:, h * 256:h * 256 + 128] = yn.astype(BF16)
                out[:, h * 256 + 128:h * 256 + 256] = yr.astype(BF16)

    row = lambda w, c: pl.BlockSpec((tm, w), lambda i: (i, c))
    full = lambda w: pl.BlockSpec((1, w), lambda i: (0, 0))
    return pl.pallas_call(
        body, name=name, grid=(s // tm,),
        in_specs=[row(1536, 0), row(2048, 0), row(128, 11), full(256), full(256), row(128, 0), row(128, 0)],
        out_specs=[row(2048, 0), row(2048, 0)],
        out_shape=[SDS((s, 2048), BF16)] * 2,
        compiler_params=_cparams(("parallel",)),
    )(qa, kva, proj, gq, gk, cos, sin)


def mla_prep_bwd(dq_p, dk_p, dv, qa, kva, proj, gq, gk, cos, sin, name):
    s = qa.shape[0]
    tm = min(256, s)

    def body(dq_ref, dk_ref, dv_ref, qa_ref, kva_ref, kpe_ref, gq_ref, gk_ref, cos_ref, sin_ref,
             dqa_ref, dkva_ref, dkpe_ref, dgq_ref, dgk_ref):
        @pl.when(pl.program_id(0) == 0)
        def _():
            dgq_ref[...] = jnp.zeros_like(dgq_ref)
            dgk_ref[...] = jnp.zeros_like(dgk_ref)

        lane = lax.broadcasted_iota(jnp.int32, (1, LANES), 1)
        pmat = _partner_matrix()
        kpe = kpe_ref[...].astype(F32)
        cs, sn = cos_ref[...], sin_ref[...]
        dkpe = jnp.zeros((tm, LANES), F32)
        dqr_pair = None
        for h in range(MLA_HEADS):
            qn, qr, kn, kr = _mla_head_inputs(h, qa_ref, kva_ref, kpe, lane)
            res = []
            for xn, xr, g_ref, d_ref, dg_ref, sc in ((qn, qr, gq_ref, dq_ref, dgq_ref, MLA_SCALE),
                                                     (kn, kr, gk_ref, dk_ref, dgk_ref, 1.0)):
                ss = _lane_sum(xn * xn + xr * xr)
                rstd = lax.rsqrt(ss * (1.0 / MLA_QK) + NORM_EPS)
                hn, hr = xn * rstd, xr * rstd
                dyn = d_ref[:, h * 256:h * 256 + 128] * sc
                dyr = d_ref[:, h * 256 + 128:h * 256 + 256] * sc
                dur = dyr * cs - _partner(dyr, pmat) * sn
                dg_ref[:, 0:128] += jnp.sum(dyn * hn, axis=0, keepdims=True)
                dg_ref[:, 128:256] += jnp.sum(dur * hr, axis=0, keepdims=True)
                dhn = dyn * g_ref[:, 0:128]
                dhr = dur * g_ref[:, 128:256]
                mu = _lane_sum(dhn * hn + dhr * hr) * (1.0 / MLA_QK)
                res.append((rstd * (dhn - hn * mu), rstd * (dhr - hr * mu)))
            (dqn, dqr), (dkn, dkr) = res
            dqa_ref[:, h * 128:(h + 1) * 128] = dqn.astype(BF16)
            if h % 2 == 0:
                dqr_pair = dqr
            else:
                dqa_ref[:, 1024 + (h // 2) * 128:1024 + (h // 2 + 1) * 128] = (dqr_pair + dqr).astype(BF16)
            dkva_ref[:, h * 256:h * 256 + 128] = dkn.astype(BF16)
            dkva_ref[:, h * 256 + 128:h * 256 + 256] = dv_ref[:, h * 128:(h + 1) * 128]
            dkpe = dkpe + (dkr if h % 2 == 0 else pltpu.roll(dkr, HALF, 1))
        dkpe_ref[...] = dkpe.astype(BF16)

    row = lambda w, c: pl.BlockSpec((tm, w), lambda i: (i, c))
    full = lambda w: pl.BlockSpec((1, w), lambda i: (0, 0))
    return pl.pallas_call(
        body, name=name, grid=(s // tm,),
        in_specs=[row(2048, 0), row(2048, 0), row(1024, 0), row(1536, 0), row(2048, 0), row(128, 11),
                  full(256), full(256), row(128, 0), row(128, 0)],
        out_specs=[row(1536, 0), row(2048, 0), row(128, 0), full(256), full(256)],
        out_shape=[SDS((s, 1536), BF16), SDS((s, 2048), BF16), SDS((s, 128), BF16),
                   SDS((1, 256), F32), SDS((1, 256), F32)],
        compiler_params=_cparams(("arbitrary",)),
    )(dq_p, dk_p, dv, qa, kva, proj, gq, gk, cos, sin)


def mla_attn_fwd(q_p, k_p, kva, name):
    s = q_p.shape[0]
    tk = min(MLA_TK, s)
    tq = min(512, s)
    r = tq // tk
    nq = s // tq

    def body(q_ref, k_ref, va_ref, vb_ref, o_ref, lse_ref):
        row = lax.broadcasted_iota(jnp.int32, (tq, tk), 0)
        col = lax.broadcasted_iota(jnp.int32, (tq, tk), 1)
        v_refs = (va_ref, vb_ref)

        def tile(hh, q, kj, m, l, acc, mask):
            k0 = pl.multiple_of(kj * tk, tk)
            sc = _dot_nt(q, k_ref[pl.ds(k0, tk), hh * 256:(hh + 1) * 256])
            if mask is not None:
                sc = jnp.where(mask, sc, -1e30)
            m_new = jnp.maximum(m, jnp.max(sc, axis=1, keepdims=True))
            alpha = jnp.exp(m - m_new)
            p = jnp.exp(sc - m_new)
            l = alpha * l + jnp.sum(p, axis=1, keepdims=True)
            acc = alpha * acc + _dot(p.astype(BF16), v_refs[hh][pl.ds(k0, tk), :])
            return m_new, l, acc

        def q_body(qi, carry):
            q0 = pl.multiple_of(qi * tq, tq)
            qs = [q_ref[pl.ds(q0, tq), hh * 256:(hh + 1) * 256] for hh in range(2)]

            def step(kj, cr, mask=None):
                out = ()
                for hh in range(2):
                    out += tile(hh, qs[hh], kj, cr[3 * hh], cr[3 * hh + 1], cr[3 * hh + 2], mask)
                return out

            cr = (jnp.full((tq, 1), -1e30, F32), jnp.zeros((tq, 1), F32), jnp.zeros((tq, LANES), F32)) * 2
            cr = lax.fori_loop(0, qi * r, step, cr)
            for d in range(r):
                cr = step(qi * r + d, cr, col + d * tk <= row)
            for hh in range(2):
                m, l, acc = cr[3 * hh:3 * hh + 3]
                o_ref[pl.ds(q0, tq), hh * 128:(hh + 1) * 128] = acc / l
                lse_ref[pl.ds(q0, tq), hh * 128:(hh + 1) * 128] = jnp.broadcast_to(m + jnp.log(l), (tq, LANES))
            return carry

        lax.fori_loop(0, nq, q_body, 0)

    return pl.pallas_call(
        body, name=name, grid=(MLA_HEADS // 2,),
        in_specs=[pl.BlockSpec((s, 512), lambda g: (0, g)),
                  pl.BlockSpec((s, 512), lambda g: (0, g)),
                  pl.BlockSpec((s, 128), lambda g: (0, 4 * g + 1)),
                  pl.BlockSpec((s, 128), lambda g: (0, 4 * g + 3))],
        out_specs=[pl.BlockSpec((s, 256), lambda g: (0, g)),
                   pl.BlockSpec((s, 256), lambda g: (0, g))],
        out_shape=[SDS((s, 1024), F32), SDS((s, 1024), F32)],
        compiler_params=_cparams(("parallel",), vmem_mb=56),
    )(q_p, k_p, kva, kva)


def mla_attn_bwd(q_p, k_p, kva, o, do, lse, name):
    s = q_p.shape[0]
    tk = min(MLA_TK, s)
    tq = min(512, s)
    r = tq // tk
    nq = s // tq

    def body(q_ref, k_ref, v_ref, o_ref, do_ref, lse_ref, dq_ref, dk_ref, dv_ref, dkt_s, dvt_s):
        dkt_s[...] = jnp.zeros_like(dkt_s)
        dvt_s[...] = jnp.zeros_like(dvt_s)
        row = lax.broadcasted_iota(jnp.int32, (tq, tk), 0)
        col = lax.broadcasted_iota(jnp.int32, (tq, tk), 1)

        def tile(q, qt, dob, dot_, lse, delta, kj, dq, mask):
            k0 = pl.multiple_of(kj * tk, tk)
            k = k_ref[pl.ds(k0, tk), :]
            v = v_ref[pl.ds(k0, tk), :]
            p = jnp.exp(_dot_nt(q, k) - lse)
            if mask is not None:
                p = jnp.where(mask, p, 0.0)
            ds = (p * (_dot_nt(dob, v) - delta)).astype(BF16)
            dq = dq + _dot(ds, k)
            dkt_s[:, pl.ds(k0, tk)] += _dot(qt, ds)
            dvt_s[:, pl.ds(k0, tk)] += _dot(dot_, p.astype(BF16))
            return dq

        def q_body(qi, carry):
            q0 = pl.multiple_of(qi * tq, tq)
            q = q_ref[pl.ds(q0, tq), :]
            dob = do_ref[pl.ds(q0, tq), :]
            qt = q.astype(F32).T.astype(BF16)
            dot_ = dob.astype(F32).T.astype(BF16)
            lse = lse_ref[pl.ds(q0, tq), 0:1]
            delta = jnp.sum(dob.astype(F32) * o_ref[pl.ds(q0, tq), :], axis=1, keepdims=True)
            dq = lax.fori_loop(0, qi * r, lambda kj, dq: tile(q, qt, dob, dot_, lse, delta, kj, dq, None),
                               jnp.zeros((tq, 256), F32))
            for d in range(r):
                dq = tile(q, qt, dob, dot_, lse, delta, qi * r + d, dq, col + d * tk <= row)
            dq_ref[pl.ds(q0, tq), :] = dq
            return carry

        lax.fori_loop(0, nq, q_body, 0)
        dk_ref[...] = dkt_s[...].T
        dv_ref[...] = dvt_s[...].T.astype(BF16)

    b256 = pl.BlockSpec((s, 256), lambda h: (0, h))
    b128 = pl.BlockSpec((s, 128), lambda h: (0, h))
    return pl.pallas_call(
        body, name=name, grid=(MLA_HEADS,),
        in_specs=[b256, b256, pl.BlockSpec((s, 128), lambda h: (0, 2 * h + 1)), b128, b128, b128],
        out_specs=[b256, b256, b128],
        out_shape=[SDS((s, 2048), F32), SDS((s, 2048), F32), SDS((s, 1024), BF16)],
        scratch_shapes=[pltpu.VMEM((256, s), F32), pltpu.VMEM((128, s), F32)],
        compiler_params=_cparams(("parallel",), vmem_mb=56),
    )(q_p, k_p, kva, o, do, lse)


SWA_HEADS = 16
SWA_BLOCK = 128


def _swa_head(tile_idx, half):
    return (2 * (tile_idx // 4) + half) * 4 + tile_idx % 4


SWA_Q_PERM = np.concatenate([np.arange(_swa_head(tt, a) * HALF, (_swa_head(tt, a) + 1) * HALF)
                             for tt in range(8) for a in range(2)])
SWA_SLOPES = [2.0 ** (-8.0 * (h + 1) / SWA_HEADS) for h in range(SWA_HEADS)]


def swa_prep_fwd(proj, g128, name):
    s = proj.shape[0]
    tm = min(512, s)

    def body(q_ref, k_ref, g_ref, o_ref):
        lane = lax.broadcasted_iota(jnp.int32, (1, LANES), 1)
        lo = lane < HALF
        for tt in range(10):
            if tt < 8:
                x = q_ref[:, tt * 128:(tt + 1) * 128].astype(F32)
            else:
                x = k_ref[:, (tt - 8) * 128:(tt - 7) * 128].astype(F32)
            rstd = lax.rsqrt(_half_sum(x * x) * (1.0 / HALF) + NORM_EPS)
            gi = 0 if tt < 8 else 1
            o_ref[:, tt * 128:(tt + 1) * 128] = (x * rstd * g_ref[gi:gi + 1, :]).astype(BF16)

    return pl.pallas_call(
        body, name=name, grid=(s // tm,),
        in_specs=[pl.BlockSpec((tm, 1024), lambda i: (i, 0)), pl.BlockSpec((tm, 256), lambda i: (i, 8)),
                  pl.BlockSpec((2, 128), lambda i: (0, 0))],
        out_specs=pl.BlockSpec((tm, 1280), lambda i: (i, 0)),
        out_shape=SDS((s, 1280), BF16),
        compiler_params=_cparams(("parallel",)),
    )(proj, proj, g128)


def swa_prep_bwd(dqn, dkn, proj, g128, name):
    s = proj.shape[0]
    tm = min(512, s)
    nsteps = s // tm

    def body(dq_ref, dk_ref, xq_ref, xk_ref, g_ref, o_ref, dg_ref):
        @pl.when(pl.program_id(0) == 0)
        def _():
            dg_ref[...] = jnp.zeros_like(dg_ref)

        lane = lax.broadcasted_iota(jnp.int32, (1, LANES), 1)
        lo = lane < HALF
        for tt in range(10):
            if tt < 8:
                x = xq_ref[:, tt * 128:(tt + 1) * 128].astype(F32)
                dy = dq_ref[:, tt * 128:(tt + 1) * 128].astype(F32)
            else:
                x = xk_ref[:, (tt - 8) * 128:(tt - 7) * 128].astype(F32)
                dy = dk_ref[:, (tt - 8) * 128:(tt - 7) * 128].astype(F32)
            gi = 0 if tt < 8 else 1
            rstd = lax.rsqrt(_half_sum(x * x) * (1.0 / HALF) + NORM_EPS)
            xh = x * rstd
            dg_ref[gi:gi + 1, :] += jnp.sum(dy * xh, axis=0, keepdims=True)
            dxh = dy * g_ref[gi:gi + 1, :]
            mu = _half_sum(dxh * xh) * (1.0 / HALF)
            o_ref[:, tt * 128:(tt + 1) * 128] = (rstd * (dxh - xh * mu)).astype(BF16)

        @pl.when(pl.program_id(0) == nsteps - 1)
        def _():
            acc = dg_ref[...]
            dg_ref[...] = acc + pltpu.roll(acc, HALF, 1)

    return pl.pallas_call(
        body, name=name, grid=(nsteps,),
        in_specs=[pl.BlockSpec((tm, 1024), lambda i: (i, 0)), pl.BlockSpec((tm, 256), lambda i: (i, 0)),
                  pl.BlockSpec((tm, 1024), lambda i: (i, 0)), pl.BlockSpec((tm, 256), lambda i: (i, 8)),
                  pl.BlockSpec((2, 128), lambda i: (0, 0))],
        out_specs=[pl.BlockSpec((tm, 1280), lambda i: (i, 0)), pl.BlockSpec((2, 128), lambda i: (0, 0))],
        out_shape=[SDS((s, 1280), BF16), SDS((2, 128), F32)],
        compiler_params=_cparams(("arbitrary",)),
    )(dqn, dkn, proj, proj, g128)


SWA_QSCALE = 1.0 / math.sqrt(HALF)


def _swa_bias():
    b = SWA_BLOCK
    row = jnp.arange(b)[:, None]
    col = jnp.arange(2 * b)[None, :]
    rel = row + b - col
    valid = (rel >= 0) & (rel < b)
    slopes = jnp.asarray([SWA_SLOPES[_swa_head(tt, a)] for tt in range(8) for a in range(2)], F32)
    bias = jnp.where(valid[None], -slopes[:, None, None] * rel[None].astype(F32), -1e30)
    return jnp.stack([jnp.where((col >= b)[None], bias, -1e30), bias])


def _swa_tile_inputs(tt, q_ref, bias_ref, sink_ref, lane):
    b = SWA_BLOCK
    qt = q_ref[:, tt * 128:(tt + 1) * 128] * SWA_QSCALE
    zq = jnp.zeros_like(qt)
    qs = jnp.concatenate([jnp.where(lane < HALF, qt, zq), jnp.where(lane >= HALF, qt, zq)], axis=0)
    bias = jnp.concatenate([bias_ref[0, 2 * tt], bias_ref[0, 2 * tt + 1]], axis=0)
    sink = jnp.concatenate([jnp.full((b, 1), sink_ref[_swa_head(tt, a)], F32) for a in range(2)], axis=0)
    return qs, bias, sink


def _swa_softmax(qs, kt, bias, sink):
    sc = _dot_nt(qs, kt) + bias
    m = jnp.maximum(jnp.max(sc, axis=1, keepdims=True), sink)
    e = jnp.exp(sc - m)
    es = jnp.exp(sink - m)
    inv = 1.0 / (jnp.sum(e, axis=1, keepdims=True) + es)
    return e * inv, es * inv


def _swa_specs(b):
    cur = lambda w, c: pl.BlockSpec((b, w), lambda n: (n, c))
    prev = lambda w, c: pl.BlockSpec((b, w), lambda n: (jnp.maximum(n - 1, 0), c))
    bias = pl.BlockSpec((1, SWA_HEADS, b, 2 * b), lambda n: (jnp.minimum(n, 1), 0, 0, 0))
    return cur, prev, bias


def swa_attn_fwd(qkn, proj, sinks, bias, name):
    s = qkn.shape[0]
    b = SWA_BLOCK

    def body(sink_ref, bias_ref, q_ref, kc_ref, kp_ref, vc_ref, vp_ref, o_ref):
        lane = lax.broadcasted_iota(jnp.int32, (1, LANES), 1)
        for tt in range(8):
            gp = tt // 4
            kt = jnp.concatenate([kp_ref[:, gp * 128:(gp + 1) * 128], kc_ref[:, gp * 128:(gp + 1) * 128]], axis=0)
            vt = jnp.concatenate([vp_ref[:, gp * 128:(gp + 1) * 128], vc_ref[:, gp * 128:(gp + 1) * 128]], axis=0)
            qt = q_ref[:, tt * 128:(tt + 1) * 128] * SWA_QSCALE
            outs = []
            for a in range(2):
                qm = jnp.where((lane < HALF) if a == 0 else (lane >= HALF), qt, jnp.zeros_like(qt))
                p, _ = _swa_softmax(qm, kt, bias_ref[0, 2 * tt + a], sink_ref[_swa_head(tt, a)])
                outs.append(_dot(p.astype(BF16), vt))
            o_ref[:, tt * 128:(tt + 1) * 128] = jnp.where(lane < HALF, outs[0], outs[1])

    cur, prev, bias_spec = _swa_specs(b)
    return pl.pallas_call(
        body, name=name, grid=(s // b,),
        in_specs=[pl.BlockSpec(memory_space=pltpu.SMEM), bias_spec, cur(1024, 0), cur(256, 4), prev(256, 4),
                  cur(256, 9), prev(256, 9)],
        out_specs=cur(1024, 0),
        out_shape=SDS((s, 1024), F32),
        compiler_params=_cparams(("parallel",)),
    )(sinks, bias, qkn, qkn, qkn, proj, proj)


def swa_attn_bwd(qkn, proj, sinks, bias, do, name):
    s = qkn.shape[0]
    b = SWA_BLOCK

    def body(sink_ref, bias_ref, q_ref, kc_ref, kp_ref, vc_ref, vp_ref, do_ref, dq_ref, dk_ref, dv_ref, ds_ref):
        n = pl.program_id(0)

        @pl.when(n == 0)
        def _():
            dk_ref[...] = jnp.zeros_like(dk_ref)
            dv_ref[...] = jnp.zeros_like(dv_ref)
            ds_ref[...] = jnp.zeros_like(ds_ref)

        lane = lax.broadcasted_iota(jnp.int32, (1, LANES), 1)
        cur0 = pl.multiple_of(n * b, b)
        prev0 = pl.multiple_of(jnp.maximum(n - 1, 0) * b, b)
        dsink = jnp.zeros((1, LANES), F32)
        for gp in range(2):
            kt = jnp.concatenate([kp_ref[:, gp * 128:(gp + 1) * 128], kc_ref[:, gp * 128:(gp + 1) * 128]], axis=0)
            vt = jnp.concatenate([vp_ref[:, gp * 128:(gp + 1) * 128], vc_ref[:, gp * 128:(gp + 1) * 128]], axis=0)
            dkt = jnp.zeros((LANES, 2 * b), F32)
            dvt = jnp.zeros((LANES, 2 * b), F32)
            for tt in range(4 * gp, 4 * gp + 4):
                qs, bias2, sink = _swa_tile_inputs(tt, q_ref, bias_ref, sink_ref, lane)
                dot_ = do_ref[:, tt * 128:(tt + 1) * 128]
                zd = jnp.zeros_like(dot_)
                dos = jnp.concatenate([jnp.where(lane < HALF, dot_, zd), jnp.where(lane >= HALF, dot_, zd)], axis=0)
                p, ps = _swa_softmax(qs, kt, bias2, sink)
                dp = _dot_nt(dos, vt)
                delta = jnp.sum(p * dp, axis=1, keepdims=True)
                dz = (p * (dp - delta)).astype(BF16)
                sd = ps * delta
                for a in range(2):
                    dsink = dsink + jnp.where(lane == _swa_head(tt, a), -jnp.sum(sd[a * b:(a + 1) * b]), 0.0)
                dq2 = _dot(dz, kt)
                dq_ref[:, tt * 128:(tt + 1) * 128] = (jnp.where(lane < HALF, dq2[:b], dq2[b:]) * SWA_QSCALE).astype(BF16)
                dkt = dkt + _dot(qs.astype(F32).T.astype(BF16), dz)
                dvt = dvt + _dot(dos.astype(F32).T.astype(BF16), p.astype(BF16))
            dkt, dvt = dkt.T, dvt.T
            cols = slice(gp * 128, (gp + 1) * 128)
            dk_ref[pl.ds(prev0, b), cols] += dkt[:b]
            dk_ref[pl.ds(cur0, b), cols] += dkt[b:]
            dv_ref[pl.ds(prev0, b), cols] += dvt[:b]
            dv_ref[pl.ds(cur0, b), cols] += dvt[b:]
        ds_ref[0:1, :] += dsink

    cur, prev, bias_spec = _swa_specs(b)
    full = pl.BlockSpec((s, 256), lambda n: (0, 0))
    return pl.pallas_call(
        body, name=name, grid=(s // b,),
        in_specs=[pl.BlockSpec(memory_space=pltpu.SMEM), bias_spec, cur(1024, 0), cur(256, 4), prev(256, 4),
                  cur(256, 9), prev(256, 9), cur(1024, 0)],
        out_specs=[cur(1024, 0), full, full, pl.BlockSpec((8, LANES), lambda n: (0, 0))],
        out_shape=[SDS((s, 1024), BF16), SDS((s, 256), F32), SDS((s, 256), F32), SDS((8, LANES), F32)],
        compiler_params=_cparams(("arbitrary",)),
    )(sinks, bias, qkn, qkn, qkn, proj, proj, do)


def _my_pos():
    return lax.axis_index("x"), lax.axis_index("y"), lax.axis_index("c")


ANY = pl.BlockSpec(memory_space=pl.ANY)


def all_gather_packed(xs, name):
    r, n = xs.shape

    def body(x_ref, out_ref, send_sems, recv_sems, local_sem):
        x, y, c = _my_pos()
        me, sibling = (x, y, c), (x, y, 1 - c)
        south = c == 0
        via = (jnp.where(south, 1 - x, x), jnp.where(south, y, 1 - y))
        far = (jnp.where(south, x, 1 - x), jnp.where(south, 1 - y, y))
        rk = jnp.where(south, 1, 2)
        diag = (1 - x, 1 - y)

        def slot(px, py, pc):
            return out_ref.at[4 * px + 2 * py + pc]

        def copy(k, block, to, src=None):
            return pltpu.make_async_remote_copy(
                src_ref=slot(*block) if src is None else src, dst_ref=slot(*block),
                send_sem=send_sems.at[k], recv_sem=recv_sems.at[k],
                device_id=to, device_id_type=MESH_ID)

        mine = pltpu.make_async_copy(x_ref, slot(*me), local_sem)
        mine.start()
        sent = [copy(0, me, sibling, src=x_ref), copy(1, me, (1 - x, y, c), src=x_ref),
                copy(2, me, (x, 1 - y, c), src=x_ref)]
        for cp in sent:
            cp.start()
        copy(rk, (*via, c), me).wait_recv()
        sent += [copy(3, (*via, c), (*far, c)), copy(3 + rk, (*via, c), sibling)]
        sent[-2].start()
        sent[-1].start()
        copy(3 - rk, (*far, c), me).wait_recv()
        sent.append(copy(6 - rk, (*far, c), sibling))
        sent[-1].start()
        copy(3, (*diag, c), me).wait_recv()
        sent.append(copy(6, (*diag, c), sibling))
        sent[-1].start()
        copy(0, sibling, me).wait_recv()
        for k, chip in ((4, (1 - x, y)), (5, (x, 1 - y)), (6, diag)):
            copy(k, (*chip, 1 - c), me).wait_recv()
        for cp in sent:
            cp.wait_send()
        mine.wait()

    return pl.pallas_call(
        body, name=name,
        out_shape=SDS((N_DEV, r, n), xs.dtype),
        in_specs=[ANY], out_specs=ANY,
        scratch_shapes=[pltpu.SemaphoreType.DMA((7,)), pltpu.SemaphoreType.DMA((7,)), pltpu.SemaphoreType.DMA],
    )(xs)


def rs_sibling_exchange(g, name):
    _, r, n = g.shape

    def body(g_ref, ra_ref, send_sems, recv_sems):
        x, y, c = _my_pos()
        copies = [pltpu.make_async_remote_copy(
            src_ref=g_ref.at[2 * chip + (1 - c)], dst_ref=ra_ref.at[chip],
            send_sem=send_sems.at[chip], recv_sem=recv_sems.at[chip],
            device_id=(x, y, 1 - c), device_id_type=MESH_ID) for chip in range(4)]
        for cp in copies:
            cp.start()
        for cp in copies:
            cp.wait()

    return pl.pallas_call(
        body, name=name, out_shape=SDS((4, r, n), g.dtype), in_specs=[ANY], out_specs=ANY,
        scratch_shapes=[pltpu.SemaphoreType.DMA((4,)), pltpu.SemaphoreType.DMA((4,))],
    )(g)


def rs_pair_add(g, ra, cidx, name):
    _, r, n = g.shape
    tr = r // 2

    def body(c_ref, g_ref, ra_ref, p_ref):
        p_ref[...] = (g_ref[...].astype(F32) + ra_ref[...].astype(F32)).astype(p_ref.dtype)

    return pl.pallas_call(
        body, name=name,
        grid_spec=pltpu.PrefetchScalarGridSpec(
            num_scalar_prefetch=1, grid=(4, r // tr),
            in_specs=[pl.BlockSpec((1, tr, n), lambda i, j, c: (2 * i + c[0], j, 0)),
                      pl.BlockSpec((1, tr, n), lambda i, j, c: (i, j, 0))],
            out_specs=pl.BlockSpec((1, tr, n), lambda i, j, c: (i, j, 0))),
        out_shape=SDS((4, r, n), g.dtype),
        compiler_params=_cparams(("parallel", "parallel")),
    )(cidx, g, ra)


def rs_cross_exchange(p, name):
    _, r, n = p.shape

    def body(p_ref, rb_ref, send_sems, recv_sems):
        x, y, c = _my_pos()
        copies = []
        for k, (dx, dy) in enumerate(((1, 0), (0, 1), (1, 1))):
            tx = 1 - x if dx else x
            ty = 1 - y if dy else y
            copies.append(pltpu.make_async_remote_copy(
                src_ref=p_ref.at[2 * tx + ty], dst_ref=rb_ref.at[k],
                send_sem=send_sems.at[k], recv_sem=recv_sems.at[k],
                device_id=(tx, ty, c), device_id_type=MESH_ID))
        for cp in copies:
            cp.start()
        for cp in copies:
            cp.wait()

    return pl.pallas_call(
        body, name=name, out_shape=SDS((3, r, n), p.dtype), in_specs=[ANY], out_specs=ANY,
        scratch_shapes=[pltpu.SemaphoreType.DMA((3,)), pltpu.SemaphoreType.DMA((3,))],
    )(p)


def _adamw(w, g, m, v):
    m = ADAM_B1 * m + (1.0 - ADAM_B1) * g
    v = ADAM_B2 * v + (1.0 - ADAM_B2) * (g * g)
    m_hat = m / (1.0 - ADAM_B1 ** ADAM_STEP)
    v_hat = v / (1.0 - ADAM_B2 ** ADAM_STEP)
    delta = -ADAM_LR * (m_hat / (jnp.sqrt(v_hat) + ADAM_EPS) + ADAM_WD * w)
    return delta, m, v


def rs_final_adamw(p, rb, chipidx, w, m, v, row0, col_blk, name):
    r, n = w.shape
    tr = min(r, 256 if n <= PAIR_COLS else 128)
    assert r % tr == 0 and row0 % tr == 0
    rb0 = row0 // tr

    def body(c_ref, p_ref, rb_ref, w_ref, m_ref, v_ref, g_out, d_out, m_out, v_out):
        g = p_ref[0].astype(F32)
        for k in range(3):
            g = g + rb_ref[k].astype(F32)
        d, mn, vn = _adamw(w_ref[...], g, m_ref[...], v_ref[...])
        g_out[...] = g
        d_out[...] = d
        m_out[...] = mn
        v_out[...] = vn

    blk = pl.BlockSpec((tr, n), lambda j, c: (j, 0))
    return pl.pallas_call(
        body, name=name,
        grid_spec=pltpu.PrefetchScalarGridSpec(
            num_scalar_prefetch=1, grid=(r // tr,),
            in_specs=[pl.BlockSpec((1, tr, n), lambda j, c: (c[0], rb0 + j, col_blk)),
                      pl.BlockSpec((3, tr, n), lambda j, c: (0, rb0 + j, col_blk)), blk, blk, blk],
            out_specs=[blk, blk, blk, blk]),
        out_shape=[SDS((r, n), F32)] * 4,
        compiler_params=_cparams(("parallel",)),
    )(chipidx, p, rb, w, m, v)


def small_allreduce_adamw(g, w, m, v, name):
    rows = g.shape[0]

    def body(g_ref, w_ref, m_ref, v_ref, g_out, d_out, m_out, v_out, buf, send_sems, recv_sems):
        x, y, c = _my_pos()
        my = 4 * x + 2 * y + c
        buf[my] = g_ref[...]
        copies = []
        for k in range(1, N_DEV):
            dx, dy, dc = (k >> 2) & 1, (k >> 1) & 1, k & 1
            tgt = (1 - x if dx else x, 1 - y if dy else y, 1 - c if dc else c)
            copies.append(pltpu.make_async_remote_copy(
                src_ref=buf.at[my], dst_ref=buf.at[my],
                send_sem=send_sems.at[k - 1], recv_sem=recv_sems.at[k - 1],
                device_id=tgt, device_id_type=MESH_ID))
        for cp in copies:
            cp.start()
        for cp in copies:
            cp.wait()
        tot = buf[0]
        for j in range(1, N_DEV):
            tot = tot + buf[j]
        d, mn, vn = _adamw(w_ref[...], tot, m_ref[...], v_ref[...])
        g_out[...] = tot
        d_out[...] = d
        m_out[...] = mn
        v_out[...] = vn

    vm = pl.BlockSpec(memory_space=pltpu.VMEM)
    return pl.pallas_call(
        body, name=name, out_shape=[SDS((rows, LANES), F32)] * 4,
        in_specs=[vm] * 4, out_specs=[vm] * 4,
        scratch_shapes=[pltpu.VMEM((N_DEV, rows, LANES), F32),
                        pltpu.SemaphoreType.DMA((N_DEV - 1,)), pltpu.SemaphoreType.DMA((N_DEV - 1,))],
    )(g, w, m, v)


BIG = ["l0_w_in", "l0_w_out", "l1_w_in", "l1_w_uq", "l1_w_ukv", "l1_w_out", "l2_w_in", "l2_w_out",
       "l3_w_in", "l3_w_out"]
SMALL = ["l0_norm", "l1_norm", "l1_q_a_norm", "l1_kv_a_norm", "l1_q_head_norm", "l1_k_head_norm",
         "l2_norm", "l2_q_head_norm", "l2_k_head_norm", "l2_sinks", "l3_norm"]

MLA_UQ_PERM = np.concatenate([np.arange(h * 192, h * 192 + 128) for h in range(8)]
                             + [np.arange(h * 192 + 128, (h + 1) * 192) for h in range(8)])
SWA_IN_PERM = np.concatenate([SWA_Q_PERM, 1536 + SWA_Q_PERM, np.arange(1024, 1536)])


def _inv(perm):
    inv = np.empty_like(perm)
    inv[perm] = np.arange(perm.size)
    return inv


OUTS = ["l0_w_out", "l1_w_out", "l2_w_out", "l3_w_out"]
REST = ["l1_w_in", "l1_w_uq", "l1_w_ukv", "l2_w_in"]


def _pack_rest(shards):
    parts = [shards[nm].reshape(-1, D_MODEL) for nm in REST]
    used = sum(p.shape[0] for p in parts)
    assert REST_ROW0 + used == ROWS_USED
    parts.append(jnp.zeros((REST_ROWS - used, D_MODEL), parts[0].dtype))
    return jnp.concatenate(parts, axis=0)


def _unpack_rest(packed, shapes):
    out, off = {}, 0
    for nm in REST:
        shp = shapes[nm]
        rows = shp[0] * shp[1] // D_MODEL
        out[nm] = packed[off:off + rows].reshape(shp)
        off += rows
    return out


def _pack(shards):
    pair = jnp.concatenate([shards["l0_w_in"], shards["l3_w_in"]], axis=1)
    return jnp.concatenate([pair] + [shards[nm] for nm in OUTS] + [_pack_rest(shards)], axis=0)


def _out_weight(gathered, layer):
    r0 = OUT_ROW0 + 128 * layer
    return gathered[:, r0:r0 + 128].reshape(D_MODEL, D_MODEL)


def _rest_full_weights(gathered, shapes):
    out, off = {}, REST_ROW0
    for nm in REST:
        shp = shapes[nm]
        rows = shp[0] * shp[1] // D_MODEL
        blk = gathered[:, off:off + rows].reshape((N_DEV,) + shp)
        out[nm] = jnp.transpose(blk, (1, 0, 2)).reshape(shp[0], N_DEV * shp[1])
        off += rows
    return out


def _grad_pieces(pair_l0, pair_l3, dw_out, rest_grads, shapes):
    parts = [jnp.concatenate([pair_l0, pair_l3], axis=2)]
    parts += [g.reshape(N_DEV, 128, D_MODEL).astype(BF16) for g in dw_out]
    used = REST_ROW0
    for nm in REST:
        shp = shapes[nm]
        blk = jnp.transpose(rest_grads[nm].reshape(shp[0], N_DEV, shp[1]), (1, 0, 2))
        parts.append(blk.reshape(N_DEV, -1, D_MODEL).astype(BF16))
        used += parts[-1].shape[1]
    parts.append(jnp.zeros((N_DEV, ROWS_PACK - used, D_MODEL), BF16))
    return jnp.concatenate(parts, axis=1)


SMALL_USED = 5008


def _pack_small(vals, extra=None):
    parts = [vals[nm].reshape(-1).astype(F32) for nm in SMALL]
    assert sum(p.shape[0] for p in parts) == SMALL_USED
    if extra is not None:
        parts.append(extra.reshape(1).astype(F32))
    flat = jnp.concatenate(parts)
    flat = jnp.concatenate([flat, jnp.zeros((SMALL_ROWS * LANES - flat.shape[0],), F32)])
    return flat.reshape(SMALL_ROWS, LANES)


def _unpack_small(packed, shapes):
    flat = packed.reshape(-1)
    out, off = {}, 0
    for nm in SMALL:
        n = shapes[nm][0]
        out[nm] = flat[off:off + n]
        off += n
    return out


def _row(v):
    return v.reshape(1, -1).astype(F32)


def _mla_gain_rows(g):
    return jnp.concatenate([g[:128], g[128:], g[128:]]).reshape(1, 256).astype(F32)


def _layer_fwd_sb(x, norm, gathered, shard_col, w_out, tag):
    proj, xn = norm_matmul(x, 0, _row(norm), gathered, f"{tag}_in", shard_col=shard_col)
    o = sb_attn_fwd(proj, f"{tag}_attn")
    y = gate_out_fwd(o, proj, 3, w_out, x, f"{tag}_out")
    return y, (x, xn, proj, o)


def _layer_bwd_sb(dy, saved, norm, gathered, shard_col, w_out, tag):
    x, xn, proj, o = saved
    do, dgate, hg = gate_out_bwd(dy, w_out, o, proj, 3, f"{tag}_outb")
    dw_out = tn_matmul(hg, dy, f"{tag}_dwout")
    dq, dk, dv = sb_attn_bwd(proj, o, do, f"{tag}_attnb")
    dproj = [dq, dk, dv, dgate]
    dw_in = tn_matmul_shards(xn, dproj, f"{tag}_dwin")
    dx, dnorm = nt_norm_bwd(dproj, gathered, x, 0, _row(norm), dy, f"{tag}_inb", F32, shard_col=shard_col)
    return dx, dw_in, dw_out, dnorm.reshape(-1)


def kernel(x, l0_norm, l0_w_in, l0_w_out, l1_norm, l1_w_in, l1_q_a_norm, l1_w_uq, l1_kv_a_norm, l1_w_ukv, l1_q_head_norm, l1_k_head_norm, l1_w_out, l2_norm, l2_w_in, l2_q_head_norm, l2_k_head_norm, l2_sinks, l2_w_out, l3_norm, l3_w_in, l3_w_out, loss_target, m_l0_norm, m_l0_w_in, m_l0_w_out, m_l1_norm, m_l1_w_in, m_l1_q_a_norm, m_l1_w_uq, m_l1_kv_a_norm, m_l1_w_ukv, m_l1_q_head_norm, m_l1_k_head_norm, m_l1_w_out, m_l2_norm, m_l2_w_in, m_l2_q_head_norm, m_l2_k_head_norm, m_l2_sinks, m_l2_w_out, m_l3_norm, m_l3_w_in, m_l3_w_out, v_l0_norm, v_l0_w_in, v_l0_w_out, v_l1_norm, v_l1_w_in, v_l1_q_a_norm, v_l1_w_uq, v_l1_kv_a_norm, v_l1_w_ukv, v_l1_q_head_norm, v_l1_k_head_norm, v_l1_w_out, v_l2_norm, v_l2_w_in, v_l2_q_head_norm, v_l2_k_head_norm, v_l2_sinks, v_l2_w_out, v_l3_norm, v_l3_w_in, v_l3_w_out):
    loc = dict(locals())
    names = BIG + SMALL
    w = {nm: loc[nm] for nm in names}
    mom = {nm: loc["m_" + nm] for nm in names}
    vel = {nm: loc["v_" + nm] for nm in names}
    shapes = {nm: w[nm].shape for nm in names}
    xs = x[0]
    tgt = loss_target[0]

    gathered = all_gather_packed(_pack({nm: w[nm].astype(BF16) for nm in BIG}), "ag_weights")
    loss_local, dx0, pieces, sg = _local_step(xs, tgt, gathered, {nm: w[nm] for nm in SMALL}, shapes)
    return _reduce_and_update(loss_local, dx0, pieces, sg, w, mom, vel, shapes)


def _local_step(xs, tgt, gathered, sm, shapes):
    s = xs.shape[0]
    fw = _rest_full_weights(gathered, shapes)
    w_out = [_out_weight(gathered, layer) for layer in range(4)]
    l0_norm, l1_norm, l2_norm, l3_norm = sm["l0_norm"], sm["l1_norm"], sm["l2_norm"], sm["l3_norm"]
    l1_q_a_norm, l1_kv_a_norm = sm["l1_q_a_norm"], sm["l1_kv_a_norm"]
    l1_q_head_norm, l1_k_head_norm = sm["l1_q_head_norm"], sm["l1_k_head_norm"]
    l2_q_head_norm, l2_k_head_norm, l2_sinks = sm["l2_q_head_norm"], sm["l2_k_head_norm"], sm["l2_sinks"]
    w1_in = jnp.concatenate([fw["l1_w_in"][:, 448:], fw["l1_w_in"][:, :448], jnp.zeros((D_MODEL, 64), BF16)], axis=1)
    w1_uq = fw["l1_w_uq"][:, MLA_UQ_PERM]
    w2_in = fw["l2_w_in"][:, SWA_IN_PERM]
    w2_out = w_out[2][SWA_Q_PERM, :]

    x1, sv0 = _layer_fwd_sb(xs, l0_norm, gathered, 0, w_out[0], "l0")

    cos, sin = _rope_tables(s)
    gq, gk = _mla_gain_rows(l1_q_head_norm), _mla_gain_rows(l1_k_head_norm)
    proj1, xn1 = norm_matmul(x1, 0, _row(l1_norm), w1_in, "l1_in")
    qa, qln = norm_matmul(proj1, 4, _row(l1_q_a_norm), w1_uq, "l1_uq")
    kva, kvn = norm_matmul(proj1, 10, _row(l1_kv_a_norm), fw["l1_w_ukv"], "l1_ukv")
    q_p, k_p = mla_prep_fwd(qa, kva, proj1, gq, gk, cos, sin, "l1_prep")
    o1, lse1 = mla_attn_fwd(q_p, k_p, kva, "l1_attn")
    x2 = gate_out_fwd(o1, proj1, 0, w_out[1], x1, "l1_out")

    g2 = jnp.stack([jnp.tile(l2_q_head_norm, 2), jnp.tile(l2_k_head_norm, 2)]).astype(F32)
    proj2, xn2 = norm_matmul(x2, 0, _row(l2_norm), w2_in, "l2_in")
    qkn2 = swa_prep_fwd(proj2, g2, "l2_prep")
    bias2 = _swa_bias()
    o2 = swa_attn_fwd(qkn2, proj2, l2_sinks, bias2, "l2_attn")
    x3 = gate_out_fwd(o2, proj2, 1, w2_out, x2, "l2_out")

    x4, sv3 = _layer_fwd_sb(x3, l3_norm, gathered, 1, w_out[3], "l3")

    lossblk, dy = loss_kernel(x4, tgt, "loss")

    fg, sg = {}, {}
    dw_out = [None] * 4
    dx3, pair_l3, dw_out[3], sg["l3_norm"] = _layer_bwd_sb(dy, sv3, l3_norm, gathered, 1, w_out[3], "l3")

    do2, dgate2, hg2 = gate_out_bwd(dx3, w2_out, o2, proj2, 1, "l2_outb")
    dw2_out = tn_matmul(hg2, dx3, "l2_dwout")
    dqn2, dkn2, dv2, dsink2 = swa_attn_bwd(qkn2, proj2, l2_sinks, bias2, do2, "l2_attnb")
    dqk2, dg2 = swa_prep_bwd(dqn2, dkn2, proj2, g2, "l2_prepb")
    dproj2 = jnp.concatenate([dqk2[:, :1024], dgate2, dqk2[:, 1024:], dv2.astype(BF16)], axis=1)
    dw2_in = tn_matmul(xn2, dproj2, "l2_dwin")
    dx2, dn2 = nt_norm_bwd(dproj2, w2_in, x2, 0, _row(l2_norm), dx3, "l2_inb", F32)
    fg["l2_w_in"] = dw2_in[:, _inv(SWA_IN_PERM)]
    dw_out[2] = dw2_out[_inv(SWA_Q_PERM), :]
    sg["l2_norm"] = dn2.reshape(-1)
    sg["l2_q_head_norm"] = dg2[0, :HALF]
    sg["l2_k_head_norm"] = dg2[1, :HALF]
    sg["l2_sinks"] = dsink2[0, :SWA_HEADS]

    do1, dgate1, hg1 = gate_out_bwd(dx2, w_out[1], o1, proj1, 0, "l1_outb")
    dw_out[1] = tn_matmul(hg1, dx2, "l1_dwout")
    dq_p, dk_p, dv1 = mla_attn_bwd(q_p, k_p, kva, o1, do1, lse1, "l1_attnb")
    dqa, dkva, dkpe, dgq, dgk = mla_prep_bwd(dq_p, dk_p, dv1, qa, kva, proj1, gq, gk, cos, sin, "l1_prepb")
    dw_uq = tn_matmul(qln, dqa, "l1_dwuq")
    fg["l1_w_ukv"] = tn_matmul(kvn, dkva, "l1_dwukv")
    dqlat, dgqa = nt_norm_bwd(dqa, w1_uq, proj1, 4, _row(l1_q_a_norm), None, "l1_uqb", BF16)
    dkvlat, dgkva = nt_norm_bwd(dkva, fw["l1_w_ukv"], proj1, 10, _row(l1_kv_a_norm), None, "l1_ukvb", BF16)
    dproj1 = jnp.concatenate([dgate1, dqlat, dkvlat, dkpe], axis=1)
    dw1_in = tn_matmul(xn1, dproj1, "l1_dwin")
    dx1, dn1 = nt_norm_bwd(dproj1, w1_in, x1, 0, _row(l1_norm), dx2, "l1_inb", F32)
    fg["l1_w_in"] = jnp.concatenate([dw1_in[:, 1024:1472], dw1_in[:, :1024]], axis=1)
    fg["l1_w_uq"] = dw_uq[:, _inv(MLA_UQ_PERM)]
    sg["l1_norm"] = dn1.reshape(-1)
    sg["l1_q_a_norm"] = dgqa.reshape(-1)
    sg["l1_kv_a_norm"] = dgkva.reshape(-1)
    sg["l1_q_head_norm"] = jnp.concatenate([dgq[0, :128], dgq[0, 128:192] + dgq[0, 192:256]])
    sg["l1_k_head_norm"] = jnp.concatenate([dgk[0, :128], dgk[0, 128:192] + dgk[0, 192:256]])

    dx0, pair_l0, dw_out[0], sg["l0_norm"] = _layer_bwd_sb(dx1, sv0, l0_norm, gathered, 0, w_out[0], "l0")
    return lossblk[0, 0], dx0, _grad_pieces(pair_l0, pair_l3, dw_out, fg, shapes), sg


def _reduce_and_update(loss_local, dx0, pieces, sg, w, mom, vel, shapes):
    xi, yi, ci = _my_pos()
    cidx = jnp.reshape(ci, (1,)).astype(jnp.int32)
    chipidx = jnp.reshape(2 * xi + yi, (1,)).astype(jnp.int32)
    ra = rs_sibling_exchange(pieces, "rs_sibling")
    part = rs_pair_add(pieces, ra, cidx, "rs_pair_add")
    rb = rs_cross_exchange(part, "rs_cross")

    def update(nm, row0, col_blk):
        return rs_final_adamw(part, rb, chipidx, w[nm], mom[nm], vel[nm], row0, col_blk, f"adamw_{nm}")

    big = {"l0_w_in": update("l0_w_in", 0, 0), "l3_w_in": update("l3_w_in", 0, 1)}
    for layer, nm in enumerate(OUTS):
        big[nm] = update(nm, OUT_ROW0 + 128 * layer, 0)
    rest = rs_final_adamw(part, rb, chipidx, _pack_rest(w), _pack_rest(mom), _pack_rest(vel), REST_ROW0, 0,
                          "adamw_rest")
    gs, ds, ms, vs = small_allreduce_adamw(_pack_small(sg, loss_local), _pack_small({nm: w[nm] for nm in SMALL}),
                                           _pack_small({nm: mom[nm] for nm in SMALL}),
                                           _pack_small({nm: vel[nm] for nm in SMALL}), "small_allreduce")

    outs = []
    for i, small in enumerate((gs, ds, ms, vs)):
        d = {nm: big[nm][i] for nm in big}
        d.update(_unpack_rest(rest[i], shapes))
        d.update(_unpack_small(small, shapes))
        outs.append(d)
    order = ["l0_norm", "l0_w_in", "l0_w_out", "l1_norm", "l1_w_in", "l1_q_a_norm", "l1_w_uq", "l1_kv_a_norm",
             "l1_w_ukv", "l1_q_head_norm", "l1_k_head_norm", "l1_w_out", "l2_norm", "l2_w_in", "l2_q_head_norm",
             "l2_k_head_norm", "l2_sinks", "l2_w_out", "l3_norm", "l3_w_in", "l3_w_out"]
    flat = [gs.reshape(-1)[SMALL_USED], dx0[None]]
    for d in outs:
        flat += [d[nm] for nm in order]
    return tuple(flat)
```

```python
import math

import numpy as np
import jax
import jax.numpy as jnp
from jax import lax
from jax.experimental import pallas as pl
from jax.experimental.pallas import tpu as pltpu

F32 = jnp.float32
BF16 = jnp.bfloat16
SDS = jax.ShapeDtypeStruct
MESH_ID = pl.DeviceIdType.MESH

D_MODEL = 1024
NORM_EPS = 1e-6
N_DEV = 8
LANES = 128
HALF = 64
ROWS_PACK = 2176
ROWS_USED = 2120
PAIR_COLS = 512
OUT_ROW0 = 1024
REST_ROW0 = 1536
REST_ROWS = ROWS_PACK - REST_ROW0
SMALL_ROWS = 40

ADAM_LR = 0.001
ADAM_B1 = 0.9
ADAM_B2 = 0.999
ADAM_EPS = 1e-08
ADAM_WD = 0.01
ADAM_STEP = 10

NT_DIMS = (((1,), (1,)), ((), ()))
TN_DIMS = (((0,), (0,)), ((), ()))


def _cparams(sem=None, vmem_mb=48):
    return pltpu.CompilerParams(dimension_semantics=sem, vmem_limit_bytes=vmem_mb * 2 ** 20)


def _dot(a, b):
    return jnp.dot(a, b, preferred_element_type=F32)


def _dot_nt(a, b):
    return lax.dot_general(a, b, NT_DIMS, preferred_element_type=F32)


def _dot_tn(a, b):
    return lax.dot_general(a, b, TN_DIMS, preferred_element_type=F32)


def _split_dot(a, b):
    hi = a.astype(BF16)
    lo = (a - hi.astype(F32)).astype(BF16)
    return _dot(hi, b) + _dot(lo, b)


def _cumsum_dot(lf, u):
    return _dot(lf.astype(BF16), u)


def _pick_tile(n, cap):
    return max(t for t in range(LANES, min(n, cap) + 1, LANES) if n % t == 0)


def _sigmoid(x):
    return 1.0 / (1.0 + jnp.exp(-x))


def norm_matmul(x, colblk, g, w, name, shard_col=None):
    s = x.shape[0]
    if shard_col is None:
        k, n = w.shape
        tn = _pick_tile(n, 1024)
        w_spec = pl.BlockSpec((k, tn), lambda i, j: (0, j))
    else:
        k, n, tn = D_MODEL, N_DEV * PAIR_COLS, 2 * PAIR_COLS
        w_spec = pl.BlockSpec((2, k, PAIR_COLS), lambda i, j: (j, 0, shard_col))
    tm = min(1024, s)
    assert s % tm == 0

    def body(x_ref, g_ref, w_ref, o_ref, xn_ref, xn_s):
        @pl.when(pl.program_id(1) == 0)
        def _():
            xf = x_ref[...].astype(F32)
            ms = jnp.mean(xf * xf, axis=-1, keepdims=True)
            y = (xf * lax.rsqrt(ms + NORM_EPS) * g_ref[...]).astype(BF16)
            xn_s[...] = y
            xn_ref[...] = y

        if shard_col is None:
            o_ref[...] = _dot(xn_s[...], w_ref[...]).astype(o_ref.dtype)
        else:
            for half in range(2):
                o_ref[:, half * PAIR_COLS:(half + 1) * PAIR_COLS] = _dot(xn_s[...], w_ref[half]).astype(o_ref.dtype)

    return pl.pallas_call(
        body, name=name, grid=(s // tm, n // tn),
        in_specs=[pl.BlockSpec((tm, k), lambda i, j: (i, colblk)),
                  pl.BlockSpec((1, k), lambda i, j: (0, 0)),
                  w_spec],
        out_specs=[pl.BlockSpec((tm, tn), lambda i, j: (i, j)),
                   pl.BlockSpec((tm, k), lambda i, j: (i, 0))],
        out_shape=[SDS((s, n), BF16), SDS((s, k), BF16)],
        scratch_shapes=[pltpu.VMEM((tm, k), BF16)],
        compiler_params=_cparams(("parallel", "arbitrary")),
    )(x, g, w)


def gate_out_fwd(o, proj, gate_blk, w_out, x, name):
    s = x.shape[0]
    d = D_MODEL
    tm = min(512, s)

    def body(o_ref, g_ref, w_ref, x_ref, y_ref):
        g = g_ref[...].astype(F32)
        hg = (o_ref[...].astype(F32) * (g * _sigmoid(g))).astype(BF16)
        y_ref[...] = x_ref[...] + _dot(hg, w_ref[...])

    return pl.pallas_call(
        body, name=name, grid=(s // tm,),
        in_specs=[pl.BlockSpec((tm, d), lambda i: (i, 0)),
                  pl.BlockSpec((tm, d), lambda i: (i, gate_blk)),
                  pl.BlockSpec((d, d), lambda i: (0, 0)),
                  pl.BlockSpec((tm, d), lambda i: (i, 0))],
        out_specs=pl.BlockSpec((tm, d), lambda i: (i, 0)),
        out_shape=SDS((s, d), F32),
        compiler_params=_cparams(("parallel",)),
    )(o, proj, w_out, x)


def gate_out_bwd(dy, w_out, o, proj, gate_blk, name):
    s = dy.shape[0]
    d = D_MODEL
    tm = min(512, s)

    def body(dy_ref, w_ref, o_ref, g_ref, do_ref, dg_ref, hg_ref):
        dh = _dot_nt(dy_ref[...].astype(BF16), w_ref[...])
        g = g_ref[...].astype(F32)
        o = o_ref[...].astype(F32)
        sg = _sigmoid(g)
        silu = g * sg
        do_ref[...] = (dh * silu).astype(BF16)
        dg_ref[...] = (dh * o * (sg * (1.0 + g * (1.0 - sg)))).astype(BF16)
        hg_ref[...] = (o * silu).astype(BF16)

    blk = pl.BlockSpec((tm, d), lambda i: (i, 0))
    return pl.pallas_call(
        body, name=name, grid=(s // tm,),
        in_specs=[blk, pl.BlockSpec((d, d), lambda i: (0, 0)), blk,
                  pl.BlockSpec((tm, d), lambda i: (i, gate_blk))],
        out_specs=[blk, blk, blk],
        out_shape=[SDS((s, d), BF16)] * 3,
        compiler_params=_cparams(("parallel",)),
    )(dy, w_out, o, proj)


def tn_matmul(a, b, name):
    s, k = a.shape
    n = b.shape[1]
    tk = min(1024, k)
    tn = _pick_tile(n, 512)
    ts = min(2048, s)
    assert k % tk == 0 and s % ts == 0

    def body(a_ref, b_ref, o_ref):
        @pl.when(pl.program_id(2) == 0)
        def _():
            o_ref[...] = jnp.zeros_like(o_ref)

        o_ref[...] += _dot_tn(a_ref[...].astype(BF16), b_ref[...].astype(BF16))

    return pl.pallas_call(
        body, name=name, grid=(k // tk, n // tn, s // ts),
        in_specs=[pl.BlockSpec((ts, tk), lambda i, j, r: (r, i)),
                  pl.BlockSpec((ts, tn), lambda i, j, r: (r, j))],
        out_specs=pl.BlockSpec((tk, tn), lambda i, j, r: (i, j)),
        out_shape=SDS((k, n), F32),
        compiler_params=_cparams(("parallel", "parallel", "arbitrary")),
    )(a, b)


def tn_matmul_shards(a, b_parts, name):
    s, k = a.shape
    per = D_MODEL // PAIR_COLS
    assert len(b_parts) * per == N_DEV and a.dtype == BF16
    assert all(b.shape == (s, D_MODEL) and b.dtype == BF16 for b in b_parts)

    def body(a_ref, *refs):
        o_ref = refs[-1]
        j = pl.program_id(0)
        for p, b_ref in enumerate(refs[:-1]):
            @pl.when(j // per == p)
            def _(b_ref=b_ref):
                o_ref[...] = _dot_tn(a_ref[...], b_ref[...]).astype(BF16)

    part_specs = [pl.BlockSpec((s, PAIR_COLS), lambda j, p=p: (0, jnp.clip(j - per * p, 0, per - 1)))
                  for p in range(len(b_parts))]
    return pl.pallas_call(
        body, name=name, grid=(N_DEV,),
        in_specs=[pl.BlockSpec((s, k), lambda j: (0, 0))] + part_specs,
        out_specs=pl.BlockSpec((None, k, PAIR_COLS), lambda j: (j, 0, 0)),
        out_shape=SDS((N_DEV, k, PAIR_COLS), BF16),
        compiler_params=_cparams(("arbitrary",)),
    )(a, *b_parts)


def nt_norm_bwd(dp, w, x, xcol, g, res, name, out_dtype, shard_col=None):
    tm = min(512, dp[0].shape[0] if shard_col is not None else dp.shape[0])
    if shard_col is None:
        s, n = dp.shape
        k = w.shape[0]
        w_spec = pl.BlockSpec((k, n), lambda i: (0, 0))
        dp_parts = [dp]
        dp_specs = [pl.BlockSpec((tm, n), lambda i: (i, 0))]
    else:
        s, k = dp[0].shape[0], D_MODEL
        w_spec = pl.BlockSpec((N_DEV, k, PAIR_COLS), lambda i: (0, 0, shard_col))
        dp_parts = list(dp)
        dp_specs = [pl.BlockSpec((tm, D_MODEL), lambda i: (i, 0)) for _ in dp_parts]
    np_ = len(dp_parts)
    per = D_MODEL // PAIR_COLS
    assert s % tm == 0
    has_res = res is not None

    def body(*refs):
        dp_refs = refs[:np_]
        if has_res:
            w_ref, x_ref, g_ref, r_ref, dx_ref, dg_ref = refs[np_:]
        else:
            w_ref, x_ref, g_ref, dx_ref, dg_ref = refs[np_:]

        @pl.when(pl.program_id(0) == 0)
        def _():
            dg_ref[...] = jnp.zeros_like(dg_ref)

        if shard_col is None:
            dxn = _dot_nt(dp_refs[0][...], w_ref[...])
        else:
            dxn = None
            for j in range(N_DEV):
                c0 = (j % per) * PAIR_COLS
                term = _dot_nt(dp_refs[j // per][:, c0:c0 + PAIR_COLS], w_ref[j])
                dxn = term if dxn is None else dxn + term
        xf = x_ref[...].astype(F32)
        rstd = lax.rsqrt(jnp.mean(xf * xf, axis=-1, keepdims=True) + NORM_EPS)
        xhat = xf * rstd
        dg_ref[...] += jnp.sum(dxn * xhat, axis=0, keepdims=True)
        dxh = dxn * g_ref[...]
        dx = rstd * (dxh - xhat * jnp.mean(dxh * xhat, axis=-1, keepdims=True))
        if has_res:
            dx = dx + r_ref[...]
        dx_ref[...] = dx.astype(out_dtype)

    in_specs = dp_specs + [w_spec,
                           pl.BlockSpec((tm, k), lambda i: (i, xcol)),
                           pl.BlockSpec((1, k), lambda i: (0, 0))]
    args = dp_parts + [w, x, g]
    if has_res:
        in_specs.append(pl.BlockSpec((tm, k), lambda i: (i, 0)))
        args.append(res)
    return pl.pallas_call(
        body, name=name, grid=(s // tm,),
        in_specs=in_specs,
        out_specs=[pl.BlockSpec((tm, k), lambda i: (i, 0)),
                   pl.BlockSpec((1, k), lambda i: (0, 0))],
        out_shape=[SDS((s, k), out_dtype), SDS((1, k), F32)],
        compiler_params=_cparams(("arbitrary",)),
    )(*args)


SB_LOG_CUTOFF = -80.0
SB_T = 256
SB_TK_LEFT = 128
SB_FWD_TILES = 4
SB_BWD_TILES = 2


def _sb_sweep(n_left, cs, rest, step):
    n = len(cs)

    def top(arrs):
        m = jnp.max(arrs[0])
        for a in arrs[1:]:
            m = jnp.maximum(m, jnp.max(a))
        return m

    def cond(st):
        return (st[0] <= n_left) & (st[1] > SB_LOG_CUTOFF)

    def body(st):
        cs2, rest2 = step(st[0], st[2:2 + n], st[2 + n:])
        return (st[0] + 1, top(cs2)) + tuple(cs2) + tuple(rest2)

    return lax.while_loop(cond, body, (jnp.int32(1), top(cs)) + tuple(cs) + tuple(rest))[2 + n:]


def _tile_masks(t):
    row = lax.broadcasted_iota(jnp.int32, (t, t), 0)
    col = lax.broadcasted_iota(jnp.int32, (t, t), 1)
    return row, col


def sb_attn_fwd(proj, name):
    s = proj.shape[0]
    n_tiles = SB_FWD_TILES
    t = min(SB_T, s)
    tko = min(SB_TK_LEFT, t)
    nq = s // t
    scale = 1.0 / math.sqrt(HALF)

    def body(q_ref, k_ref, v_ref, o_ref):
        lane = lax.broadcasted_iota(jnp.int32, (1, LANES), 1)
        row, col = _tile_masks(t)
        u_incl = (row >= col).astype(BF16)
        u_off = u_incl[:tko, :tko]
        tri = jnp.concatenate([col < row] * 2, axis=0)

        def scores(pr, qs, k0, tk):
            k = k_ref[pl.ds(k0, tk), pr * LANES:(pr + 1) * LANES]
            z = _dot_nt(qs, k)
            nz = -z
            lf = jnp.minimum(nz, 0.0) - jnp.log(1.0 + jnp.exp(jnp.minimum(z, nz)))
            return z, lf

        def accumulate(pr, k0, tk, z, lf, c, acc, diag):
            v = v_ref[pl.ds(k0, tk), pr * LANES:(pr + 1) * LANES]
            if diag:
                lf = jnp.where(tri, lf, 0.0)
            incl = _cumsum_dot(lf, u_incl if diag else u_off) + c
            a = jnp.exp(z + incl)
            if diag:
                a = jnp.where(tri, a, 0.0)
            acc = acc + _dot(a.astype(BF16), v)
            c = c + jnp.sum(lf, axis=1, keepdims=True)
            return c, acc

        def q_body(qi, carry):
            q0 = pl.multiple_of(qi * t, t)
            qss = []
            for pr in range(n_tiles):
                q = q_ref[pl.ds(q0, t), pr * LANES:(pr + 1) * LANES] * scale
                zq = jnp.zeros_like(q)
                qss.append(jnp.concatenate([jnp.where(lane < HALF, q, zq), jnp.where(lane >= HALF, q, zq)], axis=0))

            def step(jj, cs, accs, diag=False):
                k0, tk = (q0, t) if diag else (pl.multiple_of(q0 - jj * tko, tko), tko)
                outs = [accumulate(pr, k0, tk, *scores(pr, qss[pr], k0, tk), cs[pr], accs[pr], diag)
                        for pr in range(n_tiles)]
                return tuple(o[0] for o in outs), tuple(o[1] for o in outs)

            zero = (jnp.zeros((2 * t, 1), F32),) * n_tiles, (jnp.zeros((2 * t, LANES), F32),) * n_tiles
            cs, accs = step(0, zero[0], zero[1], True)
            accs = _sb_sweep(qi * (t // tko), cs, accs, step)
            for pr in range(n_tiles):
                o_ref[pl.ds(q0, t), pr * LANES:(pr + 1) * LANES] = jnp.where(lane < HALF, accs[pr][:t], accs[pr][t:])
            return carry

        lax.fori_loop(0, nq, q_body, 0)

    w = n_tiles * LANES
    return pl.pallas_call(
        body, name=name, grid=(D_MODEL // w,),
        in_specs=[pl.BlockSpec((s, w), lambda p: (0, p)),
                  pl.BlockSpec((s, w), lambda p: (0, D_MODEL // w + p)),
                  pl.BlockSpec((s, w), lambda p: (0, 2 * (D_MODEL // w) + p))],
        out_specs=pl.BlockSpec((s, w), lambda p: (0, p)),
        out_shape=SDS((s, D_MODEL), F32),
        compiler_params=_cparams(("parallel",), vmem_mb=56),
    )(proj, proj, proj)


def sb_attn_bwd(proj, o, do, name):
    s = proj.shape[0]
    n_tiles = SB_BWD_TILES
    t = min(SB_T, s)
    tko = min(SB_TK_LEFT, t)
    nq = s // t
    scale = 1.0 / math.sqrt(HALF)

    def body(q_ref, k_ref, v_ref, o_ref, do_ref, dq_ref, dk_ref, dv_ref, dk_s, dv_s):
        dk_s[...] = jnp.zeros_like(dk_s)
        dv_s[...] = jnp.zeros_like(dv_s)
        lane = lax.broadcasted_iota(jnp.int32, (1, LANES), 1)
        row, col = _tile_masks(t)
        u_incl = (row >= col).astype(BF16)
        u_excl = (row > col).astype(BF16)
        tri = jnp.concatenate([col < row] * 2, axis=0)

        def scores(pr, qs, dos, k0, tk):
            cols = slice(pr * LANES, (pr + 1) * LANES)
            z = _dot_nt(qs, k_ref[pl.ds(k0, tk), cols])
            nz = -z
            lf = jnp.minimum(nz, 0.0) - jnp.log(1.0 + jnp.exp(jnp.minimum(z, nz)))
            return z, lf, _dot_nt(dos, v_ref[pl.ds(k0, tk), cols])

        def accumulate(pr, qs, dos, tsum, k0, tk, z, lf, da, c, r, dq, diag):
            cols = slice(pr * LANES, (pr + 1) * LANES)
            ui, ue = (u_incl, u_excl) if diag else (u_incl[:tko, :tko], u_excl[:tko, :tko])
            sig = jnp.exp(z + lf)
            if diag:
                lf = jnp.where(tri, lf, 0.0)
            incl = _cumsum_dot(lf, ui) + c
            a = jnp.exp(z + incl)
            if diag:
                a = jnp.where(tri, a, 0.0)
            ab = a.astype(BF16)
            dl = ab.astype(F32) * da
            pre = tsum - r - _split_dot(dl, ue)
            dz = dl - sig * pre
            if diag:
                dz = jnp.where(tri, dz, 0.0)
            dzb = dz.astype(BF16)
            dq = dq + _dot(dzb, k_ref[pl.ds(k0, tk), cols])
            dk_s[pl.ds(k0, tk), cols] += _dot_tn(dzb, qs)
            dv_s[pl.ds(k0, tk), cols] += _dot_tn(ab, dos)
            c = c + jnp.sum(lf, axis=1, keepdims=True)
            r = r + jnp.sum(dl, axis=1, keepdims=True)
            return c, r, dq

        def q_body(qi, carry):
            q0 = pl.multiple_of(qi * t, t)
            lo, hi = lane < HALF, lane >= HALF
            qss, doss, tsums = [], [], []
            for pr in range(n_tiles):
                cols = slice(pr * LANES, (pr + 1) * LANES)
                q = q_ref[pl.ds(q0, t), cols] * scale
                dob = do_ref[pl.ds(q0, t), cols]
                prod = dob.astype(F32) * o_ref[pl.ds(q0, t), cols]
                zb = jnp.zeros_like(q)
                qss.append(jnp.concatenate([jnp.where(lo, q, zb), jnp.where(hi, q, zb)], axis=0))
                doss.append(jnp.concatenate([jnp.where(lo, dob, zb), jnp.where(hi, dob, zb)], axis=0))
                tsums.append(jnp.concatenate([jnp.sum(jnp.where(lo, prod, 0.0), axis=1, keepdims=True),
                                              jnp.sum(jnp.where(hi, prod, 0.0), axis=1, keepdims=True)], axis=0))

            def step(jj, cs, rest, diag=False):
                k0, tk = (q0, t) if diag else (pl.multiple_of(q0 - jj * tko, tko), tko)
                outs = [accumulate(pr, qss[pr], doss[pr], tsums[pr], k0, tk,
                                   *scores(pr, qss[pr], doss[pr], k0, tk),
                                   cs[pr], rest[pr], rest[n_tiles + pr], diag) for pr in range(n_tiles)]
                return tuple(o[0] for o in outs), tuple(o[1] for o in outs) + tuple(o[2] for o in outs)

            zc = (jnp.zeros((2 * t, 1), F32),) * n_tiles
            cs, rest = step(0, zc, zc + (jnp.zeros((2 * t, LANES), F32),) * n_tiles, True)
            rest = _sb_sweep(qi * (t // tko), cs, rest, step)
            for pr in range(n_tiles):
                dq = rest[n_tiles + pr]
                dq_ref[pl.ds(q0, t), pr * LANES:(pr + 1) * LANES] = (jnp.where(lo, dq[:t], dq[t:]) * scale).astype(BF16)
            return carry

        lax.fori_loop(0, nq, q_body, 0)
        dk_ref[...] = dk_s[...].astype(BF16)
        dv_ref[...] = dv_s[...].astype(BF16)

    w = n_tiles * LANES
    nblk = D_MODEL // w
    blk = lambda off: pl.BlockSpec((s, w), lambda p: (0, off * nblk + p))
    return pl.pallas_call(
        body, name=name, grid=(nblk,),
        in_specs=[blk(0), blk(1), blk(2), blk(0), blk(0)],
        out_specs=[blk(0), blk(0), blk(0)],
        out_shape=[SDS((s, D_MODEL), BF16)] * 3,
        scratch_shapes=[pltpu.VMEM((s, w), F32), pltpu.VMEM((s, w), F32)],
        compiler_params=_cparams(("parallel",), vmem_mb=56),
    )(proj, proj, proj, o, do)


MLA_HEADS = 8
MLA_QK = 192
MLA_SCALE = 1.0 / math.sqrt(MLA_QK)
MLA_TK = 512


def _rope_tables(s):
    inv_freq = 10000.0 ** (-jnp.arange(32, dtype=F32) / 32)
    ang = jnp.arange(s, dtype=F32)[:, None] * inv_freq[None, :]
    cos = jnp.tile(jnp.cos(ang), (1, 4))
    sin = jnp.sin(ang)
    sin_signed = jnp.tile(jnp.concatenate([-sin, sin], axis=1), (1, 2))
    return cos, sin_signed


def _partner_matrix():
    j = lax.broadcasted_iota(jnp.int32, (LANES, LANES), 0)
    l = lax.broadcasted_iota(jnp.int32, (LANES, LANES), 1)
    return (j == l + jnp.where((l % HALF) < 32, 32, -32)).astype(BF16)


def _partner(u, pmat):
    return _split_dot(u, pmat)


def _half_sum(x):
    j = lax.broadcasted_iota(jnp.int32, (LANES, LANES), 0)
    l = lax.broadcasted_iota(jnp.int32, (LANES, LANES), 1)
    return _dot(x.astype(BF16), ((j < HALF) == (l < HALF)).astype(BF16))


def _lane_sum(x):
    return _dot(x.astype(BF16), jnp.ones((LANES, LANES), BF16))


def _mla_head_inputs(h, qa_ref, kva_ref, kpe, lane):
    a = h % 2
    hm = (lane < HALF) if a == 0 else (lane >= HALF)
    qn = qa_ref[:, h * 128:(h + 1) * 128].astype(F32)
    qr = jnp.where(hm, qa_ref[:, 1024 + (h // 2) * 128:1024 + (h // 2 + 1) * 128].astype(F32), 0.0)
    kn = kva_ref[:, h * 256:h * 256 + 128].astype(F32)
    kr = kpe if a == 0 else pltpu.roll(kpe, HALF, 1)
    return qn, qr, kn, kr


def mla_prep_fwd(qa, kva, proj, gq, gk, cos, sin, name):
    s = qa.shape[0]
    tm = min(256, s)

    def body(qa_ref, kva_ref, kpe_ref, gq_ref, gk_ref, cos_ref, sin_ref, q_out, k_out):
        lane = lax.broadcasted_iota(jnp.int32, (1, LANES), 1)
        pmat = _partner_matrix()
        kpe = kpe_ref[...].astype(F32)
        cs, sn = cos_ref[...], sin_ref[...]
        for h in range(MLA_HEADS):
            qn, qr, kn, kr = _mla_head_inputs(h, qa_ref, kva_ref, kpe, lane)
            for xn, xr, g_ref, out, sc in ((qn, qr, gq_ref, q_out, MLA_SCALE), (kn, kr, gk_ref, k_out, 1.0)):
                ss = _lane_sum(xn * xn + xr * xr)
                rstd = lax.rsqrt(ss * (1.0 / MLA_QK) + NORM_EPS) * sc
                yn = xn * rstd * g_ref[:, 0:128]
                ur = xr * rstd * g_ref[:, 128:256]
                yr = ur * cs + _partner(ur, pmat) * sn
                out[:, h * 256:h * 256 + 128] = yn.astype(BF16)
                out[:, h * 256 + 128:h * 256 + 256] = yr.astype(BF16)

    row = lambda w, c: pl.BlockSpec((tm, w), lambda i: (i, c))
    full = lambda w: pl.BlockSpec((1, w), lambda i: (0, 0))
    return pl.pallas_call(
        body, name=name, grid=(s // tm,),
        in_specs=[row(1536, 0), row(2048, 0), row(128, 11), full(256), full(256), row(128, 0), row(128, 0)],
        out_specs=[row(2048, 0), row(2048, 0)],
        out_shape=[SDS((s, 2048), BF16)] * 2,
        compiler_params=_cparams(("parallel",)),
    )(qa, kva, proj, gq, gk, cos, sin)


def mla_prep_bwd(dq_p, dk_p, dv, qa, kva, proj, gq, gk, cos, sin, name):
    s = qa.shape[0]
    tm = min(256, s)

    def body(dq_ref, dk_ref, dv_ref, qa_ref, kva_ref, kpe_ref, gq_ref, gk_ref, cos_ref, sin_ref,
             dqa_ref, dkva_ref, dkpe_ref, dgq_ref, dgk_ref):
        @pl.when(pl.program_id(0) == 0)
        def _():
            dgq_ref[...] = jnp.zeros_like(dgq_ref)
            dgk_ref[...] = jnp.zeros_like(dgk_ref)

        lane = lax.broadcasted_iota(jnp.int32, (1, LANES), 1)
        pmat = _partner_matrix()
        kpe = kpe_ref[...].astype(F32)
        cs, sn = cos_ref[...], sin_ref[...]
        dkpe = jnp.zeros((tm, LANES), F32)
        dqr_pair = None
        for h in range(MLA_HEADS):
            qn, qr, kn, kr = _mla_head_inputs(h, qa_ref, kva_ref, kpe, lane)
            res = []
            for xn, xr, g_ref, d_ref, dg_ref, sc in ((qn, qr, gq_ref, dq_ref, dgq_ref, MLA_SCALE),
                                                     (kn, kr, gk_ref, dk_ref, dgk_ref, 1.0)):
                ss = _lane_sum(xn * xn + xr * xr)
                rstd = lax.rsqrt(ss * (1.0 / MLA_QK) + NORM_EPS)
                hn, hr = xn * rstd, xr * rstd
                dyn = d_ref[:, h * 256:h * 256 + 128] * sc
                dyr = d_ref[:, h * 256 + 128:h * 256 + 256] * sc
                dur = dyr * cs - _partner(dyr, pmat) * sn
                dg_ref[:, 0:128] += jnp.sum(dyn * hn, axis=0, keepdims=True)
                dg_ref[:, 128:256] += jnp.sum(dur * hr, axis=0, keepdims=True)
                dhn = dyn * g_ref[:, 0:128]
                dhr = dur * g_ref[:, 128:256]
                mu = _lane_sum(dhn * hn + dhr * hr) * (1.0 / MLA_QK)
                res.append((rstd * (dhn - hn * mu), rstd * (dhr - hr * mu)))
            (dqn, dqr), (dkn, dkr) = res
            dqa_ref[:, h * 128:(h + 1) * 128] = dqn.astype(BF16)
            if h % 2 == 0:
                dqr_pair = dqr
            else:
                dqa_ref[:, 1024 + (h // 2) * 128:1024 + (h // 2 + 1) * 128] = (dqr_pair + dqr).astype(BF16)
            dkva_ref[:, h * 256:h * 256 + 128] = dkn.astype(BF16)
            dkva_ref[:, h * 256 + 128:h * 256 + 256] = dv_ref[:, h * 128:(h + 1) * 128]
            dkpe = dkpe + (dkr if h % 2 == 0 else pltpu.roll(dkr, HALF, 1))
        dkpe_ref[...] = dkpe.astype(BF16)

    row = lambda w, c: pl.BlockSpec((tm, w), lambda i: (i, c))
    full = lambda w: pl.BlockSpec((1, w), lambda i: (0, 0))
    return pl.pallas_call(
        body, name=name, grid=(s // tm,),
        in_specs=[row(2048, 0), row(2048, 0), row(1024, 0), row(1536, 0), row(2048, 0), row(128, 11),
                  full(256), full(256), row(128, 0), row(128, 0)],
        out_specs=[row(1536, 0), row(2048, 0), row(128, 0), full(256), full(256)],
        out_shape=[SDS((s, 1536), BF16), SDS((s, 2048), BF16), SDS((s, 128), BF16),
                   SDS((1, 256), F32), SDS((1, 256), F32)],
        compiler_params=_cparams(("arbitrary",)),
    )(dq_p, dk_p, dv, qa, kva, proj, gq, gk, cos, sin)


def mla_attn_fwd(q_p, k_p, kva, name):
    s = q_p.shape[0]
    tk = min(MLA_TK, s)
    tq = min(512, s)
    r = tq // tk
    nq = s // tq

    def body(q_ref, k_ref, va_ref, vb_ref, o_ref, lse_ref):
        row = lax.broadcasted_iota(jnp.int32, (tq, tk), 0)
        col = lax.broadcasted_iota(jnp.int32, (tq, tk), 1)
        v_refs = (va_ref, vb_ref)

        def tile(hh, q, kj, m, l, acc, mask):
            k0 = pl.multiple_of(kj * tk, tk)
            sc = _dot_nt(q, k_ref[pl.ds(k0, tk), hh * 256:(hh + 1) * 256])
            if mask is not None:
                sc = jnp.where(mask, sc, -1e30)
            m_new = jnp.maximum(m, jnp.max(sc, axis=1, keepdims=True))
            alpha = jnp.exp(m - m_new)
            p = jnp.exp(sc - m_new)
            l = alpha * l + jnp.sum(p, axis=1, keepdims=True)
            acc = alpha * acc + _dot(p.astype(BF16), v_refs[hh][pl.ds(k0, tk), :])
            return m_new, l, acc

        def q_body(qi, carry):
            q0 = pl.multiple_of(qi * tq, tq)
            qs = [q_ref[pl.ds(q0, tq), hh * 256:(hh + 1) * 256] for hh in range(2)]

            def step(kj, cr, mask=None):
                out = ()
                for hh in range(2):
                    out += tile(hh, qs[hh], kj, cr[3 * hh], cr[3 * hh + 1], cr[3 * hh + 2], mask)
                return out

            cr = (jnp.full((tq, 1), -1e30, F32), jnp.zeros((tq, 1), F32), jnp.zeros((tq, LANES), F32)) * 2
            cr = lax.fori_loop(0, qi * r, step, cr)
            for d in range(r):
                cr = step(qi * r + d, cr, col + d * tk <= row)
            for hh in range(2):
                m, l, acc = cr[3 * hh:3 * hh + 3]
                o_ref[pl.ds(q0, tq), hh * 128:(hh + 1) * 128] = acc / l
                lse_ref[pl.ds(q0, tq), hh * 128:(hh + 1) * 128] = jnp.broadcast_to(m + jnp.log(l), (tq, LANES))
            return carry

        lax.fori_loop(0, nq, q_body, 0)

    return pl.pallas_call(
        body, name=name, grid=(MLA_HEADS // 2,),
        in_specs=[pl.BlockSpec((s, 512), lambda g: (0, g)),
                  pl.BlockSpec((s, 512), lambda g: (0, g)),
                  pl.BlockSpec((s, 128), lambda g: (0, 4 * g + 1)),
                  pl.BlockSpec((s, 128), lambda g: (0, 4 * g + 3))],
        out_specs=[pl.BlockSpec((s, 256), lambda g: (0, g)),
                   pl.BlockSpec((s, 256), lambda g: (0, g))],
        out_shape=[SDS((s, 1024), F32), SDS((s, 1024), F32)],
        compiler_params=_cparams(("parallel",), vmem_mb=56),
    )(q_p, k_p, kva, kva)


def mla_attn_bwd(q_p, k_p, kva, o, do, lse, name):
    s = q_p.shape[0]
    tk = min(MLA_TK, s)
    tq = min(512, s)
    r = tq // tk
    nq = s // tq

    def body(q_ref, k_ref, v_ref, o_ref, do_ref, lse_ref, dq_ref, dk_ref, dv_ref, dkt_s, dvt_s):
        dkt_s[...] = jnp.zeros_like(dkt_s)
        dvt_s[...] = jnp.zeros_like(dvt_s)
        row = lax.broadcasted_iota(jnp.int32, (tq, tk), 0)
        col = lax.broadcasted_iota(jnp.int32, (tq, tk), 1)

        def tile(q, qt, dob, dot_, lse, delta, kj, dq, mask):
            k0 = pl.multiple_of(kj * tk, tk)
            k = k_ref[pl.ds(k0, tk), :]
            v = v_ref[pl.ds(k0, tk), :]
            p = jnp.exp(_dot_nt(q, k) - lse)
            if mask is not None:
                p = jnp.where(mask, p, 0.0)
            ds = (p * (_dot_nt(dob, v) - delta)).astype(BF16)
            dq = dq + _dot(ds, k)
            dkt_s[:, pl.ds(k0, tk)] += _dot(qt, ds)
            dvt_s[:, pl.ds(k0, tk)] += _dot(dot_, p.astype(BF16))
            return dq

        def q_body(qi, carry):
            q0 = pl.multiple_of(qi * tq, tq)
            q = q_ref[pl.ds(q0, tq), :]
            dob = do_ref[pl.ds(q0, tq), :]
            qt = q.astype(F32).T.astype(BF16)
            dot_ = dob.astype(F32).T.astype(BF16)
            lse = lse_ref[pl.ds(q0, tq), 0:1]
            delta = jnp.sum(dob.astype(F32) * o_ref[pl.ds(q0, tq), :], axis=1, keepdims=True)
            dq = lax.fori_loop(0, qi * r, lambda kj, dq: tile(q, qt, dob, dot_, lse, delta, kj, dq, None),
                               jnp.zeros((tq, 256), F32))
            for d in range(r):
                dq = tile(q, qt, dob, dot_, lse, delta, qi * r + d, dq, col + d * tk <= row)
            dq_ref[pl.ds(q0, tq), :] = dq
            return carry

        lax.fori_loop(0, nq, q_body, 0)
        dk_ref[...] = dkt_s[...].T
        dv_ref[...] = dvt_s[...].T.astype(BF16)

    b256 = pl.BlockSpec((s, 256), lambda h: (0, h))
    b128 = pl.BlockSpec((s, 128), lambda h: (0, h))
    return pl.pallas_call(
        body, name=name, grid=(MLA_HEADS,),
        in_specs=[b256, b256, pl.BlockSpec((s, 128), lambda h: (0, 2 * h + 1)), b128, b128, b128],
        out_specs=[b256, b256, b128],
        out_shape=[SDS((s, 2048), F32), SDS((s, 2048), F32), SDS((s, 1024), BF16)],
        scratch_shapes=[pltpu.VMEM((256, s), F32), pltpu.VMEM((128, s), F32)],
        compiler_params=_cparams(("parallel",), vmem_mb=56),
    )(q_p, k_p, kva, o, do, lse)


SWA_HEADS = 16
SWA_BLOCK = 128


def _swa_head(tile_idx, half):
    return (2 * (tile_idx // 4) + half) * 4 + tile_idx % 4


SWA_Q_PERM = np.concatenate([np.arange(_swa_head(tt, a) * HALF, (_swa_head(tt, a) + 1) * HALF)
                             for tt in range(8) for a in range(2)])
SWA_SLOPES = [2.0 ** (-8.0 * (h + 1) / SWA_HEADS) for h in range(SWA_HEADS)]


def swa_prep_fwd(proj, g128, name):
    s = proj.shape[0]
    tm = min(512, s)

    def body(q_ref, k_ref, g_ref, o_ref):
        lane = lax.broadcasted_iota(jnp.int32, (1, LANES), 1)
        lo = lane < HALF
        for tt in range(10):
            if tt < 8:
                x = q_ref[:, tt * 128:(tt + 1) * 128].astype(F32)
            else:
                x = k_ref[:, (tt - 8) * 128:(tt - 7) * 128].astype(F32)
            rstd = lax.rsqrt(_half_sum(x * x) * (1.0 / HALF) + NORM_EPS)
            gi = 0 if tt < 8 else 1
            o_ref[:, tt * 128:(tt + 1) * 128] = (x * rstd * g_ref[gi:gi + 1, :]).astype(BF16)

    return pl.pallas_call(
        body, name=name, grid=(s // tm,),
        in_specs=[pl.BlockSpec((tm, 1024), lambda i: (i, 0)), pl.BlockSpec((tm, 256), lambda i: (i, 8)),
                  pl.BlockSpec((2, 128), lambda i: (0, 0))],
        out_specs=pl.BlockSpec((tm, 1280), lambda i: (i, 0)),
        out_shape=SDS((s, 1280), BF16),
        compiler_params=_cparams(("parallel",)),
    )(proj, proj, g128)


def swa_prep_bwd(dqn, dkn, proj, g128, name):
    s = proj.shape[0]
    tm = min(512, s)
    nsteps = s // tm

    def body(dq_ref, dk_ref, xq_ref, xk_ref, g_ref, o_ref, dg_ref):
        @pl.when(pl.program_id(0) == 0)
        def _():
            dg_ref[...] = jnp.zeros_like(dg_ref)

        lane = lax.broadcasted_iota(jnp.int32, (1, LANES), 1)
        lo = lane < HALF
        for tt in range(10):
            if tt < 8:
                x = xq_ref[:, tt * 128:(tt + 1) * 128].astype(F32)
                dy = dq_ref[:, tt * 128:(tt + 1) * 128].astype(F32)
            else:
                x = xk_ref[:, (tt - 8) * 128:(tt - 7) * 128].astype(F32)
                dy = dk_ref[:, (tt - 8) * 128:(tt - 7) * 128].astype(F32)
            gi = 0 if tt < 8 else 1
            rstd = lax.rsqrt(_half_sum(x * x) * (1.0 / HALF) + NORM_EPS)
            xh = x * rstd
            dg_ref[gi:gi + 1, :] += jnp.sum(dy * xh, axis=0, keepdims=True)
            dxh = dy * g_ref[gi:gi + 1, :]
            mu = _half_sum(dxh * xh) * (1.0 / HALF)
            o_ref[:, tt * 128:(tt + 1) * 128] = (rstd * (dxh - xh * mu)).astype(BF16)

        @pl.when(pl.program_id(0) == nsteps - 1)
        def _():
            acc = dg_ref[...]
            dg_ref[...] = acc + pltpu.roll(acc, HALF, 1)

    return pl.pallas_call(
        body, name=name, grid=(nsteps,),
        in_specs=[pl.BlockSpec((tm, 1024), lambda i: (i, 0)), pl.BlockSpec((tm, 256), lambda i: (i, 0)),
                  pl.BlockSpec((tm, 1024), lambda i: (i, 0)), pl.BlockSpec((tm, 256), lambda i: (i, 8)),
                  pl.BlockSpec((2, 128), lambda i: (0, 0))],
        out_specs=[pl.BlockSpec((tm, 1280), lambda i: (i, 0)), pl.BlockSpec((2, 128), lambda i: (0, 0))],
        out_shape=[SDS((s, 1280), BF16), SDS((2, 128), F32)],
        compiler_params=_cparams(("arbitrary",)),
    )(dqn, dkn, proj, proj, g128)


SWA_QSCALE = 1.0 / math.sqrt(HALF)


def _swa_bias():
    b = SWA_BLOCK
    row = jnp.arange(b)[:, None]
    col = jnp.arange(2 * b)[None, :]
    rel = row + b - col
    valid = (rel >= 0) & (rel < b)
    slopes = jnp.asarray([SWA_SLOPES[_swa_head(tt, a)] for tt in range(8) for a in range(2)], F32)
    bias = jnp.where(valid[None], -slopes[:, None, None] * rel[None].astype(F32), -1e30)
    return jnp.stack([jnp.where((col >= b)[None], bias, -1e30), bias])


def _swa_tile_inputs(tt, q_ref, bias_ref, sink_ref, lane):
    b = SWA_BLOCK
    qt = q_ref[:, tt * 128:(tt + 1) * 128] * SWA_QSCALE
    zq = jnp.zeros_like(qt)
    qs = jnp.concatenate([jnp.where(lane < HALF, qt, zq), jnp.where(lane >= HALF, qt, zq)], axis=0)
    bias = jnp.concatenate([bias_ref[0, 2 * tt], bias_ref[0, 2 * tt + 1]], axis=0)
    sink = jnp.concatenate([jnp.full((b, 1), sink_ref[_swa_head(tt, a)], F32) for a in range(2)], axis=0)
    return qs, bias, sink


def _swa_softmax(qs, kt, bias, sink):
    sc = _dot_nt(qs, kt) + bias
    m = jnp.maximum(jnp.max(sc, axis=1, keepdims=True), sink)
    e = jnp.exp(sc - m)
    es = jnp.exp(sink - m)
    inv = 1.0 / (jnp.sum(e, axis=1, keepdims=True) + es)
    return e * inv, es * inv


def _swa_specs(b):
    cur = lambda w, c: pl.BlockSpec((b, w), lambda n: (n, c))
    prev = lambda w, c: pl.BlockSpec((b, w), lambda n: (jnp.maximum(n - 1, 0), c))
    bias = pl.BlockSpec((1, SWA_HEADS, b, 2 * b), lambda n: (jnp.minimum(n, 1), 0, 0, 0))
    return cur, prev, bias


def swa_attn_fwd(qkn, proj, sinks, bias, name):
    s = qkn.shape[0]
    b = SWA_BLOCK

    def body(sink_ref, bias_ref, q_ref, kc_ref, kp_ref, vc_ref, vp_ref, o_ref):
        lane = lax.broadcasted_iota(jnp.int32, (1, LANES), 1)
        for tt in range(8):
            gp = tt // 4
            kt = jnp.concatenate([kp_ref[:, gp * 128:(gp + 1) * 128], kc_ref[:, gp * 128:(gp + 1) * 128]], axis=0)
            vt = jnp.concatenate([vp_ref[:, gp * 128:(gp + 1) * 128], vc_ref[:, gp * 128:(gp + 1) * 128]], axis=0)
            qt = q_ref[:, tt * 128:(tt + 1) * 128] * SWA_QSCALE
            outs = []
            for a in range(2):
                qm = jnp.where((lane < HALF) if a == 0 else (lane >= HALF), qt, jnp.zeros_like(qt))
                p, _ = _swa_softmax(qm, kt, bias_ref[0, 2 * tt + a], sink_ref[_swa_head(tt, a)])
                outs.append(_dot(p.astype(BF16), vt))
            o_ref[:, tt * 128:(tt + 1) * 128] = jnp.where(lane < HALF, outs[0], outs[1])

    cur, prev, bias_spec = _swa_specs(b)
    return pl.pallas_call(
        body, name=name, grid=(s // b,),
        in_specs=[pl.BlockSpec(memory_space=pltpu.SMEM), bias_spec, cur(1024, 0), cur(256, 4), prev(256, 4),
                  cur(256, 9), prev(256, 9)],
        out_specs=cur(1024, 0),
        out_shape=SDS((s, 1024), F32),
        compiler_params=_cparams(("parallel",)),
    )(sinks, bias, qkn, qkn, qkn, proj, proj)


def swa_attn_bwd(qkn, proj, sinks, bias, do, name):
    s = qkn.shape[0]
    b = SWA_BLOCK

    def body(sink_ref, bias_ref, q_ref, kc_ref, kp_ref, vc_ref, vp_ref, do_ref, dq_ref, dk_ref, dv_ref, ds_ref):
        n = pl.program_id(0)

        @pl.when(n == 0)
        def _():
            dk_ref[...] = jnp.zeros_like(dk_ref)
            dv_ref[...] = jnp.zeros_like(dv_ref)
            ds_ref[...] = jnp.zeros_like(ds_ref)

        lane = lax.broadcasted_iota(jnp.int32, (1, LANES), 1)
        cur0 = pl.multiple_of(n * b, b)
        prev0 = pl.multiple_of(jnp.maximum(n - 1, 0) * b, b)
        dsink = jnp.zeros((1, LANES), F32)
        for gp in range(2):
            kt = jnp.concatenate([kp_ref[:, gp * 128:(gp + 1) * 128], kc_ref[:, gp * 128:(gp + 1) * 128]], axis=0)
            vt = jnp.concatenate([vp_ref[:, gp * 128:(gp + 1) * 128], vc_ref[:, gp * 128:(gp + 1) * 128]], axis=0)
            dkt = jnp.zeros((LANES, 2 * b), F32)
            dvt = jnp.zeros((LANES, 2 * b), F32)
            for tt in range(4 * gp, 4 * gp + 4):
                qs, bias2, sink = _swa_tile_inputs(tt, q_ref, bias_ref, sink_ref, lane)
                dot_ = do_ref[:, tt * 128:(tt + 1) * 128]
                zd = jnp.zeros_like(dot_)
                dos = jnp.concatenate([jnp.where(lane < HALF, dot_, zd), jnp.where(lane >= HALF, dot_, zd)], axis=0)
                p, ps = _swa_softmax(qs, kt, bias2, sink)
                dp = _dot_nt(dos, vt)
                delta = jnp.sum(p * dp, axis=1, keepdims=True)
                dz = (p * (dp - delta)).astype(BF16)
                sd = ps * delta
                for a in range(2):
                    dsink = dsink + jnp.where(lane == _swa_head(tt, a), -jnp.sum(sd[a * b:(a + 1) * b]), 0.0)
                dq2 = _dot(dz, kt)
                dq_ref[:, tt * 128:(tt + 1) * 128] = (jnp.where(lane < HALF, dq2[:b], dq2[b:]) * SWA_QSCALE).astype(BF16)
                dkt = dkt + _dot(qs.astype(F32).T.astype(BF16), dz)
                dvt = dvt + _dot(dos.astype(F32).T.astype(BF16), p.astype(BF16))
            dkt, dvt = dkt.T, dvt.T
            cols = slice(gp * 128, (gp + 1) * 128)
            dk_ref[pl.ds(prev0, b), cols] += dkt[:b]
            dk_ref[pl.ds(cur0, b), cols] += dkt[b:]
            dv_ref[pl.ds(prev0, b), cols] += dvt[:b]
            dv_ref[pl.ds(cur0, b), cols] += dvt[b:]
        ds_ref[0:1, :] += dsink

    cur, prev, bias_spec = _swa_specs(b)
    full = pl.BlockSpec((s, 256), lambda n: (0, 0))
    return pl.pallas_call(
        body, name=name, grid=(s // b,),
        in_specs=[pl.BlockSpec(memory_space=pltpu.SMEM), bias_spec, cur(1024, 0), cur(256, 4), prev(256, 4),
                  cur(256, 9), prev(256, 9), cur(1024, 0)],
        out_specs=[cur(1024, 0), full, full, pl.BlockSpec((8, LANES), lambda n: (0, 0))],
        out_shape=[SDS((s, 1024), BF16), SDS((s, 256), F32), SDS((s, 256), F32), SDS((8, LANES), F32)],
        compiler_params=_cparams(("arbitrary",)),
    )(sinks, bias, qkn, qkn, qkn, proj, proj, do)


def _my_pos():
    return lax.axis_index("x"), lax.axis_index("y"), lax.axis_index("c")


ANY = pl.BlockSpec(memory_space=pl.ANY)


def all_gather_packed(xs, name):
    r, n = xs.shape

    def body(x_ref, out_ref, send_sems, recv_sems, local_sem):
        x, y, c = _my_pos()
        me, sibling = (x, y, c), (x, y, 1 - c)
        south = c == 0
        via = (jnp.where(south, 1 - x, x), jnp.where(south, y, 1 - y))
        far = (jnp.where(south, x, 1 - x), jnp.where(south, 1 - y, y))
        rk = jnp.where(south, 1, 2)
        diag = (1 - x, 1 - y)

        def slot(px, py, pc):
            return out_ref.at[4 * px + 2 * py + pc]

        def copy(k, block, to, src=None):
            return pltpu.make_async_remote_copy(
                src_ref=slot(*block) if src is None else src, dst_ref=slot(*block),
                send_sem=send_sems.at[k], recv_sem=recv_sems.at[k],
                device_id=to, device_id_type=MESH_ID)

        mine = pltpu.make_async_copy(x_ref, slot(*me), local_sem)
        mine.start()
        sent = [copy(0, me, sibling, src=x_ref), copy(1, me, (1 - x, y, c), src=x_ref),
                copy(2, me, (x, 1 - y, c), src=x_ref)]
        for cp in sent:
            cp.start()
        copy(rk, (*via, c), me).wait_recv()
        sent += [copy(3, (*via, c), (*far, c)), copy(3 + rk, (*via, c), sibling)]
        sent[-2].start()
        sent[-1].start()
        copy(3 - rk, (*far, c), me).wait_recv()
        sent.append(copy(6 - rk, (*far, c), sibling))
        sent[-1].start()
        copy(3, (*diag, c), me).wait_recv()
        sent.append(copy(6, (*diag, c), sibling))
        sent[-1].start()
        copy(0, sibling, me).wait_recv()
        for k, chip in ((4, (1 - x, y)), (5, (x, 1 - y)), (6, diag)):
            copy(k, (*chip, 1 - c), me).wait_recv()
        for cp in sent:
            cp.wait_send()
        mine.wait()

    return pl.pallas_call(
        body, name=name,
        out_shape=SDS((N_DEV, r, n), xs.dtype),
        in_specs=[ANY], out_specs=ANY,
        scratch_shapes=[pltpu.SemaphoreType.DMA((7,)), pltpu.SemaphoreType.DMA((7,)), pltpu.SemaphoreType.DMA],
    )(xs)


def rs_sibling_exchange(g, name):
    _, r, n = g.shape

    def body(g_ref, ra_ref, send_sems, recv_sems):
        x, y, c = _my_pos()
        copies = [pltpu.make_async_remote_copy(
            src_ref=g_ref.at[2 * chip + (1 - c)], dst_ref=ra_ref.at[chip],
            send_sem=send_sems.at[chip], recv_sem=recv_sems.at[chip],
            device_id=(x, y, 1 - c), device_id_type=MESH_ID) for chip in range(4)]
        for cp in copies:
            cp.start()
        for cp in copies:
            cp.wait()

    return pl.pallas_call(
        body, name=name, out_shape=SDS((4, r, n), g.dtype), in_specs=[ANY], out_specs=ANY,
        scratch_shapes=[pltpu.SemaphoreType.DMA((4,)), pltpu.SemaphoreType.DMA((4,))],
    )(g)


def rs_pair_add(g, ra, cidx, name):
    _, r, n = g.shape
    tr = r // 2

    def body(c_ref, g_ref, ra_ref, p_ref):
        p_ref[...] = (g_ref[...].astype(F32) + ra_ref[...].astype(F32)).astype(p_ref.dtype)

    return pl.pallas_call(
        body, name=name,
        grid_spec=pltpu.PrefetchScalarGridSpec(
            num_scalar_prefetch=1, grid=(4, r // tr),
            in_specs=[pl.BlockSpec((1, tr, n), lambda i, j, c: (2 * i + c[0], j, 0)),
                      pl.BlockSpec((1, tr, n), lambda i, j, c: (i, j, 0))],
            out_specs=pl.BlockSpec((1, tr, n), lambda i, j, c: (i, j, 0))),
        out_shape=SDS((4, r, n), g.dtype),
        compiler_params=_cparams(("parallel", "parallel")),
    )(cidx, g, ra)


def rs_cross_exchange(p, name):
    _, r, n = p.shape

    def body(p_ref, rb_ref, send_sems, recv_sems):
        x, y, c = _my_pos()
        copies = []
        for k, (dx, dy) in enumerate(((1, 0), (0, 1), (1, 1))):
            tx = 1 - x if dx else x
            ty = 1 - y if dy else y
            copies.append(pltpu.make_async_remote_copy(
                src_ref=p_ref.at[2 * tx + ty], dst_ref=rb_ref.at[k],
                send_sem=send_sems.at[k], recv_sem=recv_sems.at[k],
                device_id=(tx, ty, c), device_id_type=MESH_ID))
        for cp in copies:
            cp.start()
        for cp in copies:
            cp.wait()

    return pl.pallas_call(
        body, name=name, out_shape=SDS((3, r, n), p.dtype), in_specs=[ANY], out_specs=ANY,
        scratch_shapes=[pltpu.SemaphoreType.DMA((3,)), pltpu.SemaphoreType.DMA((3,))],
    )(p)


def _adamw(w, g, m, v):
    m = ADAM_B1 * m + (1.0 - ADAM_B1) * g
    v = ADAM_B2 * v + (1.0 - ADAM_B2) * (g * g)
    m_hat = m / (1.0 - ADAM_B1 ** ADAM_STEP)
    v_hat = v / (1.0 - ADAM_B2 ** ADAM_STEP)
    delta = -ADAM_LR * (m_hat / (jnp.sqrt(v_hat) + ADAM_EPS) + ADAM_WD * w)
    return delta, m, v


def rs_final_adamw(p, rb, chipidx, w, m, v, row0, col_blk, name):
    r, n = w.shape
    tr = min(r, 256 if n <= PAIR_COLS else 128)
    assert r % tr == 0 and row0 % tr == 0
    rb0 = row0 // tr

    def body(c_ref, p_ref, rb_ref, w_ref, m_ref, v_ref, g_out, d_out, m_out, v_out):
        g = p_ref[0].astype(F32)
        for k in range(3):
            g = g + rb_ref[k].astype(F32)
        d, mn, vn = _adamw(w_ref[...], g, m_ref[...], v_ref[...])
        g_out[...] = g
        d_out[...] = d
        m_out[...] = mn
        v_out[...] = vn

    blk = pl.BlockSpec((tr, n), lambda j, c: (j, 0))
    return pl.pallas_call(
        body, name=name,
        grid_spec=pltpu.PrefetchScalarGridSpec(
            num_scalar_prefetch=1, grid=(r // tr,),
            in_specs=[pl.BlockSpec((1, tr, n), lambda j, c: (c[0], rb0 + j, col_blk)),
                      pl.BlockSpec((3, tr, n), lambda j, c: (0, rb0 + j, col_blk)), blk, blk, blk],
            out_specs=[blk, blk, blk, blk]),
        out_shape=[SDS((r, n), F32)] * 4,
        compiler_params=_cparams(("parallel",)),
    )(chipidx, p, rb, w, m, v)


def small_allreduce_adamw(g, w, m, v, name):
    rows = g.shape[0]

    def body(g_ref, w_ref, m_ref, v_ref, g_out, d_out, m_out, v_out, buf, send_sems, recv_sems):
        x, y, c = _my_pos()
        my = 4 * x + 2 * y + c
        buf[my] = g_ref[...]
        copies = []
        for k in range(1, N_DEV):
            dx, dy, dc = (k >> 2) & 1, (k >> 1) & 1, k & 1
            tgt = (1 - x if dx else x, 1 - y if dy else y, 1 - c if dc else c)
            copies.append(pltpu.make_async_remote_copy(
                src_ref=buf.at[my], dst_ref=buf.at[my],
                send_sem=send_sems.at[k - 1], recv_sem=recv_sems.at[k - 1],
                device_id=tgt, device_id_type=MESH_ID))
        for cp in copies:
            cp.start()
        for cp in copies:
            cp.wait()
        tot = buf[0]
        for j in range(1, N_DEV):
            tot = tot + buf[j]
        d, mn, vn = _adamw(w_ref[...], tot, m_ref[...], v_ref[...])
        g_out[...] = tot
        d_out[...] = d
        m_out[...] = mn
        v_out[...] = vn

    vm = pl.BlockSpec(memory_space=pltpu.VMEM)
    return pl.pallas_call(
        body, name=name, out_shape=[SDS((rows, LANES), F32)] * 4,
        in_specs=[vm] * 4, out_specs=[vm] * 4,
        scratch_shapes=[pltpu.VMEM((N_DEV, rows, LANES), F32),
                        pltpu.SemaphoreType.DMA((N_DEV - 1,)), pltpu.SemaphoreType.DMA((N_DEV - 1,))],
    )(g, w, m, v)


BIG = ["l0_w_in", "l0_w_out", "l1_w_in", "l1_w_uq", "l1_w_ukv", "l1_w_out", "l2_w_in", "l2_w_out",
       "l3_w_in", "l3_w_out"]
SMALL = ["l0_norm", "l1_norm", "l1_q_a_norm", "l1_kv_a_norm", "l1_q_head_norm", "l1_k_head_norm",
         "l2_norm", "l2_q_head_norm", "l2_k_head_norm", "l2_sinks", "l3_norm"]

MLA_UQ_PERM = np.concatenate([np.arange(h * 192, h * 192 + 128) for h in range(8)]
                             + [np.arange(h * 192 + 128, (h + 1) * 192) for h in range(8)])
SWA_IN_PERM = np.concatenate([SWA_Q_PERM, 1536 + SWA_Q_PERM, np.arange(1024, 1536)])


def _inv(perm):
    inv = np.empty_like(perm)
    inv[perm] = np.arange(perm.size)
    return inv


OUTS = ["l0_w_out", "l1_w_out", "l2_w_out", "l3_w_out"]
REST = ["l1_w_in", "l1_w_uq", "l1_w_ukv", "l2_w_in"]


def _pack_rest(shards):
    parts = [shards[nm].reshape(-1, D_MODEL) for nm in REST]
    used = sum(p.shape[0] for p in parts)
    assert REST_ROW0 + used == ROWS_USED
    parts.append(jnp.zeros((REST_ROWS - used, D_MODEL), parts[0].dtype))
    return jnp.concatenate(parts, axis=0)


def _unpack_rest(packed, shapes):
    out, off = {}, 0
    for nm in REST:
        shp = shapes[nm]
        rows = shp[0] * shp[1] // D_MODEL
        out[nm] = packed[off:off + rows].reshape(shp)
        off += rows
    return out


def _pack(shards):
    pair = jnp.concatenate([shards["l0_w_in"], shards["l3_w_in"]], axis=1)
    return jnp.concatenate([pair] + [shards[nm] for nm in OUTS] + [_pack_rest(shards)], axis=0)


def _out_weight(gathered, layer):
    r0 = OUT_ROW0 + 128 * layer
    return gathered[:, r0:r0 + 128].reshape(D_MODEL, D_MODEL)


def _rest_full_weights(gathered, shapes):
    out, off = {}, REST_ROW0
    for nm in REST:
        shp = shapes[nm]
        rows = shp[0] * shp[1] // D_MODEL
        blk = gathered[:, off:off + rows].reshape((N_DEV,) + shp)
        out[nm] = jnp.transpose(blk, (1, 0, 2)).reshape(shp[0], N_DEV * shp[1])
        off += rows
    return out


def _grad_pieces(pair_l0, pair_l3, dw_out, rest_grads, shapes):
    parts = [jnp.concatenate([pair_l0, pair_l3], axis=2)]
    parts += [g.reshape(N_DEV, 128, D_MODEL).astype(BF16) for g in dw_out]
    used = REST_ROW0
    for nm in REST:
        shp = shapes[nm]
        blk = jnp.transpose(rest_grads[nm].reshape(shp[0], N_DEV, shp[1]), (1, 0, 2))
        parts.append(blk.reshape(N_DEV, -1, D_MODEL).astype(BF16))
        used += parts[-1].shape[1]
    parts.append(jnp.zeros((N_DEV, ROWS_PACK - used, D_MODEL), BF16))
    return jnp.concatenate(parts, axis=1)


SMALL_USED = 5008


def _pack_small(vals, extra=None):
    parts = [vals[nm].reshape(-1).astype(F32) for nm in SMALL]
    assert sum(p.shape[0] for p in parts) == SMALL_USED
    if extra is not None:
        parts.append(extra.reshape(1).astype(F32))
    flat = jnp.concatenate(parts)
    flat = jnp.concatenate([flat, jnp.zeros((SMALL_ROWS * LANES - flat.shape[0],), F32)])
    return flat.reshape(SMALL_ROWS, LANES)


def _unpack_small(packed, shapes):
    flat = packed.reshape(-1)
    out, off = {}, 0
    for nm in SMALL:
        n = shapes[nm][0]
        out[nm] = flat[off:off + n]
        off += n
    return out


def _row(v):
    return v.reshape(1, -1).astype(F32)


def _mla_gain_rows(g):
    return jnp.concatenate([g[:128], g[128:], g[128:]]).reshape(1, 256).astype(F32)


def gate_out_loss(o, proj, gate_blk, w_out, x, target, name):
    s = x.shape[0]
    d = D_MODEL
    tm = min(512, s)

    def body(o_ref, g_ref, w_ref, x_ref, t_ref, l_ref, dy_ref):
        @pl.when(pl.program_id(0) == 0)
        def _():
            l_ref[...] = jnp.zeros_like(l_ref)

        g = g_ref[...].astype(F32)
        hg = (o_ref[...].astype(F32) * (g * _sigmoid(g))).astype(BF16)
        e = x_ref[...] + _dot(hg, w_ref[...]) - t_ref[...]
        dy_ref[...] = e * (1.0 / d)
        l_ref[...] += 0.5 * jnp.sum(jnp.mean(e * e, axis=-1, keepdims=True))

    blk = pl.BlockSpec((tm, d), lambda i: (i, 0))
    return pl.pallas_call(
        body, name=name, grid=(s // tm,),
        in_specs=[blk, pl.BlockSpec((tm, d), lambda i: (i, gate_blk)), pl.BlockSpec((d, d), lambda i: (0, 0)),
                  blk, blk],
        out_specs=[pl.BlockSpec((8, LANES), lambda i: (0, 0)), blk],
        out_shape=[SDS((8, LANES), F32), SDS((s, d), F32)],
        compiler_params=_cparams(("arbitrary",)),
    )(o, proj, w_out, x, target)


def _layer_fwd_sb(x, norm, gathered, shard_col, w_out, tag, target=None):
    proj, xn = norm_matmul(x, 0, _row(norm), gathered, f"{tag}_in", shard_col=shard_col)
    o = sb_attn_fwd(proj, f"{tag}_attn")
    if target is None:
        y = gate_out_fwd(o, proj, 3, w_out, x, f"{tag}_out")
    else:
        y = gate_out_loss(o, proj, 3, w_out, x, target, f"{tag}_out_loss")
    return y, (x, xn, proj, o)


def _layer_bwd_sb(dy, saved, norm, gathered, shard_col, w_out, tag):
    x, xn, proj, o = saved
    do, dgate, hg = gate_out_bwd(dy, w_out, o, proj, 3, f"{tag}_outb")
    dw_out = tn_matmul(hg, dy, f"{tag}_dwout")
    dq, dk, dv = sb_attn_bwd(proj, o, do, f"{tag}_attnb")
    dproj = [dq, dk, dv, dgate]
    dw_in = tn_matmul_shards(xn, dproj, f"{tag}_dwin")
    dx, dnorm = nt_norm_bwd(dproj, gathered, x, 0, _row(norm), dy, f"{tag}_inb", F32, shard_col=shard_col)
    return dx, dw_in, dw_out, dnorm.reshape(-1)


def kernel(x, l0_norm, l0_w_in, l0_w_out, l1_norm, l1_w_in, l1_q_a_norm, l1_w_uq, l1_kv_a_norm, l1_w_ukv, l1_q_head_norm, l1_k_head_norm, l1_w_out, l2_norm, l2_w_in, l2_q_head_norm, l2_k_head_norm, l2_sinks, l2_w_out, l3_norm, l3_w_in, l3_w_out, loss_target, m_l0_norm, m_l0_w_in, m_l0_w_out, m_l1_norm, m_l1_w_in, m_l1_q_a_norm, m_l1_w_uq, m_l1_kv_a_norm, m_l1_w_ukv, m_l1_q_head_norm, m_l1_k_head_norm, m_l1_w_out, m_l2_norm, m_l2_w_in, m_l2_q_head_norm, m_l2_k_head_norm, m_l2_sinks, m_l2_w_out, m_l3_norm, m_l3_w_in, m_l3_w_out, v_l0_norm, v_l0_w_in, v_l0_w_out, v_l1_norm, v_l1_w_in, v_l1_q_a_norm, v_l1_w_uq, v_l1_kv_a_norm, v_l1_w_ukv, v_l1_q_head_norm, v_l1_k_head_norm, v_l1_w_out, v_l2_norm, v_l2_w_in, v_l2_q_head_norm, v_l2_k_head_norm, v_l2_sinks, v_l2_w_out, v_l3_norm, v_l3_w_in, v_l3_w_out):
    loc = dict(locals())
    names = BIG + SMALL
    w = {nm: loc[nm] for nm in names}
    mom = {nm: loc["m_" + nm] for nm in names}
    vel = {nm: loc["v_" + nm] for nm in names}
    shapes = {nm: w[nm].shape for nm in names}
    xs = x[0]
    tgt = loss_target[0]

    gathered = all_gather_packed(_pack({nm: w[nm].astype(BF16) for nm in BIG}), "ag_weights")
    loss_local, dx0, pieces, sg = _local_step(xs, tgt, gathered, {nm: w[nm] for nm in SMALL}, shapes)
    return _reduce_and_update(loss_local, dx0, pieces, sg, w, mom, vel, shapes)


def _local_step(xs, tgt, gathered, sm, shapes):
    s = xs.shape[0]
    fw = _rest_full_weights(gathered, shapes)
    w_out = [_out_weight(gathered, layer) for layer in range(4)]
    l0_norm, l1_norm, l2_norm, l3_norm = sm["l0_norm"], sm["l1_norm"], sm["l2_norm"], sm["l3_norm"]
    l1_q_a_norm, l1_kv_a_norm = sm["l1_q_a_norm"], sm["l1_kv_a_norm"]
    l1_q_head_norm, l1_k_head_norm = sm["l1_q_head_norm"], sm["l1_k_head_norm"]
    l2_q_head_norm, l2_k_head_norm, l2_sinks = sm["l2_q_head_norm"], sm["l2_k_head_norm"], sm["l2_sinks"]
    w1_in = jnp.concatenate([fw["l1_w_in"][:, 448:], fw["l1_w_in"][:, :448], jnp.zeros((D_MODEL, 64), BF16)], axis=1)
    w1_uq = fw["l1_w_uq"][:, MLA_UQ_PERM]
    w2_in = fw["l2_w_in"][:, SWA_IN_PERM]
    w2_out = w_out[2][SWA_Q_PERM, :]

    x1, sv0 = _layer_fwd_sb(xs, l0_norm, gathered, 0, w_out[0], "l0")

    cos, sin = _rope_tables(s)
    gq, gk = _mla_gain_rows(l1_q_head_norm), _mla_gain_rows(l1_k_head_norm)
    proj1, xn1 = norm_matmul(x1, 0, _row(l1_norm), w1_in, "l1_in")
    qa, qln = norm_matmul(proj1, 4, _row(l1_q_a_norm), w1_uq, "l1_uq")
    kva, kvn = norm_matmul(proj1, 10, _row(l1_kv_a_norm), fw["l1_w_ukv"], "l1_ukv")
    q_p, k_p = mla_prep_fwd(qa, kva, proj1, gq, gk, cos, sin, "l1_prep")
    o1, lse1 = mla_attn_fwd(q_p, k_p, kva, "l1_attn")
    x2 = gate_out_fwd(o1, proj1, 0, w_out[1], x1, "l1_out")

    g2 = jnp.stack([jnp.tile(l2_q_head_norm, 2), jnp.tile(l2_k_head_norm, 2)]).astype(F32)
    proj2, xn2 = norm_matmul(x2, 0, _row(l2_norm), w2_in, "l2_in")
    qkn2 = swa_prep_fwd(proj2, g2, "l2_prep")
    bias2 = _swa_bias()
    o2 = swa_attn_fwd(qkn2, proj2, l2_sinks, bias2, "l2_attn")
    x3 = gate_out_fwd(o2, proj2, 1, w2_out, x2, "l2_out")

    (lossblk, dy), sv3 = _layer_fwd_sb(x3, l3_norm, gathered, 1, w_out[3], "l3", target=tgt)

    fg, sg = {}, {}
    dw_out = [None] * 4
    dx3, pair_l3, dw_out[3], sg["l3_norm"] = _layer_bwd_sb(dy, sv3, l3_norm, gathered, 1, w_out[3], "l3")

    do2, dgate2, hg2 = gate_out_bwd(dx3, w2_out, o2, proj2, 1, "l2_outb")
    dw2_out = tn_matmul(hg2, dx3, "l2_dwout")
    dqn2, dkn2, dv2, dsink2 = swa_attn_bwd(qkn2, proj2, l2_sinks, bias2, do2, "l2_attnb")
    dqk2, dg2 = swa_prep_bwd(dqn2, dkn2, proj2, g2, "l2_prepb")
    dproj2 = jnp.concatenate([dqk2[:, :1024], dgate2, dqk2[:, 1024:], dv2.astype(BF16)], axis=1)
    dw2_in = tn_matmul(xn2, dproj2, "l2_dwin")
    dx2, dn2 = nt_norm_bwd(dproj2, w2_in, x2, 0, _row(l2_norm), dx3, "l2_inb", F32)
    fg["l2_w_in"] = dw2_in[:, _inv(SWA_IN_PERM)]
    dw_out[2] = dw2_out[_inv(SWA_Q_PERM), :]
    sg["l2_norm"] = dn2.reshape(-1)
    sg["l2_q_head_norm"] = dg2[0, :HALF]
    sg["l2_k_head_norm"] = dg2[1, :HALF]
    sg["l2_sinks"] = dsink2[0, :SWA_HEADS]

    do1, dgate1, hg1 = gate_out_bwd(dx2, w_out[1], o1, proj1, 0, "l1_outb")
    dw_out[1] = tn_matmul(hg1, dx2, "l1_dwout")
    dq_p, dk_p, dv1 = mla_attn_bwd(q_p, k_p, kva, o1, do1, lse1, "l1_attnb")
    dqa, dkva, dkpe, dgq, dgk = mla_prep_bwd(dq_p, dk_p, dv1, qa, kva, proj1, gq, gk, cos, sin, "l1_prepb")
    dw_uq = tn_matmul(qln, dqa, "l1_dwuq")
    fg["l1_w_ukv"] = tn_matmul(kvn, dkva, "l1_dwukv")
    dqlat, dgqa = nt_norm_bwd(dqa, w1_uq, proj1, 4, _row(l1_q_a_norm), None, "l1_uqb", BF16)
    dkvlat, dgkva = nt_norm_bwd(dkva, fw["l1_w_ukv"], proj1, 10, _row(l1_kv_a_norm), None, "l1_ukvb", BF16)
    dproj1 = jnp.concatenate([dgate1, dqlat, dkvlat, dkpe], axis=1)
    dw1_in = tn_matmul(xn1, dproj1, "l1_dwin")
    dx1, dn1 = nt_norm_bwd(dproj1, w1_in, x1, 0, _row(l1_norm), dx2, "l1_inb", F32)
    fg["l1_w_in"] = jnp.concatenate([dw1_in[:, 1024:1472], dw1_in[:, :1024]], axis=1)
    fg["l1_w_uq"] = dw_uq[:, _inv(MLA_UQ_PERM)]
    sg["l1_norm"] = dn1.reshape(-1)
    sg["l1_q_a_norm"] = dgqa.reshape(-1)
    sg["l1_kv_a_norm"] = dgkva.reshape(-1)
    sg["l1_q_head_norm"] = jnp.concatenate([dgq[0, :128], dgq[0, 128:192] + dgq[0, 192:256]])
    sg["l1_k_head_norm"] = jnp.concatenate([dgk[0, :128], dgk[0, 128:192] + dgk[0, 192:256]])

    dx0, pair_l0, dw_out[0], sg["l0_norm"] = _layer_bwd_sb(dx1, sv0, l0_norm, gathered, 0, w_out[0], "l0")
    return lossblk[0, 0], dx0, _grad_pieces(pair_l0, pair_l3, dw_out, fg, shapes), sg


def _reduce_and_update(loss_local, dx0, pieces, sg, w, mom, vel, shapes):
    xi, yi, ci = _my_pos()
    cidx = jnp.reshape(ci, (1,)).astype(jnp.int32)
    chipidx = jnp.reshape(2 * xi + yi, (1,)).astype(jnp.int32)
    ra = rs_sibling_exchange(pieces, "rs_sibling")
    part = rs_pair_add(pieces, ra, cidx, "rs_pair_add")
    rb = rs_cross_exchange(part, "rs_cross")

    def update(nm, row0, col_blk):
        return rs_final_adamw(part, rb, chipidx, w[nm], mom[nm], vel[nm], row0, col_blk, f"adamw_{nm}")

    big = {"l0_w_in": update("l0_w_in", 0, 0), "l3_w_in": update("l3_w_in", 0, 1)}
    for layer, nm in enumerate(OUTS):
        big[nm] = update(nm, OUT_ROW0 + 128 * layer, 0)
    rest = rs_final_adamw(part, rb, chipidx, _pack_rest(w), _pack_rest(mom), _pack_rest(vel), REST_ROW0, 0,
                          "adamw_rest")
    gs, ds, ms, vs = small_allreduce_adamw(_pack_small(sg, loss_local), _pack_small({nm: w[nm] for nm in SMALL}),
                                           _pack_small({nm: mom[nm] for nm in SMALL}),
                                           _pack_small({nm: vel[nm] for nm in SMALL}), "small_allreduce")

    outs = []
    for i, small in enumerate((gs, ds, ms, vs)):
        d = {nm: big[nm][i] for nm in big}
        d.update(_unpack_rest(rest[i], shapes))
        d.update(_unpack_small(small, shapes))
        outs.append(d)
    order = ["l0_norm", "l0_w_in", "l0_w_out", "l1_norm", "l1_w_in", "l1_q_a_norm", "l1_w_uq", "l1_kv_a_norm",
             "l1_w_ukv", "l1_q_head_norm", "l1_k_head_norm", "l1_w_out", "l2_norm", "l2_w_in", "l2_q_head_norm",
             "l2_k_head_norm", "l2_sinks", "l2_w_out", "l3_norm", "l3_w_in", "l3_w_out"]
    flat = [gs.reshape(-1)[SMALL_USED], dx0[None]]
    for d in outs:
        flat += [d[nm] for nm in order]
    return tuple(flat)
```

```python
import math

import numpy as np
import jax
import jax.numpy as jnp
from jax import lax
from jax.experimental import pallas as pl
from jax.experimental.pallas import tpu as pltpu

F32 = jnp.float32
BF16 = jnp.bfloat16
SDS = jax.ShapeDtypeStruct
MESH_ID = pl.DeviceIdType.MESH

D_MODEL = 1024
NORM_EPS = 1e-6
N_DEV = 8
LANES = 128
HALF = 64
ROWS_PACK = 2176
ROWS_USED = 2120
PAIR_COLS = 512
OUT_ROW0 = 1024
REST_ROW0 = 1536
REST_ROWS = ROWS_PACK - REST_ROW0
SMALL_ROWS = 40

ADAM_LR = 0.001
ADAM_B1 = 0.9
ADAM_B2 = 0.999
ADAM_EPS = 1e-08
ADAM_WD = 0.01
ADAM_STEP = 10

NT_DIMS = (((1,), (1,)), ((), ()))
TN_DIMS = (((0,), (0,)), ((), ()))


def _cparams(sem=None, vmem_mb=48):
    return pltpu.CompilerParams(dimension_semantics=sem, vmem_limit_bytes=vmem_mb * 2 ** 20)


def _dot(a, b):
    return jnp.dot(a, b, preferred_element_type=F32)


def _dot_nt(a, b):
    return lax.dot_general(a, b, NT_DIMS, preferred_element_type=F32)


def _dot_tn(a, b):
    return lax.dot_general(a, b, TN_DIMS, preferred_element_type=F32)


def _split_dot(a, b):
    hi = a.astype(BF16)
    lo = (a - hi.astype(F32)).astype(BF16)
    return _dot(hi, b) + _dot(lo, b)


def _cumsum_dot(lf, u):
    return _dot(lf.astype(BF16), u)


def _pick_tile(n, cap):
    return max(t for t in range(LANES, min(n, cap) + 1, LANES) if n % t == 0)


def _sigmoid(x):
    return 1.0 / (1.0 + jnp.exp(-x))


def norm_matmul(x, colblk, g, w, name, shard_col=None):
    s = x.shape[0]
    if shard_col is None:
        k, n = w.shape
        tn = _pick_tile(n, 1024)
        w_spec = pl.BlockSpec((k, tn), lambda i, j: (0, j))
    else:
        k, n, tn = D_MODEL, N_DEV * PAIR_COLS, 4 * PAIR_COLS
        w_spec = pl.BlockSpec((4, k, PAIR_COLS), lambda i, j: (j, 0, shard_col))
    tm = min(1024, s)
    assert s % tm == 0

    def body(x_ref, g_ref, w_ref, o_ref, xn_ref, xn_s):
        @pl.when(pl.program_id(1) == 0)
        def _():
            xf = x_ref[...].astype(F32)
            ms = jnp.mean(xf * xf, axis=-1, keepdims=True)
            y = (xf * lax.rsqrt(ms + NORM_EPS) * g_ref[...]).astype(BF16)
            xn_s[...] = y
            xn_ref[...] = y

        if shard_col is None:
            o_ref[...] = _dot(xn_s[...], w_ref[...]).astype(o_ref.dtype)
        else:
            for half in range(4):
                o_ref[:, half * PAIR_COLS:(half + 1) * PAIR_COLS] = _dot(xn_s[...], w_ref[half]).astype(o_ref.dtype)

    return pl.pallas_call(
        body, name=name, grid=(s // tm, n // tn),
        in_specs=[pl.BlockSpec((tm, k), lambda i, j: (i, colblk)),
                  pl.BlockSpec((1, k), lambda i, j: (0, 0)),
                  w_spec],
        out_specs=[pl.BlockSpec((tm, tn), lambda i, j: (i, j)),
                   pl.BlockSpec((tm, k), lambda i, j: (i, 0))],
        out_shape=[SDS((s, n), BF16), SDS((s, k), BF16)],
        scratch_shapes=[pltpu.VMEM((tm, k), BF16)],
        compiler_params=_cparams(("parallel", "arbitrary")),
    )(x, g, w)


def gate_out_fwd(o, proj, gate_blk, w_out, x, name):
    s = x.shape[0]
    d = D_MODEL
    tm = min(512, s)

    def body(o_ref, g_ref, w_ref, x_ref, y_ref):
        g = g_ref[...].astype(F32)
        hg = (o_ref[...].astype(F32) * (g * _sigmoid(g))).astype(BF16)
        y_ref[...] = x_ref[...] + _dot(hg, w_ref[...])

    return pl.pallas_call(
        body, name=name, grid=(s // tm,),
        in_specs=[pl.BlockSpec((tm, d), lambda i: (i, 0)),
                  pl.BlockSpec((tm, d), lambda i: (i, gate_blk)),
                  pl.BlockSpec((d, d), lambda i: (0, 0)),
                  pl.BlockSpec((tm, d), lambda i: (i, 0))],
        out_specs=pl.BlockSpec((tm, d), lambda i: (i, 0)),
        out_shape=SDS((s, d), F32),
        compiler_params=_cparams(("parallel",)),
    )(o, proj, w_out, x)


def gate_out_bwd(dy, w_out, o, proj, gate_blk, name):
    s = dy.shape[0]
    d = D_MODEL
    tm = min(512, s)

    def body(dy_ref, w_ref, o_ref, g_ref, do_ref, dg_ref, hg_ref):
        dh = _dot_nt(dy_ref[...].astype(BF16), w_ref[...])
        g = g_ref[...].astype(F32)
        o = o_ref[...].astype(F32)
        sg = _sigmoid(g)
        silu = g * sg
        do_ref[...] = (dh * silu).astype(BF16)
        dg_ref[...] = (dh * o * (sg * (1.0 + g * (1.0 - sg)))).astype(BF16)
        hg_ref[...] = (o * silu).astype(BF16)

    blk = pl.BlockSpec((tm, d), lambda i: (i, 0))
    return pl.pallas_call(
        body, name=name, grid=(s // tm,),
        in_specs=[blk, pl.BlockSpec((d, d), lambda i: (0, 0)), blk,
                  pl.BlockSpec((tm, d), lambda i: (i, gate_blk))],
        out_specs=[blk, blk, blk],
        out_shape=[SDS((s, d), BF16)] * 3,
        compiler_params=_cparams(("parallel",)),
    )(dy, w_out, o, proj)


def tn_matmul(a, b, name):
    s, k = a.shape
    n = b.shape[1]
    tk = min(1024, k)
    tn = _pick_tile(n, 512)
    ts = min(2048, s)
    assert k % tk == 0 and s % ts == 0

    def body(a_ref, b_ref, o_ref):
        @pl.when(pl.program_id(2) == 0)
        def _():
            o_ref[...] = jnp.zeros_like(o_ref)

        o_ref[...] += _dot_tn(a_ref[...].astype(BF16), b_ref[...].astype(BF16))

    return pl.pallas_call(
        body, name=name, grid=(k // tk, n // tn, s // ts),
        in_specs=[pl.BlockSpec((ts, tk), lambda i, j, r: (r, i)),
                  pl.BlockSpec((ts, tn), lambda i, j, r: (r, j))],
        out_specs=pl.BlockSpec((tk, tn), lambda i, j, r: (i, j)),
        out_shape=SDS((k, n), F32),
        compiler_params=_cparams(("parallel", "parallel", "arbitrary")),
    )(a, b)


def tn_matmul_shards(a, b_parts, name):
    s, k = a.shape
    per = D_MODEL // PAIR_COLS
    assert len(b_parts) * per == N_DEV and a.dtype == BF16
    assert all(b.shape == (s, D_MODEL) and b.dtype == BF16 for b in b_parts)

    def body(a_ref, *refs):
        o_ref = refs[-1]
        j = pl.program_id(0)
        for p, b_ref in enumerate(refs[:-1]):
            @pl.when(j // per == p)
            def _(b_ref=b_ref):
                o_ref[...] = _dot_tn(a_ref[...], b_ref[...]).astype(BF16)

    part_specs = [pl.BlockSpec((s, PAIR_COLS), lambda j, p=p: (0, jnp.clip(j - per * p, 0, per - 1)))
                  for p in range(len(b_parts))]
    return pl.pallas_call(
        body, name=name, grid=(N_DEV,),
        in_specs=[pl.BlockSpec((s, k), lambda j: (0, 0))] + part_specs,
        out_specs=pl.BlockSpec((None, k, PAIR_COLS), lambda j: (j, 0, 0)),
        out_shape=SDS((N_DEV, k, PAIR_COLS), BF16),
        compiler_params=_cparams(("arbitrary",)),
    )(a, *b_parts)


def nt_norm_bwd(dp, w, x, xcol, g, res, name, out_dtype, shard_col=None):
    tm = min(512, dp[0].shape[0] if shard_col is not None else dp.shape[0])
    if shard_col is None:
        s, n = dp.shape
        k = w.shape[0]
        w_spec = pl.BlockSpec((k, n), lambda i: (0, 0))
        dp_parts = [dp]
        dp_specs = [pl.BlockSpec((tm, n), lambda i: (i, 0))]
    else:
        s, k = dp[0].shape[0], D_MODEL
        w_spec = pl.BlockSpec((N_DEV, k, PAIR_COLS), lambda i: (0, 0, shard_col))
        dp_parts = list(dp)
        dp_specs = [pl.BlockSpec((tm, D_MODEL), lambda i: (i, 0)) for _ in dp_parts]
    np_ = len(dp_parts)
    per = D_MODEL // PAIR_COLS
    assert s % tm == 0
    has_res = res is not None

    def body(*refs):
        dp_refs = refs[:np_]
        if has_res:
            w_ref, x_ref, g_ref, r_ref, dx_ref, dg_ref = refs[np_:]
        else:
            w_ref, x_ref, g_ref, dx_ref, dg_ref = refs[np_:]

        @pl.when(pl.program_id(0) == 0)
        def _():
            dg_ref[...] = jnp.zeros_like(dg_ref)

        if shard_col is None:
            dxn = _dot_nt(dp_refs[0][...], w_ref[...])
        else:
            dxn = None
            for j in range(N_DEV):
                c0 = (j % per) * PAIR_COLS
                term = _dot_nt(dp_refs[j // per][:, c0:c0 + PAIR_COLS], w_ref[j])
                dxn = term if dxn is None else dxn + term
        xf = x_ref[...].astype(F32)
        rstd = lax.rsqrt(jnp.mean(xf * xf, axis=-1, keepdims=True) + NORM_EPS)
        xhat = xf * rstd
        dg_ref[...] += jnp.sum(dxn * xhat, axis=0, keepdims=True)
        dxh = dxn * g_ref[...]
        dx = rstd * (dxh - xhat * jnp.mean(dxh * xhat, axis=-1, keepdims=True))
        if has_res:
            dx = dx + r_ref[...]
        dx_ref[...] = dx.astype(out_dtype)

    in_specs = dp_specs + [w_spec,
                           pl.BlockSpec((tm, k), lambda i: (i, xcol)),
                           pl.BlockSpec((1, k), lambda i: (0, 0))]
    args = dp_parts + [w, x, g]
    if has_res:
        in_specs.append(pl.BlockSpec((tm, k), lambda i: (i, 0)))
        args.append(res)
    return pl.pallas_call(
        body, name=name, grid=(s // tm,),
        in_specs=in_specs,
        out_specs=[pl.BlockSpec((tm, k), lambda i: (i, 0)),
                   pl.BlockSpec((1, k), lambda i: (0, 0))],
        out_shape=[SDS((s, k), out_dtype), SDS((1, k), F32)],
        compiler_params=_cparams(("arbitrary",)),
    )(*args)


SB_LOG_CUTOFF = -80.0
SB_T = 256
SB_TK_LEFT = 128
SB_FWD_TILES = 4
SB_BWD_TILES = 2


def _sb_sweep(n_left, cs, rest, step):
    n = len(cs)

    def top(arrs):
        m = jnp.max(arrs[0])
        for a in arrs[1:]:
            m = jnp.maximum(m, jnp.max(a))
        return m

    def cond(st):
        return (st[0] <= n_left) & (st[1] > SB_LOG_CUTOFF)

    def body(st):
        cs2, rest2 = step(st[0], st[2:2 + n], st[2 + n:])
        return (st[0] + 1, top(cs2)) + tuple(cs2) + tuple(rest2)

    return lax.while_loop(cond, body, (jnp.int32(1), top(cs)) + tuple(cs) + tuple(rest))[2 + n:]


def _tile_masks(t):
    row = lax.broadcasted_iota(jnp.int32, (t, t), 0)
    col = lax.broadcasted_iota(jnp.int32, (t, t), 1)
    return row, col


def sb_attn_fwd(proj, name):
    s = proj.shape[0]
    n_tiles = SB_FWD_TILES
    t = min(SB_T, s)
    tko = min(SB_TK_LEFT, t)
    nq = s // t
    scale = 1.0 / math.sqrt(HALF)

    def body(q_ref, k_ref, v_ref, o_ref):
        lane = lax.broadcasted_iota(jnp.int32, (1, LANES), 1)
        row, col = _tile_masks(t)
        u_incl = (row >= col).astype(BF16)
        u_off = u_incl[:tko, :tko]
        tri = jnp.concatenate([col < row] * 2, axis=0)

        def scores(pr, qs, k0, tk):
            k = k_ref[pl.ds(k0, tk), pr * LANES:(pr + 1) * LANES]
            z = _dot_nt(qs, k)
            nz = -z
            lf = jnp.minimum(nz, 0.0) - jnp.log(1.0 + jnp.exp(jnp.minimum(z, nz)))
            return z, lf

        def accumulate(pr, k0, tk, z, lf, c, acc, diag):
            v = v_ref[pl.ds(k0, tk), pr * LANES:(pr + 1) * LANES]
            if diag:
                lf = jnp.where(tri, lf, 0.0)
            incl = _cumsum_dot(lf, u_incl if diag else u_off) + c
            a = jnp.exp(z + incl)
            if diag:
                a = jnp.where(tri, a, 0.0)
            acc = acc + _dot(a.astype(BF16), v)
            c = c + jnp.sum(lf, axis=1, keepdims=True)
            return c, acc

        def q_body(qi, carry):
            q0 = pl.multiple_of(qi * t, t)
            qss = []
            for pr in range(n_tiles):
                q = q_ref[pl.ds(q0, t), pr * LANES:(pr + 1) * LANES] * scale
                zq = jnp.zeros_like(q)
                qss.append(jnp.concatenate([jnp.where(lane < HALF, q, zq), jnp.where(lane >= HALF, q, zq)], axis=0))

            def step(jj, cs, accs, diag=False):
                k0, tk = (q0, t) if diag else (pl.multiple_of(q0 - jj * tko, tko), tko)
                outs = [accumulate(pr, k0, tk, *scores(pr, qss[pr], k0, tk), cs[pr], accs[pr], diag)
                        for pr in range(n_tiles)]
                return tuple(o[0] for o in outs), tuple(o[1] for o in outs)

            zero = (jnp.zeros((2 * t, 1), F32),) * n_tiles, (jnp.zeros((2 * t, LANES), F32),) * n_tiles
            cs, accs = step(0, zero[0], zero[1], True)
            accs = _sb_sweep(qi * (t // tko), cs, accs, step)
            for pr in range(n_tiles):
                o_ref[pl.ds(q0, t), pr * LANES:(pr + 1) * LANES] = jnp.where(lane < HALF, accs[pr][:t], accs[pr][t:])
            return carry

        lax.fori_loop(0, nq, q_body, 0)

    w = n_tiles * LANES
    return pl.pallas_call(
        body, name=name, grid=(D_MODEL // w,),
        in_specs=[pl.BlockSpec((s, w), lambda p: (0, p)),
                  pl.BlockSpec((s, w), lambda p: (0, D_MODEL // w + p)),
                  pl.BlockSpec((s, w), lambda p: (0, 2 * (D_MODEL // w) + p))],
        out_specs=pl.BlockSpec((s, w), lambda p: (0, p)),
        out_shape=SDS((s, D_MODEL), F32),
        compiler_params=_cparams(("parallel",), vmem_mb=56),
    )(proj, proj, proj)


def sb_attn_bwd(proj, o, do, name):
    s = proj.shape[0]
    n_tiles = SB_BWD_TILES
    t = min(SB_T, s)
    tko = min(SB_TK_LEFT, t)
    nq = s // t
    scale = 1.0 / math.sqrt(HALF)

    def body(q_ref, k_ref, v_ref, o_ref, do_ref, dq_ref, dk_ref, dv_ref, dk_s, dv_s):
        dk_s[...] = jnp.zeros_like(dk_s)
        dv_s[...] = jnp.zeros_like(dv_s)
        lane = lax.broadcasted_iota(jnp.int32, (1, LANES), 1)
        row, col = _tile_masks(t)
        u_incl = (row >= col).astype(BF16)
        u_excl = (row > col).astype(BF16)
        tri = jnp.concatenate([col < row] * 2, axis=0)

        def scores(pr, qs, dos, k0, tk):
            cols = slice(pr * LANES, (pr + 1) * LANES)
            z = _dot_nt(qs, k_ref[pl.ds(k0, tk), cols])
            nz = -z
            lf = jnp.minimum(nz, 0.0) - jnp.log(1.0 + jnp.exp(jnp.minimum(z, nz)))
            return z, lf, _dot_nt(dos, v_ref[pl.ds(k0, tk), cols])

        def accumulate(pr, qs, dos, tsum, k0, tk, z, lf, da, c, r, dq, diag):
            cols = slice(pr * LANES, (pr + 1) * LANES)
            ui, ue = (u_incl, u_excl) if diag else (u_incl[:tko, :tko], u_excl[:tko, :tko])
            sig = jnp.exp(z + lf)
            if diag:
                lf = jnp.where(tri, lf, 0.0)
            incl = _cumsum_dot(lf, ui) + c
            a = jnp.exp(z + incl)
            if diag:
                a = jnp.where(tri, a, 0.0)
            ab = a.astype(BF16)
            dl = ab.astype(F32) * da
            pre = tsum - r - _split_dot(dl, ue)
            dz = dl - sig * pre
            if diag:
                dz = jnp.where(tri, dz, 0.0)
            dzb = dz.astype(BF16)
            dq = dq + _dot(dzb, k_ref[pl.ds(k0, tk), cols])
            dk_s[pl.ds(k0, tk), cols] += _dot_tn(dzb, qs)
            dv_s[pl.ds(k0, tk), cols] += _dot_tn(ab, dos)
            c = c + jnp.sum(lf, axis=1, keepdims=True)
            r = r + jnp.sum(dl, axis=1, keepdims=True)
            return c, r, dq

        def q_body(qi, carry):
            q0 = pl.multiple_of(qi * t, t)
            lo, hi = lane < HALF, lane >= HALF
            qss, doss, tsums = [], [], []
            for pr in range(n_tiles):
                cols = slice(pr * LANES, (pr + 1) * LANES)
                q = q_ref[pl.ds(q0, t), cols] * scale
                dob = do_ref[pl.ds(q0, t), cols]
                prod = dob.astype(F32) * o_ref[pl.ds(q0, t), cols]
                zb = jnp.zeros_like(q)
                qss.append(jnp.concatenate([jnp.where(lo, q, zb), jnp.where(hi, q, zb)], axis=0))
                doss.append(jnp.concatenate([jnp.where(lo, dob, zb), jnp.where(hi, dob, zb)], axis=0))
                tsums.append(jnp.concatenate([jnp.sum(jnp.where(lo, prod, 0.0), axis=1, keepdims=True),
                                              jnp.sum(jnp.where(hi, prod, 0.0), axis=1, keepdims=True)], axis=0))

            def step(jj, cs, rest, diag=False):
                k0, tk = (q0, t) if diag else (pl.multiple_of(q0 - jj * tko, tko), tko)
                outs = [accumulate(pr, qss[pr], doss[pr], tsums[pr], k0, tk,
                                   *scores(pr, qss[pr], doss[pr], k0, tk),
                                   cs[pr], rest[pr], rest[n_tiles + pr], diag) for pr in range(n_tiles)]
                return tuple(o[0] for o in outs), tuple(o[1] for o in outs) + tuple(o[2] for o in outs)

            zc = (jnp.zeros((2 * t, 1), F32),) * n_tiles
            cs, rest = step(0, zc, zc + (jnp.zeros((2 * t, LANES), F32),) * n_tiles, True)
            rest = _sb_sweep(qi * (t // tko), cs, rest, step)
            for pr in range(n_tiles):
                dq = rest[n_tiles + pr]
                dq_ref[pl.ds(q0, t), pr * LANES:(pr + 1) * LANES] = (jnp.where(lo, dq[:t], dq[t:]) * scale).astype(BF16)
            return carry

        lax.fori_loop(0, nq, q_body, 0)
        dk_ref[...] = dk_s[...].astype(BF16)
        dv_ref[...] = dv_s[...].astype(BF16)

    w = n_tiles * LANES
    nblk = D_MODEL // w
    blk = lambda off: pl.BlockSpec((s, w), lambda p: (0, off * nblk + p))
    return pl.pallas_call(
        body, name=name, grid=(nblk,),
        in_specs=[blk(0), blk(1), blk(2), blk(0), blk(0)],
        out_specs=[blk(0), blk(0), blk(0)],
        out_shape=[SDS((s, D_MODEL), BF16)] * 3,
        scratch_shapes=[pltpu.VMEM((s, w), F32), pltpu.VMEM((s, w), F32)],
        compiler_params=_cparams(("parallel",), vmem_mb=56),
    )(proj, proj, proj, o, do)


MLA_HEADS = 8
MLA_QK = 192
MLA_SCALE = 1.0 / math.sqrt(MLA_QK)
MLA_TK = 512


def _rope_tables(s):
    inv_freq = 10000.0 ** (-jnp.arange(32, dtype=F32) / 32)
    ang = jnp.arange(s, dtype=F32)[:, None] * inv_freq[None, :]
    cos = jnp.tile(jnp.cos(ang), (1, 4))
    sin = jnp.sin(ang)
    sin_signed = jnp.tile(jnp.concatenate([-sin, sin], axis=1), (1, 2))
    return cos, sin_signed


def _partner_matrix():
    j = lax.broadcasted_iota(jnp.int32, (LANES, LANES), 0)
    l = lax.broadcasted_iota(jnp.int32, (LANES, LANES), 1)
    return (j == l + jnp.where((l % HALF) < 32, 32, -32)).astype(BF16)


def _partner(u, pmat):
    return _split_dot(u, pmat)


def _half_sum(x):
    j = lax.broadcasted_iota(jnp.int32, (LANES, LANES), 0)
    l = lax.broadcasted_iota(jnp.int32, (LANES, LANES), 1)
    return _dot(x.astype(BF16), ((j < HALF) == (l < HALF)).astype(BF16))


def _lane_sum(x):
    return _dot(x.astype(BF16), jnp.ones((LANES, LANES), BF16))


def _mla_head_inputs(h, qa_ref, kva_ref, kpe, lane):
    a = h % 2
    hm = (lane < HALF) if a == 0 else (lane >= HALF)
    qn = qa_ref[:, h * 128:(h + 1) * 128].astype(F32)
    qr = jnp.where(hm, qa_ref[:, 1024 + (h // 2) * 128:1024 + (h // 2 + 1) * 128].astype(F32), 0.0)
    kn = kva_ref[:, h * 256:h * 256 + 128].astype(F32)
    kr = kpe if a == 0 else pltpu.roll(kpe, HALF, 1)
    return qn, qr, kn, kr


def mla_prep_fwd(qa, kva, proj, gq, gk, cos, sin, name):
    s = qa.shape[0]
    tm = min(256, s)

    def body(qa_ref, kva_ref, kpe_ref, gq_ref, gk_ref, cos_ref, sin_ref, q_out, k_out):
        lane = lax.broadcasted_iota(jnp.int32, (1, LANES), 1)
        pmat = _partner_matrix()
        kpe = kpe_ref[...].astype(F32)
        cs, sn = cos_ref[...], sin_ref[...]
        for h in range(MLA_HEADS):
            qn, qr, kn, kr = _mla_head_inputs(h, qa_ref, kva_ref, kpe, lane)
            for xn, xr, g_ref, out, sc in ((qn, qr, gq_ref, q_out, MLA_SCALE), (kn, kr, gk_ref, k_out, 1.0)):
                ss = _lane_sum(xn * xn + xr * xr)
                rstd = lax.rsqrt(ss * (1.0 / MLA_QK) + NORM_EPS) * sc
                yn = xn * rstd * g_ref[:, 0:128]
                ur = xr * rstd * g_ref[:, 128:256]
                yr = ur * cs + _partner(ur, pmat) * sn
                out[:, h * 256:h * 256 + 128] = yn.astype(BF16)
                out[:, h * 256 + 128:h * 256 + 256] = yr.astype(BF16)

    row = lambda w, c: pl.BlockSpec((tm, w), lambda i: (i, c))
    full = lambda w: pl.BlockSpec((1, w), lambda i: (0, 0))
    return pl.pallas_call(
        body, name=name, grid=(s // tm,),
        in_specs=[row(1536, 0), row(2048, 0), row(128, 11), full(256), full(256), row(128, 0), row(128, 0)],
        out_specs=[row(2048, 0), row(2048, 0)],
        out_shape=[SDS((s, 2048), BF16)] * 2,
        compiler_params=_cparams(("parallel",)),
    )(qa, kva, proj, gq, gk, cos, sin)


def mla_prep_bwd(dq_p, dk_p, dv, qa, kva, proj, gq, gk, cos, sin, name):
    s = qa.shape[0]
    tm = min(256, s)

    def body(dq_ref, dk_ref, dv_ref, qa_ref, kva_ref, kpe_ref, gq_ref, gk_ref, cos_ref, sin_ref,
             dqa_ref, dkva_ref, dkpe_ref, dgq_ref, dgk_ref):
        @pl.when(pl.program_id(0) == 0)
        def _():
            dgq_ref[...] = jnp.zeros_like(dgq_ref)
            dgk_ref[...] = jnp.zeros_like(dgk_ref)

        lane = lax.broadcasted_iota(jnp.int32, (1, LANES), 1)
        pmat = _partner_matrix()
        kpe = kpe_ref[...].astype(F32)
        cs, sn = cos_ref[...], sin_ref[...]
        dkpe = jnp.zeros((tm, LANES), F32)
        dqr_pair = None
        for h in range(MLA_HEADS):
            qn, qr, kn, kr = _mla_head_inputs(h, qa_ref, kva_ref, kpe, lane)
            res = []
            for xn, xr, g_ref, d_ref, dg_ref, sc in ((qn, qr, gq_ref, dq_ref, dgq_ref, MLA_SCALE),
                                                     (kn, kr, gk_ref, dk_ref, dgk_ref, 1.0)):
                ss = _lane_sum(xn * xn + xr * xr)
                rstd = lax.rsqrt(ss * (1.0 / MLA_QK) + NORM_EPS)
                hn, hr = xn * rstd, xr * rstd
                dyn = d_ref[:, h * 256:h * 256 + 128] * sc
                dyr = d_ref[:, h * 256 + 128:h * 256 + 256] * sc
                dur = dyr * cs - _partner(dyr, pmat) * sn
                dg_ref[:, 0:128] += jnp.sum(dyn * hn, axis=0, keepdims=True)
                dg_ref[:, 128:256] += jnp.sum(dur * hr, axis=0, keepdims=True)
                dhn = dyn * g_ref[:, 0:128]
                dhr = dur * g_ref[:, 128:256]
                mu = _lane_sum(dhn * hn + dhr * hr) * (1.0 / MLA_QK)
                res.append((rstd * (dhn - hn * mu), rstd * (dhr - hr * mu)))
            (dqn, dqr), (dkn, dkr) = res
            dqa_ref[:, h * 128:(h + 1) * 128] = dqn.astype(BF16)
            if h % 2 == 0:
                dqr_pair = dqr
            else:
                dqa_ref[:, 1024 + (h // 2) * 128:1024 + (h // 2 + 1) * 128] = (dqr_pair + dqr).astype(BF16)
            dkva_ref[:, h * 256:h * 256 + 128] = dkn.astype(BF16)
            dkva_ref[:, h * 256 + 128:h * 256 + 256] = dv_ref[:, h * 128:(h + 1) * 128]
            dkpe = dkpe + (dkr if h % 2 == 0 else pltpu.roll(dkr, HALF, 1))
        dkpe_ref[...] = dkpe.astype(BF16)

    row = lambda w, c: pl.BlockSpec((tm, w), lambda i: (i, c))
    full = lambda w: pl.BlockSpec((1, w), lambda i: (0, 0))
    return pl.pallas_call(
        body, name=name, grid=(s // tm,),
        in_specs=[row(2048, 0), row(2048, 0), row(1024, 0), row(1536, 0), row(2048, 0), row(128, 11),
                  full(256), full(256), row(128, 0), row(128, 0)],
        out_specs=[row(1536, 0), row(2048, 0), row(128, 0), full(256), full(256)],
        out_shape=[SDS((s, 1536), BF16), SDS((s, 2048), BF16), SDS((s, 128), BF16),
                   SDS((1, 256), F32), SDS((1, 256), F32)],
        compiler_params=_cparams(("arbitrary",)),
    )(dq_p, dk_p, dv, qa, kva, proj, gq, gk, cos, sin)


def mla_attn_fwd(q_p, k_p, kva, name):
    s = q_p.shape[0]
    tk = min(MLA_TK, s)
    tq = min(512, s)
    r = tq // tk
    nq = s // tq

    def body(q_ref, k_ref, va_ref, vb_ref, o_ref, lse_ref):
        row = lax.broadcasted_iota(jnp.int32, (tq, tk), 0)
        col = lax.broadcasted_iota(jnp.int32, (tq, tk), 1)
        v_refs = (va_ref, vb_ref)

        def tile(hh, q, kj, m, l, acc, mask):
            k0 = pl.multiple_of(kj * tk, tk)
            sc = _dot_nt(q, k_ref[pl.ds(k0, tk), hh * 256:(hh + 1) * 256])
            if mask is not None:
                sc = jnp.where(mask, sc, -1e30)
            m_new = jnp.maximum(m, jnp.max(sc, axis=1, keepdims=True))
            alpha = jnp.exp(m - m_new)
            p = jnp.exp(sc - m_new)
            l = alpha * l + jnp.sum(p, axis=1, keepdims=True)
            acc = alpha * acc + _dot(p.astype(BF16), v_refs[hh][pl.ds(k0, tk), :])
            return m_new, l, acc

        def q_body(qi, carry):
            q0 = pl.multiple_of(qi * tq, tq)
            qs = [q_ref[pl.ds(q0, tq), hh * 256:(hh + 1) * 256] for hh in range(2)]

            def step(kj, cr, mask=None):
                out = ()
                for hh in range(2):
                    out += tile(hh, qs[hh], kj, cr[3 * hh], cr[3 * hh + 1], cr[3 * hh + 2], mask)
                return out

            cr = (jnp.full((tq, 1), -1e30, F32), jnp.zeros((tq, 1), F32), jnp.zeros((tq, LANES), F32)) * 2
            cr = lax.fori_loop(0, qi * r, step, cr)
            for d in range(r):
                cr = step(qi * r + d, cr, col + d * tk <= row)
            for hh in range(2):
                m, l, acc = cr[3 * hh:3 * hh + 3]
                o_ref[pl.ds(q0, tq), hh * 128:(hh + 1) * 128] = acc / l
                lse_ref[pl.ds(q0, tq), hh * 128:(hh + 1) * 128] = jnp.broadcast_to(m + jnp.log(l), (tq, LANES))
            return carry

        lax.fori_loop(0, nq, q_body, 0)

    return pl.pallas_call(
        body, name=name, grid=(MLA_HEADS // 2,),
        in_specs=[pl.BlockSpec((s, 512), lambda g: (0, g)),
                  pl.BlockSpec((s, 512), lambda g: (0, g)),
                  pl.BlockSpec((s, 128), lambda g: (0, 4 * g + 1)),
                  pl.BlockSpec((s, 128), lambda g: (0, 4 * g + 3))],
        out_specs=[pl.BlockSpec((s, 256), lambda g: (0, g)),
                   pl.BlockSpec((s, 256), lambda g: (0, g))],
        out_shape=[SDS((s, 1024), F32), SDS((s, 1024), F32)],
        compiler_params=_cparams(("parallel",), vmem_mb=56),
    )(q_p, k_p, kva, kva)


def mla_attn_bwd(q_p, k_p, kva, o, do, lse, name):
    s = q_p.shape[0]
    tk = min(MLA_TK, s)
    tq = min(512, s)
    r = tq // tk
    nq = s // tq

    def body(q_ref, k_ref, v_ref, o_ref, do_ref, lse_ref, dq_ref, dk_ref, dv_ref, dkt_s, dvt_s):
        dkt_s[...] = jnp.zeros_like(dkt_s)
        dvt_s[...] = jnp.zeros_like(dvt_s)
        row = lax.broadcasted_iota(jnp.int32, (tq, tk), 0)
        col = lax.broadcasted_iota(jnp.int32, (tq, tk), 1)

        def tile(q, qt, dob, dot_, lse, delta, kj, dq, mask):
            k0 = pl.multiple_of(kj * tk, tk)
            k = k_ref[pl.ds(k0, tk), :]
            v = v_ref[pl.ds(k0, tk), :]
            p = jnp.exp(_dot_nt(q, k) - lse)
            if mask is not None:
                p = jnp.where(mask, p, 0.0)
            ds = (p * (_dot_nt(dob, v) - delta)).astype(BF16)
            dq = dq + _dot(ds, k)
            dkt_s[:, pl.ds(k0, tk)] += _dot(qt, ds)
            dvt_s[:, pl.ds(k0, tk)] += _dot(dot_, p.astype(BF16))
            return dq

        def q_body(qi, carry):
            q0 = pl.multiple_of(qi * tq, tq)
            q = q_ref[pl.ds(q0, tq), :]
            dob = do_ref[pl.ds(q0, tq), :]
            qt = q.astype(F32).T.astype(BF16)
            dot_ = dob.astype(F32).T.astype(BF16)
            lse = lse_ref[pl.ds(q0, tq), 0:1]
            delta = jnp.sum(dob.astype(F32) * o_ref[pl.ds(q0, tq), :], axis=1, keepdims=True)
            dq = lax.fori_loop(0, qi * r, lambda kj, dq: tile(q, qt, dob, dot_, lse, delta, kj, dq, None),
                               jnp.zeros((tq, 256), F32))
            for d in range(r):
                dq = tile(q, qt, dob, dot_, lse, delta, qi * r + d, dq, col + d * tk <= row)
            dq_ref[pl.ds(q0, tq), :] = dq
            return carry

        lax.fori_loop(0, nq, q_body, 0)
        dk_ref[...] = dkt_s[...].T
        dv_ref[...] = dvt_s[...].T.astype(BF16)

    b256 = pl.BlockSpec((s, 256), lambda h: (0, h))
    b128 = pl.BlockSpec((s, 128), lambda h: (0, h))
    return pl.pallas_call(
        body, name=name, grid=(MLA_HEADS,),
        in_specs=[b256, b256, pl.BlockSpec((s, 128), lambda h: (0, 2 * h + 1)), b128, b128, b128],
        out_specs=[b256, b256, b128],
        out_shape=[SDS((s, 2048), F32), SDS((s, 2048), F32), SDS((s, 1024), BF16)],
        scratch_shapes=[pltpu.VMEM((256, s), F32), pltpu.VMEM((128, s), F32)],
        compiler_params=_cparams(("parallel",), vmem_mb=56),
    )(q_p, k_p, kva, o, do, lse)


SWA_HEADS = 16
SWA_BLOCK = 128


def _swa_head(tile_idx, half):
    return (2 * (tile_idx // 4) + half) * 4 + tile_idx % 4


SWA_Q_PERM = np.concatenate([np.arange(_swa_head(tt, a) * HALF, (_swa_head(tt, a) + 1) * HALF)
                             for tt in range(8) for a in range(2)])
SWA_SLOPES = [2.0 ** (-8.0 * (h + 1) / SWA_HEADS) for h in range(SWA_HEADS)]


def swa_prep_fwd(proj, g128, name):
    s = proj.shape[0]
    tm = min(512, s)

    def body(q_ref, k_ref, g_ref, o_ref):
        lane = lax.broadcasted_iota(jnp.int32, (1, LANES), 1)
        lo = lane < HALF
        for tt in range(10):
            if tt < 8:
                x = q_ref[:, tt * 128:(tt + 1) * 128].astype(F32)
            else:
                x = k_ref[:, (tt - 8) * 128:(tt - 7) * 128].astype(F32)
            rstd = lax.rsqrt(_half_sum(x * x) * (1.0 / HALF) + NORM_EPS)
            gi = 0 if tt < 8 else 1
            o_ref[:, tt * 128:(tt + 1) * 128] = (x * rstd * g_ref[gi:gi + 1, :]).astype(BF16)

    return pl.pallas_call(
        body, name=name, grid=(s // tm,),
        in_specs=[pl.BlockSpec((tm, 1024), lambda i: (i, 0)), pl.BlockSpec((tm, 256), lambda i: (i, 8)),
                  pl.BlockSpec((2, 128), lambda i: (0, 0))],
        out_specs=pl.BlockSpec((tm, 1280), lambda i: (i, 0)),
        out_shape=SDS((s, 1280), BF16),
        compiler_params=_cparams(("parallel",)),
    )(proj, proj, g128)


def swa_prep_bwd(dqn, dkn, proj, g128, name):
    s = proj.shape[0]
    tm = min(512, s)
    nsteps = s // tm

    def body(dq_ref, dk_ref, xq_ref, xk_ref, g_ref, o_ref, dg_ref):
        @pl.when(pl.program_id(0) == 0)
        def _():
            dg_ref[...] = jnp.zeros_like(dg_ref)

        lane = lax.broadcasted_iota(jnp.int32, (1, LANES), 1)
        lo = lane < HALF
        for tt in range(10):
            if tt < 8:
                x = xq_ref[:, tt * 128:(tt + 1) * 128].astype(F32)
                dy = dq_ref[:, tt * 128:(tt + 1) * 128].astype(F32)
            else:
                x = xk_ref[:, (tt - 8) * 128:(tt - 7) * 128].astype(F32)
                dy = dk_ref[:, (tt - 8) * 128:(tt - 7) * 128].astype(F32)
            gi = 0 if tt < 8 else 1
            rstd = lax.rsqrt(_half_sum(x * x) * (1.0 / HALF) + NORM_EPS)
            xh = x * rstd
            dg_ref[gi:gi + 1, :] += jnp.sum(dy * xh, axis=0, keepdims=True)
            dxh = dy * g_ref[gi:gi + 1, :]
            mu = _half_sum(dxh * xh) * (1.0 / HALF)
            o_ref[:, tt * 128:(tt + 1) * 128] = (rstd * (dxh - xh * mu)).astype(BF16)

        @pl.when(pl.program_id(0) == nsteps - 1)
        def _():
            acc = dg_ref[...]
            dg_ref[...] = acc + pltpu.roll(acc, HALF, 1)

    return pl.pallas_call(
        body, name=name, grid=(nsteps,),
        in_specs=[pl.BlockSpec((tm, 1024), lambda i: (i, 0)), pl.BlockSpec((tm, 256), lambda i: (i, 0)),
                  pl.BlockSpec((tm, 1024), lambda i: (i, 0)), pl.BlockSpec((tm, 256), lambda i: (i, 8)),
                  pl.BlockSpec((2, 128), lambda i: (0, 0))],
        out_specs=[pl.BlockSpec((tm, 1280), lambda i: (i, 0)), pl.BlockSpec((2, 128), lambda i: (0, 0))],
        out_shape=[SDS((s, 1280), BF16), SDS((2, 128), F32)],
        compiler_params=_cparams(("arbitrary",)),
    )(dqn, dkn, proj, proj, g128)


SWA_QSCALE = 1.0 / math.sqrt(HALF)


def _swa_bias():
    b = SWA_BLOCK
    row = jnp.arange(b)[:, None]
    col = jnp.arange(2 * b)[None, :]
    rel = row + b - col
    valid = (rel >= 0) & (rel < b)
    slopes = jnp.asarray([SWA_SLOPES[_swa_head(tt, a)] for tt in range(8) for a in range(2)], F32)
    bias = jnp.where(valid[None], -slopes[:, None, None] * rel[None].astype(F32), -1e30)
    return jnp.stack([jnp.where((col >= b)[None], bias, -1e30), bias])


def _swa_tile_inputs(tt, q_ref, bias_ref, sink_ref, lane):
    b = SWA_BLOCK
    qt = q_ref[:, tt * 128:(tt + 1) * 128] * SWA_QSCALE
    zq = jnp.zeros_like(qt)
    qs = jnp.concatenate([jnp.where(lane < HALF, qt, zq), jnp.where(lane >= HALF, qt, zq)], axis=0)
    bias = jnp.concatenate([bias_ref[0, 2 * tt], bias_ref[0, 2 * tt + 1]], axis=0)
    sink = jnp.concatenate([jnp.full((b, 1), sink_ref[_swa_head(tt, a)], F32) for a in range(2)], axis=0)
    return qs, bias, sink


def _swa_softmax(qs, kt, bias, sink):
    sc = _dot_nt(qs, kt) + bias
    m = jnp.maximum(jnp.max(sc, axis=1, keepdims=True), sink)
    e = jnp.exp(sc - m)
    es = jnp.exp(sink - m)
    inv = 1.0 / (jnp.sum(e, axis=1, keepdims=True) + es)
    return e * inv, es * inv


def _swa_specs(b):
    cur = lambda w, c: pl.BlockSpec((b, w), lambda n: (n, c))
    prev = lambda w, c: pl.BlockSpec((b, w), lambda n: (jnp.maximum(n - 1, 0), c))
    bias = pl.BlockSpec((1, SWA_HEADS, b, 2 * b), lambda n: (jnp.minimum(n, 1), 0, 0, 0))
    return cur, prev, bias


def swa_attn_fwd(qkn, proj, sinks, bias, name):
    s = qkn.shape[0]
    b = SWA_BLOCK

    def body(sink_ref, bias_ref, q_ref, kc_ref, kp_ref, vc_ref, vp_ref, o_ref):
        lane = lax.broadcasted_iota(jnp.int32, (1, LANES), 1)
        for tt in range(8):
            gp = tt // 4
            kt = jnp.concatenate([kp_ref[:, gp * 128:(gp + 1) * 128], kc_ref[:, gp * 128:(gp + 1) * 128]], axis=0)
            vt = jnp.concatenate([vp_ref[:, gp * 128:(gp + 1) * 128], vc_ref[:, gp * 128:(gp + 1) * 128]], axis=0)
            qt = q_ref[:, tt * 128:(tt + 1) * 128] * SWA_QSCALE
            outs = []
            for a in range(2):
                qm = jnp.where((lane < HALF) if a == 0 else (lane >= HALF), qt, jnp.zeros_like(qt))
                p, _ = _swa_softmax(qm, kt, bias_ref[0, 2 * tt + a], sink_ref[_swa_head(tt, a)])
                outs.append(_dot(p.astype(BF16), vt))
            o_ref[:, tt * 128:(tt + 1) * 128] = jnp.where(lane < HALF, outs[0], outs[1])

    cur, prev, bias_spec = _swa_specs(b)
    return pl.pallas_call(
        body, name=name, grid=(s // b,),
        in_specs=[pl.BlockSpec(memory_space=pltpu.SMEM), bias_spec, cur(1024, 0), cur(256, 4), prev(256, 4),
                  cur(256, 9), prev(256, 9)],
        out_specs=cur(1024, 0),
        out_shape=SDS((s, 1024), F32),
        compiler_params=_cparams(("parallel",)),
    )(sinks, bias, qkn, qkn, qkn, proj, proj)


def swa_attn_bwd(qkn, proj, sinks, bias, do, name):
    s = qkn.shape[0]
    b = SWA_BLOCK

    def body(sink_ref, bias_ref, q_ref, kc_ref, kp_ref, vc_ref, vp_ref, do_ref, dq_ref, dk_ref, dv_ref, ds_ref):
        n = pl.program_id(0)

        @pl.when(n == 0)
        def _():
            dk_ref[...] = jnp.zeros_like(dk_ref)
            dv_ref[...] = jnp.zeros_like(dv_ref)
            ds_ref[...] = jnp.zeros_like(ds_ref)

        lane = lax.broadcasted_iota(jnp.int32, (1, LANES), 1)
        cur0 = pl.multiple_of(n * b, b)
        prev0 = pl.multiple_of(jnp.maximum(n - 1, 0) * b, b)
        dsink = jnp.zeros((1, LANES), F32)
        for gp in range(2):
            kt = jnp.concatenate([kp_ref[:, gp * 128:(gp + 1) * 128], kc_ref[:, gp * 128:(gp + 1) * 128]], axis=0)
            vt = jnp.concatenate([vp_ref[:, gp * 128:(gp + 1) * 128], vc_ref[:, gp * 128:(gp + 1) * 128]], axis=0)
            dkt = jnp.zeros((LANES, 2 * b), F32)
            dvt = jnp.zeros((LANES, 2 * b), F32)
            for tt in range(4 * gp, 4 * gp + 4):
                qs, bias2, sink = _swa_tile_inputs(tt, q_ref, bias_ref, sink_ref, lane)
                dot_ = do_ref[:, tt * 128:(tt + 1) * 128]
                zd = jnp.zeros_like(dot_)
                dos = jnp.concatenate([jnp.where(lane < HALF, dot_, zd), jnp.where(lane >= HALF, dot_, zd)], axis=0)
                p, ps = _swa_softmax(qs, kt, bias2, sink)
                dp = _dot_nt(dos, vt)
                delta = jnp.sum(p * dp, axis=1, keepdims=True)
                dz = (p * (dp - delta)).astype(BF16)
                sd = ps * delta
                for a in range(2):
                    dsink = dsink + jnp.where(lane == _swa_head(tt, a), -jnp.sum(sd[a * b:(a + 1) * b]), 0.0)
                dq2 = _dot(dz, kt)
                dq_ref[:, tt * 128:(tt + 1) * 128] = (jnp.where(lane < HALF, dq2[:b], dq2[b:]) * SWA_QSCALE).astype(BF16)
                dkt = dkt + _dot(qs.astype(F32).T.astype(BF16), dz)
                dvt = dvt + _dot(dos.astype(F32).T.astype(BF16), p.astype(BF16))
            dkt, dvt = dkt.T, dvt.T
            cols = slice(gp * 128, (gp + 1) * 128)
            dk_ref[pl.ds(prev0, b), cols] += dkt[:b]
            dk_ref[pl.ds(cur0, b), cols] += dkt[b:]
            dv_ref[pl.ds(prev0, b), cols] += dvt[:b]
            dv_ref[pl.ds(cur0, b), cols] += dvt[b:]
        ds_ref[0:1, :] += dsink

    cur, prev, bias_spec = _swa_specs(b)
    full = pl.BlockSpec((s, 256), lambda n: (0, 0))
    return pl.pallas_call(
        body, name=name, grid=(s // b,),
        in_specs=[pl.BlockSpec(memory_space=pltpu.SMEM), bias_spec, cur(1024, 0), cur(256, 4), prev(256, 4),
                  cur(256, 9), prev(256, 9), cur(1024, 0)],
        out_specs=[cur(1024, 0), full, full, pl.BlockSpec((8, LANES), lambda n: (0, 0))],
        out_shape=[SDS((s, 1024), BF16), SDS((s, 256), F32), SDS((s, 256), F32), SDS((8, LANES), F32)],
        compiler_params=_cparams(("arbitrary",)),
    )(sinks, bias, qkn, qkn, qkn, proj, proj, do)


def _my_pos():
    return lax.axis_index("x"), lax.axis_index("y"), lax.axis_index("c")


ANY = pl.BlockSpec(memory_space=pl.ANY)


def all_gather_packed(xs, name):
    r, n = xs.shape

    def body(x_ref, out_ref, send_sems, recv_sems, local_sem):
        x, y, c = _my_pos()
        me, sibling = (x, y, c), (x, y, 1 - c)
        south = c == 0
        via = (jnp.where(south, 1 - x, x), jnp.where(south, y, 1 - y))
        far = (jnp.where(south, x, 1 - x), jnp.where(south, 1 - y, y))
        rk = jnp.where(south, 1, 2)
        diag = (1 - x, 1 - y)

        def slot(px, py, pc):
            return out_ref.at[4 * px + 2 * py + pc]

        def copy(k, block, to, src=None):
            return pltpu.make_async_remote_copy(
                src_ref=slot(*block) if src is None else src, dst_ref=slot(*block),
                send_sem=send_sems.at[k], recv_sem=recv_sems.at[k],
                device_id=to, device_id_type=MESH_ID)

        mine = pltpu.make_async_copy(x_ref, slot(*me), local_sem)
        mine.start()
        sent = [copy(0, me, sibling, src=x_ref), copy(1, me, (1 - x, y, c), src=x_ref),
                copy(2, me, (x, 1 - y, c), src=x_ref)]
        for cp in sent:
            cp.start()
        copy(rk, (*via, c), me).wait_recv()
        sent += [copy(3, (*via, c), (*far, c)), copy(3 + rk, (*via, c), sibling)]
        sent[-2].start()
        sent[-1].start()
        copy(3 - rk, (*far, c), me).wait_recv()
        sent.append(copy(6 - rk, (*far, c), sibling))
        sent[-1].start()
        copy(3, (*diag, c), me).wait_recv()
        sent.append(copy(6, (*diag, c), sibling))
        sent[-1].start()
        copy(0, sibling, me).wait_recv()
        for k, chip in ((4, (1 - x, y)), (5, (x, 1 - y)), (6, diag)):
            copy(k, (*chip, 1 - c), me).wait_recv()
        for cp in sent:
            cp.wait_send()
        mine.wait()

    return pl.pallas_call(
        body, name=name,
        out_shape=SDS((N_DEV, r, n), xs.dtype),
        in_specs=[ANY], out_specs=ANY,
        scratch_shapes=[pltpu.SemaphoreType.DMA((7,)), pltpu.SemaphoreType.DMA((7,)), pltpu.SemaphoreType.DMA],
    )(xs)


def rs_sibling_exchange(g, name):
    _, r, n = g.shape

    def body(g_ref, ra_ref, send_sems, recv_sems):
        x, y, c = _my_pos()
        copies = [pltpu.make_async_remote_copy(
            src_ref=g_ref.at[2 * chip + (1 - c)], dst_ref=ra_ref.at[chip],
            send_sem=send_sems.at[chip], recv_sem=recv_sems.at[chip],
            device_id=(x, y, 1 - c), device_id_type=MESH_ID) for chip in range(4)]
        for cp in copies:
            cp.start()
        for cp in copies:
            cp.wait()

    return pl.pallas_call(
        body, name=name, out_shape=SDS((4, r, n), g.dtype), in_specs=[ANY], out_specs=ANY,
        scratch_shapes=[pltpu.SemaphoreType.DMA((4,)), pltpu.SemaphoreType.DMA((4,))],
    )(g)


def rs_pair_add(g, ra, cidx, name):
    _, r, n = g.shape
    tr = r // 2

    def body(c_ref, g_ref, ra_ref, p_ref):
        p_ref[...] = (g_ref[...].astype(F32) + ra_ref[...].astype(F32)).astype(p_ref.dtype)

    return pl.pallas_call(
        body, name=name,
        grid_spec=pltpu.PrefetchScalarGridSpec(
            num_scalar_prefetch=1, grid=(4, r // tr),
            in_specs=[pl.BlockSpec((1, tr, n), lambda i, j, c: (2 * i + c[0], j, 0)),
                      pl.BlockSpec((1, tr, n), lambda i, j, c: (i, j, 0))],
            out_specs=pl.BlockSpec((1, tr, n), lambda i, j, c: (i, j, 0))),
        out_shape=SDS((4, r, n), g.dtype),
        compiler_params=_cparams(("parallel", "parallel")),
    )(cidx, g, ra)


def rs_cross_exchange(p, name):
    _, r, n = p.shape

    def body(p_ref, rb_ref, send_sems, recv_sems):
        x, y, c = _my_pos()
        copies = []
        for k, (dx, dy) in enumerate(((1, 0), (0, 1), (1, 1))):
            tx = 1 - x if dx else x
            ty = 1 - y if dy else y
            copies.append(pltpu.make_async_remote_copy(
                src_ref=p_ref.at[2 * tx + ty], dst_ref=rb_ref.at[k],
                send_sem=send_sems.at[k], recv_sem=recv_sems.at[k],
                device_id=(tx, ty, c), device_id_type=MESH_ID))
        for cp in copies:
            cp.start()
        for cp in copies:
            cp.wait()

    return pl.pallas_call(
        body, name=name, out_shape=SDS((3, r, n), p.dtype), in_specs=[ANY], out_specs=ANY,
        scratch_shapes=[pltpu.SemaphoreType.DMA((3,)), pltpu.SemaphoreType.DMA((3,))],
    )(p)


def _adamw(w, g, m, v):
    m = ADAM_B1 * m + (1.0 - ADAM_B1) * g
    v = ADAM_B2 * v + (1.0 - ADAM_B2) * (g * g)
    m_hat = m / (1.0 - ADAM_B1 ** ADAM_STEP)
    v_hat = v / (1.0 - ADAM_B2 ** ADAM_STEP)
    delta = -ADAM_LR * (m_hat / (jnp.sqrt(v_hat) + ADAM_EPS) + ADAM_WD * w)
    return delta, m, v


def rs_final_adamw(p, rb, chipidx, w, m, v, row0, col_blk, name):
    r, n = w.shape
    tr = min(r, 256 if n <= PAIR_COLS else 128)
    assert r % tr == 0 and row0 % tr == 0
    rb0 = row0 // tr

    def body(c_ref, p_ref, rb_ref, w_ref, m_ref, v_ref, g_out, d_out, m_out, v_out):
        g = p_ref[0].astype(F32)
        for k in range(3):
            g = g + rb_ref[k].astype(F32)
        d, mn, vn = _adamw(w_ref[...], g, m_ref[...], v_ref[...])
        g_out[...] = g
        d_out[...] = d
        m_out[...] = mn
        v_out[...] = vn

    blk = pl.BlockSpec((tr, n), lambda j, c: (j, 0))
    return pl.pallas_call(
        body, name=name,
        grid_spec=pltpu.PrefetchScalarGridSpec(
            num_scalar_prefetch=1, grid=(r // tr,),
            in_specs=[pl.BlockSpec((1, tr, n), lambda j, c: (c[0], rb0 + j, col_blk)),
                      pl.BlockSpec((3, tr, n), lambda j, c: (0, rb0 + j, col_blk)), blk, blk, blk],
            out_specs=[blk, blk, blk, blk]),
        out_shape=[SDS((r, n), F32)] * 4,
        compiler_params=_cparams(("parallel",)),
    )(chipidx, p, rb, w, m, v)


def small_allreduce_adamw(g, w, m, v, name):
    rows = g.shape[0]

    def body(g_ref, w_ref, m_ref, v_ref, g_out, d_out, m_out, v_out, buf, send_sems, recv_sems):
        x, y, c = _my_pos()
        my = 4 * x + 2 * y + c
        buf[my] = g_ref[...]
        copies = []
        for k in range(1, N_DEV):
            dx, dy, dc = (k >> 2) & 1, (k >> 1) & 1, k & 1
            tgt = (1 - x if dx else x, 1 - y if dy else y, 1 - c if dc else c)
            copies.append(pltpu.make_async_remote_copy(
                src_ref=buf.at[my], dst_ref=buf.at[my],
                send_sem=send_sems.at[k - 1], recv_sem=recv_sems.at[k - 1],
                device_id=tgt, device_id_type=MESH_ID))
        for cp in copies:
            cp.start()
        for cp in copies:
            cp.wait()
        tot = buf[0]
        for j in range(1, N_DEV):
            tot = tot + buf[j]
        d, mn, vn = _adamw(w_ref[...], tot, m_ref[...], v_ref[...])
        g_out[...] = tot
        d_out[...] = d
        m_out[...] = mn
        v_out[...] = vn

    vm = pl.BlockSpec(memory_space=pltpu.VMEM)
    return pl.pallas_call(
        body, name=name, out_shape=[SDS((rows, LANES), F32)] * 4,
        in_specs=[vm] * 4, out_specs=[vm] * 4,
        scratch_shapes=[pltpu.VMEM((N_DEV, rows, LANES), F32),
                        pltpu.SemaphoreType.DMA((N_DEV - 1,)), pltpu.SemaphoreType.DMA((N_DEV - 1,))],
    )(g, w, m, v)


BIG = ["l0_w_in", "l0_w_out", "l1_w_in", "l1_w_uq", "l1_w_ukv", "l1_w_out", "l2_w_in", "l2_w_out",
       "l3_w_in", "l3_w_out"]
SMALL = ["l0_norm", "l1_norm", "l1_q_a_norm", "l1_kv_a_norm", "l1_q_head_norm", "l1_k_head_norm",
         "l2_norm", "l2_q_head_norm", "l2_k_head_norm", "l2_sinks", "l3_norm"]

MLA_UQ_PERM = np.concatenate([np.arange(h * 192, h * 192 + 128) for h in range(8)]
                             + [np.arange(h * 192 + 128, (h + 1) * 192) for h in range(8)])
SWA_IN_PERM = np.concatenate([SWA_Q_PERM, 1536 + SWA_Q_PERM, np.arange(1024, 1536)])


def _inv(perm):
    inv = np.empty_like(perm)
    inv[perm] = np.arange(perm.size)
    return inv


OUTS = ["l0_w_out", "l1_w_out", "l2_w_out", "l3_w_out"]
REST = ["l1_w_in", "l1_w_uq", "l1_w_ukv", "l2_w_in"]


def _pack_rest(shards):
    parts = [shards[nm].reshape(-1, D_MODEL) for nm in REST]
    used = sum(p.shape[0] for p in parts)
    assert REST_ROW0 + used == ROWS_USED
    parts.append(jnp.zeros((REST_ROWS - used, D_MODEL), parts[0].dtype))
    return jnp.concatenate(parts, axis=0)


def _unpack_rest(packed, shapes):
    out, off = {}, 0
    for nm in REST:
        shp = shapes[nm]
        rows = shp[0] * shp[1] // D_MODEL
        out[nm] = packed[off:off + rows].reshape(shp)
        off += rows
    return out


def _pack(shards):
    pair = jnp.concatenate([shards["l0_w_in"], shards["l3_w_in"]], axis=1)
    return jnp.concatenate([pair] + [shards[nm] for nm in OUTS] + [_pack_rest(shards)], axis=0)


def _out_weight(gathered, layer):
    r0 = OUT_ROW0 + 128 * layer
    return gathered[:, r0:r0 + 128].reshape(D_MODEL, D_MODEL)


def _rest_full_weights(gathered, shapes):
    out, off = {}, REST_ROW0
    for nm in REST:
        shp = shapes[nm]
        rows = shp[0] * shp[1] // D_MODEL
        blk = gathered[:, off:off + rows].reshape((N_DEV,) + shp)
        out[nm] = jnp.transpose(blk, (1, 0, 2)).reshape(shp[0], N_DEV * shp[1])
        off += rows
    return out


def _grad_pieces(pair_l0, pair_l3, dw_out, rest_grads, shapes):
    parts = [jnp.concatenate([pair_l0, pair_l3], axis=2)]
    parts += [g.reshape(N_DEV, 128, D_MODEL).astype(BF16) for g in dw_out]
    used = REST_ROW0
    for nm in REST:
        shp = shapes[nm]
        blk = jnp.transpose(rest_grads[nm].reshape(shp[0], N_DEV, shp[1]), (1, 0, 2))
        parts.append(blk.reshape(N_DEV, -1, D_MODEL).astype(BF16))
        used += parts[-1].shape[1]
    parts.append(jnp.zeros((N_DEV, ROWS_PACK - used, D_MODEL), BF16))
    return jnp.concatenate(parts, axis=1)


SMALL_USED = 5008


def _pack_small(vals, extra=None):
    parts = [vals[nm].reshape(-1).astype(F32) for nm in SMALL]
    assert sum(p.shape[0] for p in parts) == SMALL_USED
    if extra is not None:
        parts.append(extra.reshape(1).astype(F32))
    flat = jnp.concatenate(parts)
    flat = jnp.concatenate([flat, jnp.zeros((SMALL_ROWS * LANES - flat.shape[0],), F32)])
    return flat.reshape(SMALL_ROWS, LANES)


def _unpack_small(packed, shapes):
    flat = packed.reshape(-1)
    out, off = {}, 0
    for nm in SMALL:
        n = shapes[nm][0]
        out[nm] = flat[off:off + n]
        off += n
    return out


def _row(v):
    return v.reshape(1, -1).astype(F32)


def _mla_gain_rows(g):
    return jnp.concatenate([g[:128], g[128:], g[128:]]).reshape(1, 256).astype(F32)


def gate_out_loss(o, proj, gate_blk, w_out, x, target, name):
    s = x.shape[0]
    d = D_MODEL
    tm = min(512, s)

    def body(o_ref, g_ref, w_ref, x_ref, t_ref, l_ref, dy_ref):
        @pl.when(pl.program_id(0) == 0)
        def _():
            l_ref[...] = jnp.zeros_like(l_ref)

        g = g_ref[...].astype(F32)
        hg = (o_ref[...].astype(F32) * (g * _sigmoid(g))).astype(BF16)
        e = x_ref[...] + _dot(hg, w_ref[...]) - t_ref[...]
        dy_ref[...] = e * (1.0 / d)
        l_ref[...] += 0.5 * jnp.sum(jnp.mean(e * e, axis=-1, keepdims=True))

    blk = pl.BlockSpec((tm, d), lambda i: (i, 0))
    return pl.pallas_call(
        body, name=name, grid=(s // tm,),
        in_specs=[blk, pl.BlockSpec((tm, d), lambda i: (i, gate_blk)), pl.BlockSpec((d, d), lambda i: (0, 0)),
                  blk, blk],
        out_specs=[pl.BlockSpec((8, LANES), lambda i: (0, 0)), blk],
        out_shape=[SDS((8, LANES), F32), SDS((s, d), F32)],
        compiler_params=_cparams(("arbitrary",)),
    )(o, proj, w_out, x, target)


def _layer_fwd_sb(x, norm, gathered, shard_col, w_out, tag, target=None):
    proj, xn = norm_matmul(x, 0, _row(norm), gathered, f"{tag}_in", shard_col=shard_col)
    o = sb_attn_fwd(proj, f"{tag}_attn")
    if target is None:
        y = gate_out_fwd(o, proj, 3, w_out, x, f"{tag}_out")
    else:
        y = gate_out_loss(o, proj, 3, w_out, x, target, f"{tag}_out_loss")
    return y, (x, xn, proj, o)


def _layer_bwd_sb(dy, saved, norm, gathered, shard_col, w_out, tag):
    x, xn, proj, o = saved
    do, dgate, hg = gate_out_bwd(dy, w_out, o, proj, 3, f"{tag}_outb")
    dw_out = tn_matmul(hg, dy, f"{tag}_dwout")
    dq, dk, dv = sb_attn_bwd(proj, o, do, f"{tag}_attnb")
    dproj = [dq, dk, dv, dgate]
    dw_in = tn_matmul_shards(xn, dproj, f"{tag}_dwin")
    dx, dnorm = nt_norm_bwd(dproj, gathered, x, 0, _row(norm), dy, f"{tag}_inb", F32, shard_col=shard_col)
    return dx, dw_in, dw_out, dnorm.reshape(-1)


def kernel(x, l0_norm, l0_w_in, l0_w_out, l1_norm, l1_w_in, l1_q_a_norm, l1_w_uq, l1_kv_a_norm, l1_w_ukv, l1_q_head_norm, l1_k_head_norm, l1_w_out, l2_norm, l2_w_in, l2_q_head_norm, l2_k_head_norm, l2_sinks, l2_w_out, l3_norm, l3_w_in, l3_w_out, loss_target, m_l0_norm, m_l0_w_in, m_l0_w_out, m_l1_norm, m_l1_w_in, m_l1_q_a_norm, m_l1_w_uq, m_l1_kv_a_norm, m_l1_w_ukv, m_l1_q_head_norm, m_l1_k_head_norm, m_l1_w_out, m_l2_norm, m_l2_w_in, m_l2_q_head_norm, m_l2_k_head_norm, m_l2_sinks, m_l2_w_out, m_l3_norm, m_l3_w_in, m_l3_w_out, v_l0_norm, v_l0_w_in, v_l0_w_out, v_l1_norm, v_l1_w_in, v_l1_q_a_norm, v_l1_w_uq, v_l1_kv_a_norm, v_l1_w_ukv, v_l1_q_head_norm, v_l1_k_head_norm, v_l1_w_out, v_l2_norm, v_l2_w_in, v_l2_q_head_norm, v_l2_k_head_norm, v_l2_sinks, v_l2_w_out, v_l3_norm, v_l3_w_in, v_l3_w_out):
    loc = dict(locals())
    names = BIG + SMALL
    w = {nm: loc[nm] for nm in names}
    mom = {nm: loc["m_" + nm] for nm in names}
    vel = {nm: loc["v_" + nm] for nm in names}
    shapes = {nm: w[nm].shape for nm in names}
    xs = x[0]
    tgt = loss_target[0]

    gathered = all_gather_packed(_pack({nm: w[nm].astype(BF16) for nm in BIG}), "ag_weights")
    loss_local, dx0, pieces, sg = _local_step(xs, tgt, gathered, {nm: w[nm] for nm in SMALL}, shapes)
    return _reduce_and_update(loss_local, dx0, pieces, sg, w, mom, vel, shapes)


def _local_step(xs, tgt, gathered, sm, shapes):
    s = xs.shape[0]
    fw = _rest_full_weights(gathered, shapes)
    w_out = [_out_weight(gathered, layer) for layer in range(4)]
    l0_norm, l1_norm, l2_norm, l3_norm = sm["l0_norm"], sm["l1_norm"], sm["l2_norm"], sm["l3_norm"]
    l1_q_a_norm, l1_kv_a_norm = sm["l1_q_a_norm"], sm["l1_kv_a_norm"]
    l1_q_head_norm, l1_k_head_norm = sm["l1_q_head_norm"], sm["l1_k_head_norm"]
    l2_q_head_norm, l2_k_head_norm, l2_sinks = sm["l2_q_head_norm"], sm["l2_k_head_norm"], sm["l2_sinks"]
    w1_in = jnp.concatenate([fw["l1_w_in"][:, 448:], fw["l1_w_in"][:, :448], jnp.zeros((D_MODEL, 64), BF16)], axis=1)
    w1_uq = fw["l1_w_uq"][:, MLA_UQ_PERM]
    w2_in = fw["l2_w_in"][:, SWA_IN_PERM]
    w2_out = w_out[2][SWA_Q_PERM, :]

    x1, sv0 = _layer_fwd_sb(xs, l0_norm, gathered, 0, w_out[0], "l0")

    cos, sin = _rope_tables(s)
    gq, gk = _mla_gain_rows(l1_q_head_norm), _mla_gain_rows(l1_k_head_norm)
    proj1, xn1 = norm_matmul(x1, 0, _row(l1_norm), w1_in, "l1_in")
    qa, qln = norm_matmul(proj1, 4, _row(l1_q_a_norm), w1_uq, "l1_uq")
    kva, kvn = norm_matmul(proj1, 10, _row(l1_kv_a_norm), fw["l1_w_ukv"], "l1_ukv")
    q_p, k_p = mla_prep_fwd(qa, kva, proj1, gq, gk, cos, sin, "l1_prep")
    o1, lse1 = mla_attn_fwd(q_p, k_p, kva, "l1_attn")
    x2 = gate_out_fwd(o1, proj1, 0, w_out[1], x1, "l1_out")

    g2 = jnp.stack([jnp.tile(l2_q_head_norm, 2), jnp.tile(l2_k_head_norm, 2)]).astype(F32)
    proj2, xn2 = norm_matmul(x2, 0, _row(l2_norm), w2_in, "l2_in")
    qkn2 = swa_prep_fwd(proj2, g2, "l2_prep")
    bias2 = _swa_bias()
    o2 = swa_attn_fwd(qkn2, proj2, l2_sinks, bias2, "l2_attn")
    x3 = gate_out_fwd(o2, proj2, 1, w2_out, x2, "l2_out")

    (lossblk, dy), sv3 = _layer_fwd_sb(x3, l3_norm, gathered, 1, w_out[3], "l3", target=tgt)

    fg, sg = {}, {}
    dw_out = [None] * 4
    dx3, pair_l3, dw_out[3], sg["l3_norm"] = _layer_bwd_sb(dy, sv3, l3_norm, gathered, 1, w_out[3], "l3")

    do2, dgate2, hg2 = gate_out_bwd(dx3, w2_out, o2, proj2, 1, "l2_outb")
    dw2_out = tn_matmul(hg2, dx3, "l2_dwout")
    dqn2, dkn2, dv2, dsink2 = swa_attn_bwd(qkn2, proj2, l2_sinks, bias2, do2, "l2_attnb")
    dqk2, dg2 = swa_prep_bwd(dqn2, dkn2, proj2, g2, "l2_prepb")
    dproj2 = jnp.concatenate([dqk2[:, :1024], dgate2, dqk2[:, 1024:], dv2.astype(BF16)], axis=1)
    dw2_in = tn_matmul(xn2, dproj2, "l2_dwin")
    dx2, dn2 = nt_norm_bwd(dproj2, w2_in, x2, 0, _row(l2_norm), dx3, "l2_inb", F32)
    fg["l2_w_in"] = dw2_in[:, _inv(SWA_IN_PERM)]
    dw_out[2] = dw2_out[_inv(SWA_Q_PERM), :]
    sg["l2_norm"] = dn2.reshape(-1)
    sg["l2_q_head_norm"] = dg2[0, :HALF]
    sg["l2_k_head_norm"] = dg2[1, :HALF]
    sg["l2_sinks"] = dsink2[0, :SWA_HEADS]

    do1, dgate1, hg1 = gate_out_bwd(dx2, w_out[1], o1, proj1, 0, "l1_outb")
    dw_out[1] = tn_matmul(hg1, dx2, "l1_dwout")
    dq_p, dk_p, dv1 = mla_attn_bwd(q_p, k_p, kva, o1, do1, lse1, "l1_attnb")
    dqa, dkva, dkpe, dgq, dgk = mla_prep_bwd(dq_p, dk_p, dv1, qa, kva, proj1, gq, gk, cos, sin, "l1_prepb")
    dw_uq = tn_matmul(qln, dqa, "l1_dwuq")
    fg["l1_w_ukv"] = tn_matmul(kvn, dkva, "l1_dwukv")
    dqlat, dgqa = nt_norm_bwd(dqa, w1_uq, proj1, 4, _row(l1_q_a_norm), None, "l1_uqb", BF16)
    dkvlat, dgkva = nt_norm_bwd(dkva, fw["l1_w_ukv"], proj1, 10, _row(l1_kv_a_norm), None, "l1_ukvb", BF16)
    dproj1 = jnp.concatenate([dgate1, dqlat, dkvlat, dkpe], axis=1)
    dw1_in = tn_matmul(xn1, dproj1, "l1_dwin")
    dx1, dn1 = nt_norm_bwd(dproj1, w1_in, x1, 0, _row(l1_norm), dx2, "l1_inb", F32)
    fg["l1_w_in"] = jnp.concatenate([dw1_in[:, 1024:1472], dw1_in[:, :1024]], axis=1)
    fg["l1_w_uq"] = dw_uq[:, _inv(MLA_UQ_PERM)]
    sg["l1_norm"] = dn1.reshape(-1)
    sg["l1_q_a_norm"] = dgqa.reshape(-1)
    sg["l1_kv_a_norm"] = dgkva.reshape(-1)
    sg["l1_q_head_norm"] = jnp.concatenate([dgq[0, :128], dgq[0, 128:192] + dgq[0, 192:256]])
    sg["l1_k_head_norm"] = jnp.concatenate([dgk[0, :128], dgk[0, 128:192] + dgk[0, 192:256]])

    dx0, pair_l0, dw_out[0], sg["l0_norm"] = _layer_bwd_sb(dx1, sv0, l0_norm, gathered, 0, w_out[0], "l0")
    return lossblk[0, 0], dx0, _grad_pieces(pair_l0, pair_l3, dw_out, fg, shapes), sg


def _reduce_and_update(loss_local, dx0, pieces, sg, w, mom, vel, shapes):
    xi, yi, ci = _my_pos()
    cidx = jnp.reshape(ci, (1,)).astype(jnp.int32)
    chipidx = jnp.reshape(2 * xi + yi, (1,)).astype(jnp.int32)
    ra = rs_sibling_exchange(pieces, "rs_sibling")
    part = rs_pair_add(pieces, ra, cidx, "rs_pair_add")
    rb = rs_cross_exchange(part, "rs_cross")

    def update(nm, row0, col_blk):
        return rs_final_adamw(part, rb, chipidx, w[nm], mom[nm], vel[nm], row0, col_blk, f"adamw_{nm}")

    big = {"l0_w_in": update("l0_w_in", 0, 0), "l3_w_in": update("l3_w_in", 0, 1)}
    for layer, nm in enumerate(OUTS):
        big[nm] = update(nm, OUT_ROW0 + 128 * layer, 0)
    rest = rs_final_adamw(part, rb, chipidx, _pack_rest(w), _pack_rest(mom), _pack_rest(vel), REST_ROW0, 0,
                          "adamw_rest")
    gs, ds, ms, vs = small_allreduce_adamw(_pack_small(sg, loss_local), _pack_small({nm: w[nm] for nm in SMALL}),
                                           _pack_small({nm: mom[nm] for nm in SMALL}),
                                           _pack_small({nm: vel[nm] for nm in SMALL}), "small_allreduce")

    outs = []
    for i, small in enumerate((gs, ds, ms, vs)):
        d = {nm: big[nm][i] for nm in big}
        d.update(_unpack_rest(rest[i], shapes))
        d.update(_unpack_small(small, shapes))
        outs.append(d)
    order = ["l0_norm", "l0_w_in", "l0_w_out", "l1_norm", "l1_w_in", "l1_q_a_norm", "l1_w_uq", "l1_kv_a_norm",
             "l1_w_ukv", "l1_q_head_norm", "l1_k_head_norm", "l1_w_out", "l2_norm", "l2_w_in", "l2_q_head_norm",
             "l2_k_head_norm", "l2_sinks", "l2_w_out", "l3_norm", "l3_w_in", "l3_w_out"]
    flat = [gs.reshape(-1)[SMALL_USED], dx0[None]]
    for d in outs:
        flat += [d[nm] for nm in order]
    return tuple(flat)
```
